```python
import jax, jax.numpy as jnp
from jax import lax
import numpy as np

D_MODEL = 1024
BATCH = 16
SEQ = 256
DEPTH = 2
DEC_BATCH = 2
DEC_SEQ = 4096
PAST_LEN = 256

GRID_W = 64
N_HEADS = 16
N_KV_HEADS = 4
HEAD_DIM = D_MODEL // N_HEADS
Q_PER_KV = N_HEADS // N_KV_HEADS
WINDOW = 128
ATTN_BLOCK = 128
ROPE_THETA = 10000.0
GLA_HEADS = 4
GLA_DK = D_MODEL // 2 // GLA_HEADS
GLA_DV = D_MODEL // GLA_HEADS
GLA_GATE_RANK = 16
GLA_GATE_TAU = 16.0
GLA_CHUNK = 64
D_FF = 2816
N_EXPERTS = 8
TOP_K = 2
D_FF_EXPERT = 3584
NORM_EPS = 1e-6
NEG_INF = -1e30
N_ATTN_LAYERS = (DEPTH + 1) // 2
N_GLA_LAYERS = DEPTH // 2

kernel_name = "hybrid_swa_gla_prefix_dit_step"

F32 = jnp.float32


def rms_norm(x, g):
    xf = x.astype(F32)
    y = xf * lax.rsqrt(jnp.mean(xf * xf, axis=-1, keepdims=True) + NORM_EPS)
    return (y * g.astype(F32)).astype(x.dtype)


def modulation(cond, w, b):
    m = jax.nn.silu(cond) @ w + b
    return jnp.split(m[:, None, :], 6, axis=-1)


def adaln(h, g, shift, scale):
    return rms_norm(h, g) * (1 + scale) + shift


def axial_rope(x):
    T = x.shape[1]
    rows = T // GRID_W
    row_id = jnp.broadcast_to(jnp.arange(rows)[:, None], (rows, GRID_W)).reshape(-1)
    col_id = jnp.broadcast_to(jnp.arange(GRID_W)[None, :], (rows, GRID_W)).reshape(-1)
    half = HEAD_DIM // 2
    inv = ROPE_THETA ** (-jnp.arange(0, half, 2, dtype=F32) / half)

    def rot(xh, pos):
        ang = pos.astype(F32)[:, None] * inv[None, :]
        cos = jnp.cos(ang)[None, :, None, :]
        sin = jnp.sin(ang)[None, :, None, :]
        x1, x2 = jnp.split(xh.astype(F32), 2, axis=-1)
        return jnp.concatenate([x1 * cos - x2 * sin, x2 * cos + x1 * sin], axis=-1)

    out = jnp.concatenate([rot(x[..., :half], row_id), rot(x[..., half:], col_id)], axis=-1)
    return out.astype(x.dtype)


def softmax_with_sink(logits, sink):
    m = jnp.maximum(jnp.max(logits, axis=-1, keepdims=True), sink)
    p = jnp.exp(logits - m)
    return p / (jnp.sum(p, axis=-1, keepdims=True) + jnp.exp(sink - m))


def attn_qkv(a, w_qkv, q_norm, k_norm):
    B, T, _ = a.shape
    q, k, v = jnp.split(a @ w_qkv, [N_HEADS * HEAD_DIM, (N_HEADS + N_KV_HEADS) * HEAD_DIM], axis=-1)
    q = rms_norm(q.reshape(B, T, N_HEADS, HEAD_DIM), q_norm)
    k = rms_norm(k.reshape(B, T, N_KV_HEADS, HEAD_DIM), k_norm)
    v = v.reshape(B, T, N_KV_HEADS, HEAD_DIM)
    return q, k, v


def context_attention(q, k, v, sink):
    B, S = q.shape[:2]
    nb = S // ATTN_BLOCK
    qb = q.reshape(B, nb, ATTN_BLOCK, N_KV_HEADS, Q_PER_KV, HEAD_DIM).transpose(1, 0, 2, 3, 4, 5)
    sk = sink.astype(F32).reshape(N_KV_HEADS, Q_PER_KV, 1, 1)
    scale = HEAD_DIM ** -0.5

    def block(qblk):
        s = jnp.einsum('bqkgd,bskd->bkgqs', qblk, k).astype(F32) * scale
        p = softmax_with_sink(s, sk)
        return jnp.einsum('bkgqs,bskd->bqkgd', p.astype(v.dtype), v)

    o = lax.map(block, qb)
    return o.transpose(1, 0, 2, 3, 4, 5).reshape(B, S, N_HEADS * HEAD_DIM)


def latent_attention(q, k, v, ck, cv, sink):
    B, T = q.shape[:2]
    nb = T // ATTN_BLOCK
    scale = HEAD_DIM ** -0.5
    qb = q.reshape(B, nb, ATTN_BLOCK, N_KV_HEADS, Q_PER_KV, HEAD_DIM)
    pad = ((0, 0), (ATTN_BLOCK, ATTN_BLOCK), (0, 0), (0, 0))

    def band(t):
        tp = jnp.pad(t, pad).reshape(B, nb + 2, ATTN_BLOCK, N_KV_HEADS, HEAD_DIM)
        return jnp.concatenate([tp[:, :nb], tp[:, 1:nb + 1], tp[:, 2:]], axis=2)

    kb, vb = band(k), band(v)
    qi = jnp.arange(ATTN_BLOCK)[:, None]
    kj = jnp.arange(3 * ATTN_BLOCK)[None, :]
    rel = qi + ATTN_BLOCK - kj
    kpos = jnp.arange(nb)[:, None, None] * ATTN_BLOCK - ATTN_BLOCK + kj[None]
    valid = (jnp.abs(rel) <= WINDOW)[None] & (kpos >= 0) & (kpos < T)
    s_loc = jnp.einsum('bnqkgd,bnskd->bnkgqs', qb, kb).astype(F32) * scale
    s_loc = jnp.where(valid[None, :, None, None], s_loc, NEG_INF)
    s_ctx = jnp.einsum('bnqkgd,bpkd->bnkgqp', qb, ck).astype(F32) * scale
    sk = sink.astype(F32).reshape(N_KV_HEADS, Q_PER_KV, 1, 1)
    p = softmax_with_sink(jnp.concatenate([s_loc, s_ctx], axis=-1), sk).astype(v.dtype)
    nloc = 3 * ATTN_BLOCK
    o = (jnp.einsum('bnkgqs,bnskd->bnqkgd', p[..., :nloc], vb)
         + jnp.einsum('bnkgqp,bpkd->bnqkgd', p[..., nloc:], cv.astype(v.dtype)))
    return o.reshape(B, T, N_HEADS * HEAD_DIM)


def gla_project(a, w_in, gate_w1, gate_w2, gate_b):
    B, T, _ = a.shape
    hk, hv = GLA_HEADS * GLA_DK, GLA_HEADS * GLA_DV
    q, k, v, r = jnp.split(a @ w_in, [hk, 2 * hk, 2 * hk + hv], axis=-1)
    q = q.reshape(B, T, GLA_HEADS, GLA_DK).astype(F32) * (GLA_DK ** -0.5)
    k = k.reshape(B, T, GLA_HEADS, GLA_DK).astype(F32)
    v = v.reshape(B, T, GLA_HEADS, GLA_DV).astype(F32)
    z = jnp.einsum('btd,ndr->nbtr', a, gate_w1)
    z = jnp.einsum('nbtr,nrk->nbtk', z, gate_w2) + gate_b[:, None, None, :]
    g = (jax.nn.log_sigmoid(z.astype(F32)) / GLA_GATE_TAU).reshape(2, B, T, GLA_HEADS, GLA_DK)
    return q, k, v, r, g


def gla_scan(q, k, v, g, s0):
    B, T, H, _ = q.shape
    dv = v.shape[-1]
    nc = T // GLA_CHUNK

    def chunks(t):
        return t.reshape(B, nc, GLA_CHUNK, H, t.shape[-1]).transpose(1, 0, 3, 2, 4)

    causal = jnp.tril(jnp.ones((GLA_CHUNK, GLA_CHUNK), dtype=bool))[:, :, None]

    def step(S, inp):
        qc, kc, vc, gc = inp
        b = lax.cumsum(gc, axis=2)
        b_last = b[:, :, -1:, :]
        o_inter = jnp.einsum('bhcd,bhde->bhce', qc * jnp.exp(b), S)
        diff = b[:, :, :, None, :] - b[:, :, None, :, :]
        decay = jnp.exp(jnp.where(causal, diff, -jnp.inf))
        att = jnp.einsum('bhid,bhijd,bhjd->bhij', qc, decay, kc)
        o = o_inter + jnp.einsum('bhij,bhje->bhie', att, vc)
        S = (S * jnp.exp(b_last[:, :, 0, :])[..., None]
             + jnp.einsum('bhcd,bhce->bhde', kc * jnp.exp(b_last - b), vc))
        return S, o

    S, o = lax.scan(step, s0, (chunks(q), chunks(k), chunks(v), chunks(g)))
    return o.transpose(1, 0, 3, 2, 4).reshape(B, T, H, dv), S


def gla_bidirectional(q, k, v, g, s_fwd, s_bwd):
    o_f, s_f = gla_scan(q, k, v, g[0], s_fwd)
    flip = lambda t: jnp.flip(t, axis=1)
    o_b, s_b = gla_scan(flip(q), flip(k), flip(v), flip(g[1]), s_bwd)
    return o_f + flip(o_b), s_f, s_b


def gla_output(o, r, out_norm, w_o):
    B, T = o.shape[:2]
    o = rms_norm(o, out_norm).reshape(B, T, GLA_HEADS * GLA_DV)
    return (o.astype(r.dtype) * jax.nn.silu(r)) @ w_o


def swiglu(x, wg, wu, wd):
    return (jax.nn.silu(x @ wg) * (x @ wu)) @ wd


def moe(x, router, wg, wu, wd):
    B, T, D = x.shape
    xf = x.reshape(B * T, D)
    logits = (xf @ router).astype(F32)
    top_val, top_idx = lax.top_k(logits, TOP_K)
    top_w = jax.nn.softmax(top_val, axis=-1)
    combine = jnp.sum(jax.nn.one_hot(top_idx, N_EXPERTS, dtype=F32) * top_w[..., None], axis=1)
    out = jnp.zeros_like(xf)
    for e in range(N_EXPERTS):
        out = out + combine[:, e:e + 1].astype(x.dtype) * swiglu(xf, wg[e], wu[e], wd[e])
    return out.reshape(B, T, D)


def setup_inputs(seed: int = 0) -> dict:
    key = jax.random.key(seed)
    ks = iter(jax.random.split(key, 40))
    nrm = lambda shape, scale: jax.random.normal(next(ks), shape, F32) * scale
    D = D_MODEL
    NA, NG = N_ATTN_LAYERS, N_GLA_LAYERS
    NDENSE, NMOE = N_ATTN_LAYERS, N_GLA_LAYERS
    qkv_w = (N_HEADS + 2 * N_KV_HEADS) * HEAD_DIM
    hk, hv = GLA_HEADS * GLA_DK, GLA_HEADS * GLA_DV
    return {
        "x_prompt": nrm((BATCH, SEQ, D), 1.0),
        "x_sample": nrm((DEC_BATCH, DEC_SEQ, D), 1.0),
        "cache_k": nrm((DEC_BATCH, NA, PAST_LEN, N_KV_HEADS, HEAD_DIM), 1.0),
        "cache_v": nrm((DEC_BATCH, NA, PAST_LEN, N_KV_HEADS, HEAD_DIM), 1.0),
        "state_fwd": nrm((DEC_BATCH, NG, GLA_HEADS, GLA_DK, GLA_DV), 1.0),
        "state_bwd": nrm((DEC_BATCH, NG, GLA_HEADS, GLA_DK, GLA_DV), 1.0),
        "c": nrm((DEC_BATCH, D), 1.0),
        "c_ctx": nrm((D,), 1.0),
        "ada_w": nrm((DEPTH, D, 6 * D), 0.5 * D ** -0.5),
        "ada_b": nrm((DEPTH, 6 * D), 0.02),
        "norm1_g": 1.0 + nrm((DEPTH, D), 0.05),
        "norm2_g": 1.0 + nrm((DEPTH, D), 0.05),
        "attn_w_qkv": nrm((NA, D, qkv_w), D ** -0.5),
        "attn_q_norm": 1.0 + nrm((NA, HEAD_DIM), 0.05),
        "attn_k_norm": 1.0 + nrm((NA, HEAD_DIM), 0.05),
        "attn_sink": nrm((NA, N_HEADS), 0.5),
        "attn_w_o": nrm((NA, N_HEADS * HEAD_DIM, D), (N_HEADS * HEAD_DIM) ** -0.5),
        "gla_w_in": nrm((NG, D, 2 * hk + 2 * hv), D ** -0.5),
        "gla_gate_w1": nrm((NG, 2, D, GLA_GATE_RANK), D ** -0.5),
        "gla_gate_w2": nrm((NG, 2, GLA_GATE_RANK, hk), GLA_GATE_RANK ** -0.5),
        "gla_gate_b": nrm((NG, 2, hk), 0.1),
        "gla_out_norm": 1.0 + nrm((NG, GLA_DV), 0.05),
        "gla_w_o": nrm((NG, hv, D), hv ** -0.5),
        "ffn_w_gate": nrm((NDENSE, D, D_FF), D ** -0.5),
        "ffn_w_up": nrm((NDENSE, D, D_FF), D ** -0.5),
        "ffn_w_down": nrm((NDENSE, D_FF, D), D_FF ** -0.5),
        "moe_router": nrm((NMOE, D, N_EXPERTS), D ** -0.5),
        "moe_w_gate": nrm((NMOE, N_EXPERTS, D, D_FF_EXPERT), D ** -0.5),
        "moe_w_up": nrm((NMOE, N_EXPERTS, D, D_FF_EXPERT), D ** -0.5),
        "moe_w_down": nrm((NMOE, N_EXPERTS, D_FF_EXPERT, D), D_FF_EXPERT ** -0.5),
    }


def reference(x_prompt, x_sample, cache_k, cache_v, state_fwd, state_bwd, c, c_ctx,
              ada_w, ada_b, norm1_g, norm2_g,
              attn_w_qkv, attn_q_norm, attn_k_norm, attn_sink, attn_w_o,
              gla_w_in, gla_gate_w1, gla_gate_w2, gla_gate_b, gla_out_norm, gla_w_o,
              ffn_w_gate, ffn_w_up, ffn_w_down,
              moe_router, moe_w_gate, moe_w_up, moe_w_down):
    hp, hs = x_prompt, x_sample
    Bp = x_prompt.shape[0]
    new_k, new_v, new_sf, new_sb = [], [], [], []
    for i in range(DEPTH):
        j = i // 2
        sh1p, sc1p, g1p, sh2p, sc2p, g2p = modulation(c_ctx[None, :], ada_w[i], ada_b[i])
        sh1s, sc1s, g1s, sh2s, sc2s, g2s = modulation(c, ada_w[i], ada_b[i])
        ap = adaln(hp, norm1_g[i], sh1p, sc1p)
        as_ = adaln(hs, norm1_g[i], sh1s, sc1s)
        if i % 2 == 0:
            qp, kp, vp = attn_qkv(ap, attn_w_qkv[j], attn_q_norm[j], attn_k_norm[j])
            op = context_attention(qp, kp, vp, attn_sink[j]) @ attn_w_o[j]
            qs, ks_, vs = attn_qkv(as_, attn_w_qkv[j], attn_q_norm[j], attn_k_norm[j])
            os_ = latent_attention(axial_rope(qs), axial_rope(ks_), vs,
                                   cache_k[:, j], cache_v[:, j], attn_sink[j]) @ attn_w_o[j]
            new_k.append(kp)
            new_v.append(vp)
        else:
            q, k, v, r, g = gla_project(ap, gla_w_in[j], gla_gate_w1[j], gla_gate_w2[j], gla_gate_b[j])
            zeros = jnp.zeros((Bp, GLA_HEADS, GLA_DK, GLA_DV), F32)
            o, sf, sb = gla_bidirectional(q, k, v, g, zeros, zeros)
            op = gla_output(o, r, gla_out_norm[j], gla_w_o[j])
            q, k, v, r, g = gla_project(as_, gla_w_in[j], gla_gate_w1[j], gla_gate_w2[j], gla_gate_b[j])
            o, _, _ = gla_bidirectional(q, k, v, g, state_fwd[:, j].astype(F32), state_bwd[:, j].astype(F32))
            os_ = gla_output(o, r, gla_out_norm[j], gla_w_o[j])
            new_sf.append(sf)
            new_sb.append(sb)
        hp = hp + g1p * op
        hs = hs + g1s * os_
        fp = adaln(hp, norm2_g[i], sh2p, sc2p)
        fs = adaln(hs, norm2_g[i], sh2s, sc2s)
        if i % 2 == 0:
            up = swiglu(fp, ffn_w_gate[j], ffn_w_up[j], ffn_w_down[j])
            us = swiglu(fs, ffn_w_gate[j], ffn_w_up[j], ffn_w_down[j])
        else:
            up = moe(fp, moe_router[j], moe_w_gate[j], moe_w_up[j], moe_w_down[j])
            us = moe(fs, moe_router[j], moe_w_gate[j], moe_w_up[j], moe_w_down[j])
        hp = hp + g2p * up
        hs = hs + g2s * us
    return (hp, hs, jnp.stack(new_k, axis=1), jnp.stack(new_v, axis=1),
            jnp.stack(new_sf, axis=1), jnp.stack(new_sb, axis=1))
```

```python
import functools
import math

import jax
import jax.numpy as jnp
from jax import lax
from jax.experimental import pallas as pl
from jax.experimental.pallas import tpu as pltpu

F32 = jnp.float32
BF16 = jnp.bfloat16
I32 = jnp.int32

D_MODEL = 1024
N_HEADS = 16
N_KV_HEADS = 4
HEAD_DIM = 64
GRID_W = 64
WINDOW = 128
ATTN_BLOCK = 128
ROPE_THETA = 10000.0
GLA_HEADS = 4
GLA_DK = 128
GLA_DV = 256
GLA_GATE_RANK = 16
GLA_GATE_TAU = 16.0
GLA_CHUNK = 64
N_EXPERTS = 8
NORM_EPS = 1e-6
NEG_INF = -1e30

LANES = 128
VMEM_LIMIT = 56 * 1024 * 1024


def _params(sem, vmem=VMEM_LIMIT):
    return pltpu.CompilerParams(dimension_semantics=sem, vmem_limit_bytes=vmem)


def _row_tile(*counts, cap=512):
    t = cap
    while any(c % t for c in counts):
        t //= 2
    assert t >= 8
    return t


def _rms(x):
    return x * lax.rsqrt(jnp.mean(x * x, axis=-1, keepdims=True) + NORM_EPS)


def _adaln(x, g, shift, scale):
    return _rms(x) * g * (1.0 + scale) + shift


def _silu(x):
    return x * jax.nn.sigmoid(x)


def _const_spec(shape):
    nd = len(shape)
    return pl.BlockSpec(shape, lambda *_: (0,) * nd)


def _mod_kernel(ct_ref, w_ref, b_ref, o_ref):
    ct = ct_ref[...]
    s = _silu(ct)
    w = w_ref[...]
    rows = [jnp.sum(w * s[:, r:r + 1], axis=0, keepdims=True) for r in range(8)]
    o_ref[...] = jnp.concatenate(rows, axis=0) + b_ref[...]


def _modulation(cond_t, n_cond, ada_w, ada_b):
    depth, d, n6 = ada_w.shape
    tn = 1024
    out = pl.pallas_call(
        _mod_kernel,
        grid=(depth, n6 // tn),
        in_specs=[
            pl.BlockSpec((d, 8), lambda l, j: (0, 0)),
            pl.BlockSpec((None, d, tn), lambda l, j: (l, 0, j)),
            pl.BlockSpec((None, 1, tn), lambda l, j: (l, 0, j)),
        ],
        out_specs=pl.BlockSpec((None, 8, tn), lambda l, j: (l, 0, j)),
        out_shape=jax.ShapeDtypeStruct((depth, 8, n6), F32),
        compiler_params=_params(("arbitrary", "arbitrary")),
        name="modulation",
    )(cond_t, ada_w, ada_b.reshape(depth, 1, n6))
    m = out[:, :n_cond].reshape(depth, n_cond, 6, d)
    return jnp.pad(m, ((0, 0), (0, 0), (0, 2), (0, 0)))


def _qkv_kernel(na, nt_s, xa_ref, xb_ref, mod_ref, g_ref, w_ref, qn_ref, kn_ref, bd_ref,
                cos_ref, sin_ref, q_out, kv_out):
    i = pl.program_id(0)
    is_ctx = i < na
    x = jnp.where(is_ctx, xa_ref[...], xb_ref[...])
    a = _adaln(x, g_ref[...], mod_ref[0:1, :], mod_ref[1:2, :])
    y = jnp.dot(a.astype(BF16), w_ref[...], preferred_element_type=F32)
    tm = x.shape[0]
    cos = jnp.where(is_ctx, 1.0, cos_ref[...])
    sin = jnp.where(is_ctx, 0.0, sin_ref[...])
    lane = lax.broadcasted_iota(I32, (tm, LANES), 1)
    first16 = (lane % 32) < 16

    def norm_rope(z, wt):
        ss = jnp.dot((z * z).astype(BF16), bd_ref[...], preferred_element_type=F32)
        zn = z * lax.rsqrt(ss * (1.0 / HEAD_DIM) + NORM_EPS) * wt
        outs = []
        for c in range(2):
            t = zn[:, c * LANES:(c + 1) * LANES]
            partner = jnp.where(first16, pltpu.roll(t, LANES - 16, 1), pltpu.roll(t, 16, 1))
            outs.append(t * cos + partner * sin)
        return jnp.concatenate(outs, axis=1)

    nq = N_HEADS * HEAD_DIM
    nk = N_KV_HEADS * HEAD_DIM
    scale = HEAD_DIM ** -0.5
    for s in range(nq // 256):
        sl = slice(s * 256, (s + 1) * 256)
        q_out[:, sl] = (norm_rope(y[:, sl], qn_ref[:, sl]) * scale).astype(BF16)
    kv_out[:, 0:nk] = norm_rope(y[:, nq:nq + nk], kn_ref[...])
    kv_out[:, nk:2 * nk] = y[:, nq + nk:nq + 2 * nk]


def _rope_tables(t_len):
    pos = jnp.arange(t_len)
    row = (pos // GRID_W).astype(F32)[:, None]
    col = (pos % GRID_W).astype(F32)[:, None]
    half = HEAD_DIM // 2
    inv = ROPE_THETA ** (-jnp.arange(0, half, 2, dtype=F32) / half)[None, :]
    ar, ac = row * inv, col * inv
    cos = jnp.concatenate([jnp.cos(ar), jnp.cos(ar), jnp.cos(ac), jnp.cos(ac)], axis=1)
    sin = jnp.concatenate([-jnp.sin(ar), jnp.sin(ar), -jnp.sin(ac), jnp.sin(ac)], axis=1)
    return jnp.tile(cos, (1, 2)), jnp.tile(sin, (1, 2))


def _qkv(xp, xs, t_len, mod, g, w, qn, kn):
    n_p, n_s = xp.shape[0], xs.shape[0]
    tm = _row_tile(n_p, t_len)
    na, nb = n_p // tm, n_s // tm
    nt_s = t_len // tm
    n = n_p + n_s
    d = D_MODEL
    nq, nk = N_HEADS * HEAD_DIM, N_KV_HEADS * HEAD_DIM
    cos, sin = _rope_tables(t_len)
    eye4 = jnp.kron(jnp.eye(4, dtype=F32), jnp.ones((HEAD_DIM, HEAD_DIM), F32)).astype(BF16)
    mod_idx = lambda i: (jnp.where(i < na, 0, 1 + (i - na) // nt_s), 0, 0)
    pos_idx = lambda i: (jnp.maximum(i - na, 0) % nt_s, 0)
    return pl.pallas_call(
        functools.partial(_qkv_kernel, na, nt_s),
        grid=(na + nb,),
        in_specs=[
            pl.BlockSpec((tm, d), lambda i: (jnp.minimum(i, na - 1), 0)),
            pl.BlockSpec((tm, d), lambda i: (jnp.maximum(i - na, 0), 0)),
            pl.BlockSpec((None, 8, d), mod_idx),
            _const_spec((1, d)),
            _const_spec((d, nq + 2 * nk)),
            _const_spec((1, nq)),
            _const_spec((1, nk)),
            _const_spec((256, 256)),
            pl.BlockSpec((tm, LANES), pos_idx),
            pl.BlockSpec((tm, LANES), pos_idx),
        ],
        out_specs=[pl.BlockSpec((tm, nq), lambda i: (i, 0)),
                   pl.BlockSpec((tm, 2 * nk), lambda i: (i, 0))],
        out_shape=[jax.ShapeDtypeStruct((n, nq), BF16), jax.ShapeDtypeStruct((n, 2 * nk), F32)],
        compiler_params=_params(("arbitrary",)),
        name="qkv",
    )(xp, xs, mod, g, w, qn, kn, eye4, cos, sin)


def _dup_half(k2, half):
    lane = lax.broadcasted_iota(I32, k2.shape, 1)
    lo = lane < HEAD_DIM
    r = pltpu.roll(k2, HEAD_DIM, 1)
    return jnp.where(lo, k2, r) if half == 0 else jnp.where(lo, r, k2)


def _attn_group(q2a, q2b, kk, vv, valid, sinks):
    tq = q2a.shape[0]
    lane = lax.broadcasted_iota(I32, (tq, LANES), 1)
    lo = lane < HEAD_DIM
    zero = jnp.zeros_like(q2a)
    qs = jnp.concatenate([jnp.where(lo, q2a, zero), jnp.where(lo, zero, q2a),
                          jnp.where(lo, q2b, zero), jnp.where(lo, zero, q2b)], axis=0)
    s = lax.dot_general(qs, kk, (((1,), (1,)), ((), ())), preferred_element_type=F32)
    if valid is not None:
        s = jnp.where(valid, s, NEG_INF)
    sink = jnp.concatenate([jnp.full((tq, 1), sk, F32) for sk in sinks], axis=0)
    m = jnp.maximum(jnp.max(s, axis=-1, keepdims=True), sink)
    p = jnp.exp(s - m)
    den = jnp.sum(p, axis=-1, keepdims=True) + jnp.exp(sink - m)
    o = jnp.dot(p.astype(BF16), vv, preferred_element_type=F32) / den
    oa = jnp.where(lo, o[0:tq], o[tq:2 * tq])
    ob = jnp.where(lo, o[2 * tq:3 * tq], o[3 * tq:4 * tq])
    return oa, ob


def _ctx_attn_kernel(sink_ref, q_ref, kv_ref, x_ref, mod_ref, wo_ref, o_ref, osc):
    nk = N_KV_HEADS * HEAD_DIM
    for g in range(N_KV_HEADS):
        c = g // 2
        kk = _dup_half(kv_ref[:, c * LANES:(c + 1) * LANES], g % 2).astype(BF16)
        vv = _dup_half(kv_ref[:, nk + c * LANES:nk + (c + 1) * LANES], g % 2).astype(BF16)
        q2a = q_ref[:, (2 * g) * LANES:(2 * g + 1) * LANES]
        q2b = q_ref[:, (2 * g + 1) * LANES:(2 * g + 2) * LANES]
        sinks = [sink_ref[4 * g + j] for j in range(4)]
        oa, ob = _attn_group(q2a, q2b, kk, vv, None, sinks)
        osc[:, (2 * g) * LANES:(2 * g + 1) * LANES] = oa.astype(BF16)
        osc[:, (2 * g + 1) * LANES:(2 * g + 2) * LANES] = ob.astype(BF16)
    att = jnp.dot(osc[...], wo_ref[...], preferred_element_type=F32)
    o_ref[...] = x_ref[...] + mod_ref[2:3, :] * att


def _ctx_attention(sink, q, kv, xp, seq, mod, wo):
    n_p, d = xp.shape
    nb = n_p // seq
    return pl.pallas_call(
        _ctx_attn_kernel,
        grid=(nb,),
        in_specs=[
            pl.BlockSpec(memory_space=pltpu.SMEM),
            pl.BlockSpec((seq, d), lambda b: (b, 0)),
            pl.BlockSpec((seq, kv.shape[1]), lambda b: (b, 0)),
            pl.BlockSpec((seq, d), lambda b: (b, 0)),
            pl.BlockSpec((None, 8, d), lambda b: (0, 0, 0)),
            _const_spec((d, d)),
        ],
        out_specs=pl.BlockSpec((seq, d), lambda b: (b, 0)),
        out_shape=jax.ShapeDtypeStruct((n_p, d), F32),
        scratch_shapes=[pltpu.VMEM((seq, d), BF16)],
        compiler_params=_params(("arbitrary",)),
        name="ctx_attention",
    )(sink, q, kv, xp, mod, wo)


def _lat_attn_kernel(t_len, sink_ref, q_ref, kv_ref, ck_ref, cv_ref, x_ref, mod_ref, wo_ref, o_ref, osc):
    n = pl.program_id(1)
    tq = ATTN_BLOCK
    nloc = 3 * ATTN_BLOCK
    npast = ck_ref.shape[0]
    nk = N_KV_HEADS * HEAD_DIM
    start = pl.multiple_of(jnp.clip(n * tq - ATTN_BLOCK, 0, t_len - nloc), ATTN_BLOCK)
    kvw = kv_ref[pl.ds(start, nloc), :]
    r = lax.broadcasted_iota(I32, (4 * tq, nloc + npast), 0)
    cidx = lax.broadcasted_iota(I32, (4 * tq, nloc + npast), 1)
    qpos = n * tq + (r & (tq - 1))
    kpos = start + cidx
    valid = (cidx >= nloc) | (jnp.abs(qpos - kpos) <= WINDOW)
    for g in range(N_KV_HEADS):
        c = g // 2
        ksl = slice(c * LANES, (c + 1) * LANES)
        vsl = slice(nk + c * LANES, nk + (c + 1) * LANES)
        kk = jnp.concatenate([_dup_half(kvw[:, ksl], g % 2), _dup_half(ck_ref[:, ksl], g % 2)],
                             axis=0).astype(BF16)
        vv = jnp.concatenate([_dup_half(kvw[:, vsl], g % 2), _dup_half(cv_ref[:, ksl], g % 2)],
                             axis=0).astype(BF16)
        q2a = q_ref[:, (2 * g) * LANES:(2 * g + 1) * LANES]
        q2b = q_ref[:, (2 * g + 1) * LANES:(2 * g + 2) * LANES]
        sinks = [sink_ref[4 * g + j] for j in range(4)]
        oa, ob = _attn_group(q2a, q2b, kk, vv, valid, sinks)
        osc[:, (2 * g) * LANES:(2 * g + 1) * LANES] = oa.astype(BF16)
        osc[:, (2 * g + 1) * LANES:(2 * g + 2) * LANES] = ob.astype(BF16)
    att = jnp.dot(osc[...], wo_ref[...], preferred_element_type=F32)
    o_ref[...] = x_ref[...] + mod_ref[2:3, :] * att


def _lat_attention(sink, q, kv, ck, cv, xs, n_p, t_len, mod, wo):
    n_s, d = xs.shape
    db = n_s // t_len
    tq = ATTN_BLOCK
    nblk = t_len // tq
    assert n_p % t_len == 0 and n_p % tq == 0
    kv_off = n_p // t_len
    q_off = n_p // tq
    npast = ck.shape[1]
    return pl.pallas_call(
        functools.partial(_lat_attn_kernel, t_len),
        grid=(db, nblk),
        in_specs=[
            pl.BlockSpec(memory_space=pltpu.SMEM),
            pl.BlockSpec((tq, d), lambda b, n: (q_off + b * nblk + n, 0)),
            pl.BlockSpec((t_len, kv.shape[1]), lambda b, n: (kv_off + b, 0)),
            pl.BlockSpec((None, npast, ck.shape[2]), lambda b, n: (b, 0, 0)),
            pl.BlockSpec((None, npast, cv.shape[2]), lambda b, n: (b, 0, 0)),
            pl.BlockSpec((tq, d), lambda b, n: (b * nblk + n, 0)),
            pl.BlockSpec((None, 8, d), lambda b, n: (1 + b, 0, 0)),
            _const_spec((d, d)),
        ],
        out_specs=pl.BlockSpec((tq, d), lambda b, n: (b * nblk + n, 0)),
        out_shape=jax.ShapeDtypeStruct((n_s, d), F32),
        scratch_shapes=[pltpu.VMEM((tq, d), BF16)],
        compiler_params=_params(("arbitrary", "arbitrary")),
        name="lat_attention",
    )(sink, q, kv, ck, cv, xs, mod, wo)


def _ffn_kernel(na, fc, xa_ref, xb_ref, mod_ref, g_ref, wg_ref, wu_ref, wd_ref, o_ref, acc):
    i = pl.program_id(0)
    x = jnp.where(i < na, xa_ref[...], xb_ref[...])
    a = _adaln(x, g_ref[...], mod_ref[3:4, :], mod_ref[4:5, :]).astype(BF16)
    nf = wg_ref.shape[1] // fc
    for f in range(nf):
        sl = slice(f * fc, (f + 1) * fc)
        hg = jnp.dot(a, wg_ref[:, sl], preferred_element_type=F32)
        hu = jnp.dot(a, wu_ref[:, sl], preferred_element_type=F32)
        act = (_silu(hg) * hu).astype(BF16)
        part = jnp.dot(act, wd_ref[sl, :], preferred_element_type=F32)
        if f == 0:
            acc[...] = part
        else:
            acc[...] += part
    o_ref[...] = x + mod_ref[5:6, :] * acc[...]


def _ffn(hp, hs, t_len, mod, g, wg, wu, wd):
    n_p, n_s = hp.shape[0], hs.shape[0]
    d, dff = wg.shape
    tm = _row_tile(n_p, t_len)
    na, nb = n_p // tm, n_s // tm
    nt_s = t_len // tm
    fc = 256
    assert dff % fc == 0
    mod_idx = lambda i: (jnp.where(i < na, 0, 1 + (i - na) // nt_s), 0, 0)
    return pl.pallas_call(
        functools.partial(_ffn_kernel, na, fc),
        grid=(na + nb,),
        in_specs=[
            pl.BlockSpec((tm, d), lambda i: (jnp.minimum(i, na - 1), 0)),
            pl.BlockSpec((tm, d), lambda i: (jnp.maximum(i - na, 0), 0)),
            pl.BlockSpec((None, 8, d), mod_idx),
            _const_spec((1, d)),
            _const_spec((d, dff)),
            _const_spec((d, dff)),
            _const_spec((dff, d)),
        ],
        out_specs=pl.BlockSpec((tm, d), lambda i: (i, 0)),
        out_shape=jax.ShapeDtypeStruct((n_p + n_s, d), F32),
        scratch_shapes=[pltpu.VMEM((tm, d), F32)],
        compiler_params=_params(("arbitrary",)),
        name="ffn",
    )(hp, hs, mod, g, wg, wu, wd)


def _split3(x):
    hi = x.astype(BF16)
    r1 = x - hi.astype(F32)
    mid = r1.astype(BF16)
    lo = (r1 - mid.astype(F32)).astype(BF16)
    return hi, mid, lo


def _gla_in_kernel(nt_s, x_ref, mod_ref, g_ref, w_ref, w1_ref, w2_ref, gb_ref, tri_ref,
                   qk_out, v_out, r_out, b_out):
    x = x_ref[...]
    tm = x.shape[0]
    a = _adaln(x, g_ref[...], mod_ref[0:1, :], mod_ref[1:2, :]).astype(BF16)
    hk = GLA_HEADS * GLA_DK
    hv = GLA_HEADS * GLA_DV
    q = jnp.dot(a, w_ref[:, 0:hk], preferred_element_type=F32)
    qk_out[:, 0:hk] = q * (GLA_DK ** -0.5)
    qk_out[:, hk:2 * hk] = jnp.dot(a, w_ref[:, hk:2 * hk], preferred_element_type=F32)
    v_out[...] = jnp.dot(a, w_ref[:, 2 * hk:2 * hk + hv], preferred_element_type=F32)
    r_out[...] = jnp.dot(a, w_ref[:, 2 * hk + hv:2 * hk + 2 * hv], preferred_element_type=F32).astype(BF16)
    z1 = jnp.dot(a, w1_ref[...], preferred_element_type=F32)
    z1h = z1.astype(BF16)
    z1l = (z1 - z1h.astype(F32)).astype(BF16)
    z = (jnp.dot(z1h, w2_ref[...], preferred_element_type=F32)
         + jnp.dot(z1l, w2_ref[...], preferred_element_type=F32) + gb_ref[...])
    gate = (jnp.minimum(z, 0.0) - jnp.log(1.0 + jnp.exp(-jnp.abs(z)))) * (1.0 / GLA_GATE_TAU)
    c = GLA_CHUNK
    tri = tri_ref[...]
    for j in range(tm // c):
        rows = slice(j * c, (j + 1) * c)
        for dr in range(2):
            cols = slice(dr * hk, (dr + 1) * hk)
            parts = _split3(gate[rows, cols])
            acc = jnp.dot(tri[dr], parts[0], preferred_element_type=F32)
            acc += jnp.dot(tri[dr], parts[1], preferred_element_type=F32)
            acc += jnp.dot(tri[dr], parts[2], preferred_element_type=F32)
            b_out[rows, cols] = acc


def _gla_in(h, n_p, t_len, mod, g, w_in, w1, w2, gate_b):
    n, d = h.shape
    tm = _row_tile(n_p, t_len)
    na = n_p // tm
    nt_s = t_len // tm
    hk, hv = GLA_HEADS * GLA_DK, GLA_HEADS * GLA_DV
    c = GLA_CHUNK
    lower = jnp.tril(jnp.ones((c, c), F32))
    tri = jnp.stack([lower, lower.T]).astype(BF16)
    mod_idx = lambda i: (jnp.where(i < na, 0, 1 + (i - na) // nt_s), 0, 0)
    row = lambda i: (i, 0)
    return pl.pallas_call(
        functools.partial(_gla_in_kernel, nt_s),
        grid=(n // tm,),
        in_specs=[
            pl.BlockSpec((tm, d), row),
            pl.BlockSpec((None, 8, d), mod_idx),
            _const_spec((1, d)),
            _const_spec(w_in.shape),
            _const_spec(w1.shape),
            _const_spec(w2.shape),
            _const_spec((1, 2 * hk)),
            _const_spec((2, c, c)),
        ],
        out_specs=[pl.BlockSpec((tm, 2 * hk), row), pl.BlockSpec((tm, hv), row),
                   pl.BlockSpec((tm, hv), row), pl.BlockSpec((tm, 2 * hk), row)],
        out_shape=[jax.ShapeDtypeStruct((n, 2 * hk), F32), jax.ShapeDtypeStruct((n, hv), F32),
                   jax.ShapeDtypeStruct((n, hv), BF16), jax.ShapeDtypeStruct((n, 2 * hk), F32)],
        compiler_params=_params(("arbitrary",)),
        name="gla_in",
    )(h, mod, g, w_in, w1, w2, gate_b, tri)


def _gla_chunk(q, k, v, b, st_ref, forward):
    c = q.shape[0]
    if forward:
        b_end, b_mid = b[c - 1:c, :], b[c // 2 - 1:c // 2, :]
    else:
        b_end, b_mid = b[0:1, :], b[c // 2:c // 2 + 1, :]
    e_up = jnp.exp(b - b_mid)
    e_dn = jnp.exp(b_mid - b)
    qa = q * e_up
    ka = k * e_dn
    qd = (qa * jnp.exp(b_mid)).astype(BF16)
    kd = (ka * jnp.exp(b_end - b_mid)).astype(BF16)
    att = lax.dot_general(qa.astype(BF16), ka.astype(BF16), (((1,), (1,)), ((), ())),
                          preferred_element_type=F32)
    ri = lax.broadcasted_iota(I32, (c, c), 0)
    ci = lax.broadcasted_iota(I32, (c, c), 1)
    keep = (ci <= ri) if forward else (ci >= ri)
    att = jnp.where(keep, att, 0.0).astype(BF16)
    vb = v.astype(BF16)
    st = st_ref[...]
    o = lax.dot_general(qd, st.astype(BF16), (((1,), (1,)), ((), ())), preferred_element_type=F32)
    o = o + jnp.dot(att, vb, preferred_element_type=F32)
    upd = lax.dot_general(vb, kd, (((0,), (0,)), ((), ())), preferred_element_type=F32)
    st_ref[...] = st * jnp.exp(b_end) + upd
    return o


def _gla_scan_kernel(q_ref, k_ref, v_ref, bf_ref, bb_ref, sf0_ref, sb0_ref, o_ref, sf_ref, sb_ref,
                     stf, stb):
    t_len = q_ref.shape[0]
    c = GLA_CHUNK
    nc = t_len // c
    stf[...] = sf0_ref[...].T
    stb[...] = sb0_ref[...].T

    def step(ci, accumulate):
        rf = pl.ds(pl.multiple_of(ci * c, c), c)
        rb = pl.ds(pl.multiple_of((nc - 1 - ci) * c, c), c)
        of = _gla_chunk(q_ref[rf, :], k_ref[rf, :], v_ref[rf, :], bf_ref[rf, :], stf, True)
        ob = _gla_chunk(q_ref[rb, :], k_ref[rb, :], v_ref[rb, :], bb_ref[rb, :], stb, False)
        if accumulate:
            o_ref[rf, :] += of
            o_ref[rb, :] += ob
        else:
            o_ref[rf, :] = of
            o_ref[rb, :] = ob

    def first(ci, carry):
        step(ci, False)
        return carry

    def second(ci, carry):
        step(ci, True)
        return carry

    lax.fori_loop(0, nc // 2, first, 0)
    lax.fori_loop(nc // 2, nc, second, 0)
    sf_ref[...] = stf[...].T
    sb_ref[...] = stb[...].T


def _gla_scan(qk, v, bc, sf0, sb0, b_off, nb, t_len):
    n = qk.shape[0]
    assert n % t_len == 0 and t_len % (2 * GLA_CHUNK) == 0
    ns = n // t_len
    qk3 = qk.reshape(ns, t_len, qk.shape[1])
    v3 = v.reshape(ns, t_len, v.shape[1])
    bc3 = bc.reshape(ns, t_len, bc.shape[1])
    h, dk, dv = GLA_HEADS, GLA_DK, GLA_DV
    st_spec = pl.BlockSpec((None, None, dk, dv), lambda b, hh: (b, hh, 0, 0))
    return pl.pallas_call(
        _gla_scan_kernel,
        grid=(nb, h),
        in_specs=[
            pl.BlockSpec((None, t_len, dk), lambda b, hh: (b_off + b, 0, hh)),
            pl.BlockSpec((None, t_len, dk), lambda b, hh: (b_off + b, 0, h + hh)),
            pl.BlockSpec((None, t_len, dv), lambda b, hh: (b_off + b, 0, hh)),
            pl.BlockSpec((None, t_len, dk), lambda b, hh: (b_off + b, 0, hh)),
            pl.BlockSpec((None, t_len, dk), lambda b, hh: (b_off + b, 0, h + hh)),
            st_spec, st_spec,
        ],
        out_specs=[pl.BlockSpec((None, t_len, dv), lambda b, hh: (b, 0, hh)), st_spec, st_spec],
        out_shape=[jax.ShapeDtypeStruct((nb, t_len, h * dv), F32),
                   jax.ShapeDtypeStruct((nb, h, dk, dv), F32),
                   jax.ShapeDtypeStruct((nb, h, dk, dv), F32)],
        scratch_shapes=[pltpu.VMEM((dv, dk), F32), pltpu.VMEM((dv, dk), F32)],
        compiler_params=_params(("arbitrary", "arbitrary")),
        name="gla_scan",
    )(qk3, qk3, v3, bc3, bc3, sf0, sb0)


def _gla_out_kernel(na, oa_ref, ob_ref, r_ref, h_ref, mod_ref, on_ref, wo_ref, g2_ref, rw_ref,
                    h_out, f_out, idx_out, wt_out):
    i = pl.program_id(0)
    o = jnp.where(i < na, oa_ref[...], ob_ref[...])
    tm = o.shape[0]
    dv = GLA_DV
    parts = []
    for hh in range(GLA_HEADS):
        oh = o[:, hh * dv:(hh + 1) * dv]
        parts.append(_rms(oh) * on_ref[...])
    on = jnp.concatenate(parts, axis=1)
    r = r_ref[...].astype(F32)
    gated = (on * _silu(r)).astype(BF16)
    h = h_ref[...] + mod_ref[2:3, :] * jnp.dot(gated, wo_ref[...], preferred_element_type=F32)
    h_out[...] = h
    f = _adaln(h, g2_ref[...], mod_ref[3:4, :], mod_ref[4:5, :])
    f_out[...] = f
    fh = f.astype(BF16)
    fl = (f - fh.astype(F32)).astype(BF16)
    logits = (jnp.dot(fh, rw_ref[0], preferred_element_type=F32)
              + jnp.dot(fl, rw_ref[0], preferred_element_type=F32)
              + jnp.dot(fh, rw_ref[1], preferred_element_type=F32))
    lane = lax.broadcasted_iota(I32, (tm, LANES), 1)
    lane_f = lane.astype(F32)
    logits = jnp.where(lane < N_EXPERTS, logits, -jnp.inf)
    m1 = jnp.max(logits, axis=-1, keepdims=True)
    i1 = jnp.min(jnp.where(logits == m1, lane_f, float(LANES)), axis=-1, keepdims=True)
    rest = jnp.where(lane_f == i1, -jnp.inf, logits)
    m2 = jnp.max(rest, axis=-1, keepdims=True)
    i2 = jnp.min(jnp.where(rest == m2, lane_f, float(LANES)), axis=-1, keepdims=True)
    e2 = jnp.exp(m2 - m1)
    w1 = 1.0 / (1.0 + e2)
    w2 = e2 / (1.0 + e2)
    idx_out[...] = jnp.where(lane == 0, i1, jnp.where(lane == 1, i2, 0.0)).astype(I32)
    wt_out[...] = jnp.where(lane == 0, w1, jnp.where(lane == 1, w2, 0.0))


def _gla_out(o_p, o_s, r, h, n_p, t_len, mod, out_norm, wo, g2, router_w):
    n, d = h.shape
    tm = _row_tile(n_p, t_len)
    na = n_p // tm
    nt_s = t_len // tm
    mod_idx = lambda i: (jnp.where(i < na, 0, 1 + (i - na) // nt_s), 0, 0)
    row = lambda i: (i, 0)
    return pl.pallas_call(
        functools.partial(_gla_out_kernel, na),
        grid=(n // tm,),
        in_specs=[
            pl.BlockSpec((tm, d), lambda i: (jnp.minimum(i, na - 1), 0)),
            pl.BlockSpec((tm, d), lambda i: (jnp.maximum(i - na, 0), 0)),
            pl.BlockSpec((tm, d), row),
            pl.BlockSpec((tm, d), row),
            pl.BlockSpec((None, 8, d), mod_idx),
            _const_spec((1, GLA_DV)),
            _const_spec((d, d)),
            _const_spec((1, d)),
            _const_spec((2, d, LANES)),
        ],
        out_specs=[pl.BlockSpec((tm, d), row), pl.BlockSpec((tm, d), row),
                   pl.BlockSpec((tm, LANES), row), pl.BlockSpec((tm, LANES), row)],
        out_shape=[jax.ShapeDtypeStruct((n, d), F32), jax.ShapeDtypeStruct((n, d), F32),
                   jax.ShapeDtypeStruct((n, LANES), I32), jax.ShapeDtypeStruct((n, LANES), F32)],
        compiler_params=_params(("arbitrary",)),
        name="gla_out_router",
    )(o_p, o_s, r, h, mod, out_norm, wo, g2, router_w)


def _dispatch_kernel(src_ref, f_hbm, x_hbm, sem):
    tm = src_ref.shape[1]
    base = pl.program_id(0) * tm

    def copy(j):
        t = src_ref[0, j]
        return pltpu.make_async_copy(f_hbm.at[pl.ds(t, 1), :], x_hbm.at[pl.ds(base + j, 1), :], sem)

    def start(j, carry):
        copy(j).start()
        return carry

    def wait(j, carry):
        copy(j).wait()
        return carry

    lax.fori_loop(0, tm, start, 0)
    lax.fori_loop(0, tm, wait, 0)


def _dispatch(f, src_rows, tm):
    p = src_rows.shape[0]
    d = f.shape[1]
    return pl.pallas_call(
        _dispatch_kernel,
        grid=(p // tm,),
        in_specs=[
            pl.BlockSpec((None, 1, tm), lambda i: (i, 0, 0), memory_space=pltpu.SMEM),
            pl.BlockSpec(memory_space=pl.ANY),
        ],
        out_specs=pl.BlockSpec(memory_space=pl.ANY),
        out_shape=jax.ShapeDtypeStruct((p, d), f.dtype),
        scratch_shapes=[pltpu.SemaphoreType.DMA(())],
        compiler_params=_params(("arbitrary",)),
        name="moe_dispatch",
    )(src_rows.reshape(p // tm, 1, tm), f)


def _expert_kernel(te_ref, ts_ref, nu_ref, x_ref, wg_ref, wu_ref, wd_ref, y_ref, acc, xb):
    i = pl.program_id(0)
    f = pl.program_id(1)
    nf = pl.num_programs(1)

    @pl.when(i < nu_ref[0])
    def _():
        @pl.when(f == 0)
        def _():
            xb[...] = x_ref[...].astype(BF16)

        x = xb[...]
        hg = jnp.dot(x, wg_ref[...], preferred_element_type=F32)
        hu = jnp.dot(x, wu_ref[...], preferred_element_type=F32)
        act = (_silu(hg) * hu).astype(BF16)
        part = jnp.dot(act, wd_ref[...], preferred_element_type=F32)

        @pl.when(f == 0)
        def _():
            acc[...] = part

        @pl.when(f > 0)
        def _():
            acc[...] += part

        @pl.when(f == nf - 1)
        def _():
            y_ref[...] = acc[...]

    @pl.when((i >= nu_ref[0]) & (f == nf - 1))
    def _():
        y_ref[...] = jnp.zeros_like(y_ref)


def _experts(x, tile_expert, tile_src, n_used, wg, wu, wd, tm, tf):
    p, d = x.shape
    ne, _, dff = wg.shape
    nf = dff // tf
    assert dff % tf == 0

    def fidx(i, f, nu):
        return jnp.where(i < nu[0], f, nf - 1)

    grid_spec = pltpu.PrefetchScalarGridSpec(
        num_scalar_prefetch=3,
        grid=(p // tm, nf),
        in_specs=[
            pl.BlockSpec((tm, d), lambda i, f, te, ts, nu: (ts[i], 0)),
            pl.BlockSpec((None, d, tf), lambda i, f, te, ts, nu: (te[i], 0, fidx(i, f, nu))),
            pl.BlockSpec((None, d, tf), lambda i, f, te, ts, nu: (te[i], 0, fidx(i, f, nu))),
            pl.BlockSpec((None, tf, d), lambda i, f, te, ts, nu: (te[i], fidx(i, f, nu), 0)),
        ],
        out_specs=pl.BlockSpec((tm, d), lambda i, f, te, ts, nu: (i, 0)),
        scratch_shapes=[pltpu.VMEM((tm, d), F32), pltpu.VMEM((tm, d), BF16)],
    )
    return pl.pallas_call(
        _expert_kernel,
        grid_spec=grid_spec,
        out_shape=jax.ShapeDtypeStruct((p, d), F32),
        compiler_params=_params(("arbitrary", "arbitrary")),
        name="moe_experts",
    )(tile_expert, tile_src, n_used, x, wg, wu, wd)


def _combine_kernel(dst_ref, wt_ref, h_ref, mod_ref, y_hbm, o_ref, buf, sem):
    tm = h_ref.shape[0]

    def copy(j):
        k = j // tm
        t = j - k * tm
        return pltpu.make_async_copy(y_hbm.at[pl.ds(dst_ref[0, j], 1), :],
                                     buf.at[k, pl.ds(t, 1), :], sem)

    def start(j, carry):
        copy(j).start()
        return carry

    def wait(j, carry):
        copy(j).wait()
        return carry

    lax.fori_loop(0, 2 * tm, start, 0)
    lax.fori_loop(0, 2 * tm, wait, 0)
    wt = wt_ref[...]
    up = wt[:, 0:1] * buf[0] + wt[:, 1:2] * buf[1]
    o_ref[...] = h_ref[...] + mod_ref[5:6, :] * up


def _combine(y, dest, wt, h, mod, row_off, n_rows, n_mod, mod_off, tm):
    d = h.shape[1]
    nt = n_rows // tm
    toff = row_off // tm
    per_mod = n_rows // n_mod // tm
    return pl.pallas_call(
        _combine_kernel,
        grid=(nt,),
        in_specs=[
            pl.BlockSpec((None, 1, 2 * tm), lambda i: (toff + i, 0, 0), memory_space=pltpu.SMEM),
            pl.BlockSpec((tm, LANES), lambda i: (toff + i, 0)),
            pl.BlockSpec((tm, d), lambda i: (toff + i, 0)),
            pl.BlockSpec((None, 8, d), lambda i: (mod_off + i // per_mod, 0, 0)),
            pl.BlockSpec(memory_space=pl.ANY),
        ],
        out_specs=pl.BlockSpec((tm, d), lambda i: (i, 0)),
        out_shape=jax.ShapeDtypeStruct((n_rows, d), F32),
        scratch_shapes=[pltpu.VMEM((2, tm, d), F32), pltpu.SemaphoreType.DMA(())],
        compiler_params=_params(("arbitrary",)),
        name="moe_combine",
    )(dest, wt, h, mod, y)


def _route(idx, n, tm):
    e = idx[:, :2].reshape(-1)
    onehot = (e[:, None] == jnp.arange(N_EXPERTS, dtype=I32)[None, :]).astype(I32)
    csum = jnp.cumsum(onehot, axis=0)
    cnt = csum[-1]
    rank = jnp.take_along_axis(csum, e[:, None], axis=1)[:, 0] - 1
    padded = ((cnt + tm - 1) // tm) * tm
    gend = jnp.cumsum(padded)
    goff = gend - padded
    dest = goff[e] + rank
    p = 2 * n + N_EXPERTS * tm
    src_rows = jnp.zeros((p,), I32).at[dest].set(jnp.arange(2 * n, dtype=I32) // 2)
    n_used = gend[-1] // tm
    tiles = jnp.arange(p // tm, dtype=I32)
    tile_src = jnp.minimum(tiles, n_used - 1)
    tile_expert = jnp.minimum(jnp.searchsorted(gend, tile_src * tm, side="right"), N_EXPERTS - 1).astype(I32)
    return dest, src_rows, tile_expert, tile_src, n_used.reshape(1).astype(I32)


def kernel(x_prompt, x_sample, cache_k, cache_v, state_fwd, state_bwd, c, c_ctx, ada_w, ada_b, norm1_g, norm2_g, attn_w_qkv, attn_q_norm, attn_k_norm, attn_sink, attn_w_o, gla_w_in, gla_gate_w1, gla_gate_w2, gla_gate_b, gla_out_norm, gla_w_o, ffn_w_gate, ffn_w_up, ffn_w_down, moe_router, moe_w_gate, moe_w_up, moe_w_down):
    bp, seq, d = x_prompt.shape
    db, t_len, _ = x_sample.shape
    n_p, n_s = bp * seq, db * t_len
    n = n_p + n_s
    xp = x_prompt.reshape(n_p, d)
    xs = x_sample.reshape(n_s, d)

    cond = jnp.concatenate([c_ctx[None, :], c], axis=0)
    assert cond.shape[0] <= 8
    cond_t = jnp.pad(cond, ((0, 8 - cond.shape[0]), (0, 0))).T
    mods = _modulation(cond_t, cond.shape[0], ada_w, ada_b)

    nk = N_KV_HEADS * HEAD_DIM
    qn = jnp.tile(attn_q_norm[0], N_HEADS)[None, :]
    kn = jnp.tile(attn_k_norm[0], N_KV_HEADS)[None, :]
    q, kv = _qkv(xp, xs, t_len, mods[0], norm1_g[0][None, :], attn_w_qkv[0].astype(BF16), qn, kn)
    wo0 = attn_w_o[0].astype(BF16)
    sink = attn_sink[0]
    hp = _ctx_attention(sink, q, kv, xp, seq, mods[0], wo0)
    ck = cache_k[:, 0].reshape(db, cache_k.shape[2], nk)
    cv = cache_v[:, 0].reshape(db, cache_v.shape[2], nk)
    hs = _lat_attention(sink, q, kv, ck, cv, xs, n_p, t_len, mods[0], wo0)
    new_k = kv[:n_p, :nk].reshape(bp, 1, seq, N_KV_HEADS, HEAD_DIM)
    new_v = kv[:n_p, nk:].reshape(bp, 1, seq, N_KV_HEADS, HEAD_DIM)
    h = _ffn(hp, hs, t_len, mods[0], norm2_g[0][None, :], ffn_w_gate[0].astype(BF16),
             ffn_w_up[0].astype(BF16), ffn_w_down[0].astype(BF16))

    hk = GLA_HEADS * GLA_DK
    rank = GLA_GATE_RANK
    w1 = jnp.concatenate([gla_gate_w1[0, 0], gla_gate_w1[0, 1]], axis=1)
    w1 = jnp.pad(w1, ((0, 0), (0, LANES - 2 * rank))).astype(BF16)
    w2 = jnp.zeros((LANES, 2 * hk), F32)
    w2 = w2.at[0:rank, 0:hk].set(gla_gate_w2[0, 0]).at[rank:2 * rank, hk:].set(gla_gate_w2[0, 1]).astype(BF16)
    gate_b = gla_gate_b[0].reshape(1, 2 * hk)
    qk, v, r, bc = _gla_in(h, n_p, t_len, mods[1], norm1_g[1][None, :], gla_w_in[0].astype(BF16),
                           w1, w2, gate_b)
    zeros = jnp.zeros((bp, GLA_HEADS, GLA_DK, GLA_DV), F32)
    o_p, new_sf, new_sb = _gla_scan(qk, v, bc, zeros, zeros, 0, bp, seq)
    o_s, _, _ = _gla_scan(qk, v, bc, state_fwd[:, 0], state_bwd[:, 0], n_p // t_len, db, t_len)
    rw = jnp.pad(moe_router[0], ((0, 0), (0, LANES - N_EXPERTS)))
    rw_hi = rw.astype(BF16)
    rw_lo = (rw - rw_hi.astype(F32)).astype(BF16)
    h, f, idx, wt = _gla_out(o_p.reshape(n_p, d), o_s.reshape(n_s, d), r, h, n_p, t_len, mods[1],
                             gla_out_norm[0][None, :], gla_w_o[0].astype(BF16), norm2_g[1][None, :],
                             jnp.stack([rw_hi, rw_lo]))

    tm_e = 512
    dest, src_rows, tile_expert, tile_src, n_used = _route(idx, n, tm_e)
    xg = _dispatch(f, src_rows, tm_e)
    y = _experts(xg, tile_expert, tile_src, n_used, moe_w_gate[0].astype(BF16),
                 moe_w_up[0].astype(BF16), moe_w_down[0].astype(BF16), tm_e, 896)
    tm_c = _row_tile(n_p, t_len, cap=256)
    dest_t = dest.reshape(n // tm_c, tm_c, 2).transpose(0, 2, 1).reshape(n // tm_c, 1, 2 * tm_c)
    y_p = _combine(y, dest_t, wt, h, mods[1], 0, n_p, 1, 0, tm_c)
    y_s = _combine(y, dest_t, wt, h, mods[1], n_p, n_s, db, 1, tm_c)

    return (y_p.reshape(bp, seq, d), y_s.reshape(db, t_len, d), new_k, new_v,
            new_sf[:, None], new_sb[:, None])
```

```python
import functools
import math

import jax
import jax.numpy as jnp
from jax import lax
from jax.experimental import pallas as pl
from jax.experimental.pallas import tpu as pltpu
from jax.experimental.pallas import tpu_sc as plsc

F32 = jnp.float32
BF16 = jnp.bfloat16
I32 = jnp.int32

D_MODEL = 1024
N_HEADS = 16
N_KV_HEADS = 4
HEAD_DIM = 64
GRID_W = 64
WINDOW = 128
ATTN_BLOCK = 128
ROPE_THETA = 10000.0
GLA_HEADS = 4
GLA_DK = 128
GLA_DV = 256
GLA_GATE_RANK = 16
GLA_GATE_TAU = 16.0
GLA_CHUNK = 64
N_EXPERTS = 8
NORM_EPS = 1e-6
NEG_INF = -1e30

LANES = 128
VMEM_LIMIT = 56 * 1024 * 1024


def _params(sem, vmem=VMEM_LIMIT):
    return pltpu.CompilerParams(dimension_semantics=sem, vmem_limit_bytes=vmem)


def _row_tile(*counts, cap=512):
    t = cap
    while any(c % t for c in counts):
        t //= 2
    assert t >= 8
    return t


def _rms(x):
    return x * lax.rsqrt(jnp.mean(x * x, axis=-1, keepdims=True) + NORM_EPS)


def _adaln(x, g, shift, scale):
    return _rms(x) * g * (1.0 + scale) + shift


def _silu(x):
    return x * jax.nn.sigmoid(x)


def _const_spec(shape):
    nd = len(shape)
    return pl.BlockSpec(shape, lambda *_: (0,) * nd)


def _mod_kernel(ct_ref, w_ref, b_ref, o_ref):
    ct = ct_ref[...]
    s = _silu(ct)
    w = w_ref[...]
    rows = [jnp.sum(w * s[:, r:r + 1], axis=0, keepdims=True) for r in range(8)]
    o_ref[...] = jnp.concatenate(rows, axis=0) + b_ref[...]


def _modulation(cond_t, n_cond, ada_w, ada_b):
    depth, d, n6 = ada_w.shape
    tn = 1024
    out = pl.pallas_call(
        _mod_kernel,
        grid=(depth, n6 // tn),
        in_specs=[
            pl.BlockSpec((d, 8), lambda l, j: (0, 0)),
            pl.BlockSpec((None, d, tn), lambda l, j: (l, 0, j)),
            pl.BlockSpec((None, 1, tn), lambda l, j: (l, 0, j)),
        ],
        out_specs=pl.BlockSpec((None, 8, tn), lambda l, j: (l, 0, j)),
        out_shape=jax.ShapeDtypeStruct((depth, 8, n6), F32),
        compiler_params=_params(("arbitrary", "arbitrary")),
        name="modulation",
    )(cond_t, ada_w, ada_b.reshape(depth, 1, n6))
    m = out[:, :n_cond].reshape(depth, n_cond, 6, d)
    return jnp.pad(m, ((0, 0), (0, 0), (0, 2), (0, 0)))


def _qkv_kernel(na, nt_s, xa_ref, xb_ref, mod_ref, g_ref, w_ref, qn_ref, kn_ref, bd_ref,
                cos_ref, sin_ref, q_out, kv_out):
    i = pl.program_id(0)
    is_ctx = i < na
    x = jnp.where(is_ctx, xa_ref[...], xb_ref[...])
    a = _adaln(x, g_ref[...], mod_ref[0:1, :], mod_ref[1:2, :])
    y = jnp.dot(a.astype(BF16), w_ref[...], preferred_element_type=F32)
    tm = x.shape[0]
    cos = jnp.where(is_ctx, 1.0, cos_ref[...])
    sin = jnp.where(is_ctx, 0.0, sin_ref[...])
    lane = lax.broadcasted_iota(I32, (tm, LANES), 1)
    first16 = (lane % 32) < 16

    def norm_rope(z, wt):
        ss = jnp.dot((z * z).astype(BF16), bd_ref[...], preferred_element_type=F32)
        zn = z * lax.rsqrt(ss * (1.0 / HEAD_DIM) + NORM_EPS) * wt
        outs = []
        for c in range(2):
            t = zn[:, c * LANES:(c + 1) * LANES]
            partner = jnp.where(first16, pltpu.roll(t, LANES - 16, 1), pltpu.roll(t, 16, 1))
            outs.append(t * cos + partner * sin)
        return jnp.concatenate(outs, axis=1)

    nq = N_HEADS * HEAD_DIM
    nk = N_KV_HEADS * HEAD_DIM
    scale = HEAD_DIM ** -0.5
    for s in range(nq // 256):
        sl = slice(s * 256, (s + 1) * 256)
        q_out[:, sl] = (norm_rope(y[:, sl], qn_ref[:, sl]) * scale).astype(BF16)
    kv_out[:, 0:nk] = norm_rope(y[:, nq:nq + nk], kn_ref[...])
    kv_out[:, nk:2 * nk] = y[:, nq + nk:nq + 2 * nk]


def _rope_tables(t_len):
    pos = jnp.arange(t_len)
    row = (pos // GRID_W).astype(F32)[:, None]
    col = (pos % GRID_W).astype(F32)[:, None]
    half = HEAD_DIM // 2
    inv = ROPE_THETA ** (-jnp.arange(0, half, 2, dtype=F32) / half)[None, :]
    ar, ac = row * inv, col * inv
    cos = jnp.concatenate([jnp.cos(ar), jnp.cos(ar), jnp.cos(ac), jnp.cos(ac)], axis=1)
    sin = jnp.concatenate([-jnp.sin(ar), jnp.sin(ar), -jnp.sin(ac), jnp.sin(ac)], axis=1)
    return jnp.tile(cos, (1, 2)), jnp.tile(sin, (1, 2))


def _qkv(xp, xs, t_len, mod, g, w, qn, kn):
    n_p, n_s = xp.shape[0], xs.shape[0]
    tm = _row_tile(n_p, t_len)
    na, nb = n_p // tm, n_s // tm
    nt_s = t_len // tm
    n = n_p + n_s
    d = D_MODEL
    nq, nk = N_HEADS * HEAD_DIM, N_KV_HEADS * HEAD_DIM
    cos, sin = _rope_tables(t_len)
    eye4 = jnp.kron(jnp.eye(4, dtype=F32), jnp.ones((HEAD_DIM, HEAD_DIM), F32)).astype(BF16)
    mod_idx = lambda i: (jnp.where(i < na, 0, 1 + (i - na) // nt_s), 0, 0)
    pos_idx = lambda i: (jnp.maximum(i - na, 0) % nt_s, 0)
    return pl.pallas_call(
        functools.partial(_qkv_kernel, na, nt_s),
        grid=(na + nb,),
        in_specs=[
            pl.BlockSpec((tm, d), lambda i: (jnp.minimum(i, na - 1), 0)),
            pl.BlockSpec((tm, d), lambda i: (jnp.maximum(i - na, 0), 0)),
            pl.BlockSpec((None, 8, d), mod_idx),
            _const_spec((1, d)),
            _const_spec((d, nq + 2 * nk)),
            _const_spec((1, nq)),
            _const_spec((1, nk)),
            _const_spec((256, 256)),
            pl.BlockSpec((tm, LANES), pos_idx),
            pl.BlockSpec((tm, LANES), pos_idx),
        ],
        out_specs=[pl.BlockSpec((tm, nq), lambda i: (i, 0)),
                   pl.BlockSpec((tm, 2 * nk), lambda i: (i, 0))],
        out_shape=[jax.ShapeDtypeStruct((n, nq), BF16), jax.ShapeDtypeStruct((n, 2 * nk), F32)],
        compiler_params=_params(("arbitrary",)),
        name="qkv",
    )(xp, xs, mod, g, w, qn, kn, eye4, cos, sin)


def _dup_half(k2, half):
    lane = lax.broadcasted_iota(I32, k2.shape, 1)
    lo = lane < HEAD_DIM
    r = pltpu.roll(k2, HEAD_DIM, 1)
    return jnp.where(lo, k2, r) if half == 0 else jnp.where(lo, r, k2)


def _attn_group(q2a, q2b, kk, vv, valid, sinks):
    tq = q2a.shape[0]
    lane = lax.broadcasted_iota(I32, (tq, LANES), 1)
    lo = lane < HEAD_DIM
    zero = jnp.zeros_like(q2a)
    qs = jnp.concatenate([jnp.where(lo, q2a, zero), jnp.where(lo, zero, q2a),
                          jnp.where(lo, q2b, zero), jnp.where(lo, zero, q2b)], axis=0)
    s = lax.dot_general(qs, kk, (((1,), (1,)), ((), ())), preferred_element_type=F32)
    if valid is not None:
        s = jnp.where(valid, s, NEG_INF)
    sink = jnp.concatenate([jnp.full((tq, 1), sk, F32) for sk in sinks], axis=0)
    m = jnp.maximum(jnp.max(s, axis=-1, keepdims=True), sink)
    p = jnp.exp(s - m)
    den = jnp.sum(p, axis=-1, keepdims=True) + jnp.exp(sink - m)
    o = jnp.dot(p.astype(BF16), vv, preferred_element_type=F32) / den
    oa = jnp.where(lo, o[0:tq], o[tq:2 * tq])
    ob = jnp.where(lo, o[2 * tq:3 * tq], o[3 * tq:4 * tq])
    return oa, ob


def _ctx_attn_kernel(sink_ref, q_ref, kv_ref, x_ref, mod_ref, wo_ref, o_ref, osc):
    nk = N_KV_HEADS * HEAD_DIM
    for g in range(N_KV_HEADS):
        c = g // 2
        kk = _dup_half(kv_ref[:, c * LANES:(c + 1) * LANES], g % 2).astype(BF16)
        vv = _dup_half(kv_ref[:, nk + c * LANES:nk + (c + 1) * LANES], g % 2).astype(BF16)
        q2a = q_ref[:, (2 * g) * LANES:(2 * g + 1) * LANES]
        q2b = q_ref[:, (2 * g + 1) * LANES:(2 * g + 2) * LANES]
        sinks = [sink_ref[4 * g + j] for j in range(4)]
        oa, ob = _attn_group(q2a, q2b, kk, vv, None, sinks)
        osc[:, (2 * g) * LANES:(2 * g + 1) * LANES] = oa.astype(BF16)
        osc[:, (2 * g + 1) * LANES:(2 * g + 2) * LANES] = ob.astype(BF16)
    att = jnp.dot(osc[...], wo_ref[...], preferred_element_type=F32)
    o_ref[...] = x_ref[...] + mod_ref[2:3, :] * att


def _ctx_attention(sink, q, kv, xp, seq, mod, wo):
    n_p, d = xp.shape
    nb = n_p // seq
    return pl.pallas_call(
        _ctx_attn_kernel,
        grid=(nb,),
        in_specs=[
            pl.BlockSpec(memory_space=pltpu.SMEM),
            pl.BlockSpec((seq, d), lambda b: (b, 0)),
            pl.BlockSpec((seq, kv.shape[1]), lambda b: (b, 0)),
            pl.BlockSpec((seq, d), lambda b: (b, 0)),
            pl.BlockSpec((None, 8, d), lambda b: (0, 0, 0)),
            _const_spec((d, d)),
        ],
        out_specs=pl.BlockSpec((seq, d), lambda b: (b, 0)),
        out_shape=jax.ShapeDtypeStruct((n_p, d), F32),
        scratch_shapes=[pltpu.VMEM((seq, d), BF16)],
        compiler_params=_params(("arbitrary",)),
        name="ctx_attention",
    )(sink, q, kv, xp, mod, wo)


def _lat_attn_kernel(t_len, sink_ref, q_ref, kv_ref, ck_ref, cv_ref, x_ref, mod_ref, wo_ref, o_ref, osc):
    n = pl.program_id(1)
    tq = ATTN_BLOCK
    nloc = 3 * ATTN_BLOCK
    npast = ck_ref.shape[0]
    nk = N_KV_HEADS * HEAD_DIM
    start = pl.multiple_of(jnp.clip(n * tq - ATTN_BLOCK, 0, t_len - nloc), ATTN_BLOCK)
    kvw = kv_ref[pl.ds(start, nloc), :]
    r = lax.broadcasted_iota(I32, (4 * tq, nloc + npast), 0)
    cidx = lax.broadcasted_iota(I32, (4 * tq, nloc + npast), 1)
    qpos = n * tq + (r & (tq - 1))
    kpos = start + cidx
    valid = (cidx >= nloc) | (jnp.abs(qpos - kpos) <= WINDOW)
    for g in range(N_KV_HEADS):
        c = g // 2
        ksl = slice(c * LANES, (c + 1) * LANES)
        vsl = slice(nk + c * LANES, nk + (c + 1) * LANES)
        kk = jnp.concatenate([_dup_half(kvw[:, ksl], g % 2), _dup_half(ck_ref[:, ksl], g % 2)],
                             axis=0).astype(BF16)
        vv = jnp.concatenate([_dup_half(kvw[:, vsl], g % 2), _dup_half(cv_ref[:, ksl], g % 2)],
                             axis=0).astype(BF16)
        q2a = q_ref[:, (2 * g) * LANES:(2 * g + 1) * LANES]
        q2b = q_ref[:, (2 * g + 1) * LANES:(2 * g + 2) * LANES]
        sinks = [sink_ref[4 * g + j] for j in range(4)]
        oa, ob = _attn_group(q2a, q2b, kk, vv, valid, sinks)
        osc[:, (2 * g) * LANES:(2 * g + 1) * LANES] = oa.astype(BF16)
        osc[:, (2 * g + 1) * LANES:(2 * g + 2) * LANES] = ob.astype(BF16)
    att = jnp.dot(osc[...], wo_ref[...], preferred_element_type=F32)
    o_ref[...] = x_ref[...] + mod_ref[2:3, :] * att


def _lat_attention(sink, q, kv, ck, cv, xs, n_p, t_len, mod, wo):
    n_s, d = xs.shape
    db = n_s // t_len
    tq = ATTN_BLOCK
    nblk = t_len // tq
    assert n_p % t_len == 0 and n_p % tq == 0
    kv_off = n_p // t_len
    q_off = n_p // tq
    npast = ck.shape[1]
    return pl.pallas_call(
        functools.partial(_lat_attn_kernel, t_len),
        grid=(db, nblk),
        in_specs=[
            pl.BlockSpec(memory_space=pltpu.SMEM),
            pl.BlockSpec((tq, d), lambda b, n: (q_off + b * nblk + n, 0)),
            pl.BlockSpec((t_len, kv.shape[1]), lambda b, n: (kv_off + b, 0)),
            pl.BlockSpec((None, npast, ck.shape[2]), lambda b, n: (b, 0, 0)),
            pl.BlockSpec((None, npast, cv.shape[2]), lambda b, n: (b, 0, 0)),
            pl.BlockSpec((tq, d), lambda b, n: (b * nblk + n, 0)),
            pl.BlockSpec((None, 8, d), lambda b, n: (1 + b, 0, 0)),
            _const_spec((d, d)),
        ],
        out_specs=pl.BlockSpec((tq, d), lambda b, n: (b * nblk + n, 0)),
        out_shape=jax.ShapeDtypeStruct((n_s, d), F32),
        scratch_shapes=[pltpu.VMEM((tq, d), BF16)],
        compiler_params=_params(("arbitrary", "arbitrary")),
        name="lat_attention",
    )(sink, q, kv, ck, cv, xs, mod, wo)


def _ffn_kernel(na, fc, xa_ref, xb_ref, mod_ref, g_ref, wg_ref, wu_ref, wd_ref, o_ref, acc):
    i = pl.program_id(0)
    x = jnp.where(i < na, xa_ref[...], xb_ref[...])
    a = _adaln(x, g_ref[...], mod_ref[3:4, :], mod_ref[4:5, :]).astype(BF16)
    nf = wg_ref.shape[1] // fc
    for f in range(nf):
        sl = slice(f * fc, (f + 1) * fc)
        hg = jnp.dot(a, wg_ref[:, sl], preferred_element_type=F32)
        hu = jnp.dot(a, wu_ref[:, sl], preferred_element_type=F32)
        act = (_silu(hg) * hu).astype(BF16)
        part = jnp.dot(act, wd_ref[sl, :], preferred_element_type=F32)
        if f == 0:
            acc[...] = part
        else:
            acc[...] += part
    o_ref[...] = x + mod_ref[5:6, :] * acc[...]


def _ffn(hp, hs, t_len, mod, g, wg, wu, wd):
    n_p, n_s = hp.shape[0], hs.shape[0]
    d, dff = wg.shape
    tm = _row_tile(n_p, t_len)
    na, nb = n_p // tm, n_s // tm
    nt_s = t_len // tm
    fc = 256
    assert dff % fc == 0
    mod_idx = lambda i: (jnp.where(i < na, 0, 1 + (i - na) // nt_s), 0, 0)
    return pl.pallas_call(
        functools.partial(_ffn_kernel, na, fc),
        grid=(na + nb,),
        in_specs=[
            pl.BlockSpec((tm, d), lambda i: (jnp.minimum(i, na - 1), 0)),
            pl.BlockSpec((tm, d), lambda i: (jnp.maximum(i - na, 0), 0)),
            pl.BlockSpec((None, 8, d), mod_idx),
            _const_spec((1, d)),
            _const_spec((d, dff)),
            _const_spec((d, dff)),
            _const_spec((dff, d)),
        ],
        out_specs=pl.BlockSpec((tm, d), lambda i: (i, 0)),
        out_shape=jax.ShapeDtypeStruct((n_p + n_s, d), F32),
        scratch_shapes=[pltpu.VMEM((tm, d), F32)],
        compiler_params=_params(("arbitrary",)),
        name="ffn",
    )(hp, hs, mod, g, wg, wu, wd)


def _split3(x):
    hi = x.astype(BF16)
    r1 = x - hi.astype(F32)
    mid = r1.astype(BF16)
    lo = (r1 - mid.astype(F32)).astype(BF16)
    return hi, mid, lo


def _gla_in_kernel(nt_s, x_ref, mod_ref, g_ref, w_ref, w1_ref, w2_ref, gb_ref, tri_ref,
                   qk_out, v_out, r_out, b_out):
    x = x_ref[...]
    tm = x.shape[0]
    a = _adaln(x, g_ref[...], mod_ref[0:1, :], mod_ref[1:2, :]).astype(BF16)
    hk = GLA_HEADS * GLA_DK
    hv = GLA_HEADS * GLA_DV
    q = jnp.dot(a, w_ref[:, 0:hk], preferred_element_type=F32)
    qk_out[:, 0:hk] = q * (GLA_DK ** -0.5)
    qk_out[:, hk:2 * hk] = jnp.dot(a, w_ref[:, hk:2 * hk], preferred_element_type=F32)
    v_out[...] = jnp.dot(a, w_ref[:, 2 * hk:2 * hk + hv], preferred_element_type=F32)
    r_out[...] = jnp.dot(a, w_ref[:, 2 * hk + hv:2 * hk + 2 * hv], preferred_element_type=F32).astype(BF16)
    z1 = jnp.dot(a, w1_ref[...], preferred_element_type=F32)
    z1h = z1.astype(BF16)
    z1l = (z1 - z1h.astype(F32)).astype(BF16)
    z = (jnp.dot(z1h, w2_ref[...], preferred_element_type=F32)
         + jnp.dot(z1l, w2_ref[...], preferred_element_type=F32) + gb_ref[...])
    gate = (jnp.minimum(z, 0.0) - jnp.log(1.0 + jnp.exp(-jnp.abs(z)))) * (1.0 / GLA_GATE_TAU)
    c = GLA_CHUNK
    tri = tri_ref[...]
    for j in range(tm // c):
        rows = slice(j * c, (j + 1) * c)
        for dr in range(2):
            cols = slice(dr * hk, (dr + 1) * hk)
            parts = _split3(gate[rows, cols])
            acc = jnp.dot(tri[dr], parts[0], preferred_element_type=F32)
            acc += jnp.dot(tri[dr], parts[1], preferred_element_type=F32)
            acc += jnp.dot(tri[dr], parts[2], preferred_element_type=F32)
            b_out[rows, cols] = acc


def _gla_in(h, n_p, t_len, mod, g, w_in, w1, w2, gate_b):
    n, d = h.shape
    tm = _row_tile(n_p, t_len)
    na = n_p // tm
    nt_s = t_len // tm
    hk, hv = GLA_HEADS * GLA_DK, GLA_HEADS * GLA_DV
    c = GLA_CHUNK
    lower = jnp.tril(jnp.ones((c, c), F32))
    tri = jnp.stack([lower, lower.T]).astype(BF16)
    mod_idx = lambda i: (jnp.where(i < na, 0, 1 + (i - na) // nt_s), 0, 0)
    row = lambda i: (i, 0)
    return pl.pallas_call(
        functools.partial(_gla_in_kernel, nt_s),
        grid=(n // tm,),
        in_specs=[
            pl.BlockSpec((tm, d), row),
            pl.BlockSpec((None, 8, d), mod_idx),
            _const_spec((1, d)),
            _const_spec(w_in.shape),
            _const_spec(w1.shape),
            _const_spec(w2.shape),
            _const_spec((1, 2 * hk)),
            _const_spec((2, c, c)),
        ],
        out_specs=[pl.BlockSpec((tm, 2 * hk), row), pl.BlockSpec((tm, hv), row),
                   pl.BlockSpec((tm, hv), row), pl.BlockSpec((tm, 2 * hk), row)],
        out_shape=[jax.ShapeDtypeStruct((n, 2 * hk), F32), jax.ShapeDtypeStruct((n, hv), F32),
                   jax.ShapeDtypeStruct((n, hv), BF16), jax.ShapeDtypeStruct((n, 2 * hk), F32)],
        compiler_params=_params(("arbitrary",)),
        name="gla_in",
    )(h, mod, g, w_in, w1, w2, gate_b, tri)


def _gla_chunk(q, k, v, b, st_ref, forward):
    c = q.shape[0]
    if forward:
        b_end, b_mid = b[c - 1:c, :], b[c // 2 - 1:c // 2, :]
    else:
        b_end, b_mid = b[0:1, :], b[c // 2:c // 2 + 1, :]
    e_up = jnp.exp(b - b_mid)
    e_dn = jnp.exp(b_mid - b)
    qa = q * e_up
    ka = k * e_dn
    qd = (qa * jnp.exp(b_mid)).astype(BF16)
    kd = (ka * jnp.exp(b_end - b_mid)).astype(BF16)
    att = lax.dot_general(qa.astype(BF16), ka.astype(BF16), (((1,), (1,)), ((), ())),
                          preferred_element_type=F32)
    ri = lax.broadcasted_iota(I32, (c, c), 0)
    ci = lax.broadcasted_iota(I32, (c, c), 1)
    keep = (ci <= ri) if forward else (ci >= ri)
    att = jnp.where(keep, att, 0.0).astype(BF16)
    vb = v.astype(BF16)
    st = st_ref[...]
    o = lax.dot_general(qd, st.astype(BF16), (((1,), (1,)), ((), ())), preferred_element_type=F32)
    o = o + jnp.dot(att, vb, preferred_element_type=F32)
    upd = lax.dot_general(vb, kd, (((0,), (0,)), ((), ())), preferred_element_type=F32)
    st_ref[...] = st * jnp.exp(b_end) + upd
    return o


def _gla_scan_kernel(q_ref, k_ref, v_ref, bf_ref, bb_ref, sf0_ref, sb0_ref, o_ref, sf_ref, sb_ref,
                     stf, stb):
    t_len = q_ref.shape[0]
    c = GLA_CHUNK
    nc = t_len // c
    stf[...] = sf0_ref[...].T
    stb[...] = sb0_ref[...].T

    def step(ci, accumulate):
        rf = pl.ds(pl.multiple_of(ci * c, c), c)
        rb = pl.ds(pl.multiple_of((nc - 1 - ci) * c, c), c)
        of = _gla_chunk(q_ref[rf, :], k_ref[rf, :], v_ref[rf, :], bf_ref[rf, :], stf, True)
        ob = _gla_chunk(q_ref[rb, :], k_ref[rb, :], v_ref[rb, :], bb_ref[rb, :], stb, False)
        if accumulate:
            o_ref[rf, :] += of
            o_ref[rb, :] += ob
        else:
            o_ref[rf, :] = of
            o_ref[rb, :] = ob

    def first(ci, carry):
        step(ci, False)
        return carry

    def second(ci, carry):
        step(ci, True)
        return carry

    lax.fori_loop(0, nc // 2, first, 0)
    lax.fori_loop(nc // 2, nc, second, 0)
    sf_ref[...] = stf[...].T
    sb_ref[...] = stb[...].T


def _gla_scan(qk, v, bc, sf0, sb0, b_off, nb, t_len):
    n = qk.shape[0]
    assert n % t_len == 0 and t_len % (2 * GLA_CHUNK) == 0
    ns = n // t_len
    qk3 = qk.reshape(ns, t_len, qk.shape[1])
    v3 = v.reshape(ns, t_len, v.shape[1])
    bc3 = bc.reshape(ns, t_len, bc.shape[1])
    h, dk, dv = GLA_HEADS, GLA_DK, GLA_DV
    st_spec = pl.BlockSpec((None, None, dk, dv), lambda b, hh: (b, hh, 0, 0))
    return pl.pallas_call(
        _gla_scan_kernel,
        grid=(nb, h),
        in_specs=[
            pl.BlockSpec((None, t_len, dk), lambda b, hh: (b_off + b, 0, hh)),
            pl.BlockSpec((None, t_len, dk), lambda b, hh: (b_off + b, 0, h + hh)),
            pl.BlockSpec((None, t_len, dv), lambda b, hh: (b_off + b, 0, hh)),
            pl.BlockSpec((None, t_len, dk), lambda b, hh: (b_off + b, 0, hh)),
            pl.BlockSpec((None, t_len, dk), lambda b, hh: (b_off + b, 0, h + hh)),
            st_spec, st_spec,
        ],
        out_specs=[pl.BlockSpec((None, t_len, dv), lambda b, hh: (b, 0, hh)), st_spec, st_spec],
        out_shape=[jax.ShapeDtypeStruct((nb, t_len, h * dv), F32),
                   jax.ShapeDtypeStruct((nb, h, dk, dv), F32),
                   jax.ShapeDtypeStruct((nb, h, dk, dv), F32)],
        scratch_shapes=[pltpu.VMEM((dv, dk), F32), pltpu.VMEM((dv, dk), F32)],
        compiler_params=_params(("arbitrary", "arbitrary")),
        name="gla_scan",
    )(qk3, qk3, v3, bc3, bc3, sf0, sb0)


def _gla_out_kernel(na, oa_ref, ob_ref, r_ref, h_ref, mod_ref, on_ref, wo_ref, g2_ref, rw_ref,
                    h_out, f_out, idx_out, wt_out):
    i = pl.program_id(0)
    o = jnp.where(i < na, oa_ref[...], ob_ref[...])
    tm = o.shape[0]
    dv = GLA_DV
    parts = []
    for hh in range(GLA_HEADS):
        oh = o[:, hh * dv:(hh + 1) * dv]
        parts.append(_rms(oh) * on_ref[...])
    on = jnp.concatenate(parts, axis=1)
    r = r_ref[...].astype(F32)
    gated = (on * _silu(r)).astype(BF16)
    h = h_ref[...] + mod_ref[2:3, :] * jnp.dot(gated, wo_ref[...], preferred_element_type=F32)
    h_out[...] = h
    f = _adaln(h, g2_ref[...], mod_ref[3:4, :], mod_ref[4:5, :])
    f_out[...] = f
    fh = f.astype(BF16)
    fl = (f - fh.astype(F32)).astype(BF16)
    logits = (jnp.dot(fh, rw_ref[0], preferred_element_type=F32)
              + jnp.dot(fl, rw_ref[0], preferred_element_type=F32)
              + jnp.dot(fh, rw_ref[1], preferred_element_type=F32))
    lane = lax.broadcasted_iota(I32, (tm, LANES), 1)
    lane_f = lane.astype(F32)
    logits = jnp.where(lane < N_EXPERTS, logits, -jnp.inf)
    m1 = jnp.max(logits, axis=-1, keepdims=True)
    i1 = jnp.min(jnp.where(logits == m1, lane_f, float(LANES)), axis=-1, keepdims=True)
    rest = jnp.where(lane_f == i1, -jnp.inf, logits)
    m2 = jnp.max(rest, axis=-1, keepdims=True)
    i2 = jnp.min(jnp.where(rest == m2, lane_f, float(LANES)), axis=-1, keepdims=True)
    e2 = jnp.exp(m2 - m1)
    w1 = 1.0 / (1.0 + e2)
    w2 = e2 / (1.0 + e2)
    idx_out[...] = jnp.where(lane == 0, i1, jnp.where(lane == 1, i2, 0.0)).astype(I32)
    wt_out[...] = jnp.where(lane == 0, w1, jnp.where(lane == 1, w2, 0.0))


def _gla_out(o_p, o_s, r, h, n_p, t_len, mod, out_norm, wo, g2, router_w):
    n, d = h.shape
    tm = _row_tile(n_p, t_len)
    na = n_p // tm
    nt_s = t_len // tm
    mod_idx = lambda i: (jnp.where(i < na, 0, 1 + (i - na) // nt_s), 0, 0)
    row = lambda i: (i, 0)
    return pl.pallas_call(
        functools.partial(_gla_out_kernel, na),
        grid=(n // tm,),
        in_specs=[
            pl.BlockSpec((tm, d), lambda i: (jnp.minimum(i, na - 1), 0)),
            pl.BlockSpec((tm, d), lambda i: (jnp.maximum(i - na, 0), 0)),
            pl.BlockSpec((tm, d), row),
            pl.BlockSpec((tm, d), row),
            pl.BlockSpec((None, 8, d), mod_idx),
            _const_spec((1, GLA_DV)),
            _const_spec((d, d)),
            _const_spec((1, d)),
            _const_spec((2, d, LANES)),
        ],
        out_specs=[pl.BlockSpec((tm, d), row), pl.BlockSpec((tm, d), row),
                   pl.BlockSpec((tm, LANES), row), pl.BlockSpec((tm, LANES), row)],
        out_shape=[jax.ShapeDtypeStruct((n, d), F32), jax.ShapeDtypeStruct((n, d), F32),
                   jax.ShapeDtypeStruct((n, LANES), I32), jax.ShapeDtypeStruct((n, LANES), F32)],
        compiler_params=_params(("arbitrary",)),
        name="gla_out_router",
    )(o_p, o_s, r, h, mod, out_norm, wo, g2, router_w)


SC_CORES = 2
SC_SUBCORES = 16
SC_CHUNK = 32


def _sc_gather_rows(table, idx):
    nw = SC_CORES * SC_SUBCORES
    b = idx.shape[0]
    d = table.shape[1]
    assert b % (nw * SC_CHUNK) == 0
    per_w = b // nw
    n_chunks = per_w // SC_CHUNK
    mesh = plsc.VectorSubcoreMesh(core_axis_name="c", subcore_axis_name="s",
                                  num_cores=SC_CORES, num_subcores=SC_SUBCORES)

    def body(table_hbm, idx_hbm, out_hbm, idx_v, rows_v, sem):
        wid = lax.axis_index("s") * SC_CORES + lax.axis_index("c")
        base = wid * per_w
        pltpu.sync_copy(idx_hbm.at[wid], idx_v)

        @pl.loop(0, n_chunks)
        def _(j):
            pltpu.async_copy(table_hbm.at[idx_v.at[j]], rows_v, sem).wait()
            pltpu.sync_copy(rows_v, out_hbm.at[pl.ds(base + j * SC_CHUNK, SC_CHUNK)])

    return pl.kernel(
        body,
        out_type=jax.ShapeDtypeStruct((b, d), table.dtype),
        mesh=mesh,
        scratch_types=[pltpu.VMEM((n_chunks, SC_CHUNK), I32),
                       pltpu.VMEM((SC_CHUNK, d), table.dtype),
                       pltpu.SemaphoreType.DMA],
        name="sc_gather_rows",
    )(table, idx.reshape(nw, n_chunks, SC_CHUNK))


def _expert_kernel(te_ref, ts_ref, nu_ref, x_ref, wg_ref, wu_ref, wd_ref, y_ref, acc, xb):
    i = pl.program_id(0)
    f = pl.program_id(1)
    nf = pl.num_programs(1)

    @pl.when(i < nu_ref[0])
    def _():
        @pl.when(f == 0)
        def _():
            xb[...] = x_ref[...].astype(BF16)

        x = xb[...]
        hg = jnp.dot(x, wg_ref[...], preferred_element_type=F32)
        hu = jnp.dot(x, wu_ref[...], preferred_element_type=F32)
        act = (_silu(hg) * hu).astype(BF16)
        part = jnp.dot(act, wd_ref[...], preferred_element_type=F32)

        @pl.when(f == 0)
        def _():
            acc[...] = part

        @pl.when(f > 0)
        def _():
            acc[...] += part

        @pl.when(f == nf - 1)
        def _():
            y_ref[...] = acc[...]

    @pl.when((i >= nu_ref[0]) & (f == nf - 1))
    def _():
        y_ref[...] = jnp.zeros_like(y_ref)


def _experts(x, tile_expert, tile_src, n_used, wg, wu, wd, tm, tf):
    p, d = x.shape
    ne, _, dff = wg.shape
    nf = dff // tf
    assert dff % tf == 0

    def fidx(i, f, nu):
        return jnp.where(i < nu[0], f, nf - 1)

    grid_spec = pltpu.PrefetchScalarGridSpec(
        num_scalar_prefetch=3,
        grid=(p // tm, nf),
        in_specs=[
            pl.BlockSpec((tm, d), lambda i, f, te, ts, nu: (ts[i], 0)),
            pl.BlockSpec((None, d, tf), lambda i, f, te, ts, nu: (te[i], 0, fidx(i, f, nu))),
            pl.BlockSpec((None, d, tf), lambda i, f, te, ts, nu: (te[i], 0, fidx(i, f, nu))),
            pl.BlockSpec((None, tf, d), lambda i, f, te, ts, nu: (te[i], fidx(i, f, nu), 0)),
        ],
        out_specs=pl.BlockSpec((tm, d), lambda i, f, te, ts, nu: (i, 0)),
        scratch_shapes=[pltpu.VMEM((tm, d), F32), pltpu.VMEM((tm, d), BF16)],
    )
    return pl.pallas_call(
        _expert_kernel,
        grid_spec=grid_spec,
        out_shape=jax.ShapeDtypeStruct((p, d), F32),
        compiler_params=_params(("arbitrary", "arbitrary")),
        name="moe_experts",
    )(tile_expert, tile_src, n_used, x, wg, wu, wd)


def _combine_kernel(wt_ref, h_ref, mod_ref, y0_ref, y1_ref, o_ref):
    wt = wt_ref[...]
    up = wt[:, 0:1] * y0_ref[...] + wt[:, 1:2] * y1_ref[...]
    o_ref[...] = h_ref[...] + mod_ref[5:6, :] * up


def _combine(yg, wt, h, mod, row_off, n_rows, n_mod, mod_off, tm):
    d = h.shape[1]
    nt = n_rows // tm
    toff = row_off // tm
    per_mod = n_rows // n_mod // tm
    return pl.pallas_call(
        _combine_kernel,
        grid=(nt,),
        in_specs=[
            pl.BlockSpec((tm, LANES), lambda i: (toff + i, 0)),
            pl.BlockSpec((tm, d), lambda i: (toff + i, 0)),
            pl.BlockSpec((None, 8, d), lambda i: (mod_off + i // per_mod, 0, 0)),
            pl.BlockSpec((None, tm, d), lambda i: (0, toff + i, 0)),
            pl.BlockSpec((None, tm, d), lambda i: (1, toff + i, 0)),
        ],
        out_specs=pl.BlockSpec((tm, d), lambda i: (i, 0)),
        out_shape=jax.ShapeDtypeStruct((n_rows, d), F32),
        compiler_params=_params(("arbitrary",)),
        name="moe_combine",
    )(wt, h, mod, yg, yg)


def _route(idx, n, tm):
    e = idx[:, :2].reshape(-1)
    onehot = (e[:, None] == jnp.arange(N_EXPERTS, dtype=I32)[None, :]).astype(I32)
    csum = jnp.cumsum(onehot, axis=0)
    cnt = csum[-1]
    rank = jnp.take_along_axis(csum, e[:, None], axis=1)[:, 0] - 1
    padded = ((cnt + tm - 1) // tm) * tm
    gend = jnp.cumsum(padded)
    goff = gend - padded
    dest = goff[e] + rank
    p = 2 * n + N_EXPERTS * tm
    src_rows = jnp.zeros((p,), I32).at[dest].set(jnp.arange(2 * n, dtype=I32) // 2)
    n_used = gend[-1] // tm
    tiles = jnp.arange(p // tm, dtype=I32)
    tile_src = jnp.minimum(tiles, n_used - 1)
    tile_expert = jnp.minimum(jnp.sum((gend[None, :] <= (tile_src * tm)[:, None]).astype(I32), axis=1),
                              N_EXPERTS - 1)
    return dest, src_rows, tile_expert, tile_src, n_used.reshape(1).astype(I32)


def kernel(x_prompt, x_sample, cache_k, cache_v, state_fwd, state_bwd, c, c_ctx, ada_w, ada_b, norm1_g, norm2_g, attn_w_qkv, attn_q_norm, attn_k_norm, attn_sink, attn_w_o, gla_w_in, gla_gate_w1, gla_gate_w2, gla_gate_b, gla_out_norm, gla_w_o, ffn_w_gate, ffn_w_up, ffn_w_down, moe_router, moe_w_gate, moe_w_up, moe_w_down):
    bp, seq, d = x_prompt.shape
    db, t_len, _ = x_sample.shape
    n_p, n_s = bp * seq, db * t_len
    n = n_p + n_s
    xp = x_prompt.reshape(n_p, d)
    xs = x_sample.reshape(n_s, d)

    cond = jnp.concatenate([c_ctx[None, :], c], axis=0)
    assert cond.shape[0] <= 8
    cond_t = jnp.pad(cond, ((0, 8 - cond.shape[0]), (0, 0))).T
    mods = _modulation(cond_t, cond.shape[0], ada_w, ada_b)

    nk = N_KV_HEADS * HEAD_DIM
    qn = jnp.tile(attn_q_norm[0], N_HEADS)[None, :]
    kn = jnp.tile(attn_k_norm[0], N_KV_HEADS)[None, :]
    q, kv = _qkv(xp, xs, t_len, mods[0], norm1_g[0][None, :], attn_w_qkv[0].astype(BF16), qn, kn)
    wo0 = attn_w_o[0].astype(BF16)
    sink = attn_sink[0]
    hp = _ctx_attention(sink, q, kv, xp, seq, mods[0], wo0)
    ck = cache_k[:, 0].reshape(db, cache_k.shape[2], nk)
    cv = cache_v[:, 0].reshape(db, cache_v.shape[2], nk)
    hs = _lat_attention(sink, q, kv, ck, cv, xs, n_p, t_len, mods[0], wo0)
    new_k = kv[:n_p, :nk].reshape(bp, 1, seq, N_KV_HEADS, HEAD_DIM)
    new_v = kv[:n_p, nk:].reshape(bp, 1, seq, N_KV_HEADS, HEAD_DIM)
    h = _ffn(hp, hs, t_len, mods[0], norm2_g[0][None, :], ffn_w_gate[0].astype(BF16),
             ffn_w_up[0].astype(BF16), ffn_w_down[0].astype(BF16))

    hk = GLA_HEADS * GLA_DK
    rank = GLA_GATE_RANK
    w1 = jnp.concatenate([gla_gate_w1[0, 0], gla_gate_w1[0, 1]], axis=1)
    w1 = jnp.pad(w1, ((0, 0), (0, LANES - 2 * rank))).astype(BF16)
    w2 = jnp.zeros((LANES, 2 * hk), F32)
    w2 = w2.at[0:rank, 0:hk].set(gla_gate_w2[0, 0]).at[rank:2 * rank, hk:].set(gla_gate_w2[0, 1]).astype(BF16)
    gate_b = gla_gate_b[0].reshape(1, 2 * hk)
    qk, v, r, bc = _gla_in(h, n_p, t_len, mods[1], norm1_g[1][None, :], gla_w_in[0].astype(BF16),
                           w1, w2, gate_b)
    zeros = jnp.zeros((bp, GLA_HEADS, GLA_DK, GLA_DV), F32)
    o_p, new_sf, new_sb = _gla_scan(qk, v, bc, zeros, zeros, 0, bp, seq)
    o_s, _, _ = _gla_scan(qk, v, bc, state_fwd[:, 0], state_bwd[:, 0], n_p // t_len, db, t_len)
    rw = jnp.pad(moe_router[0], ((0, 0), (0, LANES - N_EXPERTS)))
    rw_hi = rw.astype(BF16)
    rw_lo = (rw - rw_hi.astype(F32)).astype(BF16)
    h, f, idx, wt = _gla_out(o_p.reshape(n_p, d), o_s.reshape(n_s, d), r, h, n_p, t_len, mods[1],
                             gla_out_norm[0][None, :], gla_w_o[0].astype(BF16), norm2_g[1][None, :],
                             jnp.stack([rw_hi, rw_lo]))

    tm_e = 512
    dest, src_rows, tile_expert, tile_src, n_used = _route(idx, n, tm_e)
    xg = _sc_gather_rows(f, src_rows)
    y = _experts(xg, tile_expert, tile_src, n_used, moe_w_gate[0].astype(BF16),
                 moe_w_up[0].astype(BF16), moe_w_down[0].astype(BF16), tm_e, 896)
    yg = _sc_gather_rows(y, dest.reshape(n, 2).T.reshape(2 * n)).reshape(2, n, d)
    tm_c = _row_tile(n_p, t_len)
    y_p = _combine(yg, wt, h, mods[1], 0, n_p, 1, 0, tm_c)
    y_s = _combine(yg, wt, h, mods[1], n_p, n_s, db, 1, tm_c)

    return (y_p.reshape(bp, seq, d), y_s.reshape(db, t_len, d), new_k, new_v,
            new_sf[:, None], new_sb[:, None])
```

```python
import functools
import math

import jax
import jax.numpy as jnp
from jax import lax
from jax.experimental import pallas as pl
from jax.experimental.pallas import tpu as pltpu
from jax.experimental.pallas import tpu_sc as plsc

F32 = jnp.float32
BF16 = jnp.bfloat16
I32 = jnp.int32

D_MODEL = 1024
N_HEADS = 16
N_KV_HEADS = 4
HEAD_DIM = 64
GRID_W = 64
WINDOW = 128
ATTN_BLOCK = 128
ROPE_THETA = 10000.0
GLA_HEADS = 4
GLA_DK = 128
GLA_DV = 256
GLA_GATE_RANK = 16
GLA_GATE_TAU = 16.0
GLA_CHUNK = 64
N_EXPERTS = 8
NORM_EPS = 1e-6
NEG_INF = -1e30

LANES = 128
VMEM_LIMIT = 56 * 1024 * 1024


def _params(sem, vmem=VMEM_LIMIT):
    return pltpu.CompilerParams(dimension_semantics=sem, vmem_limit_bytes=vmem)


def _row_tile(*counts, cap=512):
    t = cap
    while any(c % t for c in counts):
        t //= 2
    assert t >= 8
    return t


def _rms(x):
    return x * lax.rsqrt(jnp.mean(x * x, axis=-1, keepdims=True) + NORM_EPS)


def _adaln(x, g, shift, scale):
    return _rms(x) * g * (1.0 + scale) + shift


def _silu(x):
    return x * jax.nn.sigmoid(x)


def _const_spec(shape):
    nd = len(shape)
    return pl.BlockSpec(shape, lambda *_: (0,) * nd)


def _mod_kernel(ct_ref, w_ref, b_ref, o_ref):
    ct = ct_ref[...]
    s = _silu(ct)
    w = w_ref[...]
    rows = [jnp.sum(w * s[:, r:r + 1], axis=0, keepdims=True) for r in range(8)]
    o_ref[...] = jnp.concatenate(rows, axis=0) + b_ref[...]


def _modulation(cond_t, n_cond, ada_w, ada_b):
    depth, d, n6 = ada_w.shape
    tn = 1024
    out = pl.pallas_call(
        _mod_kernel,
        grid=(depth, n6 // tn),
        in_specs=[
            pl.BlockSpec((d, 8), lambda l, j: (0, 0)),
            pl.BlockSpec((None, d, tn), lambda l, j: (l, 0, j)),
            pl.BlockSpec((None, 1, tn), lambda l, j: (l, 0, j)),
        ],
        out_specs=pl.BlockSpec((None, 8, tn), lambda l, j: (l, 0, j)),
        out_shape=jax.ShapeDtypeStruct((depth, 8, n6), F32),
        compiler_params=_params(("arbitrary", "arbitrary")),
        name="modulation",
    )(cond_t, ada_w, ada_b.reshape(depth, 1, n6))
    m = out[:, :n_cond].reshape(depth, n_cond, 6, d)
    return jnp.pad(m, ((0, 0), (0, 0), (0, 2), (0, 0)))


def _qkv_kernel(na, nt_s, xa_ref, xb_ref, mod_ref, g_ref, w_ref, qn_ref, kn_ref, bd_ref,
                cos_ref, sin_ref, q_out, kv_out):
    i = pl.program_id(0)
    is_ctx = i < na
    x = jnp.where(is_ctx, xa_ref[...], xb_ref[...])
    a = _adaln(x, g_ref[...], mod_ref[0:1, :], mod_ref[1:2, :])
    y = jnp.dot(a.astype(BF16), w_ref[...], preferred_element_type=F32)
    tm = x.shape[0]
    cos = jnp.where(is_ctx, 1.0, cos_ref[...])
    sin = jnp.where(is_ctx, 0.0, sin_ref[...])
    lane = lax.broadcasted_iota(I32, (tm, LANES), 1)
    first16 = (lane % 32) < 16

    def norm_rope(z, wt):
        ss = jnp.dot((z * z).astype(BF16), bd_ref[...], preferred_element_type=F32)
        zn = z * lax.rsqrt(ss * (1.0 / HEAD_DIM) + NORM_EPS) * wt
        outs = []
        for c in range(2):
            t = zn[:, c * LANES:(c + 1) * LANES]
            partner = jnp.where(first16, pltpu.roll(t, LANES - 16, 1), pltpu.roll(t, 16, 1))
            outs.append(t * cos + partner * sin)
        return jnp.concatenate(outs, axis=1)

    nq = N_HEADS * HEAD_DIM
    nk = N_KV_HEADS * HEAD_DIM
    scale = HEAD_DIM ** -0.5
    for s in range(nq // 256):
        sl = slice(s * 256, (s + 1) * 256)
        q_out[:, sl] = (norm_rope(y[:, sl], qn_ref[:, sl]) * scale).astype(BF16)
    kv_out[:, 0:nk] = norm_rope(y[:, nq:nq + nk], kn_ref[...])
    kv_out[:, nk:2 * nk] = y[:, nq + nk:nq + 2 * nk]


def _rope_tables(t_len):
    pos = jnp.arange(t_len)
    row = (pos // GRID_W).astype(F32)[:, None]
    col = (pos % GRID_W).astype(F32)[:, None]
    half = HEAD_DIM // 2
    inv = ROPE_THETA ** (-jnp.arange(0, half, 2, dtype=F32) / half)[None, :]
    ar, ac = row * inv, col * inv
    cos = jnp.concatenate([jnp.cos(ar), jnp.cos(ar), jnp.cos(ac), jnp.cos(ac)], axis=1)
    sin = jnp.concatenate([-jnp.sin(ar), jnp.sin(ar), -jnp.sin(ac), jnp.sin(ac)], axis=1)
    return jnp.tile(cos, (1, 2)), jnp.tile(sin, (1, 2))


def _qkv(xp, xs, t_len, mod, g, w, qn, kn):
    n_p, n_s = xp.shape[0], xs.shape[0]
    tm = _row_tile(n_p, t_len)
    na, nb = n_p // tm, n_s // tm
    nt_s = t_len // tm
    n = n_p + n_s
    d = D_MODEL
    nq, nk = N_HEADS * HEAD_DIM, N_KV_HEADS * HEAD_DIM
    cos, sin = _rope_tables(t_len)
    eye4 = jnp.kron(jnp.eye(4, dtype=F32), jnp.ones((HEAD_DIM, HEAD_DIM), F32)).astype(BF16)
    mod_idx = lambda i: (jnp.where(i < na, 0, 1 + (i - na) // nt_s), 0, 0)
    pos_idx = lambda i: (jnp.maximum(i - na, 0) % nt_s, 0)
    return pl.pallas_call(
        functools.partial(_qkv_kernel, na, nt_s),
        grid=(na + nb,),
        in_specs=[
            pl.BlockSpec((tm, d), lambda i: (jnp.minimum(i, na - 1), 0)),
            pl.BlockSpec((tm, d), lambda i: (jnp.maximum(i - na, 0), 0)),
            pl.BlockSpec((None, 8, d), mod_idx),
            _const_spec((1, d)),
            _const_spec((d, nq + 2 * nk)),
            _const_spec((1, nq)),
            _const_spec((1, nk)),
            _const_spec((256, 256)),
            pl.BlockSpec((tm, LANES), pos_idx),
            pl.BlockSpec((tm, LANES), pos_idx),
        ],
        out_specs=[pl.BlockSpec((tm, nq), lambda i: (i, 0)),
                   pl.BlockSpec((tm, 2 * nk), lambda i: (i, 0))],
        out_shape=[jax.ShapeDtypeStruct((n, nq), BF16), jax.ShapeDtypeStruct((n, 2 * nk), F32)],
        compiler_params=_params(("arbitrary",)),
        name="qkv",
    )(xp, xs, mod, g, w, qn, kn, eye4, cos, sin)


def _dup_half(k2, half):
    lane = lax.broadcasted_iota(I32, k2.shape, 1)
    lo = lane < HEAD_DIM
    r = pltpu.roll(k2, HEAD_DIM, 1)
    return jnp.where(lo, k2, r) if half == 0 else jnp.where(lo, r, k2)


def _attn_group(q2a, q2b, kk, vv, valid, sinks):
    tq = q2a.shape[0]
    lane = lax.broadcasted_iota(I32, (tq, LANES), 1)
    lo = lane < HEAD_DIM
    zero = jnp.zeros_like(q2a)
    qs = jnp.concatenate([jnp.where(lo, q2a, zero), jnp.where(lo, zero, q2a),
                          jnp.where(lo, q2b, zero), jnp.where(lo, zero, q2b)], axis=0)
    s = lax.dot_general(qs, kk, (((1,), (1,)), ((), ())), preferred_element_type=F32)
    if valid is not None:
        s = jnp.where(valid, s, NEG_INF)
    sink = jnp.concatenate([jnp.full((tq, 1), sk, F32) for sk in sinks], axis=0)
    m = jnp.maximum(jnp.max(s, axis=-1, keepdims=True), sink)
    p = jnp.exp(s - m)
    den = jnp.sum(p, axis=-1, keepdims=True) + jnp.exp(sink - m)
    o = jnp.dot(p.astype(BF16), vv, preferred_element_type=F32) / den
    oa = jnp.where(lo, o[0:tq], o[tq:2 * tq])
    ob = jnp.where(lo, o[2 * tq:3 * tq], o[3 * tq:4 * tq])
    return oa, ob


def _ctx_attn_kernel(sink_ref, q_ref, kv_ref, x_ref, mod_ref, wo_ref, o_ref, osc):
    nk = N_KV_HEADS * HEAD_DIM
    for g in range(N_KV_HEADS):
        c = g // 2
        kk = _dup_half(kv_ref[:, c * LANES:(c + 1) * LANES], g % 2).astype(BF16)
        vv = _dup_half(kv_ref[:, nk + c * LANES:nk + (c + 1) * LANES], g % 2).astype(BF16)
        q2a = q_ref[:, (2 * g) * LANES:(2 * g + 1) * LANES]
        q2b = q_ref[:, (2 * g + 1) * LANES:(2 * g + 2) * LANES]
        sinks = [sink_ref[4 * g + j] for j in range(4)]
        oa, ob = _attn_group(q2a, q2b, kk, vv, None, sinks)
        osc[:, (2 * g) * LANES:(2 * g + 1) * LANES] = oa.astype(BF16)
        osc[:, (2 * g + 1) * LANES:(2 * g + 2) * LANES] = ob.astype(BF16)
    att = jnp.dot(osc[...], wo_ref[...], preferred_element_type=F32)
    o_ref[...] = x_ref[...] + mod_ref[2:3, :] * att


def _ctx_attention(sink, q, kv, xp, seq, mod, wo):
    n_p, d = xp.shape
    nb = n_p // seq
    return pl.pallas_call(
        _ctx_attn_kernel,
        grid=(nb,),
        in_specs=[
            pl.BlockSpec(memory_space=pltpu.SMEM),
            pl.BlockSpec((seq, d), lambda b: (b, 0)),
            pl.BlockSpec((seq, kv.shape[1]), lambda b: (b, 0)),
            pl.BlockSpec((seq, d), lambda b: (b, 0)),
            pl.BlockSpec((None, 8, d), lambda b: (0, 0, 0)),
            _const_spec((d, d)),
        ],
        out_specs=pl.BlockSpec((seq, d), lambda b: (b, 0)),
        out_shape=jax.ShapeDtypeStruct((n_p, d), F32),
        scratch_shapes=[pltpu.VMEM((seq, d), BF16)],
        compiler_params=_params(("arbitrary",)),
        name="ctx_attention",
    )(sink, q, kv, xp, mod, wo)


def _lat_attn_kernel(t_len, sink_ref, q_ref, kv_ref, ck_ref, cv_ref, x_ref, mod_ref, wo_ref, o_ref, osc):
    n = pl.program_id(1)
    tq = ATTN_BLOCK
    nloc = 3 * ATTN_BLOCK
    npast = ck_ref.shape[0]
    nk = N_KV_HEADS * HEAD_DIM
    start = pl.multiple_of(jnp.clip(n * tq - ATTN_BLOCK, 0, t_len - nloc), ATTN_BLOCK)
    kvw = kv_ref[pl.ds(start, nloc), :]
    r = lax.broadcasted_iota(I32, (4 * tq, nloc + npast), 0)
    cidx = lax.broadcasted_iota(I32, (4 * tq, nloc + npast), 1)
    qpos = n * tq + (r & (tq - 1))
    kpos = start + cidx
    valid = (cidx >= nloc) | (jnp.abs(qpos - kpos) <= WINDOW)
    for g in range(N_KV_HEADS):
        c = g // 2
        ksl = slice(c * LANES, (c + 1) * LANES)
        vsl = slice(nk + c * LANES, nk + (c + 1) * LANES)
        kk = jnp.concatenate([_dup_half(kvw[:, ksl], g % 2), _dup_half(ck_ref[:, ksl], g % 2)],
                             axis=0).astype(BF16)
        vv = jnp.concatenate([_dup_half(kvw[:, vsl], g % 2), _dup_half(cv_ref[:, ksl], g % 2)],
                             axis=0).astype(BF16)
        q2a = q_ref[:, (2 * g) * LANES:(2 * g + 1) * LANES]
        q2b = q_ref[:, (2 * g + 1) * LANES:(2 * g + 2) * LANES]
        sinks = [sink_ref[4 * g + j] for j in range(4)]
        oa, ob = _attn_group(q2a, q2b, kk, vv, valid, sinks)
        osc[:, (2 * g) * LANES:(2 * g + 1) * LANES] = oa.astype(BF16)
        osc[:, (2 * g + 1) * LANES:(2 * g + 2) * LANES] = ob.astype(BF16)
    att = jnp.dot(osc[...], wo_ref[...], preferred_element_type=F32)
    o_ref[...] = x_ref[...] + mod_ref[2:3, :] * att


def _lat_attention(sink, q, kv, ck, cv, xs, n_p, t_len, mod, wo):
    n_s, d = xs.shape
    db = n_s // t_len
    tq = ATTN_BLOCK
    nblk = t_len // tq
    assert n_p % t_len == 0 and n_p % tq == 0
    kv_off = n_p // t_len
    q_off = n_p // tq
    npast = ck.shape[1]
    return pl.pallas_call(
        functools.partial(_lat_attn_kernel, t_len),
        grid=(db, nblk),
        in_specs=[
            pl.BlockSpec(memory_space=pltpu.SMEM),
            pl.BlockSpec((tq, d), lambda b, n: (q_off + b * nblk + n, 0)),
            pl.BlockSpec((t_len, kv.shape[1]), lambda b, n: (kv_off + b, 0)),
            pl.BlockSpec((None, npast, ck.shape[2]), lambda b, n: (b, 0, 0)),
            pl.BlockSpec((None, npast, cv.shape[2]), lambda b, n: (b, 0, 0)),
            pl.BlockSpec((tq, d), lambda b, n: (b * nblk + n, 0)),
            pl.BlockSpec((None, 8, d), lambda b, n: (1 + b, 0, 0)),
            _const_spec((d, d)),
        ],
        out_specs=pl.BlockSpec((tq, d), lambda b, n: (b * nblk + n, 0)),
        out_shape=jax.ShapeDtypeStruct((n_s, d), F32),
        scratch_shapes=[pltpu.VMEM((tq, d), BF16)],
        compiler_params=_params(("arbitrary", "arbitrary")),
        name="lat_attention",
    )(sink, q, kv, ck, cv, xs, mod, wo)


def _ffn_kernel(na, fc, xa_ref, xb_ref, mod_ref, g_ref, wg_ref, wu_ref, wd_ref, o_ref, acc):
    i = pl.program_id(0)
    x = jnp.where(i < na, xa_ref[...], xb_ref[...])
    a = _adaln(x, g_ref[...], mod_ref[3:4, :], mod_ref[4:5, :]).astype(BF16)
    nf = wg_ref.shape[1] // fc
    for f in range(nf):
        sl = slice(f * fc, (f + 1) * fc)
        hg = jnp.dot(a, wg_ref[:, sl], preferred_element_type=F32)
        hu = jnp.dot(a, wu_ref[:, sl], preferred_element_type=F32)
        act = (_silu(hg) * hu).astype(BF16)
        part = jnp.dot(act, wd_ref[sl, :], preferred_element_type=F32)
        if f == 0:
            acc[...] = part
        else:
            acc[...] += part
    o_ref[...] = x + mod_ref[5:6, :] * acc[...]


def _ffn(hp, hs, t_len, mod, g, wg, wu, wd):
    n_p, n_s = hp.shape[0], hs.shape[0]
    d, dff = wg.shape
    tm = _row_tile(n_p, t_len)
    na, nb = n_p // tm, n_s // tm
    nt_s = t_len // tm
    fc = 256
    assert dff % fc == 0
    mod_idx = lambda i: (jnp.where(i < na, 0, 1 + (i - na) // nt_s), 0, 0)
    return pl.pallas_call(
        functools.partial(_ffn_kernel, na, fc),
        grid=(na + nb,),
        in_specs=[
            pl.BlockSpec((tm, d), lambda i: (jnp.minimum(i, na - 1), 0)),
            pl.BlockSpec((tm, d), lambda i: (jnp.maximum(i - na, 0), 0)),
            pl.BlockSpec((None, 8, d), mod_idx),
            _const_spec((1, d)),
            _const_spec((d, dff)),
            _const_spec((d, dff)),
            _const_spec((dff, d)),
        ],
        out_specs=pl.BlockSpec((tm, d), lambda i: (i, 0)),
        out_shape=jax.ShapeDtypeStruct((n_p + n_s, d), F32),
        scratch_shapes=[pltpu.VMEM((tm, d), F32)],
        compiler_params=_params(("arbitrary",)),
        name="ffn",
    )(hp, hs, mod, g, wg, wu, wd)


def _split3(x):
    hi = x.astype(BF16)
    r1 = x - hi.astype(F32)
    mid = r1.astype(BF16)
    lo = (r1 - mid.astype(F32)).astype(BF16)
    return hi, mid, lo


def _gla_in_kernel(nt_s, x_ref, mod_ref, g_ref, w_ref, w1_ref, w2_ref, gb_ref, tri_ref,
                   qk_out, v_out, r_out, b_out):
    x = x_ref[...]
    tm = x.shape[0]
    a = _adaln(x, g_ref[...], mod_ref[0:1, :], mod_ref[1:2, :]).astype(BF16)
    hk = GLA_HEADS * GLA_DK
    hv = GLA_HEADS * GLA_DV
    q = jnp.dot(a, w_ref[:, 0:hk], preferred_element_type=F32)
    qk_out[:, 0:hk] = q * (GLA_DK ** -0.5)
    qk_out[:, hk:2 * hk] = jnp.dot(a, w_ref[:, hk:2 * hk], preferred_element_type=F32)
    v_out[...] = jnp.dot(a, w_ref[:, 2 * hk:2 * hk + hv], preferred_element_type=F32)
    r_out[...] = jnp.dot(a, w_ref[:, 2 * hk + hv:2 * hk + 2 * hv], preferred_element_type=F32).astype(BF16)
    z1 = jnp.dot(a, w1_ref[...], preferred_element_type=F32)
    z1h = z1.astype(BF16)
    z1l = (z1 - z1h.astype(F32)).astype(BF16)
    z = (jnp.dot(z1h, w2_ref[...], preferred_element_type=F32)
         + jnp.dot(z1l, w2_ref[...], preferred_element_type=F32) + gb_ref[...])
    gate = (jnp.minimum(z, 0.0) - jnp.log(1.0 + jnp.exp(-jnp.abs(z)))) * (1.0 / GLA_GATE_TAU)
    c = GLA_CHUNK
    tri = tri_ref[...]
    for j in range(tm // c):
        rows = slice(j * c, (j + 1) * c)
        for dr in range(2):
            cols = slice(dr * hk, (dr + 1) * hk)
            parts = _split3(gate[rows, cols])
            acc = jnp.dot(tri[dr], parts[0], preferred_element_type=F32)
            acc += jnp.dot(tri[dr], parts[1], preferred_element_type=F32)
            acc += jnp.dot(tri[dr], parts[2], preferred_element_type=F32)
            b_out[rows, cols] = acc


def _gla_in(h, n_p, t_len, mod, g, w_in, w1, w2, gate_b):
    n, d = h.shape
    tm = _row_tile(n_p, t_len)
    na = n_p // tm
    nt_s = t_len // tm
    hk, hv = GLA_HEADS * GLA_DK, GLA_HEADS * GLA_DV
    c = GLA_CHUNK
    lower = jnp.tril(jnp.ones((c, c), F32))
    tri = jnp.stack([lower, lower.T]).astype(BF16)
    mod_idx = lambda i: (jnp.where(i < na, 0, 1 + (i - na) // nt_s), 0, 0)
    row = lambda i: (i, 0)
    return pl.pallas_call(
        functools.partial(_gla_in_kernel, nt_s),
        grid=(n // tm,),
        in_specs=[
            pl.BlockSpec((tm, d), row),
            pl.BlockSpec((None, 8, d), mod_idx),
            _const_spec((1, d)),
            _const_spec(w_in.shape),
            _const_spec(w1.shape),
            _const_spec(w2.shape),
            _const_spec((1, 2 * hk)),
            _const_spec((2, c, c)),
        ],
        out_specs=[pl.BlockSpec((tm, 2 * hk), row), pl.BlockSpec((tm, hv), row),
                   pl.BlockSpec((tm, hv), row), pl.BlockSpec((tm, 2 * hk), row)],
        out_shape=[jax.ShapeDtypeStruct((n, 2 * hk), F32), jax.ShapeDtypeStruct((n, hv), F32),
                   jax.ShapeDtypeStruct((n, hv), BF16), jax.ShapeDtypeStruct((n, 2 * hk), F32)],
        compiler_params=_params(("arbitrary",)),
        name="gla_in",
    )(h, mod, g, w_in, w1, w2, gate_b, tri)


def _gla_chunk(q, k, v, b, st_ref, forward):
    c = q.shape[0]
    if forward:
        b_end, b_mid = b[c - 1:c, :], b[c // 2 - 1:c // 2, :]
    else:
        b_end, b_mid = b[0:1, :], b[c // 2:c // 2 + 1, :]
    e_up = jnp.exp(b - b_mid)
    e_dn = jnp.exp(b_mid - b)
    qa = q * e_up
    ka = k * e_dn
    qd = (qa * jnp.exp(b_mid)).astype(BF16)
    kd = (ka * jnp.exp(b_end - b_mid)).astype(BF16)
    att = lax.dot_general(qa.astype(BF16), ka.astype(BF16), (((1,), (1,)), ((), ())),
                          preferred_element_type=F32)
    ri = lax.broadcasted_iota(I32, (c, c), 0)
    ci = lax.broadcasted_iota(I32, (c, c), 1)
    keep = (ci <= ri) if forward else (ci >= ri)
    att = jnp.where(keep, att, 0.0).astype(BF16)
    vb = v.astype(BF16)
    st = st_ref[...]
    o = lax.dot_general(qd, st.astype(BF16), (((1,), (1,)), ((), ())), preferred_element_type=F32)
    o = o + jnp.dot(att, vb, preferred_element_type=F32)
    upd = lax.dot_general(vb, kd, (((0,), (0,)), ((), ())), preferred_element_type=F32)
    st_ref[...] = st * jnp.exp(b_end) + upd
    return o


def _gla_scan_kernel(q_ref, k_ref, v_ref, bf_ref, bb_ref, sf0_ref, sb0_ref, o_ref, sf_ref, sb_ref,
                     stf, stb):
    t_len = q_ref.shape[0]
    c = GLA_CHUNK
    nc = t_len // c
    stf[...] = sf0_ref[...].T
    stb[...] = sb0_ref[...].T

    def step(ci, accumulate):
        rf = pl.ds(pl.multiple_of(ci * c, c), c)
        rb = pl.ds(pl.multiple_of((nc - 1 - ci) * c, c), c)
        of = _gla_chunk(q_ref[rf, :], k_ref[rf, :], v_ref[rf, :], bf_ref[rf, :], stf, True)
        ob = _gla_chunk(q_ref[rb, :], k_ref[rb, :], v_ref[rb, :], bb_ref[rb, :], stb, False)
        if accumulate:
            o_ref[rf, :] += of
            o_ref[rb, :] += ob
        else:
            o_ref[rf, :] = of
            o_ref[rb, :] = ob

    def first(ci, carry):
        step(ci, False)
        return carry

    def second(ci, carry):
        step(ci, True)
        return carry

    lax.fori_loop(0, nc // 2, first, 0)
    lax.fori_loop(nc // 2, nc, second, 0)
    sf_ref[...] = stf[...].T
    sb_ref[...] = stb[...].T


def _gla_scan(qk, v, bc, sf0, sb0, b_off, nb, t_len):
    n = qk.shape[0]
    assert n % t_len == 0 and t_len % (2 * GLA_CHUNK) == 0
    ns = n // t_len
    qk3 = qk.reshape(ns, t_len, qk.shape[1])
    v3 = v.reshape(ns, t_len, v.shape[1])
    bc3 = bc.reshape(ns, t_len, bc.shape[1])
    h, dk, dv = GLA_HEADS, GLA_DK, GLA_DV
    st_spec = pl.BlockSpec((None, None, dk, dv), lambda b, hh: (b, hh, 0, 0))
    return pl.pallas_call(
        _gla_scan_kernel,
        grid=(nb, h),
        in_specs=[
            pl.BlockSpec((None, t_len, dk), lambda b, hh: (b_off + b, 0, hh)),
            pl.BlockSpec((None, t_len, dk), lambda b, hh: (b_off + b, 0, h + hh)),
            pl.BlockSpec((None, t_len, dv), lambda b, hh: (b_off + b, 0, hh)),
            pl.BlockSpec((None, t_len, dk), lambda b, hh: (b_off + b, 0, hh)),
            pl.BlockSpec((None, t_len, dk), lambda b, hh: (b_off + b, 0, h + hh)),
            st_spec, st_spec,
        ],
        out_specs=[pl.BlockSpec((None, t_len, dv), lambda b, hh: (b, 0, hh)), st_spec, st_spec],
        out_shape=[jax.ShapeDtypeStruct((nb, t_len, h * dv), F32),
                   jax.ShapeDtypeStruct((nb, h, dk, dv), F32),
                   jax.ShapeDtypeStruct((nb, h, dk, dv), F32)],
        scratch_shapes=[pltpu.VMEM((dv, dk), F32), pltpu.VMEM((dv, dk), F32)],
        compiler_params=_params(("arbitrary", "arbitrary")),
        name="gla_scan",
    )(qk3, qk3, v3, bc3, bc3, sf0, sb0)


def _gla_out_kernel(na, oa_ref, ob_ref, r_ref, h_ref, mod_ref, on_ref, wo_ref, g2_ref, rw_ref,
                    h_out, f_out, idx_out, wt_out):
    i = pl.program_id(0)
    o = jnp.where(i < na, oa_ref[...], ob_ref[...])
    tm = o.shape[0]
    dv = GLA_DV
    parts = []
    for hh in range(GLA_HEADS):
        oh = o[:, hh * dv:(hh + 1) * dv]
        parts.append(_rms(oh) * on_ref[...])
    on = jnp.concatenate(parts, axis=1)
    r = r_ref[...].astype(F32)
    gated = (on * _silu(r)).astype(BF16)
    h = h_ref[...] + mod_ref[2:3, :] * jnp.dot(gated, wo_ref[...], preferred_element_type=F32)
    h_out[...] = h
    f = _adaln(h, g2_ref[...], mod_ref[3:4, :], mod_ref[4:5, :])
    f_out[...] = f
    fh = f.astype(BF16)
    fl = (f - fh.astype(F32)).astype(BF16)
    logits = (jnp.dot(fh, rw_ref[0], preferred_element_type=F32)
              + jnp.dot(fl, rw_ref[0], preferred_element_type=F32)
              + jnp.dot(fh, rw_ref[1], preferred_element_type=F32))
    lane = lax.broadcasted_iota(I32, (tm, LANES), 1)
    lane_f = lane.astype(F32)
    logits = jnp.where(lane < N_EXPERTS, logits, -jnp.inf)
    m1 = jnp.max(logits, axis=-1, keepdims=True)
    i1 = jnp.min(jnp.where(logits == m1, lane_f, float(LANES)), axis=-1, keepdims=True)
    rest = jnp.where(lane_f == i1, -jnp.inf, logits)
    m2 = jnp.max(rest, axis=-1, keepdims=True)
    i2 = jnp.min(jnp.where(rest == m2, lane_f, float(LANES)), axis=-1, keepdims=True)
    e2 = jnp.exp(m2 - m1)
    w1 = 1.0 / (1.0 + e2)
    w2 = e2 / (1.0 + e2)
    idx_out[...] = jnp.where(lane == 0, i1, jnp.where(lane == 1, i2, 0.0)).astype(I32)
    wt_out[...] = jnp.where(lane == 0, w1, jnp.where(lane == 1, w2, 0.0))


def _gla_out(o_p, o_s, r, h, n_p, t_len, mod, out_norm, wo, g2, router_w):
    n, d = h.shape
    tm = _row_tile(n_p, t_len)
    na = n_p // tm
    nt_s = t_len // tm
    mod_idx = lambda i: (jnp.where(i < na, 0, 1 + (i - na) // nt_s), 0, 0)
    row = lambda i: (i, 0)
    return pl.pallas_call(
        functools.partial(_gla_out_kernel, na),
        grid=(n // tm,),
        in_specs=[
            pl.BlockSpec((tm, d), lambda i: (jnp.minimum(i, na - 1), 0)),
            pl.BlockSpec((tm, d), lambda i: (jnp.maximum(i - na, 0), 0)),
            pl.BlockSpec((tm, d), row),
            pl.BlockSpec((tm, d), row),
            pl.BlockSpec((None, 8, d), mod_idx),
            _const_spec((1, GLA_DV)),
            _const_spec((d, d)),
            _const_spec((1, d)),
            _const_spec((2, d, LANES)),
        ],
        out_specs=[pl.BlockSpec((tm, d), row), pl.BlockSpec((tm, d), row),
                   pl.BlockSpec((tm, LANES), row), pl.BlockSpec((tm, LANES), row)],
        out_shape=[jax.ShapeDtypeStruct((n, d), F32), jax.ShapeDtypeStruct((n, d), F32),
                   jax.ShapeDtypeStruct((n, LANES), I32), jax.ShapeDtypeStruct((n, LANES), F32)],
        compiler_params=_params(("arbitrary",)),
        name="gla_out_router",
    )(o_p, o_s, r, h, mod, out_norm, wo, g2, router_w)


SC_CORES = 2
SC_SUBCORES = 16
SC_CHUNK = 32


def _sc_gather_rows(table, idx):
    nw = SC_CORES * SC_SUBCORES
    b = idx.shape[0]
    d = table.shape[1]
    assert b % (nw * SC_CHUNK) == 0
    per_w = b // nw
    n_chunks = per_w // SC_CHUNK
    mesh = plsc.VectorSubcoreMesh(core_axis_name="c", subcore_axis_name="s",
                                  num_cores=SC_CORES, num_subcores=SC_SUBCORES)

    def body(table_hbm, idx_hbm, out_hbm, idx_v, rows_v, sem):
        wid = lax.axis_index("s") * SC_CORES + lax.axis_index("c")
        base = wid * per_w
        pltpu.sync_copy(idx_hbm.at[wid], idx_v)

        @pl.loop(0, n_chunks)
        def _(j):
            pltpu.async_copy(table_hbm.at[idx_v.at[j]], rows_v, sem).wait()
            pltpu.sync_copy(rows_v, out_hbm.at[pl.ds(base + j * SC_CHUNK, SC_CHUNK)])

    return pl.kernel(
        body,
        out_type=jax.ShapeDtypeStruct((b, d), table.dtype),
        mesh=mesh,
        scratch_types=[pltpu.VMEM((n_chunks, SC_CHUNK), I32),
                       pltpu.VMEM((SC_CHUNK, d), table.dtype),
                       pltpu.SemaphoreType.DMA],
        name="sc_gather_rows",
    )(table, idx.reshape(nw, n_chunks, SC_CHUNK))


def _sc_scatter_rows(rows, idx2, p):
    nw = SC_CORES * SC_SUBCORES
    n, d = rows.shape
    assert n % (nw * SC_CHUNK) == 0
    per_w = n // nw
    n_chunks = per_w // SC_CHUNK
    mesh = plsc.VectorSubcoreMesh(core_axis_name="c", subcore_axis_name="s",
                                  num_cores=SC_CORES, num_subcores=SC_SUBCORES)

    def body(rows_hbm, idx0_hbm, idx1_hbm, out_hbm, idx0_v, idx1_v, rows_v):
        wid = lax.axis_index("s") * SC_CORES + lax.axis_index("c")
        base = wid * per_w
        pltpu.sync_copy(idx0_hbm.at[wid], idx0_v)
        pltpu.sync_copy(idx1_hbm.at[wid], idx1_v)

        @pl.loop(0, n_chunks)
        def _(j):
            pltpu.sync_copy(rows_hbm.at[pl.ds(base + j * SC_CHUNK, SC_CHUNK)], rows_v)
            pltpu.sync_copy(rows_v, out_hbm.at[idx0_v.at[j]])
            pltpu.sync_copy(rows_v, out_hbm.at[idx1_v.at[j]])

    return pl.kernel(
        body,
        out_type=jax.ShapeDtypeStruct((p, d), rows.dtype),
        mesh=mesh,
        scratch_types=[pltpu.VMEM((n_chunks, SC_CHUNK), I32),
                       pltpu.VMEM((n_chunks, SC_CHUNK), I32),
                       pltpu.VMEM((SC_CHUNK, d), rows.dtype)],
        name="sc_scatter_rows",
    )(rows, idx2[0].reshape(nw, n_chunks, SC_CHUNK), idx2[1].reshape(nw, n_chunks, SC_CHUNK))


EXPERT_SUB = 512


def _expert_kernel(te_ref, ts_ref, nu_ref, nv_ref, x_ref, wg_ref, wu_ref, wd_ref, y_ref,
                   acc, xb, wgb, wub, wdb):
    i = pl.program_id(0)
    f = pl.program_id(1)
    nf = pl.num_programs(1)
    tm = x_ref.shape[0]
    sub = EXPERT_SUB
    nvalid = nv_ref[i]

    @pl.when(nvalid > 0)
    def _():
        wgb[...] = wg_ref[...].astype(BF16)
        wub[...] = wu_ref[...].astype(BF16)
        wdb[...] = wd_ref[...].astype(BF16)

    for s in range(tm // sub):
        rows = slice(s * sub, (s + 1) * sub)

        @pl.when(s * sub < nvalid)
        def _():
            @pl.when(f == 0)
            def _():
                rid = s * sub + lax.broadcasted_iota(I32, (sub, 1), 0)
                xb[rows, :] = jnp.where(rid < nvalid, x_ref[rows, :], 0.0).astype(BF16)
                acc[rows, :] = jnp.zeros((sub, acc.shape[1]), F32)

            x = xb[rows, :]
            hg = jnp.dot(x, wgb[...], preferred_element_type=F32)
            hu = jnp.dot(x, wub[...], preferred_element_type=F32)
            act = (_silu(hg) * hu).astype(BF16)
            acc[rows, :] += jnp.dot(act, wdb[...], preferred_element_type=F32)

            @pl.when(f == nf - 1)
            def _():
                y_ref[rows, :] = acc[rows, :]

        @pl.when((s * sub >= nvalid) & (f == nf - 1))
        def _():
            y_ref[rows, :] = jnp.zeros((sub, y_ref.shape[1]), F32)


def _experts(x, tile_expert, tile_src, n_used, n_valid, wg, wu, wd, tm, tf):
    p, d = x.shape
    ne, _, dff = wg.shape
    nf = dff // tf
    assert dff % tf == 0 and tm % EXPERT_SUB == 0

    def fidx(i, f, nu):
        return jnp.where(i < nu[0], f, nf - 1)

    grid_spec = pltpu.PrefetchScalarGridSpec(
        num_scalar_prefetch=4,
        grid=(p // tm, nf),
        in_specs=[
            pl.BlockSpec((tm, d), lambda i, f, te, ts, nu, nv: (ts[i], 0)),
            pl.BlockSpec((None, d, tf), lambda i, f, te, ts, nu, nv: (te[i], 0, fidx(i, f, nu))),
            pl.BlockSpec((None, d, tf), lambda i, f, te, ts, nu, nv: (te[i], 0, fidx(i, f, nu))),
            pl.BlockSpec((None, tf, d), lambda i, f, te, ts, nu, nv: (te[i], fidx(i, f, nu), 0)),
        ],
        out_specs=pl.BlockSpec((tm, d), lambda i, f, te, ts, nu, nv: (i, 0)),
        scratch_shapes=[pltpu.VMEM((tm, d), F32), pltpu.VMEM((tm, d), BF16),
                        pltpu.VMEM((d, tf), BF16), pltpu.VMEM((d, tf), BF16), pltpu.VMEM((tf, d), BF16)],
    )
    return pl.pallas_call(
        _expert_kernel,
        grid_spec=grid_spec,
        out_shape=jax.ShapeDtypeStruct((p, d), F32),
        compiler_params=_params(("arbitrary", "arbitrary"), 60 * 1024 * 1024),
        name="moe_experts",
    )(tile_expert, tile_src, n_used, n_valid, x, wg, wu, wd)


def _combine_kernel(wt_ref, h_ref, mod_ref, y0_ref, y1_ref, o_ref):
    wt = wt_ref[...]
    up = wt[:, 0:1] * y0_ref[...] + wt[:, 1:2] * y1_ref[...]
    o_ref[...] = h_ref[...] + mod_ref[5:6, :] * up


def _combine(yg, wt, h, mod, row_off, n_rows, n_mod, mod_off, tm):
    d = h.shape[1]
    nt = n_rows // tm
    toff = row_off // tm
    per_mod = n_rows // n_mod // tm
    return pl.pallas_call(
        _combine_kernel,
        grid=(nt,),
        in_specs=[
            pl.BlockSpec((tm, LANES), lambda i: (toff + i, 0)),
            pl.BlockSpec((tm, d), lambda i: (toff + i, 0)),
            pl.BlockSpec((None, 8, d), lambda i: (mod_off + i // per_mod, 0, 0)),
            pl.BlockSpec((None, tm, d), lambda i: (0, toff + i, 0)),
            pl.BlockSpec((None, tm, d), lambda i: (1, toff + i, 0)),
        ],
        out_specs=pl.BlockSpec((tm, d), lambda i: (i, 0)),
        out_shape=jax.ShapeDtypeStruct((n_rows, d), F32),
        compiler_params=_params(("arbitrary",)),
        name="moe_combine",
    )(wt, h, mod, yg, yg)


def _route(idx, n, tm):
    e = idx[:, :2].reshape(-1)
    onehot = (e[:, None] == jnp.arange(N_EXPERTS, dtype=I32)[None, :]).astype(I32)
    csum = jnp.cumsum(onehot, axis=0)
    cnt = csum[-1]
    rank = jnp.take_along_axis(csum, e[:, None], axis=1)[:, 0] - 1
    padded = ((cnt + tm - 1) // tm) * tm
    gend = jnp.cumsum(padded)
    goff = gend - padded
    dest = goff[e] + rank
    p = 2 * n + N_EXPERTS * tm
    n_used = gend[-1] // tm
    tiles = jnp.arange(p // tm, dtype=I32)
    tile_src = jnp.minimum(tiles, n_used - 1)
    tile_expert = jnp.minimum(jnp.sum((gend[None, :] <= (tile_src * tm)[:, None]).astype(I32), axis=1),
                              N_EXPERTS - 1)
    used = goff[tile_expert] + cnt[tile_expert]
    n_valid = jnp.where(tiles < n_used, jnp.clip(used - tiles * tm, 0, tm), 0).astype(I32)
    return dest, p, tile_expert, tile_src, n_used.reshape(1).astype(I32), n_valid


def kernel(x_prompt, x_sample, cache_k, cache_v, state_fwd, state_bwd, c, c_ctx, ada_w, ada_b, norm1_g, norm2_g, attn_w_qkv, attn_q_norm, attn_k_norm, attn_sink, attn_w_o, gla_w_in, gla_gate_w1, gla_gate_w2, gla_gate_b, gla_out_norm, gla_w_o, ffn_w_gate, ffn_w_up, ffn_w_down, moe_router, moe_w_gate, moe_w_up, moe_w_down):
    bp, seq, d = x_prompt.shape
    db, t_len, _ = x_sample.shape
    n_p, n_s = bp * seq, db * t_len
    n = n_p + n_s
    xp = x_prompt.reshape(n_p, d)
    xs = x_sample.reshape(n_s, d)

    cond = jnp.concatenate([c_ctx[None, :], c], axis=0)
    assert cond.shape[0] <= 8
    cond_t = jnp.pad(cond, ((0, 8 - cond.shape[0]), (0, 0))).T
    mods = _modulation(cond_t, cond.shape[0], ada_w, ada_b)

    nk = N_KV_HEADS * HEAD_DIM
    qn = jnp.tile(attn_q_norm[0], N_HEADS)[None, :]
    kn = jnp.tile(attn_k_norm[0], N_KV_HEADS)[None, :]
    q, kv = _qkv(xp, xs, t_len, mods[0], norm1_g[0][None, :], attn_w_qkv[0].astype(BF16), qn, kn)
    wo0 = attn_w_o[0].astype(BF16)
    sink = attn_sink[0]
    hp = _ctx_attention(sink, q, kv, xp, seq, mods[0], wo0)
    ck = cache_k[:, 0].reshape(db, cache_k.shape[2], nk)
    cv = cache_v[:, 0].reshape(db, cache_v.shape[2], nk)
    hs = _lat_attention(sink, q, kv, ck, cv, xs, n_p, t_len, mods[0], wo0)
    new_k = kv[:n_p, :nk].reshape(bp, 1, seq, N_KV_HEADS, HEAD_DIM)
    new_v = kv[:n_p, nk:].reshape(bp, 1, seq, N_KV_HEADS, HEAD_DIM)
    h = _ffn(hp, hs, t_len, mods[0], norm2_g[0][None, :], ffn_w_gate[0].astype(BF16),
             ffn_w_up[0].astype(BF16), ffn_w_down[0].astype(BF16))

    hk = GLA_HEADS * GLA_DK
    rank = GLA_GATE_RANK
    w1 = jnp.concatenate([gla_gate_w1[0, 0], gla_gate_w1[0, 1]], axis=1)
    w1 = jnp.pad(w1, ((0, 0), (0, LANES - 2 * rank))).astype(BF16)
    w2 = jnp.zeros((LANES, 2 * hk), F32)
    w2 = w2.at[0:rank, 0:hk].set(gla_gate_w2[0, 0]).at[rank:2 * rank, hk:].set(gla_gate_w2[0, 1]).astype(BF16)
    gate_b = gla_gate_b[0].reshape(1, 2 * hk)
    qk, v, r, bc = _gla_in(h, n_p, t_len, mods[1], norm1_g[1][None, :], gla_w_in[0].astype(BF16),
                           w1, w2, gate_b)
    zeros = jnp.zeros((bp, GLA_HEADS, GLA_DK, GLA_DV), F32)
    o_p, new_sf, new_sb = _gla_scan(qk, v, bc, zeros, zeros, 0, bp, seq)
    o_s, _, _ = _gla_scan(qk, v, bc, state_fwd[:, 0], state_bwd[:, 0], n_p // t_len, db, t_len)
    rw = jnp.pad(moe_router[0], ((0, 0), (0, LANES - N_EXPERTS)))
    rw_hi = rw.astype(BF16)
    rw_lo = (rw - rw_hi.astype(F32)).astype(BF16)
    h, f, idx, wt = _gla_out(o_p.reshape(n_p, d), o_s.reshape(n_s, d), r, h, n_p, t_len, mods[1],
                             gla_out_norm[0][None, :], gla_w_o[0].astype(BF16), norm2_g[1][None, :],
                             jnp.stack([rw_hi, rw_lo]))

    tm_e = 1024
    dest, p_rows, tile_expert, tile_src, n_used, n_valid = _route(idx, n, tm_e)
    dest_k = dest.reshape(n, 2).T
    xg = _sc_scatter_rows(f, dest_k, p_rows)
    y = _experts(xg, tile_expert, tile_src, n_used, n_valid, moe_w_gate[0], moe_w_up[0],
                 moe_w_down[0], tm_e, 896)
    yg = _sc_gather_rows(y, dest_k.reshape(2 * n)).reshape(2, n, d)
    tm_c = _row_tile(n_p, t_len)
    y_p = _combine(yg, wt, h, mods[1], 0, n_p, 1, 0, tm_c)
    y_s = _combine(yg, wt, h, mods[1], n_p, n_s, db, 1, tm_c)

    return (y_p.reshape(bp, seq, d), y_s.reshape(db, t_len, d), new_k, new_v,
            new_sf[:, None], new_sb[:, None])
```

```python
import functools
import math

import jax
import jax.numpy as jnp
from jax import lax
from jax.experimental import pallas as pl
from jax.experimental.pallas import tpu as pltpu
from jax.experimental.pallas import tpu_sc as plsc

F32 = jnp.float32
BF16 = jnp.bfloat16
I32 = jnp.int32

D_MODEL = 1024
N_HEADS = 16
N_KV_HEADS = 4
HEAD_DIM = 64
GRID_W = 64
WINDOW = 128
ATTN_BLOCK = 128
ROPE_THETA = 10000.0
GLA_HEADS = 4
GLA_DK = 128
GLA_DV = 256
GLA_GATE_RANK = 16
GLA_GATE_TAU = 16.0
GLA_CHUNK = 64
N_EXPERTS = 8
NORM_EPS = 1e-6
NEG_INF = -1e30

LANES = 128
VMEM_LIMIT = 56 * 1024 * 1024


def _params(sem, vmem=VMEM_LIMIT):
    return pltpu.CompilerParams(dimension_semantics=sem, vmem_limit_bytes=vmem)


def _row_tile(*counts, cap=512):
    t = cap
    while any(c % t for c in counts):
        t //= 2
    assert t >= 8
    return t


def _rms(x):
    return x * lax.rsqrt(jnp.mean(x * x, axis=-1, keepdims=True) + NORM_EPS)


def _adaln(x, g, shift, scale):
    return _rms(x) * g * (1.0 + scale) + shift


def _silu(x):
    return x * jax.nn.sigmoid(x)


def _const_spec(shape):
    nd = len(shape)
    return pl.BlockSpec(shape, lambda *_: (0,) * nd)


def _mod_kernel(ct_ref, w_ref, b_ref, o_ref):
    ct = ct_ref[...]
    s = _silu(ct)
    w = w_ref[...]
    rows = [jnp.sum(w * s[:, r:r + 1], axis=0, keepdims=True) for r in range(8)]
    o_ref[...] = jnp.concatenate(rows, axis=0) + b_ref[...]


def _modulation(cond_t, n_cond, ada_w, ada_b):
    depth, d, n6 = ada_w.shape
    tn = 1024
    out = pl.pallas_call(
        _mod_kernel,
        grid=(depth, n6 // tn),
        in_specs=[
            pl.BlockSpec((d, 8), lambda l, j: (0, 0)),
            pl.BlockSpec((None, d, tn), lambda l, j: (l, 0, j)),
            pl.BlockSpec((None, 1, tn), lambda l, j: (l, 0, j)),
        ],
        out_specs=pl.BlockSpec((None, 8, tn), lambda l, j: (l, 0, j)),
        out_shape=jax.ShapeDtypeStruct((depth, 8, n6), F32),
        compiler_params=_params(("arbitrary", "arbitrary")),
        name="modulation",
    )(cond_t, ada_w, ada_b.reshape(depth, 1, n6))
    m = out[:, :n_cond].reshape(depth, n_cond, 6, d)
    return jnp.pad(m, ((0, 0), (0, 0), (0, 2), (0, 0)))


def _qkv_kernel(na, nt_s, xa_ref, xb_ref, mod_ref, g_ref, w_ref, qn_ref, kn_ref, bd_ref,
                cos_ref, sin_ref, q_out, kv_out):
    i = pl.program_id(0)
    is_ctx = i < na
    x = jnp.where(is_ctx, xa_ref[...], xb_ref[...])
    a = _adaln(x, g_ref[...], mod_ref[0:1, :], mod_ref[1:2, :])
    y = jnp.dot(a.astype(BF16), w_ref[...], preferred_element_type=F32)
    tm = x.shape[0]
    cos = jnp.where(is_ctx, 1.0, cos_ref[...])
    sin = jnp.where(is_ctx, 0.0, sin_ref[...])
    lane = lax.broadcasted_iota(I32, (tm, LANES), 1)
    first16 = (lane % 32) < 16

    def norm_rope(z, wt):
        ss = jnp.dot((z * z).astype(BF16), bd_ref[...], preferred_element_type=F32)
        zn = z * lax.rsqrt(ss * (1.0 / HEAD_DIM) + NORM_EPS) * wt
        outs = []
        for c in range(2):
            t = zn[:, c * LANES:(c + 1) * LANES]
            partner = jnp.where(first16, pltpu.roll(t, LANES - 16, 1), pltpu.roll(t, 16, 1))
            outs.append(t * cos + partner * sin)
        return jnp.concatenate(outs, axis=1)

    nq = N_HEADS * HEAD_DIM
    nk = N_KV_HEADS * HEAD_DIM
    scale = HEAD_DIM ** -0.5
    for s in range(nq // 256):
        sl = slice(s * 256, (s + 1) * 256)
        q_out[:, sl] = (norm_rope(y[:, sl], qn_ref[:, sl]) * scale).astype(BF16)
    kv_out[:, 0:nk] = norm_rope(y[:, nq:nq + nk], kn_ref[...])
    kv_out[:, nk:2 * nk] = y[:, nq + nk:nq + 2 * nk]


def _rope_tables(t_len):
    pos = jnp.arange(t_len)
    row = (pos // GRID_W).astype(F32)[:, None]
    col = (pos % GRID_W).astype(F32)[:, None]
    half = HEAD_DIM // 2
    inv = ROPE_THETA ** (-jnp.arange(0, half, 2, dtype=F32) / half)[None, :]
    ar, ac = row * inv, col * inv
    cos = jnp.concatenate([jnp.cos(ar), jnp.cos(ar), jnp.cos(ac), jnp.cos(ac)], axis=1)
    sin = jnp.concatenate([-jnp.sin(ar), jnp.sin(ar), -jnp.sin(ac), jnp.sin(ac)], axis=1)
    return jnp.tile(cos, (1, 2)), jnp.tile(sin, (1, 2))


def _qkv(xp, xs, t_len, mod, g, w, qn, kn):
    n_p, n_s = xp.shape[0], xs.shape[0]
    tm = _row_tile(n_p, t_len)
    na, nb = n_p // tm, n_s // tm
    nt_s = t_len // tm
    n = n_p + n_s
    d = D_MODEL
    nq, nk = N_HEADS * HEAD_DIM, N_KV_HEADS * HEAD_DIM
    cos, sin = _rope_tables(t_len)
    eye4 = jnp.kron(jnp.eye(4, dtype=F32), jnp.ones((HEAD_DIM, HEAD_DIM), F32)).astype(BF16)
    mod_idx = lambda i: (jnp.where(i < na, 0, 1 + (i - na) // nt_s), 0, 0)
    pos_idx = lambda i: (jnp.maximum(i - na, 0) % nt_s, 0)
    return pl.pallas_call(
        functools.partial(_qkv_kernel, na, nt_s),
        grid=(na + nb,),
        in_specs=[
            pl.BlockSpec((tm, d), lambda i: (jnp.minimum(i, na - 1), 0)),
            pl.BlockSpec((tm, d), lambda i: (jnp.maximum(i - na, 0), 0)),
            pl.BlockSpec((None, 8, d), mod_idx),
            _const_spec((1, d)),
            _const_spec((d, nq + 2 * nk)),
            _const_spec((1, nq)),
            _const_spec((1, nk)),
            _const_spec((256, 256)),
            pl.BlockSpec((tm, LANES), pos_idx),
            pl.BlockSpec((tm, LANES), pos_idx),
        ],
        out_specs=[pl.BlockSpec((tm, nq), lambda i: (i, 0)),
                   pl.BlockSpec((tm, 2 * nk), lambda i: (i, 0))],
        out_shape=[jax.ShapeDtypeStruct((n, nq), BF16), jax.ShapeDtypeStruct((n, 2 * nk), F32)],
        compiler_params=_params(("arbitrary",)),
        name="qkv",
    )(xp, xs, mod, g, w, qn, kn, eye4, cos, sin)


def _dup_half(k2, half):
    lane = lax.broadcasted_iota(I32, k2.shape, 1)
    lo = lane < HEAD_DIM
    r = pltpu.roll(k2, HEAD_DIM, 1)
    return jnp.where(lo, k2, r) if half == 0 else jnp.where(lo, r, k2)


def _attn_group(q2a, q2b, kk, vv, valid, sinks):
    tq = q2a.shape[0]
    lane = lax.broadcasted_iota(I32, (tq, LANES), 1)
    lo = lane < HEAD_DIM
    zero = jnp.zeros_like(q2a)
    qs = jnp.concatenate([jnp.where(lo, q2a, zero), jnp.where(lo, zero, q2a),
                          jnp.where(lo, q2b, zero), jnp.where(lo, zero, q2b)], axis=0)
    s = lax.dot_general(qs, kk, (((1,), (1,)), ((), ())), preferred_element_type=F32)
    if valid is not None:
        s = jnp.where(valid, s, NEG_INF)
    sink = jnp.concatenate([jnp.full((tq, 1), sk, F32) for sk in sinks], axis=0)
    m = jnp.maximum(jnp.max(s, axis=-1, keepdims=True), sink)
    p = jnp.exp(s - m)
    den = jnp.sum(p, axis=-1, keepdims=True) + jnp.exp(sink - m)
    o = jnp.dot(p.astype(BF16), vv, preferred_element_type=F32) / den
    oa = jnp.where(lo, o[0:tq], o[tq:2 * tq])
    ob = jnp.where(lo, o[2 * tq:3 * tq], o[3 * tq:4 * tq])
    return oa, ob


def _ctx_attn_kernel(sink_ref, q_ref, kv_ref, x_ref, mod_ref, wo_ref, o_ref, osc):
    nk = N_KV_HEADS * HEAD_DIM
    for g in range(N_KV_HEADS):
        c = g // 2
        kk = _dup_half(kv_ref[:, c * LANES:(c + 1) * LANES], g % 2).astype(BF16)
        vv = _dup_half(kv_ref[:, nk + c * LANES:nk + (c + 1) * LANES], g % 2).astype(BF16)
        q2a = q_ref[:, (2 * g) * LANES:(2 * g + 1) * LANES]
        q2b = q_ref[:, (2 * g + 1) * LANES:(2 * g + 2) * LANES]
        sinks = [sink_ref[4 * g + j] for j in range(4)]
        oa, ob = _attn_group(q2a, q2b, kk, vv, None, sinks)
        osc[:, (2 * g) * LANES:(2 * g + 1) * LANES] = oa.astype(BF16)
        osc[:, (2 * g + 1) * LANES:(2 * g + 2) * LANES] = ob.astype(BF16)
    att = jnp.dot(osc[...], wo_ref[...], preferred_element_type=F32)
    o_ref[...] = x_ref[...] + mod_ref[2:3, :] * att


def _ctx_attention(sink, q, kv, xp, seq, mod, wo):
    n_p, d = xp.shape
    nb = n_p // seq
    return pl.pallas_call(
        _ctx_attn_kernel,
        grid=(nb,),
        in_specs=[
            pl.BlockSpec(memory_space=pltpu.SMEM),
            pl.BlockSpec((seq, d), lambda b: (b, 0)),
            pl.BlockSpec((seq, kv.shape[1]), lambda b: (b, 0)),
            pl.BlockSpec((seq, d), lambda b: (b, 0)),
            pl.BlockSpec((None, 8, d), lambda b: (0, 0, 0)),
            _const_spec((d, d)),
        ],
        out_specs=pl.BlockSpec((seq, d), lambda b: (b, 0)),
        out_shape=jax.ShapeDtypeStruct((n_p, d), F32),
        scratch_shapes=[pltpu.VMEM((seq, d), BF16)],
        compiler_params=_params(("arbitrary",)),
        name="ctx_attention",
    )(sink, q, kv, xp, mod, wo)


def _lat_attn_kernel(t_len, sink_ref, q_ref, kv_ref, ck_ref, cv_ref, x_ref, mod_ref, wo_ref, o_ref, osc):
    n = pl.program_id(1)
    tq = ATTN_BLOCK
    nloc = 3 * ATTN_BLOCK
    npast = ck_ref.shape[0]
    nk = N_KV_HEADS * HEAD_DIM
    start = pl.multiple_of(jnp.clip(n * tq - ATTN_BLOCK, 0, t_len - nloc), ATTN_BLOCK)
    kvw = kv_ref[pl.ds(start, nloc), :]
    r = lax.broadcasted_iota(I32, (4 * tq, nloc + npast), 0)
    cidx = lax.broadcasted_iota(I32, (4 * tq, nloc + npast), 1)
    qpos = n * tq + (r & (tq - 1))
    kpos = start + cidx
    valid = (cidx >= nloc) | (jnp.abs(qpos - kpos) <= WINDOW)
    for g in range(N_KV_HEADS):
        c = g // 2
        ksl = slice(c * LANES, (c + 1) * LANES)
        vsl = slice(nk + c * LANES, nk + (c + 1) * LANES)
        kk = jnp.concatenate([_dup_half(kvw[:, ksl], g % 2), _dup_half(ck_ref[:, ksl], g % 2)],
                             axis=0).astype(BF16)
        vv = jnp.concatenate([_dup_half(kvw[:, vsl], g % 2), _dup_half(cv_ref[:, ksl], g % 2)],
                             axis=0).astype(BF16)
        q2a = q_ref[:, (2 * g) * LANES:(2 * g + 1) * LANES]
        q2b = q_ref[:, (2 * g + 1) * LANES:(2 * g + 2) * LANES]
        sinks = [sink_ref[4 * g + j] for j in range(4)]
        oa, ob = _attn_group(q2a, q2b, kk, vv, valid, sinks)
        osc[:, (2 * g) * LANES:(2 * g + 1) * LANES] = oa.astype(BF16)
        osc[:, (2 * g + 1) * LANES:(2 * g + 2) * LANES] = ob.astype(BF16)
    att = jnp.dot(osc[...], wo_ref[...], preferred_element_type=F32)
    o_ref[...] = x_ref[...] + mod_ref[2:3, :] * att


def _lat_attention(sink, q, kv, ck, cv, xs, n_p, t_len, mod, wo):
    n_s, d = xs.shape
    db = n_s // t_len
    tq = ATTN_BLOCK
    nblk = t_len // tq
    assert n_p % t_len == 0 and n_p % tq == 0
    kv_off = n_p // t_len
    q_off = n_p // tq
    npast = ck.shape[1]
    return pl.pallas_call(
        functools.partial(_lat_attn_kernel, t_len),
        grid=(db, nblk),
        in_specs=[
            pl.BlockSpec(memory_space=pltpu.SMEM),
            pl.BlockSpec((tq, d), lambda b, n: (q_off + b * nblk + n, 0)),
            pl.BlockSpec((t_len, kv.shape[1]), lambda b, n: (kv_off + b, 0)),
            pl.BlockSpec((None, npast, ck.shape[2]), lambda b, n: (b, 0, 0)),
            pl.BlockSpec((None, npast, cv.shape[2]), lambda b, n: (b, 0, 0)),
            pl.BlockSpec((tq, d), lambda b, n: (b * nblk + n, 0)),
            pl.BlockSpec((None, 8, d), lambda b, n: (1 + b, 0, 0)),
            _const_spec((d, d)),
        ],
        out_specs=pl.BlockSpec((tq, d), lambda b, n: (b * nblk + n, 0)),
        out_shape=jax.ShapeDtypeStruct((n_s, d), F32),
        scratch_shapes=[pltpu.VMEM((tq, d), BF16)],
        compiler_params=_params(("arbitrary", "arbitrary")),
        name="lat_attention",
    )(sink, q, kv, ck, cv, xs, mod, wo)


def _ffn_kernel(na, fc, xa_ref, xb_ref, mod_ref, g_ref, wg_ref, wu_ref, wd_ref, o_ref, acc):
    i = pl.program_id(0)
    x = jnp.where(i < na, xa_ref[...], xb_ref[...])
    a = _adaln(x, g_ref[...], mod_ref[3:4, :], mod_ref[4:5, :]).astype(BF16)
    nf = wg_ref.shape[1] // fc
    for f in range(nf):
        sl = slice(f * fc, (f + 1) * fc)
        hg = jnp.dot(a, wg_ref[:, sl], preferred_element_type=F32)
        hu = jnp.dot(a, wu_ref[:, sl], preferred_element_type=F32)
        act = (_silu(hg) * hu).astype(BF16)
        part = jnp.dot(act, wd_ref[sl, :], preferred_element_type=F32)
        if f == 0:
            acc[...] = part
        else:
            acc[...] += part
    o_ref[...] = x + mod_ref[5:6, :] * acc[...]


def _ffn(hp, hs, t_len, mod, g, wg, wu, wd):
    n_p, n_s = hp.shape[0], hs.shape[0]
    d, dff = wg.shape
    tm = _row_tile(n_p, t_len)
    na, nb = n_p // tm, n_s // tm
    nt_s = t_len // tm
    fc = 256
    assert dff % fc == 0
    mod_idx = lambda i: (jnp.where(i < na, 0, 1 + (i - na) // nt_s), 0, 0)
    return pl.pallas_call(
        functools.partial(_ffn_kernel, na, fc),
        grid=(na + nb,),
        in_specs=[
            pl.BlockSpec((tm, d), lambda i: (jnp.minimum(i, na - 1), 0)),
            pl.BlockSpec((tm, d), lambda i: (jnp.maximum(i - na, 0), 0)),
            pl.BlockSpec((None, 8, d), mod_idx),
            _const_spec((1, d)),
            _const_spec((d, dff)),
            _const_spec((d, dff)),
            _const_spec((dff, d)),
        ],
        out_specs=pl.BlockSpec((tm, d), lambda i: (i, 0)),
        out_shape=jax.ShapeDtypeStruct((n_p + n_s, d), F32),
        scratch_shapes=[pltpu.VMEM((tm, d), F32)],
        compiler_params=_params(("arbitrary",)),
        name="ffn",
    )(hp, hs, mod, g, wg, wu, wd)


def _split3(x):
    hi = x.astype(BF16)
    r1 = x - hi.astype(F32)
    mid = r1.astype(BF16)
    lo = (r1 - mid.astype(F32)).astype(BF16)
    return hi, mid, lo


def _gla_in_kernel(nt_s, x_ref, mod_ref, g_ref, w_ref, w1_ref, w2_ref, gb_ref, tri_ref,
                   qk_out, v_out, r_out, b_out):
    x = x_ref[...]
    tm = x.shape[0]
    a = _adaln(x, g_ref[...], mod_ref[0:1, :], mod_ref[1:2, :]).astype(BF16)
    hk = GLA_HEADS * GLA_DK
    hv = GLA_HEADS * GLA_DV
    q = jnp.dot(a, w_ref[:, 0:hk], preferred_element_type=F32)
    qk_out[:, 0:hk] = q * (GLA_DK ** -0.5)
    qk_out[:, hk:2 * hk] = jnp.dot(a, w_ref[:, hk:2 * hk], preferred_element_type=F32)
    v_out[...] = jnp.dot(a, w_ref[:, 2 * hk:2 * hk + hv], preferred_element_type=F32)
    r_out[...] = jnp.dot(a, w_ref[:, 2 * hk + hv:2 * hk + 2 * hv], preferred_element_type=F32).astype(BF16)
    z1 = jnp.dot(a, w1_ref[...], preferred_element_type=F32)
    z1h = z1.astype(BF16)
    z1l = (z1 - z1h.astype(F32)).astype(BF16)
    z = (jnp.dot(z1h, w2_ref[...], preferred_element_type=F32)
         + jnp.dot(z1l, w2_ref[...], preferred_element_type=F32) + gb_ref[...])
    gate = (jnp.minimum(z, 0.0) - jnp.log(1.0 + jnp.exp(-jnp.abs(z)))) * (1.0 / GLA_GATE_TAU)
    c = GLA_CHUNK
    tri = tri_ref[...]
    for j in range(tm // c):
        rows = slice(j * c, (j + 1) * c)
        for dr in range(2):
            cols = slice(dr * hk, (dr + 1) * hk)
            parts = _split3(gate[rows, cols])
            acc = jnp.dot(tri[dr], parts[0], preferred_element_type=F32)
            acc += jnp.dot(tri[dr], parts[1], preferred_element_type=F32)
            acc += jnp.dot(tri[dr], parts[2], preferred_element_type=F32)
            b_out[rows, cols] = acc


def _gla_in(h, n_p, t_len, mod, g, w_in, w1, w2, gate_b):
    n, d = h.shape
    tm = _row_tile(n_p, t_len)
    na = n_p // tm
    nt_s = t_len // tm
    hk, hv = GLA_HEADS * GLA_DK, GLA_HEADS * GLA_DV
    c = GLA_CHUNK
    lower = jnp.tril(jnp.ones((c, c), F32))
    tri = jnp.stack([lower, lower.T]).astype(BF16)
    mod_idx = lambda i: (jnp.where(i < na, 0, 1 + (i - na) // nt_s), 0, 0)
    row = lambda i: (i, 0)
    return pl.pallas_call(
        functools.partial(_gla_in_kernel, nt_s),
        grid=(n // tm,),
        in_specs=[
            pl.BlockSpec((tm, d), row),
            pl.BlockSpec((None, 8, d), mod_idx),
            _const_spec((1, d)),
            _const_spec(w_in.shape),
            _const_spec(w1.shape),
            _const_spec(w2.shape),
            _const_spec((1, 2 * hk)),
            _const_spec((2, c, c)),
        ],
        out_specs=[pl.BlockSpec((tm, 2 * hk), row), pl.BlockSpec((tm, hv), row),
                   pl.BlockSpec((tm, hv), row), pl.BlockSpec((tm, 2 * hk), row)],
        out_shape=[jax.ShapeDtypeStruct((n, 2 * hk), F32), jax.ShapeDtypeStruct((n, hv), F32),
                   jax.ShapeDtypeStruct((n, hv), BF16), jax.ShapeDtypeStruct((n, 2 * hk), F32)],
        compiler_params=_params(("arbitrary",)),
        name="gla_in",
    )(h, mod, g, w_in, w1, w2, gate_b, tri)


def _gla_chunk(q, k, v, b, st_ref, forward):
    c = q.shape[0]
    if forward:
        b_end, b_mid = b[c - 1:c, :], b[c // 2 - 1:c // 2, :]
    else:
        b_end, b_mid = b[0:1, :], b[c // 2:c // 2 + 1, :]
    e_up = jnp.exp(b - b_mid)
    e_dn = jnp.exp(b_mid - b)
    qa = q * e_up
    ka = k * e_dn
    qd = (qa * jnp.exp(b_mid)).astype(BF16)
    kd = (ka * jnp.exp(b_end - b_mid)).astype(BF16)
    att = lax.dot_general(qa.astype(BF16), ka.astype(BF16), (((1,), (1,)), ((), ())),
                          preferred_element_type=F32)
    ri = lax.broadcasted_iota(I32, (c, c), 0)
    ci = lax.broadcasted_iota(I32, (c, c), 1)
    keep = (ci <= ri) if forward else (ci >= ri)
    att = jnp.where(keep, att, 0.0).astype(BF16)
    vb = v.astype(BF16)
    st = st_ref[...]
    o = lax.dot_general(qd, st.astype(BF16), (((1,), (1,)), ((), ())), preferred_element_type=F32)
    o = o + jnp.dot(att, vb, preferred_element_type=F32)
    upd = lax.dot_general(vb, kd, (((0,), (0,)), ((), ())), preferred_element_type=F32)
    st_ref[...] = st * jnp.exp(b_end) + upd
    return o


def _gla_scan_kernel(q_ref, k_ref, v_ref, bf_ref, bb_ref, sf0_ref, sb0_ref, o_ref, sf_ref, sb_ref,
                     stf, stb):
    t_len = q_ref.shape[0]
    c = GLA_CHUNK
    nc = t_len // c
    stf[...] = sf0_ref[...].T
    stb[...] = sb0_ref[...].T

    def step(ci, accumulate):
        rf = pl.ds(pl.multiple_of(ci * c, c), c)
        rb = pl.ds(pl.multiple_of((nc - 1 - ci) * c, c), c)
        of = _gla_chunk(q_ref[rf, :], k_ref[rf, :], v_ref[rf, :], bf_ref[rf, :], stf, True)
        ob = _gla_chunk(q_ref[rb, :], k_ref[rb, :], v_ref[rb, :], bb_ref[rb, :], stb, False)
        if accumulate:
            o_ref[rf, :] += of
            o_ref[rb, :] += ob
        else:
            o_ref[rf, :] = of
            o_ref[rb, :] = ob

    def first(ci, carry):
        step(ci, False)
        return carry

    def second(ci, carry):
        step(ci, True)
        return carry

    lax.fori_loop(0, nc // 2, first, 0)
    lax.fori_loop(nc // 2, nc, second, 0)
    sf_ref[...] = stf[...].T
    sb_ref[...] = stb[...].T


def _gla_scan(qk, v, bc, sf0, sb0, b_off, nb, t_len):
    n = qk.shape[0]
    assert n % t_len == 0 and t_len % (2 * GLA_CHUNK) == 0
    ns = n // t_len
    qk3 = qk.reshape(ns, t_len, qk.shape[1])
    v3 = v.reshape(ns, t_len, v.shape[1])
    bc3 = bc.reshape(ns, t_len, bc.shape[1])
    h, dk, dv = GLA_HEADS, GLA_DK, GLA_DV
    st_spec = pl.BlockSpec((None, None, dk, dv), lambda b, hh: (b, hh, 0, 0))
    return pl.pallas_call(
        _gla_scan_kernel,
        grid=(nb, h),
        in_specs=[
            pl.BlockSpec((None, t_len, dk), lambda b, hh: (b_off + b, 0, hh)),
            pl.BlockSpec((None, t_len, dk), lambda b, hh: (b_off + b, 0, h + hh)),
            pl.BlockSpec((None, t_len, dv), lambda b, hh: (b_off + b, 0, hh)),
            pl.BlockSpec((None, t_len, dk), lambda b, hh: (b_off + b, 0, hh)),
            pl.BlockSpec((None, t_len, dk), lambda b, hh: (b_off + b, 0, h + hh)),
            st_spec, st_spec,
        ],
        out_specs=[pl.BlockSpec((None, t_len, dv), lambda b, hh: (b, 0, hh)), st_spec, st_spec],
        out_shape=[jax.ShapeDtypeStruct((nb, t_len, h * dv), F32),
                   jax.ShapeDtypeStruct((nb, h, dk, dv), F32),
                   jax.ShapeDtypeStruct((nb, h, dk, dv), F32)],
        scratch_shapes=[pltpu.VMEM((dv, dk), F32), pltpu.VMEM((dv, dk), F32)],
        compiler_params=_params(("arbitrary", "arbitrary")),
        name="gla_scan",
    )(qk3, qk3, v3, bc3, bc3, sf0, sb0)


def _gla_out_kernel(na, oa_ref, ob_ref, r_ref, h_ref, mod_ref, on_ref, wo_ref, g2_ref, rw_ref,
                    h_out, f_out, idx_out, wt_out):
    i = pl.program_id(0)
    o = jnp.where(i < na, oa_ref[...], ob_ref[...])
    tm = o.shape[0]
    dv = GLA_DV
    parts = []
    for hh in range(GLA_HEADS):
        oh = o[:, hh * dv:(hh + 1) * dv]
        parts.append(_rms(oh) * on_ref[...])
    on = jnp.concatenate(parts, axis=1)
    r = r_ref[...].astype(F32)
    gated = (on * _silu(r)).astype(BF16)
    h = h_ref[...] + mod_ref[2:3, :] * jnp.dot(gated, wo_ref[...], preferred_element_type=F32)
    h_out[...] = h
    f = _adaln(h, g2_ref[...], mod_ref[3:4, :], mod_ref[4:5, :])
    f_out[...] = _pack_halves(f)
    fh = f.astype(BF16)
    fl = (f - fh.astype(F32)).astype(BF16)
    logits = (jnp.dot(fh, rw_ref[0], preferred_element_type=F32)
              + jnp.dot(fl, rw_ref[0], preferred_element_type=F32)
              + jnp.dot(fh, rw_ref[1], preferred_element_type=F32))
    lane = lax.broadcasted_iota(I32, (tm, LANES), 1)
    lane_f = lane.astype(F32)
    logits = jnp.where(lane < N_EXPERTS, logits, -jnp.inf)
    m1 = jnp.max(logits, axis=-1, keepdims=True)
    i1 = jnp.min(jnp.where(logits == m1, lane_f, float(LANES)), axis=-1, keepdims=True)
    rest = jnp.where(lane_f == i1, -jnp.inf, logits)
    m2 = jnp.max(rest, axis=-1, keepdims=True)
    i2 = jnp.min(jnp.where(rest == m2, lane_f, float(LANES)), axis=-1, keepdims=True)
    e2 = jnp.exp(m2 - m1)
    w1 = 1.0 / (1.0 + e2)
    w2 = e2 / (1.0 + e2)
    idx_out[...] = jnp.where(lane == 0, i1, jnp.where(lane == 1, i2, 0.0)).astype(I32)
    wt_out[...] = jnp.where(lane == 0, w1, jnp.where(lane == 1, w2, 0.0))


def _gla_out(o_p, o_s, r, h, n_p, t_len, mod, out_norm, wo, g2, router_w):
    n, d = h.shape
    tm = _row_tile(n_p, t_len)
    na = n_p // tm
    nt_s = t_len // tm
    mod_idx = lambda i: (jnp.where(i < na, 0, 1 + (i - na) // nt_s), 0, 0)
    row = lambda i: (i, 0)
    return pl.pallas_call(
        functools.partial(_gla_out_kernel, na),
        grid=(n // tm,),
        in_specs=[
            pl.BlockSpec((tm, d), lambda i: (jnp.minimum(i, na - 1), 0)),
            pl.BlockSpec((tm, d), lambda i: (jnp.maximum(i - na, 0), 0)),
            pl.BlockSpec((tm, d), row),
            pl.BlockSpec((tm, d), row),
            pl.BlockSpec((None, 8, d), mod_idx),
            _const_spec((1, GLA_DV)),
            _const_spec((d, d)),
            _const_spec((1, d)),
            _const_spec((2, d, LANES)),
        ],
        out_specs=[pl.BlockSpec((tm, d), row), pl.BlockSpec((tm, d // 2), row),
                   pl.BlockSpec((tm, LANES), row), pl.BlockSpec((tm, LANES), row)],
        out_shape=[jax.ShapeDtypeStruct((n, d), F32), jax.ShapeDtypeStruct((n, d // 2), jnp.uint32),
                   jax.ShapeDtypeStruct((n, LANES), I32), jax.ShapeDtypeStruct((n, LANES), F32)],
        compiler_params=_params(("arbitrary",)),
        name="gla_out_router",
    )(o_p, o_s, r, h, mod, out_norm, wo, g2, router_w)


SC_CORES = 2
SC_SUBCORES = 16
SC_CHUNK = 32


def _sc_gather_rows(table, idx):
    nw = SC_CORES * SC_SUBCORES
    b = idx.shape[0]
    d = table.shape[1]
    assert b % (nw * SC_CHUNK) == 0
    per_w = b // nw
    n_chunks = per_w // SC_CHUNK
    mesh = plsc.VectorSubcoreMesh(core_axis_name="c", subcore_axis_name="s",
                                  num_cores=SC_CORES, num_subcores=SC_SUBCORES)

    def body(table_hbm, idx_hbm, out_hbm, idx_v, rows_v, sem):
        wid = lax.axis_index("s") * SC_CORES + lax.axis_index("c")
        base = wid * per_w
        pltpu.sync_copy(idx_hbm.at[wid], idx_v)

        @pl.loop(0, n_chunks)
        def _(j):
            pltpu.async_copy(table_hbm.at[idx_v.at[j]], rows_v, sem).wait()
            pltpu.sync_copy(rows_v, out_hbm.at[pl.ds(base + j * SC_CHUNK, SC_CHUNK)])

    return pl.kernel(
        body,
        out_type=jax.ShapeDtypeStruct((b, d), table.dtype),
        mesh=mesh,
        scratch_types=[pltpu.VMEM((n_chunks, SC_CHUNK), I32),
                       pltpu.VMEM((SC_CHUNK, d), table.dtype),
                       pltpu.SemaphoreType.DMA],
        name="sc_gather_rows",
    )(table, idx.reshape(nw, n_chunks, SC_CHUNK))


def _sc_scatter_rows(rows, idx2, p):
    nw = SC_CORES * SC_SUBCORES
    n, d = rows.shape
    assert n % (nw * SC_CHUNK) == 0
    per_w = n // nw
    n_chunks = per_w // SC_CHUNK
    mesh = plsc.VectorSubcoreMesh(core_axis_name="c", subcore_axis_name="s",
                                  num_cores=SC_CORES, num_subcores=SC_SUBCORES)

    def body(rows_hbm, idx0_hbm, idx1_hbm, out_hbm, idx0_v, idx1_v, rows_v):
        wid = lax.axis_index("s") * SC_CORES + lax.axis_index("c")
        base = wid * per_w
        pltpu.sync_copy(idx0_hbm.at[wid], idx0_v)
        pltpu.sync_copy(idx1_hbm.at[wid], idx1_v)

        @pl.loop(0, n_chunks)
        def _(j):
            pltpu.sync_copy(rows_hbm.at[pl.ds(base + j * SC_CHUNK, SC_CHUNK)], rows_v)
            pltpu.sync_copy(rows_v, out_hbm.at[idx0_v.at[j]])
            pltpu.sync_copy(rows_v, out_hbm.at[idx1_v.at[j]])

    return pl.kernel(
        body,
        out_type=jax.ShapeDtypeStruct((p, d), rows.dtype),
        mesh=mesh,
        scratch_types=[pltpu.VMEM((n_chunks, SC_CHUNK), I32),
                       pltpu.VMEM((n_chunks, SC_CHUNK), I32),
                       pltpu.VMEM((SC_CHUNK, d), rows.dtype)],
        name="sc_scatter_rows",
    )(rows, idx2[0].reshape(nw, n_chunks, SC_CHUNK), idx2[1].reshape(nw, n_chunks, SC_CHUNK))


EXPERT_SUB = 512


def _pack_halves(x):
    k = x.shape[1] // 2
    lo = lax.bitcast_convert_type(x[:, :k].astype(BF16).astype(F32), jnp.uint32)
    hi = lax.bitcast_convert_type(x[:, k:].astype(BF16).astype(F32), jnp.uint32)
    return (lo >> 16) | (hi & jnp.uint32(0xFFFF0000))


def _unpack_halves(w):
    lo = lax.bitcast_convert_type(w << 16, F32).astype(BF16)
    hi = lax.bitcast_convert_type(w & jnp.uint32(0xFFFF0000), F32).astype(BF16)
    return lo, hi


def _expert_kernel(te_ref, ts_ref, nu_ref, nv_ref, x_ref, wg_ref, wu_ref, wd_ref, y_ref, acc, xb):
    i = pl.program_id(0)
    f = pl.program_id(1)
    nf = pl.num_programs(1)
    tm = x_ref.shape[0]
    half = x_ref.shape[1]
    sub = EXPERT_SUB
    nvalid = nv_ref[i]

    @pl.when(f == 0)
    def _():
        acc[...] = jnp.zeros_like(acc)

    @pl.when((f == 0) & (nvalid > 0))
    def _():
        rid = lax.broadcasted_iota(I32, (tm, 1), 0)
        lo, hi = _unpack_halves(x_ref[...])
        zero = jnp.zeros_like(lo)
        xb[:, 0:half] = jnp.where(rid < nvalid, lo, zero)
        xb[:, half:2 * half] = jnp.where(rid < nvalid, hi, zero)

    def compute(rows):
        x = xb[rows, :]
        hg = jnp.dot(x, wg_ref[...].astype(BF16), preferred_element_type=F32)
        hu = jnp.dot(x, wu_ref[...].astype(BF16), preferred_element_type=F32)
        act = (_silu(hg) * hu).astype(BF16)
        acc[rows, :] += jnp.dot(act, wd_ref[...].astype(BF16), preferred_element_type=F32)

    @pl.when(nvalid > sub)
    def _():
        compute(slice(0, tm))

    @pl.when((nvalid > 0) & (nvalid <= sub))
    def _():
        compute(slice(0, sub))

    @pl.when(f == nf - 1)
    def _():
        y_ref[...] = acc[...]


def _experts(x, tile_expert, tile_src, n_used, n_valid, wg, wu, wd, tm, tf):
    p, half = x.shape
    d = 2 * half
    ne, _, dff = wg.shape
    nf = dff // tf
    assert dff % tf == 0 and tm % EXPERT_SUB == 0

    def fidx(i, f, nu):
        return jnp.where(i < nu[0], f, nf - 1)

    grid_spec = pltpu.PrefetchScalarGridSpec(
        num_scalar_prefetch=4,
        grid=(p // tm, nf),
        in_specs=[
            pl.BlockSpec((tm, half), lambda i, f, te, ts, nu, nv: (ts[i], 0)),
            pl.BlockSpec((None, d, tf), lambda i, f, te, ts, nu, nv: (te[i], 0, fidx(i, f, nu))),
            pl.BlockSpec((None, d, tf), lambda i, f, te, ts, nu, nv: (te[i], 0, fidx(i, f, nu))),
            pl.BlockSpec((None, tf, d), lambda i, f, te, ts, nu, nv: (te[i], fidx(i, f, nu), 0)),
        ],
        out_specs=pl.BlockSpec((tm, d), lambda i, f, te, ts, nu, nv: (i, 0)),
        scratch_shapes=[pltpu.VMEM((tm, d), F32), pltpu.VMEM((tm, d), BF16)],
    )
    return pl.pallas_call(
        _expert_kernel,
        grid_spec=grid_spec,
        out_shape=jax.ShapeDtypeStruct((p, d), F32),
        compiler_params=_params(("arbitrary", "arbitrary"), 60 * 1024 * 1024),
        name="moe_experts",
    )(tile_expert, tile_src, n_used, n_valid, x, wg, wu, wd)


def _combine_kernel(wt_ref, h_ref, mod_ref, y0_ref, y1_ref, o_ref):
    wt = wt_ref[...]
    up = wt[:, 0:1] * y0_ref[...] + wt[:, 1:2] * y1_ref[...]
    o_ref[...] = h_ref[...] + mod_ref[5:6, :] * up


def _combine(yg, wt, h, mod, row_off, n_rows, n_mod, mod_off, tm):
    d = h.shape[1]
    nt = n_rows // tm
    toff = row_off // tm
    per_mod = n_rows // n_mod // tm
    return pl.pallas_call(
        _combine_kernel,
        grid=(nt,),
        in_specs=[
            pl.BlockSpec((tm, LANES), lambda i: (toff + i, 0)),
            pl.BlockSpec((tm, d), lambda i: (toff + i, 0)),
            pl.BlockSpec((None, 8, d), lambda i: (mod_off + i // per_mod, 0, 0)),
            pl.BlockSpec((None, tm, d), lambda i: (0, toff + i, 0)),
            pl.BlockSpec((None, tm, d), lambda i: (1, toff + i, 0)),
        ],
        out_specs=pl.BlockSpec((tm, d), lambda i: (i, 0)),
        out_shape=jax.ShapeDtypeStruct((n_rows, d), F32),
        compiler_params=_params(("arbitrary",)),
        name="moe_combine",
    )(wt, h, mod, yg, yg)


def _route(idx, n, tm):
    e = idx[:, :2].reshape(-1)
    onehot = (e[:, None] == jnp.arange(N_EXPERTS, dtype=I32)[None, :]).astype(I32)
    csum = jnp.cumsum(onehot, axis=0)
    cnt = csum[-1]
    rank = jnp.take_along_axis(csum, e[:, None], axis=1)[:, 0] - 1
    padded = ((cnt + tm - 1) // tm) * tm
    gend = jnp.cumsum(padded)
    goff = gend - padded
    dest = goff[e] + rank
    p = 2 * n + N_EXPERTS * tm
    n_used = gend[-1] // tm
    tiles = jnp.arange(p // tm, dtype=I32)
    tile_src = jnp.minimum(tiles, n_used - 1)
    tile_expert = jnp.minimum(jnp.sum((gend[None, :] <= (tile_src * tm)[:, None]).astype(I32), axis=1),
                              N_EXPERTS - 1)
    used = goff[tile_expert] + cnt[tile_expert]
    n_valid = jnp.where(tiles < n_used, jnp.clip(used - tiles * tm, 0, tm), 0).astype(I32)
    return dest, p, tile_expert, tile_src, n_used.reshape(1).astype(I32), n_valid


def kernel(x_prompt, x_sample, cache_k, cache_v, state_fwd, state_bwd, c, c_ctx, ada_w, ada_b, norm1_g, norm2_g, attn_w_qkv, attn_q_norm, attn_k_norm, attn_sink, attn_w_o, gla_w_in, gla_gate_w1, gla_gate_w2, gla_gate_b, gla_out_norm, gla_w_o, ffn_w_gate, ffn_w_up, ffn_w_down, moe_router, moe_w_gate, moe_w_up, moe_w_down):
    bp, seq, d = x_prompt.shape
    db, t_len, _ = x_sample.shape
    n_p, n_s = bp * seq, db * t_len
    n = n_p + n_s
    xp = x_prompt.reshape(n_p, d)
    xs = x_sample.reshape(n_s, d)

    cond = jnp.concatenate([c_ctx[None, :], c], axis=0)
    assert cond.shape[0] <= 8
    cond_t = jnp.pad(cond, ((0, 8 - cond.shape[0]), (0, 0))).T
    mods = _modulation(cond_t, cond.shape[0], ada_w, ada_b)

    nk = N_KV_HEADS * HEAD_DIM
    qn = jnp.tile(attn_q_norm[0], N_HEADS)[None, :]
    kn = jnp.tile(attn_k_norm[0], N_KV_HEADS)[None, :]
    q, kv = _qkv(xp, xs, t_len, mods[0], norm1_g[0][None, :], attn_w_qkv[0].astype(BF16), qn, kn)
    wo0 = attn_w_o[0].astype(BF16)
    sink = attn_sink[0]
    hp = _ctx_attention(sink, q, kv, xp, seq, mods[0], wo0)
    ck = cache_k[:, 0].reshape(db, cache_k.shape[2], nk)
    cv = cache_v[:, 0].reshape(db, cache_v.shape[2], nk)
    hs = _lat_attention(sink, q, kv, ck, cv, xs, n_p, t_len, mods[0], wo0)
    new_k = kv[:n_p, :nk].reshape(bp, 1, seq, N_KV_HEADS, HEAD_DIM)
    new_v = kv[:n_p, nk:].reshape(bp, 1, seq, N_KV_HEADS, HEAD_DIM)
    h = _ffn(hp, hs, t_len, mods[0], norm2_g[0][None, :], ffn_w_gate[0].astype(BF16),
             ffn_w_up[0].astype(BF16), ffn_w_down[0].astype(BF16))

    hk = GLA_HEADS * GLA_DK
    rank = GLA_GATE_RANK
    w1 = jnp.concatenate([gla_gate_w1[0, 0], gla_gate_w1[0, 1]], axis=1)
    w1 = jnp.pad(w1, ((0, 0), (0, LANES - 2 * rank))).astype(BF16)
    w2 = jnp.zeros((LANES, 2 * hk), F32)
    w2 = w2.at[0:rank, 0:hk].set(gla_gate_w2[0, 0]).at[rank:2 * rank, hk:].set(gla_gate_w2[0, 1]).astype(BF16)
    gate_b = gla_gate_b[0].reshape(1, 2 * hk)
    qk, v, r, bc = _gla_in(h, n_p, t_len, mods[1], norm1_g[1][None, :], gla_w_in[0].astype(BF16),
                           w1, w2, gate_b)
    zeros = jnp.zeros((bp, GLA_HEADS, GLA_DK, GLA_DV), F32)
    o_p, new_sf, new_sb = _gla_scan(qk, v, bc, zeros, zeros, 0, bp, seq)
    o_s, _, _ = _gla_scan(qk, v, bc, state_fwd[:, 0], state_bwd[:, 0], n_p // t_len, db, t_len)
    rw = jnp.pad(moe_router[0], ((0, 0), (0, LANES - N_EXPERTS)))
    rw_hi = rw.astype(BF16)
    rw_lo = (rw - rw_hi.astype(F32)).astype(BF16)
    h, f, idx, wt = _gla_out(o_p.reshape(n_p, d), o_s.reshape(n_s, d), r, h, n_p, t_len, mods[1],
                             gla_out_norm[0][None, :], gla_w_o[0].astype(BF16), norm2_g[1][None, :],
                             jnp.stack([rw_hi, rw_lo]))

    tm_e = 1024
    dest, p_rows, tile_expert, tile_src, n_used, n_valid = _route(idx, n, tm_e)
    dest_k = dest.reshape(n, 2).T
    xg = _sc_scatter_rows(f, dest_k, p_rows)
    y = _experts(xg, tile_expert, tile_src, n_used, n_valid, moe_w_gate[0], moe_w_up[0],
                 moe_w_down[0], tm_e, 896)
    yg = _sc_gather_rows(y, dest_k.reshape(2 * n)).reshape(2, n, d)
    tm_c = _row_tile(n_p, t_len)
    y_p = _combine(yg, wt, h, mods[1], 0, n_p, 1, 0, tm_c)
    y_s = _combine(yg, wt, h, mods[1], n_p, n_s, db, 1, tm_c)

    return (y_p.reshape(bp, seq, d), y_s.reshape(db, t_len, d), new_k, new_v,
            new_sf[:, None], new_sb[:, None])
```

```python
import functools
import math

import jax
import jax.numpy as jnp
from jax import lax
from jax.experimental import pallas as pl
from jax.experimental.pallas import tpu as pltpu
from jax.experimental.pallas import tpu_sc as plsc

F32 = jnp.float32
BF16 = jnp.bfloat16
I32 = jnp.int32

D_MODEL = 1024
N_HEADS = 16
N_KV_HEADS = 4
HEAD_DIM = 64
GRID_W = 64
WINDOW = 128
ATTN_BLOCK = 128
ROPE_THETA = 10000.0
GLA_HEADS = 4
GLA_DK = 128
GLA_DV = 256
GLA_GATE_RANK = 16
GLA_GATE_TAU = 16.0
GLA_CHUNK = 64
N_EXPERTS = 8
NORM_EPS = 1e-6
NEG_INF = -1e30

LANES = 128
VMEM_LIMIT = 56 * 1024 * 1024


def _params(sem, vmem=VMEM_LIMIT):
    return pltpu.CompilerParams(dimension_semantics=sem, vmem_limit_bytes=vmem)


def _row_tile(*counts, cap=512):
    t = cap
    while any(c % t for c in counts):
        t //= 2
    assert t >= 8
    return t


def _rms(x):
    return x * lax.rsqrt(jnp.mean(x * x, axis=-1, keepdims=True) + NORM_EPS)


def _adaln(x, g, shift, scale):
    return _rms(x) * g * (1.0 + scale) + shift


def _silu(x):
    return x * jax.nn.sigmoid(x)


def _const_spec(shape):
    nd = len(shape)
    return pl.BlockSpec(shape, lambda *_: (0,) * nd)


def _mod_kernel(n_cond, ct_ref, w_ref, b_ref, o_ref):
    ct = ct_ref[...]
    s = _silu(ct)
    w = w_ref[...]
    rows = [jnp.sum(w * s[:, r:r + 1], axis=0, keepdims=True) for r in range(n_cond)]
    rows += [jnp.zeros_like(rows[0])] * (8 - n_cond)
    o_ref[...] = jnp.concatenate(rows, axis=0) + b_ref[...]


def _modulation(cond_t, n_cond, ada_w, ada_b):
    depth, d, n6 = ada_w.shape
    tn = 1024
    out = pl.pallas_call(
        functools.partial(_mod_kernel, n_cond),
        grid=(depth, n6 // tn),
        in_specs=[
            pl.BlockSpec((d, 8), lambda l, j: (0, 0)),
            pl.BlockSpec((None, d, tn), lambda l, j: (l, 0, j)),
            pl.BlockSpec((None, 1, tn), lambda l, j: (l, 0, j)),
        ],
        out_specs=pl.BlockSpec((None, 8, tn), lambda l, j: (l, 0, j)),
        out_shape=jax.ShapeDtypeStruct((depth, 8, n6), F32),
        compiler_params=_params(("arbitrary", "arbitrary")),
        name="modulation",
    )(cond_t, ada_w, ada_b.reshape(depth, 1, n6))
    m = out[:, :n_cond].reshape(depth, n_cond, 6, d)
    return jnp.pad(m, ((0, 0), (0, 0), (0, 2), (0, 0)))


def _qkv_kernel(na, nt_s, xa_ref, xb_ref, mod_ref, g_ref, w_ref, qn_ref, kn_ref, bd_ref,
                cos_ref, sin_ref, q_out, kv_out):
    i = pl.program_id(0)
    is_ctx = i < na
    x = jnp.where(is_ctx, xa_ref[...], xb_ref[...])
    a = _adaln(x, g_ref[...], mod_ref[0:1, :], mod_ref[1:2, :])
    y = jnp.dot(a.astype(BF16), w_ref[...], preferred_element_type=F32)
    tm = x.shape[0]
    cos = jnp.where(is_ctx, 1.0, cos_ref[...])
    sin = jnp.where(is_ctx, 0.0, sin_ref[...])
    lane = lax.broadcasted_iota(I32, (tm, LANES), 1)
    first16 = (lane % 32) < 16

    def norm_rope(z, wt):
        ss = jnp.dot((z * z).astype(BF16), bd_ref[...], preferred_element_type=F32)
        zn = z * lax.rsqrt(ss * (1.0 / HEAD_DIM) + NORM_EPS) * wt
        outs = []
        for c in range(2):
            t = zn[:, c * LANES:(c + 1) * LANES]
            partner = jnp.where(first16, pltpu.roll(t, LANES - 16, 1), pltpu.roll(t, 16, 1))
            outs.append(t * cos + partner * sin)
        return jnp.concatenate(outs, axis=1)

    nq = N_HEADS * HEAD_DIM
    nk = N_KV_HEADS * HEAD_DIM
    scale = HEAD_DIM ** -0.5
    for s in range(nq // 256):
        sl = slice(s * 256, (s + 1) * 256)
        q_out[:, sl] = (norm_rope(y[:, sl], qn_ref[:, sl]) * scale).astype(BF16)
    kv_out[:, 0:nk] = norm_rope(y[:, nq:nq + nk], kn_ref[...])
    kv_out[:, nk:2 * nk] = y[:, nq + nk:nq + 2 * nk]


def _rope_tables(t_len):
    pos = jnp.arange(t_len)
    row = (pos // GRID_W).astype(F32)[:, None]
    col = (pos % GRID_W).astype(F32)[:, None]
    half = HEAD_DIM // 2
    inv = ROPE_THETA ** (-jnp.arange(0, half, 2, dtype=F32) / half)[None, :]
    ar, ac = row * inv, col * inv
    cos = jnp.concatenate([jnp.cos(ar), jnp.cos(ar), jnp.cos(ac), jnp.cos(ac)], axis=1)
    sin = jnp.concatenate([-jnp.sin(ar), jnp.sin(ar), -jnp.sin(ac), jnp.sin(ac)], axis=1)
    return jnp.tile(cos, (1, 2)), jnp.tile(sin, (1, 2))


def _qkv(xp, xs, t_len, mod, g, w, qn, kn):
    n_p, n_s = xp.shape[0], xs.shape[0]
    tm = _row_tile(n_p, t_len)
    na, nb = n_p // tm, n_s // tm
    nt_s = t_len // tm
    n = n_p + n_s
    d = D_MODEL
    nq, nk = N_HEADS * HEAD_DIM, N_KV_HEADS * HEAD_DIM
    cos, sin = _rope_tables(t_len)
    eye4 = jnp.kron(jnp.eye(4, dtype=F32), jnp.ones((HEAD_DIM, HEAD_DIM), F32)).astype(BF16)
    mod_idx = lambda i: (jnp.where(i < na, 0, 1 + (i - na) // nt_s), 0, 0)
    pos_idx = lambda i: (jnp.maximum(i - na, 0) % nt_s, 0)
    return pl.pallas_call(
        functools.partial(_qkv_kernel, na, nt_s),
        grid=(na + nb,),
        in_specs=[
            pl.BlockSpec((tm, d), lambda i: (jnp.minimum(i, na - 1), 0)),
            pl.BlockSpec((tm, d), lambda i: (jnp.maximum(i - na, 0), 0)),
            pl.BlockSpec((None, 8, d), mod_idx),
            _const_spec((1, d)),
            _const_spec((d, nq + 2 * nk)),
            _const_spec((1, nq)),
            _const_spec((1, nk)),
            _const_spec((256, 256)),
            pl.BlockSpec((tm, LANES), pos_idx),
            pl.BlockSpec((tm, LANES), pos_idx),
        ],
        out_specs=[pl.BlockSpec((tm, nq), lambda i: (i, 0)),
                   pl.BlockSpec((tm, 2 * nk), lambda i: (i, 0))],
        out_shape=[jax.ShapeDtypeStruct((n, nq), BF16), jax.ShapeDtypeStruct((n, 2 * nk), F32)],
        compiler_params=_params(("arbitrary",)),
        name="qkv",
    )(xp, xs, mod, g, w, qn, kn, eye4, cos, sin)


def _dup_half(k2, half):
    lane = lax.broadcasted_iota(I32, k2.shape, 1)
    lo = lane < HEAD_DIM
    r = pltpu.roll(k2, HEAD_DIM, 1)
    return jnp.where(lo, k2, r) if half == 0 else jnp.where(lo, r, k2)


def _attn_group(q2a, q2b, kk, vv, valid, sinks):
    tq = q2a.shape[0]
    lane = lax.broadcasted_iota(I32, (tq, LANES), 1)
    lo = lane < HEAD_DIM
    zero = jnp.zeros_like(q2a)
    qs = jnp.concatenate([jnp.where(lo, q2a, zero), jnp.where(lo, zero, q2a),
                          jnp.where(lo, q2b, zero), jnp.where(lo, zero, q2b)], axis=0)
    s = lax.dot_general(qs, kk, (((1,), (1,)), ((), ())), preferred_element_type=F32)
    if valid is not None:
        s = jnp.where(valid, s, NEG_INF)
    sink = jnp.concatenate([jnp.full((tq, 1), sk, F32) for sk in sinks], axis=0)
    m = jnp.maximum(jnp.max(s, axis=-1, keepdims=True), sink)
    p = jnp.exp(s - m)
    den = jnp.sum(p, axis=-1, keepdims=True) + jnp.exp(sink - m)
    o = jnp.dot(p.astype(BF16), vv, preferred_element_type=F32) / den
    oa = jnp.where(lo, o[0:tq], o[tq:2 * tq])
    ob = jnp.where(lo, o[2 * tq:3 * tq], o[3 * tq:4 * tq])
    return oa, ob


def _ctx_attn_kernel(sink_ref, q_ref, kv_ref, x_ref, mod_ref, wo_ref, o_ref, osc):
    nk = N_KV_HEADS * HEAD_DIM
    for g in range(N_KV_HEADS):
        c = g // 2
        kk = _dup_half(kv_ref[:, c * LANES:(c + 1) * LANES], g % 2).astype(BF16)
        vv = _dup_half(kv_ref[:, nk + c * LANES:nk + (c + 1) * LANES], g % 2).astype(BF16)
        q2a = q_ref[:, (2 * g) * LANES:(2 * g + 1) * LANES]
        q2b = q_ref[:, (2 * g + 1) * LANES:(2 * g + 2) * LANES]
        sinks = [sink_ref[4 * g + j] for j in range(4)]
        oa, ob = _attn_group(q2a, q2b, kk, vv, None, sinks)
        osc[:, (2 * g) * LANES:(2 * g + 1) * LANES] = oa.astype(BF16)
        osc[:, (2 * g + 1) * LANES:(2 * g + 2) * LANES] = ob.astype(BF16)
    att = jnp.dot(osc[...], wo_ref[...], preferred_element_type=F32)
    o_ref[...] = x_ref[...] + mod_ref[2:3, :] * att


def _ctx_attention(sink, q, kv, xp, seq, mod, wo):
    n_p, d = xp.shape
    nb = n_p // seq
    return pl.pallas_call(
        _ctx_attn_kernel,
        grid=(nb,),
        in_specs=[
            pl.BlockSpec(memory_space=pltpu.SMEM),
            pl.BlockSpec((seq, d), lambda b: (b, 0)),
            pl.BlockSpec((seq, kv.shape[1]), lambda b: (b, 0)),
            pl.BlockSpec((seq, d), lambda b: (b, 0)),
            pl.BlockSpec((None, 8, d), lambda b: (0, 0, 0)),
            _const_spec((d, d)),
        ],
        out_specs=pl.BlockSpec((seq, d), lambda b: (b, 0)),
        out_shape=jax.ShapeDtypeStruct((n_p, d), F32),
        scratch_shapes=[pltpu.VMEM((seq, d), BF16)],
        compiler_params=_params(("arbitrary",)),
        name="ctx_attention",
    )(sink, q, kv, xp, mod, wo)


def _lat_attn_kernel(t_len, sink_ref, q_ref, kv_ref, ck_ref, cv_ref, x_ref, mod_ref, wo_ref, o_ref, osc):
    n = pl.program_id(1)
    tq = ATTN_BLOCK
    nloc = 3 * ATTN_BLOCK
    npast = ck_ref.shape[0]
    nk = N_KV_HEADS * HEAD_DIM
    start = pl.multiple_of(jnp.clip(n * tq - ATTN_BLOCK, 0, t_len - nloc), ATTN_BLOCK)
    kvw = kv_ref[pl.ds(start, nloc), :]
    r = lax.broadcasted_iota(I32, (4 * tq, nloc + npast), 0)
    cidx = lax.broadcasted_iota(I32, (4 * tq, nloc + npast), 1)
    qpos = n * tq + (r & (tq - 1))
    kpos = start + cidx
    valid = (cidx >= nloc) | (jnp.abs(qpos - kpos) <= WINDOW)
    for g in range(N_KV_HEADS):
        c = g // 2
        ksl = slice(c * LANES, (c + 1) * LANES)
        vsl = slice(nk + c * LANES, nk + (c + 1) * LANES)
        kk = jnp.concatenate([_dup_half(kvw[:, ksl], g % 2), _dup_half(ck_ref[:, ksl], g % 2)],
                             axis=0).astype(BF16)
        vv = jnp.concatenate([_dup_half(kvw[:, vsl], g % 2), _dup_half(cv_ref[:, ksl], g % 2)],
                             axis=0).astype(BF16)
        q2a = q_ref[:, (2 * g) * LANES:(2 * g + 1) * LANES]
        q2b = q_ref[:, (2 * g + 1) * LANES:(2 * g + 2) * LANES]
        sinks = [sink_ref[4 * g + j] for j in range(4)]
        oa, ob = _attn_group(q2a, q2b, kk, vv, valid, sinks)
        osc[:, (2 * g) * LANES:(2 * g + 1) * LANES] = oa.astype(BF16)
        osc[:, (2 * g + 1) * LANES:(2 * g + 2) * LANES] = ob.astype(BF16)
    att = jnp.dot(osc[...], wo_ref[...], preferred_element_type=F32)
    o_ref[...] = x_ref[...] + mod_ref[2:3, :] * att


def _lat_attention(sink, q, kv, ck, cv, xs, n_p, t_len, mod, wo):
    n_s, d = xs.shape
    db = n_s // t_len
    tq = ATTN_BLOCK
    nblk = t_len // tq
    assert n_p % t_len == 0 and n_p % tq == 0
    kv_off = n_p // t_len
    q_off = n_p // tq
    npast = ck.shape[1]
    return pl.pallas_call(
        functools.partial(_lat_attn_kernel, t_len),
        grid=(db, nblk),
        in_specs=[
            pl.BlockSpec(memory_space=pltpu.SMEM),
            pl.BlockSpec((tq, d), lambda b, n: (q_off + b * nblk + n, 0)),
            pl.BlockSpec((t_len, kv.shape[1]), lambda b, n: (kv_off + b, 0)),
            pl.BlockSpec((None, npast, ck.shape[2]), lambda b, n: (b, 0, 0)),
            pl.BlockSpec((None, npast, cv.shape[2]), lambda b, n: (b, 0, 0)),
            pl.BlockSpec((tq, d), lambda b, n: (b * nblk + n, 0)),
            pl.BlockSpec((None, 8, d), lambda b, n: (1 + b, 0, 0)),
            _const_spec((d, d)),
        ],
        out_specs=pl.BlockSpec((tq, d), lambda b, n: (b * nblk + n, 0)),
        out_shape=jax.ShapeDtypeStruct((n_s, d), F32),
        scratch_shapes=[pltpu.VMEM((tq, d), BF16)],
        compiler_params=_params(("arbitrary", "arbitrary")),
        name="lat_attention",
    )(sink, q, kv, ck, cv, xs, mod, wo)


def _ffn_kernel(na, fc, xa_ref, xb_ref, mod_ref, g_ref, wg_ref, wu_ref, wd_ref, o_ref, acc):
    i = pl.program_id(0)
    x = jnp.where(i < na, xa_ref[...], xb_ref[...])
    a = _adaln(x, g_ref[...], mod_ref[3:4, :], mod_ref[4:5, :]).astype(BF16)
    nf = wg_ref.shape[1] // fc
    for f in range(nf):
        sl = slice(f * fc, (f + 1) * fc)
        hg = jnp.dot(a, wg_ref[:, sl], preferred_element_type=F32)
        hu = jnp.dot(a, wu_ref[:, sl], preferred_element_type=F32)
        act = (_silu(hg) * hu).astype(BF16)
        part = jnp.dot(act, wd_ref[sl, :], preferred_element_type=F32)
        if f == 0:
            acc[...] = part
        else:
            acc[...] += part
    o_ref[...] = x + mod_ref[5:6, :] * acc[...]


def _ffn(hp, hs, t_len, mod, g, wg, wu, wd):
    n_p, n_s = hp.shape[0], hs.shape[0]
    d, dff = wg.shape
    tm = _row_tile(n_p, t_len)
    na, nb = n_p // tm, n_s // tm
    nt_s = t_len // tm
    fc = 256
    assert dff % fc == 0
    mod_idx = lambda i: (jnp.where(i < na, 0, 1 + (i - na) // nt_s), 0, 0)
    return pl.pallas_call(
        functools.partial(_ffn_kernel, na, fc),
        grid=(na + nb,),
        in_specs=[
            pl.BlockSpec((tm, d), lambda i: (jnp.minimum(i, na - 1), 0)),
            pl.BlockSpec((tm, d), lambda i: (jnp.maximum(i - na, 0), 0)),
            pl.BlockSpec((None, 8, d), mod_idx),
            _const_spec((1, d)),
            _const_spec((d, dff)),
            _const_spec((d, dff)),
            _const_spec((dff, d)),
        ],
        out_specs=pl.BlockSpec((tm, d), lambda i: (i, 0)),
        out_shape=jax.ShapeDtypeStruct((n_p + n_s, d), F32),
        scratch_shapes=[pltpu.VMEM((tm, d), F32)],
        compiler_params=_params(("arbitrary",)),
        name="ffn",
    )(hp, hs, mod, g, wg, wu, wd)


def _split3(x):
    hi = x.astype(BF16)
    r1 = x - hi.astype(F32)
    mid = r1.astype(BF16)
    lo = (r1 - mid.astype(F32)).astype(BF16)
    return hi, mid, lo


def _gla_in_kernel(nt_s, x_ref, mod_ref, g_ref, w_ref, w1_ref, w2_ref, gb_ref, tri_ref,
                   qk_out, v_out, r_out, b_out):
    x = x_ref[...]
    tm = x.shape[0]
    a = _adaln(x, g_ref[...], mod_ref[0:1, :], mod_ref[1:2, :]).astype(BF16)
    hk = GLA_HEADS * GLA_DK
    hv = GLA_HEADS * GLA_DV
    q = jnp.dot(a, w_ref[:, 0:hk], preferred_element_type=F32)
    qk_out[:, 0:hk] = q * (GLA_DK ** -0.5)
    qk_out[:, hk:2 * hk] = jnp.dot(a, w_ref[:, hk:2 * hk], preferred_element_type=F32)
    v_out[...] = jnp.dot(a, w_ref[:, 2 * hk:2 * hk + hv], preferred_element_type=F32)
    r_out[...] = jnp.dot(a, w_ref[:, 2 * hk + hv:2 * hk + 2 * hv], preferred_element_type=F32).astype(BF16)
    z1 = jnp.dot(a, w1_ref[...], preferred_element_type=F32)
    z1h = z1.astype(BF16)
    z1l = (z1 - z1h.astype(F32)).astype(BF16)
    z = (jnp.dot(z1h, w2_ref[...], preferred_element_type=F32)
         + jnp.dot(z1l, w2_ref[...], preferred_element_type=F32) + gb_ref[...])
    gate = (jnp.minimum(z, 0.0) - jnp.log(1.0 + jnp.exp(-jnp.abs(z)))) * (1.0 / GLA_GATE_TAU)
    c = GLA_CHUNK
    tri = tri_ref[...]
    for j in range(tm // c):
        rows = slice(j * c, (j + 1) * c)
        for dr in range(2):
            cols = slice(dr * hk, (dr + 1) * hk)
            parts = _split3(gate[rows, cols])
            acc = jnp.dot(tri[dr], parts[0], preferred_element_type=F32)
            acc += jnp.dot(tri[dr], parts[1], preferred_element_type=F32)
            acc += jnp.dot(tri[dr], parts[2], preferred_element_type=F32)
            b_out[rows, cols] = acc


def _gla_in(h, n_p, t_len, mod, g, w_in, w1, w2, gate_b):
    n, d = h.shape
    tm = _row_tile(n_p, t_len)
    na = n_p // tm
    nt_s = t_len // tm
    hk, hv = GLA_HEADS * GLA_DK, GLA_HEADS * GLA_DV
    c = GLA_CHUNK
    lower = jnp.tril(jnp.ones((c, c), F32))
    tri = jnp.stack([lower, lower.T]).astype(BF16)
    mod_idx = lambda i: (jnp.where(i < na, 0, 1 + (i - na) // nt_s), 0, 0)
    row = lambda i: (i, 0)
    return pl.pallas_call(
        functools.partial(_gla_in_kernel, nt_s),
        grid=(n // tm,),
        in_specs=[
            pl.BlockSpec((tm, d), row),
            pl.BlockSpec((None, 8, d), mod_idx),
            _const_spec((1, d)),
            _const_spec(w_in.shape),
            _const_spec(w1.shape),
            _const_spec(w2.shape),
            _const_spec((1, 2 * hk)),
            _const_spec((2, c, c)),
        ],
        out_specs=[pl.BlockSpec((tm, 2 * hk), row), pl.BlockSpec((tm, hv), row),
                   pl.BlockSpec((tm, hv), row), pl.BlockSpec((tm, 2 * hk), row)],
        out_shape=[jax.ShapeDtypeStruct((n, 2 * hk), F32), jax.ShapeDtypeStruct((n, hv), F32),
                   jax.ShapeDtypeStruct((n, hv), BF16), jax.ShapeDtypeStruct((n, 2 * hk), F32)],
        compiler_params=_params(("arbitrary",)),
        name="gla_in",
    )(h, mod, g, w_in, w1, w2, gate_b, tri)


def _gla_chunk(q, k, v, b, st_ref, forward):
    c = q.shape[0]
    if forward:
        b_end, b_mid = b[c - 1:c, :], b[c // 2 - 1:c // 2, :]
    else:
        b_end, b_mid = b[0:1, :], b[c // 2:c // 2 + 1, :]
    e_up = jnp.exp(b - b_mid)
    e_dn = jnp.exp(b_mid - b)
    qa = q * e_up
    ka = k * e_dn
    qd = (qa * jnp.exp(b_mid)).astype(BF16)
    kd = (ka * jnp.exp(b_end - b_mid)).astype(BF16)
    att = lax.dot_general(qa.astype(BF16), ka.astype(BF16), (((1,), (1,)), ((), ())),
                          preferred_element_type=F32)
    ri = lax.broadcasted_iota(I32, (c, c), 0)
    ci = lax.broadcasted_iota(I32, (c, c), 1)
    keep = (ci <= ri) if forward else (ci >= ri)
    att = jnp.where(keep, att, 0.0).astype(BF16)
    vb = v.astype(BF16)
    st = st_ref[...]
    o = lax.dot_general(qd, st.astype(BF16), (((1,), (1,)), ((), ())), preferred_element_type=F32)
    o = o + jnp.dot(att, vb, preferred_element_type=F32)
    upd = lax.dot_general(vb, kd, (((0,), (0,)), ((), ())), preferred_element_type=F32)
    st_ref[...] = st * jnp.exp(b_end) + upd
    return o


def _gla_scan_kernel(qkf_ref, vf_ref, bf_ref, qkb_ref, vb_ref, bb_ref, sf0_ref, sb0_ref,
                     of_ref, ob_ref, sf_ref, sb_ref, stf, stb):
    i = pl.program_id(1)
    nblk = pl.num_programs(1)
    c = GLA_CHUNK
    ncb = qkf_ref.shape[0] // c
    hk = GLA_HEADS * GLA_DK
    dk, dv = GLA_DK, GLA_DV

    @pl.when(i == 0)
    def _():
        for h in range(GLA_HEADS):
            stf[h] = sf0_ref[h].T
            stb[h] = sb0_ref[h].T

    def step(ci, carry):
        rf = pl.ds(pl.multiple_of(ci * c, c), c)
        rb = pl.ds(pl.multiple_of((ncb - 1 - ci) * c, c), c)
        for h in range(GLA_HEADS):
            qs = slice(h * dk, (h + 1) * dk)
            ks = slice(hk + h * dk, hk + (h + 1) * dk)
            vs = slice(h * dv, (h + 1) * dv)
            of = _gla_chunk(qkf_ref[rf, qs], qkf_ref[rf, ks], vf_ref[rf, vs], bf_ref[rf, qs],
                            stf.at[h], True)
            of_ref[rf, vs] = of.astype(of_ref.dtype)
            ob = _gla_chunk(qkb_ref[rb, qs], qkb_ref[rb, ks], vb_ref[rb, vs], bb_ref[rb, qs],
                            stb.at[h], False)
            ob_ref[rb, vs] = ob.astype(ob_ref.dtype)
        return carry

    lax.fori_loop(0, ncb, step, 0)

    @pl.when(i == nblk - 1)
    def _():
        for h in range(GLA_HEADS):
            sf_ref[h] = stf[h].T
            sb_ref[h] = stb[h].T


def _gla_scan(qk, v, bc, sf0, sb0, b_off, nb, t_len):
    n = qk.shape[0]
    assert n % t_len == 0
    ns = n // t_len
    tb = _row_tile(t_len)
    nblk = t_len // tb
    qk3 = qk.reshape(ns, t_len, qk.shape[1])
    v3 = v.reshape(ns, t_len, v.shape[1])
    bc3 = bc.reshape(ns, t_len, bc.shape[1])
    h, dk, dv = GLA_HEADS, GLA_DK, GLA_DV
    hk = h * dk
    st_spec = pl.BlockSpec((None, h, dk, dv), lambda b, i: (b, 0, 0, 0))
    fwd = lambda b, i: (b_off + b, i, 0)
    bwd = lambda b, i: (b_off + b, nblk - 1 - i, 0)
    return pl.pallas_call(
        _gla_scan_kernel,
        grid=(nb, nblk),
        in_specs=[
            pl.BlockSpec((None, tb, 2 * hk), fwd),
            pl.BlockSpec((None, tb, h * dv), fwd),
            pl.BlockSpec((None, tb, hk), fwd),
            pl.BlockSpec((None, tb, 2 * hk), bwd),
            pl.BlockSpec((None, tb, h * dv), bwd),
            pl.BlockSpec((None, tb, hk), lambda b, i: (b_off + b, nblk - 1 - i, 1)),
            st_spec, st_spec,
        ],
        out_specs=[pl.BlockSpec((None, tb, h * dv), lambda b, i: (b, i, 0)),
                   pl.BlockSpec((None, tb, h * dv), lambda b, i: (b, nblk - 1 - i, 0)),
                   st_spec, st_spec],
        out_shape=[jax.ShapeDtypeStruct((nb, t_len, h * dv), BF16),
                   jax.ShapeDtypeStruct((nb, t_len, h * dv), BF16),
                   jax.ShapeDtypeStruct((nb, h, dk, dv), F32),
                   jax.ShapeDtypeStruct((nb, h, dk, dv), F32)],
        scratch_shapes=[pltpu.VMEM((h, dv, dk), F32), pltpu.VMEM((h, dv, dk), F32)],
        compiler_params=_params(("arbitrary", "arbitrary")),
        name="gla_scan",
    )(qk3, v3, bc3, qk3, v3, bc3, sf0, sb0)


def _gla_out_kernel(na, ofa_ref, oba_ref, ofb_ref, obb_ref, r_ref, h_ref, mod_ref, on_ref, wo_ref,
                    g2_ref, rw_ref, tri_ref, h_out, f_out, route_out, wt_out, cnt_out, run):
    i = pl.program_id(0)
    o = jnp.where(i < na, ofa_ref[...].astype(F32) + oba_ref[...].astype(F32),
                  ofb_ref[...].astype(F32) + obb_ref[...].astype(F32))
    tm = o.shape[0]
    dv = GLA_DV
    parts = []
    for hh in range(GLA_HEADS):
        oh = o[:, hh * dv:(hh + 1) * dv]
        parts.append(_rms(oh) * on_ref[...])
    on = jnp.concatenate(parts, axis=1)
    r = r_ref[...].astype(F32)
    gated = (on * _silu(r)).astype(BF16)
    h = h_ref[...] + mod_ref[2:3, :] * jnp.dot(gated, wo_ref[...], preferred_element_type=F32)
    h_out[...] = h
    f = _adaln(h, g2_ref[...], mod_ref[3:4, :], mod_ref[4:5, :])
    f_out[...] = _pack_halves(f)
    fh = f.astype(BF16)
    fl = (f - fh.astype(F32)).astype(BF16)
    logits = (jnp.dot(fh, rw_ref[0], preferred_element_type=F32)
              + jnp.dot(fl, rw_ref[0], preferred_element_type=F32)
              + jnp.dot(fh, rw_ref[1], preferred_element_type=F32))
    lane = lax.broadcasted_iota(I32, (tm, LANES), 1)
    lane_f = lane.astype(F32)
    logits = jnp.where(lane < N_EXPERTS, logits, -jnp.inf)
    m1 = jnp.max(logits, axis=-1, keepdims=True)
    i1 = jnp.min(jnp.where(logits == m1, lane_f, float(LANES)), axis=-1, keepdims=True)
    rest = jnp.where(lane_f == i1, -jnp.inf, logits)
    m2 = jnp.max(rest, axis=-1, keepdims=True)
    i2 = jnp.min(jnp.where(rest == m2, lane_f, float(LANES)), axis=-1, keepdims=True)
    e2 = jnp.exp(m2 - m1)
    w1 = 1.0 / (1.0 + e2)
    w2 = e2 / (1.0 + e2)
    wt_out[...] = jnp.where(lane == 0, w1, jnp.where(lane == 1, w2, 0.0))
    sel1 = lane_f == i1
    sel2 = lane_f == i2
    picked = jnp.where(sel1, 1.0, jnp.where(sel2, 1.0, 0.0))

    @pl.when(i == 0)
    def _():
        run[...] = jnp.zeros_like(run)

    before = jnp.dot(tri_ref[...], picked.astype(BF16), preferred_element_type=F32) + run[0:1, :]
    rank1 = jnp.sum(jnp.where(sel1, before, 0.0), axis=-1, keepdims=True)
    rank2 = jnp.sum(jnp.where(sel2, before, 0.0), axis=-1, keepdims=True)
    run[...] = run[...] + jnp.sum(picked, axis=0, keepdims=True)
    cnt_out[...] = run[...]
    routing = jnp.where(lane == 0, i1, jnp.where(lane == 1, i2,
                        jnp.where(lane == 2, rank1, jnp.where(lane == 3, rank2, 0.0))))
    route_out[...] = routing.T[0:8, :]


def _gla_out(of_p, ob_p, of_s, ob_s, r, h, n_p, t_len, mod, out_norm, wo, g2, router_w):
    n, d = h.shape
    tm = _row_tile(n_p, t_len)
    na = n_p // tm
    nt_s = t_len // tm
    mod_idx = lambda i: (jnp.where(i < na, 0, 1 + (i - na) // nt_s), 0, 0)
    row = lambda i: (i, 0)
    ctx_row = lambda i: (jnp.minimum(i, na - 1), 0)
    lat_row = lambda i: (jnp.maximum(i - na, 0), 0)
    tri = jnp.tril(jnp.ones((tm, tm), F32), -1).astype(BF16)
    return pl.pallas_call(
        functools.partial(_gla_out_kernel, na),
        grid=(n // tm,),
        in_specs=[
            pl.BlockSpec((tm, d), ctx_row),
            pl.BlockSpec((tm, d), ctx_row),
            pl.BlockSpec((tm, d), lat_row),
            pl.BlockSpec((tm, d), lat_row),
            pl.BlockSpec((tm, d), row),
            pl.BlockSpec((tm, d), row),
            pl.BlockSpec((None, 8, d), mod_idx),
            _const_spec((1, GLA_DV)),
            _const_spec((d, d)),
            _const_spec((1, d)),
            _const_spec((2, d, LANES)),
            _const_spec((tm, tm)),
        ],
        out_specs=[pl.BlockSpec((tm, d), row), pl.BlockSpec((tm, d // 2), row),
                   pl.BlockSpec((8, tm), lambda i: (0, i)), pl.BlockSpec((tm, LANES), row),
                   _const_spec((8, LANES))],
        out_shape=[jax.ShapeDtypeStruct((n, d), F32), jax.ShapeDtypeStruct((n, d // 2), jnp.uint32),
                   jax.ShapeDtypeStruct((8, n), F32), jax.ShapeDtypeStruct((n, LANES), F32),
                   jax.ShapeDtypeStruct((8, LANES), F32)],
        scratch_shapes=[pltpu.VMEM((8, LANES), F32)],
        compiler_params=_params(("arbitrary",)),
        name="gla_out_router",
    )(of_p, ob_p, of_s, ob_s, r, h, mod, out_norm, wo, g2, router_w, tri)


SC_CORES = 2
SC_SUBCORES = 16
SC_CHUNK = 32


def _sc_gather_rows(table, idx):
    nw = SC_CORES * SC_SUBCORES
    b = idx.shape[0]
    d = table.shape[1]
    assert b % (nw * SC_CHUNK) == 0
    per_w = b // nw
    n_chunks = per_w // SC_CHUNK
    mesh = plsc.VectorSubcoreMesh(core_axis_name="c", subcore_axis_name="s",
                                  num_cores=SC_CORES, num_subcores=SC_SUBCORES)

    def body(table_hbm, idx_hbm, out_hbm, idx_v, rows_v, sem):
        wid = lax.axis_index("s") * SC_CORES + lax.axis_index("c")
        base = wid * per_w
        pltpu.sync_copy(idx_hbm.at[wid], idx_v)

        @pl.loop(0, n_chunks)
        def _(j):
            pltpu.async_copy(table_hbm.at[idx_v.at[j]], rows_v, sem).wait()
            pltpu.sync_copy(rows_v, out_hbm.at[pl.ds(base + j * SC_CHUNK, SC_CHUNK)])

    return pl.kernel(
        body,
        out_type=jax.ShapeDtypeStruct((b, d), table.dtype),
        mesh=mesh,
        scratch_types=[pltpu.VMEM((n_chunks, SC_CHUNK), I32),
                       pltpu.VMEM((SC_CHUNK, d), table.dtype),
                       pltpu.SemaphoreType.DMA],
        name="sc_gather_rows",
    )(table, idx.reshape(nw, n_chunks, SC_CHUNK))


def _sc_scatter_rows(rows, idx2, p):
    nw = SC_CORES * SC_SUBCORES
    n, d = rows.shape
    assert n % (nw * SC_CHUNK) == 0
    per_w = n // nw
    n_chunks = per_w // SC_CHUNK
    mesh = plsc.VectorSubcoreMesh(core_axis_name="c", subcore_axis_name="s",
                                  num_cores=SC_CORES, num_subcores=SC_SUBCORES)

    def body(rows_hbm, idx0_hbm, idx1_hbm, out_hbm, idx0_v, idx1_v, rows_v):
        wid = lax.axis_index("s") * SC_CORES + lax.axis_index("c")
        base = wid * per_w
        pltpu.sync_copy(idx0_hbm.at[wid], idx0_v)
        pltpu.sync_copy(idx1_hbm.at[wid], idx1_v)

        @pl.loop(0, n_chunks)
        def _(j):
            pltpu.sync_copy(rows_hbm.at[pl.ds(base + j * SC_CHUNK, SC_CHUNK)], rows_v)
            pltpu.sync_copy(rows_v, out_hbm.at[idx0_v.at[j]])
            pltpu.sync_copy(rows_v, out_hbm.at[idx1_v.at[j]])

    return pl.kernel(
        body,
        out_type=jax.ShapeDtypeStruct((p, d), rows.dtype),
        mesh=mesh,
        scratch_types=[pltpu.VMEM((n_chunks, SC_CHUNK), I32),
                       pltpu.VMEM((n_chunks, SC_CHUNK), I32),
                       pltpu.VMEM((SC_CHUNK, d), rows.dtype)],
        name="sc_scatter_rows",
    )(rows, idx2[0].reshape(nw, n_chunks, SC_CHUNK), idx2[1].reshape(nw, n_chunks, SC_CHUNK))


EXPERT_SUB = 512


def _pack_halves(x):
    k = x.shape[1] // 2
    lo = lax.bitcast_convert_type(x[:, :k].astype(BF16).astype(F32), jnp.uint32)
    hi = lax.bitcast_convert_type(x[:, k:].astype(BF16).astype(F32), jnp.uint32)
    return (lo >> 16) | (hi & jnp.uint32(0xFFFF0000))


def _unpack_halves(w):
    lo = lax.bitcast_convert_type(w << 16, F32).astype(BF16)
    hi = lax.bitcast_convert_type(w & jnp.uint32(0xFFFF0000), F32).astype(BF16)
    return lo, hi


def _expert_kernel(te_ref, ts_ref, nu_ref, nv_ref, x_ref, wg_ref, wu_ref, wd_ref, y_ref, acc, xb):
    i = pl.program_id(0)
    f = pl.program_id(1)
    nf = pl.num_programs(1)
    tm = x_ref.shape[0]
    half = x_ref.shape[1]
    sub = EXPERT_SUB
    nvalid = nv_ref[i]

    @pl.when(f == 0)
    def _():
        acc[...] = jnp.zeros_like(acc)

    @pl.when((f == 0) & (nvalid > 0))
    def _():
        rid = lax.broadcasted_iota(I32, (tm, 1), 0)
        lo, hi = _unpack_halves(x_ref[...])
        zero = jnp.zeros_like(lo)
        xb[:, 0:half] = jnp.where(rid < nvalid, lo, zero)
        xb[:, half:2 * half] = jnp.where(rid < nvalid, hi, zero)

    def compute(rows):
        x = xb[rows, :]
        hg = jnp.dot(x, wg_ref[...].astype(BF16), preferred_element_type=F32)
        hu = jnp.dot(x, wu_ref[...].astype(BF16), preferred_element_type=F32)
        act = (_silu(hg) * hu).astype(BF16)
        acc[rows, :] += jnp.dot(act, wd_ref[...].astype(BF16), preferred_element_type=F32)

    @pl.when(nvalid > sub)
    def _():
        compute(slice(0, tm))

    @pl.when((nvalid > 0) & (nvalid <= sub))
    def _():
        compute(slice(0, sub))

    @pl.when(f == nf - 1)
    def _():
        y_ref[...] = _pack_halves(acc[...])


def _experts(x, tile_expert, tile_src, n_used, n_valid, wg, wu, wd, tm, tf):
    p, half = x.shape
    d = 2 * half
    ne, _, dff = wg.shape
    nf = dff // tf
    assert dff % tf == 0 and tm % EXPERT_SUB == 0

    def fidx(i, f, nu):
        return jnp.where(i < nu[0], f, nf - 1)

    grid_spec = pltpu.PrefetchScalarGridSpec(
        num_scalar_prefetch=4,
        grid=(p // tm, nf),
        in_specs=[
            pl.BlockSpec((tm, half), lambda i, f, te, ts, nu, nv: (ts[i], 0)),
            pl.BlockSpec((None, d, tf), lambda i, f, te, ts, nu, nv: (te[i], 0, fidx(i, f, nu))),
            pl.BlockSpec((None, d, tf), lambda i, f, te, ts, nu, nv: (te[i], 0, fidx(i, f, nu))),
            pl.BlockSpec((None, tf, d), lambda i, f, te, ts, nu, nv: (te[i], fidx(i, f, nu), 0)),
        ],
        out_specs=pl.BlockSpec((tm, half), lambda i, f, te, ts, nu, nv: (i, 0)),
        scratch_shapes=[pltpu.VMEM((tm, d), F32), pltpu.VMEM((tm, d), BF16)],
    )
    return pl.pallas_call(
        _expert_kernel,
        grid_spec=grid_spec,
        out_shape=jax.ShapeDtypeStruct((p, half), jnp.uint32),
        compiler_params=_params(("arbitrary", "arbitrary"), 60 * 1024 * 1024),
        name="moe_experts",
    )(tile_expert, tile_src, n_used, n_valid, x, wg, wu, wd)


def _combine_kernel(wt_ref, h_ref, mod_ref, y0_ref, y1_ref, o_ref):
    wt = wt_ref[...]
    half = y0_ref.shape[1]
    lo0, hi0 = _unpack_halves(y0_ref[...])
    lo1, hi1 = _unpack_halves(y1_ref[...])
    w0, w1 = wt[:, 0:1], wt[:, 1:2]
    h = h_ref[...]
    gate = mod_ref[5:6, :]
    o_ref[:, 0:half] = h[:, 0:half] + gate[:, 0:half] * (w0 * lo0.astype(F32) + w1 * lo1.astype(F32))
    o_ref[:, half:] = h[:, half:] + gate[:, half:] * (w0 * hi0.astype(F32) + w1 * hi1.astype(F32))


def _combine(yg, wt, h, mod, row_off, n_rows, n_mod, mod_off, tm):
    d = h.shape[1]
    nt = n_rows // tm
    toff = row_off // tm
    per_mod = n_rows // n_mod // tm
    return pl.pallas_call(
        _combine_kernel,
        grid=(nt,),
        in_specs=[
            pl.BlockSpec((tm, LANES), lambda i: (toff + i, 0)),
            pl.BlockSpec((tm, d), lambda i: (toff + i, 0)),
            pl.BlockSpec((None, 8, d), lambda i: (mod_off + i // per_mod, 0, 0)),
            pl.BlockSpec((None, tm, d // 2), lambda i: (0, toff + i, 0)),
            pl.BlockSpec((None, tm, d // 2), lambda i: (1, toff + i, 0)),
        ],
        out_specs=pl.BlockSpec((tm, d), lambda i: (i, 0)),
        out_shape=jax.ShapeDtypeStruct((n_rows, d), F32),
        compiler_params=_params(("arbitrary",)),
        name="moe_combine",
    )(wt, h, mod, yg, yg)


def _route(route, counts, n, tm):
    cnt = counts[0, :N_EXPERTS].astype(I32)
    padded = ((cnt + tm - 1) // tm) * tm
    gend = jnp.cumsum(padded)
    goff = gend - padded
    e = route[0:2].astype(I32)
    rank = route[2:4].astype(I32)
    onehot = e[:, :, None] == jnp.arange(N_EXPERTS, dtype=I32)[None, None, :]
    dest = jnp.sum(jnp.where(onehot, goff[None, None, :], 0), axis=-1) + rank
    p = 2 * n + N_EXPERTS * tm
    n_used = gend[-1] // tm
    tiles = jnp.arange(p // tm, dtype=I32)
    tile_src = jnp.minimum(tiles, n_used - 1)
    tile_expert = jnp.minimum(jnp.sum((gend[None, :] <= (tile_src * tm)[:, None]).astype(I32), axis=1),
                              N_EXPERTS - 1)
    used = goff[tile_expert] + cnt[tile_expert]
    n_valid = jnp.where(tiles < n_used, jnp.clip(used - tiles * tm, 0, tm), 0).astype(I32)
    return dest, p, tile_expert, tile_src, n_used.reshape(1).astype(I32), n_valid


def kernel(x_prompt, x_sample, cache_k, cache_v, state_fwd, state_bwd, c, c_ctx, ada_w, ada_b, norm1_g, norm2_g, attn_w_qkv, attn_q_norm, attn_k_norm, attn_sink, attn_w_o, gla_w_in, gla_gate_w1, gla_gate_w2, gla_gate_b, gla_out_norm, gla_w_o, ffn_w_gate, ffn_w_up, ffn_w_down, moe_router, moe_w_gate, moe_w_up, moe_w_down):
    bp, seq, d = x_prompt.shape
    db, t_len, _ = x_sample.shape
    n_p, n_s = bp * seq, db * t_len
    n = n_p + n_s
    xp = x_prompt.reshape(n_p, d)
    xs = x_sample.reshape(n_s, d)

    cond = jnp.concatenate([c_ctx[None, :], c], axis=0)
    assert cond.shape[0] <= 8
    cond_t = jnp.pad(cond, ((0, 8 - cond.shape[0]), (0, 0))).T
    mods = _modulation(cond_t, cond.shape[0], ada_w, ada_b)

    nk = N_KV_HEADS * HEAD_DIM
    qn = jnp.tile(attn_q_norm[0], N_HEADS)[None, :]
    kn = jnp.tile(attn_k_norm[0], N_KV_HEADS)[None, :]
    q, kv = _qkv(xp, xs, t_len, mods[0], norm1_g[0][None, :], attn_w_qkv[0].astype(BF16), qn, kn)
    wo0 = attn_w_o[0].astype(BF16)
    sink = attn_sink[0]
    hp = _ctx_attention(sink, q, kv, xp, seq, mods[0], wo0)
    ck = cache_k[:, 0].reshape(db, cache_k.shape[2], nk)
    cv = cache_v[:, 0].reshape(db, cache_v.shape[2], nk)
    hs = _lat_attention(sink, q, kv, ck, cv, xs, n_p, t_len, mods[0], wo0)
    new_k = kv[:n_p, :nk].reshape(bp, 1, seq, N_KV_HEADS, HEAD_DIM)
    new_v = kv[:n_p, nk:].reshape(bp, 1, seq, N_KV_HEADS, HEAD_DIM)
    h = _ffn(hp, hs, t_len, mods[0], norm2_g[0][None, :], ffn_w_gate[0].astype(BF16),
             ffn_w_up[0].astype(BF16), ffn_w_down[0].astype(BF16))

    hk = GLA_HEADS * GLA_DK
    rank = GLA_GATE_RANK
    w1 = jnp.concatenate([gla_gate_w1[0, 0], gla_gate_w1[0, 1]], axis=1)
    w1 = jnp.pad(w1, ((0, 0), (0, LANES - 2 * rank))).astype(BF16)
    w2 = jnp.zeros((LANES, 2 * hk), F32)
    w2 = w2.at[0:rank, 0:hk].set(gla_gate_w2[0, 0]).at[rank:2 * rank, hk:].set(gla_gate_w2[0, 1]).astype(BF16)
    gate_b = gla_gate_b[0].reshape(1, 2 * hk)
    qk, v, r, bc = _gla_in(h, n_p, t_len, mods[1], norm1_g[1][None, :], gla_w_in[0].astype(BF16),
                           w1, w2, gate_b)
    zeros = jnp.zeros((bp, GLA_HEADS, GLA_DK, GLA_DV), F32)
    of_p, ob_p, new_sf, new_sb = _gla_scan(qk, v, bc, zeros, zeros, 0, bp, seq)
    of_s, ob_s, _, _ = _gla_scan(qk, v, bc, state_fwd[:, 0], state_bwd[:, 0], n_p // t_len, db, t_len)
    rw = jnp.pad(moe_router[0], ((0, 0), (0, LANES - N_EXPERTS)))
    rw_hi = rw.astype(BF16)
    rw_lo = (rw - rw_hi.astype(F32)).astype(BF16)
    h, f, route, wt, counts = _gla_out(
        of_p.reshape(n_p, d), ob_p.reshape(n_p, d), of_s.reshape(n_s, d), ob_s.reshape(n_s, d), r, h,
        n_p, t_len, mods[1], gla_out_norm[0][None, :], gla_w_o[0].astype(BF16), norm2_g[1][None, :],
        jnp.stack([rw_hi, rw_lo]))

    tm_e = 1024
    dest_k, p_rows, tile_expert, tile_src, n_used, n_valid = _route(route, counts, n, tm_e)
    xg = _sc_scatter_rows(f, dest_k, p_rows)
    y = _experts(xg, tile_expert, tile_src, n_used, n_valid, moe_w_gate[0], moe_w_up[0],
                 moe_w_down[0], tm_e, 896)
    yg = _sc_gather_rows(y, dest_k.reshape(2 * n)).reshape(2, n, d // 2)
    tm_c = _row_tile(n_p, t_len)
    y_p = _combine(yg, wt, h, mods[1], 0, n_p, 1, 0, tm_c)
    y_s = _combine(yg, wt, h, mods[1], n_p, n_s, db, 1, tm_c)

    return (y_p.reshape(bp, seq, d), y_s.reshape(db, t_len, d), new_k, new_v,
            new_sf[:, None], new_sb[:, None])
```

```python
import functools
import math

import jax
import jax.numpy as jnp
from jax import lax
from jax.experimental import pallas as pl
from jax.experimental.pallas import tpu as pltpu
from jax.experimental.pallas import tpu_sc as plsc

F32 = jnp.float32
BF16 = jnp.bfloat16
I32 = jnp.int32

D_MODEL = 1024
N_HEADS = 16
N_KV_HEADS = 4
HEAD_DIM = 64
GRID_W = 64
WINDOW = 128
ATTN_BLOCK = 128
ROPE_THETA = 10000.0
GLA_HEADS = 4
GLA_DK = 128
GLA_DV = 256
GLA_GATE_RANK = 16
GLA_GATE_TAU = 16.0
GLA_CHUNK = 64
N_EXPERTS = 8
NORM_EPS = 1e-6
NEG_INF = -1e30

LANES = 128
VMEM_LIMIT = 56 * 1024 * 1024


def _params(sem, vmem=VMEM_LIMIT):
    return pltpu.CompilerParams(dimension_semantics=sem, vmem_limit_bytes=vmem)


def _row_tile(*counts, cap=512):
    t = cap
    while any(c % t for c in counts):
        t //= 2
    assert t >= 8
    return t


def _rms(x):
    return x * lax.rsqrt(jnp.mean(x * x, axis=-1, keepdims=True) + NORM_EPS)


def _adaln(x, g, shift, scale):
    return _rms(x) * g * (1.0 + scale) + shift


def _silu(x):
    return x * jax.nn.sigmoid(x)


def _const_spec(shape):
    nd = len(shape)
    return pl.BlockSpec(shape, lambda *_: (0,) * nd)


def _mod_kernel(n_cond, ct_ref, w_ref, b_ref, o_ref):
    ct = ct_ref[...]
    s = _silu(ct)
    w = w_ref[...]
    rows = [jnp.sum(w * s[:, r:r + 1], axis=0, keepdims=True) for r in range(n_cond)]
    rows += [jnp.zeros_like(rows[0])] * (8 - n_cond)
    o_ref[...] = jnp.concatenate(rows, axis=0) + b_ref[...]


def _modulation(cond_t, n_cond, ada_w, ada_b):
    depth, d, n6 = ada_w.shape
    tn = 1024
    out = pl.pallas_call(
        functools.partial(_mod_kernel, n_cond),
        grid=(depth, n6 // tn),
        in_specs=[
            pl.BlockSpec((d, 8), lambda l, j: (0, 0)),
            pl.BlockSpec((None, d, tn), lambda l, j: (l, 0, j)),
            pl.BlockSpec((None, 1, tn), lambda l, j: (l, 0, j)),
        ],
        out_specs=pl.BlockSpec((None, 8, tn), lambda l, j: (l, 0, j)),
        out_shape=jax.ShapeDtypeStruct((depth, 8, n6), F32),
        compiler_params=_params(("arbitrary", "arbitrary")),
        name="modulation",
    )(cond_t, ada_w, ada_b.reshape(depth, 1, n6))
    m = out[:, :n_cond].reshape(depth, n_cond, 6, d)
    return jnp.pad(m, ((0, 0), (0, 0), (0, 2), (0, 0)))


LOG2E = math.log2(math.e)


def _dup_half(k2, half):
    lane = lax.broadcasted_iota(I32, k2.shape, 1)
    lo = lane < HEAD_DIM
    r = pltpu.roll(k2, HEAD_DIM, 1)
    return jnp.where(lo, k2, r) if half == 0 else jnp.where(lo, r, k2)


def _qkv_kernel(na, xa_ref, xb_ref, mod_ref, g_ref, w_ref, qn_ref, kn_ref, bd_ref,
                cos_ref, sin_ref, q_out, kt_out, vv_out, kv_out):
    i = pl.program_id(0)
    is_ctx = i < na
    x = jnp.where(is_ctx, xa_ref[...], xb_ref[...])
    a = _adaln(x, g_ref[...], mod_ref[0:1, :], mod_ref[1:2, :])
    y = jnp.dot(a.astype(BF16), w_ref[...], preferred_element_type=F32)
    tm = x.shape[0]
    cos = jnp.where(is_ctx, 1.0, cos_ref[...])
    sin = jnp.where(is_ctx, 0.0, sin_ref[...])
    lane = lax.broadcasted_iota(I32, (tm, LANES), 1)
    first16 = (lane % 32) < 16

    def norm_rope(z, wt):
        ss = jnp.dot((z * z).astype(BF16), bd_ref[...], preferred_element_type=F32)
        zn = z * lax.rsqrt(ss * (1.0 / HEAD_DIM) + NORM_EPS) * wt
        outs = []
        for c in range(2):
            t = zn[:, c * LANES:(c + 1) * LANES]
            partner = jnp.where(first16, pltpu.roll(t, LANES - 16, 1), pltpu.roll(t, 16, 1))
            outs.append(t * cos + partner * sin)
        return jnp.concatenate(outs, axis=1)

    nq = N_HEADS * HEAD_DIM
    nk = N_KV_HEADS * HEAD_DIM
    scale = HEAD_DIM ** -0.5 * LOG2E
    for s in range(nq // 256):
        sl = slice(s * 256, (s + 1) * 256)
        q_out[:, sl] = (norm_rope(y[:, sl], qn_ref[:, sl]) * scale).astype(BF16)
    k = norm_rope(y[:, nq:nq + nk], kn_ref[...])
    v = y[:, nq + nk:nq + 2 * nk]
    for g in range(N_KV_HEADS):
        c = g // 2
        kk = _dup_half(k[:, c * LANES:(c + 1) * LANES], g % 2)
        kt_out[g] = kk.T.astype(BF16)
        vv_out[:, g * LANES:(g + 1) * LANES] = _dup_half(v[:, c * LANES:(c + 1) * LANES], g % 2).astype(BF16)

    @pl.when(is_ctx)
    def _():
        kv_out[:, 0:nk] = k
        kv_out[:, nk:2 * nk] = v


def _rope_tables(t_len):
    pos = jnp.arange(t_len)
    row = (pos // GRID_W).astype(F32)[:, None]
    col = (pos % GRID_W).astype(F32)[:, None]
    half = HEAD_DIM // 2
    inv = ROPE_THETA ** (-jnp.arange(0, half, 2, dtype=F32) / half)[None, :]
    ar, ac = row * inv, col * inv
    cos = jnp.concatenate([jnp.cos(ar), jnp.cos(ar), jnp.cos(ac), jnp.cos(ac)], axis=1)
    sin = jnp.concatenate([-jnp.sin(ar), jnp.sin(ar), -jnp.sin(ac), jnp.sin(ac)], axis=1)
    return jnp.tile(cos, (1, 2)), jnp.tile(sin, (1, 2))


def _qkv(xp, xs, t_len, mod, g, w, qn, kn):
    n_p, n_s = xp.shape[0], xs.shape[0]
    tm = _row_tile(n_p, t_len)
    na, nb = n_p // tm, n_s // tm
    nt_s = t_len // tm
    n = n_p + n_s
    d = D_MODEL
    nq, nk = N_HEADS * HEAD_DIM, N_KV_HEADS * HEAD_DIM
    cos, sin = _rope_tables(t_len)
    eye4 = jnp.kron(jnp.eye(4, dtype=F32), jnp.ones((HEAD_DIM, HEAD_DIM), F32)).astype(BF16)
    mod_idx = lambda i: (jnp.where(i < na, 0, 1 + (i - na) // nt_s), 0, 0)
    pos_idx = lambda i: (jnp.maximum(i - na, 0) % nt_s, 0)
    return pl.pallas_call(
        functools.partial(_qkv_kernel, na),
        grid=(na + nb,),
        in_specs=[
            pl.BlockSpec((tm, d), lambda i: (jnp.minimum(i, na - 1), 0)),
            pl.BlockSpec((tm, d), lambda i: (jnp.maximum(i - na, 0), 0)),
            pl.BlockSpec((None, 8, d), mod_idx),
            _const_spec((1, d)),
            _const_spec((d, nq + 2 * nk)),
            _const_spec((1, nq)),
            _const_spec((1, nk)),
            _const_spec((256, 256)),
            pl.BlockSpec((tm, LANES), pos_idx),
            pl.BlockSpec((tm, LANES), pos_idx),
        ],
        out_specs=[pl.BlockSpec((tm, nq), lambda i: (i, 0)),
                   pl.BlockSpec((N_KV_HEADS, LANES, tm), lambda i: (0, 0, i)),
                   pl.BlockSpec((tm, N_KV_HEADS * LANES), lambda i: (i, 0)),
                   pl.BlockSpec((tm, 2 * nk), lambda i: (jnp.minimum(i, na - 1), 0))],
        out_shape=[jax.ShapeDtypeStruct((n, nq), BF16),
                   jax.ShapeDtypeStruct((N_KV_HEADS, LANES, n), BF16),
                   jax.ShapeDtypeStruct((n, N_KV_HEADS * LANES), BF16),
                   jax.ShapeDtypeStruct((n_p, 2 * nk), F32)],
        compiler_params=_params(("arbitrary",)),
        name="qkv",
    )(xp, xs, mod, g, w, qn, kn, eye4, cos, sin)


def _attn_group(q2a, q2b, kt, vv, bias, n_bias, sinks):
    tq = q2a.shape[0]
    lane = lax.broadcasted_iota(I32, (tq, LANES), 1)
    lo = lane < HEAD_DIM
    zero = jnp.zeros_like(q2a)
    qs = jnp.concatenate([jnp.where(lo, q2a, zero), jnp.where(lo, zero, q2a),
                          jnp.where(lo, q2b, zero), jnp.where(lo, zero, q2b)], axis=0)
    s = jnp.dot(qs, kt, preferred_element_type=F32)
    if bias is not None:
        s = jnp.concatenate([s[:, :n_bias] + jnp.concatenate([bias] * 4, axis=0), s[:, n_bias:]], axis=1)
    sink = jnp.concatenate([jnp.full((tq, 1), sk, F32) for sk in sinks], axis=0)
    m = jnp.maximum(jnp.max(s, axis=-1, keepdims=True), sink)
    p = jnp.exp2(s - m)
    den = jnp.sum(p, axis=-1, keepdims=True) + jnp.exp2(sink - m)
    o = jnp.dot(p.astype(BF16), vv, preferred_element_type=F32) / den
    oa = jnp.where(lo, o[0:tq], o[tq:2 * tq])
    ob = jnp.where(lo, o[2 * tq:3 * tq], o[3 * tq:4 * tq])
    return oa, ob


def _ctx_attn_kernel(sink_ref, q_ref, kt_ref, vv_ref, x_ref, mod_ref, wo_ref, o_ref, osc):
    for g in range(N_KV_HEADS):
        q2a = q_ref[:, (2 * g) * LANES:(2 * g + 1) * LANES]
        q2b = q_ref[:, (2 * g + 1) * LANES:(2 * g + 2) * LANES]
        sinks = [sink_ref[4 * g + j] * LOG2E for j in range(4)]
        oa, ob = _attn_group(q2a, q2b, kt_ref[g], vv_ref[:, g * LANES:(g + 1) * LANES], None, 0, sinks)
        osc[:, (2 * g) * LANES:(2 * g + 1) * LANES] = oa.astype(BF16)
        osc[:, (2 * g + 1) * LANES:(2 * g + 2) * LANES] = ob.astype(BF16)
    att = jnp.dot(osc[...], wo_ref[...], preferred_element_type=F32)
    o_ref[...] = x_ref[...] + mod_ref[2:3, :] * att


def _ctx_attention(sink, q, kt, vv, xp, seq, mod, wo):
    n_p, d = xp.shape
    nb = n_p // seq
    return pl.pallas_call(
        _ctx_attn_kernel,
        grid=(nb,),
        in_specs=[
            pl.BlockSpec(memory_space=pltpu.SMEM),
            pl.BlockSpec((seq, d), lambda b: (b, 0)),
            pl.BlockSpec((N_KV_HEADS, LANES, seq), lambda b: (0, 0, b)),
            pl.BlockSpec((seq, N_KV_HEADS * LANES), lambda b: (b, 0)),
            pl.BlockSpec((seq, d), lambda b: (b, 0)),
            pl.BlockSpec((None, 8, d), lambda b: (0, 0, 0)),
            _const_spec((d, d)),
        ],
        out_specs=pl.BlockSpec((seq, d), lambda b: (b, 0)),
        out_shape=jax.ShapeDtypeStruct((n_p, d), F32),
        scratch_shapes=[pltpu.VMEM((seq, d), BF16)],
        compiler_params=_params(("arbitrary",)),
        name="ctx_attention",
    )(sink, q, kt, vv, xp, mod, wo)


def _lat_attn_kernel(t_len, sink_ref, q_ref, ktp_ref, kto_ref, ktn_ref, vvp_ref, vvo_ref, vvn_ref,
                     ckt_ref, cvv_ref, x_ref, mod_ref, wo_ref, o_ref, osc):
    n = pl.program_id(1)
    tq = ATTN_BLOCK
    nloc = 3 * ATTN_BLOCK
    qi = lax.broadcasted_iota(I32, (tq, nloc), 0)
    kj = lax.broadcasted_iota(I32, (tq, nloc), 1)
    kpos = (n - 1) * tq + kj
    valid = (jnp.abs(qi + tq - kj) <= WINDOW) & (kpos >= 0) & (kpos < t_len)
    bias = jnp.where(valid, 0.0, NEG_INF)
    for g in range(N_KV_HEADS):
        vs = slice(g * LANES, (g + 1) * LANES)
        kt = jnp.concatenate([ktp_ref[g], kto_ref[g], ktn_ref[g], ckt_ref[g]], axis=1)
        vv = jnp.concatenate([vvp_ref[:, vs], vvo_ref[:, vs], vvn_ref[:, vs], cvv_ref[:, vs]], axis=0)
        q2a = q_ref[:, (2 * g) * LANES:(2 * g + 1) * LANES]
        q2b = q_ref[:, (2 * g + 1) * LANES:(2 * g + 2) * LANES]
        sinks = [sink_ref[4 * g + j] * LOG2E for j in range(4)]
        oa, ob = _attn_group(q2a, q2b, kt, vv, bias, nloc, sinks)
        osc[:, (2 * g) * LANES:(2 * g + 1) * LANES] = oa.astype(BF16)
        osc[:, (2 * g + 1) * LANES:(2 * g + 2) * LANES] = ob.astype(BF16)
    att = jnp.dot(osc[...], wo_ref[...], preferred_element_type=F32)
    o_ref[...] = x_ref[...] + mod_ref[2:3, :] * att


def _lat_attention(sink, q, kt, vv, ckt, cvv, xs, n_p, t_len, mod, wo):
    n_s, d = xs.shape
    db = n_s // t_len
    tq = ATTN_BLOCK
    nblk = t_len // tq
    assert n_p % tq == 0
    off = n_p // tq
    npast = ckt.shape[3]
    g4 = N_KV_HEADS

    def blk(delta):
        return lambda b, n: off + b * nblk + jnp.clip(n + delta, 0, nblk - 1)

    kt_spec = lambda delta: pl.BlockSpec((g4, LANES, tq), lambda b, n: (0, 0, blk(delta)(b, n)))
    vv_spec = lambda delta: pl.BlockSpec((tq, g4 * LANES), lambda b, n: (blk(delta)(b, n), 0))
    return pl.pallas_call(
        functools.partial(_lat_attn_kernel, t_len),
        grid=(db, nblk),
        in_specs=[
            pl.BlockSpec(memory_space=pltpu.SMEM),
            pl.BlockSpec((tq, d), lambda b, n: (off + b * nblk + n, 0)),
            kt_spec(-1), kt_spec(0), kt_spec(1),
            vv_spec(-1), vv_spec(0), vv_spec(1),
            pl.BlockSpec((None, g4, LANES, npast), lambda b, n: (b, 0, 0, 0)),
            pl.BlockSpec((None, npast, g4 * LANES), lambda b, n: (b, 0, 0)),
            pl.BlockSpec((tq, d), lambda b, n: (b * nblk + n, 0)),
            pl.BlockSpec((None, 8, d), lambda b, n: (1 + b, 0, 0)),
            _const_spec((d, d)),
        ],
        out_specs=pl.BlockSpec((tq, d), lambda b, n: (b * nblk + n, 0)),
        out_shape=jax.ShapeDtypeStruct((n_s, d), F32),
        scratch_shapes=[pltpu.VMEM((tq, d), BF16)],
        compiler_params=_params(("arbitrary", "arbitrary")),
        name="lat_attention",
    )(sink, q, kt, kt, kt, vv, vv, vv, ckt, cvv, xs, mod, wo)


def _ffn_kernel(na, fc, xa_ref, xb_ref, mod_ref, g_ref, wg_ref, wu_ref, wd_ref, o_ref, acc):
    i = pl.program_id(0)
    x = jnp.where(i < na, xa_ref[...], xb_ref[...])
    a = _adaln(x, g_ref[...], mod_ref[3:4, :], mod_ref[4:5, :]).astype(BF16)
    nf = wg_ref.shape[1] // fc
    for f in range(nf):
        sl = slice(f * fc, (f + 1) * fc)
        hg = jnp.dot(a, wg_ref[:, sl], preferred_element_type=F32)
        hu = jnp.dot(a, wu_ref[:, sl], preferred_element_type=F32)
        act = (_silu(hg) * hu).astype(BF16)
        part = jnp.dot(act, wd_ref[sl, :], preferred_element_type=F32)
        if f == 0:
            acc[...] = part
        else:
            acc[...] += part
    o_ref[...] = x + mod_ref[5:6, :] * acc[...]


def _ffn(hp, hs, t_len, mod, g, wg, wu, wd):
    n_p, n_s = hp.shape[0], hs.shape[0]
    d, dff = wg.shape
    tm = _row_tile(n_p, t_len)
    na, nb = n_p // tm, n_s // tm
    nt_s = t_len // tm
    fc = 256
    assert dff % fc == 0
    mod_idx = lambda i: (jnp.where(i < na, 0, 1 + (i - na) // nt_s), 0, 0)
    return pl.pallas_call(
        functools.partial(_ffn_kernel, na, fc),
        grid=(na + nb,),
        in_specs=[
            pl.BlockSpec((tm, d), lambda i: (jnp.minimum(i, na - 1), 0)),
            pl.BlockSpec((tm, d), lambda i: (jnp.maximum(i - na, 0), 0)),
            pl.BlockSpec((None, 8, d), mod_idx),
            _const_spec((1, d)),
            _const_spec((d, dff)),
            _const_spec((d, dff)),
            _const_spec((dff, d)),
        ],
        out_specs=pl.BlockSpec((tm, d), lambda i: (i, 0)),
        out_shape=jax.ShapeDtypeStruct((n_p + n_s, d), F32),
        scratch_shapes=[pltpu.VMEM((tm, d), F32)],
        compiler_params=_params(("arbitrary",)),
        name="ffn",
    )(hp, hs, mod, g, wg, wu, wd)


def _split3(x):
    hi = x.astype(BF16)
    r1 = x - hi.astype(F32)
    mid = r1.astype(BF16)
    lo = (r1 - mid.astype(F32)).astype(BF16)
    return hi, mid, lo


def _gla_in_kernel(nt_s, x_ref, mod_ref, g_ref, w_ref, w1_ref, w2_ref, gb_ref, tri_ref,
                   qk_out, v_out, r_out, b_out):
    x = x_ref[...]
    tm = x.shape[0]
    a = _adaln(x, g_ref[...], mod_ref[0:1, :], mod_ref[1:2, :]).astype(BF16)
    hk = GLA_HEADS * GLA_DK
    hv = GLA_HEADS * GLA_DV
    q = jnp.dot(a, w_ref[:, 0:hk], preferred_element_type=F32)
    qk_out[:, 0:hk] = q * (GLA_DK ** -0.5)
    qk_out[:, hk:2 * hk] = jnp.dot(a, w_ref[:, hk:2 * hk], preferred_element_type=F32)
    v_out[...] = jnp.dot(a, w_ref[:, 2 * hk:2 * hk + hv], preferred_element_type=F32)
    r_out[...] = jnp.dot(a, w_ref[:, 2 * hk + hv:2 * hk + 2 * hv], preferred_element_type=F32).astype(BF16)
    z1 = jnp.dot(a, w1_ref[...], preferred_element_type=F32)
    z1h = z1.astype(BF16)
    z1l = (z1 - z1h.astype(F32)).astype(BF16)
    z = (jnp.dot(z1h, w2_ref[...], preferred_element_type=F32)
         + jnp.dot(z1l, w2_ref[...], preferred_element_type=F32) + gb_ref[...])
    gate = (jnp.minimum(z, 0.0) - jnp.log(1.0 + jnp.exp(-jnp.abs(z)))) * (1.0 / GLA_GATE_TAU)
    c = GLA_CHUNK
    tri = tri_ref[...]
    for j in range(tm // c):
        rows = slice(j * c, (j + 1) * c)
        for dr in range(2):
            cols = slice(dr * hk, (dr + 1) * hk)
            parts = _split3(gate[rows, cols])
            acc = jnp.dot(tri[dr], parts[0], preferred_element_type=F32)
            acc += jnp.dot(tri[dr], parts[1], preferred_element_type=F32)
            acc += jnp.dot(tri[dr], parts[2], preferred_element_type=F32)
            b_out[rows, cols] = acc


def _gla_in(h, n_p, t_len, mod, g, w_in, w1, w2, gate_b):
    n, d = h.shape
    tm = _row_tile(n_p, t_len)
    na = n_p // tm
    nt_s = t_len // tm
    hk, hv = GLA_HEADS * GLA_DK, GLA_HEADS * GLA_DV
    c = GLA_CHUNK
    lower = jnp.tril(jnp.ones((c, c), F32))
    tri = jnp.stack([lower, lower.T]).astype(BF16)
    mod_idx = lambda i: (jnp.where(i < na, 0, 1 + (i - na) // nt_s), 0, 0)
    row = lambda i: (i, 0)
    return pl.pallas_call(
        functools.partial(_gla_in_kernel, nt_s),
        grid=(n // tm,),
        in_specs=[
            pl.BlockSpec((tm, d), row),
            pl.BlockSpec((None, 8, d), mod_idx),
            _const_spec((1, d)),
            _const_spec(w_in.shape),
            _const_spec(w1.shape),
            _const_spec(w2.shape),
            _const_spec((1, 2 * hk)),
            _const_spec((2, c, c)),
        ],
        out_specs=[pl.BlockSpec((tm, 2 * hk), row), pl.BlockSpec((tm, hv), row),
                   pl.BlockSpec((tm, hv), row), pl.BlockSpec((tm, 2 * hk), row)],
        out_shape=[jax.ShapeDtypeStruct((n, 2 * hk), F32), jax.ShapeDtypeStruct((n, hv), F32),
                   jax.ShapeDtypeStruct((n, hv), BF16), jax.ShapeDtypeStruct((n, 2 * hk), F32)],
        compiler_params=_params(("arbitrary",)),
        name="gla_in",
    )(h, mod, g, w_in, w1, w2, gate_b, tri)


def _gla_chunk(q, k, v, b, st_ref, forward):
    c = q.shape[0]
    if forward:
        b_end, b_mid = b[c - 1:c, :], b[c // 2 - 1:c // 2, :]
    else:
        b_end, b_mid = b[0:1, :], b[c // 2:c // 2 + 1, :]
    e_up = jnp.exp(b - b_mid)
    e_dn = jnp.exp(b_mid - b)
    qa = q * e_up
    ka = k * e_dn
    qd = (qa * jnp.exp(b_mid)).astype(BF16)
    kd = (ka * jnp.exp(b_end - b_mid)).astype(BF16)
    att = lax.dot_general(qa.astype(BF16), ka.astype(BF16), (((1,), (1,)), ((), ())),
                          preferred_element_type=F32)
    ri = lax.broadcasted_iota(I32, (c, c), 0)
    ci = lax.broadcasted_iota(I32, (c, c), 1)
    keep = (ci <= ri) if forward else (ci >= ri)
    att = jnp.where(keep, att, 0.0).astype(BF16)
    vb = v.astype(BF16)
    st = st_ref[...]
    o = lax.dot_general(qd, st.astype(BF16), (((1,), (1,)), ((), ())), preferred_element_type=F32)
    o = o + jnp.dot(att, vb, preferred_element_type=F32)
    upd = lax.dot_general(vb, kd, (((0,), (0,)), ((), ())), preferred_element_type=F32)
    st_ref[...] = st * jnp.exp(b_end) + upd
    return o


def _gla_scan_kernel(qkf_ref, vf_ref, bf_ref, qkb_ref, vb_ref, bb_ref, sf0_ref, sb0_ref,
                     of_ref, ob_ref, sf_ref, sb_ref, stf, stb):
    i = pl.program_id(1)
    nblk = pl.num_programs(1)
    c = GLA_CHUNK
    ncb = qkf_ref.shape[0] // c
    hk = GLA_HEADS * GLA_DK
    dk, dv = GLA_DK, GLA_DV

    @pl.when(i == 0)
    def _():
        for h in range(GLA_HEADS):
            stf[h] = sf0_ref[h].T
            stb[h] = sb0_ref[h].T

    def step(ci, carry):
        rf = pl.ds(pl.multiple_of(ci * c, c), c)
        rb = pl.ds(pl.multiple_of((ncb - 1 - ci) * c, c), c)
        for h in range(GLA_HEADS):
            qs = slice(h * dk, (h + 1) * dk)
            ks = slice(hk + h * dk, hk + (h + 1) * dk)
            vs = slice(h * dv, (h + 1) * dv)
            of = _gla_chunk(qkf_ref[rf, qs], qkf_ref[rf, ks], vf_ref[rf, vs], bf_ref[rf, qs],
                            stf.at[h], True)
            of_ref[rf, vs] = of.astype(of_ref.dtype)
            ob = _gla_chunk(qkb_ref[rb, qs], qkb_ref[rb, ks], vb_ref[rb, vs], bb_ref[rb, qs],
                            stb.at[h], False)
            ob_ref[rb, vs] = ob.astype(ob_ref.dtype)
        return carry

    lax.fori_loop(0, ncb, step, 0)

    @pl.when(i == nblk - 1)
    def _():
        for h in range(GLA_HEADS):
            sf_ref[h] = stf[h].T
            sb_ref[h] = stb[h].T


def _gla_scan(qk, v, bc, sf0, sb0, b_off, nb, t_len):
    n = qk.shape[0]
    assert n % t_len == 0
    ns = n // t_len
    tb = _row_tile(t_len)
    nblk = t_len // tb
    qk3 = qk.reshape(ns, t_len, qk.shape[1])
    v3 = v.reshape(ns, t_len, v.shape[1])
    bc3 = bc.reshape(ns, t_len, bc.shape[1])
    h, dk, dv = GLA_HEADS, GLA_DK, GLA_DV
    hk = h * dk
    st_spec = pl.BlockSpec((None, h, dk, dv), lambda b, i: (b, 0, 0, 0))
    fwd = lambda b, i: (b_off + b, i, 0)
    bwd = lambda b, i: (b_off + b, nblk - 1 - i, 0)
    return pl.pallas_call(
        _gla_scan_kernel,
        grid=(nb, nblk),
        in_specs=[
            pl.BlockSpec((None, tb, 2 * hk), fwd),
            pl.BlockSpec((None, tb, h * dv), fwd),
            pl.BlockSpec((None, tb, hk), fwd),
            pl.BlockSpec((None, tb, 2 * hk), bwd),
            pl.BlockSpec((None, tb, h * dv), bwd),
            pl.BlockSpec((None, tb, hk), lambda b, i: (b_off + b, nblk - 1 - i, 1)),
            st_spec, st_spec,
        ],
        out_specs=[pl.BlockSpec((None, tb, h * dv), lambda b, i: (b, i, 0)),
                   pl.BlockSpec((None, tb, h * dv), lambda b, i: (b, nblk - 1 - i, 0)),
                   st_spec, st_spec],
        out_shape=[jax.ShapeDtypeStruct((nb, t_len, h * dv), BF16),
                   jax.ShapeDtypeStruct((nb, t_len, h * dv), BF16),
                   jax.ShapeDtypeStruct((nb, h, dk, dv), F32),
                   jax.ShapeDtypeStruct((nb, h, dk, dv), F32)],
        scratch_shapes=[pltpu.VMEM((h, dv, dk), F32), pltpu.VMEM((h, dv, dk), F32)],
        compiler_params=_params(("arbitrary", "arbitrary")),
        name="gla_scan",
    )(qk3, v3, bc3, qk3, v3, bc3, sf0, sb0)


def _gla_out_kernel(na, ofa_ref, oba_ref, ofb_ref, obb_ref, r_ref, h_ref, mod_ref, on_ref, wo_ref,
                    g2_ref, rw_ref, tri_ref, h_out, f_out, route_out, wt_out, cnt_out, run):
    i = pl.program_id(0)
    o = jnp.where(i < na, ofa_ref[...].astype(F32) + oba_ref[...].astype(F32),
                  ofb_ref[...].astype(F32) + obb_ref[...].astype(F32))
    tm = o.shape[0]
    dv = GLA_DV
    parts = []
    for hh in range(GLA_HEADS):
        oh = o[:, hh * dv:(hh + 1) * dv]
        parts.append(_rms(oh) * on_ref[...])
    on = jnp.concatenate(parts, axis=1)
    r = r_ref[...].astype(F32)
    gated = (on * _silu(r)).astype(BF16)
    h = h_ref[...] + mod_ref[2:3, :] * jnp.dot(gated, wo_ref[...], preferred_element_type=F32)
    h_out[...] = h
    f = _adaln(h, g2_ref[...], mod_ref[3:4, :], mod_ref[4:5, :])
    f_out[...] = _pack_halves(f)
    fh = f.astype(BF16)
    fl = (f - fh.astype(F32)).astype(BF16)
    logits = (jnp.dot(fh, rw_ref[0], preferred_element_type=F32)
              + jnp.dot(fl, rw_ref[0], preferred_element_type=F32)
              + jnp.dot(fh, rw_ref[1], preferred_element_type=F32))
    lane = lax.broadcasted_iota(I32, (tm, LANES), 1)
    lane_f = lane.astype(F32)
    logits = jnp.where(lane < N_EXPERTS, logits, -jnp.inf)
    m1 = jnp.max(logits, axis=-1, keepdims=True)
    i1 = jnp.min(jnp.where(logits == m1, lane_f, float(LANES)), axis=-1, keepdims=True)
    rest = jnp.where(lane_f == i1, -jnp.inf, logits)
    m2 = jnp.max(rest, axis=-1, keepdims=True)
    i2 = jnp.min(jnp.where(rest == m2, lane_f, float(LANES)), axis=-1, keepdims=True)
    e2 = jnp.exp(m2 - m1)
    w1 = 1.0 / (1.0 + e2)
    w2 = e2 / (1.0 + e2)
    wt_out[...] = jnp.where(lane == 0, w1, jnp.where(lane == 1, w2, 0.0))
    sel1 = lane_f == i1
    sel2 = lane_f == i2
    picked = jnp.where(sel1, 1.0, jnp.where(sel2, 1.0, 0.0))

    @pl.when(i == 0)
    def _():
        run[...] = jnp.zeros_like(run)

    before = jnp.dot(tri_ref[...], picked.astype(BF16), preferred_element_type=F32) + run[0:1, :]
    rank1 = jnp.sum(jnp.where(sel1, before, 0.0), axis=-1, keepdims=True)
    rank2 = jnp.sum(jnp.where(sel2, before, 0.0), axis=-1, keepdims=True)
    run[...] = run[...] + jnp.sum(picked, axis=0, keepdims=True)
    cnt_out[...] = run[...]
    routing = jnp.where(lane == 0, i1, jnp.where(lane == 1, i2,
                        jnp.where(lane == 2, rank1, jnp.where(lane == 3, rank2, 0.0))))
    route_out[...] = routing.T[0:8, :]


def _gla_out(of_p, ob_p, of_s, ob_s, r, h, n_p, t_len, mod, out_norm, wo, g2, router_w):
    n, d = h.shape
    tm = _row_tile(n_p, t_len)
    na = n_p // tm
    nt_s = t_len // tm
    mod_idx = lambda i: (jnp.where(i < na, 0, 1 + (i - na) // nt_s), 0, 0)
    row = lambda i: (i, 0)
    ctx_row = lambda i: (jnp.minimum(i, na - 1), 0)
    lat_row = lambda i: (jnp.maximum(i - na, 0), 0)
    tri = jnp.tril(jnp.ones((tm, tm), F32), -1).astype(BF16)
    return pl.pallas_call(
        functools.partial(_gla_out_kernel, na),
        grid=(n // tm,),
        in_specs=[
            pl.BlockSpec((tm, d), ctx_row),
            pl.BlockSpec((tm, d), ctx_row),
            pl.BlockSpec((tm, d), lat_row),
            pl.BlockSpec((tm, d), lat_row),
            pl.BlockSpec((tm, d), row),
            pl.BlockSpec((tm, d), row),
            pl.BlockSpec((None, 8, d), mod_idx),
            _const_spec((1, GLA_DV)),
            _const_spec((d, d)),
            _const_spec((1, d)),
            _const_spec((2, d, LANES)),
            _const_spec((tm, tm)),
        ],
        out_specs=[pl.BlockSpec((tm, d), row), pl.BlockSpec((tm, d // 2), row),
                   pl.BlockSpec((8, tm), lambda i: (0, i)), pl.BlockSpec((tm, LANES), row),
                   _const_spec((8, LANES))],
        out_shape=[jax.ShapeDtypeStruct((n, d), F32), jax.ShapeDtypeStruct((n, d // 2), jnp.uint32),
                   jax.ShapeDtypeStruct((8, n), F32), jax.ShapeDtypeStruct((n, LANES), F32),
                   jax.ShapeDtypeStruct((8, LANES), F32)],
        scratch_shapes=[pltpu.VMEM((8, LANES), F32)],
        compiler_params=_params(("arbitrary",)),
        name="gla_out_router",
    )(of_p, ob_p, of_s, ob_s, r, h, mod, out_norm, wo, g2, router_w, tri)


SC_CORES = 2
SC_SUBCORES = 16
SC_CHUNK = 32


def _sc_gather_rows(table, idx):
    nw = SC_CORES * SC_SUBCORES
    b = idx.shape[0]
    d = table.shape[1]
    assert b % (nw * SC_CHUNK) == 0
    per_w = b // nw
    n_chunks = per_w // SC_CHUNK
    mesh = plsc.VectorSubcoreMesh(core_axis_name="c", subcore_axis_name="s",
                                  num_cores=SC_CORES, num_subcores=SC_SUBCORES)

    def body(table_hbm, idx_hbm, out_hbm, idx_v, rows_v, sem):
        wid = lax.axis_index("s") * SC_CORES + lax.axis_index("c")
        base = wid * per_w
        pltpu.sync_copy(idx_hbm.at[wid], idx_v)

        @pl.loop(0, n_chunks)
        def _(j):
            pltpu.async_copy(table_hbm.at[idx_v.at[j]], rows_v, sem).wait()
            pltpu.sync_copy(rows_v, out_hbm.at[pl.ds(base + j * SC_CHUNK, SC_CHUNK)])

    return pl.kernel(
        body,
        out_type=jax.ShapeDtypeStruct((b, d), table.dtype),
        mesh=mesh,
        scratch_types=[pltpu.VMEM((n_chunks, SC_CHUNK), I32),
                       pltpu.VMEM((SC_CHUNK, d), table.dtype),
                       pltpu.SemaphoreType.DMA],
        name="sc_gather_rows",
    )(table, idx.reshape(nw, n_chunks, SC_CHUNK))


def _sc_scatter_rows(rows, idx2, p):
    nw = SC_CORES * SC_SUBCORES
    n, d = rows.shape
    assert n % (nw * SC_CHUNK) == 0
    per_w = n // nw
    n_chunks = per_w // SC_CHUNK
    mesh = plsc.VectorSubcoreMesh(core_axis_name="c", subcore_axis_name="s",
                                  num_cores=SC_CORES, num_subcores=SC_SUBCORES)

    def body(rows_hbm, idx0_hbm, idx1_hbm, out_hbm, idx0_v, idx1_v, rows_v):
        wid = lax.axis_index("s") * SC_CORES + lax.axis_index("c")
        base = wid * per_w
        pltpu.sync_copy(idx0_hbm.at[wid], idx0_v)
        pltpu.sync_copy(idx1_hbm.at[wid], idx1_v)

        @pl.loop(0, n_chunks)
        def _(j):
            pltpu.sync_copy(rows_hbm.at[pl.ds(base + j * SC_CHUNK, SC_CHUNK)], rows_v)
            pltpu.sync_copy(rows_v, out_hbm.at[idx0_v.at[j]])
            pltpu.sync_copy(rows_v, out_hbm.at[idx1_v.at[j]])

    return pl.kernel(
        body,
        out_type=jax.ShapeDtypeStruct((p, d), rows.dtype),
        mesh=mesh,
        scratch_types=[pltpu.VMEM((n_chunks, SC_CHUNK), I32),
                       pltpu.VMEM((n_chunks, SC_CHUNK), I32),
                       pltpu.VMEM((SC_CHUNK, d), rows.dtype)],
        name="sc_scatter_rows",
    )(rows, idx2[0].reshape(nw, n_chunks, SC_CHUNK), idx2[1].reshape(nw, n_chunks, SC_CHUNK))


EXPERT_SUB = 512


def _pack_halves(x):
    k = x.shape[1] // 2
    lo = lax.bitcast_convert_type(x[:, :k].astype(BF16).astype(F32), jnp.uint32)
    hi = lax.bitcast_convert_type(x[:, k:].astype(BF16).astype(F32), jnp.uint32)
    return (lo >> 16) | (hi & jnp.uint32(0xFFFF0000))


def _unpack_halves(w):
    lo = lax.bitcast_convert_type(w << 16, F32).astype(BF16)
    hi = lax.bitcast_convert_type(w & jnp.uint32(0xFFFF0000), F32).astype(BF16)
    return lo, hi


def _expert_kernel(te_ref, ts_ref, nu_ref, nv_ref, x_ref, wg_ref, wu_ref, wd_ref, y_ref, acc, xb):
    i = pl.program_id(0)
    f = pl.program_id(1)
    nf = pl.num_programs(1)
    tm = x_ref.shape[0]
    half = x_ref.shape[1]
    sub = EXPERT_SUB
    nvalid = nv_ref[i]

    @pl.when(f == 0)
    def _():
        acc[...] = jnp.zeros_like(acc)

    @pl.when((f == 0) & (nvalid > 0))
    def _():
        rid = lax.broadcasted_iota(I32, (tm, 1), 0)
        lo, hi = _unpack_halves(x_ref[...])
        zero = jnp.zeros_like(lo)
        xb[:, 0:half] = jnp.where(rid < nvalid, lo, zero)
        xb[:, half:2 * half] = jnp.where(rid < nvalid, hi, zero)

    def compute(rows):
        x = xb[rows, :]
        hg = jnp.dot(x, wg_ref[...].astype(BF16), preferred_element_type=F32)
        hu = jnp.dot(x, wu_ref[...].astype(BF16), preferred_element_type=F32)
        act = (_silu(hg) * hu).astype(BF16)
        acc[rows, :] += jnp.dot(act, wd_ref[...].astype(BF16), preferred_element_type=F32)

    @pl.when(nvalid > sub)
    def _():
        compute(slice(0, tm))

    @pl.when((nvalid > 0) & (nvalid <= sub))
    def _():
        compute(slice(0, sub))

    @pl.when(f == nf - 1)
    def _():
        y_ref[...] = _pack_halves(acc[...])


def _experts(x, tile_expert, tile_src, n_used, n_valid, wg, wu, wd, tm, tf):
    p, half = x.shape
    d = 2 * half
    ne, _, dff = wg.shape
    nf = dff // tf
    assert dff % tf == 0 and tm % EXPERT_SUB == 0

    def fidx(i, f, nu):
        return jnp.where(i < nu[0], f, nf - 1)

    grid_spec = pltpu.PrefetchScalarGridSpec(
        num_scalar_prefetch=4,
        grid=(p // tm, nf),
        in_specs=[
            pl.BlockSpec((tm, half), lambda i, f, te, ts, nu, nv: (ts[i], 0)),
            pl.BlockSpec((None, d, tf), lambda i, f, te, ts, nu, nv: (te[i], 0, fidx(i, f, nu))),
            pl.BlockSpec((None, d, tf), lambda i, f, te, ts, nu, nv: (te[i], 0, fidx(i, f, nu))),
            pl.BlockSpec((None, tf, d), lambda i, f, te, ts, nu, nv: (te[i], fidx(i, f, nu), 0)),
        ],
        out_specs=pl.BlockSpec((tm, half), lambda i, f, te, ts, nu, nv: (i, 0)),
        scratch_shapes=[pltpu.VMEM((tm, d), F32), pltpu.VMEM((tm, d), BF16)],
    )
    return pl.pallas_call(
        _expert_kernel,
        grid_spec=grid_spec,
        out_shape=jax.ShapeDtypeStruct((p, half), jnp.uint32),
        compiler_params=_params(("arbitrary", "arbitrary"), 60 * 1024 * 1024),
        name="moe_experts",
    )(tile_expert, tile_src, n_used, n_valid, x, wg, wu, wd)


def _combine_kernel(wt_ref, h_ref, mod_ref, y0_ref, y1_ref, o_ref):
    wt = wt_ref[...]
    half = y0_ref.shape[1]
    lo0, hi0 = _unpack_halves(y0_ref[...])
    lo1, hi1 = _unpack_halves(y1_ref[...])
    w0, w1 = wt[:, 0:1], wt[:, 1:2]
    h = h_ref[...]
    gate = mod_ref[5:6, :]
    o_ref[:, 0:half] = h[:, 0:half] + gate[:, 0:half] * (w0 * lo0.astype(F32) + w1 * lo1.astype(F32))
    o_ref[:, half:] = h[:, half:] + gate[:, half:] * (w0 * hi0.astype(F32) + w1 * hi1.astype(F32))


def _combine(yg, wt, h, mod, row_off, n_rows, n_mod, mod_off, tm):
    d = h.shape[1]
    nt = n_rows // tm
    toff = row_off // tm
    per_mod = n_rows // n_mod // tm
    return pl.pallas_call(
        _combine_kernel,
        grid=(nt,),
        in_specs=[
            pl.BlockSpec((tm, LANES), lambda i: (toff + i, 0)),
            pl.BlockSpec((tm, d), lambda i: (toff + i, 0)),
            pl.BlockSpec((None, 8, d), lambda i: (mod_off + i // per_mod, 0, 0)),
            pl.BlockSpec((None, tm, d // 2), lambda i: (0, toff + i, 0)),
            pl.BlockSpec((None, tm, d // 2), lambda i: (1, toff + i, 0)),
        ],
        out_specs=pl.BlockSpec((tm, d), lambda i: (i, 0)),
        out_shape=jax.ShapeDtypeStruct((n_rows, d), F32),
        compiler_params=_params(("arbitrary",)),
        name="moe_combine",
    )(wt, h, mod, yg, yg)


def _route(route, counts, n, tm):
    cnt = counts[0, :N_EXPERTS].astype(I32)
    padded = ((cnt + tm - 1) // tm) * tm
    gend = jnp.cumsum(padded)
    goff = gend - padded
    e = route[0:2].astype(I32)
    rank = route[2:4].astype(I32)
    onehot = e[:, :, None] == jnp.arange(N_EXPERTS, dtype=I32)[None, None, :]
    dest = jnp.sum(jnp.where(onehot, goff[None, None, :], 0), axis=-1) + rank
    p = 2 * n + N_EXPERTS * tm
    n_used = gend[-1] // tm
    tiles = jnp.arange(p // tm, dtype=I32)
    tile_src = jnp.minimum(tiles, n_used - 1)
    tile_expert = jnp.minimum(jnp.sum((gend[None, :] <= (tile_src * tm)[:, None]).astype(I32), axis=1),
                              N_EXPERTS - 1)
    used = goff[tile_expert] + cnt[tile_expert]
    n_valid = jnp.where(tiles < n_used, jnp.clip(used - tiles * tm, 0, tm), 0).astype(I32)
    return dest, p, tile_expert, tile_src, n_used.reshape(1).astype(I32), n_valid


def kernel(x_prompt, x_sample, cache_k, cache_v, state_fwd, state_bwd, c, c_ctx, ada_w, ada_b, norm1_g, norm2_g, attn_w_qkv, attn_q_norm, attn_k_norm, attn_sink, attn_w_o, gla_w_in, gla_gate_w1, gla_gate_w2, gla_gate_b, gla_out_norm, gla_w_o, ffn_w_gate, ffn_w_up, ffn_w_down, moe_router, moe_w_gate, moe_w_up, moe_w_down):
    bp, seq, d = x_prompt.shape
    db, t_len, _ = x_sample.shape
    n_p, n_s = bp * seq, db * t_len
    n = n_p + n_s
    xp = x_prompt.reshape(n_p, d)
    xs = x_sample.reshape(n_s, d)

    cond = jnp.concatenate([c_ctx[None, :], c], axis=0)
    assert cond.shape[0] <= 8
    cond_t = jnp.pad(cond, ((0, 8 - cond.shape[0]), (0, 0))).T
    mods = _modulation(cond_t, cond.shape[0], ada_w, ada_b)

    nk = N_KV_HEADS * HEAD_DIM
    qn = jnp.tile(attn_q_norm[0], N_HEADS)[None, :]
    kn = jnp.tile(attn_k_norm[0], N_KV_HEADS)[None, :]
    q, kt, vv, kv = _qkv(xp, xs, t_len, mods[0], norm1_g[0][None, :], attn_w_qkv[0].astype(BF16), qn, kn)
    wo0 = attn_w_o[0].astype(BF16)
    sink = attn_sink[0]
    hp = _ctx_attention(sink, q, kt, vv, xp, seq, mods[0], wo0)
    ck = cache_k[:, 0].astype(BF16)
    cv = cache_v[:, 0].astype(BF16)
    ckt = jnp.concatenate([ck, ck], axis=-1).transpose(0, 2, 3, 1)
    cvv = jnp.concatenate([cv, cv], axis=-1).reshape(db, cv.shape[1], N_KV_HEADS * LANES)
    hs = _lat_attention(sink, q, kt, vv, ckt, cvv, xs, n_p, t_len, mods[0], wo0)
    new_k = kv[:n_p, :nk].reshape(bp, 1, seq, N_KV_HEADS, HEAD_DIM)
    new_v = kv[:n_p, nk:].reshape(bp, 1, seq, N_KV_HEADS, HEAD_DIM)
    h = _ffn(hp, hs, t_len, mods[0], norm2_g[0][None, :], ffn_w_gate[0].astype(BF16),
             ffn_w_up[0].astype(BF16), ffn_w_down[0].astype(BF16))

    hk = GLA_HEADS * GLA_DK
    rank = GLA_GATE_RANK
    w1 = jnp.concatenate([gla_gate_w1[0, 0], gla_gate_w1[0, 1]], axis=1)
    w1 = jnp.pad(w1, ((0, 0), (0, LANES - 2 * rank))).astype(BF16)
    w2 = jnp.zeros((LANES, 2 * hk), F32)
    w2 = w2.at[0:rank, 0:hk].set(gla_gate_w2[0, 0]).at[rank:2 * rank, hk:].set(gla_gate_w2[0, 1]).astype(BF16)
    gate_b = gla_gate_b[0].reshape(1, 2 * hk)
    qk, v, r, bc = _gla_in(h, n_p, t_len, mods[1], norm1_g[1][None, :], gla_w_in[0].astype(BF16),
                           w1, w2, gate_b)
    zeros = jnp.zeros((bp, GLA_HEADS, GLA_DK, GLA_DV), F32)
    of_p, ob_p, new_sf, new_sb = _gla_scan(qk, v, bc, zeros, zeros, 0, bp, seq)
    of_s, ob_s, _, _ = _gla_scan(qk, v, bc, state_fwd[:, 0], state_bwd[:, 0], n_p // t_len, db, t_len)
    rw = jnp.pad(moe_router[0], ((0, 0), (0, LANES - N_EXPERTS)))
    rw_hi = rw.astype(BF16)
    rw_lo = (rw - rw_hi.astype(F32)).astype(BF16)
    h, f, route, wt, counts = _gla_out(
        of_p.reshape(n_p, d), ob_p.reshape(n_p, d), of_s.reshape(n_s, d), ob_s.reshape(n_s, d), r, h,
        n_p, t_len, mods[1], gla_out_norm[0][None, :], gla_w_o[0].astype(BF16), norm2_g[1][None, :],
        jnp.stack([rw_hi, rw_lo]))

    tm_e = 1024
    dest_k, p_rows, tile_expert, tile_src, n_used, n_valid = _route(route, counts, n, tm_e)
    xg = _sc_scatter_rows(f, dest_k, p_rows)
    y = _experts(xg, tile_expert, tile_src, n_used, n_valid, moe_w_gate[0], moe_w_up[0],
                 moe_w_down[0], tm_e, 896)
    yg = _sc_gather_rows(y, dest_k.reshape(2 * n)).reshape(2, n, d // 2)
    tm_c = _row_tile(n_p, t_len)
    y_p = _combine(yg, wt, h, mods[1], 0, n_p, 1, 0, tm_c)
    y_s = _combine(yg, wt, h, mods[1], n_p, n_s, db, 1, tm_c)

    return (y_p.reshape(bp, seq, d), y_s.reshape(db, t_len, d), new_k, new_v,
            new_sf[:, None], new_sb[:, None])
```

```python
import functools
import math

import jax
import jax.numpy as jnp
from jax import lax
from jax.experimental import pallas as pl
from jax.experimental.pallas import tpu as pltpu
from jax.experimental.pallas import tpu_sc as plsc

F32 = jnp.float32
BF16 = jnp.bfloat16
I32 = jnp.int32

D_MODEL = 1024
N_HEADS = 16
N_KV_HEADS = 4
HEAD_DIM = 64
GRID_W = 64
WINDOW = 128
ATTN_BLOCK = 128
ROPE_THETA = 10000.0
GLA_HEADS = 4
GLA_DK = 128
GLA_DV = 256
GLA_GATE_RANK = 16
GLA_GATE_TAU = 16.0
GLA_CHUNK = 64
N_EXPERTS = 8
NORM_EPS = 1e-6
NEG_INF = -1e30

LANES = 128
VMEM_LIMIT = 56 * 1024 * 1024


def _params(sem, vmem=VMEM_LIMIT):
    return pltpu.CompilerParams(dimension_semantics=sem, vmem_limit_bytes=vmem)


def _row_tile(*counts, cap=512):
    t = cap
    while any(c % t for c in counts):
        t //= 2
    assert t >= 8
    return t


def _rms(x):
    return x * lax.rsqrt(jnp.mean(x * x, axis=-1, keepdims=True) + NORM_EPS)


def _adaln(x, g, shift, scale):
    return _rms(x) * (g * (1.0 + scale)) + shift


def _silu(x):
    return x * jax.nn.sigmoid(x)


def _const_spec(shape):
    nd = len(shape)
    return pl.BlockSpec(shape, lambda *_: (0,) * nd)


def _mod_kernel(n_cond, ct_ref, w_ref, b_ref, o_ref):
    ct = ct_ref[...]
    s = _silu(ct)
    w = w_ref[...]
    rows = [jnp.sum(w * s[:, r:r + 1], axis=0, keepdims=True) for r in range(n_cond)]
    rows += [jnp.zeros_like(rows[0])] * (8 - n_cond)
    o_ref[...] = jnp.concatenate(rows, axis=0) + b_ref[...]


def _modulation(cond_t, n_cond, ada_w, ada_b):
    depth, d, n6 = ada_w.shape
    tn = 1024
    out = pl.pallas_call(
        functools.partial(_mod_kernel, n_cond),
        grid=(depth, n6 // tn),
        in_specs=[
            pl.BlockSpec((d, 8), lambda l, j: (0, 0)),
            pl.BlockSpec((None, d, tn), lambda l, j: (l, 0, j)),
            pl.BlockSpec((None, 1, tn), lambda l, j: (l, 0, j)),
        ],
        out_specs=pl.BlockSpec((None, 8, tn), lambda l, j: (l, 0, j)),
        out_shape=jax.ShapeDtypeStruct((depth, 8, n6), F32),
        compiler_params=_params(("arbitrary", "arbitrary")),
        name="modulation",
    )(cond_t, ada_w, ada_b.reshape(depth, 1, n6))
    m = out[:, :n_cond].reshape(depth, n_cond, 6, d)
    return jnp.pad(m, ((0, 0), (0, 0), (0, 2), (0, 0)))


LOG2E = math.log2(math.e)


def _dup_half(k2, half):
    lane = lax.broadcasted_iota(I32, k2.shape, 1)
    lo = lane < HEAD_DIM
    r = pltpu.roll(k2, HEAD_DIM, 1)
    return jnp.where(lo, k2, r) if half == 0 else jnp.where(lo, r, k2)


def _qkv_kernel(na, xa_ref, xb_ref, mod_ref, g_ref, w_ref, qn_ref, kn_ref, bd_ref,
                cos_ref, sin_ref, q_out, kt_out, vv_out, kv_out):
    i = pl.program_id(0)
    is_ctx = i < na
    x = jnp.where(is_ctx, xa_ref[...], xb_ref[...])
    a = _adaln(x, g_ref[...], mod_ref[0:1, :], mod_ref[1:2, :])
    y = jnp.dot(a.astype(BF16), w_ref[...], preferred_element_type=F32)
    tm = x.shape[0]
    cos = jnp.where(is_ctx, 1.0, cos_ref[...])
    sin = jnp.where(is_ctx, 0.0, sin_ref[...])
    lane = lax.broadcasted_iota(I32, (tm, LANES), 1)
    first16 = (lane % 32) < 16

    def norm_rope(z, wt):
        ss = jnp.dot((z * z).astype(BF16), bd_ref[...], preferred_element_type=F32)
        zn = z * lax.rsqrt(ss * (1.0 / HEAD_DIM) + NORM_EPS) * wt
        outs = []
        for c in range(2):
            t = zn[:, c * LANES:(c + 1) * LANES]
            partner = jnp.where(first16, pltpu.roll(t, LANES - 16, 1), pltpu.roll(t, 16, 1))
            outs.append(t * cos + partner * sin)
        return jnp.concatenate(outs, axis=1)

    nq = N_HEADS * HEAD_DIM
    nk = N_KV_HEADS * HEAD_DIM
    scale = HEAD_DIM ** -0.5 * LOG2E
    for s in range(nq // 256):
        sl = slice(s * 256, (s + 1) * 256)
        q_out[:, sl] = (norm_rope(y[:, sl], qn_ref[:, sl]) * scale).astype(BF16)
    k = norm_rope(y[:, nq:nq + nk], kn_ref[...])
    v = y[:, nq + nk:nq + 2 * nk]
    for g in range(N_KV_HEADS):
        c = g // 2
        kk = _dup_half(k[:, c * LANES:(c + 1) * LANES], g % 2)
        kt_out[g] = kk.T.astype(BF16)
        vv_out[:, g * LANES:(g + 1) * LANES] = _dup_half(v[:, c * LANES:(c + 1) * LANES], g % 2).astype(BF16)

    @pl.when(is_ctx)
    def _():
        kv_out[:, 0:nk] = k
        kv_out[:, nk:2 * nk] = v


def _rope_tables(t_len):
    pos = jnp.arange(t_len)
    row = (pos // GRID_W).astype(F32)[:, None]
    col = (pos % GRID_W).astype(F32)[:, None]
    half = HEAD_DIM // 2
    inv = ROPE_THETA ** (-jnp.arange(0, half, 2, dtype=F32) / half)[None, :]
    ar, ac = row * inv, col * inv
    cos = jnp.concatenate([jnp.cos(ar), jnp.cos(ar), jnp.cos(ac), jnp.cos(ac)], axis=1)
    sin = jnp.concatenate([-jnp.sin(ar), jnp.sin(ar), -jnp.sin(ac), jnp.sin(ac)], axis=1)
    return jnp.tile(cos, (1, 2)), jnp.tile(sin, (1, 2))


def _qkv(xp, xs, t_len, mod, g, w, qn, kn):
    n_p, n_s = xp.shape[0], xs.shape[0]
    tm = _row_tile(n_p, t_len)
    na, nb = n_p // tm, n_s // tm
    nt_s = t_len // tm
    n = n_p + n_s
    d = D_MODEL
    nq, nk = N_HEADS * HEAD_DIM, N_KV_HEADS * HEAD_DIM
    cos, sin = _rope_tables(t_len)
    eye4 = jnp.kron(jnp.eye(4, dtype=F32), jnp.ones((HEAD_DIM, HEAD_DIM), F32)).astype(BF16)
    mod_idx = lambda i: (jnp.where(i < na, 0, 1 + (i - na) // nt_s), 0, 0)
    pos_idx = lambda i: (jnp.maximum(i - na, 0) % nt_s, 0)
    return pl.pallas_call(
        functools.partial(_qkv_kernel, na),
        grid=(na + nb,),
        in_specs=[
            pl.BlockSpec((tm, d), lambda i: (jnp.minimum(i, na - 1), 0)),
            pl.BlockSpec((tm, d), lambda i: (jnp.maximum(i - na, 0), 0)),
            pl.BlockSpec((None, 8, d), mod_idx),
            _const_spec((1, d)),
            _const_spec((d, nq + 2 * nk)),
            _const_spec((1, nq)),
            _const_spec((1, nk)),
            _const_spec((256, 256)),
            pl.BlockSpec((tm, LANES), pos_idx),
            pl.BlockSpec((tm, LANES), pos_idx),
        ],
        out_specs=[pl.BlockSpec((tm, nq), lambda i: (i, 0)),
                   pl.BlockSpec((N_KV_HEADS, LANES, tm), lambda i: (0, 0, i)),
                   pl.BlockSpec((tm, N_KV_HEADS * LANES), lambda i: (i, 0)),
                   pl.BlockSpec((tm, 2 * nk), lambda i: (jnp.minimum(i, na - 1), 0))],
        out_shape=[jax.ShapeDtypeStruct((n, nq), BF16),
                   jax.ShapeDtypeStruct((N_KV_HEADS, LANES, n), BF16),
                   jax.ShapeDtypeStruct((n, N_KV_HEADS * LANES), BF16),
                   jax.ShapeDtypeStruct((n_p, 2 * nk), F32)],
        compiler_params=_params(("arbitrary",)),
        name="qkv",
    )(xp, xs, mod, g, w, qn, kn, eye4, cos, sin)


def _attn_group(q2a, q2b, kt, vv, bias, n_bias, sinks):
    tq = q2a.shape[0]
    lane = lax.broadcasted_iota(I32, (tq, LANES), 1)
    lo = lane < HEAD_DIM
    zero = jnp.zeros_like(q2a)
    qs = jnp.concatenate([jnp.where(lo, q2a, zero), jnp.where(lo, zero, q2a),
                          jnp.where(lo, q2b, zero), jnp.where(lo, zero, q2b)], axis=0)
    s = jnp.dot(qs, kt, preferred_element_type=F32)
    ps, dens = [], []
    for h in range(4):
        sh = s[h * tq:(h + 1) * tq]
        if bias is not None:
            sh = jnp.concatenate([sh[:, :n_bias] + bias, sh[:, n_bias:]], axis=1)
        m = jnp.maximum(jnp.max(sh, axis=-1, keepdims=True), sinks[h])
        p = jnp.exp2(sh - m)
        dens.append(jnp.sum(p, axis=-1, keepdims=True) + jnp.exp2(sinks[h] - m))
        ps.append(p.astype(BF16))
    o = (jnp.dot(jnp.concatenate(ps, axis=0), vv, preferred_element_type=F32)
         / jnp.concatenate(dens, axis=0))
    oa = jnp.where(lo, o[0:tq], o[tq:2 * tq])
    ob = jnp.where(lo, o[2 * tq:3 * tq], o[3 * tq:4 * tq])
    return oa, ob


def _ctx_attn_kernel(sink_ref, q_ref, kt_ref, vv_ref, x_ref, mod_ref, wo_ref, o_ref, osc):
    for g in range(N_KV_HEADS):
        q2a = q_ref[:, (2 * g) * LANES:(2 * g + 1) * LANES]
        q2b = q_ref[:, (2 * g + 1) * LANES:(2 * g + 2) * LANES]
        sinks = [sink_ref[4 * g + j] * LOG2E for j in range(4)]
        oa, ob = _attn_group(q2a, q2b, kt_ref[g], vv_ref[:, g * LANES:(g + 1) * LANES], None, 0, sinks)
        osc[:, (2 * g) * LANES:(2 * g + 1) * LANES] = oa.astype(BF16)
        osc[:, (2 * g + 1) * LANES:(2 * g + 2) * LANES] = ob.astype(BF16)
    att = jnp.dot(osc[...], wo_ref[...], preferred_element_type=F32)
    o_ref[...] = x_ref[...] + mod_ref[2:3, :] * att


def _ctx_attention(sink, q, kt, vv, xp, seq, mod, wo):
    n_p, d = xp.shape
    nb = n_p // seq
    return pl.pallas_call(
        _ctx_attn_kernel,
        grid=(nb,),
        in_specs=[
            pl.BlockSpec(memory_space=pltpu.SMEM),
            pl.BlockSpec((seq, d), lambda b: (b, 0)),
            pl.BlockSpec((N_KV_HEADS, LANES, seq), lambda b: (0, 0, b)),
            pl.BlockSpec((seq, N_KV_HEADS * LANES), lambda b: (b, 0)),
            pl.BlockSpec((seq, d), lambda b: (b, 0)),
            pl.BlockSpec((None, 8, d), lambda b: (0, 0, 0)),
            _const_spec((d, d)),
        ],
        out_specs=pl.BlockSpec((seq, d), lambda b: (b, 0)),
        out_shape=jax.ShapeDtypeStruct((n_p, d), F32),
        scratch_shapes=[pltpu.VMEM((seq, d), BF16)],
        compiler_params=_params(("arbitrary",)),
        name="ctx_attention",
    )(sink, q, kt, vv, xp, mod, wo)


LAT_QB = 4


def _lat_attn_kernel(t_len, sink_ref, q_ref, ktp_ref, kto_ref, ktn_ref, vvp_ref, vvo_ref, vvn_ref,
                     ckt_ref, cvv_ref, x_ref, mod_ref, wo_ref, o_ref, osc):
    step = pl.program_id(1)
    tq = ATTN_BLOCK
    nloc = 3 * ATTN_BLOCK
    qi = lax.broadcasted_iota(I32, (tq, nloc), 0)
    kj = lax.broadcasted_iota(I32, (tq, nloc), 1)
    in_window = jnp.abs(qi + tq - kj) <= WINDOW
    for u in range(LAT_QB):
        n = step * LAT_QB + u
        kpos = (n - 1) * tq + kj
        bias = jnp.where(in_window & (kpos >= 0) & (kpos < t_len), 0.0, NEG_INF)
        rows = slice(u * tq, (u + 1) * tq)
        for g in range(N_KV_HEADS):
            vs = slice(g * LANES, (g + 1) * LANES)
            kts = [ktp_ref[g]] + [kto_ref[g, :, j * tq:(j + 1) * tq] for j in range(LAT_QB)] + [ktn_ref[g]]
            vvs = [vvp_ref[:, vs]] + [vvo_ref[j * tq:(j + 1) * tq, vs] for j in range(LAT_QB)] + [vvn_ref[:, vs]]
            kt = jnp.concatenate(kts[u:u + 3] + [ckt_ref[g]], axis=1)
            vv = jnp.concatenate(vvs[u:u + 3] + [cvv_ref[:, vs]], axis=0)
            q2a = q_ref[rows, (2 * g) * LANES:(2 * g + 1) * LANES]
            q2b = q_ref[rows, (2 * g + 1) * LANES:(2 * g + 2) * LANES]
            sinks = [sink_ref[4 * g + j] * LOG2E for j in range(4)]
            oa, ob = _attn_group(q2a, q2b, kt, vv, bias, nloc, sinks)
            osc[rows, (2 * g) * LANES:(2 * g + 1) * LANES] = oa.astype(BF16)
            osc[rows, (2 * g + 1) * LANES:(2 * g + 2) * LANES] = ob.astype(BF16)
    att = jnp.dot(osc[...], wo_ref[...], preferred_element_type=F32)
    o_ref[...] = x_ref[...] + mod_ref[2:3, :] * att


def _lat_attention(sink, q, kt, vv, ckt, cvv, xs, n_p, t_len, mod, wo):
    n_s, d = xs.shape
    db = n_s // t_len
    tq = ATTN_BLOCK
    ts = LAT_QB * tq
    nblk = t_len // tq
    nstep = t_len // ts
    assert n_p % ts == 0 and t_len % ts == 0
    off = n_p // tq
    npast = ckt.shape[3]
    g4 = N_KV_HEADS

    def nbr(delta):
        return lambda b, m: off + b * nblk + jnp.clip(m * LAT_QB + delta, 0, nblk - 1)

    own = lambda b, m: n_p // ts + b * nstep + m
    return pl.pallas_call(
        functools.partial(_lat_attn_kernel, t_len),
        grid=(db, nstep),
        in_specs=[
            pl.BlockSpec(memory_space=pltpu.SMEM),
            pl.BlockSpec((ts, d), lambda b, m: (own(b, m), 0)),
            pl.BlockSpec((g4, LANES, tq), lambda b, m: (0, 0, nbr(-1)(b, m))),
            pl.BlockSpec((g4, LANES, ts), lambda b, m: (0, 0, own(b, m))),
            pl.BlockSpec((g4, LANES, tq), lambda b, m: (0, 0, nbr(LAT_QB)(b, m))),
            pl.BlockSpec((tq, g4 * LANES), lambda b, m: (nbr(-1)(b, m), 0)),
            pl.BlockSpec((ts, g4 * LANES), lambda b, m: (own(b, m), 0)),
            pl.BlockSpec((tq, g4 * LANES), lambda b, m: (nbr(LAT_QB)(b, m), 0)),
            pl.BlockSpec((None, g4, LANES, npast), lambda b, m: (b, 0, 0, 0)),
            pl.BlockSpec((None, npast, g4 * LANES), lambda b, m: (b, 0, 0)),
            pl.BlockSpec((ts, d), lambda b, m: (b * nstep + m, 0)),
            pl.BlockSpec((None, 8, d), lambda b, m: (1 + b, 0, 0)),
            _const_spec((d, d)),
        ],
        out_specs=pl.BlockSpec((ts, d), lambda b, m: (b * nstep + m, 0)),
        out_shape=jax.ShapeDtypeStruct((n_s, d), F32),
        scratch_shapes=[pltpu.VMEM((ts, d), BF16)],
        compiler_params=_params(("arbitrary", "arbitrary")),
        name="lat_attention",
    )(sink, q, kt, kt, kt, vv, vv, vv, ckt, cvv, xs, mod, wo)


def _ffn_kernel(na, fc, xa_ref, xb_ref, mod_ref, g_ref, wg_ref, wu_ref, wd_ref, o_ref, acc):
    i = pl.program_id(0)
    x = jnp.where(i < na, xa_ref[...], xb_ref[...])
    a = _adaln(x, g_ref[...], mod_ref[3:4, :], mod_ref[4:5, :]).astype(BF16)
    nf = wg_ref.shape[1] // fc
    for f in range(nf):
        sl = slice(f * fc, (f + 1) * fc)
        hg = jnp.dot(a, wg_ref[:, sl], preferred_element_type=F32)
        hu = jnp.dot(a, wu_ref[:, sl], preferred_element_type=F32)
        act = (_silu(hg) * hu).astype(BF16)
        part = jnp.dot(act, wd_ref[sl, :], preferred_element_type=F32)
        if f == 0:
            acc[...] = part
        else:
            acc[...] += part
    o_ref[...] = x + mod_ref[5:6, :] * acc[...]


def _ffn(hp, hs, t_len, mod, g, wg, wu, wd):
    n_p, n_s = hp.shape[0], hs.shape[0]
    d, dff = wg.shape
    tm = _row_tile(n_p, t_len)
    na, nb = n_p // tm, n_s // tm
    nt_s = t_len // tm
    fc = 256
    assert dff % fc == 0
    mod_idx = lambda i: (jnp.where(i < na, 0, 1 + (i - na) // nt_s), 0, 0)
    return pl.pallas_call(
        functools.partial(_ffn_kernel, na, fc),
        grid=(na + nb,),
        in_specs=[
            pl.BlockSpec((tm, d), lambda i: (jnp.minimum(i, na - 1), 0)),
            pl.BlockSpec((tm, d), lambda i: (jnp.maximum(i - na, 0), 0)),
            pl.BlockSpec((None, 8, d), mod_idx),
            _const_spec((1, d)),
            _const_spec((d, dff)),
            _const_spec((d, dff)),
            _const_spec((dff, d)),
        ],
        out_specs=pl.BlockSpec((tm, d), lambda i: (i, 0)),
        out_shape=jax.ShapeDtypeStruct((n_p + n_s, d), F32),
        scratch_shapes=[pltpu.VMEM((tm, d), F32)],
        compiler_params=_params(("arbitrary",)),
        name="ffn",
    )(hp, hs, mod, g, wg, wu, wd)


def _split3(x):
    hi = x.astype(BF16)
    r1 = x - hi.astype(F32)
    mid = r1.astype(BF16)
    lo = (r1 - mid.astype(F32)).astype(BF16)
    return hi, mid, lo


def _gla_in_kernel(nt_s, x_ref, mod_ref, g_ref, w_ref, w1_ref, w2_ref, gb_ref, tri_ref,
                   qk_out, v_out, r_out, b_out):
    x = x_ref[...]
    tm = x.shape[0]
    a = _adaln(x, g_ref[...], mod_ref[0:1, :], mod_ref[1:2, :]).astype(BF16)
    hk = GLA_HEADS * GLA_DK
    hv = GLA_HEADS * GLA_DV
    q = jnp.dot(a, w_ref[:, 0:hk], preferred_element_type=F32)
    qk_out[:, 0:hk] = q * (GLA_DK ** -0.5)
    qk_out[:, hk:2 * hk] = jnp.dot(a, w_ref[:, hk:2 * hk], preferred_element_type=F32)
    v_out[...] = jnp.dot(a, w_ref[:, 2 * hk:2 * hk + hv], preferred_element_type=F32)
    r_out[...] = jnp.dot(a, w_ref[:, 2 * hk + hv:2 * hk + 2 * hv], preferred_element_type=F32).astype(BF16)
    z1 = jnp.dot(a, w1_ref[...], preferred_element_type=F32)
    z1h = z1.astype(BF16)
    z1l = (z1 - z1h.astype(F32)).astype(BF16)
    z = (jnp.dot(z1h, w2_ref[...], preferred_element_type=F32)
         + jnp.dot(z1l, w2_ref[...], preferred_element_type=F32) + gb_ref[...])
    gate = (jnp.minimum(z, 0.0) - jnp.log(1.0 + jnp.exp(-jnp.abs(z)))) * (1.0 / GLA_GATE_TAU)
    c = GLA_CHUNK
    tri = tri_ref[...]
    for j in range(tm // c):
        rows = slice(j * c, (j + 1) * c)
        for dr in range(2):
            cols = slice(dr * hk, (dr + 1) * hk)
            parts = _split3(gate[rows, cols])
            acc = jnp.dot(tri[dr], parts[0], preferred_element_type=F32)
            acc += jnp.dot(tri[dr], parts[1], preferred_element_type=F32)
            acc += jnp.dot(tri[dr], parts[2], preferred_element_type=F32)
            b_out[rows, cols] = acc


def _gla_in(h, n_p, t_len, mod, g, w_in, w1, w2, gate_b):
    n, d = h.shape
    tm = _row_tile(n_p, t_len)
    na = n_p // tm
    nt_s = t_len // tm
    hk, hv = GLA_HEADS * GLA_DK, GLA_HEADS * GLA_DV
    c = GLA_CHUNK
    lower = jnp.tril(jnp.ones((c, c), F32))
    tri = jnp.stack([lower, lower.T]).astype(BF16)
    mod_idx = lambda i: (jnp.where(i < na, 0, 1 + (i - na) // nt_s), 0, 0)
    row = lambda i: (i, 0)
    return pl.pallas_call(
        functools.partial(_gla_in_kernel, nt_s),
        grid=(n // tm,),
        in_specs=[
            pl.BlockSpec((tm, d), row),
            pl.BlockSpec((None, 8, d), mod_idx),
            _const_spec((1, d)),
            _const_spec(w_in.shape),
            _const_spec(w1.shape),
            _const_spec(w2.shape),
            _const_spec((1, 2 * hk)),
            _const_spec((2, c, c)),
        ],
        out_specs=[pl.BlockSpec((tm, 2 * hk), row), pl.BlockSpec((tm, hv), row),
                   pl.BlockSpec((tm, hv), row), pl.BlockSpec((tm, 2 * hk), row)],
        out_shape=[jax.ShapeDtypeStruct((n, 2 * hk), F32), jax.ShapeDtypeStruct((n, hv), F32),
                   jax.ShapeDtypeStruct((n, hv), BF16), jax.ShapeDtypeStruct((n, 2 * hk), F32)],
        compiler_params=_params(("arbitrary",)),
        name="gla_in",
    )(h, mod, g, w_in, w1, w2, gate_b, tri)


def _gla_chunk(q, k, v, b, st_ref, forward):
    c = q.shape[0]
    if forward:
        b_end, b_mid = b[c - 1:c, :], b[c // 2 - 1:c // 2, :]
    else:
        b_end, b_mid = b[0:1, :], b[c // 2:c // 2 + 1, :]
    e_up = jnp.exp(b - b_mid)
    e_dn = jnp.exp(b_mid - b)
    qa = q * e_up
    ka = k * e_dn
    qd = (qa * jnp.exp(b_mid)).astype(BF16)
    kd = (ka * jnp.exp(b_end - b_mid)).astype(BF16)
    att = lax.dot_general(qa.astype(BF16), ka.astype(BF16), (((1,), (1,)), ((), ())),
                          preferred_element_type=F32)
    ri = lax.broadcasted_iota(I32, (c, c), 0)
    ci = lax.broadcasted_iota(I32, (c, c), 1)
    keep = (ci <= ri) if forward else (ci >= ri)
    att = jnp.where(keep, att, 0.0).astype(BF16)
    vb = v.astype(BF16)
    st = st_ref[...]
    o = lax.dot_general(qd, st.astype(BF16), (((1,), (1,)), ((), ())), preferred_element_type=F32)
    o = o + jnp.dot(att, vb, preferred_element_type=F32)
    upd = lax.dot_general(vb, kd, (((0,), (0,)), ((), ())), preferred_element_type=F32)
    st_ref[...] = st * jnp.exp(b_end) + upd
    return o


def _gla_scan_kernel(qkf_ref, vf_ref, bf_ref, qkb_ref, vb_ref, bb_ref, sf0_ref, sb0_ref,
                     of_ref, ob_ref, sf_ref, sb_ref, stf, stb):
    i = pl.program_id(1)
    nblk = pl.num_programs(1)
    c = GLA_CHUNK
    ncb = qkf_ref.shape[0] // c
    hk = GLA_HEADS * GLA_DK
    dk, dv = GLA_DK, GLA_DV

    @pl.when(i == 0)
    def _():
        for h in range(GLA_HEADS):
            stf[h] = sf0_ref[h].T
            stb[h] = sb0_ref[h].T

    def step(ci, carry):
        rf = pl.ds(pl.multiple_of(ci * c, c), c)
        rb = pl.ds(pl.multiple_of((ncb - 1 - ci) * c, c), c)
        for h in range(GLA_HEADS):
            qs = slice(h * dk, (h + 1) * dk)
            ks = slice(hk + h * dk, hk + (h + 1) * dk)
            vs = slice(h * dv, (h + 1) * dv)
            of = _gla_chunk(qkf_ref[rf, qs], qkf_ref[rf, ks], vf_ref[rf, vs], bf_ref[rf, qs],
                            stf.at[h], True)
            of_ref[rf, vs] = of.astype(of_ref.dtype)
            ob = _gla_chunk(qkb_ref[rb, qs], qkb_ref[rb, ks], vb_ref[rb, vs], bb_ref[rb, qs],
                            stb.at[h], False)
            ob_ref[rb, vs] = ob.astype(ob_ref.dtype)
        return carry

    lax.fori_loop(0, ncb, step, 0)

    @pl.when(i == nblk - 1)
    def _():
        for h in range(GLA_HEADS):
            sf_ref[h] = stf[h].T
            sb_ref[h] = stb[h].T


def _gla_scan(qk, v, bc, sf0, sb0, b_off, nb, t_len):
    n = qk.shape[0]
    assert n % t_len == 0
    ns = n // t_len
    tb = _row_tile(t_len)
    nblk = t_len // tb
    qk3 = qk.reshape(ns, t_len, qk.shape[1])
    v3 = v.reshape(ns, t_len, v.shape[1])
    bc3 = bc.reshape(ns, t_len, bc.shape[1])
    h, dk, dv = GLA_HEADS, GLA_DK, GLA_DV
    hk = h * dk
    st_spec = pl.BlockSpec((None, h, dk, dv), lambda b, i: (b, 0, 0, 0))
    fwd = lambda b, i: (b_off + b, i, 0)
    bwd = lambda b, i: (b_off + b, nblk - 1 - i, 0)
    return pl.pallas_call(
        _gla_scan_kernel,
        grid=(nb, nblk),
        in_specs=[
            pl.BlockSpec((None, tb, 2 * hk), fwd),
            pl.BlockSpec((None, tb, h * dv), fwd),
            pl.BlockSpec((None, tb, hk), fwd),
            pl.BlockSpec((None, tb, 2 * hk), bwd),
            pl.BlockSpec((None, tb, h * dv), bwd),
            pl.BlockSpec((None, tb, hk), lambda b, i: (b_off + b, nblk - 1 - i, 1)),
            st_spec, st_spec,
        ],
        out_specs=[pl.BlockSpec((None, tb, h * dv), lambda b, i: (b, i, 0)),
                   pl.BlockSpec((None, tb, h * dv), lambda b, i: (b, nblk - 1 - i, 0)),
                   st_spec, st_spec],
        out_shape=[jax.ShapeDtypeStruct((nb, t_len, h * dv), BF16),
                   jax.ShapeDtypeStruct((nb, t_len, h * dv), BF16),
                   jax.ShapeDtypeStruct((nb, h, dk, dv), F32),
                   jax.ShapeDtypeStruct((nb, h, dk, dv), F32)],
        scratch_shapes=[pltpu.VMEM((h, dv, dk), F32), pltpu.VMEM((h, dv, dk), F32)],
        compiler_params=_params(("arbitrary", "arbitrary")),
        name="gla_scan",
    )(qk3, v3, bc3, qk3, v3, bc3, sf0, sb0)


def _gla_out_kernel(na, ofa_ref, oba_ref, ofb_ref, obb_ref, r_ref, h_ref, mod_ref, on_ref, wo_ref,
                    g2_ref, rw_ref, tri_ref, h_out, f_out, route_out, wt_out, cnt_out, run):
    i = pl.program_id(0)
    is_ctx = i < na
    o = (jnp.where(is_ctx, ofa_ref[...], ofb_ref[...]).astype(F32)
         + jnp.where(is_ctx, oba_ref[...], obb_ref[...]).astype(F32))
    tm = o.shape[0]
    dv = GLA_DV
    parts = []
    for hh in range(GLA_HEADS):
        oh = o[:, hh * dv:(hh + 1) * dv]
        parts.append(_rms(oh) * on_ref[...])
    on = jnp.concatenate(parts, axis=1)
    r = r_ref[...].astype(F32)
    gated = (on * _silu(r)).astype(BF16)
    h = h_ref[...] + mod_ref[2:3, :] * jnp.dot(gated, wo_ref[...], preferred_element_type=F32)
    h_out[...] = h
    f = _adaln(h, g2_ref[...], mod_ref[3:4, :], mod_ref[4:5, :])
    f_out[...] = _pack_halves(f)
    fh = f.astype(BF16)
    fl = (f - fh.astype(F32)).astype(BF16)
    logits = (jnp.dot(fh, rw_ref[0], preferred_element_type=F32)
              + jnp.dot(fl, rw_ref[0], preferred_element_type=F32)
              + jnp.dot(fh, rw_ref[1], preferred_element_type=F32))
    lane = lax.broadcasted_iota(I32, (tm, LANES), 1)
    lane_f = lane.astype(F32)
    logits = jnp.where(lane < N_EXPERTS, logits, -jnp.inf)
    m1 = jnp.max(logits, axis=-1, keepdims=True)
    i1 = jnp.min(jnp.where(logits == m1, lane_f, float(LANES)), axis=-1, keepdims=True)
    rest = jnp.where(lane_f == i1, -jnp.inf, logits)
    m2 = jnp.max(rest, axis=-1, keepdims=True)
    i2 = jnp.min(jnp.where(rest == m2, lane_f, float(LANES)), axis=-1, keepdims=True)
    e2 = jnp.exp(m2 - m1)
    w1 = 1.0 / (1.0 + e2)
    w2 = e2 / (1.0 + e2)
    wt_out[...] = jnp.where(lane == 0, w1, jnp.where(lane == 1, w2, 0.0))
    sel1 = lane_f == i1
    sel2 = lane_f == i2
    picked = jnp.where(sel1, 1.0, jnp.where(sel2, 1.0, 0.0))

    @pl.when(i == 0)
    def _():
        run[...] = jnp.zeros_like(run)

    before = jnp.dot(tri_ref[...], picked.astype(BF16), preferred_element_type=F32) + run[0:1, :]
    rank1 = jnp.sum(jnp.where(sel1, before, 0.0), axis=-1, keepdims=True)
    rank2 = jnp.sum(jnp.where(sel2, before, 0.0), axis=-1, keepdims=True)
    run[...] = run[...] + jnp.sum(picked, axis=0, keepdims=True)
    cnt_out[...] = run[...]
    routing = jnp.where(lane == 0, i1, jnp.where(lane == 1, i2,
                        jnp.where(lane == 2, rank1, jnp.where(lane == 3, rank2, 0.0))))
    route_out[...] = routing.T[0:8, :]


def _gla_out(of_p, ob_p, of_s, ob_s, r, h, n_p, t_len, mod, out_norm, wo, g2, router_w):
    n, d = h.shape
    tm = _row_tile(n_p, t_len)
    na = n_p // tm
    nt_s = t_len // tm
    mod_idx = lambda i: (jnp.where(i < na, 0, 1 + (i - na) // nt_s), 0, 0)
    row = lambda i: (i, 0)
    ctx_row = lambda i: (jnp.minimum(i, na - 1), 0)
    lat_row = lambda i: (jnp.maximum(i - na, 0), 0)
    tri = jnp.tril(jnp.ones((tm, tm), F32), -1).astype(BF16)
    return pl.pallas_call(
        functools.partial(_gla_out_kernel, na),
        grid=(n // tm,),
        in_specs=[
            pl.BlockSpec((tm, d), ctx_row),
            pl.BlockSpec((tm, d), ctx_row),
            pl.BlockSpec((tm, d), lat_row),
            pl.BlockSpec((tm, d), lat_row),
            pl.BlockSpec((tm, d), row),
            pl.BlockSpec((tm, d), row),
            pl.BlockSpec((None, 8, d), mod_idx),
            _const_spec((1, GLA_DV)),
            _const_spec((d, d)),
            _const_spec((1, d)),
            _const_spec((2, d, LANES)),
            _const_spec((tm, tm)),
        ],
        out_specs=[pl.BlockSpec((tm, d), row), pl.BlockSpec((tm, d // 2), row),
                   pl.BlockSpec((8, tm), lambda i: (0, i)), pl.BlockSpec((tm, LANES), row),
                   _const_spec((8, LANES))],
        out_shape=[jax.ShapeDtypeStruct((n, d), F32), jax.ShapeDtypeStruct((n, d // 2), jnp.uint32),
                   jax.ShapeDtypeStruct((8, n), F32), jax.ShapeDtypeStruct((n, LANES), F32),
                   jax.ShapeDtypeStruct((8, LANES), F32)],
        scratch_shapes=[pltpu.VMEM((8, LANES), F32)],
        compiler_params=_params(("arbitrary",)),
        name="gla_out_router",
    )(of_p, ob_p, of_s, ob_s, r, h, mod, out_norm, wo, g2, router_w, tri)


SC_CORES = 2
SC_SUBCORES = 16
SC_CHUNK = 32


def _sc_gather_rows(table, idx):
    nw = SC_CORES * SC_SUBCORES
    b = idx.shape[0]
    d = table.shape[1]
    assert b % (nw * SC_CHUNK) == 0
    per_w = b // nw
    n_chunks = per_w // SC_CHUNK
    mesh = plsc.VectorSubcoreMesh(core_axis_name="c", subcore_axis_name="s",
                                  num_cores=SC_CORES, num_subcores=SC_SUBCORES)

    def body(table_hbm, idx_hbm, out_hbm, idx_v, rows_v, sem):
        wid = lax.axis_index("s") * SC_CORES + lax.axis_index("c")
        base = wid * per_w
        pltpu.sync_copy(idx_hbm.at[wid], idx_v)

        @pl.loop(0, n_chunks)
        def _(j):
            pltpu.async_copy(table_hbm.at[idx_v.at[j]], rows_v, sem).wait()
            pltpu.sync_copy(rows_v, out_hbm.at[pl.ds(base + j * SC_CHUNK, SC_CHUNK)])

    return pl.kernel(
        body,
        out_type=jax.ShapeDtypeStruct((b, d), table.dtype),
        mesh=mesh,
        scratch_types=[pltpu.VMEM((n_chunks, SC_CHUNK), I32),
                       pltpu.VMEM((SC_CHUNK, d), table.dtype),
                       pltpu.SemaphoreType.DMA],
        name="sc_gather_rows",
    )(table, idx.reshape(nw, n_chunks, SC_CHUNK))


def _sc_scatter_rows(rows, idx2, p):
    nw = SC_CORES * SC_SUBCORES
    n, d = rows.shape
    assert n % (nw * SC_CHUNK) == 0
    per_w = n // nw
    n_chunks = per_w // SC_CHUNK
    mesh = plsc.VectorSubcoreMesh(core_axis_name="c", subcore_axis_name="s",
                                  num_cores=SC_CORES, num_subcores=SC_SUBCORES)

    def body(rows_hbm, idx0_hbm, idx1_hbm, out_hbm, idx0_v, idx1_v, rows_v):
        wid = lax.axis_index("s") * SC_CORES + lax.axis_index("c")
        base = wid * per_w
        pltpu.sync_copy(idx0_hbm.at[wid], idx0_v)
        pltpu.sync_copy(idx1_hbm.at[wid], idx1_v)

        @pl.loop(0, n_chunks)
        def _(j):
            pltpu.sync_copy(rows_hbm.at[pl.ds(base + j * SC_CHUNK, SC_CHUNK)], rows_v)
            pltpu.sync_copy(rows_v, out_hbm.at[idx0_v.at[j]])
            pltpu.sync_copy(rows_v, out_hbm.at[idx1_v.at[j]])

    return pl.kernel(
        body,
        out_type=jax.ShapeDtypeStruct((p, d), rows.dtype),
        mesh=mesh,
        scratch_types=[pltpu.VMEM((n_chunks, SC_CHUNK), I32),
                       pltpu.VMEM((n_chunks, SC_CHUNK), I32),
                       pltpu.VMEM((SC_CHUNK, d), rows.dtype)],
        name="sc_scatter_rows",
    )(rows, idx2[0].reshape(nw, n_chunks, SC_CHUNK), idx2[1].reshape(nw, n_chunks, SC_CHUNK))


EXPERT_SUB = 512


def _pack_halves(x):
    k = x.shape[1] // 2
    lo = lax.bitcast_convert_type(x[:, :k].astype(BF16).astype(F32), jnp.uint32)
    hi = lax.bitcast_convert_type(x[:, k:].astype(BF16).astype(F32), jnp.uint32)
    return (lo >> 16) | (hi & jnp.uint32(0xFFFF0000))


def _unpack_halves(w):
    lo = lax.bitcast_convert_type(w << 16, F32).astype(BF16)
    hi = lax.bitcast_convert_type(w & jnp.uint32(0xFFFF0000), F32).astype(BF16)
    return lo, hi


def _expert_kernel(te_ref, ts_ref, nu_ref, nv_ref, x_ref, wg_ref, wu_ref, wd_ref, y_ref, acc, xb):
    i = pl.program_id(0)
    f = pl.program_id(1)
    nf = pl.num_programs(1)
    tm = x_ref.shape[0]
    half = x_ref.shape[1]
    sub = EXPERT_SUB
    nvalid = nv_ref[i]

    @pl.when(f == 0)
    def _():
        acc[...] = jnp.zeros_like(acc)

    @pl.when((f == 0) & (nvalid > 0))
    def _():
        rid = lax.broadcasted_iota(I32, (tm, half), 0)
        lo, hi = _unpack_halves(x_ref[...])
        zero = jnp.zeros_like(lo)
        xb[:, 0:half] = jnp.where(rid < nvalid, lo, zero)
        xb[:, half:2 * half] = jnp.where(rid < nvalid, hi, zero)

    def compute(rows):
        x = xb[rows, :]
        hg = jnp.dot(x, wg_ref[...].astype(BF16), preferred_element_type=F32)
        hu = jnp.dot(x, wu_ref[...].astype(BF16), preferred_element_type=F32)
        act = (_silu(hg) * hu).astype(BF16)
        acc[rows, :] += jnp.dot(act, wd_ref[...].astype(BF16), preferred_element_type=F32)

    @pl.when(nvalid > sub)
    def _():
        compute(slice(0, tm))

    @pl.when((nvalid > 0) & (nvalid <= sub))
    def _():
        compute(slice(0, sub))

    @pl.when(f == nf - 1)
    def _():
        y_ref[...] = _pack_halves(acc[...])


def _experts(x, tile_expert, tile_src, n_used, n_valid, wg, wu, wd, tm, tf):
    p, half = x.shape
    d = 2 * half
    ne, _, dff = wg.shape
    nf = dff // tf
    assert dff % tf == 0 and tm % EXPERT_SUB == 0

    def fidx(i, f, nu):
        return jnp.where(i < nu[0], f, nf - 1)

    grid_spec = pltpu.PrefetchScalarGridSpec(
        num_scalar_prefetch=4,
        grid=(p // tm, nf),
        in_specs=[
            pl.BlockSpec((tm, half), lambda i, f, te, ts, nu, nv: (ts[i], 0)),
            pl.BlockSpec((None, d, tf), lambda i, f, te, ts, nu, nv: (te[i], 0, fidx(i, f, nu))),
            pl.BlockSpec((None, d, tf), lambda i, f, te, ts, nu, nv: (te[i], 0, fidx(i, f, nu))),
            pl.BlockSpec((None, tf, d), lambda i, f, te, ts, nu, nv: (te[i], fidx(i, f, nu), 0)),
        ],
        out_specs=pl.BlockSpec((tm, half), lambda i, f, te, ts, nu, nv: (i, 0)),
        scratch_shapes=[pltpu.VMEM((tm, d), F32), pltpu.VMEM((tm, d), BF16)],
    )
    return pl.pallas_call(
        _expert_kernel,
        grid_spec=grid_spec,
        out_shape=jax.ShapeDtypeStruct((p, half), jnp.uint32),
        compiler_params=_params(("arbitrary", "arbitrary"), 60 * 1024 * 1024),
        name="moe_experts",
    )(tile_expert, tile_src, n_used, n_valid, x, wg, wu, wd)


def _combine_kernel(wt_ref, h_ref, mod_ref, y0_ref, y1_ref, o_ref):
    wt = wt_ref[...]
    half = y0_ref.shape[1]
    lo0, hi0 = _unpack_halves(y0_ref[...])
    lo1, hi1 = _unpack_halves(y1_ref[...])
    w0, w1 = wt[:, 0:1], wt[:, 1:2]
    h = h_ref[...]
    gate = mod_ref[5:6, :]
    o_ref[:, 0:half] = h[:, 0:half] + gate[:, 0:half] * (w0 * lo0.astype(F32) + w1 * lo1.astype(F32))
    o_ref[:, half:] = h[:, half:] + gate[:, half:] * (w0 * hi0.astype(F32) + w1 * hi1.astype(F32))


def _combine(yg, wt, h, mod, row_off, n_rows, n_mod, mod_off, tm):
    d = h.shape[1]
    nt = n_rows // tm
    toff = row_off // tm
    per_mod = n_rows // n_mod // tm
    return pl.pallas_call(
        _combine_kernel,
        grid=(nt,),
        in_specs=[
            pl.BlockSpec((tm, LANES), lambda i: (toff + i, 0)),
            pl.BlockSpec((tm, d), lambda i: (toff + i, 0)),
            pl.BlockSpec((None, 8, d), lambda i: (mod_off + i // per_mod, 0, 0)),
            pl.BlockSpec((None, tm, d // 2), lambda i: (0, toff + i, 0)),
            pl.BlockSpec((None, tm, d // 2), lambda i: (1, toff + i, 0)),
        ],
        out_specs=pl.BlockSpec((tm, d), lambda i: (i, 0)),
        out_shape=jax.ShapeDtypeStruct((n_rows, d), F32),
        compiler_params=_params(("arbitrary",)),
        name="moe_combine",
    )(wt, h, mod, yg, yg)


def _route(route, counts, n, tm):
    cnt = counts[0, :N_EXPERTS].astype(I32)
    padded = ((cnt + tm - 1) // tm) * tm
    gend = jnp.cumsum(padded)
    goff = gend - padded
    e = route[0:2].astype(I32)
    rank = route[2:4].astype(I32)
    onehot = e[:, :, None] == jnp.arange(N_EXPERTS, dtype=I32)[None, None, :]
    dest = jnp.sum(jnp.where(onehot, goff[None, None, :], 0), axis=-1) + rank
    p = 2 * n + N_EXPERTS * tm
    n_used = gend[-1] // tm
    tiles = jnp.arange(p // tm, dtype=I32)
    tile_src = jnp.minimum(tiles, n_used - 1)
    tile_expert = jnp.minimum(jnp.sum((gend[None, :] <= (tile_src * tm)[:, None]).astype(I32), axis=1),
                              N_EXPERTS - 1)
    used = goff[tile_expert] + cnt[tile_expert]
    n_valid = jnp.where(tiles < n_used, jnp.clip(used - tiles * tm, 0, tm), 0).astype(I32)
    return dest, p, tile_expert, tile_src, n_used.reshape(1).astype(I32), n_valid


def kernel(x_prompt, x_sample, cache_k, cache_v, state_fwd, state_bwd, c, c_ctx, ada_w, ada_b, norm1_g, norm2_g, attn_w_qkv, attn_q_norm, attn_k_norm, attn_sink, attn_w_o, gla_w_in, gla_gate_w1, gla_gate_w2, gla_gate_b, gla_out_norm, gla_w_o, ffn_w_gate, ffn_w_up, ffn_w_down, moe_router, moe_w_gate, moe_w_up, moe_w_down):
    bp, seq, d = x_prompt.shape
    db, t_len, _ = x_sample.shape
    n_p, n_s = bp * seq, db * t_len
    n = n_p + n_s
    xp = x_prompt.reshape(n_p, d)
    xs = x_sample.reshape(n_s, d)

    cond = jnp.concatenate([c_ctx[None, :], c], axis=0)
    assert cond.shape[0] <= 8
    cond_t = jnp.pad(cond, ((0, 8 - cond.shape[0]), (0, 0))).T
    mods = _modulation(cond_t, cond.shape[0], ada_w, ada_b)

    nk = N_KV_HEADS * HEAD_DIM
    qn = jnp.tile(attn_q_norm[0], N_HEADS)[None, :]
    kn = jnp.tile(attn_k_norm[0], N_KV_HEADS)[None, :]
    q, kt, vv, kv = _qkv(xp, xs, t_len, mods[0], norm1_g[0][None, :], attn_w_qkv[0].astype(BF16), qn, kn)
    wo0 = attn_w_o[0].astype(BF16)
    sink = attn_sink[0]
    hp = _ctx_attention(sink, q, kt, vv, xp, seq, mods[0], wo0)
    ck = cache_k[:, 0].astype(BF16)
    cv = cache_v[:, 0].astype(BF16)
    ckt = jnp.concatenate([ck, ck], axis=-1).transpose(0, 2, 3, 1)
    cvv = jnp.concatenate([cv, cv], axis=-1).reshape(db, cv.shape[1], N_KV_HEADS * LANES)
    hs = _lat_attention(sink, q, kt, vv, ckt, cvv, xs, n_p, t_len, mods[0], wo0)
    new_k = kv[:n_p, :nk].reshape(bp, 1, seq, N_KV_HEADS, HEAD_DIM)
    new_v = kv[:n_p, nk:].reshape(bp, 1, seq, N_KV_HEADS, HEAD_DIM)
    h = _ffn(hp, hs, t_len, mods[0], norm2_g[0][None, :], ffn_w_gate[0].astype(BF16),
             ffn_w_up[0].astype(BF16), ffn_w_down[0].astype(BF16))

    hk = GLA_HEADS * GLA_DK
    rank = GLA_GATE_RANK
    w1 = jnp.concatenate([gla_gate_w1[0, 0], gla_gate_w1[0, 1]], axis=1)
    w1 = jnp.pad(w1, ((0, 0), (0, LANES - 2 * rank))).astype(BF16)
    w2 = jnp.zeros((LANES, 2 * hk), F32)
    w2 = w2.at[0:rank, 0:hk].set(gla_gate_w2[0, 0]).at[rank:2 * rank, hk:].set(gla_gate_w2[0, 1]).astype(BF16)
    gate_b = gla_gate_b[0].reshape(1, 2 * hk)
    qk, v, r, bc = _gla_in(h, n_p, t_len, mods[1], norm1_g[1][None, :], gla_w_in[0].astype(BF16),
                           w1, w2, gate_b)
    zeros = jnp.zeros((bp, GLA_HEADS, GLA_DK, GLA_DV), F32)
    of_p, ob_p, new_sf, new_sb = _gla_scan(qk, v, bc, zeros, zeros, 0, bp, seq)
    of_s, ob_s, _, _ = _gla_scan(qk, v, bc, state_fwd[:, 0], state_bwd[:, 0], n_p // t_len, db, t_len)
    rw = jnp.pad(moe_router[0], ((0, 0), (0, LANES - N_EXPERTS)))
    rw_hi = rw.astype(BF16)
    rw_lo = (rw - rw_hi.astype(F32)).astype(BF16)
    h, f, route, wt, counts = _gla_out(
        of_p.reshape(n_p, d), ob_p.reshape(n_p, d), of_s.reshape(n_s, d), ob_s.reshape(n_s, d), r, h,
        n_p, t_len, mods[1], gla_out_norm[0][None, :], gla_w_o[0].astype(BF16), norm2_g[1][None, :],
        jnp.stack([rw_hi, rw_lo]))

    tm_e = 1024
    dest_k, p_rows, tile_expert, tile_src, n_used, n_valid = _route(route, counts, n, tm_e)
    xg = _sc_scatter_rows(f, dest_k, p_rows)
    y = _experts(xg, tile_expert, tile_src, n_used, n_valid, moe_w_gate[0], moe_w_up[0],
                 moe_w_down[0], tm_e, 512)
    yg = _sc_gather_rows(y, dest_k.reshape(2 * n)).reshape(2, n, d // 2)
    tm_c = _row_tile(n_p, t_len)
    y_p = _combine(yg, wt, h, mods[1], 0, n_p, 1, 0, tm_c)
    y_s = _combine(yg, wt, h, mods[1], n_p, n_s, db, 1, tm_c)

    return (y_p.reshape(bp, seq, d), y_s.reshape(db, t_len, d), new_k, new_v,
            new_sf[:, None], new_sb[:, None])
```

```python
import functools
import math

import jax
import jax.numpy as jnp
from jax import lax
from jax.experimental import pallas as pl
from jax.experimental.pallas import tpu as pltpu
from jax.experimental.pallas import tpu_sc as plsc

F32 = jnp.float32
BF16 = jnp.bfloat16
I32 = jnp.int32

D_MODEL = 1024
N_HEADS = 16
N_KV_HEADS = 4
HEAD_DIM = 64
GRID_W = 64
WINDOW = 128
ATTN_BLOCK = 128
ROPE_THETA = 10000.0
GLA_HEADS = 4
GLA_DK = 128
GLA_DV = 256
GLA_GATE_RANK = 16
GLA_GATE_TAU = 16.0
GLA_CHUNK = 64
N_EXPERTS = 8
NORM_EPS = 1e-6
NEG_INF = -1e30

LANES = 128
VMEM_LIMIT = 56 * 1024 * 1024


def _params(sem, vmem=VMEM_LIMIT):
    return pltpu.CompilerParams(dimension_semantics=sem, vmem_limit_bytes=vmem)


def _row_tile(*counts, cap=512):
    t = cap
    while any(c % t for c in counts):
        t //= 2
    assert t >= 8
    return t


def _rms(x):
    return x * lax.rsqrt(jnp.mean(x * x, axis=-1, keepdims=True) + NORM_EPS)


def _adaln(x, g, shift, scale):
    return _rms(x) * (g * (1.0 + scale)) + shift


def _silu(x):
    return x * jax.nn.sigmoid(x)


def _const_spec(shape):
    nd = len(shape)
    return pl.BlockSpec(shape, lambda *_: (0,) * nd)


def _mod_kernel(n_cond, ct_ref, w_ref, b_ref, o_ref):
    ct = ct_ref[...]
    s = _silu(ct)
    w = w_ref[...]
    rows = [jnp.sum(w * s[:, r:r + 1], axis=0, keepdims=True) for r in range(n_cond)]
    rows += [jnp.zeros_like(rows[0])] * (8 - n_cond)
    o_ref[...] = jnp.concatenate(rows, axis=0) + b_ref[...]


def _modulation(cond_t, n_cond, ada_w, ada_b):
    depth, d, n6 = ada_w.shape
    tn = 1024
    out = pl.pallas_call(
        functools.partial(_mod_kernel, n_cond),
        grid=(depth, n6 // tn),
        in_specs=[
            pl.BlockSpec((d, 8), lambda l, j: (0, 0)),
            pl.BlockSpec((None, d, tn), lambda l, j: (l, 0, j)),
            pl.BlockSpec((None, 1, tn), lambda l, j: (l, 0, j)),
        ],
        out_specs=pl.BlockSpec((None, 8, tn), lambda l, j: (l, 0, j)),
        out_shape=jax.ShapeDtypeStruct((depth, 8, n6), F32),
        compiler_params=_params(("arbitrary", "arbitrary")),
        name="modulation",
    )(cond_t, ada_w, ada_b.reshape(depth, 1, n6))
    m = out[:, :n_cond].reshape(depth, n_cond, 6, d)
    return jnp.pad(m, ((0, 0), (0, 0), (0, 2), (0, 0)))


LOG2E = math.log2(math.e)


def _dup_half(k2, half):
    lane = lax.broadcasted_iota(I32, k2.shape, 1)
    lo = lane < HEAD_DIM
    r = pltpu.roll(k2, HEAD_DIM, 1)
    return jnp.where(lo, k2, r) if half == 0 else jnp.where(lo, r, k2)


def _qkv_kernel(na, xa_ref, xb_ref, mod_ref, g_ref, w_ref, qn_ref, kn_ref, bd_ref,
                cos_ref, sin_ref, q_out, kt_out, vv_out, kv_out):
    i = pl.program_id(0)
    is_ctx = i < na
    x = jnp.where(is_ctx, xa_ref[...], xb_ref[...])
    a = _adaln(x, g_ref[...], mod_ref[0:1, :], mod_ref[1:2, :])
    y = jnp.dot(a.astype(BF16), w_ref[...], preferred_element_type=F32)
    tm = x.shape[0]
    cos = jnp.where(is_ctx, 1.0, cos_ref[...])
    sin = jnp.where(is_ctx, 0.0, sin_ref[...])
    lane = lax.broadcasted_iota(I32, (tm, LANES), 1)
    first16 = (lane % 32) < 16

    def norm_rope(z, wt):
        ss = jnp.dot((z * z).astype(BF16), bd_ref[...], preferred_element_type=F32)
        zn = z * lax.rsqrt(ss * (1.0 / HEAD_DIM) + NORM_EPS) * wt
        outs = []
        for c in range(2):
            t = zn[:, c * LANES:(c + 1) * LANES]
            partner = jnp.where(first16, pltpu.roll(t, LANES - 16, 1), pltpu.roll(t, 16, 1))
            outs.append(t * cos + partner * sin)
        return jnp.concatenate(outs, axis=1)

    nq = N_HEADS * HEAD_DIM
    nk = N_KV_HEADS * HEAD_DIM
    scale = HEAD_DIM ** -0.5 * LOG2E
    for s in range(nq // 256):
        sl = slice(s * 256, (s + 1) * 256)
        q_out[:, sl] = (norm_rope(y[:, sl], qn_ref[:, sl]) * scale).astype(BF16)
    k = norm_rope(y[:, nq:nq + nk], kn_ref[...])
    v = y[:, nq + nk:nq + 2 * nk]
    for g in range(N_KV_HEADS):
        c = g // 2
        kk = _dup_half(k[:, c * LANES:(c + 1) * LANES], g % 2)
        kt_out[g] = kk.T.astype(BF16)
        vv_out[:, g * LANES:(g + 1) * LANES] = _dup_half(v[:, c * LANES:(c + 1) * LANES], g % 2).astype(BF16)

    @pl.when(is_ctx)
    def _():
        kv_out[:, 0:nk] = k
        kv_out[:, nk:2 * nk] = v


def _rope_tables(t_len):
    pos = jnp.arange(t_len)
    row = (pos // GRID_W).astype(F32)[:, None]
    col = (pos % GRID_W).astype(F32)[:, None]
    half = HEAD_DIM // 2
    inv = ROPE_THETA ** (-jnp.arange(0, half, 2, dtype=F32) / half)[None, :]
    ar, ac = row * inv, col * inv
    cos = jnp.concatenate([jnp.cos(ar), jnp.cos(ar), jnp.cos(ac), jnp.cos(ac)], axis=1)
    sin = jnp.concatenate([-jnp.sin(ar), jnp.sin(ar), -jnp.sin(ac), jnp.sin(ac)], axis=1)
    return jnp.tile(cos, (1, 2)), jnp.tile(sin, (1, 2))


def _qkv(xp, xs, t_len, mod, g, w, qn, kn):
    n_p, n_s = xp.shape[0], xs.shape[0]
    tm = _row_tile(n_p, t_len)
    na, nb = n_p // tm, n_s // tm
    nt_s = t_len // tm
    n = n_p + n_s
    d = D_MODEL
    nq, nk = N_HEADS * HEAD_DIM, N_KV_HEADS * HEAD_DIM
    cos, sin = _rope_tables(t_len)
    eye4 = jnp.kron(jnp.eye(4, dtype=F32), jnp.ones((HEAD_DIM, HEAD_DIM), F32)).astype(BF16)
    mod_idx = lambda i: (jnp.where(i < na, 0, 1 + (i - na) // nt_s), 0, 0)
    pos_idx = lambda i: (jnp.maximum(i - na, 0) % nt_s, 0)
    return pl.pallas_call(
        functools.partial(_qkv_kernel, na),
        grid=(na + nb,),
        in_specs=[
            pl.BlockSpec((tm, d), lambda i: (jnp.minimum(i, na - 1), 0)),
            pl.BlockSpec((tm, d), lambda i: (jnp.maximum(i - na, 0), 0)),
            pl.BlockSpec((None, 8, d), mod_idx),
            _const_spec((1, d)),
            _const_spec((d, nq + 2 * nk)),
            _const_spec((1, nq)),
            _const_spec((1, nk)),
            _const_spec((256, 256)),
            pl.BlockSpec((tm, LANES), pos_idx),
            pl.BlockSpec((tm, LANES), pos_idx),
        ],
        out_specs=[pl.BlockSpec((tm, nq), lambda i: (i, 0)),
                   pl.BlockSpec((N_KV_HEADS, LANES, tm), lambda i: (0, 0, i)),
                   pl.BlockSpec((tm, N_KV_HEADS * LANES), lambda i: (i, 0)),
                   pl.BlockSpec((tm, 2 * nk), lambda i: (jnp.minimum(i, na - 1), 0))],
        out_shape=[jax.ShapeDtypeStruct((n, nq), BF16),
                   jax.ShapeDtypeStruct((N_KV_HEADS, LANES, n), BF16),
                   jax.ShapeDtypeStruct((n, N_KV_HEADS * LANES), BF16),
                   jax.ShapeDtypeStruct((n_p, 2 * nk), F32)],
        compiler_params=_params(("arbitrary",)),
        name="qkv",
    )(xp, xs, mod, g, w, qn, kn, eye4, cos, sin)


def _attn_group(q2a, q2b, kt, vv, bias, n_bias, sinks):
    tq = q2a.shape[0]
    lane = lax.broadcasted_iota(I32, (tq, LANES), 1)
    lo = lane < HEAD_DIM
    zero = jnp.zeros_like(q2a)
    qs = jnp.concatenate([jnp.where(lo, q2a, zero), jnp.where(lo, zero, q2a),
                          jnp.where(lo, q2b, zero), jnp.where(lo, zero, q2b)], axis=0)
    s = jnp.dot(qs, kt, preferred_element_type=F32)
    ps, dens = [], []
    for h in range(4):
        sh = s[h * tq:(h + 1) * tq]
        if bias is not None:
            sh = jnp.concatenate([sh[:, :n_bias] + bias, sh[:, n_bias:]], axis=1)
        m = jnp.maximum(jnp.max(sh, axis=-1, keepdims=True), sinks[h])
        p = jnp.exp2(sh - m)
        dens.append(jnp.sum(p, axis=-1, keepdims=True) + jnp.exp2(sinks[h] - m))
        ps.append(p.astype(BF16))
    o = (jnp.dot(jnp.concatenate(ps, axis=0), vv, preferred_element_type=F32)
         / jnp.concatenate(dens, axis=0))
    oa = jnp.where(lo, o[0:tq], o[tq:2 * tq])
    ob = jnp.where(lo, o[2 * tq:3 * tq], o[3 * tq:4 * tq])
    return oa, ob


def _ctx_attn_kernel(sink_ref, q_ref, kt_ref, vv_ref, x_ref, mod_ref, wo_ref, o_ref, osc):
    for g in range(N_KV_HEADS):
        q2a = q_ref[:, (2 * g) * LANES:(2 * g + 1) * LANES]
        q2b = q_ref[:, (2 * g + 1) * LANES:(2 * g + 2) * LANES]
        sinks = [sink_ref[4 * g + j] * LOG2E for j in range(4)]
        oa, ob = _attn_group(q2a, q2b, kt_ref[g], vv_ref[:, g * LANES:(g + 1) * LANES], None, 0, sinks)
        osc[:, (2 * g) * LANES:(2 * g + 1) * LANES] = oa.astype(BF16)
        osc[:, (2 * g + 1) * LANES:(2 * g + 2) * LANES] = ob.astype(BF16)
    att = jnp.dot(osc[...], wo_ref[...], preferred_element_type=F32)
    o_ref[...] = x_ref[...] + mod_ref[2:3, :] * att


def _ctx_attention(sink, q, kt, vv, xp, seq, mod, wo):
    n_p, d = xp.shape
    nb = n_p // seq
    return pl.pallas_call(
        _ctx_attn_kernel,
        grid=(nb,),
        in_specs=[
            pl.BlockSpec(memory_space=pltpu.SMEM),
            pl.BlockSpec((seq, d), lambda b: (b, 0)),
            pl.BlockSpec((N_KV_HEADS, LANES, seq), lambda b: (0, 0, b)),
            pl.BlockSpec((seq, N_KV_HEADS * LANES), lambda b: (b, 0)),
            pl.BlockSpec((seq, d), lambda b: (b, 0)),
            pl.BlockSpec((None, 8, d), lambda b: (0, 0, 0)),
            _const_spec((d, d)),
        ],
        out_specs=pl.BlockSpec((seq, d), lambda b: (b, 0)),
        out_shape=jax.ShapeDtypeStruct((n_p, d), F32),
        scratch_shapes=[pltpu.VMEM((seq, d), BF16)],
        compiler_params=_params(("arbitrary",)),
        name="ctx_attention",
    )(sink, q, kt, vv, xp, mod, wo)


LAT_QB = 4


def _lat_attn_kernel(t_len, sink_ref, q_ref, ktp_ref, kto_ref, ktn_ref, vvp_ref, vvo_ref, vvn_ref,
                     ckt_ref, cvv_ref, x_ref, mod_ref, wo_ref, o_ref, osc):
    step = pl.program_id(1)
    tq = ATTN_BLOCK
    nloc = 3 * ATTN_BLOCK
    qi = lax.broadcasted_iota(I32, (tq, nloc), 0)
    kj = lax.broadcasted_iota(I32, (tq, nloc), 1)
    in_window = jnp.abs(qi + tq - kj) <= WINDOW
    for u in range(LAT_QB):
        n = step * LAT_QB + u
        kpos = (n - 1) * tq + kj
        bias = jnp.where(in_window & (kpos >= 0) & (kpos < t_len), 0.0, NEG_INF)
        rows = slice(u * tq, (u + 1) * tq)
        for g in range(N_KV_HEADS):
            vs = slice(g * LANES, (g + 1) * LANES)
            kts = [ktp_ref[g]] + [kto_ref[g, :, j * tq:(j + 1) * tq] for j in range(LAT_QB)] + [ktn_ref[g]]
            vvs = [vvp_ref[:, vs]] + [vvo_ref[j * tq:(j + 1) * tq, vs] for j in range(LAT_QB)] + [vvn_ref[:, vs]]
            kt = jnp.concatenate(kts[u:u + 3] + [ckt_ref[g]], axis=1)
            vv = jnp.concatenate(vvs[u:u + 3] + [cvv_ref[:, vs]], axis=0)
            q2a = q_ref[rows, (2 * g) * LANES:(2 * g + 1) * LANES]
            q2b = q_ref[rows, (2 * g + 1) * LANES:(2 * g + 2) * LANES]
            sinks = [sink_ref[4 * g + j] * LOG2E for j in range(4)]
            oa, ob = _attn_group(q2a, q2b, kt, vv, bias, nloc, sinks)
            osc[rows, (2 * g) * LANES:(2 * g + 1) * LANES] = oa.astype(BF16)
            osc[rows, (2 * g + 1) * LANES:(2 * g + 2) * LANES] = ob.astype(BF16)
    att = jnp.dot(osc[...], wo_ref[...], preferred_element_type=F32)
    o_ref[...] = x_ref[...] + mod_ref[2:3, :] * att


def _lat_attention(sink, q, kt, vv, ckt, cvv, xs, n_p, t_len, mod, wo):
    n_s, d = xs.shape
    db = n_s // t_len
    tq = ATTN_BLOCK
    ts = LAT_QB * tq
    nblk = t_len // tq
    nstep = t_len // ts
    assert n_p % ts == 0 and t_len % ts == 0
    off = n_p // tq
    npast = ckt.shape[3]
    g4 = N_KV_HEADS

    def nbr(delta):
        return lambda b, m: off + b * nblk + jnp.clip(m * LAT_QB + delta, 0, nblk - 1)

    own = lambda b, m: n_p // ts + b * nstep + m
    return pl.pallas_call(
        functools.partial(_lat_attn_kernel, t_len),
        grid=(db, nstep),
        in_specs=[
            pl.BlockSpec(memory_space=pltpu.SMEM),
            pl.BlockSpec((ts, d), lambda b, m: (own(b, m), 0)),
            pl.BlockSpec((g4, LANES, tq), lambda b, m: (0, 0, nbr(-1)(b, m))),
            pl.BlockSpec((g4, LANES, ts), lambda b, m: (0, 0, own(b, m))),
            pl.BlockSpec((g4, LANES, tq), lambda b, m: (0, 0, nbr(LAT_QB)(b, m))),
            pl.BlockSpec((tq, g4 * LANES), lambda b, m: (nbr(-1)(b, m), 0)),
            pl.BlockSpec((ts, g4 * LANES), lambda b, m: (own(b, m), 0)),
            pl.BlockSpec((tq, g4 * LANES), lambda b, m: (nbr(LAT_QB)(b, m), 0)),
            pl.BlockSpec((None, g4, LANES, npast), lambda b, m: (b, 0, 0, 0)),
            pl.BlockSpec((None, npast, g4 * LANES), lambda b, m: (b, 0, 0)),
            pl.BlockSpec((ts, d), lambda b, m: (b * nstep + m, 0)),
            pl.BlockSpec((None, 8, d), lambda b, m: (1 + b, 0, 0)),
            _const_spec((d, d)),
        ],
        out_specs=pl.BlockSpec((ts, d), lambda b, m: (b * nstep + m, 0)),
        out_shape=jax.ShapeDtypeStruct((n_s, d), F32),
        scratch_shapes=[pltpu.VMEM((ts, d), BF16)],
        compiler_params=_params(("arbitrary", "arbitrary")),
        name="lat_attention",
    )(sink, q, kt, kt, kt, vv, vv, vv, ckt, cvv, xs, mod, wo)


def _ffn_kernel(na, fc, xa_ref, xb_ref, mod_ref, g_ref, wg_ref, wu_ref, wd_ref, o_ref, acc):
    i = pl.program_id(0)
    x = jnp.where(i < na, xa_ref[...], xb_ref[...])
    a = _adaln(x, g_ref[...], mod_ref[3:4, :], mod_ref[4:5, :]).astype(BF16)
    nf = wg_ref.shape[1] // fc
    for f in range(nf):
        sl = slice(f * fc, (f + 1) * fc)
        hg = jnp.dot(a, wg_ref[:, sl], preferred_element_type=F32)
        hu = jnp.dot(a, wu_ref[:, sl], preferred_element_type=F32)
        act = (_silu(hg) * hu).astype(BF16)
        part = jnp.dot(act, wd_ref[sl, :], preferred_element_type=F32)
        if f == 0:
            acc[...] = part
        else:
            acc[...] += part
    o_ref[...] = x + mod_ref[5:6, :] * acc[...]


def _ffn(hp, hs, t_len, mod, g, wg, wu, wd):
    n_p, n_s = hp.shape[0], hs.shape[0]
    d, dff = wg.shape
    tm = _row_tile(n_p, t_len)
    na, nb = n_p // tm, n_s // tm
    nt_s = t_len // tm
    fc = 256
    assert dff % fc == 0
    mod_idx = lambda i: (jnp.where(i < na, 0, 1 + (i - na) // nt_s), 0, 0)
    return pl.pallas_call(
        functools.partial(_ffn_kernel, na, fc),
        grid=(na + nb,),
        in_specs=[
            pl.BlockSpec((tm, d), lambda i: (jnp.minimum(i, na - 1), 0)),
            pl.BlockSpec((tm, d), lambda i: (jnp.maximum(i - na, 0), 0)),
            pl.BlockSpec((None, 8, d), mod_idx),
            _const_spec((1, d)),
            _const_spec((d, dff)),
            _const_spec((d, dff)),
            _const_spec((dff, d)),
        ],
        out_specs=pl.BlockSpec((tm, d), lambda i: (i, 0)),
        out_shape=jax.ShapeDtypeStruct((n_p + n_s, d), F32),
        scratch_shapes=[pltpu.VMEM((tm, d), F32)],
        compiler_params=_params(("arbitrary",)),
        name="ffn",
    )(hp, hs, mod, g, wg, wu, wd)


def _split2(x):
    hi = x.astype(BF16)
    lo = (x - hi.astype(F32)).astype(BF16)
    return hi, lo


def _gla_in_kernel(nt_s, x_ref, mod_ref, g_ref, w_ref, w1_ref, w2_ref, gb_ref, tri_ref,
                   qk_out, v_out, r_out, b_out):
    x = x_ref[...]
    tm = x.shape[0]
    a = _adaln(x, g_ref[...], mod_ref[0:1, :], mod_ref[1:2, :]).astype(BF16)
    hk = GLA_HEADS * GLA_DK
    hv = GLA_HEADS * GLA_DV
    q = jnp.dot(a, w_ref[:, 0:hk], preferred_element_type=F32)
    qk_out[:, 0:hk] = q * (GLA_DK ** -0.5)
    qk_out[:, hk:2 * hk] = jnp.dot(a, w_ref[:, hk:2 * hk], preferred_element_type=F32)
    v_out[...] = jnp.dot(a, w_ref[:, 2 * hk:2 * hk + hv], preferred_element_type=F32)
    r_out[...] = jnp.dot(a, w_ref[:, 2 * hk + hv:2 * hk + 2 * hv], preferred_element_type=F32).astype(BF16)
    z1 = jnp.dot(a, w1_ref[...], preferred_element_type=F32)
    z1h = z1.astype(BF16)
    z1l = (z1 - z1h.astype(F32)).astype(BF16)
    z = (jnp.dot(z1h, w2_ref[...], preferred_element_type=F32)
         + jnp.dot(z1l, w2_ref[...], preferred_element_type=F32) + gb_ref[...])
    gate = (jnp.minimum(z, 0.0) - jnp.log(1.0 + jnp.exp(-jnp.abs(z)))) * (1.0 / GLA_GATE_TAU)
    c = GLA_CHUNK
    tri = tri_ref[...]
    for j in range(tm // c):
        rows = slice(j * c, (j + 1) * c)
        for dr in range(2):
            cols = slice(dr * hk, (dr + 1) * hk)
            hi, lo = _split2(gate[rows, cols])
            b_out[rows, cols] = (jnp.dot(tri[dr], hi, preferred_element_type=F32)
                                 + jnp.dot(tri[dr], lo, preferred_element_type=F32))


def _gla_in(h, n_p, t_len, mod, g, w_in, w1, w2, gate_b):
    n, d = h.shape
    tm = _row_tile(n_p, t_len)
    na = n_p // tm
    nt_s = t_len // tm
    hk, hv = GLA_HEADS * GLA_DK, GLA_HEADS * GLA_DV
    c = GLA_CHUNK
    lower = jnp.tril(jnp.ones((c, c), F32))
    tri = jnp.stack([lower, lower.T]).astype(BF16)
    mod_idx = lambda i: (jnp.where(i < na, 0, 1 + (i - na) // nt_s), 0, 0)
    row = lambda i: (i, 0)
    return pl.pallas_call(
        functools.partial(_gla_in_kernel, nt_s),
        grid=(n // tm,),
        in_specs=[
            pl.BlockSpec((tm, d), row),
            pl.BlockSpec((None, 8, d), mod_idx),
            _const_spec((1, d)),
            _const_spec(w_in.shape),
            _const_spec(w1.shape),
            _const_spec(w2.shape),
            _const_spec((1, 2 * hk)),
            _const_spec((2, c, c)),
        ],
        out_specs=[pl.BlockSpec((tm, 2 * hk), row), pl.BlockSpec((tm, hv), row),
                   pl.BlockSpec((tm, hv), row), pl.BlockSpec((tm, 2 * hk), row)],
        out_shape=[jax.ShapeDtypeStruct((n, 2 * hk), F32), jax.ShapeDtypeStruct((n, hv), F32),
                   jax.ShapeDtypeStruct((n, hv), BF16), jax.ShapeDtypeStruct((n, 2 * hk), F32)],
        compiler_params=_params(("arbitrary",)),
        name="gla_in",
    )(h, mod, g, w_in, w1, w2, gate_b, tri)


def _gla_chunk(q, k, v, b, st_ref, forward):
    c = q.shape[0]
    if forward:
        b_end, b_mid = b[c - 1:c, :], b[c // 2 - 1:c // 2, :]
    else:
        b_end, b_mid = b[0:1, :], b[c // 2:c // 2 + 1, :]
    e_up = jnp.exp(b - b_mid)
    e_dn = jnp.exp(b_mid - b)
    qa = q * e_up
    ka = k * e_dn
    qd = (qa * jnp.exp(b_mid)).astype(BF16)
    kd = (ka * jnp.exp(b_end - b_mid)).astype(BF16)
    att = lax.dot_general(qa.astype(BF16), ka.astype(BF16), (((1,), (1,)), ((), ())),
                          preferred_element_type=F32)
    ri = lax.broadcasted_iota(I32, (c, c), 0)
    ci = lax.broadcasted_iota(I32, (c, c), 1)
    keep = (ci <= ri) if forward else (ci >= ri)
    att = jnp.where(keep, att, 0.0).astype(BF16)
    vb = v.astype(BF16)
    st = st_ref[...]
    o = lax.dot_general(qd, st.astype(BF16), (((1,), (1,)), ((), ())), preferred_element_type=F32)
    o = o + jnp.dot(att, vb, preferred_element_type=F32)
    upd = lax.dot_general(vb, kd, (((0,), (0,)), ((), ())), preferred_element_type=F32)
    st_ref[...] = st * jnp.exp(b_end) + upd
    return o


def _gla_scan_kernel(qkf_ref, vf_ref, bf_ref, qkb_ref, vb_ref, bb_ref, sf0_ref, sb0_ref,
                     of_ref, ob_ref, sf_ref, sb_ref, stf, stb):
    i = pl.program_id(1)
    nblk = pl.num_programs(1)
    c = GLA_CHUNK
    ncb = qkf_ref.shape[0] // c
    hk = GLA_HEADS * GLA_DK
    dk, dv = GLA_DK, GLA_DV

    @pl.when(i == 0)
    def _():
        for h in range(GLA_HEADS):
            stf[h] = sf0_ref[h].T
            stb[h] = sb0_ref[h].T

    def step(ci, carry):
        rf = pl.ds(pl.multiple_of(ci * c, c), c)
        rb = pl.ds(pl.multiple_of((ncb - 1 - ci) * c, c), c)
        for h in range(GLA_HEADS):
            qs = slice(h * dk, (h + 1) * dk)
            ks = slice(hk + h * dk, hk + (h + 1) * dk)
            vs = slice(h * dv, (h + 1) * dv)
            of = _gla_chunk(qkf_ref[rf, qs], qkf_ref[rf, ks], vf_ref[rf, vs], bf_ref[rf, qs],
                            stf.at[h], True)
            of_ref[rf, vs] = of.astype(of_ref.dtype)
            ob = _gla_chunk(qkb_ref[rb, qs], qkb_ref[rb, ks], vb_ref[rb, vs], bb_ref[rb, qs],
                            stb.at[h], False)
            ob_ref[rb, vs] = ob.astype(ob_ref.dtype)
        return carry

    lax.fori_loop(0, ncb, step, 0, unroll=2)

    @pl.when(i == nblk - 1)
    def _():
        for h in range(GLA_HEADS):
            sf_ref[h] = stf[h].T
            sb_ref[h] = stb[h].T


def _gla_scan(qk, v, bc, sf0, sb0, b_off, nb, t_len):
    n = qk.shape[0]
    assert n % t_len == 0
    ns = n // t_len
    tb = _row_tile(t_len)
    nblk = t_len // tb
    qk3 = qk.reshape(ns, t_len, qk.shape[1])
    v3 = v.reshape(ns, t_len, v.shape[1])
    bc3 = bc.reshape(ns, t_len, bc.shape[1])
    h, dk, dv = GLA_HEADS, GLA_DK, GLA_DV
    hk = h * dk
    st_spec = pl.BlockSpec((None, h, dk, dv), lambda b, i: (b, 0, 0, 0))
    fwd = lambda b, i: (b_off + b, i, 0)
    bwd = lambda b, i: (b_off + b, nblk - 1 - i, 0)
    return pl.pallas_call(
        _gla_scan_kernel,
        grid=(nb, nblk),
        in_specs=[
            pl.BlockSpec((None, tb, 2 * hk), fwd),
            pl.BlockSpec((None, tb, h * dv), fwd),
            pl.BlockSpec((None, tb, hk), fwd),
            pl.BlockSpec((None, tb, 2 * hk), bwd),
            pl.BlockSpec((None, tb, h * dv), bwd),
            pl.BlockSpec((None, tb, hk), lambda b, i: (b_off + b, nblk - 1 - i, 1)),
            st_spec, st_spec,
        ],
        out_specs=[pl.BlockSpec((None, tb, h * dv), lambda b, i: (b, i, 0)),
                   pl.BlockSpec((None, tb, h * dv), lambda b, i: (b, nblk - 1 - i, 0)),
                   st_spec, st_spec],
        out_shape=[jax.ShapeDtypeStruct((nb, t_len, h * dv), BF16),
                   jax.ShapeDtypeStruct((nb, t_len, h * dv), BF16),
                   jax.ShapeDtypeStruct((nb, h, dk, dv), F32),
                   jax.ShapeDtypeStruct((nb, h, dk, dv), F32)],
        scratch_shapes=[pltpu.VMEM((h, dv, dk), F32), pltpu.VMEM((h, dv, dk), F32)],
        compiler_params=_params(("arbitrary", "arbitrary")),
        name="gla_scan",
    )(qk3, v3, bc3, qk3, v3, bc3, sf0, sb0)


def _gla_out_kernel(na, ofa_ref, oba_ref, ofb_ref, obb_ref, r_ref, h_ref, mod_ref, on_ref, wo_ref,
                    g2_ref, rw_ref, tri_ref, h_out, f_out, route_out, wt_out, cnt_out, run):
    i = pl.program_id(0)
    is_ctx = i < na
    o = (jnp.where(is_ctx, ofa_ref[...], ofb_ref[...]).astype(F32)
         + jnp.where(is_ctx, oba_ref[...], obb_ref[...]).astype(F32))
    tm = o.shape[0]
    dv = GLA_DV
    parts = []
    for hh in range(GLA_HEADS):
        oh = o[:, hh * dv:(hh + 1) * dv]
        parts.append(_rms(oh) * on_ref[...])
    on = jnp.concatenate(parts, axis=1)
    r = r_ref[...].astype(F32)
    gated = (on * _silu(r)).astype(BF16)
    h = h_ref[...] + mod_ref[2:3, :] * jnp.dot(gated, wo_ref[...], preferred_element_type=F32)
    h_out[...] = h
    f = _adaln(h, g2_ref[...], mod_ref[3:4, :], mod_ref[4:5, :])
    f_out[...] = _pack_halves(f)
    fh = f.astype(BF16)
    fl = (f - fh.astype(F32)).astype(BF16)
    logits = (jnp.dot(fh, rw_ref[0], preferred_element_type=F32)
              + jnp.dot(fl, rw_ref[0], preferred_element_type=F32)
              + jnp.dot(fh, rw_ref[1], preferred_element_type=F32))
    lane = lax.broadcasted_iota(I32, (tm, LANES), 1)
    lane_f = lane.astype(F32)
    logits = jnp.where(lane < N_EXPERTS, logits, -jnp.inf)
    m1 = jnp.max(logits, axis=-1, keepdims=True)
    i1 = jnp.min(jnp.where(logits == m1, lane_f, float(LANES)), axis=-1, keepdims=True)
    rest = jnp.where(lane_f == i1, -jnp.inf, logits)
    m2 = jnp.max(rest, axis=-1, keepdims=True)
    i2 = jnp.min(jnp.where(rest == m2, lane_f, float(LANES)), axis=-1, keepdims=True)
    e2 = jnp.exp(m2 - m1)
    w1 = 1.0 / (1.0 + e2)
    w2 = e2 / (1.0 + e2)
    wt_out[...] = jnp.where(lane == 0, w1, jnp.where(lane == 1, w2, 0.0))
    sel1 = lane_f == i1
    sel2 = lane_f == i2
    picked = jnp.where(sel1, 1.0, jnp.where(sel2, 1.0, 0.0))

    @pl.when(i == 0)
    def _():
        run[...] = jnp.zeros_like(run)

    before = jnp.dot(tri_ref[...], picked.astype(BF16), preferred_element_type=F32) + run[0:1, :]
    rank1 = jnp.sum(jnp.where(sel1, before, 0.0), axis=-1, keepdims=True)
    rank2 = jnp.sum(jnp.where(sel2, before, 0.0), axis=-1, keepdims=True)
    run[...] = run[...] + jnp.sum(picked, axis=0, keepdims=True)
    cnt_out[...] = run[...]
    routing = jnp.where(lane == 0, i1, jnp.where(lane == 1, i2,
                        jnp.where(lane == 2, rank1, jnp.where(lane == 3, rank2, 0.0))))
    route_out[...] = routing.T[0:8, :]


def _gla_out(of_p, ob_p, of_s, ob_s, r, h, n_p, t_len, mod, out_norm, wo, g2, router_w):
    n, d = h.shape
    tm = _row_tile(n_p, t_len)
    na = n_p // tm
    nt_s = t_len // tm
    mod_idx = lambda i: (jnp.where(i < na, 0, 1 + (i - na) // nt_s), 0, 0)
    row = lambda i: (i, 0)
    ctx_row = lambda i: (jnp.minimum(i, na - 1), 0)
    lat_row = lambda i: (jnp.maximum(i - na, 0), 0)
    tri = jnp.tril(jnp.ones((tm, tm), F32), -1).astype(BF16)
    return pl.pallas_call(
        functools.partial(_gla_out_kernel, na),
        grid=(n // tm,),
        in_specs=[
            pl.BlockSpec((tm, d), ctx_row),
            pl.BlockSpec((tm, d), ctx_row),
            pl.BlockSpec((tm, d), lat_row),
            pl.BlockSpec((tm, d), lat_row),
            pl.BlockSpec((tm, d), row),
            pl.BlockSpec((tm, d), row),
            pl.BlockSpec((None, 8, d), mod_idx),
            _const_spec((1, GLA_DV)),
            _const_spec((d, d)),
            _const_spec((1, d)),
            _const_spec((2, d, LANES)),
            _const_spec((tm, tm)),
        ],
        out_specs=[pl.BlockSpec((tm, d), row), pl.BlockSpec((tm, d // 2), row),
                   pl.BlockSpec((8, tm), lambda i: (0, i)), pl.BlockSpec((tm, LANES), row),
                   _const_spec((8, LANES))],
        out_shape=[jax.ShapeDtypeStruct((n, d), F32), jax.ShapeDtypeStruct((n, d // 2), jnp.uint32),
                   jax.ShapeDtypeStruct((8, n), F32), jax.ShapeDtypeStruct((n, LANES), F32),
                   jax.ShapeDtypeStruct((8, LANES), F32)],
        scratch_shapes=[pltpu.VMEM((8, LANES), F32)],
        compiler_params=_params(("arbitrary",)),
        name="gla_out_router",
    )(of_p, ob_p, of_s, ob_s, r, h, mod, out_norm, wo, g2, router_w, tri)


SC_CORES = 2
SC_SUBCORES = 16
SC_CHUNK = 64


def _sc_gather_rows(table, idx):
    nw = SC_CORES * SC_SUBCORES
    b = idx.shape[0]
    d = table.shape[1]
    assert b % (nw * SC_CHUNK) == 0
    per_w = b // nw
    n_chunks = per_w // SC_CHUNK
    mesh = plsc.VectorSubcoreMesh(core_axis_name="c", subcore_axis_name="s",
                                  num_cores=SC_CORES, num_subcores=SC_SUBCORES)

    assert n_chunks % 2 == 0

    def body(table_hbm, idx_hbm, out_hbm, idx_v, rows_v, gsem, wsem):
        wid = lax.axis_index("s") * SC_CORES + lax.axis_index("c")
        base = wid * per_w
        pltpu.sync_copy(idx_hbm.at[wid], idx_v)

        def gather(j, slot):
            return pltpu.make_async_copy(table_hbm.at[idx_v.at[j]], rows_v.at[slot], gsem.at[slot])

        def write(j, slot):
            return pltpu.make_async_copy(rows_v.at[slot], out_hbm.at[pl.ds(base + j * SC_CHUNK, SC_CHUNK)],
                                         wsem.at[slot])

        gather(0, 0).start()

        @pl.loop(0, n_chunks, step=2)
        def _(j):
            for slot in range(2):
                jj = j + slot
                gather(jj, slot).wait()

                @pl.when(jj + 1 < n_chunks)
                def _():
                    @pl.when(jj >= 1)
                    def _():
                        write(jj - 1, 1 - slot).wait()

                    gather(jj + 1, 1 - slot).start()

                write(jj, slot).start()

        write(n_chunks - 2, 0).wait()
        write(n_chunks - 1, 1).wait()

    return pl.kernel(
        body,
        out_type=jax.ShapeDtypeStruct((b, d), table.dtype),
        mesh=mesh,
        scratch_types=[pltpu.VMEM((n_chunks, SC_CHUNK), I32),
                       pltpu.VMEM((2, SC_CHUNK, d), table.dtype),
                       pltpu.SemaphoreType.DMA((2,)),
                       pltpu.SemaphoreType.DMA((2,))],
        name="sc_gather_rows",
    )(table, idx.reshape(nw, n_chunks, SC_CHUNK))


def _sc_scatter_rows(rows, idx2, p):
    nw = SC_CORES * SC_SUBCORES
    n, d = rows.shape
    assert n % (nw * SC_CHUNK) == 0
    per_w = n // nw
    n_chunks = per_w // SC_CHUNK
    mesh = plsc.VectorSubcoreMesh(core_axis_name="c", subcore_axis_name="s",
                                  num_cores=SC_CORES, num_subcores=SC_SUBCORES)

    assert n_chunks % 2 == 0

    def body(rows_hbm, idx0_hbm, idx1_hbm, out_hbm, idx0_v, idx1_v, rows_v, rsem, ssem):
        wid = lax.axis_index("s") * SC_CORES + lax.axis_index("c")
        base = wid * per_w
        pltpu.sync_copy(idx0_hbm.at[wid], idx0_v)
        pltpu.sync_copy(idx1_hbm.at[wid], idx1_v)

        def read(j, slot):
            return pltpu.make_async_copy(rows_hbm.at[pl.ds(base + j * SC_CHUNK, SC_CHUNK)], rows_v.at[slot],
                                         rsem.at[slot])

        read(0, 0).start()

        @pl.loop(0, n_chunks, step=2)
        def _(j):
            for slot in range(2):
                jj = j + slot
                read(jj, slot).wait()

                @pl.when(jj + 1 < n_chunks)
                def _():
                    read(jj + 1, 1 - slot).start()

                s0 = pltpu.make_async_copy(rows_v.at[slot], out_hbm.at[idx0_v.at[jj]], ssem.at[0])
                s1 = pltpu.make_async_copy(rows_v.at[slot], out_hbm.at[idx1_v.at[jj]], ssem.at[1])
                s0.start()
                s1.start()
                s0.wait()
                s1.wait()

    return pl.kernel(
        body,
        out_type=jax.ShapeDtypeStruct((p, d), rows.dtype),
        mesh=mesh,
        scratch_types=[pltpu.VMEM((n_chunks, SC_CHUNK), I32),
                       pltpu.VMEM((n_chunks, SC_CHUNK), I32),
                       pltpu.VMEM((2, SC_CHUNK, d), rows.dtype),
                       pltpu.SemaphoreType.DMA((2,)),
                       pltpu.SemaphoreType.DMA((2,))],
        name="sc_scatter_rows",
    )(rows, idx2[0].reshape(nw, n_chunks, SC_CHUNK), idx2[1].reshape(nw, n_chunks, SC_CHUNK))


EXPERT_SUB = 512


def _pack_halves(x):
    k = x.shape[1] // 2
    lo = lax.bitcast_convert_type(x[:, :k].astype(BF16).astype(F32), jnp.uint32)
    hi = lax.bitcast_convert_type(x[:, k:].astype(BF16).astype(F32), jnp.uint32)
    return (lo >> 16) | (hi & jnp.uint32(0xFFFF0000))


def _unpack_halves(w):
    lo = lax.bitcast_convert_type(w << 16, F32).astype(BF16)
    hi = lax.bitcast_convert_type(w & jnp.uint32(0xFFFF0000), F32).astype(BF16)
    return lo, hi


def _expert_kernel(te_ref, ts_ref, nu_ref, nv_ref, x_ref, wg_ref, wu_ref, wd_ref, y_ref, acc, xb):
    i = pl.program_id(0)
    f = pl.program_id(1)
    nf = pl.num_programs(1)
    tm = x_ref.shape[0]
    half = x_ref.shape[1]
    sub = EXPERT_SUB
    nvalid = nv_ref[i]

    @pl.when(f == 0)
    def _():
        acc[...] = jnp.zeros_like(acc)

    @pl.when((f == 0) & (nvalid > 0))
    def _():
        rid = lax.broadcasted_iota(I32, (tm, half), 0)
        lo, hi = _unpack_halves(x_ref[...])
        zero = jnp.zeros_like(lo)
        xb[:, 0:half] = jnp.where(rid < nvalid, lo, zero)
        xb[:, half:2 * half] = jnp.where(rid < nvalid, hi, zero)

    def compute(rows):
        x = xb[rows, :]
        hg = jnp.dot(x, wg_ref[...].astype(BF16), preferred_element_type=F32)
        hu = jnp.dot(x, wu_ref[...].astype(BF16), preferred_element_type=F32)
        act = (_silu(hg) * hu).astype(BF16)
        acc[rows, :] += jnp.dot(act, wd_ref[...].astype(BF16), preferred_element_type=F32)

    @pl.when(nvalid > sub)
    def _():
        compute(slice(0, tm))

    @pl.when((nvalid > 0) & (nvalid <= sub))
    def _():
        compute(slice(0, sub))

    @pl.when(f == nf - 1)
    def _():
        y_ref[...] = _pack_halves(acc[...])


def _experts(x, tile_expert, tile_src, n_used, n_valid, wg, wu, wd, tm, tf):
    p, half = x.shape
    d = 2 * half
    ne, _, dff = wg.shape
    nf = dff // tf
    assert dff % tf == 0 and tm % EXPERT_SUB == 0

    def fidx(i, f, nu):
        return jnp.where(i < nu[0], f, nf - 1)

    grid_spec = pltpu.PrefetchScalarGridSpec(
        num_scalar_prefetch=4,
        grid=(p // tm, nf),
        in_specs=[
            pl.BlockSpec((tm, half), lambda i, f, te, ts, nu, nv: (ts[i], 0)),
            pl.BlockSpec((None, d, tf), lambda i, f, te, ts, nu, nv: (te[i], 0, fidx(i, f, nu))),
            pl.BlockSpec((None, d, tf), lambda i, f, te, ts, nu, nv: (te[i], 0, fidx(i, f, nu))),
            pl.BlockSpec((None, tf, d), lambda i, f, te, ts, nu, nv: (te[i], fidx(i, f, nu), 0)),
        ],
        out_specs=pl.BlockSpec((tm, half), lambda i, f, te, ts, nu, nv: (i, 0)),
        scratch_shapes=[pltpu.VMEM((tm, d), F32), pltpu.VMEM((tm, d), BF16)],
    )
    return pl.pallas_call(
        _expert_kernel,
        grid_spec=grid_spec,
        out_shape=jax.ShapeDtypeStruct((p, half), jnp.uint32),
        compiler_params=_params(("arbitrary", "arbitrary"), 60 * 1024 * 1024),
        name="moe_experts",
    )(tile_expert, tile_src, n_used, n_valid, x, wg, wu, wd)


def _combine_kernel(wt_ref, h_ref, mod_ref, y0_ref, y1_ref, o_ref):
    wt = wt_ref[...]
    half = y0_ref.shape[1]
    lo0, hi0 = _unpack_halves(y0_ref[...])
    lo1, hi1 = _unpack_halves(y1_ref[...])
    w0, w1 = wt[:, 0:1], wt[:, 1:2]
    h = h_ref[...]
    gate = mod_ref[5:6, :]
    o_ref[:, 0:half] = h[:, 0:half] + gate[:, 0:half] * (w0 * lo0.astype(F32) + w1 * lo1.astype(F32))
    o_ref[:, half:] = h[:, half:] + gate[:, half:] * (w0 * hi0.astype(F32) + w1 * hi1.astype(F32))


def _combine(yg, wt, h, mod, row_off, n_rows, n_mod, mod_off, tm):
    d = h.shape[1]
    nt = n_rows // tm
    toff = row_off // tm
    per_mod = n_rows // n_mod // tm
    return pl.pallas_call(
        _combine_kernel,
        grid=(nt,),
        in_specs=[
            pl.BlockSpec((tm, LANES), lambda i: (toff + i, 0)),
            pl.BlockSpec((tm, d), lambda i: (toff + i, 0)),
            pl.BlockSpec((None, 8, d), lambda i: (mod_off + i // per_mod, 0, 0)),
            pl.BlockSpec((None, tm, d // 2), lambda i: (0, i, 0)),
            pl.BlockSpec((None, tm, d // 2), lambda i: (1, i, 0)),
        ],
        out_specs=pl.BlockSpec((tm, d), lambda i: (i, 0)),
        out_shape=jax.ShapeDtypeStruct((n_rows, d), F32),
        compiler_params=_params(("arbitrary",)),
        name="moe_combine",
    )(wt, h, mod, yg, yg)


def _route(route, counts, n, tm):
    cnt = counts[0, :N_EXPERTS].astype(I32)
    padded = ((cnt + tm - 1) // tm) * tm
    gend = jnp.cumsum(padded)
    goff = gend - padded
    e = route[0:2].astype(I32)
    rank = route[2:4].astype(I32)
    onehot = e[:, :, None] == jnp.arange(N_EXPERTS, dtype=I32)[None, None, :]
    dest = jnp.sum(jnp.where(onehot, goff[None, None, :], 0), axis=-1) + rank
    p = 2 * n + N_EXPERTS * tm
    n_used = gend[-1] // tm
    tiles = jnp.arange(p // tm, dtype=I32)
    tile_src = jnp.minimum(tiles, n_used - 1)
    tile_expert = jnp.minimum(jnp.sum((gend[None, :] <= (tile_src * tm)[:, None]).astype(I32), axis=1),
                              N_EXPERTS - 1)
    used = goff[tile_expert] + cnt[tile_expert]
    n_valid = jnp.where(tiles < n_used, jnp.clip(used - tiles * tm, 0, tm), 0).astype(I32)
    return dest, p, tile_expert, tile_src, n_used.reshape(1).astype(I32), n_valid


def kernel(x_prompt, x_sample, cache_k, cache_v, state_fwd, state_bwd, c, c_ctx, ada_w, ada_b, norm1_g, norm2_g, attn_w_qkv, attn_q_norm, attn_k_norm, attn_sink, attn_w_o, gla_w_in, gla_gate_w1, gla_gate_w2, gla_gate_b, gla_out_norm, gla_w_o, ffn_w_gate, ffn_w_up, ffn_w_down, moe_router, moe_w_gate, moe_w_up, moe_w_down):
    bp, seq, d = x_prompt.shape
    db, t_len, _ = x_sample.shape
    n_p, n_s = bp * seq, db * t_len
    n = n_p + n_s
    xp = x_prompt.reshape(n_p, d)
    xs = x_sample.reshape(n_s, d)

    cond = jnp.concatenate([c_ctx[None, :], c], axis=0)
    assert cond.shape[0] <= 8
    cond_t = jnp.pad(cond, ((0, 8 - cond.shape[0]), (0, 0))).T
    mods = _modulation(cond_t, cond.shape[0], ada_w, ada_b)

    nk = N_KV_HEADS * HEAD_DIM
    qn = jnp.tile(attn_q_norm[0], N_HEADS)[None, :]
    kn = jnp.tile(attn_k_norm[0], N_KV_HEADS)[None, :]
    q, kt, vv, kv = _qkv(xp, xs, t_len, mods[0], norm1_g[0][None, :], attn_w_qkv[0].astype(BF16), qn, kn)
    wo0 = attn_w_o[0].astype(BF16)
    sink = attn_sink[0]
    hp = _ctx_attention(sink, q, kt, vv, xp, seq, mods[0], wo0)
    ck = cache_k[:, 0].astype(BF16)
    cv = cache_v[:, 0].astype(BF16)
    ckt = jnp.concatenate([ck, ck], axis=-1).transpose(0, 2, 3, 1)
    cvv = jnp.concatenate([cv, cv], axis=-1).reshape(db, cv.shape[1], N_KV_HEADS * LANES)
    hs = _lat_attention(sink, q, kt, vv, ckt, cvv, xs, n_p, t_len, mods[0], wo0)
    new_k = kv[:n_p, :nk].reshape(bp, 1, seq, N_KV_HEADS, HEAD_DIM)
    new_v = kv[:n_p, nk:].reshape(bp, 1, seq, N_KV_HEADS, HEAD_DIM)
    h = _ffn(hp, hs, t_len, mods[0], norm2_g[0][None, :], ffn_w_gate[0].astype(BF16),
             ffn_w_up[0].astype(BF16), ffn_w_down[0].astype(BF16))

    hk = GLA_HEADS * GLA_DK
    rank = GLA_GATE_RANK
    w1 = jnp.concatenate([gla_gate_w1[0, 0], gla_gate_w1[0, 1]], axis=1)
    w1 = jnp.pad(w1, ((0, 0), (0, LANES - 2 * rank))).astype(BF16)
    w2 = jnp.zeros((LANES, 2 * hk), F32)
    w2 = w2.at[0:rank, 0:hk].set(gla_gate_w2[0, 0]).at[rank:2 * rank, hk:].set(gla_gate_w2[0, 1]).astype(BF16)
    gate_b = gla_gate_b[0].reshape(1, 2 * hk)
    qk, v, r, bc = _gla_in(h, n_p, t_len, mods[1], norm1_g[1][None, :], gla_w_in[0].astype(BF16),
                           w1, w2, gate_b)
    zeros = jnp.zeros((bp, GLA_HEADS, GLA_DK, GLA_DV), F32)
    of_p, ob_p, new_sf, new_sb = _gla_scan(qk, v, bc, zeros, zeros, 0, bp, seq)
    of_s, ob_s, _, _ = _gla_scan(qk, v, bc, state_fwd[:, 0], state_bwd[:, 0], n_p // t_len, db, t_len)
    rw = jnp.pad(moe_router[0], ((0, 0), (0, LANES - N_EXPERTS)))
    rw_hi = rw.astype(BF16)
    rw_lo = (rw - rw_hi.astype(F32)).astype(BF16)
    h, f, route, wt, counts = _gla_out(
        of_p.reshape(n_p, d), ob_p.reshape(n_p, d), of_s.reshape(n_s, d), ob_s.reshape(n_s, d), r, h,
        n_p, t_len, mods[1], gla_out_norm[0][None, :], gla_w_o[0].astype(BF16), norm2_g[1][None, :],
        jnp.stack([rw_hi, rw_lo]))

    tm_e = 1024
    dest_k, p_rows, tile_expert, tile_src, n_used, n_valid = _route(route, counts, n, tm_e)
    xg = _sc_scatter_rows(f, dest_k, p_rows)
    y = _experts(xg, tile_expert, tile_src, n_used, n_valid, moe_w_gate[0], moe_w_up[0],
                 moe_w_down[0], tm_e, 512)
    yg_p = _sc_gather_rows(y, dest_k[:, :n_p].reshape(2 * n_p)).reshape(2, n_p, d // 2)
    yg_s = _sc_gather_rows(y, dest_k[:, n_p:].reshape(2 * n_s)).reshape(2, n_s, d // 2)
    tm_c = _row_tile(n_p, t_len)
    y_p = _combine(yg_p, wt, h, mods[1], 0, n_p, 1, 0, tm_c)
    y_s = _combine(yg_s, wt, h, mods[1], n_p, n_s, db, 1, tm_c)

    return (y_p.reshape(bp, seq, d), y_s.reshape(db, t_len, d), new_k, new_v,
            new_sf[:, None], new_sb[:, None])
```

```python
import functools
import math

import jax
import jax.numpy as jnp
from jax import lax
from jax.experimental import pallas as pl
from jax.experimental.pallas import tpu as pltpu
from jax.experimental.pallas import tpu_sc as plsc

F32 = jnp.float32
BF16 = jnp.bfloat16
I32 = jnp.int32

D_MODEL = 1024
N_HEADS = 16
N_KV_HEADS = 4
HEAD_DIM = 64
GRID_W = 64
WINDOW = 128
ATTN_BLOCK = 128
ROPE_THETA = 10000.0
GLA_HEADS = 4
GLA_DK = 128
GLA_DV = 256
GLA_GATE_RANK = 16
GLA_GATE_TAU = 16.0
GLA_CHUNK = 64
N_EXPERTS = 8
NORM_EPS = 1e-6
NEG_INF = -1e30

LANES = 128
VMEM_LIMIT = 56 * 1024 * 1024


def _params(sem, vmem=VMEM_LIMIT):
    return pltpu.CompilerParams(dimension_semantics=sem, vmem_limit_bytes=vmem)


def _row_tile(*counts, cap=512):
    t = cap
    while any(c % t for c in counts):
        t //= 2
    assert t >= 8
    return t


def _rms(x):
    return x * lax.rsqrt(jnp.mean(x * x, axis=-1, keepdims=True) + NORM_EPS)


def _adaln(x, g, shift, scale):
    return _rms(x) * (g * (1.0 + scale)) + shift


def _silu(x):
    return x * jax.nn.sigmoid(x)


def _const_spec(shape):
    nd = len(shape)
    return pl.BlockSpec(shape, lambda *_: (0,) * nd)


def _mod_kernel(n_cond, ct_ref, w_ref, b_ref, o_ref):
    ct = ct_ref[...]
    s = _silu(ct)
    w = w_ref[...]
    rows = [jnp.sum(w * s[:, r:r + 1], axis=0, keepdims=True) for r in range(n_cond)]
    rows += [jnp.zeros_like(rows[0])] * (8 - n_cond)
    o_ref[...] = jnp.concatenate(rows, axis=0) + b_ref[...]


def _modulation(cond_t, n_cond, ada_w, ada_b):
    depth, d, n6 = ada_w.shape
    tn = 1024
    out = pl.pallas_call(
        functools.partial(_mod_kernel, n_cond),
        grid=(depth, n6 // tn),
        in_specs=[
            pl.BlockSpec((d, 8), lambda l, j: (0, 0)),
            pl.BlockSpec((None, d, tn), lambda l, j: (l, 0, j)),
            pl.BlockSpec((None, 1, tn), lambda l, j: (l, 0, j)),
        ],
        out_specs=pl.BlockSpec((None, 8, tn), lambda l, j: (l, 0, j)),
        out_shape=jax.ShapeDtypeStruct((depth, 8, n6), F32),
        compiler_params=_params(("arbitrary", "arbitrary")),
        name="modulation",
    )(cond_t, ada_w, ada_b.reshape(depth, 1, n6))
    m = out[:, :n_cond].reshape(depth, n_cond, 6, d)
    return jnp.pad(m, ((0, 0), (0, 0), (0, 2), (0, 0)))


LOG2E = math.log2(math.e)


def _dup_half(k2, half):
    lane = lax.broadcasted_iota(I32, k2.shape, 1)
    lo = lane < HEAD_DIM
    r = pltpu.roll(k2, HEAD_DIM, 1)
    return jnp.where(lo, k2, r) if half == 0 else jnp.where(lo, r, k2)


def _qkv_kernel(na, xa_ref, xb_ref, mod_ref, g_ref, w_ref, qn_ref, kn_ref, bd_ref,
                cos_ref, sin_ref, q_out, kt_out, vv_out, kv_out):
    i = pl.program_id(0)
    is_ctx = i < na
    x = jnp.where(is_ctx, xa_ref[...], xb_ref[...])
    a = _adaln(x, g_ref[...], mod_ref[0:1, :], mod_ref[1:2, :])
    y = jnp.dot(a.astype(BF16), w_ref[...], preferred_element_type=F32)
    tm = x.shape[0]
    cos = jnp.where(is_ctx, 1.0, cos_ref[...])
    sin = jnp.where(is_ctx, 0.0, sin_ref[...])
    lane = lax.broadcasted_iota(I32, (tm, LANES), 1)
    first16 = (lane % 32) < 16

    def norm_rope(z, wt):
        ss = jnp.dot((z * z).astype(BF16), bd_ref[...], preferred_element_type=F32)
        zn = z * lax.rsqrt(ss * (1.0 / HEAD_DIM) + NORM_EPS) * wt
        outs = []
        for c in range(2):
            t = zn[:, c * LANES:(c + 1) * LANES]
            partner = jnp.where(first16, pltpu.roll(t, LANES - 16, 1), pltpu.roll(t, 16, 1))
            outs.append(t * cos + partner * sin)
        return jnp.concatenate(outs, axis=1)

    nq = N_HEADS * HEAD_DIM
    nk = N_KV_HEADS * HEAD_DIM
    scale = HEAD_DIM ** -0.5 * LOG2E
    for s in range(nq // 256):
        sl = slice(s * 256, (s + 1) * 256)
        q_out[:, sl] = (norm_rope(y[:, sl], qn_ref[:, sl]) * scale).astype(BF16)
    k = norm_rope(y[:, nq:nq + nk], kn_ref[...])
    v = y[:, nq + nk:nq + 2 * nk]
    for g in range(N_KV_HEADS):
        c = g // 2
        kk = _dup_half(k[:, c * LANES:(c + 1) * LANES], g % 2)
        kt_out[g] = kk.T.astype(BF16)
        vv_out[:, g * LANES:(g + 1) * LANES] = _dup_half(v[:, c * LANES:(c + 1) * LANES], g % 2).astype(BF16)

    @pl.when(is_ctx)
    def _():
        kv_out[:, 0:nk] = k
        kv_out[:, nk:2 * nk] = v


def _rope_tables(t_len):
    pos = jnp.arange(t_len)
    row = (pos // GRID_W).astype(F32)[:, None]
    col = (pos % GRID_W).astype(F32)[:, None]
    half = HEAD_DIM // 2
    inv = ROPE_THETA ** (-jnp.arange(0, half, 2, dtype=F32) / half)[None, :]
    ar, ac = row * inv, col * inv
    cos = jnp.concatenate([jnp.cos(ar), jnp.cos(ar), jnp.cos(ac), jnp.cos(ac)], axis=1)
    sin = jnp.concatenate([-jnp.sin(ar), jnp.sin(ar), -jnp.sin(ac), jnp.sin(ac)], axis=1)
    return jnp.tile(cos, (1, 2)), jnp.tile(sin, (1, 2))


def _qkv(xp, xs, t_len, mod, g, w, qn, kn):
    n_p, n_s = xp.shape[0], xs.shape[0]
    tm = _row_tile(n_p, t_len)
    na, nb = n_p // tm, n_s // tm
    nt_s = t_len // tm
    n = n_p + n_s
    d = D_MODEL
    nq, nk = N_HEADS * HEAD_DIM, N_KV_HEADS * HEAD_DIM
    cos, sin = _rope_tables(t_len)
    eye4 = jnp.kron(jnp.eye(4, dtype=F32), jnp.ones((HEAD_DIM, HEAD_DIM), F32)).astype(BF16)
    mod_idx = lambda i: (jnp.where(i < na, 0, 1 + (i - na) // nt_s), 0, 0)
    pos_idx = lambda i: (jnp.maximum(i - na, 0) % nt_s, 0)
    return pl.pallas_call(
        functools.partial(_qkv_kernel, na),
        grid=(na + nb,),
        in_specs=[
            pl.BlockSpec((tm, d), lambda i: (jnp.minimum(i, na - 1), 0)),
            pl.BlockSpec((tm, d), lambda i: (jnp.maximum(i - na, 0), 0)),
            pl.BlockSpec((None, 8, d), mod_idx),
            _const_spec((1, d)),
            _const_spec((d, nq + 2 * nk)),
            _const_spec((1, nq)),
            _const_spec((1, nk)),
            _const_spec((256, 256)),
            pl.BlockSpec((tm, LANES), pos_idx),
            pl.BlockSpec((tm, LANES), pos_idx),
        ],
        out_specs=[pl.BlockSpec((tm, nq), lambda i: (i, 0)),
                   pl.BlockSpec((N_KV_HEADS, LANES, tm), lambda i: (0, 0, i)),
                   pl.BlockSpec((tm, N_KV_HEADS * LANES), lambda i: (i, 0)),
                   pl.BlockSpec((tm, 2 * nk), lambda i: (jnp.minimum(i, na - 1), 0))],
        out_shape=[jax.ShapeDtypeStruct((n, nq), BF16),
                   jax.ShapeDtypeStruct((N_KV_HEADS, LANES, n), BF16),
                   jax.ShapeDtypeStruct((n, N_KV_HEADS * LANES), BF16),
                   jax.ShapeDtypeStruct((n_p, 2 * nk), F32)],
        compiler_params=_params(("arbitrary",)),
        name="qkv",
    )(xp, xs, mod, g, w, qn, kn, eye4, cos, sin)


def _attn_group(q2a, q2b, kt, vv, bias, n_bias, sinks):
    tq = q2a.shape[0]
    lane = lax.broadcasted_iota(I32, (tq, LANES), 1)
    lo = lane < HEAD_DIM
    zero = jnp.zeros_like(q2a)
    qs = jnp.concatenate([jnp.where(lo, q2a, zero), jnp.where(lo, zero, q2a),
                          jnp.where(lo, q2b, zero), jnp.where(lo, zero, q2b)], axis=0)
    s = jnp.dot(qs, kt, preferred_element_type=F32)
    ps, dens = [], []
    for h in range(4):
        sh = s[h * tq:(h + 1) * tq]
        if bias is not None:
            sh = jnp.concatenate([sh[:, :n_bias] + bias, sh[:, n_bias:]], axis=1)
        m = jnp.maximum(jnp.max(sh, axis=-1, keepdims=True), sinks[h])
        p = jnp.exp2(sh - m)
        dens.append(jnp.sum(p, axis=-1, keepdims=True) + jnp.exp2(sinks[h] - m))
        ps.append(p.astype(BF16))
    o = (jnp.dot(jnp.concatenate(ps, axis=0), vv, preferred_element_type=F32)
         / jnp.concatenate(dens, axis=0))
    oa = jnp.where(lo, o[0:tq], o[tq:2 * tq])
    ob = jnp.where(lo, o[2 * tq:3 * tq], o[3 * tq:4 * tq])
    return oa, ob


def _ctx_attn_kernel(sink_ref, q_ref, kt_ref, vv_ref, x_ref, mod_ref, wo_ref, o_ref, osc):
    for g in range(N_KV_HEADS):
        q2a = q_ref[:, (2 * g) * LANES:(2 * g + 1) * LANES]
        q2b = q_ref[:, (2 * g + 1) * LANES:(2 * g + 2) * LANES]
        sinks = [sink_ref[4 * g + j] * LOG2E for j in range(4)]
        oa, ob = _attn_group(q2a, q2b, kt_ref[g], vv_ref[:, g * LANES:(g + 1) * LANES], None, 0, sinks)
        osc[:, (2 * g) * LANES:(2 * g + 1) * LANES] = oa.astype(BF16)
        osc[:, (2 * g + 1) * LANES:(2 * g + 2) * LANES] = ob.astype(BF16)
    att = jnp.dot(osc[...], wo_ref[...], preferred_element_type=F32)
    o_ref[...] = x_ref[...] + mod_ref[2:3, :] * att


def _ctx_attention(sink, q, kt, vv, xp, seq, mod, wo):
    n_p, d = xp.shape
    nb = n_p // seq
    return pl.pallas_call(
        _ctx_attn_kernel,
        grid=(nb,),
        in_specs=[
            pl.BlockSpec(memory_space=pltpu.SMEM),
            pl.BlockSpec((seq, d), lambda b: (b, 0)),
            pl.BlockSpec((N_KV_HEADS, LANES, seq), lambda b: (0, 0, b)),
            pl.BlockSpec((seq, N_KV_HEADS * LANES), lambda b: (b, 0)),
            pl.BlockSpec((seq, d), lambda b: (b, 0)),
            pl.BlockSpec((None, 8, d), lambda b: (0, 0, 0)),
            _const_spec((d, d)),
        ],
        out_specs=pl.BlockSpec((seq, d), lambda b: (b, 0)),
        out_shape=jax.ShapeDtypeStruct((n_p, d), F32),
        scratch_shapes=[pltpu.VMEM((seq, d), BF16)],
        compiler_params=_params(("arbitrary",)),
        name="ctx_attention",
    )(sink, q, kt, vv, xp, mod, wo)


LAT_QB = 4


def _lat_attn_kernel(t_len, sink_ref, q_ref, ktp_ref, kto_ref, ktn_ref, vvp_ref, vvo_ref, vvn_ref,
                     ckt_ref, cvv_ref, x_ref, mod_ref, wo_ref, o_ref, osc):
    step = pl.program_id(1)
    tq = ATTN_BLOCK
    nloc = 3 * ATTN_BLOCK
    qi = lax.broadcasted_iota(I32, (tq, nloc), 0)
    kj = lax.broadcasted_iota(I32, (tq, nloc), 1)
    in_window = jnp.abs(qi + tq - kj) <= WINDOW
    for u in range(LAT_QB):
        n = step * LAT_QB + u
        kpos = (n - 1) * tq + kj
        bias = jnp.where(in_window & (kpos >= 0) & (kpos < t_len), 0.0, NEG_INF)
        rows = slice(u * tq, (u + 1) * tq)
        for g in range(N_KV_HEADS):
            vs = slice(g * LANES, (g + 1) * LANES)
            kts = [ktp_ref[g]] + [kto_ref[g, :, j * tq:(j + 1) * tq] for j in range(LAT_QB)] + [ktn_ref[g]]
            vvs = [vvp_ref[:, vs]] + [vvo_ref[j * tq:(j + 1) * tq, vs] for j in range(LAT_QB)] + [vvn_ref[:, vs]]
            kt = jnp.concatenate(kts[u:u + 3] + [ckt_ref[g]], axis=1)
            vv = jnp.concatenate(vvs[u:u + 3] + [cvv_ref[:, vs]], axis=0)
            q2a = q_ref[rows, (2 * g) * LANES:(2 * g + 1) * LANES]
            q2b = q_ref[rows, (2 * g + 1) * LANES:(2 * g + 2) * LANES]
            sinks = [sink_ref[4 * g + j] * LOG2E for j in range(4)]
            oa, ob = _attn_group(q2a, q2b, kt, vv, bias, nloc, sinks)
            osc[rows, (2 * g) * LANES:(2 * g + 1) * LANES] = oa.astype(BF16)
            osc[rows, (2 * g + 1) * LANES:(2 * g + 2) * LANES] = ob.astype(BF16)
    att = jnp.dot(osc[...], wo_ref[...], preferred_element_type=F32)
    o_ref[...] = x_ref[...] + mod_ref[2:3, :] * att


def _lat_attention(sink, q, kt, vv, ckt, cvv, xs, n_p, t_len, mod, wo):
    n_s, d = xs.shape
    db = n_s // t_len
    tq = ATTN_BLOCK
    ts = LAT_QB * tq
    nblk = t_len // tq
    nstep = t_len // ts
    assert n_p % ts == 0 and t_len % ts == 0
    off = n_p // tq
    npast = ckt.shape[3]
    g4 = N_KV_HEADS

    def nbr(delta):
        return lambda b, m: off + b * nblk + jnp.clip(m * LAT_QB + delta, 0, nblk - 1)

    own = lambda b, m: n_p // ts + b * nstep + m
    return pl.pallas_call(
        functools.partial(_lat_attn_kernel, t_len),
        grid=(db, nstep),
        in_specs=[
            pl.BlockSpec(memory_space=pltpu.SMEM),
            pl.BlockSpec((ts, d), lambda b, m: (own(b, m), 0)),
            pl.BlockSpec((g4, LANES, tq), lambda b, m: (0, 0, nbr(-1)(b, m))),
            pl.BlockSpec((g4, LANES, ts), lambda b, m: (0, 0, own(b, m))),
            pl.BlockSpec((g4, LANES, tq), lambda b, m: (0, 0, nbr(LAT_QB)(b, m))),
            pl.BlockSpec((tq, g4 * LANES), lambda b, m: (nbr(-1)(b, m), 0)),
            pl.BlockSpec((ts, g4 * LANES), lambda b, m: (own(b, m), 0)),
            pl.BlockSpec((tq, g4 * LANES), lambda b, m: (nbr(LAT_QB)(b, m), 0)),
            pl.BlockSpec((None, g4, LANES, npast), lambda b, m: (b, 0, 0, 0)),
            pl.BlockSpec((None, npast, g4 * LANES), lambda b, m: (b, 0, 0)),
            pl.BlockSpec((ts, d), lambda b, m: (b * nstep + m, 0)),
            pl.BlockSpec((None, 8, d), lambda b, m: (1 + b, 0, 0)),
            _const_spec((d, d)),
        ],
        out_specs=pl.BlockSpec((ts, d), lambda b, m: (b * nstep + m, 0)),
        out_shape=jax.ShapeDtypeStruct((n_s, d), F32),
        scratch_shapes=[pltpu.VMEM((ts, d), BF16)],
        compiler_params=_params(("arbitrary", "arbitrary")),
        name="lat_attention",
    )(sink, q, kt, kt, kt, vv, vv, vv, ckt, cvv, xs, mod, wo)


def _ffn_kernel(na, fc, xa_ref, xb_ref, mod_ref, g_ref, wg_ref, wu_ref, wd_ref, o_ref, acc):
    i = pl.program_id(0)
    x = jnp.where(i < na, xa_ref[...], xb_ref[...])
    a = _adaln(x, g_ref[...], mod_ref[3:4, :], mod_ref[4:5, :]).astype(BF16)
    nf = wg_ref.shape[1] // fc
    for f in range(nf):
        sl = slice(f * fc, (f + 1) * fc)
        hg = jnp.dot(a, wg_ref[:, sl], preferred_element_type=F32)
        hu = jnp.dot(a, wu_ref[:, sl], preferred_element_type=F32)
        act = (_silu(hg) * hu).astype(BF16)
        part = jnp.dot(act, wd_ref[sl, :], preferred_element_type=F32)
        if f == 0:
            acc[...] = part
        else:
            acc[...] += part
    o_ref[...] = x + mod_ref[5:6, :] * acc[...]


def _ffn(hp, hs, t_len, mod, g, wg, wu, wd):
    n_p, n_s = hp.shape[0], hs.shape[0]
    d, dff = wg.shape
    tm = _row_tile(n_p, t_len)
    na, nb = n_p // tm, n_s // tm
    nt_s = t_len // tm
    fc = 256
    assert dff % fc == 0
    mod_idx = lambda i: (jnp.where(i < na, 0, 1 + (i - na) // nt_s), 0, 0)
    return pl.pallas_call(
        functools.partial(_ffn_kernel, na, fc),
        grid=(na + nb,),
        in_specs=[
            pl.BlockSpec((tm, d), lambda i: (jnp.minimum(i, na - 1), 0)),
            pl.BlockSpec((tm, d), lambda i: (jnp.maximum(i - na, 0), 0)),
            pl.BlockSpec((None, 8, d), mod_idx),
            _const_spec((1, d)),
            _const_spec((d, dff)),
            _const_spec((d, dff)),
            _const_spec((dff, d)),
        ],
        out_specs=pl.BlockSpec((tm, d), lambda i: (i, 0)),
        out_shape=jax.ShapeDtypeStruct((n_p + n_s, d), F32),
        scratch_shapes=[pltpu.VMEM((tm, d), F32)],
        compiler_params=_params(("arbitrary",)),
        name="ffn",
    )(hp, hs, mod, g, wg, wu, wd)


def _split2(x):
    hi = x.astype(BF16)
    lo = (x - hi.astype(F32)).astype(BF16)
    return hi, lo


def _gla_in_kernel(nt_s, x_ref, mod_ref, g_ref, w_ref, w1_ref, w2_ref, gb_ref, tri_ref,
                   qk_out, v_out, r_out, b_out):
    x = x_ref[...]
    tm = x.shape[0]
    a = _adaln(x, g_ref[...], mod_ref[0:1, :], mod_ref[1:2, :]).astype(BF16)
    hk = GLA_HEADS * GLA_DK
    hv = GLA_HEADS * GLA_DV
    q = jnp.dot(a, w_ref[:, 0:hk], preferred_element_type=F32)
    qk_out[:, 0:hk] = q * (GLA_DK ** -0.5)
    qk_out[:, hk:2 * hk] = jnp.dot(a, w_ref[:, hk:2 * hk], preferred_element_type=F32)
    v_out[...] = jnp.dot(a, w_ref[:, 2 * hk:2 * hk + hv], preferred_element_type=F32)
    r_out[...] = jnp.dot(a, w_ref[:, 2 * hk + hv:2 * hk + 2 * hv], preferred_element_type=F32).astype(BF16)
    z1 = jnp.dot(a, w1_ref[...], preferred_element_type=F32)
    z1h = z1.astype(BF16)
    z1l = (z1 - z1h.astype(F32)).astype(BF16)
    z = (jnp.dot(z1h, w2_ref[...], preferred_element_type=F32)
         + jnp.dot(z1l, w2_ref[...], preferred_element_type=F32) + gb_ref[...])
    gate = (jnp.minimum(z, 0.0) - jnp.log(1.0 + jnp.exp(-jnp.abs(z)))) * (1.0 / GLA_GATE_TAU)
    c = GLA_CHUNK
    tri = tri_ref[...]
    for j in range(tm // c):
        rows = slice(j * c, (j + 1) * c)
        for dr in range(2):
            cols = slice(dr * hk, (dr + 1) * hk)
            hi, lo = _split2(gate[rows, cols])
            b_out[rows, cols] = (jnp.dot(tri[dr], hi, preferred_element_type=F32)
                                 + jnp.dot(tri[dr], lo, preferred_element_type=F32))


def _gla_in(h, n_p, t_len, mod, g, w_in, w1, w2, gate_b):
    n, d = h.shape
    tm = _row_tile(n_p, t_len)
    na = n_p // tm
    nt_s = t_len // tm
    hk, hv = GLA_HEADS * GLA_DK, GLA_HEADS * GLA_DV
    c = GLA_CHUNK
    lower = jnp.tril(jnp.ones((c, c), F32))
    tri = jnp.stack([lower, lower.T]).astype(BF16)
    mod_idx = lambda i: (jnp.where(i < na, 0, 1 + (i - na) // nt_s), 0, 0)
    row = lambda i: (i, 0)
    return pl.pallas_call(
        functools.partial(_gla_in_kernel, nt_s),
        grid=(n // tm,),
        in_specs=[
            pl.BlockSpec((tm, d), row),
            pl.BlockSpec((None, 8, d), mod_idx),
            _const_spec((1, d)),
            _const_spec(w_in.shape),
            _const_spec(w1.shape),
            _const_spec(w2.shape),
            _const_spec((1, 2 * hk)),
            _const_spec((2, c, c)),
        ],
        out_specs=[pl.BlockSpec((tm, 2 * hk), row), pl.BlockSpec((tm, hv), row),
                   pl.BlockSpec((tm, hv), row), pl.BlockSpec((tm, 2 * hk), row)],
        out_shape=[jax.ShapeDtypeStruct((n, 2 * hk), F32), jax.ShapeDtypeStruct((n, hv), F32),
                   jax.ShapeDtypeStruct((n, hv), BF16), jax.ShapeDtypeStruct((n, 2 * hk), F32)],
        compiler_params=_params(("arbitrary",)),
        name="gla_in",
    )(h, mod, g, w_in, w1, w2, gate_b, tri)


def _gla_direction(qk_ref, v_ref, b_ref, st, o_ref, forward, qd_s, vb_s, att_s, upd_s, dec_s):
    c = GLA_CHUNK
    ncb = qk_ref.shape[0] // c
    hk = GLA_HEADS * GLA_DK
    dk, dv = GLA_DK, GLA_DV
    ri = lax.broadcasted_iota(I32, (c, c), 0)
    ci_ = lax.broadcasted_iota(I32, (c, c), 1)
    keep = (ci_ <= ri) if forward else (ci_ >= ri)

    vb_s[...] = v_ref[...].astype(BF16)
    for ch in range(ncb):
        rows = slice(ch * c, (ch + 1) * c)
        b = b_ref[rows, :]
        if forward:
            b_end, b_mid = b[c - 1:c, :], b[c // 2 - 1:c // 2, :]
        else:
            b_end, b_mid = b[0:1, :], b[c // 2:c // 2 + 1, :]
        qa = qk_ref[rows, 0:hk] * jnp.exp(b - b_mid)
        ka = qk_ref[rows, hk:2 * hk] * jnp.exp(b_mid - b)
        qd_s[rows, :] = (qa * jnp.exp(b_mid)).astype(BF16)
        kd = (ka * jnp.exp(b_end - b_mid)).astype(BF16)
        dec_s[ch:ch + 1, :] = jnp.exp(b_end)
        qab, kab = qa.astype(BF16), ka.astype(BF16)
        for h in range(GLA_HEADS):
            ks = slice(h * dk, (h + 1) * dk)
            vs = slice(h * dv, (h + 1) * dv)
            att = lax.dot_general(qab[:, ks], kab[:, ks], (((1,), (1,)), ((), ())),
                                  preferred_element_type=F32)
            att_s[ch, h] = jnp.where(keep, att, 0.0).astype(BF16)
            upd_s[ch, h] = lax.dot_general(vb_s[rows, vs], kd[:, ks], (((0,), (0,)), ((), ())),
                                           preferred_element_type=F32)

    for ch in (range(ncb) if forward else reversed(range(ncb))):
        rows = slice(ch * c, (ch + 1) * c)
        for h in range(GLA_HEADS):
            ks = slice(h * dk, (h + 1) * dk)
            vs = slice(h * dv, (h + 1) * dv)
            s = st[h]
            o = lax.dot_general(qd_s[rows, ks], s.astype(BF16), (((1,), (1,)), ((), ())),
                                preferred_element_type=F32)
            o = o + jnp.dot(att_s[ch, h], vb_s[rows, vs], preferred_element_type=F32)
            o_ref[rows, vs] = o.astype(o_ref.dtype)
            st[h] = s * dec_s[ch:ch + 1, ks] + upd_s[ch, h]


def _gla_scan_kernel(qkf_ref, vf_ref, bf_ref, qkb_ref, vb_ref, bb_ref, sf0_ref, sb0_ref,
                     of_ref, ob_ref, sf_ref, sb_ref, stf, stb, qd_s, vb_s, att_s, upd_s, dec_s):
    i = pl.program_id(1)
    nblk = pl.num_programs(1)

    @pl.when(i == 0)
    def _():
        for h in range(GLA_HEADS):
            stf[h] = sf0_ref[h].T
            stb[h] = sb0_ref[h].T

    _gla_direction(qkf_ref, vf_ref, bf_ref, stf, of_ref, True, qd_s, vb_s, att_s, upd_s, dec_s)
    _gla_direction(qkb_ref, vb_ref, bb_ref, stb, ob_ref, False, qd_s, vb_s, att_s, upd_s, dec_s)

    @pl.when(i == nblk - 1)
    def _():
        for h in range(GLA_HEADS):
            sf_ref[h] = stf[h].T
            sb_ref[h] = stb[h].T


def _gla_scan(qk, v, bc, sf0, sb0, b_off, nb, t_len):
    n = qk.shape[0]
    assert n % t_len == 0
    ns = n // t_len
    tb = _row_tile(t_len)
    nblk = t_len // tb
    qk3 = qk.reshape(ns, t_len, qk.shape[1])
    v3 = v.reshape(ns, t_len, v.shape[1])
    bc3 = bc.reshape(ns, t_len, bc.shape[1])
    h, dk, dv = GLA_HEADS, GLA_DK, GLA_DV
    hk = h * dk
    st_spec = pl.BlockSpec((None, h, dk, dv), lambda b, i: (b, 0, 0, 0))
    fwd = lambda b, i: (b_off + b, i, 0)
    bwd = lambda b, i: (b_off + b, nblk - 1 - i, 0)
    return pl.pallas_call(
        _gla_scan_kernel,
        grid=(nb, nblk),
        in_specs=[
            pl.BlockSpec((None, tb, 2 * hk), fwd),
            pl.BlockSpec((None, tb, h * dv), fwd),
            pl.BlockSpec((None, tb, hk), fwd),
            pl.BlockSpec((None, tb, 2 * hk), bwd),
            pl.BlockSpec((None, tb, h * dv), bwd),
            pl.BlockSpec((None, tb, hk), lambda b, i: (b_off + b, nblk - 1 - i, 1)),
            st_spec, st_spec,
        ],
        out_specs=[pl.BlockSpec((None, tb, h * dv), lambda b, i: (b, i, 0)),
                   pl.BlockSpec((None, tb, h * dv), lambda b, i: (b, nblk - 1 - i, 0)),
                   st_spec, st_spec],
        out_shape=[jax.ShapeDtypeStruct((nb, t_len, h * dv), BF16),
                   jax.ShapeDtypeStruct((nb, t_len, h * dv), BF16),
                   jax.ShapeDtypeStruct((nb, h, dk, dv), F32),
                   jax.ShapeDtypeStruct((nb, h, dk, dv), F32)],
        scratch_shapes=[pltpu.VMEM((h, dv, dk), F32), pltpu.VMEM((h, dv, dk), F32),
                        pltpu.VMEM((tb, hk), BF16), pltpu.VMEM((tb, h * dv), BF16),
                        pltpu.VMEM((tb // GLA_CHUNK, h, GLA_CHUNK, GLA_CHUNK), BF16),
                        pltpu.VMEM((tb // GLA_CHUNK, h, dv, dk), F32),
                        pltpu.VMEM((max(tb // GLA_CHUNK, 8), hk), F32)],
        compiler_params=_params(("arbitrary", "arbitrary")),
        name="gla_scan",
    )(qk3, v3, bc3, qk3, v3, bc3, sf0, sb0)


def _gla_out_kernel(na, ofa_ref, oba_ref, ofb_ref, obb_ref, r_ref, h_ref, mod_ref, on_ref, wo_ref,
                    g2_ref, rw_ref, tri_ref, h_out, f_out, route_out, wt_out, cnt_out, run):
    i = pl.program_id(0)
    is_ctx = i < na
    o = (jnp.where(is_ctx, ofa_ref[...], ofb_ref[...]).astype(F32)
         + jnp.where(is_ctx, oba_ref[...], obb_ref[...]).astype(F32))
    tm = o.shape[0]
    dv = GLA_DV
    parts = []
    for hh in range(GLA_HEADS):
        oh = o[:, hh * dv:(hh + 1) * dv]
        parts.append(_rms(oh) * on_ref[...])
    on = jnp.concatenate(parts, axis=1)
    r = r_ref[...].astype(F32)
    gated = (on * _silu(r)).astype(BF16)
    h = h_ref[...] + mod_ref[2:3, :] * jnp.dot(gated, wo_ref[...], preferred_element_type=F32)
    h_out[...] = h
    f = _adaln(h, g2_ref[...], mod_ref[3:4, :], mod_ref[4:5, :])
    f_out[...] = _pack_halves(f)
    fh = f.astype(BF16)
    fl = (f - fh.astype(F32)).astype(BF16)
    logits = (jnp.dot(fh, rw_ref[0], preferred_element_type=F32)
              + jnp.dot(fl, rw_ref[0], preferred_element_type=F32)
              + jnp.dot(fh, rw_ref[1], preferred_element_type=F32))
    lane = lax.broadcasted_iota(I32, (tm, LANES), 1)
    lane_f = lane.astype(F32)
    logits = jnp.where(lane < N_EXPERTS, logits, -jnp.inf)
    m1 = jnp.max(logits, axis=-1, keepdims=True)
    i1 = jnp.min(jnp.where(logits == m1, lane_f, float(LANES)), axis=-1, keepdims=True)
    rest = jnp.where(lane_f == i1, -jnp.inf, logits)
    m2 = jnp.max(rest, axis=-1, keepdims=True)
    i2 = jnp.min(jnp.where(rest == m2, lane_f, float(LANES)), axis=-1, keepdims=True)
    e2 = jnp.exp(m2 - m1)
    w1 = 1.0 / (1.0 + e2)
    w2 = e2 / (1.0 + e2)
    wt_out[...] = jnp.where(lane == 0, w1, jnp.where(lane == 1, w2, 0.0))
    sel1 = lane_f == i1
    sel2 = lane_f == i2
    picked = jnp.where(sel1, 1.0, jnp.where(sel2, 1.0, 0.0))

    @pl.when(i == 0)
    def _():
        run[...] = jnp.zeros_like(run)

    before = jnp.dot(tri_ref[...], picked.astype(BF16), preferred_element_type=F32) + run[0:1, :]
    rank1 = jnp.sum(jnp.where(sel1, before, 0.0), axis=-1, keepdims=True)
    rank2 = jnp.sum(jnp.where(sel2, before, 0.0), axis=-1, keepdims=True)
    run[...] = run[...] + jnp.sum(picked, axis=0, keepdims=True)
    cnt_out[...] = run[...]
    routing = jnp.where(lane == 0, i1, jnp.where(lane == 1, i2,
                        jnp.where(lane == 2, rank1, jnp.where(lane == 3, rank2, 0.0))))
    route_out[...] = routing.T[0:8, :]


def _gla_out(of_p, ob_p, of_s, ob_s, r, h, n_p, t_len, mod, out_norm, wo, g2, router_w):
    n, d = h.shape
    tm = _row_tile(n_p, t_len)
    na = n_p // tm
    nt_s = t_len // tm
    mod_idx = lambda i: (jnp.where(i < na, 0, 1 + (i - na) // nt_s), 0, 0)
    row = lambda i: (i, 0)
    ctx_row = lambda i: (jnp.minimum(i, na - 1), 0)
    lat_row = lambda i: (jnp.maximum(i - na, 0), 0)
    tri = jnp.tril(jnp.ones((tm, tm), F32), -1).astype(BF16)
    return pl.pallas_call(
        functools.partial(_gla_out_kernel, na),
        grid=(n // tm,),
        in_specs=[
            pl.BlockSpec((tm, d), ctx_row),
            pl.BlockSpec((tm, d), ctx_row),
            pl.BlockSpec((tm, d), lat_row),
            pl.BlockSpec((tm, d), lat_row),
            pl.BlockSpec((tm, d), row),
            pl.BlockSpec((tm, d), row),
            pl.BlockSpec((None, 8, d), mod_idx),
            _const_spec((1, GLA_DV)),
            _const_spec((d, d)),
            _const_spec((1, d)),
            _const_spec((2, d, LANES)),
            _const_spec((tm, tm)),
        ],
        out_specs=[pl.BlockSpec((tm, d), row), pl.BlockSpec((tm, d // 2), row),
                   pl.BlockSpec((8, tm), lambda i: (0, i)), pl.BlockSpec((tm, LANES), row),
                   _const_spec((8, LANES))],
        out_shape=[jax.ShapeDtypeStruct((n, d), F32), jax.ShapeDtypeStruct((n, d // 2), jnp.uint32),
                   jax.ShapeDtypeStruct((8, n), F32), jax.ShapeDtypeStruct((n, LANES), F32),
                   jax.ShapeDtypeStruct((8, LANES), F32)],
        scratch_shapes=[pltpu.VMEM((8, LANES), F32)],
        compiler_params=_params(("arbitrary",)),
        name="gla_out_router",
    )(of_p, ob_p, of_s, ob_s, r, h, mod, out_norm, wo, g2, router_w, tri)


SC_CORES = 2
SC_SUBCORES = 16
SC_CHUNK = 64


def _sc_gather_rows(table, idx):
    nw = SC_CORES * SC_SUBCORES
    b = idx.shape[0]
    d = table.shape[1]
    assert b % (nw * SC_CHUNK) == 0
    per_w = b // nw
    n_chunks = per_w // SC_CHUNK
    mesh = plsc.VectorSubcoreMesh(core_axis_name="c", subcore_axis_name="s",
                                  num_cores=SC_CORES, num_subcores=SC_SUBCORES)

    assert n_chunks % 2 == 0

    def body(table_hbm, idx_hbm, out_hbm, idx_v, rows_v, gsem, wsem):
        wid = lax.axis_index("s") * SC_CORES + lax.axis_index("c")
        base = wid * per_w
        pltpu.sync_copy(idx_hbm.at[wid], idx_v)

        def gather(j, slot):
            return pltpu.make_async_copy(table_hbm.at[idx_v.at[j]], rows_v.at[slot], gsem.at[slot])

        def write(j, slot):
            return pltpu.make_async_copy(rows_v.at[slot], out_hbm.at[pl.ds(base + j * SC_CHUNK, SC_CHUNK)],
                                         wsem.at[slot])

        gather(0, 0).start()

        @pl.loop(0, n_chunks, step=2)
        def _(j):
            for slot in range(2):
                jj = j + slot
                gather(jj, slot).wait()

                @pl.when(jj + 1 < n_chunks)
                def _():
                    @pl.when(jj >= 1)
                    def _():
                        write(jj - 1, 1 - slot).wait()

                    gather(jj + 1, 1 - slot).start()

                write(jj, slot).start()

        write(n_chunks - 2, 0).wait()
        write(n_chunks - 1, 1).wait()

    return pl.kernel(
        body,
        out_type=jax.ShapeDtypeStruct((b, d), table.dtype),
        mesh=mesh,
        scratch_types=[pltpu.VMEM((n_chunks, SC_CHUNK), I32),
                       pltpu.VMEM((2, SC_CHUNK, d), table.dtype),
                       pltpu.SemaphoreType.DMA((2,)),
                       pltpu.SemaphoreType.DMA((2,))],
        name="sc_gather_rows",
    )(table, idx.reshape(nw, n_chunks, SC_CHUNK))


def _sc_scatter_rows(rows, idx2, p):
    nw = SC_CORES * SC_SUBCORES
    n, d = rows.shape
    assert n % (nw * SC_CHUNK) == 0
    per_w = n // nw
    n_chunks = per_w // SC_CHUNK
    mesh = plsc.VectorSubcoreMesh(core_axis_name="c", subcore_axis_name="s",
                                  num_cores=SC_CORES, num_subcores=SC_SUBCORES)

    assert n_chunks % 2 == 0

    def body(rows_hbm, idx0_hbm, idx1_hbm, out_hbm, idx0_v, idx1_v, rows_v, rsem, ssem):
        wid = lax.axis_index("s") * SC_CORES + lax.axis_index("c")
        base = wid * per_w
        pltpu.sync_copy(idx0_hbm.at[wid], idx0_v)
        pltpu.sync_copy(idx1_hbm.at[wid], idx1_v)

        def read(j, slot):
            return pltpu.make_async_copy(rows_hbm.at[pl.ds(base + j * SC_CHUNK, SC_CHUNK)], rows_v.at[slot],
                                         rsem.at[slot])

        read(0, 0).start()

        @pl.loop(0, n_chunks, step=2)
        def _(j):
            for slot in range(2):
                jj = j + slot
                read(jj, slot).wait()

                @pl.when(jj + 1 < n_chunks)
                def _():
                    read(jj + 1, 1 - slot).start()

                s0 = pltpu.make_async_copy(rows_v.at[slot], out_hbm.at[idx0_v.at[jj]], ssem.at[0])
                s1 = pltpu.make_async_copy(rows_v.at[slot], out_hbm.at[idx1_v.at[jj]], ssem.at[1])
                s0.start()
                s1.start()
                s0.wait()
                s1.wait()

    return pl.kernel(
        body,
        out_type=jax.ShapeDtypeStruct((p, d), rows.dtype),
        mesh=mesh,
        scratch_types=[pltpu.VMEM((n_chunks, SC_CHUNK), I32),
                       pltpu.VMEM((n_chunks, SC_CHUNK), I32),
                       pltpu.VMEM((2, SC_CHUNK, d), rows.dtype),
                       pltpu.SemaphoreType.DMA((2,)),
                       pltpu.SemaphoreType.DMA((2,))],
        name="sc_scatter_rows",
    )(rows, idx2[0].reshape(nw, n_chunks, SC_CHUNK), idx2[1].reshape(nw, n_chunks, SC_CHUNK))


EXPERT_SUB = 512


def _pack_halves(x):
    k = x.shape[1] // 2
    lo = lax.bitcast_convert_type(x[:, :k].astype(BF16).astype(F32), jnp.uint32)
    hi = lax.bitcast_convert_type(x[:, k:].astype(BF16).astype(F32), jnp.uint32)
    return (lo >> 16) | (hi & jnp.uint32(0xFFFF0000))


def _unpack_halves(w):
    lo = lax.bitcast_convert_type(w << 16, F32).astype(BF16)
    hi = lax.bitcast_convert_type(w & jnp.uint32(0xFFFF0000), F32).astype(BF16)
    return lo, hi


def _expert_kernel(te_ref, ts_ref, nu_ref, nv_ref, x_ref, wg_ref, wu_ref, wd_ref, y_ref, acc, xb):
    i = pl.program_id(0)
    f = pl.program_id(1)
    nf = pl.num_programs(1)
    tm = x_ref.shape[0]
    half = x_ref.shape[1]
    sub = EXPERT_SUB
    nvalid = nv_ref[i]

    @pl.when(f == 0)
    def _():
        acc[...] = jnp.zeros_like(acc)

    @pl.when((f == 0) & (nvalid > 0))
    def _():
        rid = lax.broadcasted_iota(I32, (tm, half), 0)
        lo, hi = _unpack_halves(x_ref[...])
        zero = jnp.zeros_like(lo)
        xb[:, 0:half] = jnp.where(rid < nvalid, lo, zero)
        xb[:, half:2 * half] = jnp.where(rid < nvalid, hi, zero)

    def compute(rows):
        x = xb[rows, :]
        hg = jnp.dot(x, wg_ref[...].astype(BF16), preferred_element_type=F32)
        hu = jnp.dot(x, wu_ref[...].astype(BF16), preferred_element_type=F32)
        act = (_silu(hg) * hu).astype(BF16)
        acc[rows, :] += jnp.dot(act, wd_ref[...].astype(BF16), preferred_element_type=F32)

    @pl.when(nvalid > sub)
    def _():
        compute(slice(0, tm))

    @pl.when((nvalid > 0) & (nvalid <= sub))
    def _():
        compute(slice(0, sub))

    @pl.when(f == nf - 1)
    def _():
        y_ref[...] = _pack_halves(acc[...])


def _experts(x, tile_expert, tile_src, n_used, n_valid, wg, wu, wd, tm, tf):
    p, half = x.shape
    d = 2 * half
    ne, _, dff = wg.shape
    nf = dff // tf
    assert dff % tf == 0 and tm % EXPERT_SUB == 0

    def fidx(i, f, nu):
        return jnp.where(i < nu[0], f, nf - 1)

    grid_spec = pltpu.PrefetchScalarGridSpec(
        num_scalar_prefetch=4,
        grid=(p // tm, nf),
        in_specs=[
            pl.BlockSpec((tm, half), lambda i, f, te, ts, nu, nv: (ts[i], 0)),
            pl.BlockSpec((None, d, tf), lambda i, f, te, ts, nu, nv: (te[i], 0, fidx(i, f, nu))),
            pl.BlockSpec((None, d, tf), lambda i, f, te, ts, nu, nv: (te[i], 0, fidx(i, f, nu))),
            pl.BlockSpec((None, tf, d), lambda i, f, te, ts, nu, nv: (te[i], fidx(i, f, nu), 0)),
        ],
        out_specs=pl.BlockSpec((tm, half), lambda i, f, te, ts, nu, nv: (i, 0)),
        scratch_shapes=[pltpu.VMEM((tm, d), F32), pltpu.VMEM((tm, d), BF16)],
    )
    return pl.pallas_call(
        _expert_kernel,
        grid_spec=grid_spec,
        out_shape=jax.ShapeDtypeStruct((p, half), jnp.uint32),
        compiler_params=_params(("arbitrary", "arbitrary"), 60 * 1024 * 1024),
        name="moe_experts",
    )(tile_expert, tile_src, n_used, n_valid, x, wg, wu, wd)


def _combine_kernel(wt_ref, h_ref, mod_ref, y0_ref, y1_ref, o_ref):
    wt = wt_ref[...]
    half = y0_ref.shape[1]
    lo0, hi0 = _unpack_halves(y0_ref[...])
    lo1, hi1 = _unpack_halves(y1_ref[...])
    w0, w1 = wt[:, 0:1], wt[:, 1:2]
    h = h_ref[...]
    gate = mod_ref[5:6, :]
    o_ref[:, 0:half] = h[:, 0:half] + gate[:, 0:half] * (w0 * lo0.astype(F32) + w1 * lo1.astype(F32))
    o_ref[:, half:] = h[:, half:] + gate[:, half:] * (w0 * hi0.astype(F32) + w1 * hi1.astype(F32))


def _combine(yg, wt, h, mod, row_off, n_rows, n_mod, mod_off, tm):
    d = h.shape[1]
    nt = n_rows // tm
    toff = row_off // tm
    per_mod = n_rows // n_mod // tm
    return pl.pallas_call(
        _combine_kernel,
        grid=(nt,),
        in_specs=[
            pl.BlockSpec((tm, LANES), lambda i: (toff + i, 0)),
            pl.BlockSpec((tm, d), lambda i: (toff + i, 0)),
            pl.BlockSpec((None, 8, d), lambda i: (mod_off + i // per_mod, 0, 0)),
            pl.BlockSpec((None, tm, d // 2), lambda i: (0, i, 0)),
            pl.BlockSpec((None, tm, d // 2), lambda i: (1, i, 0)),
        ],
        out_specs=pl.BlockSpec((tm, d), lambda i: (i, 0)),
        out_shape=jax.ShapeDtypeStruct((n_rows, d), F32),
        compiler_params=_params(("arbitrary",)),
        name="moe_combine",
    )(wt, h, mod, yg, yg)


def _route(route, counts, n, tm):
    cnt = counts[0, :N_EXPERTS].astype(I32)
    padded = ((cnt + tm - 1) // tm) * tm
    gend = jnp.cumsum(padded)
    goff = gend - padded
    e = route[0:2].astype(I32)
    rank = route[2:4].astype(I32)
    onehot = e[:, :, None] == jnp.arange(N_EXPERTS, dtype=I32)[None, None, :]
    dest = jnp.sum(jnp.where(onehot, goff[None, None, :], 0), axis=-1) + rank
    p = 2 * n + N_EXPERTS * tm
    n_used = gend[-1] // tm
    tiles = jnp.arange(p // tm, dtype=I32)
    tile_src = jnp.minimum(tiles, n_used - 1)
    tile_expert = jnp.minimum(jnp.sum((gend[None, :] <= (tile_src * tm)[:, None]).astype(I32), axis=1),
                              N_EXPERTS - 1)
    used = goff[tile_expert] + cnt[tile_expert]
    n_valid = jnp.where(tiles < n_used, jnp.clip(used - tiles * tm, 0, tm), 0).astype(I32)
    return dest, p, tile_expert, tile_src, n_used.reshape(1).astype(I32), n_valid


def kernel(x_prompt, x_sample, cache_k, cache_v, state_fwd, state_bwd, c, c_ctx, ada_w, ada_b, norm1_g, norm2_g, attn_w_qkv, attn_q_norm, attn_k_norm, attn_sink, attn_w_o, gla_w_in, gla_gate_w1, gla_gate_w2, gla_gate_b, gla_out_norm, gla_w_o, ffn_w_gate, ffn_w_up, ffn_w_down, moe_router, moe_w_gate, moe_w_up, moe_w_down):
    bp, seq, d = x_prompt.shape
    db, t_len, _ = x_sample.shape
    n_p, n_s = bp * seq, db * t_len
    n = n_p + n_s
    xp = x_prompt.reshape(n_p, d)
    xs = x_sample.reshape(n_s, d)

    cond = jnp.concatenate([c_ctx[None, :], c], axis=0)
    assert cond.shape[0] <= 8
    cond_t = jnp.pad(cond, ((0, 8 - cond.shape[0]), (0, 0))).T
    mods = _modulation(cond_t, cond.shape[0], ada_w, ada_b)

    nk = N_KV_HEADS * HEAD_DIM
    qn = jnp.tile(attn_q_norm[0], N_HEADS)[None, :]
    kn = jnp.tile(attn_k_norm[0], N_KV_HEADS)[None, :]
    q, kt, vv, kv = _qkv(xp, xs, t_len, mods[0], norm1_g[0][None, :], attn_w_qkv[0].astype(BF16), qn, kn)
    wo0 = attn_w_o[0].astype(BF16)
    sink = attn_sink[0]
    hp = _ctx_attention(sink, q, kt, vv, xp, seq, mods[0], wo0)
    ck = cache_k[:, 0].astype(BF16)
    cv = cache_v[:, 0].astype(BF16)
    ckt = jnp.concatenate([ck, ck], axis=-1).transpose(0, 2, 3, 1)
    cvv = jnp.concatenate([cv, cv], axis=-1).reshape(db, cv.shape[1], N_KV_HEADS * LANES)
    hs = _lat_attention(sink, q, kt, vv, ckt, cvv, xs, n_p, t_len, mods[0], wo0)
    new_k = kv[:n_p, :nk].reshape(bp, 1, seq, N_KV_HEADS, HEAD_DIM)
    new_v = kv[:n_p, nk:].reshape(bp, 1, seq, N_KV_HEADS, HEAD_DIM)
    h = _ffn(hp, hs, t_len, mods[0], norm2_g[0][None, :], ffn_w_gate[0].astype(BF16),
             ffn_w_up[0].astype(BF16), ffn_w_down[0].astype(BF16))

    hk = GLA_HEADS * GLA_DK
    rank = GLA_GATE_RANK
    w1 = jnp.concatenate([gla_gate_w1[0, 0], gla_gate_w1[0, 1]], axis=1)
    w1 = jnp.pad(w1, ((0, 0), (0, LANES - 2 * rank))).astype(BF16)
    w2 = jnp.zeros((LANES, 2 * hk), F32)
    w2 = w2.at[0:rank, 0:hk].set(gla_gate_w2[0, 0]).at[rank:2 * rank, hk:].set(gla_gate_w2[0, 1]).astype(BF16)
    gate_b = gla_gate_b[0].reshape(1, 2 * hk)
    qk, v, r, bc = _gla_in(h, n_p, t_len, mods[1], norm1_g[1][None, :], gla_w_in[0].astype(BF16),
                           w1, w2, gate_b)
    zeros = jnp.zeros((bp, GLA_HEADS, GLA_DK, GLA_DV), F32)
    of_p, ob_p, new_sf, new_sb = _gla_scan(qk, v, bc, zeros, zeros, 0, bp, seq)
    of_s, ob_s, _, _ = _gla_scan(qk, v, bc, state_fwd[:, 0], state_bwd[:, 0], n_p // t_len, db, t_len)
    rw = jnp.pad(moe_router[0], ((0, 0), (0, LANES - N_EXPERTS)))
    rw_hi = rw.astype(BF16)
    rw_lo = (rw - rw_hi.astype(F32)).astype(BF16)
    h, f, route, wt, counts = _gla_out(
        of_p.reshape(n_p, d), ob_p.reshape(n_p, d), of_s.reshape(n_s, d), ob_s.reshape(n_s, d), r, h,
        n_p, t_len, mods[1], gla_out_norm[0][None, :], gla_w_o[0].astype(BF16), norm2_g[1][None, :],
        jnp.stack([rw_hi, rw_lo]))

    tm_e = 1024
    dest_k, p_rows, tile_expert, tile_src, n_used, n_valid = _route(route, counts, n, tm_e)
    xg = _sc_scatter_rows(f, dest_k, p_rows)
    y = _experts(xg, tile_expert, tile_src, n_used, n_valid, moe_w_gate[0], moe_w_up[0],
                 moe_w_down[0], tm_e, 512)
    yg_p = _sc_gather_rows(y, dest_k[:, :n_p].reshape(2 * n_p)).reshape(2, n_p, d // 2)
    yg_s = _sc_gather_rows(y, dest_k[:, n_p:].reshape(2 * n_s)).reshape(2, n_s, d // 2)
    tm_c = _row_tile(n_p, t_len)
    y_p = _combine(yg_p, wt, h, mods[1], 0, n_p, 1, 0, tm_c)
    y_s = _combine(yg_s, wt, h, mods[1], n_p, n_s, db, 1, tm_c)

    return (y_p.reshape(bp, seq, d), y_s.reshape(db, t_len, d), new_k, new_v,
            new_sf[:, None], new_sb[:, None])
```

```python
import functools
import math

import jax
import jax.numpy as jnp
from jax import lax
from jax.experimental import pallas as pl
from jax.experimental.pallas import tpu as pltpu
from jax.experimental.pallas import tpu_sc as plsc

F32 = jnp.float32
BF16 = jnp.bfloat16
I32 = jnp.int32

D_MODEL = 1024
N_HEADS = 16
N_KV_HEADS = 4
HEAD_DIM = 64
GRID_W = 64
WINDOW = 128
ATTN_BLOCK = 128
ROPE_THETA = 10000.0
GLA_HEADS = 4
GLA_DK = 128
GLA_DV = 256
GLA_GATE_RANK = 16
GLA_GATE_TAU = 16.0
GLA_CHUNK = 64
N_EXPERTS = 8
NORM_EPS = 1e-6
NEG_INF = -1e30

LANES = 128
VMEM_LIMIT = 56 * 1024 * 1024


def _params(sem, vmem=VMEM_LIMIT):
    return pltpu.CompilerParams(dimension_semantics=sem, vmem_limit_bytes=vmem)


def _row_tile(*counts, cap=512):
    t = cap
    while any(c % t for c in counts):
        t //= 2
    assert t >= 8
    return t


def _rms(x):
    return x * lax.rsqrt(jnp.mean(x * x, axis=-1, keepdims=True) + NORM_EPS)


def _adaln(x, g, shift, scale):
    return _rms(x) * (g * (1.0 + scale)) + shift


def _silu(x):
    return x * jax.nn.sigmoid(x)


def _const_spec(shape):
    nd = len(shape)
    return pl.BlockSpec(shape, lambda *_: (0,) * nd)


def _mod_kernel(n_cond, ct_ref, w_ref, b_ref, o_ref):
    ct = ct_ref[...]
    s = _silu(ct)
    w = w_ref[...]
    rows = [jnp.sum(w * s[:, r:r + 1], axis=0, keepdims=True) for r in range(n_cond)]
    rows += [jnp.zeros_like(rows[0])] * (8 - n_cond)
    o_ref[...] = jnp.concatenate(rows, axis=0) + b_ref[...]


def _modulation(cond_t, n_cond, ada_w, ada_b):
    depth, d, n6 = ada_w.shape
    tn = 1024
    out = pl.pallas_call(
        functools.partial(_mod_kernel, n_cond),
        grid=(depth, n6 // tn),
        in_specs=[
            pl.BlockSpec((d, 8), lambda l, j: (0, 0)),
            pl.BlockSpec((None, d, tn), lambda l, j: (l, 0, j)),
            pl.BlockSpec((None, 1, tn), lambda l, j: (l, 0, j)),
        ],
        out_specs=pl.BlockSpec((None, 8, tn), lambda l, j: (l, 0, j)),
        out_shape=jax.ShapeDtypeStruct((depth, 8, n6), F32),
        compiler_params=_params(("arbitrary", "arbitrary")),
        name="modulation",
    )(cond_t, ada_w, ada_b.reshape(depth, 1, n6))
    m = out[:, :n_cond].reshape(depth, n_cond, 6, d)
    return jnp.pad(m, ((0, 0), (0, 0), (0, 2), (0, 0)))


LOG2E = math.log2(math.e)


def _dup_half(k2, half):
    lane = lax.broadcasted_iota(I32, k2.shape, 1)
    lo = lane < HEAD_DIM
    r = pltpu.roll(k2, HEAD_DIM, 1)
    return jnp.where(lo, k2, r) if half == 0 else jnp.where(lo, r, k2)


def _qkv_kernel(na, xa_ref, xb_ref, mod_ref, g_ref, w_ref, qn_ref, kn_ref, bd_ref,
                cos_ref, sin_ref, q_out, kt_out, vv_out, kv_out):
    i = pl.program_id(0)
    is_ctx = i < na
    x = jnp.where(is_ctx, xa_ref[...], xb_ref[...])
    a = _adaln(x, g_ref[...], mod_ref[0:1, :], mod_ref[1:2, :])
    y = jnp.dot(a.astype(BF16), w_ref[...], preferred_element_type=F32)
    tm = x.shape[0]
    cos = jnp.where(is_ctx, 1.0, cos_ref[...])
    sin = jnp.where(is_ctx, 0.0, sin_ref[...])
    lane = lax.broadcasted_iota(I32, (tm, LANES), 1)
    first16 = (lane % 32) < 16

    def norm_rope(z, wt):
        ss = jnp.dot((z * z).astype(BF16), bd_ref[...], preferred_element_type=F32)
        zn = z * lax.rsqrt(ss * (1.0 / HEAD_DIM) + NORM_EPS) * wt
        outs = []
        for c in range(2):
            t = zn[:, c * LANES:(c + 1) * LANES]
            partner = jnp.where(first16, pltpu.roll(t, LANES - 16, 1), pltpu.roll(t, 16, 1))
            outs.append(t * cos + partner * sin)
        return jnp.concatenate(outs, axis=1)

    nq = N_HEADS * HEAD_DIM
    nk = N_KV_HEADS * HEAD_DIM
    scale = HEAD_DIM ** -0.5 * LOG2E
    for s in range(nq // 256):
        sl = slice(s * 256, (s + 1) * 256)
        q_out[:, sl] = (norm_rope(y[:, sl], qn_ref[:, sl]) * scale).astype(BF16)
    k = norm_rope(y[:, nq:nq + nk], kn_ref[...])
    v = y[:, nq + nk:nq + 2 * nk]
    for g in range(N_KV_HEADS):
        c = g // 2
        kk = _dup_half(k[:, c * LANES:(c + 1) * LANES], g % 2)
        kt_out[g] = kk.T.astype(BF16)
        vv_out[:, g * LANES:(g + 1) * LANES] = _dup_half(v[:, c * LANES:(c + 1) * LANES], g % 2).astype(BF16)

    @pl.when(is_ctx)
    def _():
        kv_out[:, 0:nk] = k
        kv_out[:, nk:2 * nk] = v


def _rope_tables(t_len):
    pos = jnp.arange(t_len)
    row = (pos // GRID_W).astype(F32)[:, None]
    col = (pos % GRID_W).astype(F32)[:, None]
    half = HEAD_DIM // 2
    inv = ROPE_THETA ** (-jnp.arange(0, half, 2, dtype=F32) / half)[None, :]
    ar, ac = row * inv, col * inv
    cos = jnp.concatenate([jnp.cos(ar), jnp.cos(ar), jnp.cos(ac), jnp.cos(ac)], axis=1)
    sin = jnp.concatenate([-jnp.sin(ar), jnp.sin(ar), -jnp.sin(ac), jnp.sin(ac)], axis=1)
    return jnp.tile(cos, (1, 2)), jnp.tile(sin, (1, 2))


def _qkv(xp, xs, t_len, mod, g, w, qn, kn):
    n_p, n_s = xp.shape[0], xs.shape[0]
    tm = _row_tile(n_p, t_len)
    na, nb = n_p // tm, n_s // tm
    nt_s = t_len // tm
    n = n_p + n_s
    d = D_MODEL
    nq, nk = N_HEADS * HEAD_DIM, N_KV_HEADS * HEAD_DIM
    cos, sin = _rope_tables(t_len)
    eye4 = jnp.kron(jnp.eye(4, dtype=F32), jnp.ones((HEAD_DIM, HEAD_DIM), F32)).astype(BF16)
    mod_idx = lambda i: (jnp.where(i < na, 0, 1 + (i - na) // nt_s), 0, 0)
    pos_idx = lambda i: (jnp.maximum(i - na, 0) % nt_s, 0)
    return pl.pallas_call(
        functools.partial(_qkv_kernel, na),
        grid=(na + nb,),
        in_specs=[
            pl.BlockSpec((tm, d), lambda i: (jnp.minimum(i, na - 1), 0)),
            pl.BlockSpec((tm, d), lambda i: (jnp.maximum(i - na, 0), 0)),
            pl.BlockSpec((None, 8, d), mod_idx),
            _const_spec((1, d)),
            _const_spec((d, nq + 2 * nk)),
            _const_spec((1, nq)),
            _const_spec((1, nk)),
            _const_spec((256, 256)),
            pl.BlockSpec((tm, LANES), pos_idx),
            pl.BlockSpec((tm, LANES), pos_idx),
        ],
        out_specs=[pl.BlockSpec((tm, nq), lambda i: (i, 0)),
                   pl.BlockSpec((N_KV_HEADS, LANES, tm), lambda i: (0, 0, i)),
                   pl.BlockSpec((tm, N_KV_HEADS * LANES), lambda i: (i, 0)),
                   pl.BlockSpec((tm, 2 * nk), lambda i: (jnp.minimum(i, na - 1), 0))],
        out_shape=[jax.ShapeDtypeStruct((n, nq), BF16),
                   jax.ShapeDtypeStruct((N_KV_HEADS, LANES, n), BF16),
                   jax.ShapeDtypeStruct((n, N_KV_HEADS * LANES), BF16),
                   jax.ShapeDtypeStruct((n_p, 2 * nk), F32)],
        compiler_params=_params(("arbitrary",)),
        name="qkv",
    )(xp, xs, mod, g, w, qn, kn, eye4, cos, sin)


def _attn_group(q2a, q2b, kt, vv, bias, n_bias, sinks):
    tq = q2a.shape[0]
    lane = lax.broadcasted_iota(I32, (tq, LANES), 1)
    lo = lane < HEAD_DIM
    zero = jnp.zeros_like(q2a)
    qs = jnp.concatenate([jnp.where(lo, q2a, zero), jnp.where(lo, zero, q2a),
                          jnp.where(lo, q2b, zero), jnp.where(lo, zero, q2b)], axis=0)
    s = jnp.dot(qs, kt, preferred_element_type=F32)
    ps, dens = [], []
    for h in range(4):
        sh = s[h * tq:(h + 1) * tq]
        if bias is not None:
            sh = jnp.concatenate([sh[:, :n_bias] + bias, sh[:, n_bias:]], axis=1)
        m = jnp.maximum(jnp.max(sh, axis=-1, keepdims=True), sinks[h])
        p = jnp.exp2(sh - m)
        dens.append(jnp.sum(p, axis=-1, keepdims=True) + jnp.exp2(sinks[h] - m))
        ps.append(p.astype(BF16))
    o = (jnp.dot(jnp.concatenate(ps, axis=0), vv, preferred_element_type=F32)
         / jnp.concatenate(dens, axis=0))
    oa = jnp.where(lo, o[0:tq], o[tq:2 * tq])
    ob = jnp.where(lo, o[2 * tq:3 * tq], o[3 * tq:4 * tq])
    return oa, ob


def _ctx_attn_kernel(sink_ref, q_ref, kt_ref, vv_ref, x_ref, mod_ref, wo_ref, o_ref, osc):
    for g in range(N_KV_HEADS):
        q2a = q_ref[:, (2 * g) * LANES:(2 * g + 1) * LANES]
        q2b = q_ref[:, (2 * g + 1) * LANES:(2 * g + 2) * LANES]
        sinks = [sink_ref[4 * g + j] * LOG2E for j in range(4)]
        oa, ob = _attn_group(q2a, q2b, kt_ref[g], vv_ref[:, g * LANES:(g + 1) * LANES], None, 0, sinks)
        osc[:, (2 * g) * LANES:(2 * g + 1) * LANES] = oa.astype(BF16)
        osc[:, (2 * g + 1) * LANES:(2 * g + 2) * LANES] = ob.astype(BF16)
    att = jnp.dot(osc[...], wo_ref[...], preferred_element_type=F32)
    o_ref[...] = x_ref[...] + mod_ref[2:3, :] * att


def _ctx_attention(sink, q, kt, vv, xp, seq, mod, wo):
    n_p, d = xp.shape
    nb = n_p // seq
    return pl.pallas_call(
        _ctx_attn_kernel,
        grid=(nb,),
        in_specs=[
            pl.BlockSpec(memory_space=pltpu.SMEM),
            pl.BlockSpec((seq, d), lambda b: (b, 0)),
            pl.BlockSpec((N_KV_HEADS, LANES, seq), lambda b: (0, 0, b)),
            pl.BlockSpec((seq, N_KV_HEADS * LANES), lambda b: (b, 0)),
            pl.BlockSpec((seq, d), lambda b: (b, 0)),
            pl.BlockSpec((None, 8, d), lambda b: (0, 0, 0)),
            _const_spec((d, d)),
        ],
        out_specs=pl.BlockSpec((seq, d), lambda b: (b, 0)),
        out_shape=jax.ShapeDtypeStruct((n_p, d), F32),
        scratch_shapes=[pltpu.VMEM((seq, d), BF16)],
        compiler_params=_params(("arbitrary",)),
        name="ctx_attention",
    )(sink, q, kt, vv, xp, mod, wo)


LAT_QB = 4


def _lat_attn_kernel(t_len, sink_ref, q_ref, ktp_ref, kto_ref, ktn_ref, vvp_ref, vvo_ref, vvn_ref,
                     ckt_ref, cvv_ref, x_ref, mod_ref, wo_ref, o_ref, osc):
    step = pl.program_id(1)
    tq = ATTN_BLOCK
    nloc = 3 * ATTN_BLOCK
    qi = lax.broadcasted_iota(I32, (tq, nloc), 0)
    kj = lax.broadcasted_iota(I32, (tq, nloc), 1)
    in_window = jnp.abs(qi + tq - kj) <= WINDOW
    for u in range(LAT_QB):
        n = step * LAT_QB + u
        kpos = (n - 1) * tq + kj
        bias = jnp.where(in_window & (kpos >= 0) & (kpos < t_len), 0.0, NEG_INF)
        rows = slice(u * tq, (u + 1) * tq)
        for g in range(N_KV_HEADS):
            vs = slice(g * LANES, (g + 1) * LANES)
            kts = [ktp_ref[g]] + [kto_ref[g, :, j * tq:(j + 1) * tq] for j in range(LAT_QB)] + [ktn_ref[g]]
            vvs = [vvp_ref[:, vs]] + [vvo_ref[j * tq:(j + 1) * tq, vs] for j in range(LAT_QB)] + [vvn_ref[:, vs]]
            kt = jnp.concatenate(kts[u:u + 3] + [ckt_ref[g]], axis=1)
            vv = jnp.concatenate(vvs[u:u + 3] + [cvv_ref[:, vs]], axis=0)
            q2a = q_ref[rows, (2 * g) * LANES:(2 * g + 1) * LANES]
            q2b = q_ref[rows, (2 * g + 1) * LANES:(2 * g + 2) * LANES]
            sinks = [sink_ref[4 * g + j] * LOG2E for j in range(4)]
            oa, ob = _attn_group(q2a, q2b, kt, vv, bias, nloc, sinks)
            osc[rows, (2 * g) * LANES:(2 * g + 1) * LANES] = oa.astype(BF16)
            osc[rows, (2 * g + 1) * LANES:(2 * g + 2) * LANES] = ob.astype(BF16)
    att = jnp.dot(osc[...], wo_ref[...], preferred_element_type=F32)
    o_ref[...] = x_ref[...] + mod_ref[2:3, :] * att


def _lat_attention(sink, q, kt, vv, ckt, cvv, xs, n_p, t_len, mod, wo):
    n_s, d = xs.shape
    db = n_s // t_len
    tq = ATTN_BLOCK
    ts = LAT_QB * tq
    nblk = t_len // tq
    nstep = t_len // ts
    assert n_p % ts == 0 and t_len % ts == 0
    off = n_p // tq
    npast = ckt.shape[3]
    g4 = N_KV_HEADS

    def nbr(delta):
        return lambda b, m: off + b * nblk + jnp.clip(m * LAT_QB + delta, 0, nblk - 1)

    own = lambda b, m: n_p // ts + b * nstep + m
    return pl.pallas_call(
        functools.partial(_lat_attn_kernel, t_len),
        grid=(db, nstep),
        in_specs=[
            pl.BlockSpec(memory_space=pltpu.SMEM),
            pl.BlockSpec((ts, d), lambda b, m: (own(b, m), 0)),
            pl.BlockSpec((g4, LANES, tq), lambda b, m: (0, 0, nbr(-1)(b, m))),
            pl.BlockSpec((g4, LANES, ts), lambda b, m: (0, 0, own(b, m))),
            pl.BlockSpec((g4, LANES, tq), lambda b, m: (0, 0, nbr(LAT_QB)(b, m))),
            pl.BlockSpec((tq, g4 * LANES), lambda b, m: (nbr(-1)(b, m), 0)),
            pl.BlockSpec((ts, g4 * LANES), lambda b, m: (own(b, m), 0)),
            pl.BlockSpec((tq, g4 * LANES), lambda b, m: (nbr(LAT_QB)(b, m), 0)),
            pl.BlockSpec((None, g4, LANES, npast), lambda b, m: (b, 0, 0, 0)),
            pl.BlockSpec((None, npast, g4 * LANES), lambda b, m: (b, 0, 0)),
            pl.BlockSpec((ts, d), lambda b, m: (b * nstep + m, 0)),
            pl.BlockSpec((None, 8, d), lambda b, m: (1 + b, 0, 0)),
            _const_spec((d, d)),
        ],
        out_specs=pl.BlockSpec((ts, d), lambda b, m: (b * nstep + m, 0)),
        out_shape=jax.ShapeDtypeStruct((n_s, d), F32),
        scratch_shapes=[pltpu.VMEM((ts, d), BF16)],
        compiler_params=_params(("arbitrary", "arbitrary")),
        name="lat_attention",
    )(sink, q, kt, kt, kt, vv, vv, vv, ckt, cvv, xs, mod, wo)


def _ffn_kernel(na, fc, xa_ref, xb_ref, mod_ref, g_ref, wg_ref, wu_ref, wd_ref, o_ref, acc):
    i = pl.program_id(0)
    x = jnp.where(i < na, xa_ref[...], xb_ref[...])
    a = _adaln(x, g_ref[...], mod_ref[3:4, :], mod_ref[4:5, :]).astype(BF16)
    nf = wg_ref.shape[1] // fc
    for f in range(nf):
        sl = slice(f * fc, (f + 1) * fc)
        hg = jnp.dot(a, wg_ref[:, sl], preferred_element_type=F32)
        hu = jnp.dot(a, wu_ref[:, sl], preferred_element_type=F32)
        act = (_silu(hg) * hu).astype(BF16)
        part = jnp.dot(act, wd_ref[sl, :], preferred_element_type=F32)
        if f == 0:
            acc[...] = part
        else:
            acc[...] += part
    o_ref[...] = x + mod_ref[5:6, :] * acc[...]


def _ffn(hp, hs, t_len, mod, g, wg, wu, wd):
    n_p, n_s = hp.shape[0], hs.shape[0]
    d, dff = wg.shape
    tm = _row_tile(n_p, t_len)
    na, nb = n_p // tm, n_s // tm
    nt_s = t_len // tm
    fc = 256
    assert dff % fc == 0
    mod_idx = lambda i: (jnp.where(i < na, 0, 1 + (i - na) // nt_s), 0, 0)
    return pl.pallas_call(
        functools.partial(_ffn_kernel, na, fc),
        grid=(na + nb,),
        in_specs=[
            pl.BlockSpec((tm, d), lambda i: (jnp.minimum(i, na - 1), 0)),
            pl.BlockSpec((tm, d), lambda i: (jnp.maximum(i - na, 0), 0)),
            pl.BlockSpec((None, 8, d), mod_idx),
            _const_spec((1, d)),
            _const_spec((d, dff)),
            _const_spec((d, dff)),
            _const_spec((dff, d)),
        ],
        out_specs=pl.BlockSpec((tm, d), lambda i: (i, 0)),
        out_shape=jax.ShapeDtypeStruct((n_p + n_s, d), F32),
        scratch_shapes=[pltpu.VMEM((tm, d), F32)],
        compiler_params=_params(("arbitrary",)),
        name="ffn",
    )(hp, hs, mod, g, wg, wu, wd)


def _split2(x):
    hi = x.astype(BF16)
    lo = (x - hi.astype(F32)).astype(BF16)
    return hi, lo


def _gla_in_kernel(nt_s, x_ref, mod_ref, g_ref, w_ref, w1_ref, w2_ref, gb_ref, tri_ref,
                   qk_out, v_out, r_out, b_out):
    x = x_ref[...]
    tm = x.shape[0]
    a = _adaln(x, g_ref[...], mod_ref[0:1, :], mod_ref[1:2, :]).astype(BF16)
    hk = GLA_HEADS * GLA_DK
    hv = GLA_HEADS * GLA_DV
    q = jnp.dot(a, w_ref[:, 0:hk], preferred_element_type=F32)
    qk_out[:, 0:hk] = q * (GLA_DK ** -0.5)
    qk_out[:, hk:2 * hk] = jnp.dot(a, w_ref[:, hk:2 * hk], preferred_element_type=F32)
    v_out[...] = jnp.dot(a, w_ref[:, 2 * hk:2 * hk + hv], preferred_element_type=F32)
    r_out[...] = jnp.dot(a, w_ref[:, 2 * hk + hv:2 * hk + 2 * hv], preferred_element_type=F32).astype(BF16)
    z1 = jnp.dot(a, w1_ref[...], preferred_element_type=F32)
    z = jnp.dot(z1.astype(BF16), w2_ref[...], preferred_element_type=F32) + gb_ref[...]
    gate = (jnp.minimum(z, 0.0) - jnp.log(1.0 + jnp.exp(-jnp.abs(z)))) * (1.0 / GLA_GATE_TAU)
    c = GLA_CHUNK
    tri = tri_ref[...]
    for j in range(tm // c):
        rows = slice(j * c, (j + 1) * c)
        for dr in range(2):
            cols = slice(dr * hk, (dr + 1) * hk)
            hi, lo = _split2(gate[rows, cols])
            b_out[rows, cols] = (jnp.dot(tri[dr], hi, preferred_element_type=F32)
                                 + jnp.dot(tri[dr], lo, preferred_element_type=F32))


def _gla_in(h, n_p, t_len, mod, g, w_in, w1, w2, gate_b):
    n, d = h.shape
    tm = _row_tile(n_p, t_len, cap=1024)
    na = n_p // tm
    nt_s = t_len // tm
    hk, hv = GLA_HEADS * GLA_DK, GLA_HEADS * GLA_DV
    c = GLA_CHUNK
    lower = jnp.tril(jnp.ones((c, c), F32))
    tri = jnp.stack([lower, lower.T]).astype(BF16)
    mod_idx = lambda i: (jnp.where(i < na, 0, 1 + (i - na) // nt_s), 0, 0)
    row = lambda i: (i, 0)
    return pl.pallas_call(
        functools.partial(_gla_in_kernel, nt_s),
        grid=(n // tm,),
        in_specs=[
            pl.BlockSpec((tm, d), row),
            pl.BlockSpec((None, 8, d), mod_idx),
            _const_spec((1, d)),
            _const_spec(w_in.shape),
            _const_spec(w1.shape),
            _const_spec(w2.shape),
            _const_spec((1, 2 * hk)),
            _const_spec((2, c, c)),
        ],
        out_specs=[pl.BlockSpec((tm, 2 * hk), row), pl.BlockSpec((tm, hv), row),
                   pl.BlockSpec((tm, hv), row), pl.BlockSpec((tm, 2 * hk), row)],
        out_shape=[jax.ShapeDtypeStruct((n, 2 * hk), F32), jax.ShapeDtypeStruct((n, hv), F32),
                   jax.ShapeDtypeStruct((n, hv), BF16), jax.ShapeDtypeStruct((n, 2 * hk), F32)],
        compiler_params=_params(("arbitrary",)),
        name="gla_in",
    )(h, mod, g, w_in, w1, w2, gate_b, tri)


def _gla_direction(qk_ref, v_ref, b_ref, st, o_ref, forward, qd_s, vb_s, att_s, upd_s, dec_s):
    c = GLA_CHUNK
    ncb = qk_ref.shape[0] // c
    hk = GLA_HEADS * GLA_DK
    dk, dv = GLA_DK, GLA_DV
    ri = lax.broadcasted_iota(I32, (c, c), 0)
    ci_ = lax.broadcasted_iota(I32, (c, c), 1)
    keep = (ci_ <= ri) if forward else (ci_ >= ri)

    vb_s[...] = v_ref[...].astype(BF16)
    for ch in range(ncb):
        rows = slice(ch * c, (ch + 1) * c)
        b = b_ref[rows, :]
        if forward:
            b_end, b_mid = b[c - 1:c, :], b[c // 2 - 1:c // 2, :]
        else:
            b_end, b_mid = b[0:1, :], b[c // 2:c // 2 + 1, :]
        qa = qk_ref[rows, 0:hk] * jnp.exp(b - b_mid)
        ka = qk_ref[rows, hk:2 * hk] * jnp.exp(b_mid - b)
        qd_s[rows, :] = (qa * jnp.exp(b_mid)).astype(BF16)
        kd = (ka * jnp.exp(b_end - b_mid)).astype(BF16)
        dec_s[ch:ch + 1, :] = jnp.exp(b_end)
        qab, kab = qa.astype(BF16), ka.astype(BF16)
        for h in range(GLA_HEADS):
            ks = slice(h * dk, (h + 1) * dk)
            vs = slice(h * dv, (h + 1) * dv)
            att = lax.dot_general(qab[:, ks], kab[:, ks], (((1,), (1,)), ((), ())),
                                  preferred_element_type=F32)
            att_s[ch, h] = jnp.where(keep, att, 0.0).astype(BF16)
            upd_s[ch, h] = lax.dot_general(vb_s[rows, vs], kd[:, ks], (((0,), (0,)), ((), ())),
                                           preferred_element_type=F32)

    for ch in (range(ncb) if forward else reversed(range(ncb))):
        rows = slice(ch * c, (ch + 1) * c)
        for h in range(GLA_HEADS):
            ks = slice(h * dk, (h + 1) * dk)
            vs = slice(h * dv, (h + 1) * dv)
            s = st[h]
            o = lax.dot_general(qd_s[rows, ks], s.astype(BF16), (((1,), (1,)), ((), ())),
                                preferred_element_type=F32)
            o = o + jnp.dot(att_s[ch, h], vb_s[rows, vs], preferred_element_type=F32)
            o_ref[rows, vs] = o.astype(o_ref.dtype)
            st[h] = s * dec_s[ch:ch + 1, ks] + upd_s[ch, h]


def _gla_scan_kernel(qkf_ref, vf_ref, bf_ref, qkb_ref, vb_ref, bb_ref, sf0_ref, sb0_ref,
                     of_ref, ob_ref, sf_ref, sb_ref, stf, stb, qd_s, vb_s, att_s, upd_s, dec_s):
    i = pl.program_id(1)
    nblk = pl.num_programs(1)

    @pl.when(i == 0)
    def _():
        for h in range(GLA_HEADS):
            stf[h] = sf0_ref[h].T
            stb[h] = sb0_ref[h].T

    _gla_direction(qkf_ref, vf_ref, bf_ref, stf, of_ref, True, qd_s, vb_s, att_s, upd_s, dec_s)
    _gla_direction(qkb_ref, vb_ref, bb_ref, stb, ob_ref, False, qd_s, vb_s, att_s, upd_s, dec_s)

    @pl.when(i == nblk - 1)
    def _():
        for h in range(GLA_HEADS):
            sf_ref[h] = stf[h].T
            sb_ref[h] = stb[h].T


def _gla_scan(qk, v, bc, sf0, sb0, b_off, nb, t_len):
    n = qk.shape[0]
    assert n % t_len == 0
    ns = n // t_len
    tb = _row_tile(t_len)
    nblk = t_len // tb
    qk3 = qk.reshape(ns, t_len, qk.shape[1])
    v3 = v.reshape(ns, t_len, v.shape[1])
    bc3 = bc.reshape(ns, t_len, bc.shape[1])
    h, dk, dv = GLA_HEADS, GLA_DK, GLA_DV
    hk = h * dk
    st_spec = pl.BlockSpec((None, h, dk, dv), lambda b, i: (b, 0, 0, 0))
    fwd = lambda b, i: (b_off + b, i, 0)
    bwd = lambda b, i: (b_off + b, nblk - 1 - i, 0)
    return pl.pallas_call(
        _gla_scan_kernel,
        grid=(nb, nblk),
        in_specs=[
            pl.BlockSpec((None, tb, 2 * hk), fwd),
            pl.BlockSpec((None, tb, h * dv), fwd),
            pl.BlockSpec((None, tb, hk), fwd),
            pl.BlockSpec((None, tb, 2 * hk), bwd),
            pl.BlockSpec((None, tb, h * dv), bwd),
            pl.BlockSpec((None, tb, hk), lambda b, i: (b_off + b, nblk - 1 - i, 1)),
            st_spec, st_spec,
        ],
        out_specs=[pl.BlockSpec((None, tb, h * dv), lambda b, i: (b, i, 0)),
                   pl.BlockSpec((None, tb, h * dv), lambda b, i: (b, nblk - 1 - i, 0)),
                   st_spec, st_spec],
        out_shape=[jax.ShapeDtypeStruct((nb, t_len, h * dv), BF16),
                   jax.ShapeDtypeStruct((nb, t_len, h * dv), BF16),
                   jax.ShapeDtypeStruct((nb, h, dk, dv), F32),
                   jax.ShapeDtypeStruct((nb, h, dk, dv), F32)],
        scratch_shapes=[pltpu.VMEM((h, dv, dk), F32), pltpu.VMEM((h, dv, dk), F32),
                        pltpu.VMEM((tb, hk), BF16), pltpu.VMEM((tb, h * dv), BF16),
                        pltpu.VMEM((tb // GLA_CHUNK, h, GLA_CHUNK, GLA_CHUNK), BF16),
                        pltpu.VMEM((tb // GLA_CHUNK, h, dv, dk), F32),
                        pltpu.VMEM((max(tb // GLA_CHUNK, 8), hk), F32)],
        compiler_params=_params(("arbitrary", "arbitrary")),
        name="gla_scan",
    )(qk3, v3, bc3, qk3, v3, bc3, sf0, sb0)


def _gla_out_kernel(na, ofa_ref, oba_ref, ofb_ref, obb_ref, r_ref, h_ref, mod_ref, on_ref, wo_ref,
                    g2_ref, rw_ref, tri_ref, h_out, f_out, route_out, wt_out, cnt_out, run):
    i = pl.program_id(0)
    is_ctx = i < na
    o = (jnp.where(is_ctx, ofa_ref[...], ofb_ref[...]).astype(F32)
         + jnp.where(is_ctx, oba_ref[...], obb_ref[...]).astype(F32))
    tm = o.shape[0]
    dv = GLA_DV
    parts = []
    for hh in range(GLA_HEADS):
        oh = o[:, hh * dv:(hh + 1) * dv]
        parts.append(_rms(oh) * on_ref[...])
    on = jnp.concatenate(parts, axis=1)
    r = r_ref[...].astype(F32)
    gated = (on * _silu(r)).astype(BF16)
    h = h_ref[...] + mod_ref[2:3, :] * jnp.dot(gated, wo_ref[...], preferred_element_type=F32)
    h_out[...] = h
    f = _adaln(h, g2_ref[...], mod_ref[3:4, :], mod_ref[4:5, :])
    f_out[...] = _pack_halves(f)
    fh = f.astype(BF16)
    fl = (f - fh.astype(F32)).astype(BF16)
    logits = (jnp.dot(fh, rw_ref[0], preferred_element_type=F32)
              + jnp.dot(fl, rw_ref[0], preferred_element_type=F32)
              + jnp.dot(fh, rw_ref[1], preferred_element_type=F32))
    lane = lax.broadcasted_iota(I32, (tm, LANES), 1)
    lane_f = lane.astype(F32)
    logits = jnp.where(lane < N_EXPERTS, logits, -jnp.inf)
    m1 = jnp.max(logits, axis=-1, keepdims=True)
    i1 = jnp.min(jnp.where(logits == m1, lane_f, float(LANES)), axis=-1, keepdims=True)
    rest = jnp.where(lane_f == i1, -jnp.inf, logits)
    m2 = jnp.max(rest, axis=-1, keepdims=True)
    i2 = jnp.min(jnp.where(rest == m2, lane_f, float(LANES)), axis=-1, keepdims=True)
    e2 = jnp.exp(m2 - m1)
    w1 = 1.0 / (1.0 + e2)
    w2 = e2 / (1.0 + e2)
    wt_out[...] = jnp.where(lane == 0, w1, jnp.where(lane == 1, w2, 0.0))
    sel1 = lane_f == i1
    sel2 = lane_f == i2
    picked = jnp.where(sel1, 1.0, jnp.where(sel2, 1.0, 0.0))

    @pl.when(i == 0)
    def _():
        run[...] = jnp.zeros_like(run)

    before = jnp.dot(tri_ref[...], picked.astype(BF16), preferred_element_type=F32) + run[0:1, :]
    rank1 = jnp.sum(jnp.where(sel1, before, 0.0), axis=-1, keepdims=True)
    rank2 = jnp.sum(jnp.where(sel2, before, 0.0), axis=-1, keepdims=True)
    run[...] = run[...] + jnp.sum(picked, axis=0, keepdims=True)
    cnt_out[...] = run[...]
    routing = jnp.where(lane == 0, i1, jnp.where(lane == 1, i2,
                        jnp.where(lane == 2, rank1, jnp.where(lane == 3, rank2, 0.0))))
    route_out[...] = routing.T[0:8, :]


def _gla_out(of_p, ob_p, of_s, ob_s, r, h, n_p, t_len, mod, out_norm, wo, g2, router_w):
    n, d = h.shape
    tm = _row_tile(n_p, t_len)
    na = n_p // tm
    nt_s = t_len // tm
    mod_idx = lambda i: (jnp.where(i < na, 0, 1 + (i - na) // nt_s), 0, 0)
    row = lambda i: (i, 0)
    ctx_row = lambda i: (jnp.minimum(i, na - 1), 0)
    lat_row = lambda i: (jnp.maximum(i - na, 0), 0)
    tri = jnp.tril(jnp.ones((tm, tm), F32), -1).astype(BF16)
    return pl.pallas_call(
        functools.partial(_gla_out_kernel, na),
        grid=(n // tm,),
        in_specs=[
            pl.BlockSpec((tm, d), ctx_row),
            pl.BlockSpec((tm, d), ctx_row),
            pl.BlockSpec((tm, d), lat_row),
            pl.BlockSpec((tm, d), lat_row),
            pl.BlockSpec((tm, d), row),
            pl.BlockSpec((tm, d), row),
            pl.BlockSpec((None, 8, d), mod_idx),
            _const_spec((1, GLA_DV)),
            _const_spec((d, d)),
            _const_spec((1, d)),
            _const_spec((2, d, LANES)),
            _const_spec((tm, tm)),
        ],
        out_specs=[pl.BlockSpec((tm, d), row), pl.BlockSpec((tm, d // 2), row),
                   pl.BlockSpec((8, tm), lambda i: (0, i)), pl.BlockSpec((tm, LANES), row),
                   _const_spec((8, LANES))],
        out_shape=[jax.ShapeDtypeStruct((n, d), F32), jax.ShapeDtypeStruct((n, d // 2), jnp.uint32),
                   jax.ShapeDtypeStruct((8, n), F32), jax.ShapeDtypeStruct((n, LANES), F32),
                   jax.ShapeDtypeStruct((8, LANES), F32)],
        scratch_shapes=[pltpu.VMEM((8, LANES), F32)],
        compiler_params=_params(("arbitrary",)),
        name="gla_out_router",
    )(of_p, ob_p, of_s, ob_s, r, h, mod, out_norm, wo, g2, router_w, tri)


SC_CORES = 2
SC_SUBCORES = 16
SC_CHUNK = 64


def _sc_gather_rows(table, idx):
    nw = SC_CORES * SC_SUBCORES
    b = idx.shape[0]
    d = table.shape[1]
    assert b % (nw * SC_CHUNK) == 0
    per_w = b // nw
    n_chunks = per_w // SC_CHUNK
    mesh = plsc.VectorSubcoreMesh(core_axis_name="c", subcore_axis_name="s",
                                  num_cores=SC_CORES, num_subcores=SC_SUBCORES)

    assert n_chunks % 2 == 0

    def body(table_hbm, idx_hbm, out_hbm, idx_v, rows_v, gsem, wsem):
        wid = lax.axis_index("s") * SC_CORES + lax.axis_index("c")
        base = wid * per_w
        pltpu.sync_copy(idx_hbm.at[wid], idx_v)

        def gather(j, slot):
            return pltpu.make_async_copy(table_hbm.at[idx_v.at[j]], rows_v.at[slot], gsem.at[slot])

        def write(j, slot):
            return pltpu.make_async_copy(rows_v.at[slot], out_hbm.at[pl.ds(base + j * SC_CHUNK, SC_CHUNK)],
                                         wsem.at[slot])

        gather(0, 0).start()

        @pl.loop(0, n_chunks, step=2)
        def _(j):
            for slot in range(2):
                jj = j + slot
                gather(jj, slot).wait()

                @pl.when(jj + 1 < n_chunks)
                def _():
                    @pl.when(jj >= 1)
                    def _():
                        write(jj - 1, 1 - slot).wait()

                    gather(jj + 1, 1 - slot).start()

                write(jj, slot).start()

        write(n_chunks - 2, 0).wait()
        write(n_chunks - 1, 1).wait()

    return pl.kernel(
        body,
        out_type=jax.ShapeDtypeStruct((b, d), table.dtype),
        mesh=mesh,
        scratch_types=[pltpu.VMEM((n_chunks, SC_CHUNK), I32),
                       pltpu.VMEM((2, SC_CHUNK, d), table.dtype),
                       pltpu.SemaphoreType.DMA((2,)),
                       pltpu.SemaphoreType.DMA((2,))],
        name="sc_gather_rows",
    )(table, idx.reshape(nw, n_chunks, SC_CHUNK))


def _sc_scatter_rows(rows, idx2, p):
    nw = SC_CORES * SC_SUBCORES
    n, d = rows.shape
    assert n % (nw * SC_CHUNK) == 0
    per_w = n // nw
    n_chunks = per_w // SC_CHUNK
    mesh = plsc.VectorSubcoreMesh(core_axis_name="c", subcore_axis_name="s",
                                  num_cores=SC_CORES, num_subcores=SC_SUBCORES)

    assert n_chunks % 2 == 0

    def body(rows_hbm, idx0_hbm, idx1_hbm, out_hbm, idx0_v, idx1_v, rows_v, rsem, ssem):
        wid = lax.axis_index("s") * SC_CORES + lax.axis_index("c")
        base = wid * per_w
        pltpu.sync_copy(idx0_hbm.at[wid], idx0_v)
        pltpu.sync_copy(idx1_hbm.at[wid], idx1_v)

        def read(j, slot):
            return pltpu.make_async_copy(rows_hbm.at[pl.ds(base + j * SC_CHUNK, SC_CHUNK)], rows_v.at[slot],
                                         rsem.at[slot])

        read(0, 0).start()

        @pl.loop(0, n_chunks, step=2)
        def _(j):
            for slot in range(2):
                jj = j + slot
                read(jj, slot).wait()

                @pl.when(jj + 1 < n_chunks)
                def _():
                    read(jj + 1, 1 - slot).start()

                s0 = pltpu.make_async_copy(rows_v.at[slot], out_hbm.at[idx0_v.at[jj]], ssem.at[0])
                s1 = pltpu.make_async_copy(rows_v.at[slot], out_hbm.at[idx1_v.at[jj]], ssem.at[1])
                s0.start()
                s1.start()
                s0.wait()
                s1.wait()

    return pl.kernel(
        body,
        out_type=jax.ShapeDtypeStruct((p, d), rows.dtype),
        mesh=mesh,
        scratch_types=[pltpu.VMEM((n_chunks, SC_CHUNK), I32),
                       pltpu.VMEM((n_chunks, SC_CHUNK), I32),
                       pltpu.VMEM((2, SC_CHUNK, d), rows.dtype),
                       pltpu.SemaphoreType.DMA((2,)),
                       pltpu.SemaphoreType.DMA((2,))],
        name="sc_scatter_rows",
    )(rows, idx2[0].reshape(nw, n_chunks, SC_CHUNK), idx2[1].reshape(nw, n_chunks, SC_CHUNK))


EXPERT_SUB = 512


def _pack_halves(x):
    k = x.shape[1] // 2
    lo = lax.bitcast_convert_type(x[:, :k].astype(BF16).astype(F32), jnp.uint32)
    hi = lax.bitcast_convert_type(x[:, k:].astype(BF16).astype(F32), jnp.uint32)
    return (lo >> 16) | (hi & jnp.uint32(0xFFFF0000))


def _unpack_halves(w):
    lo = lax.bitcast_convert_type(w << 16, F32).astype(BF16)
    hi = lax.bitcast_convert_type(w & jnp.uint32(0xFFFF0000), F32).astype(BF16)
    return lo, hi


def _expert_kernel(te_ref, ts_ref, nu_ref, nv_ref, x_ref, wg_ref, wu_ref, wd_ref, y_ref, acc, xb):
    i = pl.program_id(0)
    f = pl.program_id(1)
    nf = pl.num_programs(1)
    tm = x_ref.shape[0]
    half = x_ref.shape[1]
    sub = EXPERT_SUB
    nvalid = nv_ref[i]

    @pl.when(f == 0)
    def _():
        acc[...] = jnp.zeros_like(acc)

    @pl.when((f == 0) & (nvalid > 0))
    def _():
        rid = lax.broadcasted_iota(I32, (tm, half), 0)
        lo, hi = _unpack_halves(x_ref[...])
        zero = jnp.zeros_like(lo)
        xb[:, 0:half] = jnp.where(rid < nvalid, lo, zero)
        xb[:, half:2 * half] = jnp.where(rid < nvalid, hi, zero)

    def compute(rows):
        x = xb[rows, :]
        hg = jnp.dot(x, wg_ref[...].astype(BF16), preferred_element_type=F32)
        hu = jnp.dot(x, wu_ref[...].astype(BF16), preferred_element_type=F32)
        act = (_silu(hg) * hu).astype(BF16)
        acc[rows, :] += jnp.dot(act, wd_ref[...].astype(BF16), preferred_element_type=F32)

    for k in range(1, tm // sub + 1):
        @pl.when((nvalid > (k - 1) * sub) & (nvalid <= k * sub))
        def _():
            compute(slice(0, k * sub))

    @pl.when(f == nf - 1)
    def _():
        y_ref[...] = _pack_halves(acc[...])


def _experts(x, tile_expert, tile_src, n_used, n_valid, wg, wu, wd, tm, tf):
    p, half = x.shape
    d = 2 * half
    ne, _, dff = wg.shape
    nf = dff // tf
    assert dff % tf == 0 and tm % EXPERT_SUB == 0

    def fidx(i, f, nu):
        return jnp.where(i < nu[0], f, nf - 1)

    grid_spec = pltpu.PrefetchScalarGridSpec(
        num_scalar_prefetch=4,
        grid=(p // tm, nf),
        in_specs=[
            pl.BlockSpec((tm, half), lambda i, f, te, ts, nu, nv: (ts[i], 0)),
            pl.BlockSpec((None, d, tf), lambda i, f, te, ts, nu, nv: (te[i], 0, fidx(i, f, nu))),
            pl.BlockSpec((None, d, tf), lambda i, f, te, ts, nu, nv: (te[i], 0, fidx(i, f, nu))),
            pl.BlockSpec((None, tf, d), lambda i, f, te, ts, nu, nv: (te[i], fidx(i, f, nu), 0)),
        ],
        out_specs=pl.BlockSpec((tm, half), lambda i, f, te, ts, nu, nv: (i, 0)),
        scratch_shapes=[pltpu.VMEM((tm, d), F32), pltpu.VMEM((tm, d), BF16)],
    )
    return pl.pallas_call(
        _expert_kernel,
        grid_spec=grid_spec,
        out_shape=jax.ShapeDtypeStruct((p, half), jnp.uint32),
        compiler_params=_params(("arbitrary", "arbitrary"), 60 * 1024 * 1024),
        name="moe_experts",
    )(tile_expert, tile_src, n_used, n_valid, x, wg, wu, wd)


def _combine_kernel(wt_ref, h_ref, mod_ref, y0_ref, y1_ref, o_ref):
    wt = wt_ref[...]
    half = y0_ref.shape[1]
    lo0, hi0 = _unpack_halves(y0_ref[...])
    lo1, hi1 = _unpack_halves(y1_ref[...])
    w0, w1 = wt[:, 0:1], wt[:, 1:2]
    h = h_ref[...]
    gate = mod_ref[5:6, :]
    o_ref[:, 0:half] = h[:, 0:half] + gate[:, 0:half] * (w0 * lo0.astype(F32) + w1 * lo1.astype(F32))
    o_ref[:, half:] = h[:, half:] + gate[:, half:] * (w0 * hi0.astype(F32) + w1 * hi1.astype(F32))


def _combine(yg, wt, h, mod, row_off, n_rows, n_mod, mod_off, tm):
    d = h.shape[1]
    nt = n_rows // tm
    toff = row_off // tm
    per_mod = n_rows // n_mod // tm
    return pl.pallas_call(
        _combine_kernel,
        grid=(nt,),
        in_specs=[
            pl.BlockSpec((tm, LANES), lambda i: (toff + i, 0)),
            pl.BlockSpec((tm, d), lambda i: (toff + i, 0)),
            pl.BlockSpec((None, 8, d), lambda i: (mod_off + i // per_mod, 0, 0)),
            pl.BlockSpec((None, tm, d // 2), lambda i: (0, i, 0)),
            pl.BlockSpec((None, tm, d // 2), lambda i: (1, i, 0)),
        ],
        out_specs=pl.BlockSpec((tm, d), lambda i: (i, 0)),
        out_shape=jax.ShapeDtypeStruct((n_rows, d), F32),
        compiler_params=_params(("arbitrary",)),
        name="moe_combine",
    )(wt, h, mod, yg, yg)


def _route(route, counts, n, tm):
    cnt = counts[0, :N_EXPERTS].astype(I32)
    padded = ((cnt + tm - 1) // tm) * tm
    gend = jnp.cumsum(padded)
    goff = gend - padded
    e = route[0:2].astype(I32)
    rank = route[2:4].astype(I32)
    onehot = e[:, :, None] == jnp.arange(N_EXPERTS, dtype=I32)[None, None, :]
    dest = jnp.sum(jnp.where(onehot, goff[None, None, :], 0), axis=-1) + rank
    p = 2 * n + N_EXPERTS * tm
    n_used = gend[-1] // tm
    tiles = jnp.arange(p // tm, dtype=I32)
    tile_src = jnp.minimum(tiles, n_used - 1)
    tile_expert = jnp.minimum(jnp.sum((gend[None, :] <= (tile_src * tm)[:, None]).astype(I32), axis=1),
                              N_EXPERTS - 1)
    used = goff[tile_expert] + cnt[tile_expert]
    n_valid = jnp.where(tiles < n_used, jnp.clip(used - tiles * tm, 0, tm), 0).astype(I32)
    return dest, p, tile_expert, tile_src, n_used.reshape(1).astype(I32), n_valid


def kernel(x_prompt, x_sample, cache_k, cache_v, state_fwd, state_bwd, c, c_ctx, ada_w, ada_b, norm1_g, norm2_g, attn_w_qkv, attn_q_norm, attn_k_norm, attn_sink, attn_w_o, gla_w_in, gla_gate_w1, gla_gate_w2, gla_gate_b, gla_out_norm, gla_w_o, ffn_w_gate, ffn_w_up, ffn_w_down, moe_router, moe_w_gate, moe_w_up, moe_w_down):
    bp, seq, d = x_prompt.shape
    db, t_len, _ = x_sample.shape
    n_p, n_s = bp * seq, db * t_len
    n = n_p + n_s
    xp = x_prompt.reshape(n_p, d)
    xs = x_sample.reshape(n_s, d)

    cond = jnp.concatenate([c_ctx[None, :], c], axis=0)
    assert cond.shape[0] <= 8
    cond_t = jnp.pad(cond, ((0, 8 - cond.shape[0]), (0, 0))).T
    mods = _modulation(cond_t, cond.shape[0], ada_w, ada_b)

    nk = N_KV_HEADS * HEAD_DIM
    qn = jnp.tile(attn_q_norm[0], N_HEADS)[None, :]
    kn = jnp.tile(attn_k_norm[0], N_KV_HEADS)[None, :]
    q, kt, vv, kv = _qkv(xp, xs, t_len, mods[0], norm1_g[0][None, :], attn_w_qkv[0].astype(BF16), qn, kn)
    wo0 = attn_w_o[0].astype(BF16)
    sink = attn_sink[0]
    hp = _ctx_attention(sink, q, kt, vv, xp, seq, mods[0], wo0)
    ck = cache_k[:, 0].astype(BF16)
    cv = cache_v[:, 0].astype(BF16)
    ckt = jnp.concatenate([ck, ck], axis=-1).transpose(0, 2, 3, 1)
    cvv = jnp.concatenate([cv, cv], axis=-1).reshape(db, cv.shape[1], N_KV_HEADS * LANES)
    hs = _lat_attention(sink, q, kt, vv, ckt, cvv, xs, n_p, t_len, mods[0], wo0)
    new_k = kv[:n_p, :nk].reshape(bp, 1, seq, N_KV_HEADS, HEAD_DIM)
    new_v = kv[:n_p, nk:].reshape(bp, 1, seq, N_KV_HEADS, HEAD_DIM)
    h = _ffn(hp, hs, t_len, mods[0], norm2_g[0][None, :], ffn_w_gate[0].astype(BF16),
             ffn_w_up[0].astype(BF16), ffn_w_down[0].astype(BF16))

    hk = GLA_HEADS * GLA_DK
    rank = GLA_GATE_RANK
    w1 = jnp.concatenate([gla_gate_w1[0, 0], gla_gate_w1[0, 1]], axis=1)
    w1 = jnp.pad(w1, ((0, 0), (0, LANES - 2 * rank))).astype(BF16)
    w2 = jnp.zeros((LANES, 2 * hk), F32)
    w2 = w2.at[0:rank, 0:hk].set(gla_gate_w2[0, 0]).at[rank:2 * rank, hk:].set(gla_gate_w2[0, 1]).astype(BF16)
    gate_b = gla_gate_b[0].reshape(1, 2 * hk)
    qk, v, r, bc = _gla_in(h, n_p, t_len, mods[1], norm1_g[1][None, :], gla_w_in[0].astype(BF16),
                           w1, w2, gate_b)
    zeros = jnp.zeros((bp, GLA_HEADS, GLA_DK, GLA_DV), F32)
    of_p, ob_p, new_sf, new_sb = _gla_scan(qk, v, bc, zeros, zeros, 0, bp, seq)
    of_s, ob_s, _, _ = _gla_scan(qk, v, bc, state_fwd[:, 0], state_bwd[:, 0], n_p // t_len, db, t_len)
    rw = jnp.pad(moe_router[0], ((0, 0), (0, LANES - N_EXPERTS)))
    rw_hi = rw.astype(BF16)
    rw_lo = (rw - rw_hi.astype(F32)).astype(BF16)
    h, f, route, wt, counts = _gla_out(
        of_p.reshape(n_p, d), ob_p.reshape(n_p, d), of_s.reshape(n_s, d), ob_s.reshape(n_s, d), r, h,
        n_p, t_len, mods[1], gla_out_norm[0][None, :], gla_w_o[0].astype(BF16), norm2_g[1][None, :],
        jnp.stack([rw_hi, rw_lo]))

    tm_e = 2048
    dest_k, p_rows, tile_expert, tile_src, n_used, n_valid = _route(route, counts, n, tm_e)
    xg = _sc_scatter_rows(f, dest_k, p_rows)
    y = _experts(xg, tile_expert, tile_src, n_used, n_valid, moe_w_gate[0], moe_w_up[0],
                 moe_w_down[0], tm_e, 512)
    yg_p = _sc_gather_rows(y, dest_k[:, :n_p].reshape(2 * n_p)).reshape(2, n_p, d // 2)
    yg_s = _sc_gather_rows(y, dest_k[:, n_p:].reshape(2 * n_s)).reshape(2, n_s, d // 2)
    tm_c = _row_tile(n_p, t_len)
    y_p = _combine(yg_p, wt, h, mods[1], 0, n_p, 1, 0, tm_c)
    y_s = _combine(yg_s, wt, h, mods[1], n_p, n_s, db, 1, tm_c)

    return (y_p.reshape(bp, seq, d), y_s.reshape(db, t_len, d), new_k, new_v,
            new_sf[:, None], new_sb[:, None])
```

```python
import functools
import math

import jax
import jax.numpy as jnp
from jax import lax
from jax.experimental import pallas as pl
from jax.experimental.pallas import tpu as pltpu
from jax.experimental.pallas import tpu_sc as plsc

F32 = jnp.float32
BF16 = jnp.bfloat16
I32 = jnp.int32

D_MODEL = 1024
N_HEADS = 16
N_KV_HEADS = 4
HEAD_DIM = 64
GRID_W = 64
WINDOW = 128
ATTN_BLOCK = 128
ROPE_THETA = 10000.0
GLA_HEADS = 4
GLA_DK = 128
GLA_DV = 256
GLA_GATE_RANK = 16
GLA_GATE_TAU = 16.0
GLA_CHUNK = 64
N_EXPERTS = 8
NORM_EPS = 1e-6
NEG_INF = -1e30

LANES = 128
VMEM_LIMIT = 56 * 1024 * 1024


def _params(sem, vmem=VMEM_LIMIT):
    return pltpu.CompilerParams(dimension_semantics=sem, vmem_limit_bytes=vmem)


def _row_tile(*counts, cap=512):
    t = cap
    while any(c % t for c in counts):
        t //= 2
    assert t >= 8
    return t


def _rms(x):
    return x * lax.rsqrt(jnp.mean(x * x, axis=-1, keepdims=True) + NORM_EPS)


def _adaln(x, g, shift, scale):
    return _rms(x) * (g * (1.0 + scale)) + shift


def _silu(x):
    return x * jax.nn.sigmoid(x)


def _const_spec(shape):
    nd = len(shape)
    return pl.BlockSpec(shape, lambda *_: (0,) * nd)


def _mod_kernel(n_cond, ct_ref, w_ref, b_ref, o_ref):
    ct = ct_ref[...]
    s = _silu(ct)
    w = w_ref[...]
    rows = [jnp.sum(w * s[:, r:r + 1], axis=0, keepdims=True) for r in range(n_cond)]
    rows += [jnp.zeros_like(rows[0])] * (8 - n_cond)
    o_ref[...] = jnp.concatenate(rows, axis=0) + b_ref[...]


def _modulation(cond_t, n_cond, ada_w, ada_b):
    depth, d, n6 = ada_w.shape
    tn = 1024
    out = pl.pallas_call(
        functools.partial(_mod_kernel, n_cond),
        grid=(depth, n6 // tn),
        in_specs=[
            pl.BlockSpec((d, 8), lambda l, j: (0, 0)),
            pl.BlockSpec((None, d, tn), lambda l, j: (l, 0, j)),
            pl.BlockSpec((None, 1, tn), lambda l, j: (l, 0, j)),
        ],
        out_specs=pl.BlockSpec((None, 8, tn), lambda l, j: (l, 0, j)),
        out_shape=jax.ShapeDtypeStruct((depth, 8, n6), F32),
        compiler_params=_params(("arbitrary", "arbitrary")),
        name="modulation",
    )(cond_t, ada_w, ada_b.reshape(depth, 1, n6))
    m = out[:, :n_cond].reshape(depth, n_cond, 6, d)
    return jnp.pad(m, ((0, 0), (0, 0), (0, 2), (0, 0)))


LOG2E = math.log2(math.e)


def _dup_half(k2, half):
    lane = lax.broadcasted_iota(I32, k2.shape, 1)
    lo = lane < HEAD_DIM
    r = pltpu.roll(k2, HEAD_DIM, 1)
    return jnp.where(lo, k2, r) if half == 0 else jnp.where(lo, r, k2)


def _qkv_kernel(na, xa_ref, xb_ref, mod_ref, g_ref, w_ref, qn_ref, kn_ref, bd_ref,
                cos_ref, sin_ref, q_out, kt_out, vv_out, kv_out):
    i = pl.program_id(0)
    is_ctx = i < na
    x = jnp.where(is_ctx, xa_ref[...], xb_ref[...])
    a = _adaln(x, g_ref[...], mod_ref[0:1, :], mod_ref[1:2, :])
    y = jnp.dot(a.astype(BF16), w_ref[...], preferred_element_type=F32)
    tm = x.shape[0]
    cos = jnp.where(is_ctx, 1.0, cos_ref[...])
    sin = jnp.where(is_ctx, 0.0, sin_ref[...])
    lane = lax.broadcasted_iota(I32, (tm, LANES), 1)
    first16 = (lane % 32) < 16

    def norm_rope(z, wt):
        ss = jnp.dot((z * z).astype(BF16), bd_ref[...], preferred_element_type=F32)
        zn = z * lax.rsqrt(ss * (1.0 / HEAD_DIM) + NORM_EPS) * wt
        outs = []
        for c in range(2):
            t = zn[:, c * LANES:(c + 1) * LANES]
            partner = jnp.where(first16, pltpu.roll(t, LANES - 16, 1), pltpu.roll(t, 16, 1))
            outs.append(t * cos + partner * sin)
        return jnp.concatenate(outs, axis=1)

    nq = N_HEADS * HEAD_DIM
    nk = N_KV_HEADS * HEAD_DIM
    scale = HEAD_DIM ** -0.5 * LOG2E
    for s in range(nq // 256):
        sl = slice(s * 256, (s + 1) * 256)
        q_out[:, sl] = (norm_rope(y[:, sl], qn_ref[:, sl]) * scale).astype(BF16)
    k = norm_rope(y[:, nq:nq + nk], kn_ref[...])
    v = y[:, nq + nk:nq + 2 * nk]
    for g in range(N_KV_HEADS):
        c = g // 2
        kk = _dup_half(k[:, c * LANES:(c + 1) * LANES], g % 2)
        kt_out[g] = kk.T.astype(BF16)
        vv_out[:, g * LANES:(g + 1) * LANES] = _dup_half(v[:, c * LANES:(c + 1) * LANES], g % 2).astype(BF16)

    @pl.when(is_ctx)
    def _():
        kv_out[:, 0:nk] = k
        kv_out[:, nk:2 * nk] = v


def _rope_tables(t_len):
    pos = jnp.arange(t_len)
    row = (pos // GRID_W).astype(F32)[:, None]
    col = (pos % GRID_W).astype(F32)[:, None]
    half = HEAD_DIM // 2
    inv = ROPE_THETA ** (-jnp.arange(0, half, 2, dtype=F32) / half)[None, :]
    ar, ac = row * inv, col * inv
    cos = jnp.concatenate([jnp.cos(ar), jnp.cos(ar), jnp.cos(ac), jnp.cos(ac)], axis=1)
    sin = jnp.concatenate([-jnp.sin(ar), jnp.sin(ar), -jnp.sin(ac), jnp.sin(ac)], axis=1)
    return jnp.tile(cos, (1, 2)), jnp.tile(sin, (1, 2))


def _qkv(xp, xs, t_len, mod, g, w, qn, kn):
    n_p, n_s = xp.shape[0], xs.shape[0]
    tm = _row_tile(n_p, t_len)
    na, nb = n_p // tm, n_s // tm
    nt_s = t_len // tm
    n = n_p + n_s
    d = D_MODEL
    nq, nk = N_HEADS * HEAD_DIM, N_KV_HEADS * HEAD_DIM
    cos, sin = _rope_tables(t_len)
    eye4 = jnp.kron(jnp.eye(4, dtype=F32), jnp.ones((HEAD_DIM, HEAD_DIM), F32)).astype(BF16)
    mod_idx = lambda i: (jnp.where(i < na, 0, 1 + (i - na) // nt_s), 0, 0)
    pos_idx = lambda i: (jnp.maximum(i - na, 0) % nt_s, 0)
    return pl.pallas_call(
        functools.partial(_qkv_kernel, na),
        grid=(na + nb,),
        in_specs=[
            pl.BlockSpec((tm, d), lambda i: (jnp.minimum(i, na - 1), 0)),
            pl.BlockSpec((tm, d), lambda i: (jnp.maximum(i - na, 0), 0)),
            pl.BlockSpec((None, 8, d), mod_idx),
            _const_spec((1, d)),
            _const_spec((d, nq + 2 * nk)),
            _const_spec((1, nq)),
            _const_spec((1, nk)),
            _const_spec((256, 256)),
            pl.BlockSpec((tm, LANES), pos_idx),
            pl.BlockSpec((tm, LANES), pos_idx),
        ],
        out_specs=[pl.BlockSpec((tm, nq), lambda i: (i, 0)),
                   pl.BlockSpec((N_KV_HEADS, LANES, tm), lambda i: (0, 0, i)),
                   pl.BlockSpec((tm, N_KV_HEADS * LANES), lambda i: (i, 0)),
                   pl.BlockSpec((tm, 2 * nk), lambda i: (jnp.minimum(i, na - 1), 0))],
        out_shape=[jax.ShapeDtypeStruct((n, nq), BF16),
                   jax.ShapeDtypeStruct((N_KV_HEADS, LANES, n), BF16),
                   jax.ShapeDtypeStruct((n, N_KV_HEADS * LANES), BF16),
                   jax.ShapeDtypeStruct((n_p, 2 * nk), F32)],
        compiler_params=_params(("arbitrary",)),
        name="qkv",
    )(xp, xs, mod, g, w, qn, kn, eye4, cos, sin)


def _attn_group(q2a, q2b, kt, vv, bias, n_bias, sinks):
    tq = q2a.shape[0]
    lane = lax.broadcasted_iota(I32, (tq, LANES), 1)
    lo = lane < HEAD_DIM
    zero = jnp.zeros_like(q2a)
    qs = jnp.concatenate([jnp.where(lo, q2a, zero), jnp.where(lo, zero, q2a),
                          jnp.where(lo, q2b, zero), jnp.where(lo, zero, q2b)], axis=0)
    s = jnp.dot(qs, kt, preferred_element_type=F32)
    ps, dens = [], []
    for h in range(4):
        sh = s[h * tq:(h + 1) * tq]
        if bias is not None:
            sh = jnp.concatenate([sh[:, :n_bias] + bias, sh[:, n_bias:]], axis=1)
        m = jnp.maximum(jnp.max(sh, axis=-1, keepdims=True), sinks[h])
        p = jnp.exp2(sh - m)
        dens.append(jnp.sum(p, axis=-1, keepdims=True) + jnp.exp2(sinks[h] - m))
        ps.append(p.astype(BF16))
    o = (jnp.dot(jnp.concatenate(ps, axis=0), vv, preferred_element_type=F32)
         / jnp.concatenate(dens, axis=0))
    oa = jnp.where(lo, o[0:tq], o[tq:2 * tq])
    ob = jnp.where(lo, o[2 * tq:3 * tq], o[3 * tq:4 * tq])
    return oa, ob


def _ctx_attn_kernel(sink_ref, q_ref, kt_ref, vv_ref, x_ref, mod_ref, wo_ref, o_ref, osc):
    for g in range(N_KV_HEADS):
        q2a = q_ref[:, (2 * g) * LANES:(2 * g + 1) * LANES]
        q2b = q_ref[:, (2 * g + 1) * LANES:(2 * g + 2) * LANES]
        sinks = [sink_ref[4 * g + j] * LOG2E for j in range(4)]
        oa, ob = _attn_group(q2a, q2b, kt_ref[g], vv_ref[:, g * LANES:(g + 1) * LANES], None, 0, sinks)
        osc[:, (2 * g) * LANES:(2 * g + 1) * LANES] = oa.astype(BF16)
        osc[:, (2 * g + 1) * LANES:(2 * g + 2) * LANES] = ob.astype(BF16)
    att = jnp.dot(osc[...], wo_ref[...], preferred_element_type=F32)
    o_ref[...] = x_ref[...] + mod_ref[2:3, :] * att


def _ctx_attention(sink, q, kt, vv, xp, seq, mod, wo):
    n_p, d = xp.shape
    nb = n_p // seq
    return pl.pallas_call(
        _ctx_attn_kernel,
        grid=(nb,),
        in_specs=[
            pl.BlockSpec(memory_space=pltpu.SMEM),
            pl.BlockSpec((seq, d), lambda b: (b, 0)),
            pl.BlockSpec((N_KV_HEADS, LANES, seq), lambda b: (0, 0, b)),
            pl.BlockSpec((seq, N_KV_HEADS * LANES), lambda b: (b, 0)),
            pl.BlockSpec((seq, d), lambda b: (b, 0)),
            pl.BlockSpec((None, 8, d), lambda b: (0, 0, 0)),
            _const_spec((d, d)),
        ],
        out_specs=pl.BlockSpec((seq, d), lambda b: (b, 0)),
        out_shape=jax.ShapeDtypeStruct((n_p, d), F32),
        scratch_shapes=[pltpu.VMEM((seq, d), BF16)],
        compiler_params=_params(("arbitrary",)),
        name="ctx_attention",
    )(sink, q, kt, vv, xp, mod, wo)


LAT_QB = 4


def _lat_attn_kernel(t_len, sink_ref, q_ref, ktp_ref, kto_ref, ktn_ref, vvp_ref, vvo_ref, vvn_ref,
                     ckt_ref, cvv_ref, x_ref, mod_ref, wo_ref, o_ref, osc):
    step = pl.program_id(1)
    tq = ATTN_BLOCK
    nloc = 3 * ATTN_BLOCK
    qi = lax.broadcasted_iota(I32, (tq, nloc), 0)
    kj = lax.broadcasted_iota(I32, (tq, nloc), 1)
    in_window = jnp.abs(qi + tq - kj) <= WINDOW
    for u in range(LAT_QB):
        n = step * LAT_QB + u
        kpos = (n - 1) * tq + kj
        bias = jnp.where(in_window & (kpos >= 0) & (kpos < t_len), 0.0, NEG_INF)
        rows = slice(u * tq, (u + 1) * tq)
        for g in range(N_KV_HEADS):
            vs = slice(g * LANES, (g + 1) * LANES)
            kts = [ktp_ref[g]] + [kto_ref[g, :, j * tq:(j + 1) * tq] for j in range(LAT_QB)] + [ktn_ref[g]]
            vvs = [vvp_ref[:, vs]] + [vvo_ref[j * tq:(j + 1) * tq, vs] for j in range(LAT_QB)] + [vvn_ref[:, vs]]
            kt = jnp.concatenate(kts[u:u + 3] + [ckt_ref[g]], axis=1)
            vv = jnp.concatenate(vvs[u:u + 3] + [cvv_ref[:, vs]], axis=0)
            q2a = q_ref[rows, (2 * g) * LANES:(2 * g + 1) * LANES]
            q2b = q_ref[rows, (2 * g + 1) * LANES:(2 * g + 2) * LANES]
            sinks = [sink_ref[4 * g + j] * LOG2E for j in range(4)]
            oa, ob = _attn_group(q2a, q2b, kt, vv, bias, nloc, sinks)
            osc[rows, (2 * g) * LANES:(2 * g + 1) * LANES] = oa.astype(BF16)
            osc[rows, (2 * g + 1) * LANES:(2 * g + 2) * LANES] = ob.astype(BF16)
    att = jnp.dot(osc[...], wo_ref[...], preferred_element_type=F32)
    o_ref[...] = x_ref[...] + mod_ref[2:3, :] * att


def _lat_attention(sink, q, kt, vv, ckt, cvv, xs, n_p, t_len, mod, wo):
    n_s, d = xs.shape
    db = n_s // t_len
    tq = ATTN_BLOCK
    ts = LAT_QB * tq
    nblk = t_len // tq
    nstep = t_len // ts
    assert n_p % ts == 0 and t_len % ts == 0
    off = n_p // tq
    npast = ckt.shape[3]
    g4 = N_KV_HEADS

    def nbr(delta):
        return lambda b, m: off + b * nblk + jnp.clip(m * LAT_QB + delta, 0, nblk - 1)

    own = lambda b, m: n_p // ts + b * nstep + m
    return pl.pallas_call(
        functools.partial(_lat_attn_kernel, t_len),
        grid=(db, nstep),
        in_specs=[
            pl.BlockSpec(memory_space=pltpu.SMEM),
            pl.BlockSpec((ts, d), lambda b, m: (own(b, m), 0)),
            pl.BlockSpec((g4, LANES, tq), lambda b, m: (0, 0, nbr(-1)(b, m))),
            pl.BlockSpec((g4, LANES, ts), lambda b, m: (0, 0, own(b, m))),
            pl.BlockSpec((g4, LANES, tq), lambda b, m: (0, 0, nbr(LAT_QB)(b, m))),
            pl.BlockSpec((tq, g4 * LANES), lambda b, m: (nbr(-1)(b, m), 0)),
            pl.BlockSpec((ts, g4 * LANES), lambda b, m: (own(b, m), 0)),
            pl.BlockSpec((tq, g4 * LANES), lambda b, m: (nbr(LAT_QB)(b, m), 0)),
            pl.BlockSpec((None, g4, LANES, npast), lambda b, m: (b, 0, 0, 0)),
            pl.BlockSpec((None, npast, g4 * LANES), lambda b, m: (b, 0, 0)),
            pl.BlockSpec((ts, d), lambda b, m: (b * nstep + m, 0)),
            pl.BlockSpec((None, 8, d), lambda b, m: (1 + b, 0, 0)),
            _const_spec((d, d)),
        ],
        out_specs=pl.BlockSpec((ts, d), lambda b, m: (b * nstep + m, 0)),
        out_shape=jax.ShapeDtypeStruct((n_s, d), F32),
        scratch_shapes=[pltpu.VMEM((ts, d), BF16)],
        compiler_params=_params(("arbitrary", "arbitrary")),
        name="lat_attention",
    )(sink, q, kt, kt, kt, vv, vv, vv, ckt, cvv, xs, mod, wo)


def _ffn_kernel(na, fc, xa_ref, xb_ref, mod_ref, g_ref, wg_ref, wu_ref, wd_ref, o_ref, acc):
    i = pl.program_id(0)
    x = jnp.where(i < na, xa_ref[...], xb_ref[...])
    a = _adaln(x, g_ref[...], mod_ref[3:4, :], mod_ref[4:5, :]).astype(BF16)
    nf = wg_ref.shape[1] // fc
    for f in range(nf):
        sl = slice(f * fc, (f + 1) * fc)
        hg = jnp.dot(a, wg_ref[:, sl], preferred_element_type=F32)
        hu = jnp.dot(a, wu_ref[:, sl], preferred_element_type=F32)
        act = (_silu(hg) * hu).astype(BF16)
        part = jnp.dot(act, wd_ref[sl, :], preferred_element_type=F32)
        if f == 0:
            acc[...] = part
        else:
            acc[...] += part
    o_ref[...] = x + mod_ref[5:6, :] * acc[...]


def _ffn(hp, hs, t_len, mod, g, wg, wu, wd):
    n_p, n_s = hp.shape[0], hs.shape[0]
    d, dff = wg.shape
    tm = _row_tile(n_p, t_len)
    na, nb = n_p // tm, n_s // tm
    nt_s = t_len // tm
    fc = 256
    assert dff % fc == 0
    mod_idx = lambda i: (jnp.where(i < na, 0, 1 + (i - na) // nt_s), 0, 0)
    return pl.pallas_call(
        functools.partial(_ffn_kernel, na, fc),
        grid=(na + nb,),
        in_specs=[
            pl.BlockSpec((tm, d), lambda i: (jnp.minimum(i, na - 1), 0)),
            pl.BlockSpec((tm, d), lambda i: (jnp.maximum(i - na, 0), 0)),
            pl.BlockSpec((None, 8, d), mod_idx),
            _const_spec((1, d)),
            _const_spec((d, dff)),
            _const_spec((d, dff)),
            _const_spec((dff, d)),
        ],
        out_specs=pl.BlockSpec((tm, d), lambda i: (i, 0)),
        out_shape=jax.ShapeDtypeStruct((n_p + n_s, d), F32),
        scratch_shapes=[pltpu.VMEM((tm, d), F32)],
        compiler_params=_params(("arbitrary",)),
        name="ffn",
    )(hp, hs, mod, g, wg, wu, wd)


def _split2(x):
    hi = x.astype(BF16)
    lo = (x - hi.astype(F32)).astype(BF16)
    return hi, lo


def _gla_in_kernel(nt_s, x_ref, mod_ref, g_ref, w_ref, w1_ref, w2_ref, gb_ref, tri_ref,
                   qk_out, v_out, r_out, b_out):
    x = x_ref[...]
    tm = x.shape[0]
    a = _adaln(x, g_ref[...], mod_ref[0:1, :], mod_ref[1:2, :]).astype(BF16)
    hk = GLA_HEADS * GLA_DK
    hv = GLA_HEADS * GLA_DV
    q = jnp.dot(a, w_ref[:, 0:hk], preferred_element_type=F32)
    qk_out[:, 0:hk] = (q * (GLA_DK ** -0.5)).astype(BF16)
    qk_out[:, hk:2 * hk] = jnp.dot(a, w_ref[:, hk:2 * hk], preferred_element_type=F32).astype(BF16)
    v_out[...] = jnp.dot(a, w_ref[:, 2 * hk:2 * hk + hv], preferred_element_type=F32).astype(BF16)
    r_out[...] = jnp.dot(a, w_ref[:, 2 * hk + hv:2 * hk + 2 * hv], preferred_element_type=F32).astype(BF16)
    z1 = jnp.dot(a, w1_ref[...], preferred_element_type=F32)
    z = jnp.dot(z1.astype(BF16), w2_ref[...], preferred_element_type=F32) + gb_ref[...]
    gate = (jnp.minimum(z, 0.0) - jnp.log(1.0 + jnp.exp(-jnp.abs(z)))) * (1.0 / GLA_GATE_TAU)
    c = GLA_CHUNK
    tri = tri_ref[...]
    for j in range(tm // c):
        rows = slice(j * c, (j + 1) * c)
        for dr in range(2):
            cols = slice(dr * hk, (dr + 1) * hk)
            hi, lo = _split2(gate[rows, cols])
            b_out[rows, cols] = (jnp.dot(tri[dr], hi, preferred_element_type=F32)
                                 + jnp.dot(tri[dr], lo, preferred_element_type=F32))


def _gla_in(h, n_p, t_len, mod, g, w_in, w1, w2, gate_b):
    n, d = h.shape
    tm = _row_tile(n_p, t_len, cap=1024)
    na = n_p // tm
    nt_s = t_len // tm
    hk, hv = GLA_HEADS * GLA_DK, GLA_HEADS * GLA_DV
    c = GLA_CHUNK
    lower = jnp.tril(jnp.ones((c, c), F32))
    tri = jnp.stack([lower, lower.T]).astype(BF16)
    mod_idx = lambda i: (jnp.where(i < na, 0, 1 + (i - na) // nt_s), 0, 0)
    row = lambda i: (i, 0)
    return pl.pallas_call(
        functools.partial(_gla_in_kernel, nt_s),
        grid=(n // tm,),
        in_specs=[
            pl.BlockSpec((tm, d), row),
            pl.BlockSpec((None, 8, d), mod_idx),
            _const_spec((1, d)),
            _const_spec(w_in.shape),
            _const_spec(w1.shape),
            _const_spec(w2.shape),
            _const_spec((1, 2 * hk)),
            _const_spec((2, c, c)),
        ],
        out_specs=[pl.BlockSpec((tm, 2 * hk), row), pl.BlockSpec((tm, hv), row),
                   pl.BlockSpec((tm, hv), row), pl.BlockSpec((tm, 2 * hk), row)],
        out_shape=[jax.ShapeDtypeStruct((n, 2 * hk), BF16), jax.ShapeDtypeStruct((n, hv), BF16),
                   jax.ShapeDtypeStruct((n, hv), BF16), jax.ShapeDtypeStruct((n, 2 * hk), F32)],
        compiler_params=_params(("arbitrary",)),
        name="gla_in",
    )(h, mod, g, w_in, w1, w2, gate_b, tri)


def _gla_direction(qk_ref, v_ref, b_ref, st, o_ref, forward, qd_s, att_s, upd_s, dec_s):
    c = GLA_CHUNK
    ncb = qk_ref.shape[0] // c
    hk = GLA_HEADS * GLA_DK
    dk, dv = GLA_DK, GLA_DV
    ri = lax.broadcasted_iota(I32, (c, c), 0)
    ci_ = lax.broadcasted_iota(I32, (c, c), 1)
    keep = (ci_ <= ri) if forward else (ci_ >= ri)

    for ch in range(ncb):
        rows = slice(ch * c, (ch + 1) * c)
        b = b_ref[rows, :]
        if forward:
            b_end, b_mid = b[c - 1:c, :], b[c // 2 - 1:c // 2, :]
        else:
            b_end, b_mid = b[0:1, :], b[c // 2:c // 2 + 1, :]
        qa = qk_ref[rows, 0:hk] * jnp.exp(b - b_mid)
        ka = qk_ref[rows, hk:2 * hk] * jnp.exp(b_mid - b)
        qd_s[rows, :] = (qa * jnp.exp(b_mid)).astype(BF16)
        kd = (ka * jnp.exp(b_end - b_mid)).astype(BF16)
        dec_s[ch:ch + 1, :] = jnp.exp(b_end)
        qab, kab = qa.astype(BF16), ka.astype(BF16)
        for h in range(GLA_HEADS):
            ks = slice(h * dk, (h + 1) * dk)
            vs = slice(h * dv, (h + 1) * dv)
            att = lax.dot_general(qab[:, ks], kab[:, ks], (((1,), (1,)), ((), ())),
                                  preferred_element_type=F32)
            att_s[ch, h] = jnp.where(keep, att, 0.0).astype(BF16)
            upd_s[ch, h] = lax.dot_general(v_ref[rows, vs], kd[:, ks], (((0,), (0,)), ((), ())),
                                           preferred_element_type=F32)

    for ch in (range(ncb) if forward else reversed(range(ncb))):
        rows = slice(ch * c, (ch + 1) * c)
        for h in range(GLA_HEADS):
            ks = slice(h * dk, (h + 1) * dk)
            vs = slice(h * dv, (h + 1) * dv)
            s = st[h]
            o = lax.dot_general(qd_s[rows, ks], s.astype(BF16), (((1,), (1,)), ((), ())),
                                preferred_element_type=F32)
            o = o + jnp.dot(att_s[ch, h], v_ref[rows, vs], preferred_element_type=F32)
            o_ref[rows, vs] = o.astype(o_ref.dtype)
            st[h] = s * dec_s[ch:ch + 1, ks] + upd_s[ch, h]


def _gla_scan_kernel(shared, *refs):
    if shared:
        qkf_ref, vf_ref, bf_ref, bb_ref = refs[:4]
        qkb_ref, vb_ref = qkf_ref, vf_ref
        rest = refs[4:]
    else:
        qkf_ref, vf_ref, bf_ref, qkb_ref, vb_ref, bb_ref = refs[:6]
        rest = refs[6:]
    sf0_ref, sb0_ref, of_ref, ob_ref, sf_ref, sb_ref, stf, stb, qd_s, att_s, upd_s, dec_s = rest
    i = pl.program_id(1)
    nblk = pl.num_programs(1)

    @pl.when(i == 0)
    def _():
        for h in range(GLA_HEADS):
            stf[h] = sf0_ref[h].T
            stb[h] = sb0_ref[h].T

    _gla_direction(qkf_ref, vf_ref, bf_ref, stf, of_ref, True, qd_s, att_s, upd_s, dec_s)
    _gla_direction(qkb_ref, vb_ref, bb_ref, stb, ob_ref, False, qd_s, att_s, upd_s, dec_s)

    @pl.when(i == nblk - 1)
    def _():
        for h in range(GLA_HEADS):
            sf_ref[h] = stf[h].T
            sb_ref[h] = stb[h].T


def _gla_scan(qk, v, bc, sf0, sb0, b_off, nb, t_len):
    n = qk.shape[0]
    assert n % t_len == 0
    ns = n // t_len
    tb = _row_tile(t_len)
    nblk = t_len // tb
    qk3 = qk.reshape(ns, t_len, qk.shape[1])
    v3 = v.reshape(ns, t_len, v.shape[1])
    bc3 = bc.reshape(ns, t_len, bc.shape[1])
    h, dk, dv = GLA_HEADS, GLA_DK, GLA_DV
    hk = h * dk
    st_spec = pl.BlockSpec((None, h, dk, dv), lambda b, i: (b, 0, 0, 0))
    fwd = lambda b, i: (b_off + b, i, 0)
    bwd = lambda b, i: (b_off + b, nblk - 1 - i, 0)
    bwd_b = pl.BlockSpec((None, tb, hk), lambda b, i: (b_off + b, nblk - 1 - i, 1))
    shared = nblk == 1
    in_specs = [pl.BlockSpec((None, tb, 2 * hk), fwd), pl.BlockSpec((None, tb, h * dv), fwd),
                pl.BlockSpec((None, tb, hk), fwd)]
    operands = [qk3, v3, bc3]
    if not shared:
        in_specs += [pl.BlockSpec((None, tb, 2 * hk), bwd), pl.BlockSpec((None, tb, h * dv), bwd)]
        operands += [qk3, v3]
    in_specs += [bwd_b, st_spec, st_spec]
    operands += [bc3, sf0, sb0]
    return pl.pallas_call(
        functools.partial(_gla_scan_kernel, shared),
        grid=(nb, nblk),
        in_specs=in_specs,
        out_specs=[pl.BlockSpec((None, tb, h * dv), lambda b, i: (b, i, 0)),
                   pl.BlockSpec((None, tb, h * dv), lambda b, i: (b, nblk - 1 - i, 0)),
                   st_spec, st_spec],
        out_shape=[jax.ShapeDtypeStruct((nb, t_len, h * dv), BF16),
                   jax.ShapeDtypeStruct((nb, t_len, h * dv), BF16),
                   jax.ShapeDtypeStruct((nb, h, dk, dv), F32),
                   jax.ShapeDtypeStruct((nb, h, dk, dv), F32)],
        scratch_shapes=[pltpu.VMEM((h, dv, dk), F32), pltpu.VMEM((h, dv, dk), F32),
                        pltpu.VMEM((tb, hk), BF16),
                        pltpu.VMEM((tb // GLA_CHUNK, h, GLA_CHUNK, GLA_CHUNK), BF16),
                        pltpu.VMEM((tb // GLA_CHUNK, h, dv, dk), F32),
                        pltpu.VMEM((max(tb // GLA_CHUNK, 8), hk), F32)],
        compiler_params=_params(("arbitrary", "arbitrary")),
        name="gla_scan",
    )(*operands)


def _gla_out_kernel(na, ofa_ref, oba_ref, ofb_ref, obb_ref, r_ref, h_ref, mod_ref, on_ref, wo_ref,
                    g2_ref, rw_ref, tri_ref, h_out, f_out, route_out, wt_out, cnt_out, run):
    i = pl.program_id(0)
    is_ctx = i < na
    o = (jnp.where(is_ctx, ofa_ref[...], ofb_ref[...]).astype(F32)
         + jnp.where(is_ctx, oba_ref[...], obb_ref[...]).astype(F32))
    tm = o.shape[0]
    dv = GLA_DV
    parts = []
    for hh in range(GLA_HEADS):
        oh = o[:, hh * dv:(hh + 1) * dv]
        parts.append(_rms(oh) * on_ref[...])
    on = jnp.concatenate(parts, axis=1)
    r = r_ref[...].astype(F32)
    gated = (on * _silu(r)).astype(BF16)
    h = h_ref[...] + mod_ref[2:3, :] * jnp.dot(gated, wo_ref[...], preferred_element_type=F32)
    h_out[...] = h
    f = _adaln(h, g2_ref[...], mod_ref[3:4, :], mod_ref[4:5, :])
    f_out[...] = _pack_halves(f)
    fh = f.astype(BF16)
    fl = (f - fh.astype(F32)).astype(BF16)
    logits = (jnp.dot(fh, rw_ref[0], preferred_element_type=F32)
              + jnp.dot(fl, rw_ref[0], preferred_element_type=F32)
              + jnp.dot(fh, rw_ref[1], preferred_element_type=F32))
    lane = lax.broadcasted_iota(I32, (tm, LANES), 1)
    lane_f = lane.astype(F32)
    logits = jnp.where(lane < N_EXPERTS, logits, -jnp.inf)
    m1 = jnp.max(logits, axis=-1, keepdims=True)
    i1 = jnp.min(jnp.where(logits == m1, lane_f, float(LANES)), axis=-1, keepdims=True)
    rest = jnp.where(lane_f == i1, -jnp.inf, logits)
    m2 = jnp.max(rest, axis=-1, keepdims=True)
    i2 = jnp.min(jnp.where(rest == m2, lane_f, float(LANES)), axis=-1, keepdims=True)
    e2 = jnp.exp(m2 - m1)
    w1 = 1.0 / (1.0 + e2)
    w2 = e2 / (1.0 + e2)
    wt_out[...] = jnp.where(lane == 0, w1, jnp.where(lane == 1, w2, 0.0))
    sel1 = lane_f == i1
    sel2 = lane_f == i2
    picked = jnp.where(sel1, 1.0, jnp.where(sel2, 1.0, 0.0))

    @pl.when(i == 0)
    def _():
        run[...] = jnp.zeros_like(run)

    before = jnp.dot(tri_ref[...], picked.astype(BF16), preferred_element_type=F32) + run[0:1, :]
    rank1 = jnp.sum(jnp.where(sel1, before, 0.0), axis=-1, keepdims=True)
    rank2 = jnp.sum(jnp.where(sel2, before, 0.0), axis=-1, keepdims=True)
    run[...] = run[...] + jnp.sum(picked, axis=0, keepdims=True)
    cnt_out[...] = run[...]
    routing = jnp.where(lane == 0, i1, jnp.where(lane == 1, i2,
                        jnp.where(lane == 2, rank1, jnp.where(lane == 3, rank2, 0.0))))
    route_out[...] = routing.T[0:8, :]


def _gla_out(of_p, ob_p, of_s, ob_s, r, h, n_p, t_len, mod, out_norm, wo, g2, router_w):
    n, d = h.shape
    tm = _row_tile(n_p, t_len)
    na = n_p // tm
    nt_s = t_len // tm
    mod_idx = lambda i: (jnp.where(i < na, 0, 1 + (i - na) // nt_s), 0, 0)
    row = lambda i: (i, 0)
    ctx_row = lambda i: (jnp.minimum(i, na - 1), 0)
    lat_row = lambda i: (jnp.maximum(i - na, 0), 0)
    tri = jnp.tril(jnp.ones((tm, tm), F32), -1).astype(BF16)
    return pl.pallas_call(
        functools.partial(_gla_out_kernel, na),
        grid=(n // tm,),
        in_specs=[
            pl.BlockSpec((tm, d), ctx_row),
            pl.BlockSpec((tm, d), ctx_row),
            pl.BlockSpec((tm, d), lat_row),
            pl.BlockSpec((tm, d), lat_row),
            pl.BlockSpec((tm, d), row),
            pl.BlockSpec((tm, d), row),
            pl.BlockSpec((None, 8, d), mod_idx),
            _const_spec((1, GLA_DV)),
            _const_spec((d, d)),
            _const_spec((1, d)),
            _const_spec((2, d, LANES)),
            _const_spec((tm, tm)),
        ],
        out_specs=[pl.BlockSpec((tm, d), row), pl.BlockSpec((tm, d // 2), row),
                   pl.BlockSpec((8, tm), lambda i: (0, i)), pl.BlockSpec((tm, LANES), row),
                   _const_spec((8, LANES))],
        out_shape=[jax.ShapeDtypeStruct((n, d), F32), jax.ShapeDtypeStruct((n, d // 2), jnp.uint32),
                   jax.ShapeDtypeStruct((8, n), F32), jax.ShapeDtypeStruct((n, LANES), F32),
                   jax.ShapeDtypeStruct((8, LANES), F32)],
        scratch_shapes=[pltpu.VMEM((8, LANES), F32)],
        compiler_params=_params(("arbitrary",)),
        name="gla_out_router",
    )(of_p, ob_p, of_s, ob_s, r, h, mod, out_norm, wo, g2, router_w, tri)


SC_CORES = 2
SC_SUBCORES = 16
SC_CHUNK = 64


def _sc_gather_rows(table, idx):
    nw = SC_CORES * SC_SUBCORES
    b = idx.shape[0]
    d = table.shape[1]
    assert b % (nw * SC_CHUNK) == 0
    per_w = b // nw
    n_chunks = per_w // SC_CHUNK
    mesh = plsc.VectorSubcoreMesh(core_axis_name="c", subcore_axis_name="s",
                                  num_cores=SC_CORES, num_subcores=SC_SUBCORES)

    assert n_chunks % 2 == 0

    def body(table_hbm, idx_hbm, out_hbm, idx_v, rows_v, gsem, wsem):
        wid = lax.axis_index("s") * SC_CORES + lax.axis_index("c")
        base = wid * per_w
        pltpu.sync_copy(idx_hbm.at[wid], idx_v)

        def gather(j, slot):
            return pltpu.make_async_copy(table_hbm.at[idx_v.at[j]], rows_v.at[slot], gsem.at[slot])

        def write(j, slot):
            return pltpu.make_async_copy(rows_v.at[slot], out_hbm.at[pl.ds(base + j * SC_CHUNK, SC_CHUNK)],
                                         wsem.at[slot])

        gather(0, 0).start()

        @pl.loop(0, n_chunks, step=2)
        def _(j):
            for slot in range(2):
                jj = j + slot
                gather(jj, slot).wait()

                @pl.when(jj + 1 < n_chunks)
                def _():
                    @pl.when(jj >= 1)
                    def _():
                        write(jj - 1, 1 - slot).wait()

                    gather(jj + 1, 1 - slot).start()

                write(jj, slot).start()

        write(n_chunks - 2, 0).wait()
        write(n_chunks - 1, 1).wait()

    return pl.kernel(
        body,
        out_type=jax.ShapeDtypeStruct((b, d), table.dtype),
        mesh=mesh,
        scratch_types=[pltpu.VMEM((n_chunks, SC_CHUNK), I32),
                       pltpu.VMEM((2, SC_CHUNK, d), table.dtype),
                       pltpu.SemaphoreType.DMA((2,)),
                       pltpu.SemaphoreType.DMA((2,))],
        name="sc_gather_rows",
    )(table, idx.reshape(nw, n_chunks, SC_CHUNK))


def _sc_scatter_rows(rows, idx2, p):
    nw = SC_CORES * SC_SUBCORES
    n, d = rows.shape
    assert n % (nw * SC_CHUNK) == 0
    per_w = n // nw
    n_chunks = per_w // SC_CHUNK
    mesh = plsc.VectorSubcoreMesh(core_axis_name="c", subcore_axis_name="s",
                                  num_cores=SC_CORES, num_subcores=SC_SUBCORES)

    assert n_chunks % 2 == 0

    def body(rows_hbm, idx0_hbm, idx1_hbm, out_hbm, idx0_v, idx1_v, rows_v, rsem, ssem):
        wid = lax.axis_index("s") * SC_CORES + lax.axis_index("c")
        base = wid * per_w
        pltpu.sync_copy(idx0_hbm.at[wid], idx0_v)
        pltpu.sync_copy(idx1_hbm.at[wid], idx1_v)

        def read(j, slot):
            return pltpu.make_async_copy(rows_hbm.at[pl.ds(base + j * SC_CHUNK, SC_CHUNK)], rows_v.at[slot],
                                         rsem.at[slot])

        read(0, 0).start()

        @pl.loop(0, n_chunks, step=2)
        def _(j):
            for slot in range(2):
                jj = j + slot
                read(jj, slot).wait()

                @pl.when(jj + 1 < n_chunks)
                def _():
                    read(jj + 1, 1 - slot).start()

                s0 = pltpu.make_async_copy(rows_v.at[slot], out_hbm.at[idx0_v.at[jj]], ssem.at[0])
                s1 = pltpu.make_async_copy(rows_v.at[slot], out_hbm.at[idx1_v.at[jj]], ssem.at[1])
                s0.start()
                s1.start()
                s0.wait()
                s1.wait()

    return pl.kernel(
        body,
        out_type=jax.ShapeDtypeStruct((p, d), rows.dtype),
        mesh=mesh,
        scratch_types=[pltpu.VMEM((n_chunks, SC_CHUNK), I32),
                       pltpu.VMEM((n_chunks, SC_CHUNK), I32),
                       pltpu.VMEM((2, SC_CHUNK, d), rows.dtype),
                       pltpu.SemaphoreType.DMA((2,)),
                       pltpu.SemaphoreType.DMA((2,))],
        name="sc_scatter_rows",
    )(rows, idx2[0].reshape(nw, n_chunks, SC_CHUNK), idx2[1].reshape(nw, n_chunks, SC_CHUNK))


EXPERT_SUB = 256


def _pack_halves(x):
    k = x.shape[1] // 2
    lo = lax.bitcast_convert_type(x[:, :k].astype(BF16).astype(F32), jnp.uint32)
    hi = lax.bitcast_convert_type(x[:, k:].astype(BF16).astype(F32), jnp.uint32)
    return (lo >> 16) | (hi & jnp.uint32(0xFFFF0000))


def _unpack_halves(w):
    lo = lax.bitcast_convert_type(w << 16, F32).astype(BF16)
    hi = lax.bitcast_convert_type(w & jnp.uint32(0xFFFF0000), F32).astype(BF16)
    return lo, hi


def _expert_kernel(te_ref, ts_ref, nu_ref, nv_ref, x_ref, wg_ref, wu_ref, wd_ref, y_ref, acc, xb):
    i = pl.program_id(0)
    f = pl.program_id(1)
    nf = pl.num_programs(1)
    tm = x_ref.shape[0]
    half = x_ref.shape[1]
    sub = EXPERT_SUB
    nvalid = nv_ref[i]

    @pl.when(f == 0)
    def _():
        acc[...] = jnp.zeros_like(acc)

    @pl.when((f == 0) & (nvalid > 0))
    def _():
        rid = lax.broadcasted_iota(I32, (tm, half), 0)
        lo, hi = _unpack_halves(x_ref[...])
        zero = jnp.zeros_like(lo)
        xb[:, 0:half] = jnp.where(rid < nvalid, lo, zero)
        xb[:, half:2 * half] = jnp.where(rid < nvalid, hi, zero)

    def compute(rows):
        x = xb[rows, :]
        hg = jnp.dot(x, wg_ref[...].astype(BF16), preferred_element_type=F32)
        hu = jnp.dot(x, wu_ref[...].astype(BF16), preferred_element_type=F32)
        act = (_silu(hg) * hu).astype(BF16)
        acc[rows, :] += jnp.dot(act, wd_ref[...].astype(BF16), preferred_element_type=F32)

    for k in range(1, tm // sub + 1):
        @pl.when((nvalid > (k - 1) * sub) & (nvalid <= k * sub))
        def _():
            compute(slice(0, k * sub))

    @pl.when(f == nf - 1)
    def _():
        y_ref[...] = _pack_halves(acc[...])


def _experts(x, tile_expert, tile_src, n_used, n_valid, wg, wu, wd, tm, tf):
    p, half = x.shape
    d = 2 * half
    ne, _, dff = wg.shape
    nf = dff // tf
    assert dff % tf == 0 and tm % EXPERT_SUB == 0

    def fidx(i, f, nu):
        return jnp.where(i < nu[0], f, nf - 1)

    grid_spec = pltpu.PrefetchScalarGridSpec(
        num_scalar_prefetch=4,
        grid=(p // tm, nf),
        in_specs=[
            pl.BlockSpec((tm, half), lambda i, f, te, ts, nu, nv: (ts[i], 0)),
            pl.BlockSpec((None, d, tf), lambda i, f, te, ts, nu, nv: (te[i], 0, fidx(i, f, nu))),
            pl.BlockSpec((None, d, tf), lambda i, f, te, ts, nu, nv: (te[i], 0, fidx(i, f, nu))),
            pl.BlockSpec((None, tf, d), lambda i, f, te, ts, nu, nv: (te[i], fidx(i, f, nu), 0)),
        ],
        out_specs=pl.BlockSpec((tm, half), lambda i, f, te, ts, nu, nv: (i, 0)),
        scratch_shapes=[pltpu.VMEM((tm, d), F32), pltpu.VMEM((tm, d), BF16)],
    )
    return pl.pallas_call(
        _expert_kernel,
        grid_spec=grid_spec,
        out_shape=jax.ShapeDtypeStruct((p, half), jnp.uint32),
        compiler_params=_params(("arbitrary", "arbitrary"), 60 * 1024 * 1024),
        name="moe_experts",
    )(tile_expert, tile_src, n_used, n_valid, x, wg, wu, wd)


def _combine_kernel(wt_ref, h_ref, mod_ref, y0_ref, y1_ref, o_ref):
    wt = wt_ref[...]
    half = y0_ref.shape[1]
    lo0, hi0 = _unpack_halves(y0_ref[...])
    lo1, hi1 = _unpack_halves(y1_ref[...])
    w0, w1 = wt[:, 0:1], wt[:, 1:2]
    h = h_ref[...]
    gate = mod_ref[5:6, :]
    o_ref[:, 0:half] = h[:, 0:half] + gate[:, 0:half] * (w0 * lo0.astype(F32) + w1 * lo1.astype(F32))
    o_ref[:, half:] = h[:, half:] + gate[:, half:] * (w0 * hi0.astype(F32) + w1 * hi1.astype(F32))


def _combine(yg, wt, h, mod, row_off, n_rows, n_mod, mod_off, tm):
    d = h.shape[1]
    nt = n_rows // tm
    toff = row_off // tm
    per_mod = n_rows // n_mod // tm
    return pl.pallas_call(
        _combine_kernel,
        grid=(nt,),
        in_specs=[
            pl.BlockSpec((tm, LANES), lambda i: (toff + i, 0)),
            pl.BlockSpec((tm, d), lambda i: (toff + i, 0)),
            pl.BlockSpec((None, 8, d), lambda i: (mod_off + i // per_mod, 0, 0)),
            pl.BlockSpec((None, tm, d // 2), lambda i: (0, i, 0)),
            pl.BlockSpec((None, tm, d // 2), lambda i: (1, i, 0)),
        ],
        out_specs=pl.BlockSpec((tm, d), lambda i: (i, 0)),
        out_shape=jax.ShapeDtypeStruct((n_rows, d), F32),
        compiler_params=_params(("arbitrary",)),
        name="moe_combine",
    )(wt, h, mod, yg, yg)


def _route(route, counts, n, tm):
    cnt = counts[0, :N_EXPERTS].astype(I32)
    padded = ((cnt + tm - 1) // tm) * tm
    gend = jnp.cumsum(padded)
    goff = gend - padded
    e = route[0:2].astype(I32)
    rank = route[2:4].astype(I32)
    onehot = e[:, :, None] == jnp.arange(N_EXPERTS, dtype=I32)[None, None, :]
    dest = jnp.sum(jnp.where(onehot, goff[None, None, :], 0), axis=-1) + rank
    p = 2 * n + N_EXPERTS * tm
    n_used = gend[-1] // tm
    tiles = jnp.arange(p // tm, dtype=I32)
    tile_src = jnp.minimum(tiles, n_used - 1)
    tile_expert = jnp.minimum(jnp.sum((gend[None, :] <= (tile_src * tm)[:, None]).astype(I32), axis=1),
                              N_EXPERTS - 1)
    used = goff[tile_expert] + cnt[tile_expert]
    n_valid = jnp.where(tiles < n_used, jnp.clip(used - tiles * tm, 0, tm), 0).astype(I32)
    return dest, p, tile_expert, tile_src, n_used.reshape(1).astype(I32), n_valid


def kernel(x_prompt, x_sample, cache_k, cache_v, state_fwd, state_bwd, c, c_ctx, ada_w, ada_b, norm1_g, norm2_g, attn_w_qkv, attn_q_norm, attn_k_norm, attn_sink, attn_w_o, gla_w_in, gla_gate_w1, gla_gate_w2, gla_gate_b, gla_out_norm, gla_w_o, ffn_w_gate, ffn_w_up, ffn_w_down, moe_router, moe_w_gate, moe_w_up, moe_w_down):
    bp, seq, d = x_prompt.shape
    db, t_len, _ = x_sample.shape
    n_p, n_s = bp * seq, db * t_len
    n = n_p + n_s
    xp = x_prompt.reshape(n_p, d)
    xs = x_sample.reshape(n_s, d)

    cond = jnp.concatenate([c_ctx[None, :], c], axis=0)
    assert cond.shape[0] <= 8
    cond_t = jnp.pad(cond, ((0, 8 - cond.shape[0]), (0, 0))).T
    mods = _modulation(cond_t, cond.shape[0], ada_w, ada_b)

    nk = N_KV_HEADS * HEAD_DIM
    qn = jnp.tile(attn_q_norm[0], N_HEADS)[None, :]
    kn = jnp.tile(attn_k_norm[0], N_KV_HEADS)[None, :]
    q, kt, vv, kv = _qkv(xp, xs, t_len, mods[0], norm1_g[0][None, :], attn_w_qkv[0].astype(BF16), qn, kn)
    wo0 = attn_w_o[0].astype(BF16)
    sink = attn_sink[0]
    hp = _ctx_attention(sink, q, kt, vv, xp, seq, mods[0], wo0)
    ck = cache_k[:, 0].astype(BF16)
    cv = cache_v[:, 0].astype(BF16)
    ckt = jnp.concatenate([ck, ck], axis=-1).transpose(0, 2, 3, 1)
    cvv = jnp.concatenate([cv, cv], axis=-1).reshape(db, cv.shape[1], N_KV_HEADS * LANES)
    hs = _lat_attention(sink, q, kt, vv, ckt, cvv, xs, n_p, t_len, mods[0], wo0)
    new_k = kv[:n_p, :nk].reshape(bp, 1, seq, N_KV_HEADS, HEAD_DIM)
    new_v = kv[:n_p, nk:].reshape(bp, 1, seq, N_KV_HEADS, HEAD_DIM)
    h = _ffn(hp, hs, t_len, mods[0], norm2_g[0][None, :], ffn_w_gate[0].astype(BF16),
             ffn_w_up[0].astype(BF16), ffn_w_down[0].astype(BF16))

    hk = GLA_HEADS * GLA_DK
    rank = GLA_GATE_RANK
    w1 = jnp.concatenate([gla_gate_w1[0, 0], gla_gate_w1[0, 1]], axis=1)
    w1 = jnp.pad(w1, ((0, 0), (0, LANES - 2 * rank))).astype(BF16)
    w2 = jnp.zeros((LANES, 2 * hk), F32)
    w2 = w2.at[0:rank, 0:hk].set(gla_gate_w2[0, 0]).at[rank:2 * rank, hk:].set(gla_gate_w2[0, 1]).astype(BF16)
    gate_b = gla_gate_b[0].reshape(1, 2 * hk)
    qk, v, r, bc = _gla_in(h, n_p, t_len, mods[1], norm1_g[1][None, :], gla_w_in[0].astype(BF16),
                           w1, w2, gate_b)
    zeros = jnp.zeros((bp, GLA_HEADS, GLA_DK, GLA_DV), F32)
    of_p, ob_p, new_sf, new_sb = _gla_scan(qk, v, bc, zeros, zeros, 0, bp, seq)
    of_s, ob_s, _, _ = _gla_scan(qk, v, bc, state_fwd[:, 0], state_bwd[:, 0], n_p // t_len, db, t_len)
    rw = jnp.pad(moe_router[0], ((0, 0), (0, LANES - N_EXPERTS)))
    rw_hi = rw.astype(BF16)
    rw_lo = (rw - rw_hi.astype(F32)).astype(BF16)
    h, f, route, wt, counts = _gla_out(
        of_p.reshape(n_p, d), ob_p.reshape(n_p, d), of_s.reshape(n_s, d), ob_s.reshape(n_s, d), r, h,
        n_p, t_len, mods[1], gla_out_norm[0][None, :], gla_w_o[0].astype(BF16), norm2_g[1][None, :],
        jnp.stack([rw_hi, rw_lo]))

    tm_e = 1024
    dest_k, p_rows, tile_expert, tile_src, n_used, n_valid = _route(route, counts, n, tm_e)
    xg = _sc_scatter_rows(f, dest_k, p_rows)
    y = _experts(xg, tile_expert, tile_src, n_used, n_valid, moe_w_gate[0], moe_w_up[0],
                 moe_w_down[0], tm_e, 512)
    yg_p = _sc_gather_rows(y, dest_k[:, :n_p].reshape(2 * n_p)).reshape(2, n_p, d // 2)
    yg_s = _sc_gather_rows(y, dest_k[:, n_p:].reshape(2 * n_s)).reshape(2, n_s, d // 2)
    tm_c = _row_tile(n_p, t_len)
    y_p = _combine(yg_p, wt, h, mods[1], 0, n_p, 1, 0, tm_c)
    y_s = _combine(yg_s, wt, h, mods[1], n_p, n_s, db, 1, tm_c)

    return (y_p.reshape(bp, seq, d), y_s.reshape(db, t_len, d), new_k, new_v,
            new_sf[:, None], new_sb[:, None])
```

```python
import functools
import math

import jax
import jax.numpy as jnp
from jax import lax
from jax.experimental import pallas as pl
from jax.experimental.pallas import tpu as pltpu
from jax.experimental.pallas import tpu_sc as plsc

F32 = jnp.float32
BF16 = jnp.bfloat16
I32 = jnp.int32

D_MODEL = 1024
N_HEADS = 16
N_KV_HEADS = 4
HEAD_DIM = 64
GRID_W = 64
WINDOW = 128
ATTN_BLOCK = 128
ROPE_THETA = 10000.0
GLA_HEADS = 4
GLA_DK = 128
GLA_DV = 256
GLA_GATE_RANK = 16
GLA_GATE_TAU = 16.0
GLA_CHUNK = 64
N_EXPERTS = 8
NORM_EPS = 1e-6
NEG_INF = -1e30

LANES = 128
VMEM_LIMIT = 56 * 1024 * 1024


def _params(sem, vmem=VMEM_LIMIT):
    return pltpu.CompilerParams(dimension_semantics=sem, vmem_limit_bytes=vmem)


def _row_tile(*counts, cap=512):
    t = cap
    while any(c % t for c in counts):
        t //= 2
    assert t >= 8
    return t


def _rms(x):
    return x * lax.rsqrt(jnp.mean(x * x, axis=-1, keepdims=True) + NORM_EPS)


def _adaln(x, g, shift, scale):
    return _rms(x) * (g * (1.0 + scale)) + shift


def _silu(x):
    return x * jax.nn.sigmoid(x)


def _const_spec(shape):
    nd = len(shape)
    return pl.BlockSpec(shape, lambda *_: (0,) * nd)


def _mod_kernel(n_cond, ct_ref, w_ref, b_ref, o_ref):
    ct = ct_ref[...]
    s = _silu(ct)
    w = w_ref[...]
    rows = [jnp.sum(w * s[:, r:r + 1], axis=0, keepdims=True) for r in range(n_cond)]
    rows += [jnp.zeros_like(rows[0])] * (8 - n_cond)
    o_ref[...] = jnp.concatenate(rows, axis=0) + b_ref[...]


def _modulation(cond_t, n_cond, ada_w, ada_b):
    depth, d, n6 = ada_w.shape
    tn = 1024
    out = pl.pallas_call(
        functools.partial(_mod_kernel, n_cond),
        grid=(depth, n6 // tn),
        in_specs=[
            pl.BlockSpec((d, 8), lambda l, j: (0, 0)),
            pl.BlockSpec((None, d, tn), lambda l, j: (l, 0, j)),
            pl.BlockSpec((None, 1, tn), lambda l, j: (l, 0, j)),
        ],
        out_specs=pl.BlockSpec((None, 8, tn), lambda l, j: (l, 0, j)),
        out_shape=jax.ShapeDtypeStruct((depth, 8, n6), F32),
        compiler_params=_params(("arbitrary", "arbitrary")),
        name="modulation",
    )(cond_t, ada_w, ada_b.reshape(depth, 1, n6))
    m = out[:, :n_cond].reshape(depth, n_cond, 6, d)
    return jnp.pad(m, ((0, 0), (0, 0), (0, 2), (0, 0)))


LOG2E = math.log2(math.e)


def _dup_half(k2, half):
    lane = lax.broadcasted_iota(I32, k2.shape, 1)
    lo = lane < HEAD_DIM
    r = pltpu.roll(k2, HEAD_DIM, 1)
    return jnp.where(lo, k2, r) if half == 0 else jnp.where(lo, r, k2)


QKV_PARTS = 2


def _qkv_kernel(na, xa_ref, xb_ref, mod_ref, g_ref, w_ref, qn_ref, kn_ref, bd_ref,
                cos_ref, sin_ref, q_out, kt_out, vv_out, kv_out):
    i = pl.program_id(0)
    is_ctx = i < na
    tm = xa_ref.shape[0] // QKV_PARTS
    for part in range(QKV_PARTS):
        _qkv_rows(slice(part * tm, (part + 1) * tm), tm, is_ctx, xa_ref, xb_ref, mod_ref, g_ref, w_ref,
                  qn_ref, kn_ref, bd_ref, cos_ref, sin_ref, q_out, kt_out, vv_out, kv_out)


def _qkv_rows(rows, tm, is_ctx, xa_ref, xb_ref, mod_ref, g_ref, w_ref, qn_ref, kn_ref, bd_ref,
              cos_ref, sin_ref, q_out, kt_out, vv_out, kv_out):
    x = jnp.where(is_ctx, xa_ref[rows, :], xb_ref[rows, :])
    a = _adaln(x, g_ref[...], mod_ref[0:1, :], mod_ref[1:2, :])
    y = jnp.dot(a.astype(BF16), w_ref[...], preferred_element_type=F32)
    cos = jnp.where(is_ctx, 1.0, cos_ref[rows, :])
    sin = jnp.where(is_ctx, 0.0, sin_ref[rows, :])
    lane = lax.broadcasted_iota(I32, (tm, LANES), 1)
    first16 = (lane % 32) < 16

    def norm_rope(z, wt):
        ss = jnp.dot((z * z).astype(BF16), bd_ref[...], preferred_element_type=F32)
        zn = z * lax.rsqrt(ss * (1.0 / HEAD_DIM) + NORM_EPS) * wt
        outs = []
        for c in range(2):
            t = zn[:, c * LANES:(c + 1) * LANES]
            partner = jnp.where(first16, pltpu.roll(t, LANES - 16, 1), pltpu.roll(t, 16, 1))
            outs.append(t * cos + partner * sin)
        return jnp.concatenate(outs, axis=1)

    nq = N_HEADS * HEAD_DIM
    nk = N_KV_HEADS * HEAD_DIM
    scale = HEAD_DIM ** -0.5 * LOG2E
    for s in range(nq // 256):
        sl = slice(s * 256, (s + 1) * 256)
        q_out[rows, sl] = (norm_rope(y[:, sl], qn_ref[:, sl]) * scale).astype(BF16)
    k = norm_rope(y[:, nq:nq + nk], kn_ref[...])
    v = y[:, nq + nk:nq + 2 * nk]
    for g in range(N_KV_HEADS):
        c = g // 2
        kk = _dup_half(k[:, c * LANES:(c + 1) * LANES], g % 2)
        kt_out[g, :, rows] = kk.T.astype(BF16)
        vv_out[rows, g * LANES:(g + 1) * LANES] = _dup_half(v[:, c * LANES:(c + 1) * LANES], g % 2).astype(BF16)

    @pl.when(is_ctx)
    def _():
        kv_out[rows, 0:nk] = k
        kv_out[rows, nk:2 * nk] = v


def _rope_tables(t_len):
    pos = jnp.arange(t_len)
    row = (pos // GRID_W).astype(F32)[:, None]
    col = (pos % GRID_W).astype(F32)[:, None]
    half = HEAD_DIM // 2
    inv = ROPE_THETA ** (-jnp.arange(0, half, 2, dtype=F32) / half)[None, :]
    ar, ac = row * inv, col * inv
    cos = jnp.concatenate([jnp.cos(ar), jnp.cos(ar), jnp.cos(ac), jnp.cos(ac)], axis=1)
    sin = jnp.concatenate([-jnp.sin(ar), jnp.sin(ar), -jnp.sin(ac), jnp.sin(ac)], axis=1)
    return jnp.tile(cos, (1, 2)), jnp.tile(sin, (1, 2))


def _qkv(xp, xs, t_len, mod, g, w, qn, kn):
    n_p, n_s = xp.shape[0], xs.shape[0]
    tm = _row_tile(n_p, t_len, cap=1024)
    na, nb = n_p // tm, n_s // tm
    nt_s = t_len // tm
    n = n_p + n_s
    d = D_MODEL
    nq, nk = N_HEADS * HEAD_DIM, N_KV_HEADS * HEAD_DIM
    cos, sin = _rope_tables(t_len)
    eye4 = jnp.kron(jnp.eye(4, dtype=F32), jnp.ones((HEAD_DIM, HEAD_DIM), F32)).astype(BF16)
    mod_idx = lambda i: (jnp.where(i < na, 0, 1 + (i - na) // nt_s), 0, 0)
    pos_idx = lambda i: (jnp.maximum(i - na, 0) % nt_s, 0)
    return pl.pallas_call(
        functools.partial(_qkv_kernel, na),
        grid=(na + nb,),
        in_specs=[
            pl.BlockSpec((tm, d), lambda i: (jnp.minimum(i, na - 1), 0)),
            pl.BlockSpec((tm, d), lambda i: (jnp.maximum(i - na, 0), 0)),
            pl.BlockSpec((None, 8, d), mod_idx),
            _const_spec((1, d)),
            _const_spec((d, nq + 2 * nk)),
            _const_spec((1, nq)),
            _const_spec((1, nk)),
            _const_spec((256, 256)),
            pl.BlockSpec((tm, LANES), pos_idx),
            pl.BlockSpec((tm, LANES), pos_idx),
        ],
        out_specs=[pl.BlockSpec((tm, nq), lambda i: (i, 0)),
                   pl.BlockSpec((N_KV_HEADS, LANES, tm), lambda i: (0, 0, i)),
                   pl.BlockSpec((tm, N_KV_HEADS * LANES), lambda i: (i, 0)),
                   pl.BlockSpec((tm, 2 * nk), lambda i: (jnp.minimum(i, na - 1), 0))],
        out_shape=[jax.ShapeDtypeStruct((n, nq), BF16),
                   jax.ShapeDtypeStruct((N_KV_HEADS, LANES, n), BF16),
                   jax.ShapeDtypeStruct((n, N_KV_HEADS * LANES), BF16),
                   jax.ShapeDtypeStruct((n_p, 2 * nk), F32)],
        compiler_params=_params(("arbitrary",)),
        name="qkv",
    )(xp, xs, mod, g, w, qn, kn, eye4, cos, sin)


def _attn_group(q2a, q2b, kt, vv, bias, n_bias, sinks):
    tq = q2a.shape[0]
    lane = lax.broadcasted_iota(I32, (tq, LANES), 1)
    lo = lane < HEAD_DIM
    zero = jnp.zeros_like(q2a)
    qs = jnp.concatenate([jnp.where(lo, q2a, zero), jnp.where(lo, zero, q2a),
                          jnp.where(lo, q2b, zero), jnp.where(lo, zero, q2b)], axis=0)
    s = jnp.dot(qs, kt, preferred_element_type=F32)
    ps, dens = [], []
    for h in range(4):
        sh = s[h * tq:(h + 1) * tq]
        if bias is not None:
            sh = jnp.concatenate([sh[:, :n_bias] + bias, sh[:, n_bias:]], axis=1)
        m = jnp.maximum(jnp.max(sh, axis=-1, keepdims=True), sinks[h])
        p = jnp.exp2(sh - m)
        dens.append(jnp.sum(p, axis=-1, keepdims=True) + jnp.exp2(sinks[h] - m))
        ps.append(p.astype(BF16))
    o = (jnp.dot(jnp.concatenate(ps, axis=0), vv, preferred_element_type=F32)
         / jnp.concatenate(dens, axis=0))
    oa = jnp.where(lo, o[0:tq], o[tq:2 * tq])
    ob = jnp.where(lo, o[2 * tq:3 * tq], o[3 * tq:4 * tq])
    return oa, ob


def _ctx_attn_kernel(sink_ref, q_ref, kt_ref, vv_ref, x_ref, mod_ref, wo_ref, o_ref, osc):
    for g in range(N_KV_HEADS):
        q2a = q_ref[:, (2 * g) * LANES:(2 * g + 1) * LANES]
        q2b = q_ref[:, (2 * g + 1) * LANES:(2 * g + 2) * LANES]
        sinks = [sink_ref[4 * g + j] * LOG2E for j in range(4)]
        oa, ob = _attn_group(q2a, q2b, kt_ref[g], vv_ref[:, g * LANES:(g + 1) * LANES], None, 0, sinks)
        osc[:, (2 * g) * LANES:(2 * g + 1) * LANES] = oa.astype(BF16)
        osc[:, (2 * g + 1) * LANES:(2 * g + 2) * LANES] = ob.astype(BF16)
    att = jnp.dot(osc[...], wo_ref[...], preferred_element_type=F32)
    o_ref[...] = x_ref[...] + mod_ref[2:3, :] * att


def _ctx_attention(sink, q, kt, vv, xp, seq, mod, wo):
    n_p, d = xp.shape
    nb = n_p // seq
    return pl.pallas_call(
        _ctx_attn_kernel,
        grid=(nb,),
        in_specs=[
            pl.BlockSpec(memory_space=pltpu.SMEM),
            pl.BlockSpec((seq, d), lambda b: (b, 0)),
            pl.BlockSpec((N_KV_HEADS, LANES, seq), lambda b: (0, 0, b)),
            pl.BlockSpec((seq, N_KV_HEADS * LANES), lambda b: (b, 0)),
            pl.BlockSpec((seq, d), lambda b: (b, 0)),
            pl.BlockSpec((None, 8, d), lambda b: (0, 0, 0)),
            _const_spec((d, d)),
        ],
        out_specs=pl.BlockSpec((seq, d), lambda b: (b, 0)),
        out_shape=jax.ShapeDtypeStruct((n_p, d), F32),
        scratch_shapes=[pltpu.VMEM((seq, d), BF16)],
        compiler_params=_params(("arbitrary",)),
        name="ctx_attention",
    )(sink, q, kt, vv, xp, mod, wo)


LAT_QB = 4


def _lat_attn_kernel(t_len, sink_ref, q_ref, ktp_ref, kto_ref, ktn_ref, vvp_ref, vvo_ref, vvn_ref,
                     ckt_ref, cvv_ref, x_ref, mod_ref, wo_ref, o_ref, osc):
    step = pl.program_id(1)
    tq = ATTN_BLOCK
    nloc = 3 * ATTN_BLOCK
    qi = lax.broadcasted_iota(I32, (tq, nloc), 0)
    kj = lax.broadcasted_iota(I32, (tq, nloc), 1)
    in_window = jnp.abs(qi + tq - kj) <= WINDOW
    for u in range(LAT_QB):
        n = step * LAT_QB + u
        kpos = (n - 1) * tq + kj
        bias = jnp.where(in_window & (kpos >= 0) & (kpos < t_len), 0.0, NEG_INF)
        rows = slice(u * tq, (u + 1) * tq)
        for g in range(N_KV_HEADS):
            vs = slice(g * LANES, (g + 1) * LANES)
            kts = [ktp_ref[g]] + [kto_ref[g, :, j * tq:(j + 1) * tq] for j in range(LAT_QB)] + [ktn_ref[g]]
            vvs = [vvp_ref[:, vs]] + [vvo_ref[j * tq:(j + 1) * tq, vs] for j in range(LAT_QB)] + [vvn_ref[:, vs]]
            kt = jnp.concatenate(kts[u:u + 3] + [ckt_ref[g]], axis=1)
            vv = jnp.concatenate(vvs[u:u + 3] + [cvv_ref[:, vs]], axis=0)
            q2a = q_ref[rows, (2 * g) * LANES:(2 * g + 1) * LANES]
            q2b = q_ref[rows, (2 * g + 1) * LANES:(2 * g + 2) * LANES]
            sinks = [sink_ref[4 * g + j] * LOG2E for j in range(4)]
            oa, ob = _attn_group(q2a, q2b, kt, vv, bias, nloc, sinks)
            osc[rows, (2 * g) * LANES:(2 * g + 1) * LANES] = oa.astype(BF16)
            osc[rows, (2 * g + 1) * LANES:(2 * g + 2) * LANES] = ob.astype(BF16)
    att = jnp.dot(osc[...], wo_ref[...], preferred_element_type=F32)
    o_ref[...] = x_ref[...] + mod_ref[2:3, :] * att


def _lat_attention(sink, q, kt, vv, ckt, cvv, xs, n_p, t_len, mod, wo):
    n_s, d = xs.shape
    db = n_s // t_len
    tq = ATTN_BLOCK
    ts = LAT_QB * tq
    nblk = t_len // tq
    nstep = t_len // ts
    assert n_p % ts == 0 and t_len % ts == 0
    off = n_p // tq
    npast = ckt.shape[3]
    g4 = N_KV_HEADS

    def nbr(delta):
        return lambda b, m: off + b * nblk + jnp.clip(m * LAT_QB + delta, 0, nblk - 1)

    own = lambda b, m: n_p // ts + b * nstep + m
    return pl.pallas_call(
        functools.partial(_lat_attn_kernel, t_len),
        grid=(db, nstep),
        in_specs=[
            pl.BlockSpec(memory_space=pltpu.SMEM),
            pl.BlockSpec((ts, d), lambda b, m: (own(b, m), 0)),
            pl.BlockSpec((g4, LANES, tq), lambda b, m: (0, 0, nbr(-1)(b, m))),
            pl.BlockSpec((g4, LANES, ts), lambda b, m: (0, 0, own(b, m))),
            pl.BlockSpec((g4, LANES, tq), lambda b, m: (0, 0, nbr(LAT_QB)(b, m))),
            pl.BlockSpec((tq, g4 * LANES), lambda b, m: (nbr(-1)(b, m), 0)),
            pl.BlockSpec((ts, g4 * LANES), lambda b, m: (own(b, m), 0)),
            pl.BlockSpec((tq, g4 * LANES), lambda b, m: (nbr(LAT_QB)(b, m), 0)),
            pl.BlockSpec((None, g4, LANES, npast), lambda b, m: (b, 0, 0, 0)),
            pl.BlockSpec((None, npast, g4 * LANES), lambda b, m: (b, 0, 0)),
            pl.BlockSpec((ts, d), lambda b, m: (b * nstep + m, 0)),
            pl.BlockSpec((None, 8, d), lambda b, m: (1 + b, 0, 0)),
            _const_spec((d, d)),
        ],
        out_specs=pl.BlockSpec((ts, d), lambda b, m: (b * nstep + m, 0)),
        out_shape=jax.ShapeDtypeStruct((n_s, d), F32),
        scratch_shapes=[pltpu.VMEM((ts, d), BF16)],
        compiler_params=_params(("arbitrary", "arbitrary")),
        name="lat_attention",
    )(sink, q, kt, kt, kt, vv, vv, vv, ckt, cvv, xs, mod, wo)


def _ffn_kernel(na, fc, xa_ref, xb_ref, mod_ref, g_ref, wg_ref, wu_ref, wd_ref, o_ref, acc):
    i = pl.program_id(0)
    x = jnp.where(i < na, xa_ref[...], xb_ref[...])
    a = _adaln(x, g_ref[...], mod_ref[3:4, :], mod_ref[4:5, :]).astype(BF16)
    nf = wg_ref.shape[1] // fc
    for f in range(nf):
        sl = slice(f * fc, (f + 1) * fc)
        hg = jnp.dot(a, wg_ref[:, sl], preferred_element_type=F32)
        hu = jnp.dot(a, wu_ref[:, sl], preferred_element_type=F32)
        act = (_silu(hg) * hu).astype(BF16)
        part = jnp.dot(act, wd_ref[sl, :], preferred_element_type=F32)
        if f == 0:
            acc[...] = part
        else:
            acc[...] += part
    o_ref[...] = x + mod_ref[5:6, :] * acc[...]


def _ffn(hp, hs, t_len, mod, g, wg, wu, wd):
    n_p, n_s = hp.shape[0], hs.shape[0]
    d, dff = wg.shape
    tm = _row_tile(n_p, t_len)
    na, nb = n_p // tm, n_s // tm
    nt_s = t_len // tm
    fc = 256
    assert dff % fc == 0
    mod_idx = lambda i: (jnp.where(i < na, 0, 1 + (i - na) // nt_s), 0, 0)
    return pl.pallas_call(
        functools.partial(_ffn_kernel, na, fc),
        grid=(na + nb,),
        in_specs=[
            pl.BlockSpec((tm, d), lambda i: (jnp.minimum(i, na - 1), 0)),
            pl.BlockSpec((tm, d), lambda i: (jnp.maximum(i - na, 0), 0)),
            pl.BlockSpec((None, 8, d), mod_idx),
            _const_spec((1, d)),
            _const_spec((d, dff)),
            _const_spec((d, dff)),
            _const_spec((dff, d)),
        ],
        out_specs=pl.BlockSpec((tm, d), lambda i: (i, 0)),
        out_shape=jax.ShapeDtypeStruct((n_p + n_s, d), F32),
        scratch_shapes=[pltpu.VMEM((tm, d), F32)],
        compiler_params=_params(("arbitrary",)),
        name="ffn",
    )(hp, hs, mod, g, wg, wu, wd)


def _split2(x):
    hi = x.astype(BF16)
    lo = (x - hi.astype(F32)).astype(BF16)
    return hi, lo


def _gla_in_kernel(nt_s, x_ref, mod_ref, g_ref, w_ref, w1_ref, w2_ref, gb_ref, tri_ref,
                   qk_out, v_out, r_out, b_out):
    x = x_ref[...]
    tm = x.shape[0]
    a = _adaln(x, g_ref[...], mod_ref[0:1, :], mod_ref[1:2, :]).astype(BF16)
    hk = GLA_HEADS * GLA_DK
    hv = GLA_HEADS * GLA_DV
    q = jnp.dot(a, w_ref[:, 0:hk], preferred_element_type=F32)
    qk_out[:, 0:hk] = (q * (GLA_DK ** -0.5)).astype(BF16)
    qk_out[:, hk:2 * hk] = jnp.dot(a, w_ref[:, hk:2 * hk], preferred_element_type=F32).astype(BF16)
    v_out[...] = jnp.dot(a, w_ref[:, 2 * hk:2 * hk + hv], preferred_element_type=F32).astype(BF16)
    r_out[...] = jnp.dot(a, w_ref[:, 2 * hk + hv:2 * hk + 2 * hv], preferred_element_type=F32).astype(BF16)
    z1 = jnp.dot(a, w1_ref[...], preferred_element_type=F32)
    z = jnp.dot(z1.astype(BF16), w2_ref[...], preferred_element_type=F32) + gb_ref[...]
    gate = (jnp.minimum(z, 0.0) - jnp.log(1.0 + jnp.exp(-jnp.abs(z)))) * (1.0 / GLA_GATE_TAU)
    c = GLA_CHUNK
    tri = tri_ref[...]
    for j in range(tm // c):
        rows = slice(j * c, (j + 1) * c)
        for dr in range(2):
            cols = slice(dr * hk, (dr + 1) * hk)
            hi, lo = _split2(gate[rows, cols])
            b_out[rows, cols] = (jnp.dot(tri[dr], hi, preferred_element_type=F32)
                                 + jnp.dot(tri[dr], lo, preferred_element_type=F32))


def _gla_in(h, n_p, t_len, mod, g, w_in, w1, w2, gate_b):
    n, d = h.shape
    tm = _row_tile(n_p, t_len, cap=1024)
    na = n_p // tm
    nt_s = t_len // tm
    hk, hv = GLA_HEADS * GLA_DK, GLA_HEADS * GLA_DV
    c = GLA_CHUNK
    lower = jnp.tril(jnp.ones((c, c), F32))
    tri = jnp.stack([lower, lower.T]).astype(BF16)
    mod_idx = lambda i: (jnp.where(i < na, 0, 1 + (i - na) // nt_s), 0, 0)
    row = lambda i: (i, 0)
    return pl.pallas_call(
        functools.partial(_gla_in_kernel, nt_s),
        grid=(n // tm,),
        in_specs=[
            pl.BlockSpec((tm, d), row),
            pl.BlockSpec((None, 8, d), mod_idx),
            _const_spec((1, d)),
            _const_spec(w_in.shape),
            _const_spec(w1.shape),
            _const_spec(w2.shape),
            _const_spec((1, 2 * hk)),
            _const_spec((2, c, c)),
        ],
        out_specs=[pl.BlockSpec((tm, 2 * hk), row), pl.BlockSpec((tm, hv), row),
                   pl.BlockSpec((tm, hv), row), pl.BlockSpec((tm, 2 * hk), row)],
        out_shape=[jax.ShapeDtypeStruct((n, 2 * hk), BF16), jax.ShapeDtypeStruct((n, hv), BF16),
                   jax.ShapeDtypeStruct((n, hv), BF16), jax.ShapeDtypeStruct((n, 2 * hk), F32)],
        compiler_params=_params(("arbitrary",)),
        name="gla_in",
    )(h, mod, g, w_in, w1, w2, gate_b, tri)


def _gla_direction(qk_ref, v_ref, b_ref, st, o_ref, forward, qd_s, att_s, upd_s, dec_s):
    c = GLA_CHUNK
    ncb = qk_ref.shape[0] // c
    hk = GLA_HEADS * GLA_DK
    dk, dv = GLA_DK, GLA_DV
    ri = lax.broadcasted_iota(I32, (c, c), 0)
    ci_ = lax.broadcasted_iota(I32, (c, c), 1)
    keep = (ci_ <= ri) if forward else (ci_ >= ri)

    for ch in range(ncb):
        rows = slice(ch * c, (ch + 1) * c)
        b = b_ref[rows, :]
        if forward:
            b_end, b_mid = b[c - 1:c, :], b[c // 2 - 1:c // 2, :]
        else:
            b_end, b_mid = b[0:1, :], b[c // 2:c // 2 + 1, :]
        qa = qk_ref[rows, 0:hk] * jnp.exp(b - b_mid)
        ka = qk_ref[rows, hk:2 * hk] * jnp.exp(b_mid - b)
        qd_s[rows, :] = (qa * jnp.exp(b_mid)).astype(BF16)
        kd = (ka * jnp.exp(b_end - b_mid)).astype(BF16)
        dec_s[ch:ch + 1, :] = jnp.exp(b_end)
        qab, kab = qa.astype(BF16), ka.astype(BF16)
        for h in range(GLA_HEADS):
            ks = slice(h * dk, (h + 1) * dk)
            vs = slice(h * dv, (h + 1) * dv)
            att = lax.dot_general(qab[:, ks], kab[:, ks], (((1,), (1,)), ((), ())),
                                  preferred_element_type=F32)
            att_s[ch, h] = jnp.where(keep, att, 0.0).astype(BF16)
            upd_s[ch, h] = lax.dot_general(v_ref[rows, vs], kd[:, ks], (((0,), (0,)), ((), ())),
                                           preferred_element_type=F32)

    for ch in (range(ncb) if forward else reversed(range(ncb))):
        rows = slice(ch * c, (ch + 1) * c)
        for h in range(GLA_HEADS):
            ks = slice(h * dk, (h + 1) * dk)
            vs = slice(h * dv, (h + 1) * dv)
            s = st[h]
            o = lax.dot_general(qd_s[rows, ks], s.astype(BF16), (((1,), (1,)), ((), ())),
                                preferred_element_type=F32)
            o = o + jnp.dot(att_s[ch, h], v_ref[rows, vs], preferred_element_type=F32)
            o_ref[rows, vs] = o.astype(o_ref.dtype)
            st[h] = s * dec_s[ch:ch + 1, ks] + upd_s[ch, h]


def _gla_scan_kernel(shared, *refs):
    if shared:
        qkf_ref, vf_ref, bf_ref, bb_ref = refs[:4]
        qkb_ref, vb_ref = qkf_ref, vf_ref
        rest = refs[4:]
    else:
        qkf_ref, vf_ref, bf_ref, qkb_ref, vb_ref, bb_ref = refs[:6]
        rest = refs[6:]
    sf0_ref, sb0_ref, of_ref, ob_ref, sf_ref, sb_ref, stf, stb, qd_s, att_s, upd_s, dec_s = rest
    i = pl.program_id(1)
    nblk = pl.num_programs(1)

    @pl.when(i == 0)
    def _():
        for h in range(GLA_HEADS):
            stf[h] = sf0_ref[h].T
            stb[h] = sb0_ref[h].T

    _gla_direction(qkf_ref, vf_ref, bf_ref, stf, of_ref, True, qd_s, att_s, upd_s, dec_s)
    _gla_direction(qkb_ref, vb_ref, bb_ref, stb, ob_ref, False, qd_s, att_s, upd_s, dec_s)

    @pl.when(i == nblk - 1)
    def _():
        for h in range(GLA_HEADS):
            sf_ref[h] = stf[h].T
            sb_ref[h] = stb[h].T


def _gla_scan(qk, v, bc, sf0, sb0, b_off, nb, t_len):
    n = qk.shape[0]
    assert n % t_len == 0
    ns = n // t_len
    tb = _row_tile(t_len)
    nblk = t_len // tb
    qk3 = qk.reshape(ns, t_len, qk.shape[1])
    v3 = v.reshape(ns, t_len, v.shape[1])
    bc3 = bc.reshape(ns, t_len, bc.shape[1])
    h, dk, dv = GLA_HEADS, GLA_DK, GLA_DV
    hk = h * dk
    st_spec = pl.BlockSpec((None, h, dk, dv), lambda b, i: (b, 0, 0, 0))
    fwd = lambda b, i: (b_off + b, i, 0)
    bwd = lambda b, i: (b_off + b, nblk - 1 - i, 0)
    bwd_b = pl.BlockSpec((None, tb, hk), lambda b, i: (b_off + b, nblk - 1 - i, 1))
    shared = nblk == 1
    in_specs = [pl.BlockSpec((None, tb, 2 * hk), fwd), pl.BlockSpec((None, tb, h * dv), fwd),
                pl.BlockSpec((None, tb, hk), fwd)]
    operands = [qk3, v3, bc3]
    if not shared:
        in_specs += [pl.BlockSpec((None, tb, 2 * hk), bwd), pl.BlockSpec((None, tb, h * dv), bwd)]
        operands += [qk3, v3]
    in_specs += [bwd_b, st_spec, st_spec]
    operands += [bc3, sf0, sb0]
    return pl.pallas_call(
        functools.partial(_gla_scan_kernel, shared),
        grid=(nb, nblk),
        in_specs=in_specs,
        out_specs=[pl.BlockSpec((None, tb, h * dv), lambda b, i: (b, i, 0)),
                   pl.BlockSpec((None, tb, h * dv), lambda b, i: (b, nblk - 1 - i, 0)),
                   st_spec, st_spec],
        out_shape=[jax.ShapeDtypeStruct((nb, t_len, h * dv), BF16),
                   jax.ShapeDtypeStruct((nb, t_len, h * dv), BF16),
                   jax.ShapeDtypeStruct((nb, h, dk, dv), F32),
                   jax.ShapeDtypeStruct((nb, h, dk, dv), F32)],
        scratch_shapes=[pltpu.VMEM((h, dv, dk), F32), pltpu.VMEM((h, dv, dk), F32),
                        pltpu.VMEM((tb, hk), BF16),
                        pltpu.VMEM((tb // GLA_CHUNK, h, GLA_CHUNK, GLA_CHUNK), BF16),
                        pltpu.VMEM((tb // GLA_CHUNK, h, dv, dk), F32),
                        pltpu.VMEM((max(tb // GLA_CHUNK, 8), hk), F32)],
        compiler_params=_params(("arbitrary", "arbitrary")),
        name="gla_scan",
    )(*operands)


GLA_OUT_PARTS = 4


def _gla_out_kernel(na, ofa_ref, oba_ref, ofb_ref, obb_ref, r_ref, h_ref, mod_ref, on_ref, wo_ref,
                    g2_ref, rw_ref, tri_ref, h_out, f_out, route_out, wt_out, cnt_out, run):
    i = pl.program_id(0)
    is_ctx = i < na

    @pl.when(i == 0)
    def _():
        run[...] = jnp.zeros_like(run)

    tm = tri_ref.shape[0]
    for part in range(h_ref.shape[0] // tm):
        rows = slice(part * tm, (part + 1) * tm)
        _gla_out_rows(rows, tm, is_ctx, ofa_ref, oba_ref, ofb_ref, obb_ref, r_ref, h_ref, mod_ref, on_ref,
                      wo_ref, g2_ref, rw_ref, tri_ref, h_out, f_out, route_out, wt_out, run)
    cnt_out[...] = run[...]


def _gla_out_rows(rows, tm, is_ctx, ofa_ref, oba_ref, ofb_ref, obb_ref, r_ref, h_ref, mod_ref, on_ref,
                  wo_ref, g2_ref, rw_ref, tri_ref, h_out, f_out, route_out, wt_out, run):
    o = (jnp.where(is_ctx, ofa_ref[rows, :], ofb_ref[rows, :]).astype(F32)
         + jnp.where(is_ctx, oba_ref[rows, :], obb_ref[rows, :]).astype(F32))
    dv = GLA_DV
    parts = []
    for hh in range(GLA_HEADS):
        oh = o[:, hh * dv:(hh + 1) * dv]
        parts.append(_rms(oh) * on_ref[...])
    on = jnp.concatenate(parts, axis=1)
    r = r_ref[rows, :].astype(F32)
    gated = (on * _silu(r)).astype(BF16)
    h = h_ref[rows, :] + mod_ref[2:3, :] * jnp.dot(gated, wo_ref[...], preferred_element_type=F32)
    h_out[rows, :] = h
    f = _adaln(h, g2_ref[...], mod_ref[3:4, :], mod_ref[4:5, :])
    f_out[rows, :] = _pack_halves(f)
    fh = f.astype(BF16)
    fl = (f - fh.astype(F32)).astype(BF16)
    logits = (jnp.dot(fh, rw_ref[0], preferred_element_type=F32)
              + jnp.dot(fl, rw_ref[0], preferred_element_type=F32)
              + jnp.dot(fh, rw_ref[1], preferred_element_type=F32))
    lane = lax.broadcasted_iota(I32, (tm, LANES), 1)
    lane_f = lane.astype(F32)
    logits = jnp.where(lane < N_EXPERTS, logits, -jnp.inf)
    m1 = jnp.max(logits, axis=-1, keepdims=True)
    i1 = jnp.min(jnp.where(logits == m1, lane_f, float(LANES)), axis=-1, keepdims=True)
    rest = jnp.where(lane_f == i1, -jnp.inf, logits)
    m2 = jnp.max(rest, axis=-1, keepdims=True)
    i2 = jnp.min(jnp.where(rest == m2, lane_f, float(LANES)), axis=-1, keepdims=True)
    e2 = jnp.exp(m2 - m1)
    w1 = 1.0 / (1.0 + e2)
    w2 = e2 / (1.0 + e2)
    wt_out[rows, :] = jnp.where(lane == 0, w1, jnp.where(lane == 1, w2, 0.0))
    sel1 = lane_f == i1
    sel2 = lane_f == i2
    picked = jnp.where(sel1, 1.0, jnp.where(sel2, 1.0, 0.0))
    before = jnp.dot(tri_ref[...], picked.astype(BF16), preferred_element_type=F32) + run[0:1, :]
    rank1 = jnp.sum(jnp.where(sel1, before, 0.0), axis=-1, keepdims=True)
    rank2 = jnp.sum(jnp.where(sel2, before, 0.0), axis=-1, keepdims=True)
    run[...] = run[...] + jnp.sum(picked, axis=0, keepdims=True)
    routing = jnp.where(lane == 0, i1, jnp.where(lane == 1, i2,
                        jnp.where(lane == 2, rank1, jnp.where(lane == 3, rank2, 0.0))))
    route_out[:, rows] = routing.T[0:8, :]


def _gla_out(of_p, ob_p, of_s, ob_s, r, h, n_p, t_len, mod, out_norm, wo, g2, router_w):
    n, d = h.shape
    tm = _row_tile(n_p, t_len, cap=1024)
    na = n_p // tm
    nt_s = t_len // tm
    mod_idx = lambda i: (jnp.where(i < na, 0, 1 + (i - na) // nt_s), 0, 0)
    row = lambda i: (i, 0)
    ctx_row = lambda i: (jnp.minimum(i, na - 1), 0)
    lat_row = lambda i: (jnp.maximum(i - na, 0), 0)
    tp = tm // GLA_OUT_PARTS
    tri = jnp.tril(jnp.ones((tp, tp), F32), -1).astype(BF16)
    return pl.pallas_call(
        functools.partial(_gla_out_kernel, na),
        grid=(n // tm,),
        in_specs=[
            pl.BlockSpec((tm, d), ctx_row),
            pl.BlockSpec((tm, d), ctx_row),
            pl.BlockSpec((tm, d), lat_row),
            pl.BlockSpec((tm, d), lat_row),
            pl.BlockSpec((tm, d), row),
            pl.BlockSpec((tm, d), row),
            pl.BlockSpec((None, 8, d), mod_idx),
            _const_spec((1, GLA_DV)),
            _const_spec((d, d)),
            _const_spec((1, d)),
            _const_spec((2, d, LANES)),
            _const_spec((tp, tp)),
        ],
        out_specs=[pl.BlockSpec((tm, d), row), pl.BlockSpec((tm, d // 2), row),
                   pl.BlockSpec((8, tm), lambda i: (0, i)), pl.BlockSpec((tm, LANES), row),
                   _const_spec((8, LANES))],
        out_shape=[jax.ShapeDtypeStruct((n, d), F32), jax.ShapeDtypeStruct((n, d // 2), jnp.uint32),
                   jax.ShapeDtypeStruct((8, n), F32), jax.ShapeDtypeStruct((n, LANES), F32),
                   jax.ShapeDtypeStruct((8, LANES), F32)],
        scratch_shapes=[pltpu.VMEM((8, LANES), F32)],
        compiler_params=_params(("arbitrary",)),
        name="gla_out_router",
    )(of_p, ob_p, of_s, ob_s, r, h, mod, out_norm, wo, g2, router_w, tri)


SC_CORES = 2
SC_SUBCORES = 16
SC_CHUNK = 64


def _sc_gather_rows(table, idx):
    nw = SC_CORES * SC_SUBCORES
    b = idx.shape[0]
    d = table.shape[1]
    assert b % (nw * SC_CHUNK) == 0
    per_w = b // nw
    n_chunks = per_w // SC_CHUNK
    mesh = plsc.VectorSubcoreMesh(core_axis_name="c", subcore_axis_name="s",
                                  num_cores=SC_CORES, num_subcores=SC_SUBCORES)

    assert n_chunks % 2 == 0

    def body(table_hbm, idx_hbm, out_hbm, idx_v, rows_v, gsem, wsem):
        wid = lax.axis_index("s") * SC_CORES + lax.axis_index("c")
        base = wid * per_w
        pltpu.sync_copy(idx_hbm.at[wid], idx_v)

        def gather(j, slot):
            return pltpu.make_async_copy(table_hbm.at[idx_v.at[j]], rows_v.at[slot], gsem.at[slot])

        def write(j, slot):
            return pltpu.make_async_copy(rows_v.at[slot], out_hbm.at[pl.ds(base + j * SC_CHUNK, SC_CHUNK)],
                                         wsem.at[slot])

        gather(0, 0).start()

        @pl.loop(0, n_chunks, step=2)
        def _(j):
            for slot in range(2):
                jj = j + slot
                gather(jj, slot).wait()

                @pl.when(jj + 1 < n_chunks)
                def _():
                    @pl.when(jj >= 1)
                    def _():
                        write(jj - 1, 1 - slot).wait()

                    gather(jj + 1, 1 - slot).start()

                write(jj, slot).start()

        write(n_chunks - 2, 0).wait()
        write(n_chunks - 1, 1).wait()

    return pl.kernel(
        body,
        out_type=jax.ShapeDtypeStruct((b, d), table.dtype),
        mesh=mesh,
        scratch_types=[pltpu.VMEM((n_chunks, SC_CHUNK), I32),
                       pltpu.VMEM((2, SC_CHUNK, d), table.dtype),
                       pltpu.SemaphoreType.DMA((2,)),
                       pltpu.SemaphoreType.DMA((2,))],
        name="sc_gather_rows",
    )(table, idx.reshape(nw, n_chunks, SC_CHUNK))


def _sc_scatter_rows(rows, idx2, p):
    nw = SC_CORES * SC_SUBCORES
    n, d = rows.shape
    assert n % (nw * SC_CHUNK) == 0
    per_w = n // nw
    n_chunks = per_w // SC_CHUNK
    mesh = plsc.VectorSubcoreMesh(core_axis_name="c", subcore_axis_name="s",
                                  num_cores=SC_CORES, num_subcores=SC_SUBCORES)

    assert n_chunks % 2 == 0

    def body(rows_hbm, idx0_hbm, idx1_hbm, out_hbm, idx0_v, idx1_v, rows_v, rsem, ssem):
        wid = lax.axis_index("s") * SC_CORES + lax.axis_index("c")
        base = wid * per_w
        pltpu.sync_copy(idx0_hbm.at[wid], idx0_v)
        pltpu.sync_copy(idx1_hbm.at[wid], idx1_v)

        def read(j, slot):
            return pltpu.make_async_copy(rows_hbm.at[pl.ds(base + j * SC_CHUNK, SC_CHUNK)], rows_v.at[slot],
                                         rsem.at[slot])

        read(0, 0).start()

        @pl.loop(0, n_chunks, step=2)
        def _(j):
            for slot in range(2):
                jj = j + slot
                read(jj, slot).wait()

                @pl.when(jj + 1 < n_chunks)
                def _():
                    read(jj + 1, 1 - slot).start()

                s0 = pltpu.make_async_copy(rows_v.at[slot], out_hbm.at[idx0_v.at[jj]], ssem.at[0])
                s1 = pltpu.make_async_copy(rows_v.at[slot], out_hbm.at[idx1_v.at[jj]], ssem.at[1])
                s0.start()
                s1.start()
                s0.wait()
                s1.wait()

    return pl.kernel(
        body,
        out_type=jax.ShapeDtypeStruct((p, d), rows.dtype),
        mesh=mesh,
        scratch_types=[pltpu.VMEM((n_chunks, SC_CHUNK), I32),
                       pltpu.VMEM((n_chunks, SC_CHUNK), I32),
                       pltpu.VMEM((2, SC_CHUNK, d), rows.dtype),
                       pltpu.SemaphoreType.DMA((2,)),
                       pltpu.SemaphoreType.DMA((2,))],
        name="sc_scatter_rows",
    )(rows, idx2[0].reshape(nw, n_chunks, SC_CHUNK), idx2[1].reshape(nw, n_chunks, SC_CHUNK))


EXPERT_SUB = 256


def _pack_halves(x):
    k = x.shape[1] // 2
    lo = lax.bitcast_convert_type(x[:, :k].astype(BF16).astype(F32), jnp.uint32)
    hi = lax.bitcast_convert_type(x[:, k:].astype(BF16).astype(F32), jnp.uint32)
    return (lo >> 16) | (hi & jnp.uint32(0xFFFF0000))


def _unpack_halves(w):
    lo = lax.bitcast_convert_type(w << 16, F32).astype(BF16)
    hi = lax.bitcast_convert_type(w & jnp.uint32(0xFFFF0000), F32).astype(BF16)
    return lo, hi


def _expert_kernel(te_ref, ts_ref, nu_ref, nv_ref, x_ref, wg_ref, wu_ref, wd_ref, y_ref, acc, xb):
    i = pl.program_id(0)
    f = pl.program_id(1)
    nf = pl.num_programs(1)
    tm = x_ref.shape[0]
    half = x_ref.shape[1]
    sub = EXPERT_SUB
    nvalid = nv_ref[i]

    @pl.when(f == 0)
    def _():
        acc[...] = jnp.zeros_like(acc)

    @pl.when((f == 0) & (nvalid > 0))
    def _():
        rid = lax.broadcasted_iota(I32, (tm, half), 0)
        lo, hi = _unpack_halves(x_ref[...])
        zero = jnp.zeros_like(lo)
        xb[:, 0:half] = jnp.where(rid < nvalid, lo, zero)
        xb[:, half:2 * half] = jnp.where(rid < nvalid, hi, zero)

    def compute(rows):
        x = xb[rows, :]
        hg = jnp.dot(x, wg_ref[...].astype(BF16), preferred_element_type=F32)
        hu = jnp.dot(x, wu_ref[...].astype(BF16), preferred_element_type=F32)
        act = (_silu(hg) * hu).astype(BF16)
        acc[rows, :] += jnp.dot(act, wd_ref[...].astype(BF16), preferred_element_type=F32)

    for k in range(1, tm // sub + 1):
        @pl.when((nvalid > (k - 1) * sub) & (nvalid <= k * sub))
        def _():
            compute(slice(0, k * sub))

    @pl.when(f == nf - 1)
    def _():
        y_ref[...] = _pack_halves(acc[...])


def _experts(x, tile_expert, tile_src, n_used, n_valid, wg, wu, wd, tm, tf):
    p, half = x.shape
    d = 2 * half
    ne, _, dff = wg.shape
    nf = dff // tf
    assert dff % tf == 0 and tm % EXPERT_SUB == 0

    def fidx(i, f, nu):
        return jnp.where(i < nu[0], f, nf - 1)

    grid_spec = pltpu.PrefetchScalarGridSpec(
        num_scalar_prefetch=4,
        grid=(p // tm, nf),
        in_specs=[
            pl.BlockSpec((tm, half), lambda i, f, te, ts, nu, nv: (ts[i], 0)),
            pl.BlockSpec((None, d, tf), lambda i, f, te, ts, nu, nv: (te[i], 0, fidx(i, f, nu))),
            pl.BlockSpec((None, d, tf), lambda i, f, te, ts, nu, nv: (te[i], 0, fidx(i, f, nu))),
            pl.BlockSpec((None, tf, d), lambda i, f, te, ts, nu, nv: (te[i], fidx(i, f, nu), 0)),
        ],
        out_specs=pl.BlockSpec((tm, half), lambda i, f, te, ts, nu, nv: (i, 0)),
        scratch_shapes=[pltpu.VMEM((tm, d), F32), pltpu.VMEM((tm, d), BF16)],
    )
    return pl.pallas_call(
        _expert_kernel,
        grid_spec=grid_spec,
        out_shape=jax.ShapeDtypeStruct((p, half), jnp.uint32),
        compiler_params=_params(("arbitrary", "arbitrary"), 60 * 1024 * 1024),
        name="moe_experts",
    )(tile_expert, tile_src, n_used, n_valid, x, wg, wu, wd)


def _combine_kernel(wt_ref, h_ref, mod_ref, y0_ref, y1_ref, o_ref):
    wt = wt_ref[...]
    half = y0_ref.shape[1]
    lo0, hi0 = _unpack_halves(y0_ref[...])
    lo1, hi1 = _unpack_halves(y1_ref[...])
    w0, w1 = wt[:, 0:1], wt[:, 1:2]
    h = h_ref[...]
    gate = mod_ref[5:6, :]
    o_ref[:, 0:half] = h[:, 0:half] + gate[:, 0:half] * (w0 * lo0.astype(F32) + w1 * lo1.astype(F32))
    o_ref[:, half:] = h[:, half:] + gate[:, half:] * (w0 * hi0.astype(F32) + w1 * hi1.astype(F32))


def _combine(yg, wt, h, mod, row_off, n_rows, n_mod, mod_off, tm):
    d = h.shape[1]
    nt = n_rows // tm
    toff = row_off // tm
    per_mod = n_rows // n_mod // tm
    return pl.pallas_call(
        _combine_kernel,
        grid=(nt,),
        in_specs=[
            pl.BlockSpec((tm, LANES), lambda i: (toff + i, 0)),
            pl.BlockSpec((tm, d), lambda i: (toff + i, 0)),
            pl.BlockSpec((None, 8, d), lambda i: (mod_off + i // per_mod, 0, 0)),
            pl.BlockSpec((None, tm, d // 2), lambda i: (0, i, 0)),
            pl.BlockSpec((None, tm, d // 2), lambda i: (1, i, 0)),
        ],
        out_specs=pl.BlockSpec((tm, d), lambda i: (i, 0)),
        out_shape=jax.ShapeDtypeStruct((n_rows, d), F32),
        compiler_params=_params(("arbitrary",)),
        name="moe_combine",
    )(wt, h, mod, yg, yg)


def _route(route, counts, n, tm):
    cnt = counts[0, :N_EXPERTS].astype(I32)
    padded = ((cnt + tm - 1) // tm) * tm
    gend = jnp.cumsum(padded)
    goff = gend - padded
    e = route[0:2].astype(I32)
    rank = route[2:4].astype(I32)
    onehot = e[:, :, None] == jnp.arange(N_EXPERTS, dtype=I32)[None, None, :]
    dest = jnp.sum(jnp.where(onehot, goff[None, None, :], 0), axis=-1) + rank
    p = 2 * n + N_EXPERTS * tm
    n_used = gend[-1] // tm
    tiles = jnp.arange(p // tm, dtype=I32)
    tile_src = jnp.minimum(tiles, n_used - 1)
    tile_expert = jnp.minimum(jnp.sum((gend[None, :] <= (tile_src * tm)[:, None]).astype(I32), axis=1),
                              N_EXPERTS - 1)
    used = goff[tile_expert] + cnt[tile_expert]
    n_valid = jnp.where(tiles < n_used, jnp.clip(used - tiles * tm, 0, tm), 0).astype(I32)
    return dest, p, tile_expert, tile_src, n_used.reshape(1).astype(I32), n_valid


def kernel(x_prompt, x_sample, cache_k, cache_v, state_fwd, state_bwd, c, c_ctx, ada_w, ada_b, norm1_g, norm2_g, attn_w_qkv, attn_q_norm, attn_k_norm, attn_sink, attn_w_o, gla_w_in, gla_gate_w1, gla_gate_w2, gla_gate_b, gla_out_norm, gla_w_o, ffn_w_gate, ffn_w_up, ffn_w_down, moe_router, moe_w_gate, moe_w_up, moe_w_down):
    bp, seq, d = x_prompt.shape
    db, t_len, _ = x_sample.shape
    n_p, n_s = bp * seq, db * t_len
    n = n_p + n_s
    xp = x_prompt.reshape(n_p, d)
    xs = x_sample.reshape(n_s, d)

    cond = jnp.concatenate([c_ctx[None, :], c], axis=0)
    assert cond.shape[0] <= 8
    cond_t = jnp.pad(cond, ((0, 8 - cond.shape[0]), (0, 0))).T
    mods = _modulation(cond_t, cond.shape[0], ada_w, ada_b)

    nk = N_KV_HEADS * HEAD_DIM
    qn = jnp.tile(attn_q_norm[0], N_HEADS)[None, :]
    kn = jnp.tile(attn_k_norm[0], N_KV_HEADS)[None, :]
    q, kt, vv, kv = _qkv(xp, xs, t_len, mods[0], norm1_g[0][None, :], attn_w_qkv[0].astype(BF16), qn, kn)
    wo0 = attn_w_o[0].astype(BF16)
    sink = attn_sink[0]
    hp = _ctx_attention(sink, q, kt, vv, xp, seq, mods[0], wo0)
    ck = cache_k[:, 0].astype(BF16)
    cv = cache_v[:, 0].astype(BF16)
    ckt = jnp.concatenate([ck, ck], axis=-1).transpose(0, 2, 3, 1)
    cvv = jnp.concatenate([cv, cv], axis=-1).reshape(db, cv.shape[1], N_KV_HEADS * LANES)
    hs = _lat_attention(sink, q, kt, vv, ckt, cvv, xs, n_p, t_len, mods[0], wo0)
    new_k = kv[:n_p, :nk].reshape(bp, 1, seq, N_KV_HEADS, HEAD_DIM)
    new_v = kv[:n_p, nk:].reshape(bp, 1, seq, N_KV_HEADS, HEAD_DIM)
    h = _ffn(hp, hs, t_len, mods[0], norm2_g[0][None, :], ffn_w_gate[0].astype(BF16),
             ffn_w_up[0].astype(BF16), ffn_w_down[0].astype(BF16))

    hk = GLA_HEADS * GLA_DK
    rank = GLA_GATE_RANK
    w1 = jnp.concatenate([gla_gate_w1[0, 0], gla_gate_w1[0, 1]], axis=1)
    w1 = jnp.pad(w1, ((0, 0), (0, LANES - 2 * rank))).astype(BF16)
    w2 = jnp.zeros((LANES, 2 * hk), F32)
    w2 = w2.at[0:rank, 0:hk].set(gla_gate_w2[0, 0]).at[rank:2 * rank, hk:].set(gla_gate_w2[0, 1]).astype(BF16)
    gate_b = gla_gate_b[0].reshape(1, 2 * hk)
    qk, v, r, bc = _gla_in(h, n_p, t_len, mods[1], norm1_g[1][None, :], gla_w_in[0].astype(BF16),
                           w1, w2, gate_b)
    zeros = jnp.zeros((bp, GLA_HEADS, GLA_DK, GLA_DV), F32)
    of_p, ob_p, new_sf, new_sb = _gla_scan(qk, v, bc, zeros, zeros, 0, bp, seq)
    of_s, ob_s, _, _ = _gla_scan(qk, v, bc, state_fwd[:, 0], state_bwd[:, 0], n_p // t_len, db, t_len)
    rw = jnp.pad(moe_router[0], ((0, 0), (0, LANES - N_EXPERTS)))
    rw_hi = rw.astype(BF16)
    rw_lo = (rw - rw_hi.astype(F32)).astype(BF16)
    h, f, route, wt, counts = _gla_out(
        of_p.reshape(n_p, d), ob_p.reshape(n_p, d), of_s.reshape(n_s, d), ob_s.reshape(n_s, d), r, h,
        n_p, t_len, mods[1], gla_out_norm[0][None, :], gla_w_o[0].astype(BF16), norm2_g[1][None, :],
        jnp.stack([rw_hi, rw_lo]))

    tm_e = 1024
    dest_k, p_rows, tile_expert, tile_src, n_used, n_valid = _route(route, counts, n, tm_e)
    xg = _sc_scatter_rows(f, dest_k, p_rows)
    y = _experts(xg, tile_expert, tile_src, n_used, n_valid, moe_w_gate[0], moe_w_up[0],
                 moe_w_down[0], tm_e, 512)
    yg_p = _sc_gather_rows(y, dest_k[:, :n_p].reshape(2 * n_p)).reshape(2, n_p, d // 2)
    yg_s = _sc_gather_rows(y, dest_k[:, n_p:].reshape(2 * n_s)).reshape(2, n_s, d // 2)
    tm_c = _row_tile(n_p, t_len)
    y_p = _combine(yg_p, wt, h, mods[1], 0, n_p, 1, 0, tm_c)
    y_s = _combine(yg_s, wt, h, mods[1], n_p, n_s, db, 1, tm_c)

    return (y_p.reshape(bp, seq, d), y_s.reshape(db, t_len, d), new_k, new_v,
            new_sf[:, None], new_sb[:, None])
```

```python
import functools
import math

import numpy as np
import jax
import jax.numpy as jnp
from jax import lax
from jax.experimental import pallas as pl
from jax.experimental.pallas import tpu as pltpu
from jax.experimental.pallas import tpu_sc as plsc

F32 = jnp.float32
BF16 = jnp.bfloat16
I32 = jnp.int32

D_MODEL = 1024
N_HEADS = 16
N_KV_HEADS = 4
HEAD_DIM = 64
GRID_W = 64
WINDOW = 128
ATTN_BLOCK = 128
ROPE_THETA = 10000.0
GLA_HEADS = 4
GLA_DK = 128
GLA_DV = 256
GLA_GATE_RANK = 16
GLA_GATE_TAU = 16.0
GLA_CHUNK = 64
N_EXPERTS = 8
NORM_EPS = 1e-6
NEG_INF = -1e30

LANES = 128
VMEM_LIMIT = 56 * 1024 * 1024


def _params(sem, vmem=VMEM_LIMIT):
    return pltpu.CompilerParams(dimension_semantics=sem, vmem_limit_bytes=vmem)


def _row_tile(*counts, cap=512):
    t = cap
    while any(c % t for c in counts):
        t //= 2
    assert t >= 8
    return t


def _rms(x):
    return x * lax.rsqrt(jnp.mean(x * x, axis=-1, keepdims=True) + NORM_EPS)


def _adaln(x, g, shift, scale):
    return _rms(x) * (g * (1.0 + scale)) + shift


def _silu(x):
    return x * jax.nn.sigmoid(x)


def _const_spec(shape):
    nd = len(shape)
    return pl.BlockSpec(shape, lambda *_: (0,) * nd)


def _mod_kernel(n_cond, ct_ref, w_ref, b_ref, o_ref):
    ct = ct_ref[...]
    s = _silu(ct)
    w = w_ref[...]
    rows = [jnp.sum(w * s[:, r:r + 1], axis=0, keepdims=True) for r in range(n_cond)]
    rows += [jnp.zeros_like(rows[0])] * (8 - n_cond)
    o_ref[...] = jnp.concatenate(rows, axis=0) + b_ref[...]


def _modulation(cond_t, n_cond, ada_w, ada_b):
    depth, d, n6 = ada_w.shape
    tn = 1024
    out = pl.pallas_call(
        functools.partial(_mod_kernel, n_cond),
        grid=(depth, n6 // tn),
        in_specs=[
            pl.BlockSpec((d, 8), lambda l, j: (0, 0)),
            pl.BlockSpec((None, d, tn), lambda l, j: (l, 0, j)),
            pl.BlockSpec((None, 1, tn), lambda l, j: (l, 0, j)),
        ],
        out_specs=pl.BlockSpec((None, 8, tn), lambda l, j: (l, 0, j)),
        out_shape=jax.ShapeDtypeStruct((depth, 8, n6), F32),
        compiler_params=_params(("arbitrary", "arbitrary")),
        name="modulation",
    )(cond_t, ada_w, ada_b.reshape(depth, 1, n6))
    m = out[:, :n_cond].reshape(depth, n_cond, 6, d)
    return jnp.pad(m, ((0, 0), (0, 0), (0, 2), (0, 0)))


LOG2E = math.log2(math.e)


def _dup_half(k2, half):
    lane = lax.broadcasted_iota(I32, k2.shape, 1)
    lo = lane < HEAD_DIM
    r = pltpu.roll(k2, HEAD_DIM, 1)
    return jnp.where(lo, k2, r) if half == 0 else jnp.where(lo, r, k2)


QKV_PARTS = 2


def _qkv_kernel(na, xa_ref, xb_ref, mod_ref, g_ref, w_ref, qn_ref, kn_ref, bd_ref,
                cos_ref, sin_ref, q_out, kt_out, vv_out, ck_out, cv_out):
    i = pl.program_id(0)
    is_ctx = i < na
    tm = xa_ref.shape[0] // QKV_PARTS
    for part in range(QKV_PARTS):
        _qkv_rows(slice(part * tm, (part + 1) * tm), tm, is_ctx, xa_ref, xb_ref, mod_ref, g_ref, w_ref,
                  qn_ref, kn_ref, bd_ref, cos_ref, sin_ref, q_out, kt_out, vv_out, ck_out, cv_out)


def _qkv_rows(rows, tm, is_ctx, xa_ref, xb_ref, mod_ref, g_ref, w_ref, qn_ref, kn_ref, bd_ref,
              cos_ref, sin_ref, q_out, kt_out, vv_out, ck_out, cv_out):
    x = jnp.where(is_ctx, xa_ref[rows, :], xb_ref[rows, :])
    a = _adaln(x, g_ref[...], mod_ref[0:1, :], mod_ref[1:2, :])
    y = jnp.dot(a.astype(BF16), w_ref[...], preferred_element_type=F32)
    cos = jnp.where(is_ctx, 1.0, cos_ref[rows, :])
    sin = jnp.where(is_ctx, 0.0, sin_ref[rows, :])
    lane = lax.broadcasted_iota(I32, (tm, LANES), 1)
    first16 = (lane % 32) < 16

    def norm_rope(z, wt):
        ss = jnp.dot((z * z).astype(BF16), bd_ref[...], preferred_element_type=F32)
        zn = z * lax.rsqrt(ss * (1.0 / HEAD_DIM) + NORM_EPS) * wt
        outs = []
        for c in range(2):
            t = zn[:, c * LANES:(c + 1) * LANES]
            partner = jnp.where(first16, pltpu.roll(t, LANES - 16, 1), pltpu.roll(t, 16, 1))
            outs.append(t * cos + partner * sin)
        return jnp.concatenate(outs, axis=1)

    nq = N_HEADS * HEAD_DIM
    nk = N_KV_HEADS * HEAD_DIM
    scale = HEAD_DIM ** -0.5 * LOG2E
    for s in range(nq // 256):
        sl = slice(s * 256, (s + 1) * 256)
        q_out[rows, sl] = (norm_rope(y[:, sl], qn_ref[:, sl]) * scale).astype(BF16)
    k = norm_rope(y[:, nq:nq + nk], kn_ref[...])
    v = y[:, nq + nk:nq + 2 * nk]
    for g in range(N_KV_HEADS):
        c = g // 2
        kk = _dup_half(k[:, c * LANES:(c + 1) * LANES], g % 2)
        kt_out[g, :, rows] = kk.T.astype(BF16)
        vv_out[rows, g * LANES:(g + 1) * LANES] = _dup_half(v[:, c * LANES:(c + 1) * LANES], g % 2).astype(BF16)

    @pl.when(is_ctx)
    def _():
        ck_out[rows, :] = k
        cv_out[rows, :] = v


def _rope_tables(t_len):
    pos = np.arange(t_len)
    row = (pos // GRID_W).astype(np.float32)[:, None]
    col = (pos % GRID_W).astype(np.float32)[:, None]
    half = HEAD_DIM // 2
    inv = (np.float32(ROPE_THETA) ** (-np.arange(0, half, 2, dtype=np.float32) / np.float32(half)))[None, :]
    ar, ac = row * inv, col * inv
    cos = np.concatenate([np.cos(ar), np.cos(ar), np.cos(ac), np.cos(ac)], axis=1)
    sin = np.concatenate([-np.sin(ar), np.sin(ar), -np.sin(ac), np.sin(ac)], axis=1)
    return (jnp.asarray(np.tile(cos, (1, 2)), dtype=F32), jnp.asarray(np.tile(sin, (1, 2)), dtype=F32))


def _qkv(xp, xs, t_len, mod, g, w, qn, kn):
    n_p, n_s = xp.shape[0], xs.shape[0]
    tm = _row_tile(n_p, t_len, cap=1024)
    na, nb = n_p // tm, n_s // tm
    nt_s = t_len // tm
    n = n_p + n_s
    d = D_MODEL
    nq, nk = N_HEADS * HEAD_DIM, N_KV_HEADS * HEAD_DIM
    cos, sin = _rope_tables(t_len)
    eye4 = jnp.kron(jnp.eye(4, dtype=F32), jnp.ones((HEAD_DIM, HEAD_DIM), F32)).astype(BF16)
    mod_idx = lambda i: (jnp.where(i < na, 0, 1 + (i - na) // nt_s), 0, 0)
    pos_idx = lambda i: (jnp.maximum(i - na, 0) % nt_s, 0)
    return pl.pallas_call(
        functools.partial(_qkv_kernel, na),
        grid=(na + nb,),
        in_specs=[
            pl.BlockSpec((tm, d), lambda i: (jnp.minimum(i, na - 1), 0)),
            pl.BlockSpec((tm, d), lambda i: (jnp.maximum(i - na, 0), 0)),
            pl.BlockSpec((None, 8, d), mod_idx),
            _const_spec((1, d)),
            _const_spec((d, nq + 2 * nk)),
            _const_spec((1, nq)),
            _const_spec((1, nk)),
            _const_spec((256, 256)),
            pl.BlockSpec((tm, LANES), pos_idx),
            pl.BlockSpec((tm, LANES), pos_idx),
        ],
        out_specs=[pl.BlockSpec((tm, nq), lambda i: (i, 0)),
                   pl.BlockSpec((N_KV_HEADS, LANES, tm), lambda i: (0, 0, i)),
                   pl.BlockSpec((tm, N_KV_HEADS * LANES), lambda i: (i, 0)),
                   pl.BlockSpec((tm, nk), lambda i: (jnp.minimum(i, na - 1), 0)),
                   pl.BlockSpec((tm, nk), lambda i: (jnp.minimum(i, na - 1), 0))],
        out_shape=[jax.ShapeDtypeStruct((n, nq), BF16),
                   jax.ShapeDtypeStruct((N_KV_HEADS, LANES, n), BF16),
                   jax.ShapeDtypeStruct((n, N_KV_HEADS * LANES), BF16),
                   jax.ShapeDtypeStruct((n_p, nk), F32),
                   jax.ShapeDtypeStruct((n_p, nk), F32)],
        compiler_params=_params(("arbitrary",)),
        name="qkv",
    )(xp, xs, mod, g, w, qn, kn, eye4, cos, sin)


def _attn_group(q2a, q2b, kt, vv, bias, n_bias, sinks):
    tq = q2a.shape[0]
    lane = lax.broadcasted_iota(I32, (tq, LANES), 1)
    lo = lane < HEAD_DIM
    zero = jnp.zeros_like(q2a)
    qs = jnp.concatenate([jnp.where(lo, q2a, zero), jnp.where(lo, zero, q2a),
                          jnp.where(lo, q2b, zero), jnp.where(lo, zero, q2b)], axis=0)
    s = jnp.dot(qs, kt, preferred_element_type=F32)
    ps, dens = [], []
    for h in range(4):
        sh = s[h * tq:(h + 1) * tq]
        if bias is not None:
            sh = jnp.concatenate([sh[:, :n_bias] + bias, sh[:, n_bias:]], axis=1)
        m = jnp.maximum(jnp.max(sh, axis=-1, keepdims=True), sinks[h])
        p = jnp.exp2(sh - m)
        dens.append(jnp.sum(p, axis=-1, keepdims=True) + jnp.exp2(sinks[h] - m))
        ps.append(p.astype(BF16))
    o = (jnp.dot(jnp.concatenate(ps, axis=0), vv, preferred_element_type=F32)
         / jnp.concatenate(dens, axis=0))
    oa = jnp.where(lo, o[0:tq], o[tq:2 * tq])
    ob = jnp.where(lo, o[2 * tq:3 * tq], o[3 * tq:4 * tq])
    return oa, ob


def _ctx_attn_kernel(sink_ref, q_ref, kt_ref, vv_ref, x_ref, mod_ref, wo_ref, o_ref, osc):
    for g in range(N_KV_HEADS):
        q2a = q_ref[:, (2 * g) * LANES:(2 * g + 1) * LANES]
        q2b = q_ref[:, (2 * g + 1) * LANES:(2 * g + 2) * LANES]
        sinks = [sink_ref[4 * g + j] * LOG2E for j in range(4)]
        oa, ob = _attn_group(q2a, q2b, kt_ref[g], vv_ref[:, g * LANES:(g + 1) * LANES], None, 0, sinks)
        osc[:, (2 * g) * LANES:(2 * g + 1) * LANES] = oa.astype(BF16)
        osc[:, (2 * g + 1) * LANES:(2 * g + 2) * LANES] = ob.astype(BF16)
    att = jnp.dot(osc[...], wo_ref[...], preferred_element_type=F32)
    o_ref[...] = x_ref[...] + mod_ref[2:3, :] * att


def _ctx_attention(sink, q, kt, vv, xp, seq, mod, wo):
    n_p, d = xp.shape
    nb = n_p // seq
    return pl.pallas_call(
        _ctx_attn_kernel,
        grid=(nb,),
        in_specs=[
            pl.BlockSpec(memory_space=pltpu.SMEM),
            pl.BlockSpec((seq, d), lambda b: (b, 0)),
            pl.BlockSpec((N_KV_HEADS, LANES, seq), lambda b: (0, 0, b)),
            pl.BlockSpec((seq, N_KV_HEADS * LANES), lambda b: (b, 0)),
            pl.BlockSpec((seq, d), lambda b: (b, 0)),
            pl.BlockSpec((None, 8, d), lambda b: (0, 0, 0)),
            _const_spec((d, d)),
        ],
        out_specs=pl.BlockSpec((seq, d), lambda b: (b, 0)),
        out_shape=jax.ShapeDtypeStruct((n_p, d), F32),
        scratch_shapes=[pltpu.VMEM((seq, d), BF16)],
        compiler_params=_params(("arbitrary",)),
        name="ctx_attention",
    )(sink, q, kt, vv, xp, mod, wo)


LAT_QB = 4


def _lat_attn_kernel(t_len, sink_ref, q_ref, ktp_ref, kto_ref, ktn_ref, vvp_ref, vvo_ref, vvn_ref,
                     ckt_ref, cvv_ref, x_ref, mod_ref, wo_ref, o_ref, osc):
    step = pl.program_id(1)
    tq = ATTN_BLOCK
    nloc = 3 * ATTN_BLOCK
    qi = lax.broadcasted_iota(I32, (tq, nloc), 0)
    kj = lax.broadcasted_iota(I32, (tq, nloc), 1)
    in_window = jnp.abs(qi + tq - kj) <= WINDOW
    for u in range(LAT_QB):
        n = step * LAT_QB + u
        kpos = (n - 1) * tq + kj
        bias = jnp.where(in_window & (kpos >= 0) & (kpos < t_len), 0.0, NEG_INF)
        rows = slice(u * tq, (u + 1) * tq)
        for g in range(N_KV_HEADS):
            vs = slice(g * LANES, (g + 1) * LANES)
            kts = [ktp_ref[g]] + [kto_ref[g, :, j * tq:(j + 1) * tq] for j in range(LAT_QB)] + [ktn_ref[g]]
            vvs = [vvp_ref[:, vs]] + [vvo_ref[j * tq:(j + 1) * tq, vs] for j in range(LAT_QB)] + [vvn_ref[:, vs]]
            kt = jnp.concatenate(kts[u:u + 3] + [ckt_ref[g]], axis=1)
            vv = jnp.concatenate(vvs[u:u + 3] + [cvv_ref[:, vs]], axis=0)
            q2a = q_ref[rows, (2 * g) * LANES:(2 * g + 1) * LANES]
            q2b = q_ref[rows, (2 * g + 1) * LANES:(2 * g + 2) * LANES]
            sinks = [sink_ref[4 * g + j] * LOG2E for j in range(4)]
            oa, ob = _attn_group(q2a, q2b, kt, vv, bias, nloc, sinks)
            osc[rows, (2 * g) * LANES:(2 * g + 1) * LANES] = oa.astype(BF16)
            osc[rows, (2 * g + 1) * LANES:(2 * g + 2) * LANES] = ob.astype(BF16)
    att = jnp.dot(osc[...], wo_ref[...], preferred_element_type=F32)
    o_ref[...] = x_ref[...] + mod_ref[2:3, :] * att


def _lat_attention(sink, q, kt, vv, ckt, cvv, xs, n_p, t_len, mod, wo):
    n_s, d = xs.shape
    db = n_s // t_len
    tq = ATTN_BLOCK
    ts = LAT_QB * tq
    nblk = t_len // tq
    nstep = t_len // ts
    assert n_p % ts == 0 and t_len % ts == 0
    off = n_p // tq
    npast = ckt.shape[3]
    g4 = N_KV_HEADS

    def nbr(delta):
        return lambda b, m: off + b * nblk + jnp.clip(m * LAT_QB + delta, 0, nblk - 1)

    own = lambda b, m: n_p // ts + b * nstep + m
    return pl.pallas_call(
        functools.partial(_lat_attn_kernel, t_len),
        grid=(db, nstep),
        in_specs=[
            pl.BlockSpec(memory_space=pltpu.SMEM),
            pl.BlockSpec((ts, d), lambda b, m: (own(b, m), 0)),
            pl.BlockSpec((g4, LANES, tq), lambda b, m: (0, 0, nbr(-1)(b, m))),
            pl.BlockSpec((g4, LANES, ts), lambda b, m: (0, 0, own(b, m))),
            pl.BlockSpec((g4, LANES, tq), lambda b, m: (0, 0, nbr(LAT_QB)(b, m))),
            pl.BlockSpec((tq, g4 * LANES), lambda b, m: (nbr(-1)(b, m), 0)),
            pl.BlockSpec((ts, g4 * LANES), lambda b, m: (own(b, m), 0)),
            pl.BlockSpec((tq, g4 * LANES), lambda b, m: (nbr(LAT_QB)(b, m), 0)),
            pl.BlockSpec((None, g4, LANES, npast), lambda b, m: (b, 0, 0, 0)),
            pl.BlockSpec((None, npast, g4 * LANES), lambda b, m: (b, 0, 0)),
            pl.BlockSpec((ts, d), lambda b, m: (b * nstep + m, 0)),
            pl.BlockSpec((None, 8, d), lambda b, m: (1 + b, 0, 0)),
            _const_spec((d, d)),
        ],
        out_specs=pl.BlockSpec((ts, d), lambda b, m: (b * nstep + m, 0)),
        out_shape=jax.ShapeDtypeStruct((n_s, d), F32),
        scratch_shapes=[pltpu.VMEM((ts, d), BF16)],
        compiler_params=_params(("arbitrary", "arbitrary")),
        name="lat_attention",
    )(sink, q, kt, kt, kt, vv, vv, vv, ckt, cvv, xs, mod, wo)


def _ffn_kernel(na, fc, xa_ref, xb_ref, mod_ref, g_ref, wg_ref, wu_ref, wd_ref, o_ref, acc):
    i = pl.program_id(0)
    x = jnp.where(i < na, xa_ref[...], xb_ref[...])
    a = _adaln(x, g_ref[...], mod_ref[3:4, :], mod_ref[4:5, :]).astype(BF16)
    nf = wg_ref.shape[1] // fc
    for f in range(nf):
        sl = slice(f * fc, (f + 1) * fc)
        hg = jnp.dot(a, wg_ref[:, sl], preferred_element_type=F32)
        hu = jnp.dot(a, wu_ref[:, sl], preferred_element_type=F32)
        act = (_silu(hg) * hu).astype(BF16)
        part = jnp.dot(act, wd_ref[sl, :], preferred_element_type=F32)
        if f == 0:
            acc[...] = part
        else:
            acc[...] += part
    o_ref[...] = x + mod_ref[5:6, :] * acc[...]


def _ffn(hp, hs, t_len, mod, g, wg, wu, wd):
    n_p, n_s = hp.shape[0], hs.shape[0]
    d, dff = wg.shape
    tm = _row_tile(n_p, t_len)
    na, nb = n_p // tm, n_s // tm
    nt_s = t_len // tm
    fc = 256
    assert dff % fc == 0
    mod_idx = lambda i: (jnp.where(i < na, 0, 1 + (i - na) // nt_s), 0, 0)
    return pl.pallas_call(
        functools.partial(_ffn_kernel, na, fc),
        grid=(na + nb,),
        in_specs=[
            pl.BlockSpec((tm, d), lambda i: (jnp.minimum(i, na - 1), 0)),
            pl.BlockSpec((tm, d), lambda i: (jnp.maximum(i - na, 0), 0)),
            pl.BlockSpec((None, 8, d), mod_idx),
            _const_spec((1, d)),
            _const_spec((d, dff)),
            _const_spec((d, dff)),
            _const_spec((dff, d)),
        ],
        out_specs=pl.BlockSpec((tm, d), lambda i: (i, 0)),
        out_shape=jax.ShapeDtypeStruct((n_p + n_s, d), F32),
        scratch_shapes=[pltpu.VMEM((tm, d), F32)],
        compiler_params=_params(("arbitrary",)),
        name="ffn",
    )(hp, hs, mod, g, wg, wu, wd)


def _split2(x):
    hi = x.astype(BF16)
    lo = (x - hi.astype(F32)).astype(BF16)
    return hi, lo


def _gla_in_kernel(nt_s, x_ref, mod_ref, g_ref, w_ref, w1_ref, w2_ref, gb_ref, tri_ref,
                   qk_out, v_out, r_out, b_out):
    x = x_ref[...]
    tm = x.shape[0]
    a = _adaln(x, g_ref[...], mod_ref[0:1, :], mod_ref[1:2, :]).astype(BF16)
    hk = GLA_HEADS * GLA_DK
    hv = GLA_HEADS * GLA_DV
    q = jnp.dot(a, w_ref[:, 0:hk], preferred_element_type=F32)
    qk_out[:, 0:hk] = (q * (GLA_DK ** -0.5)).astype(BF16)
    qk_out[:, hk:2 * hk] = jnp.dot(a, w_ref[:, hk:2 * hk], preferred_element_type=F32).astype(BF16)
    v_out[...] = jnp.dot(a, w_ref[:, 2 * hk:2 * hk + hv], preferred_element_type=F32).astype(BF16)
    r_out[...] = jnp.dot(a, w_ref[:, 2 * hk + hv:2 * hk + 2 * hv], preferred_element_type=F32).astype(BF16)
    z1 = jnp.dot(a, w1_ref[...], preferred_element_type=F32)
    z = jnp.dot(z1.astype(BF16), w2_ref[...], preferred_element_type=F32) + gb_ref[...]
    gate = (jnp.minimum(z, 0.0) - jnp.log(1.0 + jnp.exp(-jnp.abs(z)))) * (1.0 / GLA_GATE_TAU)
    c = GLA_CHUNK
    tri = tri_ref[...]
    for j in range(tm // c):
        rows = slice(j * c, (j + 1) * c)
        for dr in range(2):
            cols = slice(dr * hk, (dr + 1) * hk)
            hi, lo = _split2(gate[rows, cols])
            b_out[rows, cols] = (jnp.dot(tri[dr], hi, preferred_element_type=F32)
                                 + jnp.dot(tri[dr], lo, preferred_element_type=F32))


def _gla_in(h, n_p, t_len, mod, g, w_in, w1, w2, gate_b):
    n, d = h.shape
    tm = _row_tile(n_p, t_len, cap=1024)
    na = n_p // tm
    nt_s = t_len // tm
    hk, hv = GLA_HEADS * GLA_DK, GLA_HEADS * GLA_DV
    c = GLA_CHUNK
    lower = jnp.tril(jnp.ones((c, c), F32))
    tri = jnp.stack([lower, lower.T]).astype(BF16)
    mod_idx = lambda i: (jnp.where(i < na, 0, 1 + (i - na) // nt_s), 0, 0)
    row = lambda i: (i, 0)
    return pl.pallas_call(
        functools.partial(_gla_in_kernel, nt_s),
        grid=(n // tm,),
        in_specs=[
            pl.BlockSpec((tm, d), row),
            pl.BlockSpec((None, 8, d), mod_idx),
            _const_spec((1, d)),
            _const_spec(w_in.shape),
            _const_spec(w1.shape),
            _const_spec(w2.shape),
            _const_spec((1, 2 * hk)),
            _const_spec((2, c, c)),
        ],
        out_specs=[pl.BlockSpec((tm, 2 * hk), row), pl.BlockSpec((tm, hv), row),
                   pl.BlockSpec((tm, hv), row), pl.BlockSpec((tm, 2 * hk), row)],
        out_shape=[jax.ShapeDtypeStruct((n, 2 * hk), BF16), jax.ShapeDtypeStruct((n, hv), BF16),
                   jax.ShapeDtypeStruct((n, hv), BF16), jax.ShapeDtypeStruct((n, 2 * hk), F32)],
        compiler_params=_params(("arbitrary",)),
        name="gla_in",
    )(h, mod, g, w_in, w1, w2, gate_b, tri)


def _gla_direction(qk_ref, v_ref, b_ref, st, o_ref, forward, qd_s, att_s, upd_s, dec_s):
    c = GLA_CHUNK
    ncb = qk_ref.shape[0] // c
    hk = GLA_HEADS * GLA_DK
    dk, dv = GLA_DK, GLA_DV
    ri = lax.broadcasted_iota(I32, (c, c), 0)
    ci_ = lax.broadcasted_iota(I32, (c, c), 1)
    keep = (ci_ <= ri) if forward else (ci_ >= ri)

    for ch in range(ncb):
        rows = slice(ch * c, (ch + 1) * c)
        b = b_ref[rows, :]
        if forward:
            b_end, b_mid = b[c - 1:c, :], b[c // 2 - 1:c // 2, :]
        else:
            b_end, b_mid = b[0:1, :], b[c // 2:c // 2 + 1, :]
        qa = qk_ref[rows, 0:hk] * jnp.exp(b - b_mid)
        ka = qk_ref[rows, hk:2 * hk] * jnp.exp(b_mid - b)
        qd_s[rows, :] = (qa * jnp.exp(b_mid)).astype(BF16)
        kd = (ka * jnp.exp(b_end - b_mid)).astype(BF16)
        dec_s[ch:ch + 1, :] = jnp.exp(b_end)
        qab, kab = qa.astype(BF16), ka.astype(BF16)
        for h in range(GLA_HEADS):
            ks = slice(h * dk, (h + 1) * dk)
            vs = slice(h * dv, (h + 1) * dv)
            att = lax.dot_general(qab[:, ks], kab[:, ks], (((1,), (1,)), ((), ())),
                                  preferred_element_type=F32)
            att_s[ch, h] = jnp.where(keep, att, 0.0).astype(BF16)
            upd_s[ch, h] = lax.dot_general(v_ref[rows, vs], kd[:, ks], (((0,), (0,)), ((), ())),
                                           preferred_element_type=F32)

    for ch in (range(ncb) if forward else reversed(range(ncb))):
        rows = slice(ch * c, (ch + 1) * c)
        for h in range(GLA_HEADS):
            ks = slice(h * dk, (h + 1) * dk)
            vs = slice(h * dv, (h + 1) * dv)
            s = st[h]
            o = lax.dot_general(qd_s[rows, ks], s.astype(BF16), (((1,), (1,)), ((), ())),
                                preferred_element_type=F32)
            o = o + jnp.dot(att_s[ch, h], v_ref[rows, vs], preferred_element_type=F32)
            o_ref[rows, vs] = o.astype(o_ref.dtype)
            st[h] = s * dec_s[ch:ch + 1, ks] + upd_s[ch, h]


def _gla_scan_kernel(shared, *refs):
    if shared:
        qkf_ref, vf_ref, bf_ref, bb_ref = refs[:4]
        qkb_ref, vb_ref = qkf_ref, vf_ref
        rest = refs[4:]
    else:
        qkf_ref, vf_ref, bf_ref, qkb_ref, vb_ref, bb_ref = refs[:6]
        rest = refs[6:]
    sf0_ref, sb0_ref, of_ref, ob_ref, sf_ref, sb_ref, stf, stb, qd_s, att_s, upd_s, dec_s = rest
    i = pl.program_id(1)
    nblk = pl.num_programs(1)

    @pl.when(i == 0)
    def _():
        for h in range(GLA_HEADS):
            stf[h] = sf0_ref[h].T
            stb[h] = sb0_ref[h].T

    _gla_direction(qkf_ref, vf_ref, bf_ref, stf, of_ref, True, qd_s, att_s, upd_s, dec_s)
    _gla_direction(qkb_ref, vb_ref, bb_ref, stb, ob_ref, False, qd_s, att_s, upd_s, dec_s)

    @pl.when(i == nblk - 1)
    def _():
        for h in range(GLA_HEADS):
            sf_ref[h] = stf[h].T
            sb_ref[h] = stb[h].T


def _gla_scan(qk, v, bc, sf0, sb0, b_off, nb, t_len):
    n = qk.shape[0]
    assert n % t_len == 0
    ns = n // t_len
    tb = _row_tile(t_len)
    nblk = t_len // tb
    qk3 = qk.reshape(ns, t_len, qk.shape[1])
    v3 = v.reshape(ns, t_len, v.shape[1])
    bc3 = bc.reshape(ns, t_len, bc.shape[1])
    h, dk, dv = GLA_HEADS, GLA_DK, GLA_DV
    hk = h * dk
    st_spec = pl.BlockSpec((None, h, dk, dv), lambda b, i: (b, 0, 0, 0))
    fwd = lambda b, i: (b_off + b, i, 0)
    bwd = lambda b, i: (b_off + b, nblk - 1 - i, 0)
    bwd_b = pl.BlockSpec((None, tb, hk), lambda b, i: (b_off + b, nblk - 1 - i, 1))
    shared = nblk == 1
    in_specs = [pl.BlockSpec((None, tb, 2 * hk), fwd), pl.BlockSpec((None, tb, h * dv), fwd),
                pl.BlockSpec((None, tb, hk), fwd)]
    operands = [qk3, v3, bc3]
    if not shared:
        in_specs += [pl.BlockSpec((None, tb, 2 * hk), bwd), pl.BlockSpec((None, tb, h * dv), bwd)]
        operands += [qk3, v3]
    in_specs += [bwd_b, st_spec, st_spec]
    operands += [bc3, sf0, sb0]
    return pl.pallas_call(
        functools.partial(_gla_scan_kernel, shared),
        grid=(nb, nblk),
        in_specs=in_specs,
        out_specs=[pl.BlockSpec((None, tb, h * dv), lambda b, i: (b, i, 0)),
                   pl.BlockSpec((None, tb, h * dv), lambda b, i: (b, nblk - 1 - i, 0)),
                   st_spec, st_spec],
        out_shape=[jax.ShapeDtypeStruct((nb, t_len, h * dv), BF16),
                   jax.ShapeDtypeStruct((nb, t_len, h * dv), BF16),
                   jax.ShapeDtypeStruct((nb, h, dk, dv), F32),
                   jax.ShapeDtypeStruct((nb, h, dk, dv), F32)],
        scratch_shapes=[pltpu.VMEM((h, dv, dk), F32), pltpu.VMEM((h, dv, dk), F32),
                        pltpu.VMEM((tb, hk), BF16),
                        pltpu.VMEM((tb // GLA_CHUNK, h, GLA_CHUNK, GLA_CHUNK), BF16),
                        pltpu.VMEM((tb // GLA_CHUNK, h, dv, dk), F32),
                        pltpu.VMEM((max(tb // GLA_CHUNK, 8), hk), F32)],
        compiler_params=_params(("arbitrary", "arbitrary")),
        name="gla_scan",
    )(*operands)


GLA_OUT_PARTS = 4


def _gla_out_kernel(na, ofa_ref, oba_ref, ofb_ref, obb_ref, r_ref, h_ref, mod_ref, on_ref, wo_ref,
                    g2_ref, rw_ref, tri_ref, h_out, f_out, route_out, wt_out, cnt_out, run):
    i = pl.program_id(0)
    is_ctx = i < na

    @pl.when(i == 0)
    def _():
        run[...] = jnp.zeros_like(run)

    tm = tri_ref.shape[0]
    for part in range(h_ref.shape[0] // tm):
        rows = slice(part * tm, (part + 1) * tm)
        _gla_out_rows(rows, tm, is_ctx, ofa_ref, oba_ref, ofb_ref, obb_ref, r_ref, h_ref, mod_ref, on_ref,
                      wo_ref, g2_ref, rw_ref, tri_ref, h_out, f_out, route_out, wt_out, run)
    cnt_out[...] = run[...]


def _gla_out_rows(rows, tm, is_ctx, ofa_ref, oba_ref, ofb_ref, obb_ref, r_ref, h_ref, mod_ref, on_ref,
                  wo_ref, g2_ref, rw_ref, tri_ref, h_out, f_out, route_out, wt_out, run):
    o = (jnp.where(is_ctx, ofa_ref[rows, :], ofb_ref[rows, :]).astype(F32)
         + jnp.where(is_ctx, oba_ref[rows, :], obb_ref[rows, :]).astype(F32))
    dv = GLA_DV
    parts = []
    for hh in range(GLA_HEADS):
        oh = o[:, hh * dv:(hh + 1) * dv]
        parts.append(_rms(oh) * on_ref[...])
    on = jnp.concatenate(parts, axis=1)
    r = r_ref[rows, :].astype(F32)
    gated = (on * _silu(r)).astype(BF16)
    h = h_ref[rows, :] + mod_ref[2:3, :] * jnp.dot(gated, wo_ref[...], preferred_element_type=F32)
    h_out[rows, :] = h
    f = _adaln(h, g2_ref[...], mod_ref[3:4, :], mod_ref[4:5, :])
    f_out[rows, :] = _pack_halves(f)
    fh = f.astype(BF16)
    fl = (f - fh.astype(F32)).astype(BF16)
    logits = (jnp.dot(fh, rw_ref[0], preferred_element_type=F32)
              + jnp.dot(fl, rw_ref[0], preferred_element_type=F32)
              + jnp.dot(fh, rw_ref[1], preferred_element_type=F32))
    lane = lax.broadcasted_iota(I32, (tm, LANES), 1)
    lane_f = lane.astype(F32)
    logits = jnp.where(lane < N_EXPERTS, logits, -jnp.inf)
    m1 = jnp.max(logits, axis=-1, keepdims=True)
    i1 = jnp.min(jnp.where(logits == m1, lane_f, float(LANES)), axis=-1, keepdims=True)
    rest = jnp.where(lane_f == i1, -jnp.inf, logits)
    m2 = jnp.max(rest, axis=-1, keepdims=True)
    i2 = jnp.min(jnp.where(rest == m2, lane_f, float(LANES)), axis=-1, keepdims=True)
    e2 = jnp.exp(m2 - m1)
    w1 = 1.0 / (1.0 + e2)
    w2 = e2 / (1.0 + e2)
    wt_out[rows, :] = jnp.where(lane == 0, w1, jnp.where(lane == 1, w2, 0.0))
    sel1 = lane_f == i1
    sel2 = lane_f == i2
    picked = jnp.where(sel1, 1.0, jnp.where(sel2, 1.0, 0.0))
    before = jnp.dot(tri_ref[...], picked.astype(BF16), preferred_element_type=F32) + run[0:1, :]
    rank1 = jnp.sum(jnp.where(sel1, before, 0.0), axis=-1, keepdims=True)
    rank2 = jnp.sum(jnp.where(sel2, before, 0.0), axis=-1, keepdims=True)
    run[...] = run[...] + jnp.sum(picked, axis=0, keepdims=True)
    routing = jnp.where(lane == 0, i1, jnp.where(lane == 1, i2,
                        jnp.where(lane == 2, rank1, jnp.where(lane == 3, rank2, 0.0))))
    route_out[:, rows] = routing.T[0:8, :]


def _gla_out(of_p, ob_p, of_s, ob_s, r, h, n_p, t_len, mod, out_norm, wo, g2, router_w):
    n, d = h.shape
    tm = _row_tile(n_p, t_len, cap=1024)
    na = n_p // tm
    nt_s = t_len // tm
    mod_idx = lambda i: (jnp.where(i < na, 0, 1 + (i - na) // nt_s), 0, 0)
    row = lambda i: (i, 0)
    ctx_row = lambda i: (jnp.minimum(i, na - 1), 0)
    lat_row = lambda i: (jnp.maximum(i - na, 0), 0)
    tp = tm // GLA_OUT_PARTS
    tri = jnp.tril(jnp.ones((tp, tp), F32), -1).astype(BF16)
    return pl.pallas_call(
        functools.partial(_gla_out_kernel, na),
        grid=(n // tm,),
        in_specs=[
            pl.BlockSpec((tm, d), ctx_row),
            pl.BlockSpec((tm, d), ctx_row),
            pl.BlockSpec((tm, d), lat_row),
            pl.BlockSpec((tm, d), lat_row),
            pl.BlockSpec((tm, d), row),
            pl.BlockSpec((tm, d), row),
            pl.BlockSpec((None, 8, d), mod_idx),
            _const_spec((1, GLA_DV)),
            _const_spec((d, d)),
            _const_spec((1, d)),
            _const_spec((2, d, LANES)),
            _const_spec((tp, tp)),
        ],
        out_specs=[pl.BlockSpec((tm, d), row), pl.BlockSpec((tm, d // 2), row),
                   pl.BlockSpec((8, tm), lambda i: (0, i)), pl.BlockSpec((tm, LANES), row),
                   _const_spec((8, LANES))],
        out_shape=[jax.ShapeDtypeStruct((n, d), F32), jax.ShapeDtypeStruct((n, d // 2), jnp.uint32),
                   jax.ShapeDtypeStruct((8, n), F32), jax.ShapeDtypeStruct((n, LANES), F32),
                   jax.ShapeDtypeStruct((8, LANES), F32)],
        scratch_shapes=[pltpu.VMEM((8, LANES), F32)],
        compiler_params=_params(("arbitrary",)),
        name="gla_out_router",
    )(of_p, ob_p, of_s, ob_s, r, h, mod, out_norm, wo, g2, router_w, tri)


SC_CORES = 2
SC_SUBCORES = 16
SC_CHUNK = 64


def _sc_gather_rows(table, idx):
    nw = SC_CORES * SC_SUBCORES
    b = idx.shape[0]
    d = table.shape[1]
    assert b % (nw * SC_CHUNK) == 0
    per_w = b // nw
    n_chunks = per_w // SC_CHUNK
    mesh = plsc.VectorSubcoreMesh(core_axis_name="c", subcore_axis_name="s",
                                  num_cores=SC_CORES, num_subcores=SC_SUBCORES)

    assert n_chunks % 2 == 0

    def body(table_hbm, idx_hbm, out_hbm, idx_v, rows_v, gsem, wsem):
        wid = lax.axis_index("s") * SC_CORES + lax.axis_index("c")
        base = wid * per_w
        pltpu.sync_copy(idx_hbm.at[wid], idx_v)

        def gather(j, slot):
            return pltpu.make_async_copy(table_hbm.at[idx_v.at[j]], rows_v.at[slot], gsem.at[slot])

        def write(j, slot):
            return pltpu.make_async_copy(rows_v.at[slot], out_hbm.at[pl.ds(base + j * SC_CHUNK, SC_CHUNK)],
                                         wsem.at[slot])

        gather(0, 0).start()

        @pl.loop(0, n_chunks, step=2)
        def _(j):
            for slot in range(2):
                jj = j + slot
                gather(jj, slot).wait()

                @pl.when(jj + 1 < n_chunks)
                def _():
                    @pl.when(jj >= 1)
                    def _():
                        write(jj - 1, 1 - slot).wait()

                    gather(jj + 1, 1 - slot).start()

                write(jj, slot).start()

        write(n_chunks - 2, 0).wait()
        write(n_chunks - 1, 1).wait()

    return pl.kernel(
        body,
        out_type=jax.ShapeDtypeStruct((b, d), table.dtype),
        mesh=mesh,
        scratch_types=[pltpu.VMEM((n_chunks, SC_CHUNK), I32),
                       pltpu.VMEM((2, SC_CHUNK, d), table.dtype),
                       pltpu.SemaphoreType.DMA((2,)),
                       pltpu.SemaphoreType.DMA((2,))],
        name="sc_gather_rows",
    )(table, idx.reshape(nw, n_chunks, SC_CHUNK))


def _sc_scatter_rows(rows, idx2, p):
    nw = SC_CORES * SC_SUBCORES
    n, d = rows.shape
    assert n % (nw * SC_CHUNK) == 0
    per_w = n // nw
    n_chunks = per_w // SC_CHUNK
    mesh = plsc.VectorSubcoreMesh(core_axis_name="c", subcore_axis_name="s",
                                  num_cores=SC_CORES, num_subcores=SC_SUBCORES)

    assert n_chunks % 2 == 0

    def body(rows_hbm, idx0_hbm, idx1_hbm, out_hbm, idx0_v, idx1_v, rows_v, rsem, ssem):
        wid = lax.axis_index("s") * SC_CORES + lax.axis_index("c")
        base = wid * per_w
        pltpu.sync_copy(idx0_hbm.at[wid], idx0_v)
        pltpu.sync_copy(idx1_hbm.at[wid], idx1_v)

        def read(j, slot):
            return pltpu.make_async_copy(rows_hbm.at[pl.ds(base + j * SC_CHUNK, SC_CHUNK)], rows_v.at[slot],
                                         rsem.at[slot])

        read(0, 0).start()

        @pl.loop(0, n_chunks, step=2)
        def _(j):
            for slot in range(2):
                jj = j + slot
                read(jj, slot).wait()

                @pl.when(jj + 1 < n_chunks)
                def _():
                    read(jj + 1, 1 - slot).start()

                s0 = pltpu.make_async_copy(rows_v.at[slot], out_hbm.at[idx0_v.at[jj]], ssem.at[0])
                s1 = pltpu.make_async_copy(rows_v.at[slot], out_hbm.at[idx1_v.at[jj]], ssem.at[1])
                s0.start()
                s1.start()
                s0.wait()
                s1.wait()

    return pl.kernel(
        body,
        out_type=jax.ShapeDtypeStruct((p, d), rows.dtype),
        mesh=mesh,
        scratch_types=[pltpu.VMEM((n_chunks, SC_CHUNK), I32),
                       pltpu.VMEM((n_chunks, SC_CHUNK), I32),
                       pltpu.VMEM((2, SC_CHUNK, d), rows.dtype),
                       pltpu.SemaphoreType.DMA((2,)),
                       pltpu.SemaphoreType.DMA((2,))],
        name="sc_scatter_rows",
    )(rows, idx2[0].reshape(nw, n_chunks, SC_CHUNK), idx2[1].reshape(nw, n_chunks, SC_CHUNK))


EXPERT_SUB = 256


def _pack_halves(x):
    k = x.shape[1] // 2
    lo = lax.bitcast_convert_type(x[:, :k].astype(BF16).astype(F32), jnp.uint32)
    hi = lax.bitcast_convert_type(x[:, k:].astype(BF16).astype(F32), jnp.uint32)
    return (lo >> 16) | (hi & jnp.uint32(0xFFFF0000))


def _unpack_halves(w):
    lo = lax.bitcast_convert_type(w << 16, F32).astype(BF16)
    hi = lax.bitcast_convert_type(w & jnp.uint32(0xFFFF0000), F32).astype(BF16)
    return lo, hi


def _expert_kernel(te_ref, ts_ref, nu_ref, nv_ref, x_ref, wg_ref, wu_ref, wd_ref, y_ref, acc, xb):
    i = pl.program_id(0)
    f = pl.program_id(1)
    nf = pl.num_programs(1)
    tm = x_ref.shape[0]
    half = x_ref.shape[1]
    sub = EXPERT_SUB
    nvalid = nv_ref[i]

    @pl.when(f == 0)
    def _():
        acc[...] = jnp.zeros_like(acc)

    @pl.when((f == 0) & (nvalid > 0))
    def _():
        rid = lax.broadcasted_iota(I32, (tm, half), 0)
        lo, hi = _unpack_halves(x_ref[...])
        zero = jnp.zeros_like(lo)
        xb[:, 0:half] = jnp.where(rid < nvalid, lo, zero)
        xb[:, half:2 * half] = jnp.where(rid < nvalid, hi, zero)

    def compute(rows):
        x = xb[rows, :]
        hg = jnp.dot(x, wg_ref[...].astype(BF16), preferred_element_type=F32)
        hu = jnp.dot(x, wu_ref[...].astype(BF16), preferred_element_type=F32)
        act = (_silu(hg) * hu).astype(BF16)
        acc[rows, :] += jnp.dot(act, wd_ref[...].astype(BF16), preferred_element_type=F32)

    for k in range(1, tm // sub + 1):
        @pl.when((nvalid > (k - 1) * sub) & (nvalid <= k * sub))
        def _():
            compute(slice(0, k * sub))

    @pl.when(f == nf - 1)
    def _():
        y_ref[...] = _pack_halves(acc[...])


def _experts(x, tile_expert, tile_src, n_used, n_valid, wg, wu, wd, tm, tf):
    p, half = x.shape
    d = 2 * half
    ne, _, dff = wg.shape
    nf = dff // tf
    assert dff % tf == 0 and tm % EXPERT_SUB == 0

    def fidx(i, f, nu):
        return jnp.where(i < nu[0], f, nf - 1)

    grid_spec = pltpu.PrefetchScalarGridSpec(
        num_scalar_prefetch=4,
        grid=(p // tm, nf),
        in_specs=[
            pl.BlockSpec((tm, half), lambda i, f, te, ts, nu, nv: (ts[i], 0)),
            pl.BlockSpec((None, d, tf), lambda i, f, te, ts, nu, nv: (te[i], 0, fidx(i, f, nu))),
            pl.BlockSpec((None, d, tf), lambda i, f, te, ts, nu, nv: (te[i], 0, fidx(i, f, nu))),
            pl.BlockSpec((None, tf, d), lambda i, f, te, ts, nu, nv: (te[i], fidx(i, f, nu), 0)),
        ],
        out_specs=pl.BlockSpec((tm, half), lambda i, f, te, ts, nu, nv: (i, 0)),
        scratch_shapes=[pltpu.VMEM((tm, d), F32), pltpu.VMEM((tm, d), BF16)],
    )
    return pl.pallas_call(
        _expert_kernel,
        grid_spec=grid_spec,
        out_shape=jax.ShapeDtypeStruct((p, half), jnp.uint32),
        compiler_params=_params(("arbitrary", "arbitrary"), 60 * 1024 * 1024),
        name="moe_experts",
    )(tile_expert, tile_src, n_used, n_valid, x, wg, wu, wd)


def _combine_kernel(wt_ref, h_ref, mod_ref, y0_ref, y1_ref, o_ref):
    wt = wt_ref[...]
    half = y0_ref.shape[1]
    lo0, hi0 = _unpack_halves(y0_ref[...])
    lo1, hi1 = _unpack_halves(y1_ref[...])
    w0, w1 = wt[:, 0:1], wt[:, 1:2]
    h = h_ref[...]
    gate = mod_ref[5:6, :]
    o_ref[:, 0:half] = h[:, 0:half] + gate[:, 0:half] * (w0 * lo0.astype(F32) + w1 * lo1.astype(F32))
    o_ref[:, half:] = h[:, half:] + gate[:, half:] * (w0 * hi0.astype(F32) + w1 * hi1.astype(F32))


def _combine(yg, wt, h, mod, row_off, n_rows, n_mod, mod_off, tm):
    d = h.shape[1]
    nt = n_rows // tm
    toff = row_off // tm
    per_mod = n_rows // n_mod // tm
    return pl.pallas_call(
        _combine_kernel,
        grid=(nt,),
        in_specs=[
            pl.BlockSpec((tm, LANES), lambda i: (toff + i, 0)),
            pl.BlockSpec((tm, d), lambda i: (toff + i, 0)),
            pl.BlockSpec((None, 8, d), lambda i: (mod_off + i // per_mod, 0, 0)),
            pl.BlockSpec((None, tm, d // 2), lambda i: (0, i, 0)),
            pl.BlockSpec((None, tm, d // 2), lambda i: (1, i, 0)),
        ],
        out_specs=pl.BlockSpec((tm, d), lambda i: (i, 0)),
        out_shape=jax.ShapeDtypeStruct((n_rows, d), F32),
        compiler_params=_params(("arbitrary",)),
        name="moe_combine",
    )(wt, h, mod, yg, yg)


def _route(route, counts, n, tm):
    cnt = counts[0, :N_EXPERTS].astype(I32)
    padded = ((cnt + tm - 1) // tm) * tm
    gend = jnp.cumsum(padded)
    goff = gend - padded
    e = route[0:2].astype(I32)
    rank = route[2:4].astype(I32)
    onehot = e[:, :, None] == jnp.arange(N_EXPERTS, dtype=I32)[None, None, :]
    dest = jnp.sum(jnp.where(onehot, goff[None, None, :], 0), axis=-1) + rank
    p = 2 * n + N_EXPERTS * tm
    n_used = gend[-1] // tm
    tiles = jnp.arange(p // tm, dtype=I32)
    tile_src = jnp.minimum(tiles, n_used - 1)
    tile_expert = jnp.minimum(jnp.sum((gend[None, :] <= (tile_src * tm)[:, None]).astype(I32), axis=1),
                              N_EXPERTS - 1)
    used = goff[tile_expert] + cnt[tile_expert]
    n_valid = jnp.where(tiles < n_used, jnp.clip(used - tiles * tm, 0, tm), 0).astype(I32)
    return dest, p, tile_expert, tile_src, n_used.reshape(1).astype(I32), n_valid


def kernel(x_prompt, x_sample, cache_k, cache_v, state_fwd, state_bwd, c, c_ctx, ada_w, ada_b, norm1_g, norm2_g, attn_w_qkv, attn_q_norm, attn_k_norm, attn_sink, attn_w_o, gla_w_in, gla_gate_w1, gla_gate_w2, gla_gate_b, gla_out_norm, gla_w_o, ffn_w_gate, ffn_w_up, ffn_w_down, moe_router, moe_w_gate, moe_w_up, moe_w_down):
    bp, seq, d = x_prompt.shape
    db, t_len, _ = x_sample.shape
    n_p, n_s = bp * seq, db * t_len
    n = n_p + n_s
    xp = x_prompt.reshape(n_p, d)
    xs = x_sample.reshape(n_s, d)

    cond = jnp.concatenate([c_ctx[None, :], c], axis=0)
    assert cond.shape[0] <= 8
    cond_t = jnp.pad(cond, ((0, 8 - cond.shape[0]), (0, 0))).T
    mods = _modulation(cond_t, cond.shape[0], ada_w, ada_b)

    nk = N_KV_HEADS * HEAD_DIM
    qn = jnp.tile(attn_q_norm[0], N_HEADS)[None, :]
    kn = jnp.tile(attn_k_norm[0], N_KV_HEADS)[None, :]
    q, kt, vv, ck_new, cv_new = _qkv(xp, xs, t_len, mods[0], norm1_g[0][None, :],
                                     attn_w_qkv[0].astype(BF16), qn, kn)
    wo0 = attn_w_o[0].astype(BF16)
    sink = attn_sink[0]
    hp = _ctx_attention(sink, q, kt, vv, xp, seq, mods[0], wo0)
    ck = cache_k[:, 0].astype(BF16)
    cv = cache_v[:, 0].astype(BF16)
    ckt = jnp.concatenate([ck, ck], axis=-1).transpose(0, 2, 3, 1)
    cvv = jnp.concatenate([cv, cv], axis=-1).reshape(db, cv.shape[1], N_KV_HEADS * LANES)
    hs = _lat_attention(sink, q, kt, vv, ckt, cvv, xs, n_p, t_len, mods[0], wo0)
    new_k = ck_new.reshape(bp, 1, seq, N_KV_HEADS, HEAD_DIM)
    new_v = cv_new.reshape(bp, 1, seq, N_KV_HEADS, HEAD_DIM)
    h = _ffn(hp, hs, t_len, mods[0], norm2_g[0][None, :], ffn_w_gate[0].astype(BF16),
             ffn_w_up[0].astype(BF16), ffn_w_down[0].astype(BF16))

    hk = GLA_HEADS * GLA_DK
    rank = GLA_GATE_RANK
    w1 = jnp.concatenate([gla_gate_w1[0, 0], gla_gate_w1[0, 1]], axis=1)
    w1 = jnp.pad(w1, ((0, 0), (0, LANES - 2 * rank))).astype(BF16)
    w2 = jnp.zeros((LANES, 2 * hk), F32)
    w2 = w2.at[0:rank, 0:hk].set(gla_gate_w2[0, 0]).at[rank:2 * rank, hk:].set(gla_gate_w2[0, 1]).astype(BF16)
    gate_b = gla_gate_b[0].reshape(1, 2 * hk)
    qk, v, r, bc = _gla_in(h, n_p, t_len, mods[1], norm1_g[1][None, :], gla_w_in[0].astype(BF16),
                           w1, w2, gate_b)
    zeros = jnp.zeros((bp, GLA_HEADS, GLA_DK, GLA_DV), F32)
    of_p, ob_p, new_sf, new_sb = _gla_scan(qk, v, bc, zeros, zeros, 0, bp, seq)
    of_s, ob_s, _, _ = _gla_scan(qk, v, bc, state_fwd[:, 0], state_bwd[:, 0], n_p // t_len, db, t_len)
    rw = jnp.pad(moe_router[0], ((0, 0), (0, LANES - N_EXPERTS)))
    rw_hi = rw.astype(BF16)
    rw_lo = (rw - rw_hi.astype(F32)).astype(BF16)
    h, f, route, wt, counts = _gla_out(
        of_p.reshape(n_p, d), ob_p.reshape(n_p, d), of_s.reshape(n_s, d), ob_s.reshape(n_s, d), r, h,
        n_p, t_len, mods[1], gla_out_norm[0][None, :], gla_w_o[0].astype(BF16), norm2_g[1][None, :],
        jnp.stack([rw_hi, rw_lo]))

    tm_e = 1024
    dest_k, p_rows, tile_expert, tile_src, n_used, n_valid = _route(route, counts, n, tm_e)
    xg = _sc_scatter_rows(f, dest_k, p_rows)
    y = _experts(xg, tile_expert, tile_src, n_used, n_valid, moe_w_gate[0], moe_w_up[0],
                 moe_w_down[0], tm_e, 512)
    yg_p = _sc_gather_rows(y, dest_k[:, :n_p].reshape(2 * n_p)).reshape(2, n_p, d // 2)
    yg_s = _sc_gather_rows(y, dest_k[:, n_p:].reshape(2 * n_s)).reshape(2, n_s, d // 2)
    tm_c = _row_tile(n_p, t_len)
    y_p = _combine(yg_p, wt, h, mods[1], 0, n_p, 1, 0, tm_c)
    y_s = _combine(yg_s, wt, h, mods[1], n_p, n_s, db, 1, tm_c)

    return (y_p.reshape(bp, seq, d), y_s.reshape(db, t_len, d), new_k, new_v,
            new_sf[:, None], new_sb[:, None])
```

```python
import functools
import math

import numpy as np
import jax
import jax.numpy as jnp
from jax import lax
from jax.experimental import pallas as pl
from jax.experimental.pallas import tpu as pltpu
from jax.experimental.pallas import tpu_sc as plsc

F32 = jnp.float32
BF16 = jnp.bfloat16
I32 = jnp.int32

D_MODEL = 1024
N_HEADS = 16
N_KV_HEADS = 4
HEAD_DIM = 64
GRID_W = 64
WINDOW = 128
ATTN_BLOCK = 128
ROPE_THETA = 10000.0
GLA_HEADS = 4
GLA_DK = 128
GLA_DV = 256
GLA_GATE_RANK = 16
GLA_GATE_TAU = 16.0
GLA_CHUNK = 64
N_EXPERTS = 8
NORM_EPS = 1e-6
NEG_INF = -1e30

LANES = 128
VMEM_LIMIT = 56 * 1024 * 1024


def _params(sem, vmem=VMEM_LIMIT):
    return pltpu.CompilerParams(dimension_semantics=sem, vmem_limit_bytes=vmem)


def _row_tile(*counts, cap=512):
    t = cap
    while any(c % t for c in counts):
        t //= 2
    assert t >= 8
    return t


def _rms(x):
    return x * lax.rsqrt(jnp.mean(x * x, axis=-1, keepdims=True) + NORM_EPS)


def _adaln(x, g, shift, scale):
    return _rms(x) * (g * (1.0 + scale)) + shift


def _silu(x):
    return x * jax.nn.sigmoid(x)


def _const_spec(shape):
    nd = len(shape)
    return pl.BlockSpec(shape, lambda *_: (0,) * nd)


def _mod_kernel(n_cond, ct_ref, w_ref, b_ref, o_ref):
    ct = ct_ref[...]
    s = _silu(ct)
    w = w_ref[...]
    rows = [jnp.sum(w * s[:, r:r + 1], axis=0, keepdims=True) for r in range(n_cond)]
    rows += [jnp.zeros_like(rows[0])] * (8 - n_cond)
    o_ref[...] = jnp.concatenate(rows, axis=0) + b_ref[...]


def _modulation(cond_t, n_cond, ada_w, ada_b):
    depth, d, n6 = ada_w.shape
    tn = 1024
    out = pl.pallas_call(
        functools.partial(_mod_kernel, n_cond),
        grid=(depth, n6 // tn),
        in_specs=[
            pl.BlockSpec((d, 8), lambda l, j: (0, 0)),
            pl.BlockSpec((None, d, tn), lambda l, j: (l, 0, j)),
            pl.BlockSpec((None, 1, tn), lambda l, j: (l, 0, j)),
        ],
        out_specs=pl.BlockSpec((None, 8, tn), lambda l, j: (l, 0, j)),
        out_shape=jax.ShapeDtypeStruct((depth, 8, n6), F32),
        compiler_params=_params(("arbitrary", "arbitrary")),
        name="modulation",
    )(cond_t, ada_w, ada_b.reshape(depth, 1, n6))
    m = out[:, :n_cond].reshape(depth, n_cond, 6, d)
    return jnp.pad(m, ((0, 0), (0, 0), (0, 2), (0, 0)))


LOG2E = math.log2(math.e)


def _dup_half(k2, half):
    lane = lax.broadcasted_iota(I32, k2.shape, 1)
    lo = lane < HEAD_DIM
    r = pltpu.roll(k2, HEAD_DIM, 1)
    return jnp.where(lo, k2, r) if half == 0 else jnp.where(lo, r, k2)


QKV_PARTS = 2


def _qkv_kernel(na, xa_ref, xb_ref, mod_ref, g_ref, w_ref, qn_ref, kn_ref, bd_ref,
                cos_ref, sin_ref, q_out, kt_out, vv_out, ck_out, cv_out):
    i = pl.program_id(0)
    is_ctx = i < na
    tm = xa_ref.shape[0] // QKV_PARTS
    for part in range(QKV_PARTS):
        _qkv_rows(slice(part * tm, (part + 1) * tm), tm, is_ctx, xa_ref, xb_ref, mod_ref, g_ref, w_ref,
                  qn_ref, kn_ref, bd_ref, cos_ref, sin_ref, q_out, kt_out, vv_out, ck_out, cv_out)


def _qkv_rows(rows, tm, is_ctx, xa_ref, xb_ref, mod_ref, g_ref, w_ref, qn_ref, kn_ref, bd_ref,
              cos_ref, sin_ref, q_out, kt_out, vv_out, ck_out, cv_out):
    x = jnp.where(is_ctx, xa_ref[rows, :], xb_ref[rows, :])
    a = _adaln(x, g_ref[...], mod_ref[0:1, :], mod_ref[1:2, :])
    y = jnp.dot(a.astype(BF16), w_ref[...], preferred_element_type=F32)
    cos = jnp.where(is_ctx, 1.0, cos_ref[rows, :])
    sin = jnp.where(is_ctx, 0.0, sin_ref[rows, :])
    lane = lax.broadcasted_iota(I32, (tm, LANES), 1)
    first16 = (lane % 32) < 16

    def norm_rope(z, wt):
        ss = jnp.dot((z * z).astype(BF16), bd_ref[...], preferred_element_type=F32)
        zn = z * lax.rsqrt(ss * (1.0 / HEAD_DIM) + NORM_EPS) * wt
        outs = []
        for c in range(2):
            t = zn[:, c * LANES:(c + 1) * LANES]
            partner = jnp.where(first16, pltpu.roll(t, LANES - 16, 1), pltpu.roll(t, 16, 1))
            outs.append(t * cos + partner * sin)
        return jnp.concatenate(outs, axis=1)

    nq = N_HEADS * HEAD_DIM
    nk = N_KV_HEADS * HEAD_DIM
    scale = HEAD_DIM ** -0.5 * LOG2E
    for s in range(nq // 256):
        sl = slice(s * 256, (s + 1) * 256)
        q_out[rows, sl] = (norm_rope(y[:, sl], qn_ref[:, sl]) * scale).astype(BF16)
    k = norm_rope(y[:, nq:nq + nk], kn_ref[...])
    v = y[:, nq + nk:nq + 2 * nk]
    for g in range(N_KV_HEADS):
        c = g // 2
        kk = _dup_half(k[:, c * LANES:(c + 1) * LANES], g % 2)
        kt_out[g, :, rows] = kk.T.astype(BF16)
        vv_out[rows, g * LANES:(g + 1) * LANES] = _dup_half(v[:, c * LANES:(c + 1) * LANES], g % 2).astype(BF16)

    @pl.when(is_ctx)
    def _():
        ck_out[rows, :] = k
        cv_out[rows, :] = v


def _rope_tables(t_len):
    pos = np.arange(t_len)
    row = (pos // GRID_W).astype(np.float32)[:, None]
    col = (pos % GRID_W).astype(np.float32)[:, None]
    half = HEAD_DIM // 2
    inv = (np.float32(ROPE_THETA) ** (-np.arange(0, half, 2, dtype=np.float32) / np.float32(half)))[None, :]
    ar, ac = row * inv, col * inv
    cos = np.concatenate([np.cos(ar), np.cos(ar), np.cos(ac), np.cos(ac)], axis=1)
    sin = np.concatenate([-np.sin(ar), np.sin(ar), -np.sin(ac), np.sin(ac)], axis=1)
    return (jnp.asarray(np.tile(cos, (1, 2)), dtype=F32), jnp.asarray(np.tile(sin, (1, 2)), dtype=F32))


def _qkv(xp, xs, t_len, mod, g, w, qn, kn):
    n_p, n_s = xp.shape[0], xs.shape[0]
    tm = _row_tile(n_p, t_len, cap=1024)
    na, nb = n_p // tm, n_s // tm
    nt_s = t_len // tm
    n = n_p + n_s
    d = D_MODEL
    nq, nk = N_HEADS * HEAD_DIM, N_KV_HEADS * HEAD_DIM
    cos, sin = _rope_tables(t_len)
    eye4 = jnp.kron(jnp.eye(4, dtype=F32), jnp.ones((HEAD_DIM, HEAD_DIM), F32)).astype(BF16)
    mod_idx = lambda i: (jnp.where(i < na, 0, 1 + (i - na) // nt_s), 0, 0)
    pos_idx = lambda i: (jnp.maximum(i - na, 0) % nt_s, 0)
    return pl.pallas_call(
        functools.partial(_qkv_kernel, na),
        grid=(na + nb,),
        in_specs=[
            pl.BlockSpec((tm, d), lambda i: (jnp.minimum(i, na - 1), 0)),
            pl.BlockSpec((tm, d), lambda i: (jnp.maximum(i - na, 0), 0)),
            pl.BlockSpec((None, 8, d), mod_idx),
            _const_spec((1, d)),
            _const_spec((d, nq + 2 * nk)),
            _const_spec((1, nq)),
            _const_spec((1, nk)),
            _const_spec((256, 256)),
            pl.BlockSpec((tm, LANES), pos_idx),
            pl.BlockSpec((tm, LANES), pos_idx),
        ],
        out_specs=[pl.BlockSpec((tm, nq), lambda i: (i, 0)),
                   pl.BlockSpec((N_KV_HEADS, LANES, tm), lambda i: (0, 0, i)),
                   pl.BlockSpec((tm, N_KV_HEADS * LANES), lambda i: (i, 0)),
                   pl.BlockSpec((tm, nk), lambda i: (jnp.minimum(i, na - 1), 0)),
                   pl.BlockSpec((tm, nk), lambda i: (jnp.minimum(i, na - 1), 0))],
        out_shape=[jax.ShapeDtypeStruct((n, nq), BF16),
                   jax.ShapeDtypeStruct((N_KV_HEADS, LANES, n), BF16),
                   jax.ShapeDtypeStruct((n, N_KV_HEADS * LANES), BF16),
                   jax.ShapeDtypeStruct((n_p, nk), F32),
                   jax.ShapeDtypeStruct((n_p, nk), F32)],
        compiler_params=_params(("arbitrary",)),
        name="qkv",
    )(xp, xs, mod, g, w, qn, kn, eye4, cos, sin)


def _attn_group(q2a, q2b, kt, vv, bias, n_bias, sinks):
    tq = q2a.shape[0]
    nk = kt.shape[1]
    lo = lax.broadcasted_iota(I32, (tq, LANES), 1) < HEAD_DIM
    top = lax.broadcasted_iota(I32, kt.shape, 0) < HEAD_DIM
    zk = jnp.zeros_like(kt)
    kbd = jnp.concatenate([jnp.where(top, kt, zk), jnp.where(top, zk, kt)], axis=1)
    left = lax.broadcasted_iota(I32, vv.shape, 1) < HEAD_DIM
    zv = jnp.zeros_like(vv)
    vbd = jnp.concatenate([jnp.where(left, vv, zv), jnp.where(left, zv, vv)], axis=0)
    s = jnp.dot(jnp.concatenate([q2a, q2b], axis=0), kbd, preferred_element_type=F32)
    p_rows, scales = [], []
    for t in range(2):
        ph, rinv = [], []
        for hf in range(2):
            sh = s[t * tq:(t + 1) * tq, hf * nk:(hf + 1) * nk]
            if bias is not None:
                sh = jnp.concatenate([sh[:, :n_bias] + bias, sh[:, n_bias:]], axis=1)
            sk = sinks[2 * t + hf]
            m = jnp.maximum(jnp.max(sh, axis=-1, keepdims=True), sk)
            p = jnp.exp2(sh - m)
            rinv.append(1.0 / (jnp.sum(p, axis=-1, keepdims=True) + jnp.exp2(sk - m)))
            ph.append(p.astype(BF16))
        p_rows.append(jnp.concatenate(ph, axis=1))
        scales.append(jnp.where(lo, rinv[0], rinv[1]))
    o = jnp.dot(jnp.concatenate(p_rows, axis=0), vbd, preferred_element_type=F32)
    return o[0:tq] * scales[0], o[tq:2 * tq] * scales[1]


def _ctx_attn_kernel(sink_ref, q_ref, kt_ref, vv_ref, x_ref, mod_ref, wo_ref, o_ref, osc):
    for g in range(N_KV_HEADS):
        q2a = q_ref[:, (2 * g) * LANES:(2 * g + 1) * LANES]
        q2b = q_ref[:, (2 * g + 1) * LANES:(2 * g + 2) * LANES]
        sinks = [sink_ref[4 * g + j] * LOG2E for j in range(4)]
        oa, ob = _attn_group(q2a, q2b, kt_ref[g], vv_ref[:, g * LANES:(g + 1) * LANES], None, 0, sinks)
        osc[:, (2 * g) * LANES:(2 * g + 1) * LANES] = oa.astype(BF16)
        osc[:, (2 * g + 1) * LANES:(2 * g + 2) * LANES] = ob.astype(BF16)
    att = jnp.dot(osc[...], wo_ref[...], preferred_element_type=F32)
    o_ref[...] = x_ref[...] + mod_ref[2:3, :] * att


def _ctx_attention(sink, q, kt, vv, xp, seq, mod, wo):
    n_p, d = xp.shape
    nb = n_p // seq
    return pl.pallas_call(
        _ctx_attn_kernel,
        grid=(nb,),
        in_specs=[
            pl.BlockSpec(memory_space=pltpu.SMEM),
            pl.BlockSpec((seq, d), lambda b: (b, 0)),
            pl.BlockSpec((N_KV_HEADS, LANES, seq), lambda b: (0, 0, b)),
            pl.BlockSpec((seq, N_KV_HEADS * LANES), lambda b: (b, 0)),
            pl.BlockSpec((seq, d), lambda b: (b, 0)),
            pl.BlockSpec((None, 8, d), lambda b: (0, 0, 0)),
            _const_spec((d, d)),
        ],
        out_specs=pl.BlockSpec((seq, d), lambda b: (b, 0)),
        out_shape=jax.ShapeDtypeStruct((n_p, d), F32),
        scratch_shapes=[pltpu.VMEM((seq, d), BF16)],
        compiler_params=_params(("arbitrary",)),
        name="ctx_attention",
    )(sink, q, kt, vv, xp, mod, wo)


LAT_QB = 4


def _lat_attn_kernel(t_len, sink_ref, q_ref, ktp_ref, kto_ref, ktn_ref, vvp_ref, vvo_ref, vvn_ref,
                     ckt_ref, cvv_ref, x_ref, mod_ref, wo_ref, o_ref, osc):
    step = pl.program_id(1)
    tq = ATTN_BLOCK
    nloc = 3 * ATTN_BLOCK
    qi = lax.broadcasted_iota(I32, (tq, nloc), 0)
    kj = lax.broadcasted_iota(I32, (tq, nloc), 1)
    in_window = jnp.abs(qi + tq - kj) <= WINDOW
    for u in range(LAT_QB):
        n = step * LAT_QB + u
        kpos = (n - 1) * tq + kj
        bias = jnp.where(in_window & (kpos >= 0) & (kpos < t_len), 0.0, NEG_INF)
        rows = slice(u * tq, (u + 1) * tq)
        for g in range(N_KV_HEADS):
            vs = slice(g * LANES, (g + 1) * LANES)
            kts = [ktp_ref[g]] + [kto_ref[g, :, j * tq:(j + 1) * tq] for j in range(LAT_QB)] + [ktn_ref[g]]
            vvs = [vvp_ref[:, vs]] + [vvo_ref[j * tq:(j + 1) * tq, vs] for j in range(LAT_QB)] + [vvn_ref[:, vs]]
            kt = jnp.concatenate(kts[u:u + 3] + [ckt_ref[g]], axis=1)
            vv = jnp.concatenate(vvs[u:u + 3] + [cvv_ref[:, vs]], axis=0)
            q2a = q_ref[rows, (2 * g) * LANES:(2 * g + 1) * LANES]
            q2b = q_ref[rows, (2 * g + 1) * LANES:(2 * g + 2) * LANES]
            sinks = [sink_ref[4 * g + j] * LOG2E for j in range(4)]
            oa, ob = _attn_group(q2a, q2b, kt, vv, bias, nloc, sinks)
            osc[rows, (2 * g) * LANES:(2 * g + 1) * LANES] = oa.astype(BF16)
            osc[rows, (2 * g + 1) * LANES:(2 * g + 2) * LANES] = ob.astype(BF16)
    att = jnp.dot(osc[...], wo_ref[...], preferred_element_type=F32)
    o_ref[...] = x_ref[...] + mod_ref[2:3, :] * att


def _lat_attention(sink, q, kt, vv, ckt, cvv, xs, n_p, t_len, mod, wo):
    n_s, d = xs.shape
    db = n_s // t_len
    tq = ATTN_BLOCK
    ts = LAT_QB * tq
    nblk = t_len // tq
    nstep = t_len // ts
    assert n_p % ts == 0 and t_len % ts == 0
    off = n_p // tq
    npast = ckt.shape[3]
    g4 = N_KV_HEADS

    def nbr(delta):
        return lambda b, m: off + b * nblk + jnp.clip(m * LAT_QB + delta, 0, nblk - 1)

    own = lambda b, m: n_p // ts + b * nstep + m
    return pl.pallas_call(
        functools.partial(_lat_attn_kernel, t_len),
        grid=(db, nstep),
        in_specs=[
            pl.BlockSpec(memory_space=pltpu.SMEM),
            pl.BlockSpec((ts, d), lambda b, m: (own(b, m), 0)),
            pl.BlockSpec((g4, LANES, tq), lambda b, m: (0, 0, nbr(-1)(b, m))),
            pl.BlockSpec((g4, LANES, ts), lambda b, m: (0, 0, own(b, m))),
            pl.BlockSpec((g4, LANES, tq), lambda b, m: (0, 0, nbr(LAT_QB)(b, m))),
            pl.BlockSpec((tq, g4 * LANES), lambda b, m: (nbr(-1)(b, m), 0)),
            pl.BlockSpec((ts, g4 * LANES), lambda b, m: (own(b, m), 0)),
            pl.BlockSpec((tq, g4 * LANES), lambda b, m: (nbr(LAT_QB)(b, m), 0)),
            pl.BlockSpec((None, g4, LANES, npast), lambda b, m: (b, 0, 0, 0)),
            pl.BlockSpec((None, npast, g4 * LANES), lambda b, m: (b, 0, 0)),
            pl.BlockSpec((ts, d), lambda b, m: (b * nstep + m, 0)),
            pl.BlockSpec((None, 8, d), lambda b, m: (1 + b, 0, 0)),
            _const_spec((d, d)),
        ],
        out_specs=pl.BlockSpec((ts, d), lambda b, m: (b * nstep + m, 0)),
        out_shape=jax.ShapeDtypeStruct((n_s, d), F32),
        scratch_shapes=[pltpu.VMEM((ts, d), BF16)],
        compiler_params=_params(("arbitrary", "arbitrary")),
        name="lat_attention",
    )(sink, q, kt, kt, kt, vv, vv, vv, ckt, cvv, xs, mod, wo)


def _ffn_kernel(na, fc, xa_ref, xb_ref, mod_ref, g_ref, wg_ref, wu_ref, wd_ref, o_ref, acc):
    i = pl.program_id(0)
    x = jnp.where(i < na, xa_ref[...], xb_ref[...])
    a = _adaln(x, g_ref[...], mod_ref[3:4, :], mod_ref[4:5, :]).astype(BF16)
    nf = wg_ref.shape[1] // fc
    for f in range(nf):
        sl = slice(f * fc, (f + 1) * fc)
        hg = jnp.dot(a, wg_ref[:, sl], preferred_element_type=F32)
        hu = jnp.dot(a, wu_ref[:, sl], preferred_element_type=F32)
        act = (_silu(hg) * hu).astype(BF16)
        part = jnp.dot(act, wd_ref[sl, :], preferred_element_type=F32)
        if f == 0:
            acc[...] = part
        else:
            acc[...] += part
    o_ref[...] = x + mod_ref[5:6, :] * acc[...]


def _ffn(hp, hs, t_len, mod, g, wg, wu, wd):
    n_p, n_s = hp.shape[0], hs.shape[0]
    d, dff = wg.shape
    tm = _row_tile(n_p, t_len)
    na, nb = n_p // tm, n_s // tm
    nt_s = t_len // tm
    fc = 256
    assert dff % fc == 0
    mod_idx = lambda i: (jnp.where(i < na, 0, 1 + (i - na) // nt_s), 0, 0)
    return pl.pallas_call(
        functools.partial(_ffn_kernel, na, fc),
        grid=(na + nb,),
        in_specs=[
            pl.BlockSpec((tm, d), lambda i: (jnp.minimum(i, na - 1), 0)),
            pl.BlockSpec((tm, d), lambda i: (jnp.maximum(i - na, 0), 0)),
            pl.BlockSpec((None, 8, d), mod_idx),
            _const_spec((1, d)),
            _const_spec((d, dff)),
            _const_spec((d, dff)),
            _const_spec((dff, d)),
        ],
        out_specs=pl.BlockSpec((tm, d), lambda i: (i, 0)),
        out_shape=jax.ShapeDtypeStruct((n_p + n_s, d), F32),
        scratch_shapes=[pltpu.VMEM((tm, d), F32)],
        compiler_params=_params(("arbitrary",)),
        name="ffn",
    )(hp, hs, mod, g, wg, wu, wd)


def _split2(x):
    hi = x.astype(BF16)
    lo = (x - hi.astype(F32)).astype(BF16)
    return hi, lo


def _gla_in_kernel(nt_s, x_ref, mod_ref, g_ref, w_ref, w1_ref, w2_ref, gb_ref, tri_ref,
                   qk_out, v_out, r_out, b_out):
    x = x_ref[...]
    tm = x.shape[0]
    a = _adaln(x, g_ref[...], mod_ref[0:1, :], mod_ref[1:2, :]).astype(BF16)
    hk = GLA_HEADS * GLA_DK
    hv = GLA_HEADS * GLA_DV
    q = jnp.dot(a, w_ref[:, 0:hk], preferred_element_type=F32)
    qk_out[:, 0:hk] = (q * (GLA_DK ** -0.5)).astype(BF16)
    qk_out[:, hk:2 * hk] = jnp.dot(a, w_ref[:, hk:2 * hk], preferred_element_type=F32).astype(BF16)
    v_out[...] = jnp.dot(a, w_ref[:, 2 * hk:2 * hk + hv], preferred_element_type=F32).astype(BF16)
    r_out[...] = jnp.dot(a, w_ref[:, 2 * hk + hv:2 * hk + 2 * hv], preferred_element_type=F32).astype(BF16)
    z1 = jnp.dot(a, w1_ref[...], preferred_element_type=F32)
    z = jnp.dot(z1.astype(BF16), w2_ref[...], preferred_element_type=F32) + gb_ref[...]
    gate = (jnp.minimum(z, 0.0) - jnp.log(1.0 + jnp.exp(-jnp.abs(z)))) * (1.0 / GLA_GATE_TAU)
    c = GLA_CHUNK
    tri = tri_ref[...]
    for j in range(tm // c):
        rows = slice(j * c, (j + 1) * c)
        for dr in range(2):
            cols = slice(dr * hk, (dr + 1) * hk)
            hi, lo = _split2(gate[rows, cols])
            b_out[rows, cols] = (jnp.dot(tri[dr], hi, preferred_element_type=F32)
                                 + jnp.dot(tri[dr], lo, preferred_element_type=F32))


def _gla_in(h, n_p, t_len, mod, g, w_in, w1, w2, gate_b):
    n, d = h.shape
    tm = _row_tile(n_p, t_len, cap=1024)
    na = n_p // tm
    nt_s = t_len // tm
    hk, hv = GLA_HEADS * GLA_DK, GLA_HEADS * GLA_DV
    c = GLA_CHUNK
    lower = jnp.tril(jnp.ones((c, c), F32))
    tri = jnp.stack([lower, lower.T]).astype(BF16)
    mod_idx = lambda i: (jnp.where(i < na, 0, 1 + (i - na) // nt_s), 0, 0)
    row = lambda i: (i, 0)
    return pl.pallas_call(
        functools.partial(_gla_in_kernel, nt_s),
        grid=(n // tm,),
        in_specs=[
            pl.BlockSpec((tm, d), row),
            pl.BlockSpec((None, 8, d), mod_idx),
            _const_spec((1, d)),
            _const_spec(w_in.shape),
            _const_spec(w1.shape),
            _const_spec(w2.shape),
            _const_spec((1, 2 * hk)),
            _const_spec((2, c, c)),
        ],
        out_specs=[pl.BlockSpec((tm, 2 * hk), row), pl.BlockSpec((tm, hv), row),
                   pl.BlockSpec((tm, hv), row), pl.BlockSpec((tm, 2 * hk), row)],
        out_shape=[jax.ShapeDtypeStruct((n, 2 * hk), BF16), jax.ShapeDtypeStruct((n, hv), BF16),
                   jax.ShapeDtypeStruct((n, hv), BF16), jax.ShapeDtypeStruct((n, 2 * hk), F32)],
        compiler_params=_params(("arbitrary",)),
        name="gla_in",
    )(h, mod, g, w_in, w1, w2, gate_b, tri)


def _gla_direction(qk_ref, v_ref, b_ref, st, o_ref, forward, qd_s, att_s, upd_s, dec_s):
    c = GLA_CHUNK
    ncb = qk_ref.shape[0] // c
    hk = GLA_HEADS * GLA_DK
    dk, dv = GLA_DK, GLA_DV
    ri = lax.broadcasted_iota(I32, (c, c), 0)
    ci_ = lax.broadcasted_iota(I32, (c, c), 1)
    keep = (ci_ <= ri) if forward else (ci_ >= ri)

    for ch in range(ncb):
        rows = slice(ch * c, (ch + 1) * c)
        b = b_ref[rows, :]
        if forward:
            b_end, b_mid = b[c - 1:c, :], b[c // 2 - 1:c // 2, :]
        else:
            b_end, b_mid = b[0:1, :], b[c // 2:c // 2 + 1, :]
        qa = qk_ref[rows, 0:hk] * jnp.exp(b - b_mid)
        ka = qk_ref[rows, hk:2 * hk] * jnp.exp(b_mid - b)
        qd_s[rows, :] = (qa * jnp.exp(b_mid)).astype(BF16)
        kd = (ka * jnp.exp(b_end - b_mid)).astype(BF16)
        dec_s[ch:ch + 1, :] = jnp.exp(b_end)
        qab, kab = qa.astype(BF16), ka.astype(BF16)
        for h in range(GLA_HEADS):
            ks = slice(h * dk, (h + 1) * dk)
            vs = slice(h * dv, (h + 1) * dv)
            att = lax.dot_general(qab[:, ks], kab[:, ks], (((1,), (1,)), ((), ())),
                                  preferred_element_type=F32)
            att_s[ch, h] = jnp.where(keep, att, 0.0).astype(BF16)
            upd_s[ch, h] = lax.dot_general(v_ref[rows, vs], kd[:, ks], (((0,), (0,)), ((), ())),
                                           preferred_element_type=F32)

    for ch in (range(ncb) if forward else reversed(range(ncb))):
        rows = slice(ch * c, (ch + 1) * c)
        for h in range(GLA_HEADS):
            ks = slice(h * dk, (h + 1) * dk)
            vs = slice(h * dv, (h + 1) * dv)
            s = st[h]
            o = lax.dot_general(qd_s[rows, ks], s.astype(BF16), (((1,), (1,)), ((), ())),
                                preferred_element_type=F32)
            o = o + jnp.dot(att_s[ch, h], v_ref[rows, vs], preferred_element_type=F32)
            o_ref[rows, vs] = o.astype(o_ref.dtype)
            st[h] = s * dec_s[ch:ch + 1, ks] + upd_s[ch, h]


def _gla_scan_kernel(shared, from_zero, *refs):
    if shared:
        qkf_ref, vf_ref, bf_ref, bb_ref = refs[:4]
        qkb_ref, vb_ref = qkf_ref, vf_ref
        rest = refs[4:]
    else:
        qkf_ref, vf_ref, bf_ref, qkb_ref, vb_ref, bb_ref = refs[:6]
        rest = refs[6:]
    if not from_zero:
        sf0_ref, sb0_ref = rest[:2]
        rest = rest[2:]
    of_ref, ob_ref, sf_ref, sb_ref, stf, stb, qd_s, att_s, upd_s, dec_s = rest
    i = pl.program_id(1)
    nblk = pl.num_programs(1)

    @pl.when(i == 0)
    def _():
        if from_zero:
            stf[...] = jnp.zeros_like(stf)
            stb[...] = jnp.zeros_like(stb)
        else:
            for h in range(GLA_HEADS):
                stf[h] = sf0_ref[h].T
                stb[h] = sb0_ref[h].T

    _gla_direction(qkf_ref, vf_ref, bf_ref, stf, of_ref, True, qd_s, att_s, upd_s, dec_s)
    _gla_direction(qkb_ref, vb_ref, bb_ref, stb, ob_ref, False, qd_s, att_s, upd_s, dec_s)

    @pl.when(i == nblk - 1)
    def _():
        for h in range(GLA_HEADS):
            sf_ref[h] = stf[h].T
            sb_ref[h] = stb[h].T


def _gla_scan(qk, v, bc, sf0, sb0, b_off, nb, t_len):
    n = qk.shape[0]
    assert n % t_len == 0
    ns = n // t_len
    tb = _row_tile(t_len)
    nblk = t_len // tb
    qk3 = qk.reshape(ns, t_len, qk.shape[1])
    v3 = v.reshape(ns, t_len, v.shape[1])
    bc3 = bc.reshape(ns, t_len, bc.shape[1])
    h, dk, dv = GLA_HEADS, GLA_DK, GLA_DV
    hk = h * dk
    st_spec = pl.BlockSpec((None, h, dk, dv), lambda b, i: (b, 0, 0, 0))
    fwd = lambda b, i: (b_off + b, i, 0)
    bwd = lambda b, i: (b_off + b, nblk - 1 - i, 0)
    bwd_b = pl.BlockSpec((None, tb, hk), lambda b, i: (b_off + b, nblk - 1 - i, 1))
    shared = nblk == 1
    in_specs = [pl.BlockSpec((None, tb, 2 * hk), fwd), pl.BlockSpec((None, tb, h * dv), fwd),
                pl.BlockSpec((None, tb, hk), fwd)]
    operands = [qk3, v3, bc3]
    if not shared:
        in_specs += [pl.BlockSpec((None, tb, 2 * hk), bwd), pl.BlockSpec((None, tb, h * dv), bwd)]
        operands += [qk3, v3]
    in_specs += [bwd_b]
    operands += [bc3]
    from_zero = sf0 is None
    if not from_zero:
        in_specs += [st_spec, st_spec]
        operands += [sf0, sb0]
    return pl.pallas_call(
        functools.partial(_gla_scan_kernel, shared, from_zero),
        grid=(nb, nblk),
        in_specs=in_specs,
        out_specs=[pl.BlockSpec((None, tb, h * dv), lambda b, i: (b, i, 0)),
                   pl.BlockSpec((None, tb, h * dv), lambda b, i: (b, nblk - 1 - i, 0)),
                   st_spec, st_spec],
        out_shape=[jax.ShapeDtypeStruct((nb, t_len, h * dv), BF16),
                   jax.ShapeDtypeStruct((nb, t_len, h * dv), BF16),
                   jax.ShapeDtypeStruct((nb, h, dk, dv), F32),
                   jax.ShapeDtypeStruct((nb, h, dk, dv), F32)],
        scratch_shapes=[pltpu.VMEM((h, dv, dk), F32), pltpu.VMEM((h, dv, dk), F32),
                        pltpu.VMEM((tb, hk), BF16),
                        pltpu.VMEM((tb // GLA_CHUNK, h, GLA_CHUNK, GLA_CHUNK), BF16),
                        pltpu.VMEM((tb // GLA_CHUNK, h, dv, dk), F32),
                        pltpu.VMEM((max(tb // GLA_CHUNK, 8), hk), F32)],
        compiler_params=_params(("arbitrary", "arbitrary")),
        name="gla_scan",
    )(*operands)


GLA_OUT_PARTS = 4


def _gla_out_kernel(na, ofa_ref, oba_ref, ofb_ref, obb_ref, r_ref, h_ref, mod_ref, on_ref, wo_ref,
                    g2_ref, rw_ref, tri_ref, h_out, f_out, route_out, wt_out, cnt_out, run):
    i = pl.program_id(0)
    is_ctx = i < na

    @pl.when(i == 0)
    def _():
        run[...] = jnp.zeros_like(run)

    tm = tri_ref.shape[0]
    for part in range(h_ref.shape[0] // tm):
        rows = slice(part * tm, (part + 1) * tm)
        _gla_out_rows(rows, tm, is_ctx, ofa_ref, oba_ref, ofb_ref, obb_ref, r_ref, h_ref, mod_ref, on_ref,
                      wo_ref, g2_ref, rw_ref, tri_ref, h_out, f_out, route_out, wt_out, run)
    cnt_out[...] = run[...]


def _gla_out_rows(rows, tm, is_ctx, ofa_ref, oba_ref, ofb_ref, obb_ref, r_ref, h_ref, mod_ref, on_ref,
                  wo_ref, g2_ref, rw_ref, tri_ref, h_out, f_out, route_out, wt_out, run):
    o = (jnp.where(is_ctx, ofa_ref[rows, :], ofb_ref[rows, :]).astype(F32)
         + jnp.where(is_ctx, oba_ref[rows, :], obb_ref[rows, :]).astype(F32))
    dv = GLA_DV
    parts = []
    for hh in range(GLA_HEADS):
        oh = o[:, hh * dv:(hh + 1) * dv]
        parts.append(_rms(oh) * on_ref[...])
    on = jnp.concatenate(parts, axis=1)
    r = r_ref[rows, :].astype(F32)
    gated = (on * _silu(r)).astype(BF16)
    h = h_ref[rows, :] + mod_ref[2:3, :] * jnp.dot(gated, wo_ref[...], preferred_element_type=F32)
    h_out[rows, :] = h
    f = _adaln(h, g2_ref[...], mod_ref[3:4, :], mod_ref[4:5, :])
    f_out[rows, :] = _pack_halves(f)
    fh = f.astype(BF16)
    fl = (f - fh.astype(F32)).astype(BF16)
    logits = (jnp.dot(fh, rw_ref[0], preferred_element_type=F32)
              + jnp.dot(fl, rw_ref[0], preferred_element_type=F32)
              + jnp.dot(fh, rw_ref[1], preferred_element_type=F32))
    lane = lax.broadcasted_iota(I32, (tm, LANES), 1)
    lane_f = lane.astype(F32)
    logits = jnp.where(lane < N_EXPERTS, logits, -jnp.inf)
    m1 = jnp.max(logits, axis=-1, keepdims=True)
    i1 = jnp.min(jnp.where(logits == m1, lane_f, float(LANES)), axis=-1, keepdims=True)
    rest = jnp.where(lane_f == i1, -jnp.inf, logits)
    m2 = jnp.max(rest, axis=-1, keepdims=True)
    i2 = jnp.min(jnp.where(rest == m2, lane_f, float(LANES)), axis=-1, keepdims=True)
    e2 = jnp.exp(m2 - m1)
    w1 = 1.0 / (1.0 + e2)
    w2 = e2 / (1.0 + e2)
    wt_out[rows, :] = jnp.where(lane == 0, w1, jnp.where(lane == 1, w2, 0.0))
    sel1 = lane_f == i1
    sel2 = lane_f == i2
    picked = jnp.where(sel1, 1.0, jnp.where(sel2, 1.0, 0.0))
    before = jnp.dot(tri_ref[...], picked.astype(BF16), preferred_element_type=F32) + run[0:1, :]
    rank1 = jnp.sum(jnp.where(sel1, before, 0.0), axis=-1, keepdims=True)
    rank2 = jnp.sum(jnp.where(sel2, before, 0.0), axis=-1, keepdims=True)
    run[...] = run[...] + jnp.sum(picked, axis=0, keepdims=True)
    routing = jnp.where(lane == 0, i1, jnp.where(lane == 1, i2,
                        jnp.where(lane == 2, rank1, jnp.where(lane == 3, rank2, 0.0))))
    route_out[:, rows] = routing.T[0:8, :]


def _gla_out(of_p, ob_p, of_s, ob_s, r, h, n_p, t_len, mod, out_norm, wo, g2, router_w):
    n, d = h.shape
    tm = _row_tile(n_p, t_len, cap=1024)
    na = n_p // tm
    nt_s = t_len // tm
    mod_idx = lambda i: (jnp.where(i < na, 0, 1 + (i - na) // nt_s), 0, 0)
    row = lambda i: (i, 0)
    ctx_row = lambda i: (jnp.minimum(i, na - 1), 0)
    lat_row = lambda i: (jnp.maximum(i - na, 0), 0)
    tp = tm // GLA_OUT_PARTS
    tri = jnp.tril(jnp.ones((tp, tp), F32), -1).astype(BF16)
    return pl.pallas_call(
        functools.partial(_gla_out_kernel, na),
        grid=(n // tm,),
        in_specs=[
            pl.BlockSpec((tm, d), ctx_row),
            pl.BlockSpec((tm, d), ctx_row),
            pl.BlockSpec((tm, d), lat_row),
            pl.BlockSpec((tm, d), lat_row),
            pl.BlockSpec((tm, d), row),
            pl.BlockSpec((tm, d), row),
            pl.BlockSpec((None, 8, d), mod_idx),
            _const_spec((1, GLA_DV)),
            _const_spec((d, d)),
            _const_spec((1, d)),
            _const_spec((2, d, LANES)),
            _const_spec((tp, tp)),
        ],
        out_specs=[pl.BlockSpec((tm, d), row), pl.BlockSpec((tm, d // 2), row),
                   pl.BlockSpec((8, tm), lambda i: (0, i)), pl.BlockSpec((tm, LANES), row),
                   _const_spec((8, LANES))],
        out_shape=[jax.ShapeDtypeStruct((n, d), F32), jax.ShapeDtypeStruct((n, d // 2), jnp.uint32),
                   jax.ShapeDtypeStruct((8, n), F32), jax.ShapeDtypeStruct((n, LANES), F32),
                   jax.ShapeDtypeStruct((8, LANES), F32)],
        scratch_shapes=[pltpu.VMEM((8, LANES), F32)],
        compiler_params=_params(("arbitrary",)),
        name="gla_out_router",
    )(of_p, ob_p, of_s, ob_s, r, h, mod, out_norm, wo, g2, router_w, tri)


SC_CORES = 2
SC_SUBCORES = 16
SC_CHUNK = 64


def _sc_gather_rows(table, idx):
    nw = SC_CORES * SC_SUBCORES
    b = idx.shape[0]
    d = table.shape[1]
    assert b % (nw * SC_CHUNK) == 0
    per_w = b // nw
    n_chunks = per_w // SC_CHUNK
    mesh = plsc.VectorSubcoreMesh(core_axis_name="c", subcore_axis_name="s",
                                  num_cores=SC_CORES, num_subcores=SC_SUBCORES)

    assert n_chunks % 2 == 0

    def body(table_hbm, idx_hbm, out_hbm, idx_v, rows_v, gsem, wsem):
        wid = lax.axis_index("s") * SC_CORES + lax.axis_index("c")
        base = wid * per_w
        pltpu.sync_copy(idx_hbm.at[wid], idx_v)

        def gather(j, slot):
            return pltpu.make_async_copy(table_hbm.at[idx_v.at[j]], rows_v.at[slot], gsem.at[slot])

        def write(j, slot):
            return pltpu.make_async_copy(rows_v.at[slot], out_hbm.at[pl.ds(base + j * SC_CHUNK, SC_CHUNK)],
                                         wsem.at[slot])

        gather(0, 0).start()

        @pl.loop(0, n_chunks, step=2)
        def _(j):
            for slot in range(2):
                jj = j + slot
                gather(jj, slot).wait()

                @pl.when(jj + 1 < n_chunks)
                def _():
                    @pl.when(jj >= 1)
                    def _():
                        write(jj - 1, 1 - slot).wait()

                    gather(jj + 1, 1 - slot).start()

                write(jj, slot).start()

        write(n_chunks - 2, 0).wait()
        write(n_chunks - 1, 1).wait()

    return pl.kernel(
        body,
        out_type=jax.ShapeDtypeStruct((b, d), table.dtype),
        mesh=mesh,
        scratch_types=[pltpu.VMEM((n_chunks, SC_CHUNK), I32),
                       pltpu.VMEM((2, SC_CHUNK, d), table.dtype),
                       pltpu.SemaphoreType.DMA((2,)),
                       pltpu.SemaphoreType.DMA((2,))],
        name="sc_gather_rows",
    )(table, idx.reshape(nw, n_chunks, SC_CHUNK))


def _sc_scatter_rows(rows, idx2, p):
    nw = SC_CORES * SC_SUBCORES
    n, d = rows.shape
    assert n % (nw * SC_CHUNK) == 0
    per_w = n // nw
    n_chunks = per_w // SC_CHUNK
    mesh = plsc.VectorSubcoreMesh(core_axis_name="c", subcore_axis_name="s",
                                  num_cores=SC_CORES, num_subcores=SC_SUBCORES)

    assert n_chunks % 2 == 0

    def body(rows_hbm, idx0_hbm, idx1_hbm, out_hbm, idx0_v, idx1_v, rows_v, rsem, ssem):
        wid = lax.axis_index("s") * SC_CORES + lax.axis_index("c")
        base = wid * per_w
        pltpu.sync_copy(idx0_hbm.at[wid], idx0_v)
        pltpu.sync_copy(idx1_hbm.at[wid], idx1_v)

        def read(j, slot):
            return pltpu.make_async_copy(rows_hbm.at[pl.ds(base + j * SC_CHUNK, SC_CHUNK)], rows_v.at[slot],
                                         rsem.at[slot])

        read(0, 0).start()

        @pl.loop(0, n_chunks, step=2)
        def _(j):
            for slot in range(2):
                jj = j + slot
                read(jj, slot).wait()

                @pl.when(jj + 1 < n_chunks)
                def _():
                    read(jj + 1, 1 - slot).start()

                s0 = pltpu.make_async_copy(rows_v.at[slot], out_hbm.at[idx0_v.at[jj]], ssem.at[0])
                s1 = pltpu.make_async_copy(rows_v.at[slot], out_hbm.at[idx1_v.at[jj]], ssem.at[1])
                s0.start()
                s1.start()
                s0.wait()
                s1.wait()

    return pl.kernel(
        body,
        out_type=jax.ShapeDtypeStruct((p, d), rows.dtype),
        mesh=mesh,
        scratch_types=[pltpu.VMEM((n_chunks, SC_CHUNK), I32),
                       pltpu.VMEM((n_chunks, SC_CHUNK), I32),
                       pltpu.VMEM((2, SC_CHUNK, d), rows.dtype),
                       pltpu.SemaphoreType.DMA((2,)),
                       pltpu.SemaphoreType.DMA((2,))],
        name="sc_scatter_rows",
    )(rows, idx2[0].reshape(nw, n_chunks, SC_CHUNK), idx2[1].reshape(nw, n_chunks, SC_CHUNK))


EXPERT_SUB = 256


def _pack_halves(x):
    k = x.shape[1] // 2
    lo = lax.bitcast_convert_type(x[:, :k].astype(BF16).astype(F32), jnp.uint32)
    hi = lax.bitcast_convert_type(x[:, k:].astype(BF16).astype(F32), jnp.uint32)
    return (lo >> 16) | (hi & jnp.uint32(0xFFFF0000))


def _unpack_halves(w):
    lo = lax.bitcast_convert_type(w << 16, F32).astype(BF16)
    hi = lax.bitcast_convert_type(w & jnp.uint32(0xFFFF0000), F32).astype(BF16)
    return lo, hi


def _expert_kernel(te_ref, ts_ref, nu_ref, nv_ref, x_ref, wg_ref, wu_ref, wd_ref, y_ref, acc, xb):
    i = pl.program_id(0)
    f = pl.program_id(1)
    nf = pl.num_programs(1)
    tm = x_ref.shape[0]
    half = x_ref.shape[1]
    sub = EXPERT_SUB
    nvalid = nv_ref[i]

    @pl.when(f == 0)
    def _():
        acc[...] = jnp.zeros_like(acc)

    @pl.when((f == 0) & (nvalid > 0))
    def _():
        rid = lax.broadcasted_iota(I32, (tm, half), 0)
        lo, hi = _unpack_halves(x_ref[...])
        zero = jnp.zeros_like(lo)
        xb[:, 0:half] = jnp.where(rid < nvalid, lo, zero)
        xb[:, half:2 * half] = jnp.where(rid < nvalid, hi, zero)

    def compute(rows):
        x = xb[rows, :]
        hg = jnp.dot(x, wg_ref[...].astype(BF16), preferred_element_type=F32)
        hu = jnp.dot(x, wu_ref[...].astype(BF16), preferred_element_type=F32)
        act = (_silu(hg) * hu).astype(BF16)
        acc[rows, :] += jnp.dot(act, wd_ref[...].astype(BF16), preferred_element_type=F32)

    for k in range(1, tm // sub + 1):
        @pl.when((nvalid > (k - 1) * sub) & (nvalid <= k * sub))
        def _():
            compute(slice(0, k * sub))

    @pl.when(f == nf - 1)
    def _():
        y_ref[...] = _pack_halves(acc[...])


def _experts(x, tile_expert, tile_src, n_used, n_valid, wg, wu, wd, tm, tf):
    p, half = x.shape
    d = 2 * half
    ne, _, dff = wg.shape
    nf = dff // tf
    assert dff % tf == 0 and tm % EXPERT_SUB == 0

    def fidx(i, f, nu):
        return jnp.where(i < nu[0], f, nf - 1)

    grid_spec = pltpu.PrefetchScalarGridSpec(
        num_scalar_prefetch=4,
        grid=(p // tm, nf),
        in_specs=[
            pl.BlockSpec((tm, half), lambda i, f, te, ts, nu, nv: (ts[i], 0)),
            pl.BlockSpec((None, d, tf), lambda i, f, te, ts, nu, nv: (te[i], 0, fidx(i, f, nu))),
            pl.BlockSpec((None, d, tf), lambda i, f, te, ts, nu, nv: (te[i], 0, fidx(i, f, nu))),
            pl.BlockSpec((None, tf, d), lambda i, f, te, ts, nu, nv: (te[i], fidx(i, f, nu), 0)),
        ],
        out_specs=pl.BlockSpec((tm, half), lambda i, f, te, ts, nu, nv: (i, 0)),
        scratch_shapes=[pltpu.VMEM((tm, d), F32), pltpu.VMEM((tm, d), BF16)],
    )
    return pl.pallas_call(
        _expert_kernel,
        grid_spec=grid_spec,
        out_shape=jax.ShapeDtypeStruct((p, half), jnp.uint32),
        compiler_params=_params(("arbitrary", "arbitrary"), 60 * 1024 * 1024),
        name="moe_experts",
    )(tile_expert, tile_src, n_used, n_valid, x, wg, wu, wd)


def _combine_kernel(wt_ref, h_ref, mod_ref, y0_ref, y1_ref, o_ref):
    wt = wt_ref[...]
    half = y0_ref.shape[1]
    lo0, hi0 = _unpack_halves(y0_ref[...])
    lo1, hi1 = _unpack_halves(y1_ref[...])
    w0, w1 = wt[:, 0:1], wt[:, 1:2]
    h = h_ref[...]
    gate = mod_ref[5:6, :]
    o_ref[:, 0:half] = h[:, 0:half] + gate[:, 0:half] * (w0 * lo0.astype(F32) + w1 * lo1.astype(F32))
    o_ref[:, half:] = h[:, half:] + gate[:, half:] * (w0 * hi0.astype(F32) + w1 * hi1.astype(F32))


def _combine(yg, wt, h, mod, row_off, n_rows, n_mod, mod_off, tm):
    d = h.shape[1]
    nt = n_rows // tm
    toff = row_off // tm
    per_mod = n_rows // n_mod // tm
    return pl.pallas_call(
        _combine_kernel,
        grid=(nt,),
        in_specs=[
            pl.BlockSpec((tm, LANES), lambda i: (toff + i, 0)),
            pl.BlockSpec((tm, d), lambda i: (toff + i, 0)),
            pl.BlockSpec((None, 8, d), lambda i: (mod_off + i // per_mod, 0, 0)),
            pl.BlockSpec((None, tm, d // 2), lambda i: (0, i, 0)),
            pl.BlockSpec((None, tm, d // 2), lambda i: (1, i, 0)),
        ],
        out_specs=pl.BlockSpec((tm, d), lambda i: (i, 0)),
        out_shape=jax.ShapeDtypeStruct((n_rows, d), F32),
        compiler_params=_params(("arbitrary",)),
        name="moe_combine",
    )(wt, h, mod, yg, yg)


def _route(route, counts, n, tm):
    cnt = counts[0, :N_EXPERTS].astype(I32)
    padded = ((cnt + tm - 1) // tm) * tm
    gend = jnp.cumsum(padded)
    goff = gend - padded
    e = route[0:2].astype(I32)
    rank = route[2:4].astype(I32)
    onehot = e[:, :, None] == jnp.arange(N_EXPERTS, dtype=I32)[None, None, :]
    dest = jnp.sum(jnp.where(onehot, goff[None, None, :], 0), axis=-1) + rank
    p = 2 * n + N_EXPERTS * tm
    n_used = gend[-1] // tm
    tiles = jnp.arange(p // tm, dtype=I32)
    tile_src = jnp.minimum(tiles, n_used - 1)
    tile_expert = jnp.minimum(jnp.sum((gend[None, :] <= (tile_src * tm)[:, None]).astype(I32), axis=1),
                              N_EXPERTS - 1)
    used = goff[tile_expert] + cnt[tile_expert]
    n_valid = jnp.where(tiles < n_used, jnp.clip(used - tiles * tm, 0, tm), 0).astype(I32)
    return dest, p, tile_expert, tile_src, n_used.reshape(1).astype(I32), n_valid


def kernel(x_prompt, x_sample, cache_k, cache_v, state_fwd, state_bwd, c, c_ctx, ada_w, ada_b, norm1_g, norm2_g, attn_w_qkv, attn_q_norm, attn_k_norm, attn_sink, attn_w_o, gla_w_in, gla_gate_w1, gla_gate_w2, gla_gate_b, gla_out_norm, gla_w_o, ffn_w_gate, ffn_w_up, ffn_w_down, moe_router, moe_w_gate, moe_w_up, moe_w_down):
    bp, seq, d = x_prompt.shape
    db, t_len, _ = x_sample.shape
    n_p, n_s = bp * seq, db * t_len
    n = n_p + n_s
    xp = x_prompt.reshape(n_p, d)
    xs = x_sample.reshape(n_s, d)

    cond = jnp.concatenate([c_ctx[None, :], c], axis=0)
    assert cond.shape[0] <= 8
    cond_t = jnp.pad(cond, ((0, 8 - cond.shape[0]), (0, 0))).T
    mods = _modulation(cond_t, cond.shape[0], ada_w, ada_b)

    nk = N_KV_HEADS * HEAD_DIM
    qn = jnp.tile(attn_q_norm[0], N_HEADS)[None, :]
    kn = jnp.tile(attn_k_norm[0], N_KV_HEADS)[None, :]
    q, kt, vv, ck_new, cv_new = _qkv(xp, xs, t_len, mods[0], norm1_g[0][None, :],
                                     attn_w_qkv[0].astype(BF16), qn, kn)
    wo0 = attn_w_o[0].astype(BF16)
    sink = attn_sink[0]
    hp = _ctx_attention(sink, q, kt, vv, xp, seq, mods[0], wo0)
    ck = cache_k[:, 0].astype(BF16)
    cv = cache_v[:, 0].astype(BF16)
    ckt = jnp.concatenate([ck, ck], axis=-1).transpose(0, 2, 3, 1)
    cvv = jnp.concatenate([cv, cv], axis=-1).reshape(db, cv.shape[1], N_KV_HEADS * LANES)
    hs = _lat_attention(sink, q, kt, vv, ckt, cvv, xs, n_p, t_len, mods[0], wo0)
    new_k = ck_new.reshape(bp, 1, seq, N_KV_HEADS, HEAD_DIM)
    new_v = cv_new.reshape(bp, 1, seq, N_KV_HEADS, HEAD_DIM)
    h = _ffn(hp, hs, t_len, mods[0], norm2_g[0][None, :], ffn_w_gate[0].astype(BF16),
             ffn_w_up[0].astype(BF16), ffn_w_down[0].astype(BF16))

    hk = GLA_HEADS * GLA_DK
    rank = GLA_GATE_RANK
    w1 = jnp.concatenate([gla_gate_w1[0, 0], gla_gate_w1[0, 1]], axis=1)
    w1 = jnp.pad(w1, ((0, 0), (0, LANES - 2 * rank))).astype(BF16)
    w2 = jnp.zeros((LANES, 2 * hk), F32)
    w2 = w2.at[0:rank, 0:hk].set(gla_gate_w2[0, 0]).at[rank:2 * rank, hk:].set(gla_gate_w2[0, 1]).astype(BF16)
    gate_b = gla_gate_b[0].reshape(1, 2 * hk)
    qk, v, r, bc = _gla_in(h, n_p, t_len, mods[1], norm1_g[1][None, :], gla_w_in[0].astype(BF16),
                           w1, w2, gate_b)
    of_p, ob_p, new_sf, new_sb = _gla_scan(qk, v, bc, None, None, 0, bp, seq)
    of_s, ob_s, _, _ = _gla_scan(qk, v, bc, state_fwd[:, 0], state_bwd[:, 0], n_p // t_len, db, t_len)
    rw = jnp.pad(moe_router[0], ((0, 0), (0, LANES - N_EXPERTS)))
    rw_hi = rw.astype(BF16)
    rw_lo = (rw - rw_hi.astype(F32)).astype(BF16)
    h, f, route, wt, counts = _gla_out(
        of_p.reshape(n_p, d), ob_p.reshape(n_p, d), of_s.reshape(n_s, d), ob_s.reshape(n_s, d), r, h,
        n_p, t_len, mods[1], gla_out_norm[0][None, :], gla_w_o[0].astype(BF16), norm2_g[1][None, :],
        jnp.stack([rw_hi, rw_lo]))

    tm_e = 1024
    dest_k, p_rows, tile_expert, tile_src, n_used, n_valid = _route(route, counts, n, tm_e)
    xg = _sc_scatter_rows(f, dest_k, p_rows)
    y = _experts(xg, tile_expert, tile_src, n_used, n_valid, moe_w_gate[0], moe_w_up[0],
                 moe_w_down[0], tm_e, 512)
    yg_p = _sc_gather_rows(y, dest_k[:, :n_p].reshape(2 * n_p)).reshape(2, n_p, d // 2)
    yg_s = _sc_gather_rows(y, dest_k[:, n_p:].reshape(2 * n_s)).reshape(2, n_s, d // 2)
    tm_c = _row_tile(n_p, t_len)
    y_p = _combine(yg_p, wt, h, mods[1], 0, n_p, 1, 0, tm_c)
    y_s = _combine(yg_s, wt, h, mods[1], n_p, n_s, db, 1, tm_c)

    return (y_p.reshape(bp, seq, d), y_s.reshape(db, t_len, d), new_k, new_v,
            new_sf[:, None], new_sb[:, None])
```

```python
import functools
import math

import numpy as np
import jax
import jax.numpy as jnp
from jax import lax
from jax.experimental import pallas as pl
from jax.experimental.pallas import tpu as pltpu
from jax.experimental.pallas import tpu_sc as plsc

F32 = jnp.float32
BF16 = jnp.bfloat16
I32 = jnp.int32

D_MODEL = 1024
N_HEADS = 16
N_KV_HEADS = 4
HEAD_DIM = 64
GRID_W = 64
WINDOW = 128
ATTN_BLOCK = 128
ROPE_THETA = 10000.0
GLA_HEADS = 4
GLA_DK = 128
GLA_DV = 256
GLA_GATE_RANK = 16
GLA_GATE_TAU = 16.0
GLA_CHUNK = 64
N_EXPERTS = 8
NORM_EPS = 1e-6
NEG_INF = -1e30

LANES = 128
VMEM_LIMIT = 56 * 1024 * 1024


def _params(sem, vmem=VMEM_LIMIT):
    return pltpu.CompilerParams(dimension_semantics=sem, vmem_limit_bytes=vmem)


def _row_tile(*counts, cap=512):
    t = cap
    while any(c % t for c in counts):
        t //= 2
    assert t >= 8
    return t


def _rms(x):
    return x * lax.rsqrt(jnp.mean(x * x, axis=-1, keepdims=True) + NORM_EPS)


def _adaln(x, g, shift, scale):
    return _rms(x) * (g * (1.0 + scale)) + shift


def _silu(x):
    return x * jax.nn.sigmoid(x)


def _const_spec(shape):
    nd = len(shape)
    return pl.BlockSpec(shape, lambda *_: (0,) * nd)


def _mod_kernel(n_cond, ct_ref, w_ref, b_ref, o_ref):
    ct = ct_ref[...]
    s = _silu(ct)
    w = w_ref[...]
    rows = [jnp.sum(w * s[:, r:r + 1], axis=0, keepdims=True) for r in range(n_cond)]
    rows += [jnp.zeros_like(rows[0])] * (8 - n_cond)
    o_ref[...] = jnp.concatenate(rows, axis=0) + b_ref[...]


def _modulation(cond_t, n_cond, ada_w, ada_b):
    depth, d, n6 = ada_w.shape
    tn = 2048
    out = pl.pallas_call(
        functools.partial(_mod_kernel, n_cond),
        grid=(depth, n6 // tn),
        in_specs=[
            pl.BlockSpec((d, 8), lambda l, j: (0, 0)),
            pl.BlockSpec((None, d, tn), lambda l, j: (l, 0, j)),
            pl.BlockSpec((None, 1, tn), lambda l, j: (l, 0, j)),
        ],
        out_specs=pl.BlockSpec((None, 8, tn), lambda l, j: (l, 0, j)),
        out_shape=jax.ShapeDtypeStruct((depth, 8, n6), F32),
        compiler_params=_params(("arbitrary", "arbitrary")),
        name="modulation",
    )(cond_t, ada_w, ada_b.reshape(depth, 1, n6))
    m = out[:, :n_cond].reshape(depth, n_cond, 6, d)
    return jnp.pad(m, ((0, 0), (0, 0), (0, 2), (0, 0)))


LOG2E = math.log2(math.e)


def _dup_half(k2, half):
    lane = lax.broadcasted_iota(I32, k2.shape, 1)
    lo = lane < HEAD_DIM
    r = pltpu.roll(k2, HEAD_DIM, 1)
    return jnp.where(lo, k2, r) if half == 0 else jnp.where(lo, r, k2)


QKV_PARTS = 2


def _qkv_kernel(na, xa_ref, xb_ref, mod_ref, g_ref, w_ref, qn_ref, kn_ref, bd_ref,
                cos_ref, sin_ref, q_out, kt_out, vv_out, ck_out, cv_out):
    i = pl.program_id(0)
    is_ctx = i < na
    tm = xa_ref.shape[0] // QKV_PARTS
    for part in range(QKV_PARTS):
        _qkv_rows(slice(part * tm, (part + 1) * tm), tm, is_ctx, xa_ref, xb_ref, mod_ref, g_ref, w_ref,
                  qn_ref, kn_ref, bd_ref, cos_ref, sin_ref, q_out, kt_out, vv_out, ck_out, cv_out)


def _qkv_rows(rows, tm, is_ctx, xa_ref, xb_ref, mod_ref, g_ref, w_ref, qn_ref, kn_ref, bd_ref,
              cos_ref, sin_ref, q_out, kt_out, vv_out, ck_out, cv_out):
    x = jnp.where(is_ctx, xa_ref[rows, :], xb_ref[rows, :])
    a = _adaln(x, g_ref[...], mod_ref[0:1, :], mod_ref[1:2, :])
    y = jnp.dot(a.astype(BF16), w_ref[...], preferred_element_type=F32)
    cos = jnp.where(is_ctx, 1.0, cos_ref[rows, :])
    sin = jnp.where(is_ctx, 0.0, sin_ref[rows, :])
    lane = lax.broadcasted_iota(I32, (tm, LANES), 1)
    first16 = (lane % 32) < 16

    def norm_rope(z, wt):
        ss = jnp.dot((z * z).astype(BF16), bd_ref[...], preferred_element_type=F32)
        zn = z * lax.rsqrt(ss * (1.0 / HEAD_DIM) + NORM_EPS) * wt
        outs = []
        for c in range(2):
            t = zn[:, c * LANES:(c + 1) * LANES]
            partner = jnp.where(first16, pltpu.roll(t, LANES - 16, 1), pltpu.roll(t, 16, 1))
            outs.append(t * cos + partner * sin)
        return jnp.concatenate(outs, axis=1)

    nq = N_HEADS * HEAD_DIM
    nk = N_KV_HEADS * HEAD_DIM
    scale = HEAD_DIM ** -0.5 * LOG2E
    for s in range(nq // 256):
        sl = slice(s * 256, (s + 1) * 256)
        q_out[rows, sl] = (norm_rope(y[:, sl], qn_ref[:, sl]) * scale).astype(BF16)
    k = norm_rope(y[:, nq:nq + nk], kn_ref[...])
    v = y[:, nq + nk:nq + 2 * nk]
    for g in range(N_KV_HEADS):
        c = g // 2
        kk = _dup_half(k[:, c * LANES:(c + 1) * LANES], g % 2)
        kt_out[g, :, rows] = kk.T.astype(BF16)
        vv_out[rows, g * LANES:(g + 1) * LANES] = _dup_half(v[:, c * LANES:(c + 1) * LANES], g % 2).astype(BF16)

    @pl.when(is_ctx)
    def _():
        ck_out[rows, :] = k
        cv_out[rows, :] = v


def _rope_tables(t_len):
    pos = np.arange(t_len)
    row = (pos // GRID_W).astype(np.float32)[:, None]
    col = (pos % GRID_W).astype(np.float32)[:, None]
    half = HEAD_DIM // 2
    inv = (np.float32(ROPE_THETA) ** (-np.arange(0, half, 2, dtype=np.float32) / np.float32(half)))[None, :]
    ar, ac = row * inv, col * inv
    cos = np.concatenate([np.cos(ar), np.cos(ar), np.cos(ac), np.cos(ac)], axis=1)
    sin = np.concatenate([-np.sin(ar), np.sin(ar), -np.sin(ac), np.sin(ac)], axis=1)
    return (jnp.asarray(np.tile(cos, (1, 2)), dtype=F32), jnp.asarray(np.tile(sin, (1, 2)), dtype=F32))


def _qkv(xp, xs, t_len, mod, g, w, qn, kn):
    n_p, n_s = xp.shape[0], xs.shape[0]
    tm = _row_tile(n_p, t_len, cap=1024)
    na, nb = n_p // tm, n_s // tm
    nt_s = t_len // tm
    n = n_p + n_s
    d = D_MODEL
    nq, nk = N_HEADS * HEAD_DIM, N_KV_HEADS * HEAD_DIM
    cos, sin = _rope_tables(t_len)
    eye4 = jnp.kron(jnp.eye(4, dtype=F32), jnp.ones((HEAD_DIM, HEAD_DIM), F32)).astype(BF16)
    mod_idx = lambda i: (jnp.where(i < na, 0, 1 + (i - na) // nt_s), 0, 0)
    pos_idx = lambda i: (jnp.maximum(i - na, 0) % nt_s, 0)
    return pl.pallas_call(
        functools.partial(_qkv_kernel, na),
        grid=(na + nb,),
        in_specs=[
            pl.BlockSpec((tm, d), lambda i: (jnp.minimum(i, na - 1), 0)),
            pl.BlockSpec((tm, d), lambda i: (jnp.maximum(i - na, 0), 0)),
            pl.BlockSpec((None, 8, d), mod_idx),
            _const_spec((1, d)),
            _const_spec((d, nq + 2 * nk)),
            _const_spec((1, nq)),
            _const_spec((1, nk)),
            _const_spec((256, 256)),
            pl.BlockSpec((tm, LANES), pos_idx),
            pl.BlockSpec((tm, LANES), pos_idx),
        ],
        out_specs=[pl.BlockSpec((tm, nq), lambda i: (i, 0)),
                   pl.BlockSpec((N_KV_HEADS, LANES, tm), lambda i: (0, 0, i)),
                   pl.BlockSpec((tm, N_KV_HEADS * LANES), lambda i: (i, 0)),
                   pl.BlockSpec((tm, nk), lambda i: (jnp.minimum(i, na - 1), 0)),
                   pl.BlockSpec((tm, nk), lambda i: (jnp.minimum(i, na - 1), 0))],
        out_shape=[jax.ShapeDtypeStruct((n, nq), BF16),
                   jax.ShapeDtypeStruct((N_KV_HEADS, LANES, n), BF16),
                   jax.ShapeDtypeStruct((n, N_KV_HEADS * LANES), BF16),
                   jax.ShapeDtypeStruct((n_p, nk), F32),
                   jax.ShapeDtypeStruct((n_p, nk), F32)],
        compiler_params=_params(("arbitrary",)),
        name="qkv",
    )(xp, xs, mod, g, w, qn, kn, eye4, cos, sin)


def _attn_group(q2a, q2b, kt, vv, bias, n_bias, sinks):
    tq = q2a.shape[0]
    nk = kt.shape[1]
    lo = lax.broadcasted_iota(I32, (tq, LANES), 1) < HEAD_DIM
    top = lax.broadcasted_iota(I32, kt.shape, 0) < HEAD_DIM
    zk = jnp.zeros_like(kt)
    kbd = jnp.concatenate([jnp.where(top, kt, zk), jnp.where(top, zk, kt)], axis=1)
    left = lax.broadcasted_iota(I32, vv.shape, 1) < HEAD_DIM
    zv = jnp.zeros_like(vv)
    vbd = jnp.concatenate([jnp.where(left, vv, zv), jnp.where(left, zv, vv)], axis=0)
    s = jnp.dot(jnp.concatenate([q2a, q2b], axis=0), kbd, preferred_element_type=F32)
    p_rows, scales = [], []
    for t in range(2):
        ph, rinv = [], []
        for hf in range(2):
            sh = s[t * tq:(t + 1) * tq, hf * nk:(hf + 1) * nk]
            if bias is not None:
                sh = jnp.concatenate([sh[:, :n_bias] + bias, sh[:, n_bias:]], axis=1)
            sk = sinks[2 * t + hf]
            m = jnp.maximum(jnp.max(sh, axis=-1, keepdims=True), sk)
            p = jnp.exp2(sh - m)
            rinv.append(1.0 / (jnp.sum(p, axis=-1, keepdims=True) + jnp.exp2(sk - m)))
            ph.append(p.astype(BF16))
        p_rows.append(jnp.concatenate(ph, axis=1))
        scales.append(jnp.where(lo, rinv[0], rinv[1]))
    o = jnp.dot(jnp.concatenate(p_rows, axis=0), vbd, preferred_element_type=F32)
    return o[0:tq] * scales[0], o[tq:2 * tq] * scales[1]


def _ctx_attn_kernel(sink_ref, q_ref, kt_ref, vv_ref, x_ref, mod_ref, wo_ref, o_ref, osc):
    for g in range(N_KV_HEADS):
        q2a = q_ref[:, (2 * g) * LANES:(2 * g + 1) * LANES]
        q2b = q_ref[:, (2 * g + 1) * LANES:(2 * g + 2) * LANES]
        sinks = [sink_ref[4 * g + j] * LOG2E for j in range(4)]
        oa, ob = _attn_group(q2a, q2b, kt_ref[g], vv_ref[:, g * LANES:(g + 1) * LANES], None, 0, sinks)
        osc[:, (2 * g) * LANES:(2 * g + 1) * LANES] = oa.astype(BF16)
        osc[:, (2 * g + 1) * LANES:(2 * g + 2) * LANES] = ob.astype(BF16)
    att = jnp.dot(osc[...], wo_ref[...], preferred_element_type=F32)
    o_ref[...] = x_ref[...] + mod_ref[2:3, :] * att


def _ctx_attention(sink, q, kt, vv, xp, seq, mod, wo):
    n_p, d = xp.shape
    nb = n_p // seq
    return pl.pallas_call(
        _ctx_attn_kernel,
        grid=(nb,),
        in_specs=[
            pl.BlockSpec(memory_space=pltpu.SMEM),
            pl.BlockSpec((seq, d), lambda b: (b, 0)),
            pl.BlockSpec((N_KV_HEADS, LANES, seq), lambda b: (0, 0, b)),
            pl.BlockSpec((seq, N_KV_HEADS * LANES), lambda b: (b, 0)),
            pl.BlockSpec((seq, d), lambda b: (b, 0)),
            pl.BlockSpec((None, 8, d), lambda b: (0, 0, 0)),
            _const_spec((d, d)),
        ],
        out_specs=pl.BlockSpec((seq, d), lambda b: (b, 0)),
        out_shape=jax.ShapeDtypeStruct((n_p, d), F32),
        scratch_shapes=[pltpu.VMEM((seq, d), BF16)],
        compiler_params=_params(("arbitrary",)),
        name="ctx_attention",
    )(sink, q, kt, vv, xp, mod, wo)


LAT_QB = 4


def _lat_attn_kernel(t_len, sink_ref, q_ref, ktp_ref, kto_ref, ktn_ref, vvp_ref, vvo_ref, vvn_ref,
                     ckt_ref, cvv_ref, x_ref, mod_ref, wo_ref, o_ref, osc):
    step = pl.program_id(1)
    tq = ATTN_BLOCK
    nloc = 3 * ATTN_BLOCK
    qi = lax.broadcasted_iota(I32, (tq, nloc), 0)
    kj = lax.broadcasted_iota(I32, (tq, nloc), 1)
    in_window = jnp.abs(qi + tq - kj) <= WINDOW
    for u in range(LAT_QB):
        n = step * LAT_QB + u
        kpos = (n - 1) * tq + kj
        bias = jnp.where(in_window & (kpos >= 0) & (kpos < t_len), 0.0, NEG_INF)
        rows = slice(u * tq, (u + 1) * tq)
        for g in range(N_KV_HEADS):
            vs = slice(g * LANES, (g + 1) * LANES)
            kts = [ktp_ref[g]] + [kto_ref[g, :, j * tq:(j + 1) * tq] for j in range(LAT_QB)] + [ktn_ref[g]]
            vvs = [vvp_ref[:, vs]] + [vvo_ref[j * tq:(j + 1) * tq, vs] for j in range(LAT_QB)] + [vvn_ref[:, vs]]
            kt = jnp.concatenate(kts[u:u + 3] + [ckt_ref[g]], axis=1)
            vv = jnp.concatenate(vvs[u:u + 3] + [cvv_ref[:, vs]], axis=0)
            q2a = q_ref[rows, (2 * g) * LANES:(2 * g + 1) * LANES]
            q2b = q_ref[rows, (2 * g + 1) * LANES:(2 * g + 2) * LANES]
            sinks = [sink_ref[4 * g + j] * LOG2E for j in range(4)]
            oa, ob = _attn_group(q2a, q2b, kt, vv, bias, nloc, sinks)
            osc[rows, (2 * g) * LANES:(2 * g + 1) * LANES] = oa.astype(BF16)
            osc[rows, (2 * g + 1) * LANES:(2 * g + 2) * LANES] = ob.astype(BF16)
    att = jnp.dot(osc[...], wo_ref[...], preferred_element_type=F32)
    o_ref[...] = x_ref[...] + mod_ref[2:3, :] * att


def _lat_attention(sink, q, kt, vv, ckt, cvv, xs, n_p, t_len, mod, wo):
    n_s, d = xs.shape
    db = n_s // t_len
    tq = ATTN_BLOCK
    ts = LAT_QB * tq
    nblk = t_len // tq
    nstep = t_len // ts
    assert n_p % ts == 0 and t_len % ts == 0
    off = n_p // tq
    npast = ckt.shape[3]
    g4 = N_KV_HEADS

    def nbr(delta):
        return lambda b, m: off + b * nblk + jnp.clip(m * LAT_QB + delta, 0, nblk - 1)

    own = lambda b, m: n_p // ts + b * nstep + m
    return pl.pallas_call(
        functools.partial(_lat_attn_kernel, t_len),
        grid=(db, nstep),
        in_specs=[
            pl.BlockSpec(memory_space=pltpu.SMEM),
            pl.BlockSpec((ts, d), lambda b, m: (own(b, m), 0)),
            pl.BlockSpec((g4, LANES, tq), lambda b, m: (0, 0, nbr(-1)(b, m))),
            pl.BlockSpec((g4, LANES, ts), lambda b, m: (0, 0, own(b, m))),
            pl.BlockSpec((g4, LANES, tq), lambda b, m: (0, 0, nbr(LAT_QB)(b, m))),
            pl.BlockSpec((tq, g4 * LANES), lambda b, m: (nbr(-1)(b, m), 0)),
            pl.BlockSpec((ts, g4 * LANES), lambda b, m: (own(b, m), 0)),
            pl.BlockSpec((tq, g4 * LANES), lambda b, m: (nbr(LAT_QB)(b, m), 0)),
            pl.BlockSpec((None, g4, LANES, npast), lambda b, m: (b, 0, 0, 0)),
            pl.BlockSpec((None, npast, g4 * LANES), lambda b, m: (b, 0, 0)),
            pl.BlockSpec((ts, d), lambda b, m: (b * nstep + m, 0)),
            pl.BlockSpec((None, 8, d), lambda b, m: (1 + b, 0, 0)),
            _const_spec((d, d)),
        ],
        out_specs=pl.BlockSpec((ts, d), lambda b, m: (b * nstep + m, 0)),
        out_shape=jax.ShapeDtypeStruct((n_s, d), F32),
        scratch_shapes=[pltpu.VMEM((ts, d), BF16)],
        compiler_params=_params(("arbitrary", "arbitrary")),
        name="lat_attention",
    )(sink, q, kt, kt, kt, vv, vv, vv, ckt, cvv, xs, mod, wo)


def _ffn_kernel(na, fc, xa_ref, xb_ref, mod_ref, g_ref, wg_ref, wu_ref, wd_ref, o_ref, acc):
    i = pl.program_id(0)
    x = jnp.where(i < na, xa_ref[...], xb_ref[...])
    a = _adaln(x, g_ref[...], mod_ref[3:4, :], mod_ref[4:5, :]).astype(BF16)
    nf = wg_ref.shape[1] // fc
    for f in range(nf):
        sl = slice(f * fc, (f + 1) * fc)
        hg = jnp.dot(a, wg_ref[:, sl], preferred_element_type=F32)
        hu = jnp.dot(a, wu_ref[:, sl], preferred_element_type=F32)
        act = (_silu(hg) * hu).astype(BF16)
        part = jnp.dot(act, wd_ref[sl, :], preferred_element_type=F32)
        if f == 0:
            acc[...] = part
        else:
            acc[...] += part
    o_ref[...] = x + mod_ref[5:6, :] * acc[...]


def _ffn(hp, hs, t_len, mod, g, wg, wu, wd):
    n_p, n_s = hp.shape[0], hs.shape[0]
    d, dff = wg.shape
    tm = _row_tile(n_p, t_len)
    na, nb = n_p // tm, n_s // tm
    nt_s = t_len // tm
    fc = 256
    assert dff % fc == 0
    mod_idx = lambda i: (jnp.where(i < na, 0, 1 + (i - na) // nt_s), 0, 0)
    return pl.pallas_call(
        functools.partial(_ffn_kernel, na, fc),
        grid=(na + nb,),
        in_specs=[
            pl.BlockSpec((tm, d), lambda i: (jnp.minimum(i, na - 1), 0)),
            pl.BlockSpec((tm, d), lambda i: (jnp.maximum(i - na, 0), 0)),
            pl.BlockSpec((None, 8, d), mod_idx),
            _const_spec((1, d)),
            _const_spec((d, dff)),
            _const_spec((d, dff)),
            _const_spec((dff, d)),
        ],
        out_specs=pl.BlockSpec((tm, d), lambda i: (i, 0)),
        out_shape=jax.ShapeDtypeStruct((n_p + n_s, d), F32),
        scratch_shapes=[pltpu.VMEM((tm, d), F32)],
        compiler_params=_params(("arbitrary",)),
        name="ffn",
    )(hp, hs, mod, g, wg, wu, wd)


def _split2(x):
    hi = x.astype(BF16)
    lo = (x - hi.astype(F32)).astype(BF16)
    return hi, lo


def _gla_in_kernel(nt_s, x_ref, mod_ref, g_ref, w_ref, w1_ref, w2_ref, gb_ref, tri_ref,
                   qk_out, v_out, r_out, b_out):
    x = x_ref[...]
    tm = x.shape[0]
    a = _adaln(x, g_ref[...], mod_ref[0:1, :], mod_ref[1:2, :]).astype(BF16)
    hk = GLA_HEADS * GLA_DK
    hv = GLA_HEADS * GLA_DV
    q = jnp.dot(a, w_ref[:, 0:hk], preferred_element_type=F32)
    qk_out[:, 0:hk] = (q * (GLA_DK ** -0.5)).astype(BF16)
    qk_out[:, hk:2 * hk] = jnp.dot(a, w_ref[:, hk:2 * hk], preferred_element_type=F32).astype(BF16)
    v_out[...] = jnp.dot(a, w_ref[:, 2 * hk:2 * hk + hv], preferred_element_type=F32).astype(BF16)
    r_out[...] = jnp.dot(a, w_ref[:, 2 * hk + hv:2 * hk + 2 * hv], preferred_element_type=F32).astype(BF16)
    z1 = jnp.dot(a, w1_ref[...], preferred_element_type=F32)
    z = jnp.dot(z1.astype(BF16), w2_ref[...], preferred_element_type=F32) + gb_ref[...]
    gate = (jnp.minimum(z, 0.0) - jnp.log(1.0 + jnp.exp(-jnp.abs(z)))) * (1.0 / GLA_GATE_TAU)
    c = GLA_CHUNK
    tri = tri_ref[...]
    for j in range(tm // c):
        rows = slice(j * c, (j + 1) * c)
        for dr in range(2):
            cols = slice(dr * hk, (dr + 1) * hk)
            hi, lo = _split2(gate[rows, cols])
            b_out[rows, cols] = (jnp.dot(tri[dr], hi, preferred_element_type=F32)
                                 + jnp.dot(tri[dr], lo, preferred_element_type=F32))


def _gla_in(h, n_p, t_len, mod, g, w_in, w1, w2, gate_b):
    n, d = h.shape
    tm = _row_tile(n_p, t_len, cap=1024)
    na = n_p // tm
    nt_s = t_len // tm
    hk, hv = GLA_HEADS * GLA_DK, GLA_HEADS * GLA_DV
    c = GLA_CHUNK
    lower = jnp.tril(jnp.ones((c, c), F32))
    tri = jnp.stack([lower, lower.T]).astype(BF16)
    mod_idx = lambda i: (jnp.where(i < na, 0, 1 + (i - na) // nt_s), 0, 0)
    row = lambda i: (i, 0)
    return pl.pallas_call(
        functools.partial(_gla_in_kernel, nt_s),
        grid=(n // tm,),
        in_specs=[
            pl.BlockSpec((tm, d), row),
            pl.BlockSpec((None, 8, d), mod_idx),
            _const_spec((1, d)),
            _const_spec(w_in.shape),
            _const_spec(w1.shape),
            _const_spec(w2.shape),
            _const_spec((1, 2 * hk)),
            _const_spec((2, c, c)),
        ],
        out_specs=[pl.BlockSpec((tm, 2 * hk), row), pl.BlockSpec((tm, hv), row),
                   pl.BlockSpec((tm, hv), row), pl.BlockSpec((tm, 2 * hk), row)],
        out_shape=[jax.ShapeDtypeStruct((n, 2 * hk), BF16), jax.ShapeDtypeStruct((n, hv), BF16),
                   jax.ShapeDtypeStruct((n, hv), BF16), jax.ShapeDtypeStruct((n, 2 * hk), F32)],
        compiler_params=_params(("arbitrary",)),
        name="gla_in",
    )(h, mod, g, w_in, w1, w2, gate_b, tri)


def _gla_direction(qk_ref, v_ref, b_ref, st, o_ref, forward, qd_s, att_s, upd_s, dec_s):
    c = GLA_CHUNK
    ncb = qk_ref.shape[0] // c
    hk = GLA_HEADS * GLA_DK
    dk, dv = GLA_DK, GLA_DV
    ri = lax.broadcasted_iota(I32, (c, c), 0)
    ci_ = lax.broadcasted_iota(I32, (c, c), 1)
    keep = (ci_ <= ri) if forward else (ci_ >= ri)

    for ch in range(ncb):
        rows = slice(ch * c, (ch + 1) * c)
        b = b_ref[rows, :]
        if forward:
            b_end, b_mid = b[c - 1:c, :], b[c // 2 - 1:c // 2, :]
        else:
            b_end, b_mid = b[0:1, :], b[c // 2:c // 2 + 1, :]
        qa = qk_ref[rows, 0:hk] * jnp.exp(b - b_mid)
        ka = qk_ref[rows, hk:2 * hk] * jnp.exp(b_mid - b)
        qd_s[rows, :] = (qa * jnp.exp(b_mid)).astype(BF16)
        kd = (ka * jnp.exp(b_end - b_mid)).astype(BF16)
        dec_s[ch:ch + 1, :] = jnp.exp(b_end)
        qab, kab = qa.astype(BF16), ka.astype(BF16)
        for h in range(GLA_HEADS):
            ks = slice(h * dk, (h + 1) * dk)
            vs = slice(h * dv, (h + 1) * dv)
            att = lax.dot_general(qab[:, ks], kab[:, ks], (((1,), (1,)), ((), ())),
                                  preferred_element_type=F32)
            att_s[ch, h] = jnp.where(keep, att, 0.0).astype(BF16)
            upd_s[ch, h] = lax.dot_general(v_ref[rows, vs], kd[:, ks], (((0,), (0,)), ((), ())),
                                           preferred_element_type=F32)

    for ch in (range(ncb) if forward else reversed(range(ncb))):
        rows = slice(ch * c, (ch + 1) * c)
        for h in range(GLA_HEADS):
            ks = slice(h * dk, (h + 1) * dk)
            vs = slice(h * dv, (h + 1) * dv)
            s = st[h]
            o = lax.dot_general(qd_s[rows, ks], s.astype(BF16), (((1,), (1,)), ((), ())),
                                preferred_element_type=F32)
            o = o + jnp.dot(att_s[ch, h], v_ref[rows, vs], preferred_element_type=F32)
            o_ref[rows, vs] = o.astype(o_ref.dtype)
            st[h] = s * dec_s[ch:ch + 1, ks] + upd_s[ch, h]


def _gla_scan_kernel(shared, from_zero, *refs):
    if shared:
        qkf_ref, vf_ref, bf_ref, bb_ref = refs[:4]
        qkb_ref, vb_ref = qkf_ref, vf_ref
        rest = refs[4:]
    else:
        qkf_ref, vf_ref, bf_ref, qkb_ref, vb_ref, bb_ref = refs[:6]
        rest = refs[6:]
    if not from_zero:
        sf0_ref, sb0_ref = rest[:2]
        rest = rest[2:]
    of_ref, ob_ref, sf_ref, sb_ref, stf, stb, qd_s, att_s, upd_s, dec_s = rest
    i = pl.program_id(1)
    nblk = pl.num_programs(1)

    @pl.when(i == 0)
    def _():
        if from_zero:
            stf[...] = jnp.zeros_like(stf)
            stb[...] = jnp.zeros_like(stb)
        else:
            for h in range(GLA_HEADS):
                stf[h] = sf0_ref[h].T
                stb[h] = sb0_ref[h].T

    _gla_direction(qkf_ref, vf_ref, bf_ref, stf, of_ref, True, qd_s, att_s, upd_s, dec_s)
    _gla_direction(qkb_ref, vb_ref, bb_ref, stb, ob_ref, False, qd_s, att_s, upd_s, dec_s)

    @pl.when(i == nblk - 1)
    def _():
        for h in range(GLA_HEADS):
            sf_ref[h] = stf[h].T
            sb_ref[h] = stb[h].T


def _gla_scan(qk, v, bc, sf0, sb0, b_off, nb, t_len):
    n = qk.shape[0]
    assert n % t_len == 0
    ns = n // t_len
    tb = _row_tile(t_len)
    nblk = t_len // tb
    qk3 = qk.reshape(ns, t_len, qk.shape[1])
    v3 = v.reshape(ns, t_len, v.shape[1])
    bc3 = bc.reshape(ns, t_len, bc.shape[1])
    h, dk, dv = GLA_HEADS, GLA_DK, GLA_DV
    hk = h * dk
    st_spec = pl.BlockSpec((None, h, dk, dv), lambda b, i: (b, 0, 0, 0))
    fwd = lambda b, i: (b_off + b, i, 0)
    bwd = lambda b, i: (b_off + b, nblk - 1 - i, 0)
    bwd_b = pl.BlockSpec((None, tb, hk), lambda b, i: (b_off + b, nblk - 1 - i, 1))
    shared = nblk == 1
    in_specs = [pl.BlockSpec((None, tb, 2 * hk), fwd), pl.BlockSpec((None, tb, h * dv), fwd),
                pl.BlockSpec((None, tb, hk), fwd)]
    operands = [qk3, v3, bc3]
    if not shared:
        in_specs += [pl.BlockSpec((None, tb, 2 * hk), bwd), pl.BlockSpec((None, tb, h * dv), bwd)]
        operands += [qk3, v3]
    in_specs += [bwd_b]
    operands += [bc3]
    from_zero = sf0 is None
    if not from_zero:
        in_specs += [st_spec, st_spec]
        operands += [sf0, sb0]
    return pl.pallas_call(
        functools.partial(_gla_scan_kernel, shared, from_zero),
        grid=(nb, nblk),
        in_specs=in_specs,
        out_specs=[pl.BlockSpec((None, tb, h * dv), lambda b, i: (b, i, 0)),
                   pl.BlockSpec((None, tb, h * dv), lambda b, i: (b, nblk - 1 - i, 0)),
                   st_spec, st_spec],
        out_shape=[jax.ShapeDtypeStruct((nb, t_len, h * dv), BF16),
                   jax.ShapeDtypeStruct((nb, t_len, h * dv), BF16),
                   jax.ShapeDtypeStruct((nb, h, dk, dv), F32),
                   jax.ShapeDtypeStruct((nb, h, dk, dv), F32)],
        scratch_shapes=[pltpu.VMEM((h, dv, dk), F32), pltpu.VMEM((h, dv, dk), F32),
                        pltpu.VMEM((tb, hk), BF16),
                        pltpu.VMEM((tb // GLA_CHUNK, h, GLA_CHUNK, GLA_CHUNK), BF16),
                        pltpu.VMEM((tb // GLA_CHUNK, h, dv, dk), F32),
                        pltpu.VMEM((max(tb // GLA_CHUNK, 8), hk), F32)],
        compiler_params=_params(("arbitrary", "arbitrary")),
        name="gla_scan",
    )(*operands)


GLA_OUT_PARTS = 4


def _gla_out_kernel(na, ofa_ref, oba_ref, ofb_ref, obb_ref, r_ref, h_ref, mod_ref, on_ref, wo_ref,
                    g2_ref, rw_ref, tri_ref, h_out, f_out, route_out, wt_out, cnt_out, run):
    i = pl.program_id(0)
    is_ctx = i < na

    @pl.when(i == 0)
    def _():
        run[...] = jnp.zeros_like(run)

    tm = tri_ref.shape[0]
    for part in range(h_ref.shape[0] // tm):
        rows = slice(part * tm, (part + 1) * tm)
        _gla_out_rows(rows, tm, is_ctx, ofa_ref, oba_ref, ofb_ref, obb_ref, r_ref, h_ref, mod_ref, on_ref,
                      wo_ref, g2_ref, rw_ref, tri_ref, h_out, f_out, route_out, wt_out, run)
    cnt_out[...] = run[...]


def _gla_out_rows(rows, tm, is_ctx, ofa_ref, oba_ref, ofb_ref, obb_ref, r_ref, h_ref, mod_ref, on_ref,
                  wo_ref, g2_ref, rw_ref, tri_ref, h_out, f_out, route_out, wt_out, run):
    o = (jnp.where(is_ctx, ofa_ref[rows, :], ofb_ref[rows, :]).astype(F32)
         + jnp.where(is_ctx, oba_ref[rows, :], obb_ref[rows, :]).astype(F32))
    dv = GLA_DV
    parts = []
    for hh in range(GLA_HEADS):
        oh = o[:, hh * dv:(hh + 1) * dv]
        parts.append(_rms(oh) * on_ref[...])
    on = jnp.concatenate(parts, axis=1)
    r = r_ref[rows, :].astype(F32)
    gated = (on * _silu(r)).astype(BF16)
    h = h_ref[rows, :] + mod_ref[2:3, :] * jnp.dot(gated, wo_ref[...], preferred_element_type=F32)
    h_out[rows, :] = h
    f = _adaln(h, g2_ref[...], mod_ref[3:4, :], mod_ref[4:5, :])
    f_out[rows, :] = _pack_halves(f)
    fh = f.astype(BF16)
    fl = (f - fh.astype(F32)).astype(BF16)
    logits = (jnp.dot(fh, rw_ref[0], preferred_element_type=F32)
              + jnp.dot(fl, rw_ref[0], preferred_element_type=F32)
              + jnp.dot(fh, rw_ref[1], preferred_element_type=F32))
    lane = lax.broadcasted_iota(I32, (tm, LANES), 1)
    lane_f = lane.astype(F32)
    logits = jnp.where(lane < N_EXPERTS, logits, -jnp.inf)
    m1 = jnp.max(logits, axis=-1, keepdims=True)
    i1 = jnp.min(jnp.where(logits == m1, lane_f, float(LANES)), axis=-1, keepdims=True)
    rest = jnp.where(lane_f == i1, -jnp.inf, logits)
    m2 = jnp.max(rest, axis=-1, keepdims=True)
    i2 = jnp.min(jnp.where(rest == m2, lane_f, float(LANES)), axis=-1, keepdims=True)
    e2 = jnp.exp(m2 - m1)
    w1 = 1.0 / (1.0 + e2)
    w2 = e2 / (1.0 + e2)
    wt_out[rows, :] = jnp.where(lane == 0, w1, jnp.where(lane == 1, w2, 0.0))
    sel1 = lane_f == i1
    sel2 = lane_f == i2
    picked = jnp.where(sel1, 1.0, jnp.where(sel2, 1.0, 0.0))
    before = jnp.dot(tri_ref[...], picked.astype(BF16), preferred_element_type=F32) + run[0:1, :]
    rank1 = jnp.sum(jnp.where(sel1, before, 0.0), axis=-1, keepdims=True)
    rank2 = jnp.sum(jnp.where(sel2, before, 0.0), axis=-1, keepdims=True)
    run[...] = run[...] + jnp.sum(picked, axis=0, keepdims=True)
    routing = jnp.where(lane == 0, i1, jnp.where(lane == 1, i2,
                        jnp.where(lane == 2, rank1, jnp.where(lane == 3, rank2, 0.0))))
    route_out[:, rows] = routing.T[0:8, :]


def _gla_out(of_p, ob_p, of_s, ob_s, r, h, n_p, t_len, mod, out_norm, wo, g2, router_w):
    n, d = h.shape
    tm = _row_tile(n_p, t_len, cap=1024)
    na = n_p // tm
    nt_s = t_len // tm
    mod_idx = lambda i: (jnp.where(i < na, 0, 1 + (i - na) // nt_s), 0, 0)
    row = lambda i: (i, 0)
    ctx_row = lambda i: (jnp.minimum(i, na - 1), 0)
    lat_row = lambda i: (jnp.maximum(i - na, 0), 0)
    tp = tm // GLA_OUT_PARTS
    tri = jnp.tril(jnp.ones((tp, tp), F32), -1).astype(BF16)
    return pl.pallas_call(
        functools.partial(_gla_out_kernel, na),
        grid=(n // tm,),
        in_specs=[
            pl.BlockSpec((tm, d), ctx_row),
            pl.BlockSpec((tm, d), ctx_row),
            pl.BlockSpec((tm, d), lat_row),
            pl.BlockSpec((tm, d), lat_row),
            pl.BlockSpec((tm, d), row),
            pl.BlockSpec((tm, d), row),
            pl.BlockSpec((None, 8, d), mod_idx),
            _const_spec((1, GLA_DV)),
            _const_spec((d, d)),
            _const_spec((1, d)),
            _const_spec((2, d, LANES)),
            _const_spec((tp, tp)),
        ],
        out_specs=[pl.BlockSpec((tm, d), row), pl.BlockSpec((tm, d // 2), row),
                   pl.BlockSpec((8, tm), lambda i: (0, i)), pl.BlockSpec((tm, LANES), row),
                   _const_spec((8, LANES))],
        out_shape=[jax.ShapeDtypeStruct((n, d), F32), jax.ShapeDtypeStruct((n, d // 2), jnp.uint32),
                   jax.ShapeDtypeStruct((8, n), F32), jax.ShapeDtypeStruct((n, LANES), F32),
                   jax.ShapeDtypeStruct((8, LANES), F32)],
        scratch_shapes=[pltpu.VMEM((8, LANES), F32)],
        compiler_params=_params(("arbitrary",)),
        name="gla_out_router",
    )(of_p, ob_p, of_s, ob_s, r, h, mod, out_norm, wo, g2, router_w, tri)


SC_CORES = 2
SC_SUBCORES = 16
SC_CHUNK = 64


def _sc_gather_rows(table, idx):
    nw = SC_CORES * SC_SUBCORES
    b = idx.shape[0]
    d = table.shape[1]
    assert b % (nw * SC_CHUNK) == 0
    per_w = b // nw
    n_chunks = per_w // SC_CHUNK
    mesh = plsc.VectorSubcoreMesh(core_axis_name="c", subcore_axis_name="s",
                                  num_cores=SC_CORES, num_subcores=SC_SUBCORES)

    assert n_chunks % 2 == 0

    def body(table_hbm, idx_hbm, out_hbm, idx_v, rows_v, gsem, wsem):
        wid = lax.axis_index("s") * SC_CORES + lax.axis_index("c")
        base = wid * per_w
        pltpu.sync_copy(idx_hbm.at[wid], idx_v)

        def gather(j, slot):
            return pltpu.make_async_copy(table_hbm.at[idx_v.at[j]], rows_v.at[slot], gsem.at[slot])

        def write(j, slot):
            return pltpu.make_async_copy(rows_v.at[slot], out_hbm.at[pl.ds(base + j * SC_CHUNK, SC_CHUNK)],
                                         wsem.at[slot])

        gather(0, 0).start()

        @pl.loop(0, n_chunks, step=2)
        def _(j):
            for slot in range(2):
                jj = j + slot
                gather(jj, slot).wait()

                @pl.when(jj + 1 < n_chunks)
                def _():
                    @pl.when(jj >= 1)
                    def _():
                        write(jj - 1, 1 - slot).wait()

                    gather(jj + 1, 1 - slot).start()

                write(jj, slot).start()

        write(n_chunks - 2, 0).wait()
        write(n_chunks - 1, 1).wait()

    return pl.kernel(
        body,
        out_type=jax.ShapeDtypeStruct((b, d), table.dtype),
        mesh=mesh,
        scratch_types=[pltpu.VMEM((n_chunks, SC_CHUNK), I32),
                       pltpu.VMEM((2, SC_CHUNK, d), table.dtype),
                       pltpu.SemaphoreType.DMA((2,)),
                       pltpu.SemaphoreType.DMA((2,))],
        name="sc_gather_rows",
    )(table, idx.reshape(nw, n_chunks, SC_CHUNK))


def _sc_scatter_rows(rows, idx2, p):
    nw = SC_CORES * SC_SUBCORES
    n, d = rows.shape
    assert n % (nw * SC_CHUNK) == 0
    per_w = n // nw
    n_chunks = per_w // SC_CHUNK
    mesh = plsc.VectorSubcoreMesh(core_axis_name="c", subcore_axis_name="s",
                                  num_cores=SC_CORES, num_subcores=SC_SUBCORES)

    assert n_chunks % 2 == 0

    def body(rows_hbm, idx0_hbm, idx1_hbm, out_hbm, idx0_v, idx1_v, rows_v, rsem, ssem):
        wid = lax.axis_index("s") * SC_CORES + lax.axis_index("c")
        base = wid * per_w
        pltpu.sync_copy(idx0_hbm.at[wid], idx0_v)
        pltpu.sync_copy(idx1_hbm.at[wid], idx1_v)

        def read(j, slot):
            return pltpu.make_async_copy(rows_hbm.at[pl.ds(base + j * SC_CHUNK, SC_CHUNK)], rows_v.at[slot],
                                         rsem.at[slot])

        read(0, 0).start()

        @pl.loop(0, n_chunks, step=2)
        def _(j):
            for slot in range(2):
                jj = j + slot
                read(jj, slot).wait()

                @pl.when(jj + 1 < n_chunks)
                def _():
                    read(jj + 1, 1 - slot).start()

                s0 = pltpu.make_async_copy(rows_v.at[slot], out_hbm.at[idx0_v.at[jj]], ssem.at[0])
                s1 = pltpu.make_async_copy(rows_v.at[slot], out_hbm.at[idx1_v.at[jj]], ssem.at[1])
                s0.start()
                s1.start()
                s0.wait()
                s1.wait()

    return pl.kernel(
        body,
        out_type=jax.ShapeDtypeStruct((p, d), rows.dtype),
        mesh=mesh,
        scratch_types=[pltpu.VMEM((n_chunks, SC_CHUNK), I32),
                       pltpu.VMEM((n_chunks, SC_CHUNK), I32),
                       pltpu.VMEM((2, SC_CHUNK, d), rows.dtype),
                       pltpu.SemaphoreType.DMA((2,)),
                       pltpu.SemaphoreType.DMA((2,))],
        name="sc_scatter_rows",
    )(rows, idx2[0].reshape(nw, n_chunks, SC_CHUNK), idx2[1].reshape(nw, n_chunks, SC_CHUNK))


EXPERT_SUB = 256


def _pack_halves(x):
    k = x.shape[1] // 2
    lo = lax.bitcast_convert_type(x[:, :k].astype(BF16).astype(F32), jnp.uint32)
    hi = lax.bitcast_convert_type(x[:, k:].astype(BF16).astype(F32), jnp.uint32)
    return (lo >> 16) | (hi & jnp.uint32(0xFFFF0000))


def _unpack_halves(w):
    lo = lax.bitcast_convert_type(w << 16, F32).astype(BF16)
    hi = lax.bitcast_convert_type(w & jnp.uint32(0xFFFF0000), F32).astype(BF16)
    return lo, hi


def _expert_kernel(te_ref, ts_ref, nu_ref, nv_ref, x_ref, wg_ref, wu_ref, wd_ref, y_ref, acc, xb):
    i = pl.program_id(0)
    f = pl.program_id(1)
    nf = pl.num_programs(1)
    tm = x_ref.shape[0]
    half = x_ref.shape[1]
    sub = EXPERT_SUB
    nvalid = nv_ref[i]

    @pl.when(f == 0)
    def _():
        acc[...] = jnp.zeros_like(acc)

    @pl.when((f == 0) & (nvalid > 0))
    def _():
        rid = lax.broadcasted_iota(I32, (tm, half), 0)
        lo, hi = _unpack_halves(x_ref[...])
        zero = jnp.zeros_like(lo)
        xb[:, 0:half] = jnp.where(rid < nvalid, lo, zero)
        xb[:, half:2 * half] = jnp.where(rid < nvalid, hi, zero)

    def compute(rows):
        x = xb[rows, :]
        hg = jnp.dot(x, wg_ref[...].astype(BF16), preferred_element_type=F32)
        hu = jnp.dot(x, wu_ref[...].astype(BF16), preferred_element_type=F32)
        act = (_silu(hg) * hu).astype(BF16)
        acc[rows, :] += jnp.dot(act, wd_ref[...].astype(BF16), preferred_element_type=F32)

    for k in range(1, tm // sub + 1):
        @pl.when((nvalid > (k - 1) * sub) & (nvalid <= k * sub))
        def _():
            compute(slice(0, k * sub))

    @pl.when(f == nf - 1)
    def _():
        y_ref[...] = _pack_halves(acc[...])


def _experts(x, tile_expert, tile_src, n_used, n_valid, wg, wu, wd, tm, tf):
    p, half = x.shape
    d = 2 * half
    ne, _, dff = wg.shape
    nf = dff // tf
    assert dff % tf == 0 and tm % EXPERT_SUB == 0

    def fidx(i, f, nu):
        return jnp.where(i < nu[0], f, nf - 1)

    grid_spec = pltpu.PrefetchScalarGridSpec(
        num_scalar_prefetch=4,
        grid=(p // tm, nf),
        in_specs=[
            pl.BlockSpec((tm, half), lambda i, f, te, ts, nu, nv: (ts[i], 0)),
            pl.BlockSpec((None, d, tf), lambda i, f, te, ts, nu, nv: (te[i], 0, fidx(i, f, nu))),
            pl.BlockSpec((None, d, tf), lambda i, f, te, ts, nu, nv: (te[i], 0, fidx(i, f, nu))),
            pl.BlockSpec((None, tf, d), lambda i, f, te, ts, nu, nv: (te[i], fidx(i, f, nu), 0)),
        ],
        out_specs=pl.BlockSpec((tm, half), lambda i, f, te, ts, nu, nv: (i, 0)),
        scratch_shapes=[pltpu.VMEM((tm, d), F32), pltpu.VMEM((tm, d), BF16)],
    )
    return pl.pallas_call(
        _expert_kernel,
        grid_spec=grid_spec,
        out_shape=jax.ShapeDtypeStruct((p, half), jnp.uint32),
        compiler_params=_params(("arbitrary", "arbitrary"), 60 * 1024 * 1024),
        name="moe_experts",
    )(tile_expert, tile_src, n_used, n_valid, x, wg, wu, wd)


def _combine_kernel(wt_ref, h_ref, mod_ref, y0_ref, y1_ref, o_ref):
    wt = wt_ref[...]
    half = y0_ref.shape[1]
    lo0, hi0 = _unpack_halves(y0_ref[...])
    lo1, hi1 = _unpack_halves(y1_ref[...])
    w0, w1 = wt[:, 0:1], wt[:, 1:2]
    h = h_ref[...]
    gate = mod_ref[5:6, :]
    o_ref[:, 0:half] = h[:, 0:half] + gate[:, 0:half] * (w0 * lo0.astype(F32) + w1 * lo1.astype(F32))
    o_ref[:, half:] = h[:, half:] + gate[:, half:] * (w0 * hi0.astype(F32) + w1 * hi1.astype(F32))


def _combine(yg, wt, h, mod, row_off, n_rows, n_mod, mod_off, tm):
    d = h.shape[1]
    nt = n_rows // tm
    toff = row_off // tm
    per_mod = n_rows // n_mod // tm
    return pl.pallas_call(
        _combine_kernel,
        grid=(nt,),
        in_specs=[
            pl.BlockSpec((tm, LANES), lambda i: (toff + i, 0)),
            pl.BlockSpec((tm, d), lambda i: (toff + i, 0)),
            pl.BlockSpec((None, 8, d), lambda i: (mod_off + i // per_mod, 0, 0)),
            pl.BlockSpec((None, tm, d // 2), lambda i: (0, i, 0)),
            pl.BlockSpec((None, tm, d // 2), lambda i: (1, i, 0)),
        ],
        out_specs=pl.BlockSpec((tm, d), lambda i: (i, 0)),
        out_shape=jax.ShapeDtypeStruct((n_rows, d), F32),
        compiler_params=_params(("arbitrary",)),
        name="moe_combine",
    )(wt, h, mod, yg, yg)


def _route(route, counts, n, tm):
    cnt = counts[0, :N_EXPERTS].astype(I32)
    padded = ((cnt + tm - 1) // tm) * tm
    gend = jnp.cumsum(padded)
    goff = gend - padded
    e = route[0:2].astype(I32)
    rank = route[2:4].astype(I32)
    onehot = e[:, :, None] == jnp.arange(N_EXPERTS, dtype=I32)[None, None, :]
    dest = jnp.sum(jnp.where(onehot, goff[None, None, :], 0), axis=-1) + rank
    p = 2 * n + N_EXPERTS * tm
    n_used = gend[-1] // tm
    tiles = jnp.arange(p // tm, dtype=I32)
    tile_src = jnp.minimum(tiles, n_used - 1)
    tile_expert = jnp.minimum(jnp.sum((gend[None, :] <= (tile_src * tm)[:, None]).astype(I32), axis=1),
                              N_EXPERTS - 1)
    used = goff[tile_expert] + cnt[tile_expert]
    n_valid = jnp.where(tiles < n_used, jnp.clip(used - tiles * tm, 0, tm), 0).astype(I32)
    return dest, p, tile_expert, tile_src, n_used.reshape(1).astype(I32), n_valid


def kernel(x_prompt, x_sample, cache_k, cache_v, state_fwd, state_bwd, c, c_ctx, ada_w, ada_b, norm1_g, norm2_g, attn_w_qkv, attn_q_norm, attn_k_norm, attn_sink, attn_w_o, gla_w_in, gla_gate_w1, gla_gate_w2, gla_gate_b, gla_out_norm, gla_w_o, ffn_w_gate, ffn_w_up, ffn_w_down, moe_router, moe_w_gate, moe_w_up, moe_w_down):
    bp, seq, d = x_prompt.shape
    db, t_len, _ = x_sample.shape
    n_p, n_s = bp * seq, db * t_len
    n = n_p + n_s
    xp = x_prompt.reshape(n_p, d)
    xs = x_sample.reshape(n_s, d)

    cond = jnp.concatenate([c_ctx[None, :], c], axis=0)
    assert cond.shape[0] <= 8
    cond_t = jnp.pad(cond, ((0, 8 - cond.shape[0]), (0, 0))).T
    mods = _modulation(cond_t, cond.shape[0], ada_w, ada_b)

    nk = N_KV_HEADS * HEAD_DIM
    qn = jnp.tile(attn_q_norm[0], N_HEADS)[None, :]
    kn = jnp.tile(attn_k_norm[0], N_KV_HEADS)[None, :]
    q, kt, vv, ck_new, cv_new = _qkv(xp, xs, t_len, mods[0], norm1_g[0][None, :],
                                     attn_w_qkv[0].astype(BF16), qn, kn)
    wo0 = attn_w_o[0].astype(BF16)
    sink = attn_sink[0]
    hp = _ctx_attention(sink, q, kt, vv, xp, seq, mods[0], wo0)
    ck = cache_k[:, 0].astype(BF16)
    cv = cache_v[:, 0].astype(BF16)
    ckt = jnp.concatenate([ck, ck], axis=-1).transpose(0, 2, 3, 1)
    cvv = jnp.concatenate([cv, cv], axis=-1).reshape(db, cv.shape[1], N_KV_HEADS * LANES)
    hs = _lat_attention(sink, q, kt, vv, ckt, cvv, xs, n_p, t_len, mods[0], wo0)
    new_k = ck_new.reshape(bp, 1, seq, N_KV_HEADS, HEAD_DIM)
    new_v = cv_new.reshape(bp, 1, seq, N_KV_HEADS, HEAD_DIM)
    h = _ffn(hp, hs, t_len, mods[0], norm2_g[0][None, :], ffn_w_gate[0].astype(BF16),
             ffn_w_up[0].astype(BF16), ffn_w_down[0].astype(BF16))

    hk = GLA_HEADS * GLA_DK
    rank = GLA_GATE_RANK
    w1 = jnp.concatenate([gla_gate_w1[0, 0], gla_gate_w1[0, 1]], axis=1)
    w1 = jnp.pad(w1, ((0, 0), (0, LANES - 2 * rank))).astype(BF16)
    w2 = jnp.zeros((LANES, 2 * hk), F32)
    w2 = w2.at[0:rank, 0:hk].set(gla_gate_w2[0, 0]).at[rank:2 * rank, hk:].set(gla_gate_w2[0, 1]).astype(BF16)
    gate_b = gla_gate_b[0].reshape(1, 2 * hk)
    qk, v, r, bc = _gla_in(h, n_p, t_len, mods[1], norm1_g[1][None, :], gla_w_in[0].astype(BF16),
                           w1, w2, gate_b)
    of_p, ob_p, new_sf, new_sb = _gla_scan(qk, v, bc, None, None, 0, bp, seq)
    of_s, ob_s, _, _ = _gla_scan(qk, v, bc, state_fwd[:, 0], state_bwd[:, 0], n_p // t_len, db, t_len)
    rw = jnp.pad(moe_router[0], ((0, 0), (0, LANES - N_EXPERTS)))
    rw_hi = rw.astype(BF16)
    rw_lo = (rw - rw_hi.astype(F32)).astype(BF16)
    h, f, route, wt, counts = _gla_out(
        of_p.reshape(n_p, d), ob_p.reshape(n_p, d), of_s.reshape(n_s, d), ob_s.reshape(n_s, d), r, h,
        n_p, t_len, mods[1], gla_out_norm[0][None, :], gla_w_o[0].astype(BF16), norm2_g[1][None, :],
        jnp.stack([rw_hi, rw_lo]))

    tm_e = 1024
    dest_k, p_rows, tile_expert, tile_src, n_used, n_valid = _route(route, counts, n, tm_e)
    xg = _sc_scatter_rows(f, dest_k, p_rows)
    xg, new_k = lax.optimization_barrier((xg, new_k))
    y = _experts(xg, tile_expert, tile_src, n_used, n_valid, moe_w_gate[0], moe_w_up[0],
                 moe_w_down[0], tm_e, 512)
    yg_p = _sc_gather_rows(y, dest_k[:, :n_p].reshape(2 * n_p)).reshape(2, n_p, d // 2)
    yg_s = _sc_gather_rows(y, dest_k[:, n_p:].reshape(2 * n_s)).reshape(2, n_s, d // 2)
    tm_c = _row_tile(n_p, t_len)
    y_p = _combine(yg_p, wt, h, mods[1], 0, n_p, 1, 0, tm_c)
    y_s = _combine(yg_s, wt, h, mods[1], n_p, n_s, db, 1, tm_c)

    return (y_p.reshape(bp, seq, d), y_s.reshape(db, t_len, d), new_k, new_v,
            new_sf[:, None], new_sb[:, None])
```

```python
import functools
import math

import numpy as np
import jax
import jax.numpy as jnp
from jax import lax
from jax.experimental import pallas as pl
from jax.experimental.pallas import tpu as pltpu
from jax.experimental.pallas import tpu_sc as plsc

F32 = jnp.float32
BF16 = jnp.bfloat16
I32 = jnp.int32

D_MODEL = 1024
N_HEADS = 16
N_KV_HEADS = 4
HEAD_DIM = 64
GRID_W = 64
WINDOW = 128
ATTN_BLOCK = 128
ROPE_THETA = 10000.0
GLA_HEADS = 4
GLA_DK = 128
GLA_DV = 256
GLA_GATE_RANK = 16
GLA_GATE_TAU = 16.0
GLA_CHUNK = 64
N_EXPERTS = 8
NORM_EPS = 1e-6
NEG_INF = -1e30

LANES = 128
VMEM_LIMIT = 56 * 1024 * 1024


def _params(sem, vmem=VMEM_LIMIT):
    return pltpu.CompilerParams(dimension_semantics=sem, vmem_limit_bytes=vmem)


def _row_tile(*counts, cap=512):
    t = cap
    while any(c % t for c in counts):
        t //= 2
    assert t >= 8
    return t


def _rms(x):
    return x * lax.rsqrt(jnp.mean(x * x, axis=-1, keepdims=True) + NORM_EPS)


def _adaln(x, g, shift, scale):
    return _rms(x) * (g * (1.0 + scale)) + shift


def _silu(x):
    return x * jax.nn.sigmoid(x)


def _const_spec(shape):
    nd = len(shape)
    return pl.BlockSpec(shape, lambda *_: (0,) * nd)


def _mod_kernel(n_cond, ct_ref, w_ref, b_ref, o_ref):
    ct = ct_ref[...]
    s = _silu(ct)
    w = w_ref[...]
    rows = [jnp.sum(w * s[:, r:r + 1], axis=0, keepdims=True) for r in range(n_cond)]
    rows += [jnp.zeros_like(rows[0])] * (8 - n_cond)
    o_ref[...] = jnp.concatenate(rows, axis=0) + b_ref[...]


def _modulation(cond_t, n_cond, ada_w, ada_b):
    depth, d, n6 = ada_w.shape
    tn = 2048
    out = pl.pallas_call(
        functools.partial(_mod_kernel, n_cond),
        grid=(depth, n6 // tn),
        in_specs=[
            pl.BlockSpec((d, 8), lambda l, j: (0, 0)),
            pl.BlockSpec((None, d, tn), lambda l, j: (l, 0, j)),
            pl.BlockSpec((None, 1, tn), lambda l, j: (l, 0, j)),
        ],
        out_specs=pl.BlockSpec((None, 8, tn), lambda l, j: (l, 0, j)),
        out_shape=jax.ShapeDtypeStruct((depth, 8, n6), F32),
        compiler_params=_params(("arbitrary", "arbitrary")),
        name="modulation",
    )(cond_t, ada_w, ada_b.reshape(depth, 1, n6))
    m = out[:, :n_cond].reshape(depth, n_cond, 6, d)
    return jnp.pad(m, ((0, 0), (0, 0), (0, 2), (0, 0)))


LOG2E = math.log2(math.e)


def _dup_half(k2, half):
    lane = lax.broadcasted_iota(I32, k2.shape, 1)
    lo = lane < HEAD_DIM
    r = pltpu.roll(k2, HEAD_DIM, 1)
    return jnp.where(lo, k2, r) if half == 0 else jnp.where(lo, r, k2)


QKV_PARTS = 2


def _qkv_kernel(na, xa_ref, xb_ref, mod_ref, g_ref, w_ref, qn_ref, kn_ref, bd_ref,
                cos_ref, sin_ref, q_out, kt_out, vv_out, ck_out, cv_out):
    i = pl.program_id(0)
    is_ctx = i < na
    tm = xa_ref.shape[0] // QKV_PARTS
    for part in range(QKV_PARTS):
        _qkv_rows(slice(part * tm, (part + 1) * tm), tm, is_ctx, xa_ref, xb_ref, mod_ref, g_ref, w_ref,
                  qn_ref, kn_ref, bd_ref, cos_ref, sin_ref, q_out, kt_out, vv_out, ck_out, cv_out)


def _qkv_rows(rows, tm, is_ctx, xa_ref, xb_ref, mod_ref, g_ref, w_ref, qn_ref, kn_ref, bd_ref,
              cos_ref, sin_ref, q_out, kt_out, vv_out, ck_out, cv_out):
    x = jnp.where(is_ctx, xa_ref[rows, :], xb_ref[rows, :])
    a = _adaln(x, g_ref[...], mod_ref[0:1, :], mod_ref[1:2, :])
    y = jnp.dot(a.astype(BF16), w_ref[...], preferred_element_type=F32)
    cos = jnp.where(is_ctx, 1.0, cos_ref[rows, :])
    sin = jnp.where(is_ctx, 0.0, sin_ref[rows, :])
    lane = lax.broadcasted_iota(I32, (tm, LANES), 1)
    first16 = (lane % 32) < 16

    def norm_rope(z, wt):
        ss = jnp.dot((z * z).astype(BF16), bd_ref[...], preferred_element_type=F32)
        zn = z * lax.rsqrt(ss * (1.0 / HEAD_DIM) + NORM_EPS) * wt
        outs = []
        for c in range(2):
            t = zn[:, c * LANES:(c + 1) * LANES]
            partner = jnp.where(first16, pltpu.roll(t, LANES - 16, 1), pltpu.roll(t, 16, 1))
            outs.append(t * cos + partner * sin)
        return jnp.concatenate(outs, axis=1)

    nq = N_HEADS * HEAD_DIM
    nk = N_KV_HEADS * HEAD_DIM
    scale = HEAD_DIM ** -0.5 * LOG2E
    for s in range(nq // 256):
        sl = slice(s * 256, (s + 1) * 256)
        q_out[rows, sl] = (norm_rope(y[:, sl], qn_ref[:, sl]) * scale).astype(BF16)
    k = norm_rope(y[:, nq:nq + nk], kn_ref[...])
    v = y[:, nq + nk:nq + 2 * nk]
    for g in range(N_KV_HEADS):
        c = g // 2
        kk = _dup_half(k[:, c * LANES:(c + 1) * LANES], g % 2)
        kt_out[g, :, rows] = kk.T.astype(BF16)
        vv_out[rows, g * LANES:(g + 1) * LANES] = _dup_half(v[:, c * LANES:(c + 1) * LANES], g % 2).astype(BF16)

    @pl.when(is_ctx)
    def _():
        ck_out[rows, :] = k
        cv_out[rows, :] = v


def _rope_tables(t_len):
    pos = np.arange(t_len)
    row = (pos // GRID_W).astype(np.float32)[:, None]
    col = (pos % GRID_W).astype(np.float32)[:, None]
    half = HEAD_DIM // 2
    inv = (np.float32(ROPE_THETA) ** (-np.arange(0, half, 2, dtype=np.float32) / np.float32(half)))[None, :]
    ar, ac = row * inv, col * inv
    cos = np.concatenate([np.cos(ar), np.cos(ar), np.cos(ac), np.cos(ac)], axis=1)
    sin = np.concatenate([-np.sin(ar), np.sin(ar), -np.sin(ac), np.sin(ac)], axis=1)
    return (jnp.asarray(np.tile(cos, (1, 2)), dtype=F32), jnp.asarray(np.tile(sin, (1, 2)), dtype=F32))


def _qkv(xp, xs, t_len, mod, g, w, qn, kn):
    n_p, n_s = xp.shape[0], xs.shape[0]
    tm = _row_tile(n_p, t_len, cap=1024)
    na, nb = n_p // tm, n_s // tm
    nt_s = t_len // tm
    n = n_p + n_s
    d = D_MODEL
    nq, nk = N_HEADS * HEAD_DIM, N_KV_HEADS * HEAD_DIM
    cos, sin = _rope_tables(t_len)
    eye4 = jnp.kron(jnp.eye(4, dtype=F32), jnp.ones((HEAD_DIM, HEAD_DIM), F32)).astype(BF16)
    mod_idx = lambda i: (jnp.where(i < na, 0, 1 + (i - na) // nt_s), 0, 0)
    pos_idx = lambda i: (jnp.maximum(i - na, 0) % nt_s, 0)
    return pl.pallas_call(
        functools.partial(_qkv_kernel, na),
        grid=(na + nb,),
        in_specs=[
            pl.BlockSpec((tm, d), lambda i: (jnp.minimum(i, na - 1), 0)),
            pl.BlockSpec((tm, d), lambda i: (jnp.maximum(i - na, 0), 0)),
            pl.BlockSpec((None, 8, d), mod_idx),
            _const_spec((1, d)),
            _const_spec((d, nq + 2 * nk)),
            _const_spec((1, nq)),
            _const_spec((1, nk)),
            _const_spec((256, 256)),
            pl.BlockSpec((tm, LANES), pos_idx),
            pl.BlockSpec((tm, LANES), pos_idx),
        ],
        out_specs=[pl.BlockSpec((tm, nq), lambda i: (i, 0)),
                   pl.BlockSpec((N_KV_HEADS, LANES, tm), lambda i: (0, 0, i)),
                   pl.BlockSpec((tm, N_KV_HEADS * LANES), lambda i: (i, 0)),
                   pl.BlockSpec((tm, nk), lambda i: (jnp.minimum(i, na - 1), 0)),
                   pl.BlockSpec((tm, nk), lambda i: (jnp.minimum(i, na - 1), 0))],
        out_shape=[jax.ShapeDtypeStruct((n, nq), BF16),
                   jax.ShapeDtypeStruct((N_KV_HEADS, LANES, n), BF16),
                   jax.ShapeDtypeStruct((n, N_KV_HEADS * LANES), BF16),
                   jax.ShapeDtypeStruct((n_p, nk), F32),
                   jax.ShapeDtypeStruct((n_p, nk), F32)],
        compiler_params=_params(("arbitrary",)),
        name="qkv",
    )(xp, xs, mod, g, w, qn, kn, eye4, cos, sin)


def _attn_group(q2a, q2b, kt, vv, bias, n_bias, sinks):
    tq = q2a.shape[0]
    nk = kt.shape[1]
    lo = lax.broadcasted_iota(I32, (tq, LANES), 1) < HEAD_DIM
    top = lax.broadcasted_iota(I32, kt.shape, 0) < HEAD_DIM
    zk = jnp.zeros_like(kt)
    kbd = jnp.concatenate([jnp.where(top, kt, zk), jnp.where(top, zk, kt)], axis=1)
    left = lax.broadcasted_iota(I32, vv.shape, 1) < HEAD_DIM
    zv = jnp.zeros_like(vv)
    vbd = jnp.concatenate([jnp.where(left, vv, zv), jnp.where(left, zv, vv)], axis=0)
    s = jnp.dot(jnp.concatenate([q2a, q2b], axis=0), kbd, preferred_element_type=F32)
    p_rows, scales = [], []
    for t in range(2):
        ph, rinv = [], []
        for hf in range(2):
            sh = s[t * tq:(t + 1) * tq, hf * nk:(hf + 1) * nk]
            if bias is not None:
                sh = jnp.concatenate([sh[:, :n_bias] + bias, sh[:, n_bias:]], axis=1)
            sk = sinks[2 * t + hf]
            m = jnp.maximum(jnp.max(sh, axis=-1, keepdims=True), sk)
            p = jnp.exp2(sh - m)
            rinv.append(1.0 / (jnp.sum(p, axis=-1, keepdims=True) + jnp.exp2(sk - m)))
            ph.append(p.astype(BF16))
        p_rows.append(jnp.concatenate(ph, axis=1))
        scales.append(jnp.where(lo, rinv[0], rinv[1]))
    o = jnp.dot(jnp.concatenate(p_rows, axis=0), vbd, preferred_element_type=F32)
    return o[0:tq] * scales[0], o[tq:2 * tq] * scales[1]


def _ctx_attn_kernel(sink_ref, q_ref, kt_ref, vv_ref, x_ref, mod_ref, wo_ref, o_ref, osc):
    for g in range(N_KV_HEADS):
        q2a = q_ref[:, (2 * g) * LANES:(2 * g + 1) * LANES]
        q2b = q_ref[:, (2 * g + 1) * LANES:(2 * g + 2) * LANES]
        sinks = [sink_ref[4 * g + j] * LOG2E for j in range(4)]
        oa, ob = _attn_group(q2a, q2b, kt_ref[g], vv_ref[:, g * LANES:(g + 1) * LANES], None, 0, sinks)
        osc[:, (2 * g) * LANES:(2 * g + 1) * LANES] = oa.astype(BF16)
        osc[:, (2 * g + 1) * LANES:(2 * g + 2) * LANES] = ob.astype(BF16)
    att = jnp.dot(osc[...], wo_ref[...], preferred_element_type=F32)
    o_ref[...] = x_ref[...] + mod_ref[2:3, :] * att


def _ctx_attention(sink, q, kt, vv, xp, seq, mod, wo):
    n_p, d = xp.shape
    nb = n_p // seq
    return pl.pallas_call(
        _ctx_attn_kernel,
        grid=(nb,),
        in_specs=[
            pl.BlockSpec(memory_space=pltpu.SMEM),
            pl.BlockSpec((seq, d), lambda b: (b, 0)),
            pl.BlockSpec((N_KV_HEADS, LANES, seq), lambda b: (0, 0, b)),
            pl.BlockSpec((seq, N_KV_HEADS * LANES), lambda b: (b, 0)),
            pl.BlockSpec((seq, d), lambda b: (b, 0)),
            pl.BlockSpec((None, 8, d), lambda b: (0, 0, 0)),
            _const_spec((d, d)),
        ],
        out_specs=pl.BlockSpec((seq, d), lambda b: (b, 0)),
        out_shape=jax.ShapeDtypeStruct((n_p, d), F32),
        scratch_shapes=[pltpu.VMEM((seq, d), BF16)],
        compiler_params=_params(("arbitrary",)),
        name="ctx_attention",
    )(sink, q, kt, vv, xp, mod, wo)


LAT_QB = 4


def _lat_attn_kernel(t_len, sink_ref, q_ref, ktp_ref, kto_ref, ktn_ref, vvp_ref, vvo_ref, vvn_ref,
                     ckt_ref, cvv_ref, x_ref, mod_ref, wo_ref, o_ref, osc):
    step = pl.program_id(1)
    tq = ATTN_BLOCK
    nloc = 3 * ATTN_BLOCK
    qi = lax.broadcasted_iota(I32, (tq, nloc), 0)
    kj = lax.broadcasted_iota(I32, (tq, nloc), 1)
    in_window = jnp.abs(qi + tq - kj) <= WINDOW
    for u in range(LAT_QB):
        n = step * LAT_QB + u
        kpos = (n - 1) * tq + kj
        bias = jnp.where(in_window & (kpos >= 0) & (kpos < t_len), 0.0, NEG_INF)
        rows = slice(u * tq, (u + 1) * tq)
        for g in range(N_KV_HEADS):
            vs = slice(g * LANES, (g + 1) * LANES)
            kts = [ktp_ref[g]] + [kto_ref[g, :, j * tq:(j + 1) * tq] for j in range(LAT_QB)] + [ktn_ref[g]]
            vvs = [vvp_ref[:, vs]] + [vvo_ref[j * tq:(j + 1) * tq, vs] for j in range(LAT_QB)] + [vvn_ref[:, vs]]
            kt = jnp.concatenate(kts[u:u + 3] + [ckt_ref[g]], axis=1)
            vv = jnp.concatenate(vvs[u:u + 3] + [cvv_ref[:, vs]], axis=0)
            q2a = q_ref[rows, (2 * g) * LANES:(2 * g + 1) * LANES]
            q2b = q_ref[rows, (2 * g + 1) * LANES:(2 * g + 2) * LANES]
            sinks = [sink_ref[4 * g + j] * LOG2E for j in range(4)]
            oa, ob = _attn_group(q2a, q2b, kt, vv, bias, nloc, sinks)
            osc[rows, (2 * g) * LANES:(2 * g + 1) * LANES] = oa.astype(BF16)
            osc[rows, (2 * g + 1) * LANES:(2 * g + 2) * LANES] = ob.astype(BF16)
    att = jnp.dot(osc[...], wo_ref[...], preferred_element_type=F32)
    o_ref[...] = x_ref[...] + mod_ref[2:3, :] * att


def _lat_attention(sink, q, kt, vv, ckt, cvv, xs, n_p, t_len, mod, wo):
    n_s, d = xs.shape
    db = n_s // t_len
    tq = ATTN_BLOCK
    ts = LAT_QB * tq
    nblk = t_len // tq
    nstep = t_len // ts
    assert n_p % ts == 0 and t_len % ts == 0
    off = n_p // tq
    npast = ckt.shape[3]
    g4 = N_KV_HEADS

    def nbr(delta):
        return lambda b, m: off + b * nblk + jnp.clip(m * LAT_QB + delta, 0, nblk - 1)

    own = lambda b, m: n_p // ts + b * nstep + m
    return pl.pallas_call(
        functools.partial(_lat_attn_kernel, t_len),
        grid=(db, nstep),
        in_specs=[
            pl.BlockSpec(memory_space=pltpu.SMEM),
            pl.BlockSpec((ts, d), lambda b, m: (own(b, m), 0)),
            pl.BlockSpec((g4, LANES, tq), lambda b, m: (0, 0, nbr(-1)(b, m))),
            pl.BlockSpec((g4, LANES, ts), lambda b, m: (0, 0, own(b, m))),
            pl.BlockSpec((g4, LANES, tq), lambda b, m: (0, 0, nbr(LAT_QB)(b, m))),
            pl.BlockSpec((tq, g4 * LANES), lambda b, m: (nbr(-1)(b, m), 0)),
            pl.BlockSpec((ts, g4 * LANES), lambda b, m: (own(b, m), 0)),
            pl.BlockSpec((tq, g4 * LANES), lambda b, m: (nbr(LAT_QB)(b, m), 0)),
            pl.BlockSpec((None, g4, LANES, npast), lambda b, m: (b, 0, 0, 0)),
            pl.BlockSpec((None, npast, g4 * LANES), lambda b, m: (b, 0, 0)),
            pl.BlockSpec((ts, d), lambda b, m: (b * nstep + m, 0)),
            pl.BlockSpec((None, 8, d), lambda b, m: (1 + b, 0, 0)),
            _const_spec((d, d)),
        ],
        out_specs=pl.BlockSpec((ts, d), lambda b, m: (b * nstep + m, 0)),
        out_shape=jax.ShapeDtypeStruct((n_s, d), F32),
        scratch_shapes=[pltpu.VMEM((ts, d), BF16)],
        compiler_params=_params(("arbitrary", "arbitrary")),
        name="lat_attention",
    )(sink, q, kt, kt, kt, vv, vv, vv, ckt, cvv, xs, mod, wo)


def _ffn_kernel(na, fc, xa_ref, xb_ref, mod_ref, g_ref, wg_ref, wu_ref, wd_ref, o_ref, acc):
    i = pl.program_id(0)
    x = jnp.where(i < na, xa_ref[...], xb_ref[...])
    a = _adaln(x, g_ref[...], mod_ref[3:4, :], mod_ref[4:5, :]).astype(BF16)
    nf = wg_ref.shape[1] // fc
    for f in range(nf):
        sl = slice(f * fc, (f + 1) * fc)
        hg = jnp.dot(a, wg_ref[:, sl], preferred_element_type=F32)
        hu = jnp.dot(a, wu_ref[:, sl], preferred_element_type=F32)
        act = (_silu(hg) * hu).astype(BF16)
        part = jnp.dot(act, wd_ref[sl, :], preferred_element_type=F32)
        if f == 0:
            acc[...] = part
        else:
            acc[...] += part
    o_ref[...] = x + mod_ref[5:6, :] * acc[...]


def _ffn(hp, hs, t_len, mod, g, wg, wu, wd):
    n_p, n_s = hp.shape[0], hs.shape[0]
    d, dff = wg.shape
    tm = _row_tile(n_p, t_len)
    na, nb = n_p // tm, n_s // tm
    nt_s = t_len // tm
    fc = 256
    assert dff % fc == 0
    mod_idx = lambda i: (jnp.where(i < na, 0, 1 + (i - na) // nt_s), 0, 0)
    return pl.pallas_call(
        functools.partial(_ffn_kernel, na, fc),
        grid=(na + nb,),
        in_specs=[
            pl.BlockSpec((tm, d), lambda i: (jnp.minimum(i, na - 1), 0)),
            pl.BlockSpec((tm, d), lambda i: (jnp.maximum(i - na, 0), 0)),
            pl.BlockSpec((None, 8, d), mod_idx),
            _const_spec((1, d)),
            _const_spec((d, dff)),
            _const_spec((d, dff)),
            _const_spec((dff, d)),
        ],
        out_specs=pl.BlockSpec((tm, d), lambda i: (i, 0)),
        out_shape=jax.ShapeDtypeStruct((n_p + n_s, d), F32),
        scratch_shapes=[pltpu.VMEM((tm, d), F32)],
        compiler_params=_params(("arbitrary",)),
        name="ffn",
    )(hp, hs, mod, g, wg, wu, wd)


def _split2(x):
    hi = x.astype(BF16)
    lo = (x - hi.astype(F32)).astype(BF16)
    return hi, lo


def _gla_in_kernel(nt_s, x_ref, mod_ref, g_ref, w_ref, w1_ref, w2_ref, gb_ref, tri_ref,
                   qk_out, v_out, r_out, b_out):
    x = x_ref[...]
    tm = x.shape[0]
    a = _adaln(x, g_ref[...], mod_ref[0:1, :], mod_ref[1:2, :]).astype(BF16)
    hk = GLA_HEADS * GLA_DK
    hv = GLA_HEADS * GLA_DV
    q = jnp.dot(a, w_ref[:, 0:hk], preferred_element_type=F32)
    qk_out[:, 0:hk] = (q * (GLA_DK ** -0.5)).astype(BF16)
    qk_out[:, hk:2 * hk] = jnp.dot(a, w_ref[:, hk:2 * hk], preferred_element_type=F32).astype(BF16)
    v_out[...] = jnp.dot(a, w_ref[:, 2 * hk:2 * hk + hv], preferred_element_type=F32).astype(BF16)
    r_out[...] = jnp.dot(a, w_ref[:, 2 * hk + hv:2 * hk + 2 * hv], preferred_element_type=F32).astype(BF16)
    z1 = jnp.dot(a, w1_ref[...], preferred_element_type=F32)
    z = jnp.dot(z1.astype(BF16), w2_ref[...], preferred_element_type=F32) + gb_ref[...]
    gate = (jnp.minimum(z, 0.0) - jnp.log(1.0 + jnp.exp(-jnp.abs(z)))) * (1.0 / GLA_GATE_TAU)
    c = GLA_CHUNK
    tri = tri_ref[...]
    for j in range(tm // c):
        rows = slice(j * c, (j + 1) * c)
        for dr in range(2):
            cols = slice(dr * hk, (dr + 1) * hk)
            hi, lo = _split2(gate[rows, cols])
            b_out[rows, cols] = (jnp.dot(tri[dr], hi, preferred_element_type=F32)
                                 + jnp.dot(tri[dr], lo, preferred_element_type=F32))


def _gla_in(h, n_p, t_len, mod, g, w_in, w1, w2, gate_b):
    n, d = h.shape
    tm = _row_tile(n_p, t_len, cap=1024)
    na = n_p // tm
    nt_s = t_len // tm
    hk, hv = GLA_HEADS * GLA_DK, GLA_HEADS * GLA_DV
    c = GLA_CHUNK
    lower = jnp.tril(jnp.ones((c, c), F32))
    tri = jnp.stack([lower, lower.T]).astype(BF16)
    mod_idx = lambda i: (jnp.where(i < na, 0, 1 + (i - na) // nt_s), 0, 0)
    row = lambda i: (i, 0)
    return pl.pallas_call(
        functools.partial(_gla_in_kernel, nt_s),
        grid=(n // tm,),
        in_specs=[
            pl.BlockSpec((tm, d), row),
            pl.BlockSpec((None, 8, d), mod_idx),
            _const_spec((1, d)),
            _const_spec(w_in.shape),
            _const_spec(w1.shape),
            _const_spec(w2.shape),
            _const_spec((1, 2 * hk)),
            _const_spec((2, c, c)),
        ],
        out_specs=[pl.BlockSpec((tm, 2 * hk), row), pl.BlockSpec((tm, hv), row),
                   pl.BlockSpec((tm, hv), row), pl.BlockSpec((tm, 2 * hk), row)],
        out_shape=[jax.ShapeDtypeStruct((n, 2 * hk), BF16), jax.ShapeDtypeStruct((n, hv), BF16),
                   jax.ShapeDtypeStruct((n, hv), BF16), jax.ShapeDtypeStruct((n, 2 * hk), F32)],
        compiler_params=_params(("arbitrary",)),
        name="gla_in",
    )(h, mod, g, w_in, w1, w2, gate_b, tri)


def _gla_direction(qk_ref, v_ref, b_ref, st, o_ref, forward, qd_s, att_s, upd_s, dec_s):
    c = GLA_CHUNK
    ncb = qk_ref.shape[0] // c
    hk = GLA_HEADS * GLA_DK
    dk, dv = GLA_DK, GLA_DV
    ri = lax.broadcasted_iota(I32, (c, c), 0)
    ci_ = lax.broadcasted_iota(I32, (c, c), 1)
    keep = (ci_ <= ri) if forward else (ci_ >= ri)

    for ch in range(ncb):
        rows = slice(ch * c, (ch + 1) * c)
        b = b_ref[rows, :]
        if forward:
            b_end, b_mid = b[c - 1:c, :], b[c // 2 - 1:c // 2, :]
        else:
            b_end, b_mid = b[0:1, :], b[c // 2:c // 2 + 1, :]
        qa = qk_ref[rows, 0:hk] * jnp.exp(b - b_mid)
        ka = qk_ref[rows, hk:2 * hk] * jnp.exp(b_mid - b)
        qd_s[rows, :] = (qa * jnp.exp(b_mid)).astype(BF16)
        kd = (ka * jnp.exp(b_end - b_mid)).astype(BF16)
        dec_s[ch:ch + 1, :] = jnp.exp(b_end)
        qab, kab = qa.astype(BF16), ka.astype(BF16)
        for h in range(GLA_HEADS):
            ks = slice(h * dk, (h + 1) * dk)
            vs = slice(h * dv, (h + 1) * dv)
            att = lax.dot_general(qab[:, ks], kab[:, ks], (((1,), (1,)), ((), ())),
                                  preferred_element_type=F32)
            att_s[ch, h] = jnp.where(keep, att, 0.0).astype(BF16)
            upd_s[ch, h] = lax.dot_general(v_ref[rows, vs], kd[:, ks], (((0,), (0,)), ((), ())),
                                           preferred_element_type=F32)

    for ch in (range(ncb) if forward else reversed(range(ncb))):
        rows = slice(ch * c, (ch + 1) * c)
        for h in range(GLA_HEADS):
            ks = slice(h * dk, (h + 1) * dk)
            vs = slice(h * dv, (h + 1) * dv)
            s = st[h]
            o = lax.dot_general(qd_s[rows, ks], s.astype(BF16), (((1,), (1,)), ((), ())),
                                preferred_element_type=F32)
            o = o + jnp.dot(att_s[ch, h], v_ref[rows, vs], preferred_element_type=F32)
            o_ref[rows, vs] = o.astype(o_ref.dtype)
            st[h] = s * dec_s[ch:ch + 1, ks] + upd_s[ch, h]


def _gla_scan_kernel(shared, from_zero, *refs):
    if shared:
        qkf_ref, vf_ref, bf_ref, bb_ref = refs[:4]
        qkb_ref, vb_ref = qkf_ref, vf_ref
        rest = refs[4:]
    else:
        qkf_ref, vf_ref, bf_ref, qkb_ref, vb_ref, bb_ref = refs[:6]
        rest = refs[6:]
    if not from_zero:
        sf0_ref, sb0_ref = rest[:2]
        rest = rest[2:]
    of_ref, ob_ref, sf_ref, sb_ref, stf, stb, qd_s, att_s, upd_s, dec_s = rest
    i = pl.program_id(1)
    nblk = pl.num_programs(1)

    @pl.when(i == 0)
    def _():
        if from_zero:
            stf[...] = jnp.zeros_like(stf)
            stb[...] = jnp.zeros_like(stb)
        else:
            for h in range(GLA_HEADS):
                stf[h] = sf0_ref[h].T
                stb[h] = sb0_ref[h].T

    _gla_direction(qkf_ref, vf_ref, bf_ref, stf, of_ref, True, qd_s, att_s, upd_s, dec_s)
    _gla_direction(qkb_ref, vb_ref, bb_ref, stb, ob_ref, False, qd_s, att_s, upd_s, dec_s)

    @pl.when(i == nblk - 1)
    def _():
        for h in range(GLA_HEADS):
            sf_ref[h] = stf[h].T
            sb_ref[h] = stb[h].T


def _gla_scan(qk, v, bc, sf0, sb0, b_off, nb, t_len):
    n = qk.shape[0]
    assert n % t_len == 0
    ns = n // t_len
    tb = _row_tile(t_len)
    nblk = t_len // tb
    qk3 = qk.reshape(ns, t_len, qk.shape[1])
    v3 = v.reshape(ns, t_len, v.shape[1])
    bc3 = bc.reshape(ns, t_len, bc.shape[1])
    h, dk, dv = GLA_HEADS, GLA_DK, GLA_DV
    hk = h * dk
    st_spec = pl.BlockSpec((None, h, dk, dv), lambda b, i: (b, 0, 0, 0))
    fwd = lambda b, i: (b_off + b, i, 0)
    bwd = lambda b, i: (b_off + b, nblk - 1 - i, 0)
    bwd_b = pl.BlockSpec((None, tb, hk), lambda b, i: (b_off + b, nblk - 1 - i, 1))
    shared = nblk == 1
    in_specs = [pl.BlockSpec((None, tb, 2 * hk), fwd), pl.BlockSpec((None, tb, h * dv), fwd),
                pl.BlockSpec((None, tb, hk), fwd)]
    operands = [qk3, v3, bc3]
    if not shared:
        in_specs += [pl.BlockSpec((None, tb, 2 * hk), bwd), pl.BlockSpec((None, tb, h * dv), bwd)]
        operands += [qk3, v3]
    in_specs += [bwd_b]
    operands += [bc3]
    from_zero = sf0 is None
    if not from_zero:
        in_specs += [st_spec, st_spec]
        operands += [sf0, sb0]
    return pl.pallas_call(
        functools.partial(_gla_scan_kernel, shared, from_zero),
        grid=(nb, nblk),
        in_specs=in_specs,
        out_specs=[pl.BlockSpec((None, tb, h * dv), lambda b, i: (b, i, 0)),
                   pl.BlockSpec((None, tb, h * dv), lambda b, i: (b, nblk - 1 - i, 0)),
                   st_spec, st_spec],
        out_shape=[jax.ShapeDtypeStruct((nb, t_len, h * dv), BF16),
                   jax.ShapeDtypeStruct((nb, t_len, h * dv), BF16),
                   jax.ShapeDtypeStruct((nb, h, dk, dv), F32),
                   jax.ShapeDtypeStruct((nb, h, dk, dv), F32)],
        scratch_shapes=[pltpu.VMEM((h, dv, dk), F32), pltpu.VMEM((h, dv, dk), F32),
                        pltpu.VMEM((tb, hk), BF16),
                        pltpu.VMEM((tb // GLA_CHUNK, h, GLA_CHUNK, GLA_CHUNK), BF16),
                        pltpu.VMEM((tb // GLA_CHUNK, h, dv, dk), F32),
                        pltpu.VMEM((max(tb // GLA_CHUNK, 8), hk), F32)],
        compiler_params=_params(("arbitrary", "arbitrary")),
        name="gla_scan",
    )(*operands)


GLA_OUT_PARTS = 4


def _gla_out_kernel(na, ofa_ref, oba_ref, ofb_ref, obb_ref, r_ref, h_ref, mod_ref, on_ref, wo_ref,
                    g2_ref, rw_ref, tri_ref, h_out, f_out, route_out, wt_out, cnt_out, run):
    i = pl.program_id(0)
    is_ctx = i < na

    @pl.when(i == 0)
    def _():
        run[...] = jnp.zeros_like(run)

    tm = tri_ref.shape[0]
    for part in range(h_ref.shape[0] // tm):
        rows = slice(part * tm, (part + 1) * tm)
        _gla_out_rows(rows, tm, is_ctx, ofa_ref, oba_ref, ofb_ref, obb_ref, r_ref, h_ref, mod_ref, on_ref,
                      wo_ref, g2_ref, rw_ref, tri_ref, h_out, f_out, route_out, wt_out, run)
    cnt_out[...] = run[...]


def _gla_out_rows(rows, tm, is_ctx, ofa_ref, oba_ref, ofb_ref, obb_ref, r_ref, h_ref, mod_ref, on_ref,
                  wo_ref, g2_ref, rw_ref, tri_ref, h_out, f_out, route_out, wt_out, run):
    o = (jnp.where(is_ctx, ofa_ref[rows, :], ofb_ref[rows, :]).astype(F32)
         + jnp.where(is_ctx, oba_ref[rows, :], obb_ref[rows, :]).astype(F32))
    dv = GLA_DV
    parts = []
    for hh in range(GLA_HEADS):
        oh = o[:, hh * dv:(hh + 1) * dv]
        parts.append(_rms(oh) * on_ref[...])
    on = jnp.concatenate(parts, axis=1)
    r = r_ref[rows, :].astype(F32)
    gated = (on * _silu(r)).astype(BF16)
    h = h_ref[rows, :] + mod_ref[2:3, :] * jnp.dot(gated, wo_ref[...], preferred_element_type=F32)
    h_out[rows, :] = h
    f = _adaln(h, g2_ref[...], mod_ref[3:4, :], mod_ref[4:5, :])
    f_out[rows, :] = _pack_halves(f)
    fh = f.astype(BF16)
    fl = (f - fh.astype(F32)).astype(BF16)
    logits = (jnp.dot(fh, rw_ref[0], preferred_element_type=F32)
              + jnp.dot(fl, rw_ref[0], preferred_element_type=F32)
              + jnp.dot(fh, rw_ref[1], preferred_element_type=F32))
    lane = lax.broadcasted_iota(I32, (tm, LANES), 1)
    lane_f = lane.astype(F32)
    logits = jnp.where(lane < N_EXPERTS, logits, -jnp.inf)
    m1 = jnp.max(logits, axis=-1, keepdims=True)
    i1 = jnp.min(jnp.where(logits == m1, lane_f, float(LANES)), axis=-1, keepdims=True)
    rest = jnp.where(lane_f == i1, -jnp.inf, logits)
    m2 = jnp.max(rest, axis=-1, keepdims=True)
    i2 = jnp.min(jnp.where(rest == m2, lane_f, float(LANES)), axis=-1, keepdims=True)
    e2 = jnp.exp(m2 - m1)
    w1 = 1.0 / (1.0 + e2)
    w2 = e2 / (1.0 + e2)
    wt_out[rows, :] = jnp.where(lane == 0, w1, jnp.where(lane == 1, w2, 0.0))
    sel1 = lane_f == i1
    sel2 = lane_f == i2
    picked = jnp.where(sel1, 1.0, jnp.where(sel2, 1.0, 0.0))
    before = jnp.dot(tri_ref[...], picked.astype(BF16), preferred_element_type=F32) + run[0:1, :]
    rank1 = jnp.sum(jnp.where(sel1, before, 0.0), axis=-1, keepdims=True)
    rank2 = jnp.sum(jnp.where(sel2, before, 0.0), axis=-1, keepdims=True)
    run[...] = run[...] + jnp.sum(picked, axis=0, keepdims=True)
    routing = jnp.where(lane == 0, i1, jnp.where(lane == 1, i2,
                        jnp.where(lane == 2, rank1, jnp.where(lane == 3, rank2, 0.0))))
    route_out[:, rows] = routing.T[0:8, :]


def _gla_out(of_p, ob_p, of_s, ob_s, r, h, n_p, t_len, mod, out_norm, wo, g2, router_w):
    n, d = h.shape
    tm = _row_tile(n_p, t_len, cap=1024)
    na = n_p // tm
    nt_s = t_len // tm
    mod_idx = lambda i: (jnp.where(i < na, 0, 1 + (i - na) // nt_s), 0, 0)
    row = lambda i: (i, 0)
    ctx_row = lambda i: (jnp.minimum(i, na - 1), 0)
    lat_row = lambda i: (jnp.maximum(i - na, 0), 0)
    tp = tm // GLA_OUT_PARTS
    tri = jnp.tril(jnp.ones((tp, tp), F32), -1).astype(BF16)
    return pl.pallas_call(
        functools.partial(_gla_out_kernel, na),
        grid=(n // tm,),
        in_specs=[
            pl.BlockSpec((tm, d), ctx_row),
            pl.BlockSpec((tm, d), ctx_row),
            pl.BlockSpec((tm, d), lat_row),
            pl.BlockSpec((tm, d), lat_row),
            pl.BlockSpec((tm, d), row),
            pl.BlockSpec((tm, d), row),
            pl.BlockSpec((None, 8, d), mod_idx),
            _const_spec((1, GLA_DV)),
            _const_spec((d, d)),
            _const_spec((1, d)),
            _const_spec((2, d, LANES)),
            _const_spec((tp, tp)),
        ],
        out_specs=[pl.BlockSpec((tm, d), row), pl.BlockSpec((tm, d // 2), row),
                   pl.BlockSpec((8, tm), lambda i: (0, i)), pl.BlockSpec((tm, LANES), row),
                   _const_spec((8, LANES))],
        out_shape=[jax.ShapeDtypeStruct((n, d), F32), jax.ShapeDtypeStruct((n, d // 2), jnp.uint32),
                   jax.ShapeDtypeStruct((8, n), F32), jax.ShapeDtypeStruct((n, LANES), F32),
                   jax.ShapeDtypeStruct((8, LANES), F32)],
        scratch_shapes=[pltpu.VMEM((8, LANES), F32)],
        compiler_params=_params(("arbitrary",)),
        name="gla_out_router",
    )(of_p, ob_p, of_s, ob_s, r, h, mod, out_norm, wo, g2, router_w, tri)


SC_CORES = 2
SC_SUBCORES = 16
SC_CHUNK = 64


def _sc_gather_rows(table, idx):
    nw = SC_CORES * SC_SUBCORES
    b = idx.shape[0]
    d = table.shape[1]
    assert b % (nw * SC_CHUNK) == 0
    per_w = b // nw
    n_chunks = per_w // SC_CHUNK
    mesh = plsc.VectorSubcoreMesh(core_axis_name="c", subcore_axis_name="s",
                                  num_cores=SC_CORES, num_subcores=SC_SUBCORES)

    assert n_chunks % 2 == 0

    def body(table_hbm, idx_hbm, out_hbm, idx_v, rows_v, gsem, wsem):
        wid = lax.axis_index("s") * SC_CORES + lax.axis_index("c")
        base = wid * per_w
        pltpu.sync_copy(idx_hbm.at[wid], idx_v)

        def gather(j, slot):
            return pltpu.make_async_copy(table_hbm.at[idx_v.at[j]], rows_v.at[slot], gsem.at[slot])

        def write(j, slot):
            return pltpu.make_async_copy(rows_v.at[slot], out_hbm.at[pl.ds(base + j * SC_CHUNK, SC_CHUNK)],
                                         wsem.at[slot])

        gather(0, 0).start()

        @pl.loop(0, n_chunks, step=2)
        def _(j):
            for slot in range(2):
                jj = j + slot
                gather(jj, slot).wait()

                @pl.when(jj + 1 < n_chunks)
                def _():
                    @pl.when(jj >= 1)
                    def _():
                        write(jj - 1, 1 - slot).wait()

                    gather(jj + 1, 1 - slot).start()

                write(jj, slot).start()

        write(n_chunks - 2, 0).wait()
        write(n_chunks - 1, 1).wait()

    return pl.kernel(
        body,
        out_type=jax.ShapeDtypeStruct((b, d), table.dtype),
        mesh=mesh,
        scratch_types=[pltpu.VMEM((n_chunks, SC_CHUNK), I32),
                       pltpu.VMEM((2, SC_CHUNK, d), table.dtype),
                       pltpu.SemaphoreType.DMA((2,)),
                       pltpu.SemaphoreType.DMA((2,))],
        name="sc_gather_rows",
    )(table, idx.reshape(nw, n_chunks, SC_CHUNK))


def _sc_scatter_rows(rows, idx2, p):
    nw = SC_CORES * SC_SUBCORES
    n, d = rows.shape
    assert n % (nw * SC_CHUNK) == 0
    per_w = n // nw
    n_chunks = per_w // SC_CHUNK
    mesh = plsc.VectorSubcoreMesh(core_axis_name="c", subcore_axis_name="s",
                                  num_cores=SC_CORES, num_subcores=SC_SUBCORES)

    assert n_chunks % 2 == 0

    def body(rows_hbm, idx0_hbm, idx1_hbm, out_hbm, idx0_v, idx1_v, rows_v, rsem, ssem):
        wid = lax.axis_index("s") * SC_CORES + lax.axis_index("c")
        base = wid * per_w
        pltpu.sync_copy(idx0_hbm.at[wid], idx0_v)
        pltpu.sync_copy(idx1_hbm.at[wid], idx1_v)

        def read(j, slot):
            return pltpu.make_async_copy(rows_hbm.at[pl.ds(base + j * SC_CHUNK, SC_CHUNK)], rows_v.at[slot],
                                         rsem.at[slot])

        read(0, 0).start()

        @pl.loop(0, n_chunks, step=2)
        def _(j):
            for slot in range(2):
                jj = j + slot
                read(jj, slot).wait()

                @pl.when(jj + 1 < n_chunks)
                def _():
                    read(jj + 1, 1 - slot).start()

                s0 = pltpu.make_async_copy(rows_v.at[slot], out_hbm.at[idx0_v.at[jj]], ssem.at[0])
                s1 = pltpu.make_async_copy(rows_v.at[slot], out_hbm.at[idx1_v.at[jj]], ssem.at[1])
                s0.start()
                s1.start()
                s0.wait()
                s1.wait()

    return pl.kernel(
        body,
        out_type=jax.ShapeDtypeStruct((p, d), rows.dtype),
        mesh=mesh,
        scratch_types=[pltpu.VMEM((n_chunks, SC_CHUNK), I32),
                       pltpu.VMEM((n_chunks, SC_CHUNK), I32),
                       pltpu.VMEM((2, SC_CHUNK, d), rows.dtype),
                       pltpu.SemaphoreType.DMA((2,)),
                       pltpu.SemaphoreType.DMA((2,))],
        name="sc_scatter_rows",
    )(rows, idx2[0].reshape(nw, n_chunks, SC_CHUNK), idx2[1].reshape(nw, n_chunks, SC_CHUNK))


EXPERT_SUB = 256


def _pack_halves(x):
    k = x.shape[1] // 2
    lo = lax.bitcast_convert_type(x[:, :k].astype(BF16).astype(F32), jnp.uint32)
    hi = lax.bitcast_convert_type(x[:, k:].astype(BF16).astype(F32), jnp.uint32)
    return (lo >> 16) | (hi & jnp.uint32(0xFFFF0000))


def _unpack_halves(w):
    lo = lax.bitcast_convert_type(w << 16, F32).astype(BF16)
    hi = lax.bitcast_convert_type(w & jnp.uint32(0xFFFF0000), F32).astype(BF16)
    return lo, hi


def _expert_kernel(te_ref, ts_ref, nu_ref, nv_ref, x_ref, wg_ref, wu_ref, wd_ref, y_ref, acc, xb):
    i = pl.program_id(0)
    f = pl.program_id(1)
    nf = pl.num_programs(1)
    tm = x_ref.shape[0]
    half = x_ref.shape[1]
    sub = EXPERT_SUB
    nvalid = nv_ref[i]

    @pl.when(f == 0)
    def _():
        acc[...] = jnp.zeros_like(acc)

    @pl.when((f == 0) & (nvalid > 0))
    def _():
        rid = lax.broadcasted_iota(I32, (tm, half), 0)
        lo, hi = _unpack_halves(x_ref[...])
        zero = jnp.zeros_like(lo)
        xb[:, 0:half] = jnp.where(rid < nvalid, lo, zero)
        xb[:, half:2 * half] = jnp.where(rid < nvalid, hi, zero)

    def compute(rows):
        x = xb[rows, :]
        hg = jnp.dot(x, wg_ref[...].astype(BF16), preferred_element_type=F32)
        hu = jnp.dot(x, wu_ref[...].astype(BF16), preferred_element_type=F32)
        act = (_silu(hg) * hu).astype(BF16)
        acc[rows, :] += jnp.dot(act, wd_ref[...].astype(BF16), preferred_element_type=F32)

    for k in range(1, tm // sub + 1):
        @pl.when((nvalid > (k - 1) * sub) & (nvalid <= k * sub))
        def _():
            compute(slice(0, k * sub))

    @pl.when(f == nf - 1)
    def _():
        y_ref[...] = _pack_halves(acc[...])


def _experts(x, tile_expert, tile_src, n_used, n_valid, wg, wu, wd, tm, tf):
    p, half = x.shape
    d = 2 * half
    ne, _, dff = wg.shape
    nf = dff // tf
    assert dff % tf == 0 and tm % EXPERT_SUB == 0

    def fidx(i, f, nu):
        return jnp.where(i < nu[0], f, nf - 1)

    grid_spec = pltpu.PrefetchScalarGridSpec(
        num_scalar_prefetch=4,
        grid=(p // tm, nf),
        in_specs=[
            pl.BlockSpec((tm, half), lambda i, f, te, ts, nu, nv: (ts[i], 0)),
            pl.BlockSpec((None, d, tf), lambda i, f, te, ts, nu, nv: (te[i], 0, fidx(i, f, nu))),
            pl.BlockSpec((None, d, tf), lambda i, f, te, ts, nu, nv: (te[i], 0, fidx(i, f, nu))),
            pl.BlockSpec((None, tf, d), lambda i, f, te, ts, nu, nv: (te[i], fidx(i, f, nu), 0)),
        ],
        out_specs=pl.BlockSpec((tm, half), lambda i, f, te, ts, nu, nv: (i, 0)),
        scratch_shapes=[pltpu.VMEM((tm, d), F32), pltpu.VMEM((tm, d), BF16)],
    )
    return pl.pallas_call(
        _expert_kernel,
        grid_spec=grid_spec,
        out_shape=jax.ShapeDtypeStruct((p, half), jnp.uint32),
        compiler_params=_params(("arbitrary", "arbitrary"), 60 * 1024 * 1024),
        name="moe_experts",
    )(tile_expert, tile_src, n_used, n_valid, x, wg, wu, wd)


def _combine_kernel(wt_ref, h_ref, mod_ref, y0_ref, y1_ref, o_ref):
    wt = wt_ref[...]
    half = y0_ref.shape[1]
    lo0, hi0 = _unpack_halves(y0_ref[...])
    lo1, hi1 = _unpack_halves(y1_ref[...])
    w0, w1 = wt[:, 0:1], wt[:, 1:2]
    h = h_ref[...]
    gate = mod_ref[5:6, :]
    o_ref[:, 0:half] = h[:, 0:half] + gate[:, 0:half] * (w0 * lo0.astype(F32) + w1 * lo1.astype(F32))
    o_ref[:, half:] = h[:, half:] + gate[:, half:] * (w0 * hi0.astype(F32) + w1 * hi1.astype(F32))


def _combine(yg, wt, h, mod, row_off, n_rows, n_mod, mod_off, tm):
    d = h.shape[1]
    nt = n_rows // tm
    toff = row_off // tm
    per_mod = n_rows // n_mod // tm
    return pl.pallas_call(
        _combine_kernel,
        grid=(nt,),
        in_specs=[
            pl.BlockSpec((tm, LANES), lambda i: (toff + i, 0)),
            pl.BlockSpec((tm, d), lambda i: (toff + i, 0)),
            pl.BlockSpec((None, 8, d), lambda i: (mod_off + i // per_mod, 0, 0)),
            pl.BlockSpec((None, tm, d // 2), lambda i: (0, i, 0)),
            pl.BlockSpec((None, tm, d // 2), lambda i: (1, i, 0)),
        ],
        out_specs=pl.BlockSpec((tm, d), lambda i: (i, 0)),
        out_shape=jax.ShapeDtypeStruct((n_rows, d), F32),
        compiler_params=_params(("arbitrary",)),
        name="moe_combine",
    )(wt, h, mod, yg, yg)


def _route(route, counts, n, tm):
    cnt = counts[0, :N_EXPERTS].astype(I32)
    padded = ((cnt + tm - 1) // tm) * tm
    gend = jnp.cumsum(padded)
    goff = gend - padded
    e = route[0:2].astype(I32)
    rank = route[2:4].astype(I32)
    onehot = e[:, :, None] == jnp.arange(N_EXPERTS, dtype=I32)[None, None, :]
    dest = jnp.sum(jnp.where(onehot, goff[None, None, :], 0), axis=-1) + rank
    p = 2 * n + N_EXPERTS * tm
    n_used = gend[-1] // tm
    tiles = jnp.arange(p // tm, dtype=I32)
    tile_src = jnp.minimum(tiles, n_used - 1)
    tile_expert = jnp.minimum(jnp.sum((gend[None, :] <= (tile_src * tm)[:, None]).astype(I32), axis=1),
                              N_EXPERTS - 1)
    used = goff[tile_expert] + cnt[tile_expert]
    n_valid = jnp.where(tiles < n_used, jnp.clip(used - tiles * tm, 0, tm), 0).astype(I32)
    return dest, p, tile_expert, tile_src, n_used.reshape(1).astype(I32), n_valid


def kernel(x_prompt, x_sample, cache_k, cache_v, state_fwd, state_bwd, c, c_ctx, ada_w, ada_b, norm1_g, norm2_g, attn_w_qkv, attn_q_norm, attn_k_norm, attn_sink, attn_w_o, gla_w_in, gla_gate_w1, gla_gate_w2, gla_gate_b, gla_out_norm, gla_w_o, ffn_w_gate, ffn_w_up, ffn_w_down, moe_router, moe_w_gate, moe_w_up, moe_w_down):
    bp, seq, d = x_prompt.shape
    db, t_len, _ = x_sample.shape
    n_p, n_s = bp * seq, db * t_len
    n = n_p + n_s
    xp = x_prompt.reshape(n_p, d)
    xs = x_sample.reshape(n_s, d)

    cond = jnp.concatenate([c_ctx[None, :], c], axis=0)
    assert cond.shape[0] <= 8
    cond_t = jnp.pad(cond, ((0, 8 - cond.shape[0]), (0, 0))).T
    mods = _modulation(cond_t, cond.shape[0], ada_w, ada_b)

    nk = N_KV_HEADS * HEAD_DIM
    qn = jnp.tile(attn_q_norm[0], N_HEADS)[None, :]
    kn = jnp.tile(attn_k_norm[0], N_KV_HEADS)[None, :]
    q, kt, vv, ck_new, cv_new = _qkv(xp, xs, t_len, mods[0], norm1_g[0][None, :],
                                     attn_w_qkv[0].astype(BF16), qn, kn)
    wo0 = attn_w_o[0].astype(BF16)
    sink = attn_sink[0]
    hp = _ctx_attention(sink, q, kt, vv, xp, seq, mods[0], wo0)
    ck = cache_k[:, 0].astype(BF16)
    cv = cache_v[:, 0].astype(BF16)
    ckt = jnp.concatenate([ck, ck], axis=-1).transpose(0, 2, 3, 1)
    cvv = jnp.concatenate([cv, cv], axis=-1).reshape(db, cv.shape[1], N_KV_HEADS * LANES)
    hs = _lat_attention(sink, q, kt, vv, ckt, cvv, xs, n_p, t_len, mods[0], wo0)
    new_k = ck_new.reshape(bp, 1, seq, N_KV_HEADS, HEAD_DIM)
    new_v = cv_new.reshape(bp, 1, seq, N_KV_HEADS, HEAD_DIM)
    h = _ffn(hp, hs, t_len, mods[0], norm2_g[0][None, :], ffn_w_gate[0].astype(BF16),
             ffn_w_up[0].astype(BF16), ffn_w_down[0].astype(BF16))

    hk = GLA_HEADS * GLA_DK
    rank = GLA_GATE_RANK
    w1 = jnp.concatenate([gla_gate_w1[0, 0], gla_gate_w1[0, 1]], axis=1)
    w1 = jnp.pad(w1, ((0, 0), (0, LANES - 2 * rank))).astype(BF16)
    w2 = jnp.zeros((LANES, 2 * hk), F32)
    w2 = w2.at[0:rank, 0:hk].set(gla_gate_w2[0, 0]).at[rank:2 * rank, hk:].set(gla_gate_w2[0, 1]).astype(BF16)
    gate_b = gla_gate_b[0].reshape(1, 2 * hk)
    qk, v, r, bc = _gla_in(h, n_p, t_len, mods[1], norm1_g[1][None, :], gla_w_in[0].astype(BF16),
                           w1, w2, gate_b)
    of_p, ob_p, new_sf, new_sb = _gla_scan(qk, v, bc, None, None, 0, bp, seq)
    of_s, ob_s, _, _ = _gla_scan(qk, v, bc, state_fwd[:, 0], state_bwd[:, 0], n_p // t_len, db, t_len)
    rw = jnp.pad(moe_router[0], ((0, 0), (0, LANES - N_EXPERTS)))
    rw_hi = rw.astype(BF16)
    rw_lo = (rw - rw_hi.astype(F32)).astype(BF16)
    h, f, route, wt, counts = _gla_out(
        of_p.reshape(n_p, d), ob_p.reshape(n_p, d), of_s.reshape(n_s, d), ob_s.reshape(n_s, d), r, h,
        n_p, t_len, mods[1], gla_out_norm[0][None, :], gla_w_o[0].astype(BF16), norm2_g[1][None, :],
        jnp.stack([rw_hi, rw_lo]))

    tm_e = 1024
    dest_k, p_rows, tile_expert, tile_src, n_used, n_valid = _route(route, counts, n, tm_e)
    xg = _sc_scatter_rows(f, dest_k, p_rows)
    xg, new_k = lax.optimization_barrier((xg, new_k))
    y = _experts(xg, tile_expert, tile_src, n_used, n_valid, moe_w_gate[0].astype(BF16), moe_w_up[0].astype(BF16),
                 moe_w_down[0].astype(BF16), tm_e, 512)
    yg_p = _sc_gather_rows(y, dest_k[:, :n_p].reshape(2 * n_p)).reshape(2, n_p, d // 2)
    yg_s = _sc_gather_rows(y, dest_k[:, n_p:].reshape(2 * n_s)).reshape(2, n_s, d // 2)
    tm_c = _row_tile(n_p, t_len)
    y_p = _combine(yg_p, wt, h, mods[1], 0, n_p, 1, 0, tm_c)
    y_s = _combine(yg_s, wt, h, mods[1], n_p, n_s, db, 1, tm_c)

    return (y_p.reshape(bp, seq, d), y_s.reshape(db, t_len, d), new_k, new_v,
            new_sf[:, None], new_sb[:, None])
```

```python
import functools
import math

import numpy as np
import jax
import jax.numpy as jnp
from jax import lax
from jax.experimental import pallas as pl
from jax.experimental.pallas import tpu as pltpu
from jax.experimental.pallas import tpu_sc as plsc

F32 = jnp.float32
BF16 = jnp.bfloat16
I32 = jnp.int32

D_MODEL = 1024
N_HEADS = 16
N_KV_HEADS = 4
HEAD_DIM = 64
GRID_W = 64
WINDOW = 128
ATTN_BLOCK = 128
ROPE_THETA = 10000.0
GLA_HEADS = 4
GLA_DK = 128
GLA_DV = 256
GLA_GATE_RANK = 16
GLA_GATE_TAU = 16.0
GLA_CHUNK = 64
N_EXPERTS = 8
NORM_EPS = 1e-6
NEG_INF = -1e30

LANES = 128
VMEM_LIMIT = 56 * 1024 * 1024


def _params(sem, vmem=VMEM_LIMIT):
    return pltpu.CompilerParams(dimension_semantics=sem, vmem_limit_bytes=vmem)


def _row_tile(*counts, cap=512):
    t = cap
    while any(c % t for c in counts):
        t //= 2
    assert t >= 8
    return t


def _rms(x):
    return x * lax.rsqrt(jnp.mean(x * x, axis=-1, keepdims=True) + NORM_EPS)


def _adaln(x, g, shift, scale):
    return _rms(x) * (g * (1.0 + scale)) + shift


def _silu(x):
    return x * jax.nn.sigmoid(x)


def _const_spec(shape):
    nd = len(shape)
    return pl.BlockSpec(shape, lambda *_: (0,) * nd)


def _mod_kernel(n_cond, ct_ref, w_ref, b_ref, o_ref):
    ct = ct_ref[...]
    s = _silu(ct)
    w = w_ref[...]
    rows = [jnp.sum(w * s[:, r:r + 1], axis=0, keepdims=True) for r in range(n_cond)]
    rows += [jnp.zeros_like(rows[0])] * (8 - n_cond)
    o_ref[...] = jnp.concatenate(rows, axis=0) + b_ref[...]


def _modulation(cond_t, n_cond, ada_w, ada_b):
    depth, d, n6 = ada_w.shape
    tn = 2048
    out = pl.pallas_call(
        functools.partial(_mod_kernel, n_cond),
        grid=(depth, n6 // tn),
        in_specs=[
            pl.BlockSpec((d, 8), lambda l, j: (0, 0)),
            pl.BlockSpec((None, d, tn), lambda l, j: (l, 0, j)),
            pl.BlockSpec((None, 1, tn), lambda l, j: (l, 0, j)),
        ],
        out_specs=pl.BlockSpec((None, 8, tn), lambda l, j: (l, 0, j)),
        out_shape=jax.ShapeDtypeStruct((depth, 8, n6), F32),
        compiler_params=_params(("arbitrary", "arbitrary")),
        name="modulation",
    )(cond_t, ada_w, ada_b.reshape(depth, 1, n6))
    m = out[:, :n_cond].reshape(depth, n_cond, 6, d)
    return jnp.pad(m, ((0, 0), (0, 0), (0, 2), (0, 0)))


LOG2E = math.log2(math.e)


def _dup_half(k2, half):
    lane = lax.broadcasted_iota(I32, k2.shape, 1)
    lo = lane < HEAD_DIM
    r = pltpu.roll(k2, HEAD_DIM, 1)
    return jnp.where(lo, k2, r) if half == 0 else jnp.where(lo, r, k2)


QKV_PARTS = 2


def _qkv_kernel(na, xa_ref, xb_ref, mod_ref, g_ref, w_ref, qn_ref, kn_ref, bd_ref,
                cos_ref, sin_ref, q_out, kt_out, vv_out, ck_out, cv_out):
    i = pl.program_id(0)
    is_ctx = i < na
    tm = xa_ref.shape[0] // QKV_PARTS
    for part in range(QKV_PARTS):
        _qkv_rows(slice(part * tm, (part + 1) * tm), tm, is_ctx, xa_ref, xb_ref, mod_ref, g_ref, w_ref,
                  qn_ref, kn_ref, bd_ref, cos_ref, sin_ref, q_out, kt_out, vv_out, ck_out, cv_out)


def _qkv_rows(rows, tm, is_ctx, xa_ref, xb_ref, mod_ref, g_ref, w_ref, qn_ref, kn_ref, bd_ref,
              cos_ref, sin_ref, q_out, kt_out, vv_out, ck_out, cv_out):
    x = jnp.where(is_ctx, xa_ref[rows, :], xb_ref[rows, :])
    a = _adaln(x, g_ref[...], mod_ref[0:1, :], mod_ref[1:2, :])
    y = jnp.dot(a.astype(BF16), w_ref[...], preferred_element_type=F32)
    cos = jnp.where(is_ctx, 1.0, cos_ref[rows, :])
    sin = jnp.where(is_ctx, 0.0, sin_ref[rows, :])
    lane = lax.broadcasted_iota(I32, (tm, LANES), 1)
    first16 = (lane % 32) < 16

    def norm_rope(z, wt):
        ss = jnp.dot((z * z).astype(BF16), bd_ref[...], preferred_element_type=F32)
        zn = z * lax.rsqrt(ss * (1.0 / HEAD_DIM) + NORM_EPS) * wt
        outs = []
        for c in range(2):
            t = zn[:, c * LANES:(c + 1) * LANES]
            partner = jnp.where(first16, pltpu.roll(t, LANES - 16, 1), pltpu.roll(t, 16, 1))
            outs.append(t * cos + partner * sin)
        return jnp.concatenate(outs, axis=1)

    nq = N_HEADS * HEAD_DIM
    nk = N_KV_HEADS * HEAD_DIM
    scale = HEAD_DIM ** -0.5 * LOG2E
    for s in range(nq // 256):
        sl = slice(s * 256, (s + 1) * 256)
        q_out[rows, sl] = (norm_rope(y[:, sl], qn_ref[:, sl]) * scale).astype(BF16)
    k = norm_rope(y[:, nq:nq + nk], kn_ref[...])
    v = y[:, nq + nk:nq + 2 * nk]
    for g in range(N_KV_HEADS):
        c = g // 2
        kk = _dup_half(k[:, c * LANES:(c + 1) * LANES], g % 2)
        kt_out[g, :, rows] = kk.T.astype(BF16)
        vv_out[rows, g * LANES:(g + 1) * LANES] = _dup_half(v[:, c * LANES:(c + 1) * LANES], g % 2).astype(BF16)

    @pl.when(is_ctx)
    def _():
        ck_out[rows, :] = k
        cv_out[rows, :] = v


def _rope_tables(t_len):
    pos = np.arange(t_len)
    row = (pos // GRID_W).astype(np.float32)[:, None]
    col = (pos % GRID_W).astype(np.float32)[:, None]
    half = HEAD_DIM // 2
    inv = (np.float32(ROPE_THETA) ** (-np.arange(0, half, 2, dtype=np.float32) / np.float32(half)))[None, :]
    ar, ac = row * inv, col * inv
    cos = np.concatenate([np.cos(ar), np.cos(ar), np.cos(ac), np.cos(ac)], axis=1)
    sin = np.concatenate([-np.sin(ar), np.sin(ar), -np.sin(ac), np.sin(ac)], axis=1)
    return (jnp.asarray(np.tile(cos, (1, 2)), dtype=F32), jnp.asarray(np.tile(sin, (1, 2)), dtype=F32))


def _qkv(xp, xs, t_len, mod, g, w, qn, kn):
    n_p, n_s = xp.shape[0], xs.shape[0]
    tm = _row_tile(n_p, t_len, cap=1024)
    na, nb = n_p // tm, n_s // tm
    nt_s = t_len // tm
    n = n_p + n_s
    d = D_MODEL
    nq, nk = N_HEADS * HEAD_DIM, N_KV_HEADS * HEAD_DIM
    cos, sin = _rope_tables(t_len)
    eye4 = jnp.kron(jnp.eye(4, dtype=F32), jnp.ones((HEAD_DIM, HEAD_DIM), F32)).astype(BF16)
    mod_idx = lambda i: (jnp.where(i < na, 0, 1 + (i - na) // nt_s), 0, 0)
    pos_idx = lambda i: (jnp.maximum(i - na, 0) % nt_s, 0)
    return pl.pallas_call(
        functools.partial(_qkv_kernel, na),
        grid=(na + nb,),
        in_specs=[
            pl.BlockSpec((tm, d), lambda i: (jnp.minimum(i, na - 1), 0)),
            pl.BlockSpec((tm, d), lambda i: (jnp.maximum(i - na, 0), 0)),
            pl.BlockSpec((None, 8, d), mod_idx),
            _const_spec((1, d)),
            _const_spec((d, nq + 2 * nk)),
            _const_spec((1, nq)),
            _const_spec((1, nk)),
            _const_spec((256, 256)),
            pl.BlockSpec((tm, LANES), pos_idx),
            pl.BlockSpec((tm, LANES), pos_idx),
        ],
        out_specs=[pl.BlockSpec((tm, nq), lambda i: (i, 0)),
                   pl.BlockSpec((N_KV_HEADS, LANES, tm), lambda i: (0, 0, i)),
                   pl.BlockSpec((tm, N_KV_HEADS * LANES), lambda i: (i, 0)),
                   pl.BlockSpec((tm, nk), lambda i: (jnp.minimum(i, na - 1), 0)),
                   pl.BlockSpec((tm, nk), lambda i: (jnp.minimum(i, na - 1), 0))],
        out_shape=[jax.ShapeDtypeStruct((n, nq), BF16),
                   jax.ShapeDtypeStruct((N_KV_HEADS, LANES, n), BF16),
                   jax.ShapeDtypeStruct((n, N_KV_HEADS * LANES), BF16),
                   jax.ShapeDtypeStruct((n_p, nk), F32),
                   jax.ShapeDtypeStruct((n_p, nk), F32)],
        compiler_params=_params(("arbitrary",)),
        name="qkv",
    )(xp, xs, mod, g, w, qn, kn, eye4, cos, sin)


def _attn_group(q2a, q2b, kt, vv, bias, n_bias, sinks):
    tq = q2a.shape[0]
    nk = kt.shape[1]
    lo = lax.broadcasted_iota(I32, (tq, LANES), 1) < HEAD_DIM
    top = lax.broadcasted_iota(I32, kt.shape, 0) < HEAD_DIM
    zk = jnp.zeros_like(kt)
    kbd = jnp.concatenate([jnp.where(top, kt, zk), jnp.where(top, zk, kt)], axis=1)
    left = lax.broadcasted_iota(I32, vv.shape, 1) < HEAD_DIM
    zv = jnp.zeros_like(vv)
    vbd = jnp.concatenate([jnp.where(left, vv, zv), jnp.where(left, zv, vv)], axis=0)
    s = jnp.dot(jnp.concatenate([q2a, q2b], axis=0), kbd, preferred_element_type=F32)
    p_rows, scales = [], []
    for t in range(2):
        ph, rinv = [], []
        for hf in range(2):
            sh = s[t * tq:(t + 1) * tq, hf * nk:(hf + 1) * nk]
            if bias is not None:
                sh = jnp.concatenate([sh[:, :n_bias] + bias, sh[:, n_bias:]], axis=1)
            sk = sinks[2 * t + hf]
            m = jnp.maximum(jnp.max(sh, axis=-1, keepdims=True), sk)
            p = jnp.exp2(sh - m)
            rinv.append(1.0 / (jnp.sum(p, axis=-1, keepdims=True) + jnp.exp2(sk - m)))
            ph.append(p.astype(BF16))
        p_rows.append(jnp.concatenate(ph, axis=1))
        scales.append(jnp.where(lo, rinv[0], rinv[1]))
    o = jnp.dot(jnp.concatenate(p_rows, axis=0), vbd, preferred_element_type=F32)
    return o[0:tq] * scales[0], o[tq:2 * tq] * scales[1]


def _ctx_attn_kernel(sink_ref, q_ref, kt_ref, vv_ref, x_ref, mod_ref, wo_ref, o_ref, osc):
    for g in range(N_KV_HEADS):
        q2a = q_ref[:, (2 * g) * LANES:(2 * g + 1) * LANES]
        q2b = q_ref[:, (2 * g + 1) * LANES:(2 * g + 2) * LANES]
        sinks = [sink_ref[4 * g + j] * LOG2E for j in range(4)]
        oa, ob = _attn_group(q2a, q2b, kt_ref[g], vv_ref[:, g * LANES:(g + 1) * LANES], None, 0, sinks)
        osc[:, (2 * g) * LANES:(2 * g + 1) * LANES] = oa.astype(BF16)
        osc[:, (2 * g + 1) * LANES:(2 * g + 2) * LANES] = ob.astype(BF16)
    att = jnp.dot(osc[...], wo_ref[...], preferred_element_type=F32)
    o_ref[...] = x_ref[...] + mod_ref[2:3, :] * att


def _ctx_attention(sink, q, kt, vv, xp, seq, mod, wo):
    n_p, d = xp.shape
    nb = n_p // seq
    return pl.pallas_call(
        _ctx_attn_kernel,
        grid=(nb,),
        in_specs=[
            pl.BlockSpec(memory_space=pltpu.SMEM),
            pl.BlockSpec((seq, d), lambda b: (b, 0)),
            pl.BlockSpec((N_KV_HEADS, LANES, seq), lambda b: (0, 0, b)),
            pl.BlockSpec((seq, N_KV_HEADS * LANES), lambda b: (b, 0)),
            pl.BlockSpec((seq, d), lambda b: (b, 0)),
            pl.BlockSpec((None, 8, d), lambda b: (0, 0, 0)),
            _const_spec((d, d)),
        ],
        out_specs=pl.BlockSpec((seq, d), lambda b: (b, 0)),
        out_shape=jax.ShapeDtypeStruct((n_p, d), F32),
        scratch_shapes=[pltpu.VMEM((seq, d), BF16)],
        compiler_params=_params(("arbitrary",)),
        name="ctx_attention",
    )(sink, q, kt, vv, xp, mod, wo)


LAT_QB = 4


def _lat_attn_kernel(t_len, sink_ref, q_ref, ktp_ref, kto_ref, ktn_ref, vvp_ref, vvo_ref, vvn_ref,
                     ckt_ref, cvv_ref, x_ref, mod_ref, wo_ref, o_ref, osc):
    step = pl.program_id(1)
    tq = ATTN_BLOCK
    nloc = 3 * ATTN_BLOCK
    qi = lax.broadcasted_iota(I32, (tq, nloc), 0)
    kj = lax.broadcasted_iota(I32, (tq, nloc), 1)
    in_window = jnp.abs(qi + tq - kj) <= WINDOW
    for u in range(LAT_QB):
        n = step * LAT_QB + u
        kpos = (n - 1) * tq + kj
        bias = jnp.where(in_window & (kpos >= 0) & (kpos < t_len), 0.0, NEG_INF)
        rows = slice(u * tq, (u + 1) * tq)
        for g in range(N_KV_HEADS):
            vs = slice(g * LANES, (g + 1) * LANES)
            kts = [ktp_ref[g]] + [kto_ref[g, :, j * tq:(j + 1) * tq] for j in range(LAT_QB)] + [ktn_ref[g]]
            vvs = [vvp_ref[:, vs]] + [vvo_ref[j * tq:(j + 1) * tq, vs] for j in range(LAT_QB)] + [vvn_ref[:, vs]]
            kt = jnp.concatenate(kts[u:u + 3] + [ckt_ref[g]], axis=1)
            vv = jnp.concatenate(vvs[u:u + 3] + [cvv_ref[:, vs]], axis=0)
            q2a = q_ref[rows, (2 * g) * LANES:(2 * g + 1) * LANES]
            q2b = q_ref[rows, (2 * g + 1) * LANES:(2 * g + 2) * LANES]
            sinks = [sink_ref[4 * g + j] * LOG2E for j in range(4)]
            oa, ob = _attn_group(q2a, q2b, kt, vv, bias, nloc, sinks)
            osc[rows, (2 * g) * LANES:(2 * g + 1) * LANES] = oa.astype(BF16)
            osc[rows, (2 * g + 1) * LANES:(2 * g + 2) * LANES] = ob.astype(BF16)
    att = jnp.dot(osc[...], wo_ref[...], preferred_element_type=F32)
    o_ref[...] = x_ref[...] + mod_ref[2:3, :] * att


def _lat_attention(sink, q, kt, vv, ckt, cvv, xs, n_p, t_len, mod, wo):
    n_s, d = xs.shape
    db = n_s // t_len
    tq = ATTN_BLOCK
    ts = LAT_QB * tq
    nblk = t_len // tq
    nstep = t_len // ts
    assert n_p % ts == 0 and t_len % ts == 0
    off = n_p // tq
    npast = ckt.shape[3]
    g4 = N_KV_HEADS

    def nbr(delta):
        return lambda b, m: off + b * nblk + jnp.clip(m * LAT_QB + delta, 0, nblk - 1)

    own = lambda b, m: n_p // ts + b * nstep + m
    return pl.pallas_call(
        functools.partial(_lat_attn_kernel, t_len),
        grid=(db, nstep),
        in_specs=[
            pl.BlockSpec(memory_space=pltpu.SMEM),
            pl.BlockSpec((ts, d), lambda b, m: (own(b, m), 0)),
            pl.BlockSpec((g4, LANES, tq), lambda b, m: (0, 0, nbr(-1)(b, m))),
            pl.BlockSpec((g4, LANES, ts), lambda b, m: (0, 0, own(b, m))),
            pl.BlockSpec((g4, LANES, tq), lambda b, m: (0, 0, nbr(LAT_QB)(b, m))),
            pl.BlockSpec((tq, g4 * LANES), lambda b, m: (nbr(-1)(b, m), 0)),
            pl.BlockSpec((ts, g4 * LANES), lambda b, m: (own(b, m), 0)),
            pl.BlockSpec((tq, g4 * LANES), lambda b, m: (nbr(LAT_QB)(b, m), 0)),
            pl.BlockSpec((None, g4, LANES, npast), lambda b, m: (b, 0, 0, 0)),
            pl.BlockSpec((None, npast, g4 * LANES), lambda b, m: (b, 0, 0)),
            pl.BlockSpec((ts, d), lambda b, m: (b * nstep + m, 0)),
            pl.BlockSpec((None, 8, d), lambda b, m: (1 + b, 0, 0)),
            _const_spec((d, d)),
        ],
        out_specs=pl.BlockSpec((ts, d), lambda b, m: (b * nstep + m, 0)),
        out_shape=jax.ShapeDtypeStruct((n_s, d), F32),
        scratch_shapes=[pltpu.VMEM((ts, d), BF16)],
        compiler_params=_params(("arbitrary", "arbitrary")),
        name="lat_attention",
    )(sink, q, kt, kt, kt, vv, vv, vv, ckt, cvv, xs, mod, wo)


def _ffn_kernel(na, fc, xa_ref, xb_ref, mod_ref, g_ref, wg_ref, wu_ref, wd_ref, o_ref, acc):
    i = pl.program_id(0)
    x = jnp.where(i < na, xa_ref[...], xb_ref[...])
    a = _adaln(x, g_ref[...], mod_ref[3:4, :], mod_ref[4:5, :]).astype(BF16)
    nf = wg_ref.shape[1] // fc
    for f in range(nf):
        sl = slice(f * fc, (f + 1) * fc)
        hg = jnp.dot(a, wg_ref[:, sl], preferred_element_type=F32)
        hu = jnp.dot(a, wu_ref[:, sl], preferred_element_type=F32)
        act = (_silu(hg) * hu).astype(BF16)
        part = jnp.dot(act, wd_ref[sl, :], preferred_element_type=F32)
        if f == 0:
            acc[...] = part
        else:
            acc[...] += part
    o_ref[...] = x + mod_ref[5:6, :] * acc[...]


def _ffn(hp, hs, t_len, mod, g, wg, wu, wd):
    n_p, n_s = hp.shape[0], hs.shape[0]
    d, dff = wg.shape
    tm = _row_tile(n_p, t_len)
    na, nb = n_p // tm, n_s // tm
    nt_s = t_len // tm
    fc = 256
    assert dff % fc == 0
    mod_idx = lambda i: (jnp.where(i < na, 0, 1 + (i - na) // nt_s), 0, 0)
    return pl.pallas_call(
        functools.partial(_ffn_kernel, na, fc),
        grid=(na + nb,),
        in_specs=[
            pl.BlockSpec((tm, d), lambda i: (jnp.minimum(i, na - 1), 0)),
            pl.BlockSpec((tm, d), lambda i: (jnp.maximum(i - na, 0), 0)),
            pl.BlockSpec((None, 8, d), mod_idx),
            _const_spec((1, d)),
            _const_spec((d, dff)),
            _const_spec((d, dff)),
            _const_spec((dff, d)),
        ],
        out_specs=pl.BlockSpec((tm, d), lambda i: (i, 0)),
        out_shape=jax.ShapeDtypeStruct((n_p + n_s, d), F32),
        scratch_shapes=[pltpu.VMEM((tm, d), F32)],
        compiler_params=_params(("arbitrary",)),
        name="ffn",
    )(hp, hs, mod, g, wg, wu, wd)


def _split2(x):
    hi = x.astype(BF16)
    lo = (x - hi.astype(F32)).astype(BF16)
    return hi, lo


def _gla_in_kernel(x_ref, mod_ref, g_ref, w_ref, w1_ref, w2_ref, gb_ref, tri_ref,
                   qk_out, v_out, r_out, b_out):
    x = x_ref[...]
    tm = x.shape[0]
    a = _adaln(x, g_ref[...], mod_ref[0:1, :], mod_ref[1:2, :]).astype(BF16)
    hk = GLA_HEADS * GLA_DK
    hv = GLA_HEADS * GLA_DV
    q = jnp.dot(a, w_ref[:, 0:hk], preferred_element_type=F32)
    qk_out[:, 0:hk] = (q * (GLA_DK ** -0.5)).astype(BF16)
    qk_out[:, hk:2 * hk] = jnp.dot(a, w_ref[:, hk:2 * hk], preferred_element_type=F32).astype(BF16)
    v_out[...] = jnp.dot(a, w_ref[:, 2 * hk:2 * hk + hv], preferred_element_type=F32).astype(BF16)
    r_out[...] = jnp.dot(a, w_ref[:, 2 * hk + hv:2 * hk + 2 * hv], preferred_element_type=F32).astype(BF16)
    z1 = jnp.dot(a, w1_ref[...], preferred_element_type=F32)
    z = jnp.dot(z1.astype(BF16), w2_ref[...], preferred_element_type=F32) + gb_ref[...]
    gate = (jnp.minimum(z, 0.0) - jnp.log(1.0 + jnp.exp(-jnp.abs(z)))) * (1.0 / GLA_GATE_TAU)
    c = GLA_CHUNK
    tri = tri_ref[...]
    for j in range(tm // c):
        rows = slice(j * c, (j + 1) * c)
        for dr in range(2):
            cols = slice(dr * hk, (dr + 1) * hk)
            hi, lo = _split2(gate[rows, cols])
            b_out[rows, cols] = (jnp.dot(tri[dr], hi, preferred_element_type=F32)
                                 + jnp.dot(tri[dr], lo, preferred_element_type=F32))


def _gla_in(h, n_p, t_len, mod, g, w_in, w1, w2, gate_b):
    n, d = h.shape
    tm = _row_tile(n_p, t_len, cap=1024)
    na = n_p // tm
    nt_s = t_len // tm
    hk, hv = GLA_HEADS * GLA_DK, GLA_HEADS * GLA_DV
    c = GLA_CHUNK
    lower = jnp.tril(jnp.ones((c, c), F32))
    tri = jnp.stack([lower, lower.T]).astype(BF16)
    mod_idx = lambda i: (jnp.where(i < na, 0, 1 + (i - na) // nt_s), 0, 0)
    row = lambda i: (i, 0)
    return pl.pallas_call(
        _gla_in_kernel,
        grid=(n // tm,),
        in_specs=[
            pl.BlockSpec((tm, d), row),
            pl.BlockSpec((None, 8, d), mod_idx),
            _const_spec((1, d)),
            _const_spec(w_in.shape),
            _const_spec(w1.shape),
            _const_spec(w2.shape),
            _const_spec((1, 2 * hk)),
            _const_spec((2, c, c)),
        ],
        out_specs=[pl.BlockSpec((tm, 2 * hk), row), pl.BlockSpec((tm, hv), row),
                   pl.BlockSpec((tm, hv), row), pl.BlockSpec((tm, 2 * hk), row)],
        out_shape=[jax.ShapeDtypeStruct((n, 2 * hk), BF16), jax.ShapeDtypeStruct((n, hv), BF16),
                   jax.ShapeDtypeStruct((n, hv), BF16), jax.ShapeDtypeStruct((n, 2 * hk), F32)],
        compiler_params=_params(("arbitrary",)),
        name="gla_in",
    )(h, mod, g, w_in, w1, w2, gate_b, tri)


def _gla_direction(qk_ref, v_ref, b_ref, st, o_ref, forward, qd_s, att_s, upd_s, dec_s):
    c = GLA_CHUNK
    ncb = qk_ref.shape[0] // c
    hk = GLA_HEADS * GLA_DK
    dk, dv = GLA_DK, GLA_DV
    ri = lax.broadcasted_iota(I32, (c, c), 0)
    ci_ = lax.broadcasted_iota(I32, (c, c), 1)
    keep = (ci_ <= ri) if forward else (ci_ >= ri)

    for ch in range(ncb):
        rows = slice(ch * c, (ch + 1) * c)
        b = b_ref[rows, :]
        if forward:
            b_end, b_mid = b[c - 1:c, :], b[c // 2 - 1:c // 2, :]
        else:
            b_end, b_mid = b[0:1, :], b[c // 2:c // 2 + 1, :]
        qa = qk_ref[rows, 0:hk] * jnp.exp(b - b_mid)
        ka = qk_ref[rows, hk:2 * hk] * jnp.exp(b_mid - b)
        qd_s[rows, :] = (qa * jnp.exp(b_mid)).astype(BF16)
        kd = (ka * jnp.exp(b_end - b_mid)).astype(BF16)
        dec_s[ch:ch + 1, :] = jnp.exp(b_end)
        qab, kab = qa.astype(BF16), ka.astype(BF16)
        for h in range(GLA_HEADS):
            ks = slice(h * dk, (h + 1) * dk)
            vs = slice(h * dv, (h + 1) * dv)
            att = lax.dot_general(qab[:, ks], kab[:, ks], (((1,), (1,)), ((), ())),
                                  preferred_element_type=F32)
            att_s[ch, h] = jnp.where(keep, att, 0.0).astype(BF16)
            upd_s[ch, h] = lax.dot_general(v_ref[rows, vs], kd[:, ks], (((0,), (0,)), ((), ())),
                                           preferred_element_type=F32)

    for ch in (range(ncb) if forward else reversed(range(ncb))):
        rows = slice(ch * c, (ch + 1) * c)
        for h in range(GLA_HEADS):
            ks = slice(h * dk, (h + 1) * dk)
            vs = slice(h * dv, (h + 1) * dv)
            s = st[h]
            o = lax.dot_general(qd_s[rows, ks], s.astype(BF16), (((1,), (1,)), ((), ())),
                                preferred_element_type=F32)
            o = o + jnp.dot(att_s[ch, h], v_ref[rows, vs], preferred_element_type=F32)
            o_ref[rows, vs] = o.astype(o_ref.dtype)
            st[h] = s * dec_s[ch:ch + 1, ks] + upd_s[ch, h]


def _gla_scan_kernel(shared, from_zero, *refs):
    if shared:
        qkf_ref, vf_ref, bf_ref, bb_ref = refs[:4]
        qkb_ref, vb_ref = qkf_ref, vf_ref
        rest = refs[4:]
    else:
        qkf_ref, vf_ref, bf_ref, qkb_ref, vb_ref, bb_ref = refs[:6]
        rest = refs[6:]
    if not from_zero:
        sf0_ref, sb0_ref = rest[:2]
        rest = rest[2:]
    of_ref, ob_ref, sf_ref, sb_ref, stf, stb, qd_s, att_s, upd_s, dec_s = rest
    i = pl.program_id(1)
    nblk = pl.num_programs(1)

    @pl.when(i == 0)
    def _():
        if from_zero:
            stf[...] = jnp.zeros_like(stf)
            stb[...] = jnp.zeros_like(stb)
        else:
            for h in range(GLA_HEADS):
                stf[h] = sf0_ref[h].T
                stb[h] = sb0_ref[h].T

    _gla_direction(qkf_ref, vf_ref, bf_ref, stf, of_ref, True, qd_s, att_s, upd_s, dec_s)
    _gla_direction(qkb_ref, vb_ref, bb_ref, stb, ob_ref, False, qd_s, att_s, upd_s, dec_s)

    @pl.when(i == nblk - 1)
    def _():
        for h in range(GLA_HEADS):
            sf_ref[h] = stf[h].T
            sb_ref[h] = stb[h].T


def _gla_scan(qk, v, bc, sf0, sb0, b_off, nb, t_len):
    n = qk.shape[0]
    assert n % t_len == 0
    ns = n // t_len
    tb = _row_tile(t_len)
    nblk = t_len // tb
    qk3 = qk.reshape(ns, t_len, qk.shape[1])
    v3 = v.reshape(ns, t_len, v.shape[1])
    bc3 = bc.reshape(ns, t_len, bc.shape[1])
    h, dk, dv = GLA_HEADS, GLA_DK, GLA_DV
    hk = h * dk
    st_spec = pl.BlockSpec((None, h, dk, dv), lambda b, i: (b, 0, 0, 0))
    fwd = lambda b, i: (b_off + b, i, 0)
    bwd = lambda b, i: (b_off + b, nblk - 1 - i, 0)
    bwd_b = pl.BlockSpec((None, tb, hk), lambda b, i: (b_off + b, nblk - 1 - i, 1))
    shared = nblk == 1
    in_specs = [pl.BlockSpec((None, tb, 2 * hk), fwd), pl.BlockSpec((None, tb, h * dv), fwd),
                pl.BlockSpec((None, tb, hk), fwd)]
    operands = [qk3, v3, bc3]
    if not shared:
        in_specs += [pl.BlockSpec((None, tb, 2 * hk), bwd), pl.BlockSpec((None, tb, h * dv), bwd)]
        operands += [qk3, v3]
    in_specs += [bwd_b]
    operands += [bc3]
    from_zero = sf0 is None
    if not from_zero:
        in_specs += [st_spec, st_spec]
        operands += [sf0, sb0]
    return pl.pallas_call(
        functools.partial(_gla_scan_kernel, shared, from_zero),
        grid=(nb, nblk),
        in_specs=in_specs,
        out_specs=[pl.BlockSpec((None, tb, h * dv), lambda b, i: (b, i, 0)),
                   pl.BlockSpec((None, tb, h * dv), lambda b, i: (b, nblk - 1 - i, 0)),
                   st_spec, st_spec],
        out_shape=[jax.ShapeDtypeStruct((nb, t_len, h * dv), BF16),
                   jax.ShapeDtypeStruct((nb, t_len, h * dv), BF16),
                   jax.ShapeDtypeStruct((nb, h, dk, dv), F32),
                   jax.ShapeDtypeStruct((nb, h, dk, dv), F32)],
        scratch_shapes=[pltpu.VMEM((h, dv, dk), F32), pltpu.VMEM((h, dv, dk), F32),
                        pltpu.VMEM((tb, hk), BF16),
                        pltpu.VMEM((tb // GLA_CHUNK, h, GLA_CHUNK, GLA_CHUNK), BF16),
                        pltpu.VMEM((tb // GLA_CHUNK, h, dv, dk), F32),
                        pltpu.VMEM((max(tb // GLA_CHUNK, 8), hk), F32)],
        compiler_params=_params(("arbitrary", "arbitrary")),
        name="gla_scan",
    )(*operands)


GLA_OUT_PARTS = 4


def _gla_out_kernel(na, ofa_ref, oba_ref, ofb_ref, obb_ref, r_ref, h_ref, mod_ref, on_ref, wo_ref,
                    g2_ref, rw_ref, tri_ref, h_out, f_out, route_out, wt_out, cnt_out, run):
    i = pl.program_id(0)
    is_ctx = i < na

    @pl.when(i == 0)
    def _():
        run[...] = jnp.zeros_like(run)

    tm = tri_ref.shape[0]
    for part in range(h_ref.shape[0] // tm):
        rows = slice(part * tm, (part + 1) * tm)
        _gla_out_rows(rows, tm, is_ctx, ofa_ref, oba_ref, ofb_ref, obb_ref, r_ref, h_ref, mod_ref, on_ref,
                      wo_ref, g2_ref, rw_ref, tri_ref, h_out, f_out, route_out, wt_out, run)
    cnt_out[...] = run[...]


def _gla_out_rows(rows, tm, is_ctx, ofa_ref, oba_ref, ofb_ref, obb_ref, r_ref, h_ref, mod_ref, on_ref,
                  wo_ref, g2_ref, rw_ref, tri_ref, h_out, f_out, route_out, wt_out, run):
    o = (jnp.where(is_ctx, ofa_ref[rows, :], ofb_ref[rows, :]).astype(F32)
         + jnp.where(is_ctx, oba_ref[rows, :], obb_ref[rows, :]).astype(F32))
    dv = GLA_DV
    parts = []
    for hh in range(GLA_HEADS):
        oh = o[:, hh * dv:(hh + 1) * dv]
        parts.append(_rms(oh) * on_ref[...])
    on = jnp.concatenate(parts, axis=1)
    r = r_ref[rows, :].astype(F32)
    gated = (on * _silu(r)).astype(BF16)
    h = h_ref[rows, :] + mod_ref[2:3, :] * jnp.dot(gated, wo_ref[...], preferred_element_type=F32)
    h_out[rows, :] = h
    f = _adaln(h, g2_ref[...], mod_ref[3:4, :], mod_ref[4:5, :])
    f_out[rows, :] = _pack_halves(f)
    fh = f.astype(BF16)
    fl = (f - fh.astype(F32)).astype(BF16)
    logits = (jnp.dot(fh, rw_ref[0], preferred_element_type=F32)
              + jnp.dot(fl, rw_ref[0], preferred_element_type=F32)
              + jnp.dot(fh, rw_ref[1], preferred_element_type=F32))
    lane = lax.broadcasted_iota(I32, (tm, LANES), 1)
    lane_f = lane.astype(F32)
    logits = jnp.where(lane < N_EXPERTS, logits, -jnp.inf)
    m1 = jnp.max(logits, axis=-1, keepdims=True)
    i1 = jnp.min(jnp.where(logits == m1, lane_f, float(LANES)), axis=-1, keepdims=True)
    rest = jnp.where(lane_f == i1, -jnp.inf, logits)
    m2 = jnp.max(rest, axis=-1, keepdims=True)
    i2 = jnp.min(jnp.where(rest == m2, lane_f, float(LANES)), axis=-1, keepdims=True)
    e2 = jnp.exp(m2 - m1)
    w1 = 1.0 / (1.0 + e2)
    w2 = e2 / (1.0 + e2)
    wt_out[rows, :] = jnp.where(lane == 0, w1, jnp.where(lane == 1, w2, 0.0))
    sel1 = lane_f == i1
    sel2 = lane_f == i2
    picked = jnp.where(sel1, 1.0, jnp.where(sel2, 1.0, 0.0))
    before = jnp.dot(tri_ref[...], picked.astype(BF16), preferred_element_type=F32) + run[0:1, :]
    rank1 = jnp.sum(jnp.where(sel1, before, 0.0), axis=-1, keepdims=True)
    rank2 = jnp.sum(jnp.where(sel2, before, 0.0), axis=-1, keepdims=True)
    run[...] = run[...] + jnp.sum(picked, axis=0, keepdims=True)
    routing = jnp.where(lane == 0, i1, jnp.where(lane == 1, i2,
                        jnp.where(lane == 2, rank1, jnp.where(lane == 3, rank2, 0.0))))
    route_out[:, rows] = routing.T[0:8, :]


def _gla_out(of_p, ob_p, of_s, ob_s, r, h, n_p, t_len, mod, out_norm, wo, g2, router_w):
    n, d = h.shape
    tm = _row_tile(n_p, t_len, cap=1024)
    na = n_p // tm
    nt_s = t_len // tm
    mod_idx = lambda i: (jnp.where(i < na, 0, 1 + (i - na) // nt_s), 0, 0)
    row = lambda i: (i, 0)
    ctx_row = lambda i: (jnp.minimum(i, na - 1), 0)
    lat_row = lambda i: (jnp.maximum(i - na, 0), 0)
    tp = tm // GLA_OUT_PARTS
    tri = jnp.tril(jnp.ones((tp, tp), F32), -1).astype(BF16)
    return pl.pallas_call(
        functools.partial(_gla_out_kernel, na),
        grid=(n // tm,),
        in_specs=[
            pl.BlockSpec((tm, d), ctx_row),
            pl.BlockSpec((tm, d), ctx_row),
            pl.BlockSpec((tm, d), lat_row),
            pl.BlockSpec((tm, d), lat_row),
            pl.BlockSpec((tm, d), row),
            pl.BlockSpec((tm, d), row),
            pl.BlockSpec((None, 8, d), mod_idx),
            _const_spec((1, GLA_DV)),
            _const_spec((d, d)),
            _const_spec((1, d)),
            _const_spec((2, d, LANES)),
            _const_spec((tp, tp)),
        ],
        out_specs=[pl.BlockSpec((tm, d), row), pl.BlockSpec((tm, d // 2), row),
                   pl.BlockSpec((8, tm), lambda i: (0, i)), pl.BlockSpec((tm, LANES), row),
                   _const_spec((8, LANES))],
        out_shape=[jax.ShapeDtypeStruct((n, d), F32), jax.ShapeDtypeStruct((n, d // 2), jnp.uint32),
                   jax.ShapeDtypeStruct((8, n), F32), jax.ShapeDtypeStruct((n, LANES), F32),
                   jax.ShapeDtypeStruct((8, LANES), F32)],
        scratch_shapes=[pltpu.VMEM((8, LANES), F32)],
        compiler_params=_params(("arbitrary",)),
        name="gla_out_router",
    )(of_p, ob_p, of_s, ob_s, r, h, mod, out_norm, wo, g2, router_w, tri)


SC_CORES = 2
SC_SUBCORES = 16
SC_CHUNK = 64


def _sc_gather_rows(table, idx):
    nw = SC_CORES * SC_SUBCORES
    b = idx.shape[0]
    d = table.shape[1]
    assert b % (nw * SC_CHUNK) == 0
    per_w = b // nw
    n_chunks = per_w // SC_CHUNK
    mesh = plsc.VectorSubcoreMesh(core_axis_name="c", subcore_axis_name="s",
                                  num_cores=SC_CORES, num_subcores=SC_SUBCORES)

    assert n_chunks % 2 == 0

    def body(table_hbm, idx_hbm, out_hbm, idx_v, rows_v, gsem, wsem):
        wid = lax.axis_index("s") * SC_CORES + lax.axis_index("c")
        base = wid * per_w
        pltpu.sync_copy(idx_hbm.at[wid], idx_v)

        def gather(j, slot):
            return pltpu.make_async_copy(table_hbm.at[idx_v.at[j]], rows_v.at[slot], gsem.at[slot])

        def write(j, slot):
            return pltpu.make_async_copy(rows_v.at[slot], out_hbm.at[pl.ds(base + j * SC_CHUNK, SC_CHUNK)],
                                         wsem.at[slot])

        gather(0, 0).start()

        @pl.loop(0, n_chunks, step=2)
        def _(j):
            for slot in range(2):
                jj = j + slot
                gather(jj, slot).wait()

                @pl.when(jj + 1 < n_chunks)
                def _():
                    @pl.when(jj >= 1)
                    def _():
                        write(jj - 1, 1 - slot).wait()

                    gather(jj + 1, 1 - slot).start()

                write(jj, slot).start()

        write(n_chunks - 2, 0).wait()
        write(n_chunks - 1, 1).wait()

    return pl.kernel(
        body,
        out_type=jax.ShapeDtypeStruct((b, d), table.dtype),
        mesh=mesh,
        scratch_types=[pltpu.VMEM((n_chunks, SC_CHUNK), I32),
                       pltpu.VMEM((2, SC_CHUNK, d), table.dtype),
                       pltpu.SemaphoreType.DMA((2,)),
                       pltpu.SemaphoreType.DMA((2,))],
        name="sc_gather_rows",
    )(table, idx.reshape(nw, n_chunks, SC_CHUNK))


def _sc_scatter_rows(rows, idx2, p):
    nw = SC_CORES * SC_SUBCORES
    n, d = rows.shape
    assert n % (nw * SC_CHUNK) == 0
    per_w = n // nw
    n_chunks = per_w // SC_CHUNK
    mesh = plsc.VectorSubcoreMesh(core_axis_name="c", subcore_axis_name="s",
                                  num_cores=SC_CORES, num_subcores=SC_SUBCORES)

    assert n_chunks % 2 == 0

    def body(rows_hbm, idx0_hbm, idx1_hbm, out_hbm, idx0_v, idx1_v, rows_v, rsem, ssem):
        wid = lax.axis_index("s") * SC_CORES + lax.axis_index("c")
        base = wid * per_w
        pltpu.sync_copy(idx0_hbm.at[wid], idx0_v)
        pltpu.sync_copy(idx1_hbm.at[wid], idx1_v)

        def read(j, slot):
            return pltpu.make_async_copy(rows_hbm.at[pl.ds(base + j * SC_CHUNK, SC_CHUNK)], rows_v.at[slot],
                                         rsem.at[slot])

        read(0, 0).start()

        @pl.loop(0, n_chunks, step=2)
        def _(j):
            for slot in range(2):
                jj = j + slot
                read(jj, slot).wait()

                @pl.when(jj + 1 < n_chunks)
                def _():
                    read(jj + 1, 1 - slot).start()

                s0 = pltpu.make_async_copy(rows_v.at[slot], out_hbm.at[idx0_v.at[jj]], ssem.at[0])
                s1 = pltpu.make_async_copy(rows_v.at[slot], out_hbm.at[idx1_v.at[jj]], ssem.at[1])
                s0.start()
                s1.start()
                s0.wait()
                s1.wait()

    return pl.kernel(
        body,
        out_type=jax.ShapeDtypeStruct((p, d), rows.dtype),
        mesh=mesh,
        scratch_types=[pltpu.VMEM((n_chunks, SC_CHUNK), I32),
                       pltpu.VMEM((n_chunks, SC_CHUNK), I32),
                       pltpu.VMEM((2, SC_CHUNK, d), rows.dtype),
                       pltpu.SemaphoreType.DMA((2,)),
                       pltpu.SemaphoreType.DMA((2,))],
        name="sc_scatter_rows",
    )(rows, idx2[0].reshape(nw, n_chunks, SC_CHUNK), idx2[1].reshape(nw, n_chunks, SC_CHUNK))


EXPERT_TILE = 1024
EXPERT_SUB = 256
EXPERT_FF = 512
EXPERT_VMEM = 60 * 1024 * 1024


def _pack_halves(x):
    k = x.shape[1] // 2
    lo = lax.bitcast_convert_type(x[:, :k].astype(BF16).astype(F32), jnp.uint32)
    hi = lax.bitcast_convert_type(x[:, k:].astype(BF16).astype(F32), jnp.uint32)
    return (lo >> 16) | (hi & jnp.uint32(0xFFFF0000))


def _unpack_halves(w):
    lo = lax.bitcast_convert_type(w << 16, F32).astype(BF16)
    hi = lax.bitcast_convert_type(w & jnp.uint32(0xFFFF0000), F32).astype(BF16)
    return lo, hi


def _expert_kernel(nf, te_ref, ts_ref, nu_ref, nv_ref, x_ref, wg_ref, wu_ref, wd_ref, y_ref, acc, xb):
    i = pl.program_id(0)
    f = pl.program_id(1)
    tm = x_ref.shape[0]
    half = x_ref.shape[1]
    sub = EXPERT_SUB
    nvalid = nv_ref[i]

    def compute(nrows, first, last):
        rows = slice(0, nrows)
        if first:
            rid = lax.broadcasted_iota(I32, (nrows, half), 0)
            lo, hi = _unpack_halves(x_ref[rows, :])
            zero = jnp.zeros_like(lo)
            xb[rows, 0:half] = jnp.where(rid < nvalid, lo, zero)
            xb[rows, half:2 * half] = jnp.where(rid < nvalid, hi, zero)
        x = xb[rows, :]
        hg = jnp.dot(x, wg_ref[...].astype(BF16), preferred_element_type=F32)
        hu = jnp.dot(x, wu_ref[...].astype(BF16), preferred_element_type=F32)
        act = (_silu(hg) * hu).astype(BF16)
        part = jnp.dot(act, wd_ref[...].astype(BF16), preferred_element_type=F32)
        total = part if first else acc[rows, :] + part
        if last:
            y_ref[rows, :] = _pack_halves(total)
        else:
            acc[rows, :] = total

    assert nf >= 2
    for k in range(1, tm // sub + 1):
        has_k = (nvalid > (k - 1) * sub) & (nvalid <= k * sub)

        @pl.when(has_k & (f == 0))
        def _():
            compute(k * sub, True, False)

        @pl.when(has_k & (f > 0) & (f < nf - 1))
        def _():
            compute(k * sub, False, False)

        @pl.when(has_k & (f == nf - 1))
        def _():
            compute(k * sub, False, True)
            if k * sub < tm:
                y_ref[k * sub:tm, :] = jnp.zeros((tm - k * sub, half), y_ref.dtype)

    @pl.when((nvalid == 0) & (f == nf - 1))
    def _():
        y_ref[...] = jnp.zeros_like(y_ref)


def _experts(x, tile_expert, tile_src, n_used, n_valid, wg, wu, wd, tm, tf):
    p, half = x.shape
    d = 2 * half
    ne, _, dff = wg.shape
    nf = dff // tf
    assert dff % tf == 0 and tm % EXPERT_SUB == 0

    def fidx(i, f, nu):
        return jnp.where(i < nu[0], f, nf - 1)

    grid_spec = pltpu.PrefetchScalarGridSpec(
        num_scalar_prefetch=4,
        grid=(p // tm, nf),
        in_specs=[
            pl.BlockSpec((tm, half), lambda i, f, te, ts, nu, nv: (ts[i], 0)),
            pl.BlockSpec((None, d, tf), lambda i, f, te, ts, nu, nv: (te[i], 0, fidx(i, f, nu))),
            pl.BlockSpec((None, d, tf), lambda i, f, te, ts, nu, nv: (te[i], 0, fidx(i, f, nu))),
            pl.BlockSpec((None, tf, d), lambda i, f, te, ts, nu, nv: (te[i], fidx(i, f, nu), 0)),
        ],
        out_specs=pl.BlockSpec((tm, half), lambda i, f, te, ts, nu, nv: (i, 0)),
        scratch_shapes=[pltpu.VMEM((tm, d), F32), pltpu.VMEM((tm, d), BF16)],
    )
    return pl.pallas_call(
        functools.partial(_expert_kernel, nf),
        grid_spec=grid_spec,
        out_shape=jax.ShapeDtypeStruct((p, half), jnp.uint32),
        compiler_params=_params(("arbitrary", "arbitrary"), EXPERT_VMEM),
        name="moe_experts",
    )(tile_expert, tile_src, n_used, n_valid, x, wg, wu, wd)


def _combine_kernel(wt_ref, h_ref, mod_ref, y0_ref, y1_ref, o_ref):
    wt = wt_ref[...]
    half = y0_ref.shape[1]
    lo0, hi0 = _unpack_halves(y0_ref[...])
    lo1, hi1 = _unpack_halves(y1_ref[...])
    w0, w1 = wt[:, 0:1], wt[:, 1:2]
    h = h_ref[...]
    gate = mod_ref[5:6, :]
    o_ref[:, 0:half] = h[:, 0:half] + gate[:, 0:half] * (w0 * lo0.astype(F32) + w1 * lo1.astype(F32))
    o_ref[:, half:] = h[:, half:] + gate[:, half:] * (w0 * hi0.astype(F32) + w1 * hi1.astype(F32))


def _combine(yg, wt, h, mod, row_off, n_rows, n_mod, mod_off, tm):
    d = h.shape[1]
    nt = n_rows // tm
    toff = row_off // tm
    per_mod = n_rows // n_mod // tm
    return pl.pallas_call(
        _combine_kernel,
        grid=(nt,),
        in_specs=[
            pl.BlockSpec((tm, LANES), lambda i: (toff + i, 0)),
            pl.BlockSpec((tm, d), lambda i: (toff + i, 0)),
            pl.BlockSpec((None, 8, d), lambda i: (mod_off + i // per_mod, 0, 0)),
            pl.BlockSpec((None, tm, d // 2), lambda i: (0, i, 0)),
            pl.BlockSpec((None, tm, d // 2), lambda i: (1, i, 0)),
        ],
        out_specs=pl.BlockSpec((tm, d), lambda i: (i, 0)),
        out_shape=jax.ShapeDtypeStruct((n_rows, d), F32),
        compiler_params=_params(("arbitrary",)),
        name="moe_combine",
    )(wt, h, mod, yg, yg)


def _route(route, counts, n, tm):
    cnt = counts[0, :N_EXPERTS].astype(I32)
    padded = ((cnt + tm - 1) // tm) * tm
    gend = jnp.cumsum(padded)
    goff = gend - padded
    e = route[0:2].astype(I32)
    rank = route[2:4].astype(I32)
    onehot = e[:, :, None] == jnp.arange(N_EXPERTS, dtype=I32)[None, None, :]
    dest = jnp.sum(jnp.where(onehot, goff[None, None, :], 0), axis=-1) + rank
    p = 2 * n + N_EXPERTS * tm
    n_used = gend[-1] // tm
    tiles = jnp.arange(p // tm, dtype=I32)
    tile_src = jnp.minimum(tiles, n_used - 1)
    tile_expert = jnp.minimum(jnp.sum((gend[None, :] <= (tile_src * tm)[:, None]).astype(I32), axis=1),
                              N_EXPERTS - 1)
    used = goff[tile_expert] + cnt[tile_expert]
    n_valid = jnp.where(tiles < n_used, jnp.clip(used - tiles * tm, 0, tm), 0).astype(I32)
    return dest, p, tile_expert, tile_src, n_used.reshape(1).astype(I32), n_valid


def kernel(x_prompt, x_sample, cache_k, cache_v, state_fwd, state_bwd, c, c_ctx, ada_w, ada_b, norm1_g, norm2_g, attn_w_qkv, attn_q_norm, attn_k_norm, attn_sink, attn_w_o, gla_w_in, gla_gate_w1, gla_gate_w2, gla_gate_b, gla_out_norm, gla_w_o, ffn_w_gate, ffn_w_up, ffn_w_down, moe_router, moe_w_gate, moe_w_up, moe_w_down):
    bp, seq, d = x_prompt.shape
    db, t_len, _ = x_sample.shape
    n_p, n_s = bp * seq, db * t_len
    n = n_p + n_s
    xp = x_prompt.reshape(n_p, d)
    xs = x_sample.reshape(n_s, d)

    cond = jnp.concatenate([c_ctx[None, :], c], axis=0)
    assert cond.shape[0] <= 8
    cond_t = jnp.pad(cond, ((0, 8 - cond.shape[0]), (0, 0))).T
    mods = _modulation(cond_t, cond.shape[0], ada_w, ada_b)

    nk = N_KV_HEADS * HEAD_DIM
    qn = jnp.tile(attn_q_norm[0], N_HEADS)[None, :]
    kn = jnp.tile(attn_k_norm[0], N_KV_HEADS)[None, :]
    q, kt, vv, ck_new, cv_new = _qkv(xp, xs, t_len, mods[0], norm1_g[0][None, :],
                                     attn_w_qkv[0].astype(BF16), qn, kn)
    wo0 = attn_w_o[0].astype(BF16)
    sink = attn_sink[0]
    hp = _ctx_attention(sink, q, kt, vv, xp, seq, mods[0], wo0)
    ck = cache_k[:, 0].astype(BF16)
    cv = cache_v[:, 0].astype(BF16)
    ckt = jnp.concatenate([ck, ck], axis=-1).transpose(0, 2, 3, 1)
    cvv = jnp.concatenate([cv, cv], axis=-1).reshape(db, cv.shape[1], N_KV_HEADS * LANES)
    hs = _lat_attention(sink, q, kt, vv, ckt, cvv, xs, n_p, t_len, mods[0], wo0)
    new_k = ck_new.reshape(bp, 1, seq, N_KV_HEADS, HEAD_DIM)
    new_v = cv_new.reshape(bp, 1, seq, N_KV_HEADS, HEAD_DIM)
    h = _ffn(hp, hs, t_len, mods[0], norm2_g[0][None, :], ffn_w_gate[0].astype(BF16),
             ffn_w_up[0].astype(BF16), ffn_w_down[0].astype(BF16))

    hk = GLA_HEADS * GLA_DK
    rank = GLA_GATE_RANK
    w1 = jnp.concatenate([gla_gate_w1[0, 0], gla_gate_w1[0, 1]], axis=1)
    w1 = jnp.pad(w1, ((0, 0), (0, LANES - 2 * rank))).astype(BF16)
    w2 = jnp.zeros((LANES, 2 * hk), F32)
    w2 = w2.at[0:rank, 0:hk].set(gla_gate_w2[0, 0]).at[rank:2 * rank, hk:].set(gla_gate_w2[0, 1]).astype(BF16)
    gate_b = gla_gate_b[0].reshape(1, 2 * hk)
    qk, v, r, bc = _gla_in(h, n_p, t_len, mods[1], norm1_g[1][None, :], gla_w_in[0].astype(BF16),
                           w1, w2, gate_b)
    of_p, ob_p, new_sf, new_sb = _gla_scan(qk, v, bc, None, None, 0, bp, seq)
    of_s, ob_s, _, _ = _gla_scan(qk, v, bc, state_fwd[:, 0], state_bwd[:, 0], n_p // t_len, db, t_len)
    rw = jnp.pad(moe_router[0], ((0, 0), (0, LANES - N_EXPERTS)))
    rw_hi = rw.astype(BF16)
    rw_lo = (rw - rw_hi.astype(F32)).astype(BF16)
    h, f, route, wt, counts = _gla_out(
        of_p.reshape(n_p, d), ob_p.reshape(n_p, d), of_s.reshape(n_s, d), ob_s.reshape(n_s, d), r, h,
        n_p, t_len, mods[1], gla_out_norm[0][None, :], gla_w_o[0].astype(BF16), norm2_g[1][None, :],
        jnp.stack([rw_hi, rw_lo]))

    tm_e = EXPERT_TILE
    dest_k, p_rows, tile_expert, tile_src, n_used, n_valid = _route(route, counts, n, tm_e)
    xg = _sc_scatter_rows(f, dest_k, p_rows)
    xg, new_k = lax.optimization_barrier((xg, new_k))
    y = _experts(xg, tile_expert, tile_src, n_used, n_valid, moe_w_gate[0], moe_w_up[0],
                 moe_w_down[0], tm_e, EXPERT_FF)
    yg_p = _sc_gather_rows(y, dest_k[:, :n_p].reshape(2 * n_p)).reshape(2, n_p, d // 2)
    yg_s = _sc_gather_rows(y, dest_k[:, n_p:].reshape(2 * n_s)).reshape(2, n_s, d // 2)
    tm_c = _row_tile(n_p, t_len)
    y_p = _combine(yg_p, wt, h, mods[1], 0, n_p, 1, 0, tm_c)
    y_s = _combine(yg_s, wt, h, mods[1], n_p, n_s, db, 1, tm_c)

    return (y_p.reshape(bp, seq, d), y_s.reshape(db, t_len, d), new_k, new_v,
            new_sf[:, None], new_sb[:, None])
```

```python
import functools
import math

import numpy as np
import jax
import jax.numpy as jnp
from jax import lax
from jax.experimental import pallas as pl
from jax.experimental.pallas import tpu as pltpu
from jax.experimental.pallas import tpu_sc as plsc

F32 = jnp.float32
BF16 = jnp.bfloat16
I32 = jnp.int32

D_MODEL = 1024
N_HEADS = 16
N_KV_HEADS = 4
HEAD_DIM = 64
GRID_W = 64
WINDOW = 128
ATTN_BLOCK = 128
ROPE_THETA = 10000.0
GLA_HEADS = 4
GLA_DK = 128
GLA_DV = 256
GLA_GATE_RANK = 16
GLA_GATE_TAU = 16.0
GLA_CHUNK = 64
N_EXPERTS = 8
NORM_EPS = 1e-6
NEG_INF = -1e30

LANES = 128
VMEM_LIMIT = 56 * 1024 * 1024


def _params(sem, vmem=VMEM_LIMIT):
    return pltpu.CompilerParams(dimension_semantics=sem, vmem_limit_bytes=vmem)


def _row_tile(*counts, cap=512):
    t = cap
    while any(c % t for c in counts):
        t //= 2
    assert t >= 8
    return t


def _rms(x):
    return x * lax.rsqrt(jnp.mean(x * x, axis=-1, keepdims=True) + NORM_EPS)


def _adaln(x, g, shift, scale):
    return _rms(x) * (g * (1.0 + scale)) + shift


def _silu(x):
    return x * jax.nn.sigmoid(x)


def _const_spec(shape):
    nd = len(shape)
    return pl.BlockSpec(shape, lambda *_: (0,) * nd)


def _mod_kernel(n_cond, ct_ref, w_ref, b_ref, o_ref):
    ct = ct_ref[...]
    s = _silu(ct)
    w = w_ref[...]
    rows = [jnp.sum(w * s[:, r:r + 1], axis=0, keepdims=True) for r in range(n_cond)]
    rows += [jnp.zeros_like(rows[0])] * (8 - n_cond)
    o_ref[...] = jnp.concatenate(rows, axis=0) + b_ref[...]


def _modulation(cond_t, n_cond, ada_w, ada_b):
    depth, d, n6 = ada_w.shape
    tn = 2048
    out = pl.pallas_call(
        functools.partial(_mod_kernel, n_cond),
        grid=(depth, n6 // tn),
        in_specs=[
            pl.BlockSpec((d, 8), lambda l, j: (0, 0)),
            pl.BlockSpec((None, d, tn), lambda l, j: (l, 0, j)),
            pl.BlockSpec((None, 1, tn), lambda l, j: (l, 0, j)),
        ],
        out_specs=pl.BlockSpec((None, 8, tn), lambda l, j: (l, 0, j)),
        out_shape=jax.ShapeDtypeStruct((depth, 8, n6), F32),
        compiler_params=_params(("arbitrary", "arbitrary")),
        name="modulation",
    )(cond_t, ada_w, ada_b.reshape(depth, 1, n6))
    m = out[:, :n_cond].reshape(depth, n_cond, 6, d)
    return jnp.pad(m, ((0, 0), (0, 0), (0, 2), (0, 0)))


LOG2E = math.log2(math.e)


def _dup_half(k2, half):
    lane = lax.broadcasted_iota(I32, k2.shape, 1)
    lo = lane < HEAD_DIM
    r = pltpu.roll(k2, HEAD_DIM, 1)
    return jnp.where(lo, k2, r) if half == 0 else jnp.where(lo, r, k2)


QKV_PARTS = 2


def _qkv_kernel(na, xa_ref, xb_ref, mod_ref, g_ref, w_ref, qn_ref, kn_ref, bd_ref,
                cos_ref, sin_ref, q_out, kt_out, vv_out, ck_out, cv_out):
    i = pl.program_id(0)
    is_ctx = i < na
    tm = xa_ref.shape[0] // QKV_PARTS
    for part in range(QKV_PARTS):
        _qkv_rows(slice(part * tm, (part + 1) * tm), tm, is_ctx, xa_ref, xb_ref, mod_ref, g_ref, w_ref,
                  qn_ref, kn_ref, bd_ref, cos_ref, sin_ref, q_out, kt_out, vv_out, ck_out, cv_out)


def _qkv_rows(rows, tm, is_ctx, xa_ref, xb_ref, mod_ref, g_ref, w_ref, qn_ref, kn_ref, bd_ref,
              cos_ref, sin_ref, q_out, kt_out, vv_out, ck_out, cv_out):
    x = jnp.where(is_ctx, xa_ref[rows, :], xb_ref[rows, :])
    a = _adaln(x, g_ref[...], mod_ref[0:1, :], mod_ref[1:2, :])
    y = jnp.dot(a.astype(BF16), w_ref[...], preferred_element_type=F32)
    cos = jnp.where(is_ctx, 1.0, cos_ref[rows, :])
    sin = jnp.where(is_ctx, 0.0, sin_ref[rows, :])
    lane = lax.broadcasted_iota(I32, (tm, LANES), 1)
    first16 = (lane % 32) < 16

    def norm_rope(z, wt):
        ss = jnp.dot((z * z).astype(BF16), bd_ref[...], preferred_element_type=F32)
        zn = z * lax.rsqrt(ss * (1.0 / HEAD_DIM) + NORM_EPS) * wt
        outs = []
        for c in range(2):
            t = zn[:, c * LANES:(c + 1) * LANES]
            partner = jnp.where(first16, pltpu.roll(t, LANES - 16, 1), pltpu.roll(t, 16, 1))
            outs.append(t * cos + partner * sin)
        return jnp.concatenate(outs, axis=1)

    nq = N_HEADS * HEAD_DIM
    nk = N_KV_HEADS * HEAD_DIM
    scale = HEAD_DIM ** -0.5 * LOG2E
    for s in range(nq // 256):
        sl = slice(s * 256, (s + 1) * 256)
        q_out[rows, sl] = (norm_rope(y[:, sl], qn_ref[:, sl]) * scale).astype(BF16)
    k = norm_rope(y[:, nq:nq + nk], kn_ref[...])
    v = y[:, nq + nk:nq + 2 * nk]
    for g in range(N_KV_HEADS):
        c = g // 2
        kk = _dup_half(k[:, c * LANES:(c + 1) * LANES], g % 2)
        kt_out[g, :, rows] = kk.T.astype(BF16)
        vv_out[rows, g * LANES:(g + 1) * LANES] = _dup_half(v[:, c * LANES:(c + 1) * LANES], g % 2).astype(BF16)

    @pl.when(is_ctx)
    def _():
        seq = ck_out.shape[2]
        for s_i in range(tm // seq):
            b_loc = rows.start // seq + s_i
            for g in range(N_KV_HEADS):
                src = (slice(s_i * seq, (s_i + 1) * seq), slice(g * HEAD_DIM, (g + 1) * HEAD_DIM))
                ck_out[b_loc, 0, :, g, :] = k[src]
                cv_out[b_loc, 0, :, g, :] = v[src]


def _rope_tables(t_len):
    pos = np.arange(t_len)
    row = (pos // GRID_W).astype(np.float32)[:, None]
    col = (pos % GRID_W).astype(np.float32)[:, None]
    half = HEAD_DIM // 2
    inv = (np.float32(ROPE_THETA) ** (-np.arange(0, half, 2, dtype=np.float32) / np.float32(half)))[None, :]
    ar, ac = row * inv, col * inv
    cos = np.concatenate([np.cos(ar), np.cos(ar), np.cos(ac), np.cos(ac)], axis=1)
    sin = np.concatenate([-np.sin(ar), np.sin(ar), -np.sin(ac), np.sin(ac)], axis=1)
    return (jnp.asarray(np.tile(cos, (1, 2)), dtype=F32), jnp.asarray(np.tile(sin, (1, 2)), dtype=F32))


def _qkv(xp, xs, seq, t_len, mod, g, w, qn, kn):
    n_p, n_s = xp.shape[0], xs.shape[0]
    tm = _row_tile(n_p, t_len, cap=1024)
    assert (tm // QKV_PARTS) % seq == 0
    na, nb = n_p // tm, n_s // tm
    nt_s = t_len // tm
    n = n_p + n_s
    d = D_MODEL
    nq, nk = N_HEADS * HEAD_DIM, N_KV_HEADS * HEAD_DIM
    cos, sin = _rope_tables(t_len)
    eye4 = jnp.kron(jnp.eye(4, dtype=F32), jnp.ones((HEAD_DIM, HEAD_DIM), F32)).astype(BF16)
    mod_idx = lambda i: (jnp.where(i < na, 0, 1 + (i - na) // nt_s), 0, 0)
    pos_idx = lambda i: (jnp.maximum(i - na, 0) % nt_s, 0)
    return pl.pallas_call(
        functools.partial(_qkv_kernel, na),
        grid=(na + nb,),
        in_specs=[
            pl.BlockSpec((tm, d), lambda i: (jnp.minimum(i, na - 1), 0)),
            pl.BlockSpec((tm, d), lambda i: (jnp.maximum(i - na, 0), 0)),
            pl.BlockSpec((None, 8, d), mod_idx),
            _const_spec((1, d)),
            _const_spec((d, nq + 2 * nk)),
            _const_spec((1, nq)),
            _const_spec((1, nk)),
            _const_spec((256, 256)),
            pl.BlockSpec((tm, LANES), pos_idx),
            pl.BlockSpec((tm, LANES), pos_idx),
        ],
        out_specs=[pl.BlockSpec((tm, nq), lambda i: (i, 0)),
                   pl.BlockSpec((N_KV_HEADS, LANES, tm), lambda i: (0, 0, i)),
                   pl.BlockSpec((tm, N_KV_HEADS * LANES), lambda i: (i, 0)),
                   pl.BlockSpec((tm // seq, 1, seq, N_KV_HEADS, HEAD_DIM),
                                lambda i: (jnp.minimum(i, na - 1), 0, 0, 0, 0)),
                   pl.BlockSpec((tm // seq, 1, seq, N_KV_HEADS, HEAD_DIM),
                                lambda i: (jnp.minimum(i, na - 1), 0, 0, 0, 0))],
        out_shape=[jax.ShapeDtypeStruct((n, nq), BF16),
                   jax.ShapeDtypeStruct((N_KV_HEADS, LANES, n), BF16),
                   jax.ShapeDtypeStruct((n, N_KV_HEADS * LANES), BF16),
                   jax.ShapeDtypeStruct((n_p // seq, 1, seq, N_KV_HEADS, HEAD_DIM), F32),
                   jax.ShapeDtypeStruct((n_p // seq, 1, seq, N_KV_HEADS, HEAD_DIM), F32)],
        compiler_params=_params(("arbitrary",)),
        name="qkv",
    )(xp, xs, mod, g, w, qn, kn, eye4, cos, sin)


def _attn_group(q2a, q2b, kt, vv, bias, n_bias, sinks):
    tq = q2a.shape[0]
    nk = kt.shape[1]
    lo = lax.broadcasted_iota(I32, (tq, LANES), 1) < HEAD_DIM
    top = lax.broadcasted_iota(I32, kt.shape, 0) < HEAD_DIM
    zk = jnp.zeros_like(kt)
    kbd = jnp.concatenate([jnp.where(top, kt, zk), jnp.where(top, zk, kt)], axis=1)
    left = lax.broadcasted_iota(I32, vv.shape, 1) < HEAD_DIM
    zv = jnp.zeros_like(vv)
    vbd = jnp.concatenate([jnp.where(left, vv, zv), jnp.where(left, zv, vv)], axis=0)
    s = jnp.dot(jnp.concatenate([q2a, q2b], axis=0), kbd, preferred_element_type=F32)
    p_rows, scales = [], []
    for t in range(2):
        ph, rinv = [], []
        for hf in range(2):
            sh = s[t * tq:(t + 1) * tq, hf * nk:(hf + 1) * nk]
            if bias is not None:
                sh = jnp.concatenate([sh[:, :n_bias] + bias, sh[:, n_bias:]], axis=1)
            sk = sinks[2 * t + hf]
            m = jnp.maximum(jnp.max(sh, axis=-1, keepdims=True), sk)
            p = jnp.exp2(sh - m)
            rinv.append(1.0 / (jnp.sum(p, axis=-1, keepdims=True) + jnp.exp2(sk - m)))
            ph.append(p.astype(BF16))
        p_rows.append(jnp.concatenate(ph, axis=1))
        scales.append(jnp.where(lo, rinv[0], rinv[1]))
    o = jnp.dot(jnp.concatenate(p_rows, axis=0), vbd, preferred_element_type=F32)
    return o[0:tq] * scales[0], o[tq:2 * tq] * scales[1]


def _ctx_attn_kernel(sink_ref, q_ref, kt_ref, vv_ref, x_ref, mod_ref, wo_ref, o_ref, osc):
    for g in range(N_KV_HEADS):
        q2a = q_ref[:, (2 * g) * LANES:(2 * g + 1) * LANES]
        q2b = q_ref[:, (2 * g + 1) * LANES:(2 * g + 2) * LANES]
        sinks = [sink_ref[4 * g + j] * LOG2E for j in range(4)]
        oa, ob = _attn_group(q2a, q2b, kt_ref[g], vv_ref[:, g * LANES:(g + 1) * LANES], None, 0, sinks)
        osc[:, (2 * g) * LANES:(2 * g + 1) * LANES] = oa.astype(BF16)
        osc[:, (2 * g + 1) * LANES:(2 * g + 2) * LANES] = ob.astype(BF16)
    att = jnp.dot(osc[...], wo_ref[...], preferred_element_type=F32)
    o_ref[...] = x_ref[...] + mod_ref[2:3, :] * att


def _ctx_attention(sink, q, kt, vv, xp, seq, mod, wo):
    n_p, d = xp.shape
    nb = n_p // seq
    return pl.pallas_call(
        _ctx_attn_kernel,
        grid=(nb,),
        in_specs=[
            pl.BlockSpec(memory_space=pltpu.SMEM),
            pl.BlockSpec((seq, d), lambda b: (b, 0)),
            pl.BlockSpec((N_KV_HEADS, LANES, seq), lambda b: (0, 0, b)),
            pl.BlockSpec((seq, N_KV_HEADS * LANES), lambda b: (b, 0)),
            pl.BlockSpec((seq, d), lambda b: (b, 0)),
            pl.BlockSpec((None, 8, d), lambda b: (0, 0, 0)),
            _const_spec((d, d)),
        ],
        out_specs=pl.BlockSpec((seq, d), lambda b: (b, 0)),
        out_shape=jax.ShapeDtypeStruct((n_p, d), F32),
        scratch_shapes=[pltpu.VMEM((seq, d), BF16)],
        compiler_params=_params(("arbitrary",)),
        name="ctx_attention",
    )(sink, q, kt, vv, xp, mod, wo)


LAT_QB = 4


def _lat_attn_kernel(t_len, sink_ref, q_ref, ktp_ref, kto_ref, ktn_ref, vvp_ref, vvo_ref, vvn_ref,
                     ckt_ref, cvv_ref, x_ref, mod_ref, wo_ref, o_ref, osc):
    step = pl.program_id(1)
    tq = ATTN_BLOCK
    nloc = 3 * ATTN_BLOCK
    qi = lax.broadcasted_iota(I32, (tq, nloc), 0)
    kj = lax.broadcasted_iota(I32, (tq, nloc), 1)
    in_window = jnp.abs(qi + tq - kj) <= WINDOW
    for u in range(LAT_QB):
        n = step * LAT_QB + u
        kpos = (n - 1) * tq + kj
        bias = jnp.where(in_window & (kpos >= 0) & (kpos < t_len), 0.0, NEG_INF)
        rows = slice(u * tq, (u + 1) * tq)
        for g in range(N_KV_HEADS):
            vs = slice(g * LANES, (g + 1) * LANES)
            kts = [ktp_ref[g]] + [kto_ref[g, :, j * tq:(j + 1) * tq] for j in range(LAT_QB)] + [ktn_ref[g]]
            vvs = [vvp_ref[:, vs]] + [vvo_ref[j * tq:(j + 1) * tq, vs] for j in range(LAT_QB)] + [vvn_ref[:, vs]]
            kt = jnp.concatenate(kts[u:u + 3] + [ckt_ref[g]], axis=1)
            vv = jnp.concatenate(vvs[u:u + 3] + [cvv_ref[:, vs]], axis=0)
            q2a = q_ref[rows, (2 * g) * LANES:(2 * g + 1) * LANES]
            q2b = q_ref[rows, (2 * g + 1) * LANES:(2 * g + 2) * LANES]
            sinks = [sink_ref[4 * g + j] * LOG2E for j in range(4)]
            oa, ob = _attn_group(q2a, q2b, kt, vv, bias, nloc, sinks)
            osc[rows, (2 * g) * LANES:(2 * g + 1) * LANES] = oa.astype(BF16)
            osc[rows, (2 * g + 1) * LANES:(2 * g + 2) * LANES] = ob.astype(BF16)
    att = jnp.dot(osc[...], wo_ref[...], preferred_element_type=F32)
    o_ref[...] = x_ref[...] + mod_ref[2:3, :] * att


def _lat_attention(sink, q, kt, vv, ckt, cvv, xs, n_p, t_len, mod, wo):
    n_s, d = xs.shape
    db = n_s // t_len
    tq = ATTN_BLOCK
    ts = LAT_QB * tq
    nblk = t_len // tq
    nstep = t_len // ts
    assert n_p % ts == 0 and t_len % ts == 0
    off = n_p // tq
    npast = ckt.shape[3]
    g4 = N_KV_HEADS

    def nbr(delta):
        return lambda b, m: off + b * nblk + jnp.clip(m * LAT_QB + delta, 0, nblk - 1)

    own = lambda b, m: n_p // ts + b * nstep + m
    return pl.pallas_call(
        functools.partial(_lat_attn_kernel, t_len),
        grid=(db, nstep),
        in_specs=[
            pl.BlockSpec(memory_space=pltpu.SMEM),
            pl.BlockSpec((ts, d), lambda b, m: (own(b, m), 0)),
            pl.BlockSpec((g4, LANES, tq), lambda b, m: (0, 0, nbr(-1)(b, m))),
            pl.BlockSpec((g4, LANES, ts), lambda b, m: (0, 0, own(b, m))),
            pl.BlockSpec((g4, LANES, tq), lambda b, m: (0, 0, nbr(LAT_QB)(b, m))),
            pl.BlockSpec((tq, g4 * LANES), lambda b, m: (nbr(-1)(b, m), 0)),
            pl.BlockSpec((ts, g4 * LANES), lambda b, m: (own(b, m), 0)),
            pl.BlockSpec((tq, g4 * LANES), lambda b, m: (nbr(LAT_QB)(b, m), 0)),
            pl.BlockSpec((None, g4, LANES, npast), lambda b, m: (b, 0, 0, 0)),
            pl.BlockSpec((None, npast, g4 * LANES), lambda b, m: (b, 0, 0)),
            pl.BlockSpec((ts, d), lambda b, m: (b * nstep + m, 0)),
            pl.BlockSpec((None, 8, d), lambda b, m: (1 + b, 0, 0)),
            _const_spec((d, d)),
        ],
        out_specs=pl.BlockSpec((ts, d), lambda b, m: (b * nstep + m, 0)),
        out_shape=jax.ShapeDtypeStruct((n_s, d), F32),
        scratch_shapes=[pltpu.VMEM((ts, d), BF16)],
        compiler_params=_params(("arbitrary", "arbitrary")),
        name="lat_attention",
    )(sink, q, kt, kt, kt, vv, vv, vv, ckt, cvv, xs, mod, wo)


def _ffn_kernel(na, fc, xa_ref, xb_ref, mod_ref, g_ref, wg_ref, wu_ref, wd_ref, o_ref, acc):
    i = pl.program_id(0)
    x = jnp.where(i < na, xa_ref[...], xb_ref[...])
    a = _adaln(x, g_ref[...], mod_ref[3:4, :], mod_ref[4:5, :]).astype(BF16)
    nf = wg_ref.shape[1] // fc
    for f in range(nf):
        sl = slice(f * fc, (f + 1) * fc)
        hg = jnp.dot(a, wg_ref[:, sl], preferred_element_type=F32)
        hu = jnp.dot(a, wu_ref[:, sl], preferred_element_type=F32)
        act = (_silu(hg) * hu).astype(BF16)
        part = jnp.dot(act, wd_ref[sl, :], preferred_element_type=F32)
        if f == 0:
            acc[...] = part
        else:
            acc[...] += part
    o_ref[...] = x + mod_ref[5:6, :] * acc[...]


def _ffn(hp, hs, t_len, mod, g, wg, wu, wd):
    n_p, n_s = hp.shape[0], hs.shape[0]
    d, dff = wg.shape
    tm = _row_tile(n_p, t_len)
    na, nb = n_p // tm, n_s // tm
    nt_s = t_len // tm
    fc = 256
    assert dff % fc == 0
    mod_idx = lambda i: (jnp.where(i < na, 0, 1 + (i - na) // nt_s), 0, 0)
    return pl.pallas_call(
        functools.partial(_ffn_kernel, na, fc),
        grid=(na + nb,),
        in_specs=[
            pl.BlockSpec((tm, d), lambda i: (jnp.minimum(i, na - 1), 0)),
            pl.BlockSpec((tm, d), lambda i: (jnp.maximum(i - na, 0), 0)),
            pl.BlockSpec((None, 8, d), mod_idx),
            _const_spec((1, d)),
            _const_spec((d, dff)),
            _const_spec((d, dff)),
            _const_spec((dff, d)),
        ],
        out_specs=pl.BlockSpec((tm, d), lambda i: (i, 0)),
        out_shape=jax.ShapeDtypeStruct((n_p + n_s, d), F32),
        scratch_shapes=[pltpu.VMEM((tm, d), F32)],
        compiler_params=_params(("arbitrary",)),
        name="ffn",
    )(hp, hs, mod, g, wg, wu, wd)


def _split2(x):
    hi = x.astype(BF16)
    lo = (x - hi.astype(F32)).astype(BF16)
    return hi, lo


def _gla_in_kernel(x_ref, mod_ref, g_ref, w_ref, w1_ref, w2_ref, gb_ref, tri_ref,
                   qk_out, v_out, r_out, b_out):
    x = x_ref[...]
    tm = x.shape[0]
    a = _adaln(x, g_ref[...], mod_ref[0:1, :], mod_ref[1:2, :]).astype(BF16)
    hk = GLA_HEADS * GLA_DK
    hv = GLA_HEADS * GLA_DV
    q = jnp.dot(a, w_ref[:, 0:hk], preferred_element_type=F32)
    qk_out[:, 0:hk] = (q * (GLA_DK ** -0.5)).astype(BF16)
    qk_out[:, hk:2 * hk] = jnp.dot(a, w_ref[:, hk:2 * hk], preferred_element_type=F32).astype(BF16)
    v_out[...] = jnp.dot(a, w_ref[:, 2 * hk:2 * hk + hv], preferred_element_type=F32).astype(BF16)
    r_out[...] = jnp.dot(a, w_ref[:, 2 * hk + hv:2 * hk + 2 * hv], preferred_element_type=F32).astype(BF16)
    z1 = jnp.dot(a, w1_ref[...], preferred_element_type=F32)
    z = jnp.dot(z1.astype(BF16), w2_ref[...], preferred_element_type=F32) + gb_ref[...]
    gate = (jnp.minimum(z, 0.0) - jnp.log(1.0 + jnp.exp(-jnp.abs(z)))) * (1.0 / GLA_GATE_TAU)
    c = GLA_CHUNK
    tri = tri_ref[...]
    for j in range(tm // c):
        rows = slice(j * c, (j + 1) * c)
        for dr in range(2):
            cols = slice(dr * hk, (dr + 1) * hk)
            hi, lo = _split2(gate[rows, cols])
            b_out[rows, cols] = (jnp.dot(tri[dr], hi, preferred_element_type=F32)
                                 + jnp.dot(tri[dr], lo, preferred_element_type=F32))


def _gla_in(h, n_p, t_len, mod, g, w_in, w1, w2, gate_b):
    n, d = h.shape
    tm = _row_tile(n_p, t_len, cap=1024)
    na = n_p // tm
    nt_s = t_len // tm
    hk, hv = GLA_HEADS * GLA_DK, GLA_HEADS * GLA_DV
    c = GLA_CHUNK
    lower = jnp.tril(jnp.ones((c, c), F32))
    tri = jnp.stack([lower, lower.T]).astype(BF16)
    mod_idx = lambda i: (jnp.where(i < na, 0, 1 + (i - na) // nt_s), 0, 0)
    row = lambda i: (i, 0)
    return pl.pallas_call(
        _gla_in_kernel,
        grid=(n // tm,),
        in_specs=[
            pl.BlockSpec((tm, d), row),
            pl.BlockSpec((None, 8, d), mod_idx),
            _const_spec((1, d)),
            _const_spec(w_in.shape),
            _const_spec(w1.shape),
            _const_spec(w2.shape),
            _const_spec((1, 2 * hk)),
            _const_spec((2, c, c)),
        ],
        out_specs=[pl.BlockSpec((tm, 2 * hk), row), pl.BlockSpec((tm, hv), row),
                   pl.BlockSpec((tm, hv), row), pl.BlockSpec((tm, 2 * hk), row)],
        out_shape=[jax.ShapeDtypeStruct((n, 2 * hk), BF16), jax.ShapeDtypeStruct((n, hv), BF16),
                   jax.ShapeDtypeStruct((n, hv), BF16), jax.ShapeDtypeStruct((n, 2 * hk), F32)],
        compiler_params=_params(("arbitrary",)),
        name="gla_in",
    )(h, mod, g, w_in, w1, w2, gate_b, tri)


def _gla_direction(qk_ref, v_ref, b_ref, st, o_ref, forward, qd_s, att_s, upd_s, dec_s):
    c = GLA_CHUNK
    ncb = qk_ref.shape[0] // c
    hk = GLA_HEADS * GLA_DK
    dk, dv = GLA_DK, GLA_DV
    ri = lax.broadcasted_iota(I32, (c, c), 0)
    ci_ = lax.broadcasted_iota(I32, (c, c), 1)
    keep = (ci_ <= ri) if forward else (ci_ >= ri)

    for ch in range(ncb):
        rows = slice(ch * c, (ch + 1) * c)
        b = b_ref[rows, :]
        if forward:
            b_end, b_mid = b[c - 1:c, :], b[c // 2 - 1:c // 2, :]
        else:
            b_end, b_mid = b[0:1, :], b[c // 2:c // 2 + 1, :]
        qa = qk_ref[rows, 0:hk] * jnp.exp(b - b_mid)
        ka = qk_ref[rows, hk:2 * hk] * jnp.exp(b_mid - b)
        qd_s[rows, :] = (qa * jnp.exp(b_mid)).astype(BF16)
        kd = (ka * jnp.exp(b_end - b_mid)).astype(BF16)
        dec_s[ch:ch + 1, :] = jnp.exp(b_end)
        qab, kab = qa.astype(BF16), ka.astype(BF16)
        for h in range(GLA_HEADS):
            ks = slice(h * dk, (h + 1) * dk)
            vs = slice(h * dv, (h + 1) * dv)
            att = lax.dot_general(qab[:, ks], kab[:, ks], (((1,), (1,)), ((), ())),
                                  preferred_element_type=F32)
            att_s[ch, h] = jnp.where(keep, att, 0.0).astype(BF16)
            upd_s[ch, h] = lax.dot_general(v_ref[rows, vs], kd[:, ks], (((0,), (0,)), ((), ())),
                                           preferred_element_type=F32)

    for ch in (range(ncb) if forward else reversed(range(ncb))):
        rows = slice(ch * c, (ch + 1) * c)
        for h in range(GLA_HEADS):
            ks = slice(h * dk, (h + 1) * dk)
            vs = slice(h * dv, (h + 1) * dv)
            s = st[h]
            o = lax.dot_general(qd_s[rows, ks], s.astype(BF16), (((1,), (1,)), ((), ())),
                                preferred_element_type=F32)
            o = o + jnp.dot(att_s[ch, h], v_ref[rows, vs], preferred_element_type=F32)
            o_ref[rows, vs] = o.astype(o_ref.dtype)
            st[h] = s * dec_s[ch:ch + 1, ks] + upd_s[ch, h]


def _gla_scan_kernel(shared, from_zero, *refs):
    if shared:
        qkf_ref, vf_ref, bf_ref, bb_ref = refs[:4]
        qkb_ref, vb_ref = qkf_ref, vf_ref
        rest = refs[4:]
    else:
        qkf_ref, vf_ref, bf_ref, qkb_ref, vb_ref, bb_ref = refs[:6]
        rest = refs[6:]
    if not from_zero:
        sf0_ref, sb0_ref = rest[:2]
        rest = rest[2:]
    of_ref, ob_ref, sf_ref, sb_ref, stf, stb, qd_s, att_s, upd_s, dec_s = rest
    i = pl.program_id(1)
    nblk = pl.num_programs(1)

    @pl.when(i == 0)
    def _():
        if from_zero:
            stf[...] = jnp.zeros_like(stf)
            stb[...] = jnp.zeros_like(stb)
        else:
            for h in range(GLA_HEADS):
                stf[h] = sf0_ref[h].T
                stb[h] = sb0_ref[h].T

    _gla_direction(qkf_ref, vf_ref, bf_ref, stf, of_ref, True, qd_s, att_s, upd_s, dec_s)
    _gla_direction(qkb_ref, vb_ref, bb_ref, stb, ob_ref, False, qd_s, att_s, upd_s, dec_s)

    @pl.when(i == nblk - 1)
    def _():
        for h in range(GLA_HEADS):
            sf_ref[h] = stf[h].T
            sb_ref[h] = stb[h].T


def _gla_scan(qk, v, bc, sf0, sb0, b_off, nb, t_len):
    n = qk.shape[0]
    assert n % t_len == 0
    ns = n // t_len
    tb = _row_tile(t_len)
    nblk = t_len // tb
    qk3 = qk.reshape(ns, t_len, qk.shape[1])
    v3 = v.reshape(ns, t_len, v.shape[1])
    bc3 = bc.reshape(ns, t_len, bc.shape[1])
    h, dk, dv = GLA_HEADS, GLA_DK, GLA_DV
    hk = h * dk
    st_spec = pl.BlockSpec((None, h, dk, dv), lambda b, i: (b, 0, 0, 0))
    fwd = lambda b, i: (b_off + b, i, 0)
    bwd = lambda b, i: (b_off + b, nblk - 1 - i, 0)
    bwd_b = pl.BlockSpec((None, tb, hk), lambda b, i: (b_off + b, nblk - 1 - i, 1))
    shared = nblk == 1
    in_specs = [pl.BlockSpec((None, tb, 2 * hk), fwd), pl.BlockSpec((None, tb, h * dv), fwd),
                pl.BlockSpec((None, tb, hk), fwd)]
    operands = [qk3, v3, bc3]
    if not shared:
        in_specs += [pl.BlockSpec((None, tb, 2 * hk), bwd), pl.BlockSpec((None, tb, h * dv), bwd)]
        operands += [qk3, v3]
    in_specs += [bwd_b]
    operands += [bc3]
    from_zero = sf0 is None
    if not from_zero:
        in_specs += [st_spec, st_spec]
        operands += [sf0, sb0]
    return pl.pallas_call(
        functools.partial(_gla_scan_kernel, shared, from_zero),
        grid=(nb, nblk),
        in_specs=in_specs,
        out_specs=[pl.BlockSpec((None, tb, h * dv), lambda b, i: (b, i, 0)),
                   pl.BlockSpec((None, tb, h * dv), lambda b, i: (b, nblk - 1 - i, 0)),
                   st_spec, st_spec],
        out_shape=[jax.ShapeDtypeStruct((nb, t_len, h * dv), BF16),
                   jax.ShapeDtypeStruct((nb, t_len, h * dv), BF16),
                   jax.ShapeDtypeStruct((nb, h, dk, dv), F32),
                   jax.ShapeDtypeStruct((nb, h, dk, dv), F32)],
        scratch_shapes=[pltpu.VMEM((h, dv, dk), F32), pltpu.VMEM((h, dv, dk), F32),
                        pltpu.VMEM((tb, hk), BF16),
                        pltpu.VMEM((tb // GLA_CHUNK, h, GLA_CHUNK, GLA_CHUNK), BF16),
                        pltpu.VMEM((tb // GLA_CHUNK, h, dv, dk), F32),
                        pltpu.VMEM((max(tb // GLA_CHUNK, 8), hk), F32)],
        compiler_params=_params(("arbitrary", "arbitrary")),
        name="gla_scan",
    )(*operands)


GLA_OUT_PARTS = 4


def _gla_out_kernel(na, ofa_ref, oba_ref, ofb_ref, obb_ref, r_ref, h_ref, mod_ref, on_ref, wo_ref,
                    g2_ref, rw_ref, tri_ref, h_out, f_out, route_out, wt_out, cnt_out, run):
    i = pl.program_id(0)
    is_ctx = i < na

    @pl.when(i == 0)
    def _():
        run[...] = jnp.zeros_like(run)

    tm = tri_ref.shape[0]
    for part in range(h_ref.shape[0] // tm):
        rows = slice(part * tm, (part + 1) * tm)
        _gla_out_rows(rows, tm, is_ctx, ofa_ref, oba_ref, ofb_ref, obb_ref, r_ref, h_ref, mod_ref, on_ref,
                      wo_ref, g2_ref, rw_ref, tri_ref, h_out, f_out, route_out, wt_out, run)
    cnt_out[...] = run[...]


def _gla_out_rows(rows, tm, is_ctx, ofa_ref, oba_ref, ofb_ref, obb_ref, r_ref, h_ref, mod_ref, on_ref,
                  wo_ref, g2_ref, rw_ref, tri_ref, h_out, f_out, route_out, wt_out, run):
    o = (jnp.where(is_ctx, ofa_ref[rows, :], ofb_ref[rows, :]).astype(F32)
         + jnp.where(is_ctx, oba_ref[rows, :], obb_ref[rows, :]).astype(F32))
    dv = GLA_DV
    parts = []
    for hh in range(GLA_HEADS):
        oh = o[:, hh * dv:(hh + 1) * dv]
        parts.append(_rms(oh) * on_ref[...])
    on = jnp.concatenate(parts, axis=1)
    r = r_ref[rows, :].astype(F32)
    gated = (on * _silu(r)).astype(BF16)
    h = h_ref[rows, :] + mod_ref[2:3, :] * jnp.dot(gated, wo_ref[...], preferred_element_type=F32)
    h_out[rows, :] = h
    f = _adaln(h, g2_ref[...], mod_ref[3:4, :], mod_ref[4:5, :])
    f_out[rows, :] = _pack_halves(f)
    fh = f.astype(BF16)
    fl = (f - fh.astype(F32)).astype(BF16)
    logits = (jnp.dot(fh, rw_ref[0], preferred_element_type=F32)
              + jnp.dot(fl, rw_ref[0], preferred_element_type=F32)
              + jnp.dot(fh, rw_ref[1], preferred_element_type=F32))
    lane = lax.broadcasted_iota(I32, (tm, LANES), 1)
    lane_f = lane.astype(F32)
    logits = jnp.where(lane < N_EXPERTS, logits, -jnp.inf)
    m1 = jnp.max(logits, axis=-1, keepdims=True)
    i1 = jnp.min(jnp.where(logits == m1, lane_f, float(LANES)), axis=-1, keepdims=True)
    rest = jnp.where(lane_f == i1, -jnp.inf, logits)
    m2 = jnp.max(rest, axis=-1, keepdims=True)
    i2 = jnp.min(jnp.where(rest == m2, lane_f, float(LANES)), axis=-1, keepdims=True)
    e2 = jnp.exp(m2 - m1)
    w1 = 1.0 / (1.0 + e2)
    w2 = e2 / (1.0 + e2)
    wt_out[rows, :] = jnp.where(lane == 0, w1, jnp.where(lane == 1, w2, 0.0))
    sel1 = lane_f == i1
    sel2 = lane_f == i2
    picked = jnp.where(sel1, 1.0, jnp.where(sel2, 1.0, 0.0))
    before = jnp.dot(tri_ref[...], picked.astype(BF16), preferred_element_type=F32) + run[0:1, :]
    rank1 = jnp.sum(jnp.where(sel1, before, 0.0), axis=-1, keepdims=True)
    rank2 = jnp.sum(jnp.where(sel2, before, 0.0), axis=-1, keepdims=True)
    run[...] = run[...] + jnp.sum(picked, axis=0, keepdims=True)
    routing = jnp.where(lane == 0, i1, jnp.where(lane == 1, i2,
                        jnp.where(lane == 2, rank1, jnp.where(lane == 3, rank2, 0.0))))
    route_out[:, rows] = routing.T[0:8, :]


def _gla_out(of_p, ob_p, of_s, ob_s, r, h, n_p, t_len, mod, out_norm, wo, g2, router_w):
    n, d = h.shape
    tm = _row_tile(n_p, t_len, cap=1024)
    na = n_p // tm
    nt_s = t_len // tm
    mod_idx = lambda i: (jnp.where(i < na, 0, 1 + (i - na) // nt_s), 0, 0)
    row = lambda i: (i, 0)
    ctx_row = lambda i: (jnp.minimum(i, na - 1), 0)
    lat_row = lambda i: (jnp.maximum(i - na, 0), 0)
    tp = tm // GLA_OUT_PARTS
    tri = jnp.tril(jnp.ones((tp, tp), F32), -1).astype(BF16)
    return pl.pallas_call(
        functools.partial(_gla_out_kernel, na),
        grid=(n // tm,),
        in_specs=[
            pl.BlockSpec((tm, d), ctx_row),
            pl.BlockSpec((tm, d), ctx_row),
            pl.BlockSpec((tm, d), lat_row),
            pl.BlockSpec((tm, d), lat_row),
            pl.BlockSpec((tm, d), row),
            pl.BlockSpec((tm, d), row),
            pl.BlockSpec((None, 8, d), mod_idx),
            _const_spec((1, GLA_DV)),
            _const_spec((d, d)),
            _const_spec((1, d)),
            _const_spec((2, d, LANES)),
            _const_spec((tp, tp)),
        ],
        out_specs=[pl.BlockSpec((tm, d), row), pl.BlockSpec((tm, d // 2), row),
                   pl.BlockSpec((8, tm), lambda i: (0, i)), pl.BlockSpec((tm, LANES), row),
                   _const_spec((8, LANES))],
        out_shape=[jax.ShapeDtypeStruct((n, d), F32), jax.ShapeDtypeStruct((n, d // 2), jnp.uint32),
                   jax.ShapeDtypeStruct((8, n), F32), jax.ShapeDtypeStruct((n, LANES), F32),
                   jax.ShapeDtypeStruct((8, LANES), F32)],
        scratch_shapes=[pltpu.VMEM((8, LANES), F32)],
        compiler_params=_params(("arbitrary",)),
        name="gla_out_router",
    )(of_p, ob_p, of_s, ob_s, r, h, mod, out_norm, wo, g2, router_w, tri)


SC_CORES = 2
SC_SUBCORES = 16
SC_CHUNK = 64


def _sc_gather_rows(table, idx):
    nw = SC_CORES * SC_SUBCORES
    b = idx.shape[0]
    d = table.shape[1]
    assert b % (nw * SC_CHUNK) == 0
    per_w = b // nw
    n_chunks = per_w // SC_CHUNK
    mesh = plsc.VectorSubcoreMesh(core_axis_name="c", subcore_axis_name="s",
                                  num_cores=SC_CORES, num_subcores=SC_SUBCORES)

    assert n_chunks % 2 == 0

    def body(table_hbm, idx_hbm, out_hbm, idx_v, rows_v, gsem, wsem):
        wid = lax.axis_index("s") * SC_CORES + lax.axis_index("c")
        base = wid * per_w
        pltpu.sync_copy(idx_hbm.at[wid], idx_v)

        def gather(j, slot):
            return pltpu.make_async_copy(table_hbm.at[idx_v.at[j]], rows_v.at[slot], gsem.at[slot])

        def write(j, slot):
            return pltpu.make_async_copy(rows_v.at[slot], out_hbm.at[pl.ds(base + j * SC_CHUNK, SC_CHUNK)],
                                         wsem.at[slot])

        gather(0, 0).start()

        @pl.loop(0, n_chunks, step=2)
        def _(j):
            for slot in range(2):
                jj = j + slot
                gather(jj, slot).wait()

                @pl.when(jj + 1 < n_chunks)
                def _():
                    @pl.when(jj >= 1)
                    def _():
                        write(jj - 1, 1 - slot).wait()

                    gather(jj + 1, 1 - slot).start()

                write(jj, slot).start()

        write(n_chunks - 2, 0).wait()
        write(n_chunks - 1, 1).wait()

    return pl.kernel(
        body,
        out_type=jax.ShapeDtypeStruct((b, d), table.dtype),
        mesh=mesh,
        scratch_types=[pltpu.VMEM((n_chunks, SC_CHUNK), I32),
                       pltpu.VMEM((2, SC_CHUNK, d), table.dtype),
                       pltpu.SemaphoreType.DMA((2,)),
                       pltpu.SemaphoreType.DMA((2,))],
        name="sc_gather_rows",
    )(table, idx.reshape(nw, n_chunks, SC_CHUNK))


def _sc_scatter_rows(rows, idx2, p):
    nw = SC_CORES * SC_SUBCORES
    n, d = rows.shape
    assert n % (nw * SC_CHUNK) == 0
    per_w = n // nw
    n_chunks = per_w // SC_CHUNK
    mesh = plsc.VectorSubcoreMesh(core_axis_name="c", subcore_axis_name="s",
                                  num_cores=SC_CORES, num_subcores=SC_SUBCORES)

    assert n_chunks % 2 == 0

    def body(rows_hbm, idx0_hbm, idx1_hbm, out_hbm, idx0_v, idx1_v, rows_v, rsem, ssem):
        wid = lax.axis_index("s") * SC_CORES + lax.axis_index("c")
        base = wid * per_w
        pltpu.sync_copy(idx0_hbm.at[wid], idx0_v)
        pltpu.sync_copy(idx1_hbm.at[wid], idx1_v)

        def read(j, slot):
            return pltpu.make_async_copy(rows_hbm.at[pl.ds(base + j * SC_CHUNK, SC_CHUNK)], rows_v.at[slot],
                                         rsem.at[slot])

        read(0, 0).start()

        @pl.loop(0, n_chunks, step=2)
        def _(j):
            for slot in range(2):
                jj = j + slot
                read(jj, slot).wait()

                @pl.when(jj + 1 < n_chunks)
                def _():
                    read(jj + 1, 1 - slot).start()

                s0 = pltpu.make_async_copy(rows_v.at[slot], out_hbm.at[idx0_v.at[jj]], ssem.at[0])
                s1 = pltpu.make_async_copy(rows_v.at[slot], out_hbm.at[idx1_v.at[jj]], ssem.at[1])
                s0.start()
                s1.start()
                s0.wait()
                s1.wait()

    return pl.kernel(
        body,
        out_type=jax.ShapeDtypeStruct((p, d), rows.dtype),
        mesh=mesh,
        scratch_types=[pltpu.VMEM((n_chunks, SC_CHUNK), I32),
                       pltpu.VMEM((n_chunks, SC_CHUNK), I32),
                       pltpu.VMEM((2, SC_CHUNK, d), rows.dtype),
                       pltpu.SemaphoreType.DMA((2,)),
                       pltpu.SemaphoreType.DMA((2,))],
        name="sc_scatter_rows",
    )(rows, idx2[0].reshape(nw, n_chunks, SC_CHUNK), idx2[1].reshape(nw, n_chunks, SC_CHUNK))


EXPERT_TILE = 1024
EXPERT_SUB = 256
EXPERT_FF = 512
EXPERT_VMEM = 60 * 1024 * 1024


def _pack_halves(x):
    k = x.shape[1] // 2
    lo = lax.bitcast_convert_type(x[:, :k].astype(BF16).astype(F32), jnp.uint32)
    hi = lax.bitcast_convert_type(x[:, k:].astype(BF16).astype(F32), jnp.uint32)
    return (lo >> 16) | (hi & jnp.uint32(0xFFFF0000))


def _unpack_halves(w):
    lo = lax.bitcast_convert_type(w << 16, F32).astype(BF16)
    hi = lax.bitcast_convert_type(w & jnp.uint32(0xFFFF0000), F32).astype(BF16)
    return lo, hi


def _expert_kernel(nf, te_ref, ts_ref, nu_ref, nv_ref, x_ref, wg_ref, wu_ref, wd_ref, y_ref, acc, xb):
    i = pl.program_id(0)
    f = pl.program_id(1)
    tm = x_ref.shape[0]
    half = x_ref.shape[1]
    sub = EXPERT_SUB
    nvalid = nv_ref[i]

    def compute(nrows, first, last):
        rows = slice(0, nrows)
        if first:
            rid = lax.broadcasted_iota(I32, (nrows, half), 0)
            lo, hi = _unpack_halves(x_ref[rows, :])
            zero = jnp.zeros_like(lo)
            xb[rows, 0:half] = jnp.where(rid < nvalid, lo, zero)
            xb[rows, half:2 * half] = jnp.where(rid < nvalid, hi, zero)
        x = xb[rows, :]
        hg = jnp.dot(x, wg_ref[...].astype(BF16), preferred_element_type=F32)
        hu = jnp.dot(x, wu_ref[...].astype(BF16), preferred_element_type=F32)
        act = (_silu(hg) * hu).astype(BF16)
        part = jnp.dot(act, wd_ref[...].astype(BF16), preferred_element_type=F32)
        total = part if first else acc[rows, :] + part
        if last:
            y_ref[rows, :] = _pack_halves(total)
        else:
            acc[rows, :] = total

    assert nf >= 2
    for k in range(1, tm // sub + 1):
        has_k = (nvalid > (k - 1) * sub) & (nvalid <= k * sub)

        @pl.when(has_k & (f == 0))
        def _():
            compute(k * sub, True, False)

        @pl.when(has_k & (f > 0) & (f < nf - 1))
        def _():
            compute(k * sub, False, False)

        @pl.when(has_k & (f == nf - 1))
        def _():
            compute(k * sub, False, True)
            if k * sub < tm:
                y_ref[k * sub:tm, :] = jnp.zeros((tm - k * sub, half), y_ref.dtype)

    @pl.when((nvalid == 0) & (f == nf - 1))
    def _():
        y_ref[...] = jnp.zeros_like(y_ref)


def _experts(x, tile_expert, tile_src, n_used, n_valid, wg, wu, wd, tm, tf):
    p, half = x.shape
    d = 2 * half
    ne, _, dff = wg.shape
    nf = dff // tf
    assert dff % tf == 0 and tm % EXPERT_SUB == 0

    def fidx(i, f, nu):
        return jnp.where(i < nu[0], f, nf - 1)

    grid_spec = pltpu.PrefetchScalarGridSpec(
        num_scalar_prefetch=4,
        grid=(p // tm, nf),
        in_specs=[
            pl.BlockSpec((tm, half), lambda i, f, te, ts, nu, nv: (ts[i], 0)),
            pl.BlockSpec((None, d, tf), lambda i, f, te, ts, nu, nv: (te[i], 0, fidx(i, f, nu))),
            pl.BlockSpec((None, d, tf), lambda i, f, te, ts, nu, nv: (te[i], 0, fidx(i, f, nu))),
            pl.BlockSpec((None, tf, d), lambda i, f, te, ts, nu, nv: (te[i], fidx(i, f, nu), 0)),
        ],
        out_specs=pl.BlockSpec((tm, half), lambda i, f, te, ts, nu, nv: (i, 0)),
        scratch_shapes=[pltpu.VMEM((tm, d), F32), pltpu.VMEM((tm, d), BF16)],
    )
    return pl.pallas_call(
        functools.partial(_expert_kernel, nf),
        grid_spec=grid_spec,
        out_shape=jax.ShapeDtypeStruct((p, half), jnp.uint32),
        compiler_params=_params(("arbitrary", "arbitrary"), EXPERT_VMEM),
        name="moe_experts",
    )(tile_expert, tile_src, n_used, n_valid, x, wg, wu, wd)


def _combine_kernel(wt_ref, h_ref, mod_ref, y0_ref, y1_ref, o_ref):
    wt = wt_ref[...]
    half = y0_ref.shape[1]
    lo0, hi0 = _unpack_halves(y0_ref[...])
    lo1, hi1 = _unpack_halves(y1_ref[...])
    w0, w1 = wt[:, 0:1], wt[:, 1:2]
    h = h_ref[...]
    gate = mod_ref[5:6, :]
    o_ref[:, 0:half] = h[:, 0:half] + gate[:, 0:half] * (w0 * lo0.astype(F32) + w1 * lo1.astype(F32))
    o_ref[:, half:] = h[:, half:] + gate[:, half:] * (w0 * hi0.astype(F32) + w1 * hi1.astype(F32))


def _combine(yg, wt, h, mod, row_off, n_rows, n_mod, mod_off, tm):
    d = h.shape[1]
    nt = n_rows // tm
    toff = row_off // tm
    per_mod = n_rows // n_mod // tm
    return pl.pallas_call(
        _combine_kernel,
        grid=(nt,),
        in_specs=[
            pl.BlockSpec((tm, LANES), lambda i: (toff + i, 0)),
            pl.BlockSpec((tm, d), lambda i: (toff + i, 0)),
            pl.BlockSpec((None, 8, d), lambda i: (mod_off + i // per_mod, 0, 0)),
            pl.BlockSpec((None, tm, d // 2), lambda i: (0, i, 0)),
            pl.BlockSpec((None, tm, d // 2), lambda i: (1, i, 0)),
        ],
        out_specs=pl.BlockSpec((tm, d), lambda i: (i, 0)),
        out_shape=jax.ShapeDtypeStruct((n_rows, d), F32),
        compiler_params=_params(("arbitrary",)),
        name="moe_combine",
    )(wt, h, mod, yg, yg)


def _route(route, counts, n, tm):
    cnt = counts[0, :N_EXPERTS].astype(I32)
    padded = ((cnt + tm - 1) // tm) * tm
    gend = jnp.cumsum(padded)
    goff = gend - padded
    e = route[0:2].astype(I32)
    rank = route[2:4].astype(I32)
    onehot = e[:, :, None] == jnp.arange(N_EXPERTS, dtype=I32)[None, None, :]
    dest = jnp.sum(jnp.where(onehot, goff[None, None, :], 0), axis=-1) + rank
    p = 2 * n + N_EXPERTS * tm
    n_used = gend[-1] // tm
    tiles = jnp.arange(p // tm, dtype=I32)
    tile_src = jnp.minimum(tiles, n_used - 1)
    tile_expert = jnp.minimum(jnp.sum((gend[None, :] <= (tile_src * tm)[:, None]).astype(I32), axis=1),
                              N_EXPERTS - 1)
    used = goff[tile_expert] + cnt[tile_expert]
    n_valid = jnp.where(tiles < n_used, jnp.clip(used - tiles * tm, 0, tm), 0).astype(I32)
    return dest, p, tile_expert, tile_src, n_used.reshape(1).astype(I32), n_valid


def kernel(x_prompt, x_sample, cache_k, cache_v, state_fwd, state_bwd, c, c_ctx, ada_w, ada_b, norm1_g, norm2_g, attn_w_qkv, attn_q_norm, attn_k_norm, attn_sink, attn_w_o, gla_w_in, gla_gate_w1, gla_gate_w2, gla_gate_b, gla_out_norm, gla_w_o, ffn_w_gate, ffn_w_up, ffn_w_down, moe_router, moe_w_gate, moe_w_up, moe_w_down):
    bp, seq, d = x_prompt.shape
    db, t_len, _ = x_sample.shape
    n_p, n_s = bp * seq, db * t_len
    n = n_p + n_s
    xp = x_prompt.reshape(n_p, d)
    xs = x_sample.reshape(n_s, d)

    cond = jnp.concatenate([c_ctx[None, :], c], axis=0)
    assert cond.shape[0] <= 8
    cond_t = jnp.pad(cond, ((0, 8 - cond.shape[0]), (0, 0))).T
    mods = _modulation(cond_t, cond.shape[0], ada_w, ada_b)

    nk = N_KV_HEADS * HEAD_DIM
    qn = jnp.tile(attn_q_norm[0], N_HEADS)[None, :]
    kn = jnp.tile(attn_k_norm[0], N_KV_HEADS)[None, :]
    q, kt, vv, new_k, new_v = _qkv(xp, xs, seq, t_len, mods[0], norm1_g[0][None, :],
                                   attn_w_qkv[0].astype(BF16), qn, kn)
    wo0 = attn_w_o[0].astype(BF16)
    sink = attn_sink[0]
    hp = _ctx_attention(sink, q, kt, vv, xp, seq, mods[0], wo0)
    ck = cache_k[:, 0].astype(BF16)
    cv = cache_v[:, 0].astype(BF16)
    ckt = jnp.concatenate([ck, ck], axis=-1).transpose(0, 2, 3, 1)
    cvv = jnp.concatenate([cv, cv], axis=-1).reshape(db, cv.shape[1], N_KV_HEADS * LANES)
    hs = _lat_attention(sink, q, kt, vv, ckt, cvv, xs, n_p, t_len, mods[0], wo0)
    h = _ffn(hp, hs, t_len, mods[0], norm2_g[0][None, :], ffn_w_gate[0].astype(BF16),
             ffn_w_up[0].astype(BF16), ffn_w_down[0].astype(BF16))

    hk = GLA_HEADS * GLA_DK
    rank = GLA_GATE_RANK
    w1 = jnp.concatenate([gla_gate_w1[0, 0], gla_gate_w1[0, 1]], axis=1)
    w1 = jnp.pad(w1, ((0, 0), (0, LANES - 2 * rank))).astype(BF16)
    w2 = jnp.zeros((LANES, 2 * hk), F32)
    w2 = w2.at[0:rank, 0:hk].set(gla_gate_w2[0, 0]).at[rank:2 * rank, hk:].set(gla_gate_w2[0, 1]).astype(BF16)
    gate_b = gla_gate_b[0].reshape(1, 2 * hk)
    qk, v, r, bc = _gla_in(h, n_p, t_len, mods[1], norm1_g[1][None, :], gla_w_in[0].astype(BF16),
                           w1, w2, gate_b)
    of_p, ob_p, new_sf, new_sb = _gla_scan(qk, v, bc, None, None, 0, bp, seq)
    of_s, ob_s, _, _ = _gla_scan(qk, v, bc, state_fwd[:, 0], state_bwd[:, 0], n_p // t_len, db, t_len)
    rw = jnp.pad(moe_router[0], ((0, 0), (0, LANES - N_EXPERTS)))
    rw_hi = rw.astype(BF16)
    rw_lo = (rw - rw_hi.astype(F32)).astype(BF16)
    h, f, route, wt, counts = _gla_out(
        of_p.reshape(n_p, d), ob_p.reshape(n_p, d), of_s.reshape(n_s, d), ob_s.reshape(n_s, d), r, h,
        n_p, t_len, mods[1], gla_out_norm[0][None, :], gla_w_o[0].astype(BF16), norm2_g[1][None, :],
        jnp.stack([rw_hi, rw_lo]))

    tm_e = EXPERT_TILE
    dest_k, p_rows, tile_expert, tile_src, n_used, n_valid = _route(route, counts, n, tm_e)
    xg = _sc_scatter_rows(f, dest_k, p_rows)
    y = _experts(xg, tile_expert, tile_src, n_used, n_valid, moe_w_gate[0], moe_w_up[0],
                 moe_w_down[0], tm_e, EXPERT_FF)
    yg_p = _sc_gather_rows(y, dest_k[:, :n_p].reshape(2 * n_p)).reshape(2, n_p, d // 2)
    yg_s = _sc_gather_rows(y, dest_k[:, n_p:].reshape(2 * n_s)).reshape(2, n_s, d // 2)
    tm_c = _row_tile(n_p, t_len)
    y_p = _combine(yg_p, wt, h, mods[1], 0, n_p, 1, 0, tm_c)
    y_s = _combine(yg_s, wt, h, mods[1], n_p, n_s, db, 1, tm_c)

    return (y_p.reshape(bp, seq, d), y_s.reshape(db, t_len, d), new_k, new_v,
            new_sf[:, None], new_sb[:, None])
```

```python
import functools
import math

import numpy as np
import jax
import jax.numpy as jnp
from jax import lax
from jax.experimental import pallas as pl
from jax.experimental.pallas import tpu as pltpu
from jax.experimental.pallas import tpu_sc as plsc

F32 = jnp.float32
BF16 = jnp.bfloat16
I32 = jnp.int32

D_MODEL = 1024
N_HEADS = 16
N_KV_HEADS = 4
HEAD_DIM = 64
GRID_W = 64
WINDOW = 128
ATTN_BLOCK = 128
ROPE_THETA = 10000.0
GLA_HEADS = 4
GLA_DK = 128
GLA_DV = 256
GLA_GATE_RANK = 16
GLA_GATE_TAU = 16.0
GLA_CHUNK = 64
N_EXPERTS = 8
NORM_EPS = 1e-6
NEG_INF = -1e30

LANES = 128
VMEM_LIMIT = 56 * 1024 * 1024


def _params(sem, vmem=VMEM_LIMIT):
    return pltpu.CompilerParams(dimension_semantics=sem, vmem_limit_bytes=vmem)


def _row_tile(*counts, cap=512):
    t = cap
    while any(c % t for c in counts):
        t //= 2
    assert t >= 8
    return t


def _rms(x):
    return x * lax.rsqrt(jnp.mean(x * x, axis=-1, keepdims=True) + NORM_EPS)


def _adaln(x, g, shift, scale):
    return _rms(x) * (g * (1.0 + scale)) + shift


def _silu(x):
    return x * jax.nn.sigmoid(x)


def _const_spec(shape):
    nd = len(shape)
    return pl.BlockSpec(shape, lambda *_: (0,) * nd)


def _mod_kernel(n_cond, ct_ref, w_ref, b_ref, o_ref):
    ct = ct_ref[...]
    s = _silu(ct)
    w = w_ref[...]
    rows = [jnp.sum(w * s[:, r:r + 1], axis=0, keepdims=True) for r in range(n_cond)]
    rows += [jnp.zeros_like(rows[0])] * (8 - n_cond)
    o_ref[...] = jnp.concatenate(rows, axis=0) + b_ref[...]


def _modulation(cond_t, n_cond, ada_w, ada_b):
    depth, d, n6 = ada_w.shape
    tn = 2048
    out = pl.pallas_call(
        functools.partial(_mod_kernel, n_cond),
        grid=(depth, n6 // tn),
        in_specs=[
            pl.BlockSpec((d, 8), lambda l, j: (0, 0)),
            pl.BlockSpec((None, d, tn), lambda l, j: (l, 0, j)),
            pl.BlockSpec((None, 1, tn), lambda l, j: (l, 0, j)),
        ],
        out_specs=pl.BlockSpec((None, 8, tn), lambda l, j: (l, 0, j)),
        out_shape=jax.ShapeDtypeStruct((depth, 8, n6), F32),
        compiler_params=_params(("arbitrary", "arbitrary")),
        name="modulation",
    )(cond_t, ada_w, ada_b.reshape(depth, 1, n6))
    m = out[:, :n_cond].reshape(depth, n_cond, 6, d)
    return jnp.pad(m, ((0, 0), (0, 0), (0, 2), (0, 0)))


LOG2E = math.log2(math.e)


def _dup_half(k2, half):
    lane = lax.broadcasted_iota(I32, k2.shape, 1)
    lo = lane < HEAD_DIM
    r = pltpu.roll(k2, HEAD_DIM, 1)
    return jnp.where(lo, k2, r) if half == 0 else jnp.where(lo, r, k2)


QKV_PARTS = 2


def _qkv_kernel(na, xa_ref, xb_ref, mod_ref, g_ref, w_ref, qn_ref, kn_ref, bd_ref,
                cos_ref, sin_ref, q_out, kt_out, vv_out, ck_out, cv_out):
    i = pl.program_id(0)
    is_ctx = i < na
    tm = xa_ref.shape[0] // QKV_PARTS
    for part in range(QKV_PARTS):
        _qkv_rows(slice(part * tm, (part + 1) * tm), tm, is_ctx, xa_ref, xb_ref, mod_ref, g_ref, w_ref,
                  qn_ref, kn_ref, bd_ref, cos_ref, sin_ref, q_out, kt_out, vv_out, ck_out, cv_out)


def _qkv_rows(rows, tm, is_ctx, xa_ref, xb_ref, mod_ref, g_ref, w_ref, qn_ref, kn_ref, bd_ref,
              cos_ref, sin_ref, q_out, kt_out, vv_out, ck_out, cv_out):
    x = jnp.where(is_ctx, xa_ref[rows, :], xb_ref[rows, :])
    a = _adaln(x, g_ref[...], mod_ref[0:1, :], mod_ref[1:2, :])
    y = jnp.dot(a.astype(BF16), w_ref[...], preferred_element_type=F32)
    cos = jnp.where(is_ctx, 1.0, cos_ref[rows, :])
    sin = jnp.where(is_ctx, 0.0, sin_ref[rows, :])
    lane = lax.broadcasted_iota(I32, (tm, LANES), 1)
    first16 = (lane % 32) < 16

    def norm_rope(z, wt):
        ss = jnp.dot((z * z).astype(BF16), bd_ref[...], preferred_element_type=F32)
        zn = z * lax.rsqrt(ss * (1.0 / HEAD_DIM) + NORM_EPS) * wt
        outs = []
        for c in range(2):
            t = zn[:, c * LANES:(c + 1) * LANES]
            partner = jnp.where(first16, pltpu.roll(t, LANES - 16, 1), pltpu.roll(t, 16, 1))
            outs.append(t * cos + partner * sin)
        return jnp.concatenate(outs, axis=1)

    nq = N_HEADS * HEAD_DIM
    nk = N_KV_HEADS * HEAD_DIM
    scale = HEAD_DIM ** -0.5 * LOG2E
    for s in range(nq // 256):
        sl = slice(s * 256, (s + 1) * 256)
        q_out[rows, sl] = (norm_rope(y[:, sl], qn_ref[:, sl]) * scale).astype(BF16)
    k = norm_rope(y[:, nq:nq + nk], kn_ref[...])
    v = y[:, nq + nk:nq + 2 * nk]
    for g in range(N_KV_HEADS):
        c = g // 2
        kk = _dup_half(k[:, c * LANES:(c + 1) * LANES], g % 2)
        kt_out[g, :, rows] = kk.T.astype(BF16)
        vv_out[rows, g * LANES:(g + 1) * LANES] = _dup_half(v[:, c * LANES:(c + 1) * LANES], g % 2).astype(BF16)

    @pl.when(is_ctx)
    def _():
        ck_out[rows, :] = k
        cv_out[rows, :] = v


def _rope_tables(t_len):
    pos = np.arange(t_len)
    row = (pos // GRID_W).astype(np.float32)[:, None]
    col = (pos % GRID_W).astype(np.float32)[:, None]
    half = HEAD_DIM // 2
    inv = (np.float32(ROPE_THETA) ** (-np.arange(0, half, 2, dtype=np.float32) / np.float32(half)))[None, :]
    ar, ac = row * inv, col * inv
    cos = np.concatenate([np.cos(ar), np.cos(ar), np.cos(ac), np.cos(ac)], axis=1)
    sin = np.concatenate([-np.sin(ar), np.sin(ar), -np.sin(ac), np.sin(ac)], axis=1)
    return (jnp.asarray(np.tile(cos, (1, 2)), dtype=F32), jnp.asarray(np.tile(sin, (1, 2)), dtype=F32))


def _qkv(xp, xs, t_len, mod, g, w, qn, kn):
    n_p, n_s = xp.shape[0], xs.shape[0]
    tm = _row_tile(n_p, t_len, cap=1024)
    na, nb = n_p // tm, n_s // tm
    nt_s = t_len // tm
    n = n_p + n_s
    d = D_MODEL
    nq, nk = N_HEADS * HEAD_DIM, N_KV_HEADS * HEAD_DIM
    cos, sin = _rope_tables(t_len)
    eye4 = jnp.asarray(np.kron(np.eye(4, dtype=np.float32), np.ones((HEAD_DIM, HEAD_DIM), np.float32)),
                       dtype=BF16)
    mod_idx = lambda i: (jnp.where(i < na, 0, 1 + (i - na) // nt_s), 0, 0)
    pos_idx = lambda i: (jnp.maximum(i - na, 0) % nt_s, 0)
    return pl.pallas_call(
        functools.partial(_qkv_kernel, na),
        grid=(na + nb,),
        in_specs=[
            pl.BlockSpec((tm, d), lambda i: (jnp.minimum(i, na - 1), 0)),
            pl.BlockSpec((tm, d), lambda i: (jnp.maximum(i - na, 0), 0)),
            pl.BlockSpec((None, 8, d), mod_idx),
            _const_spec((1, d)),
            _const_spec((d, nq + 2 * nk)),
            _const_spec((1, nq)),
            _const_spec((1, nk)),
            _const_spec((256, 256)),
            pl.BlockSpec((tm, LANES), pos_idx),
            pl.BlockSpec((tm, LANES), pos_idx),
        ],
        out_specs=[pl.BlockSpec((tm, nq), lambda i: (i, 0)),
                   pl.BlockSpec((N_KV_HEADS, LANES, tm), lambda i: (0, 0, i)),
                   pl.BlockSpec((tm, N_KV_HEADS * LANES), lambda i: (i, 0)),
                   pl.BlockSpec((tm, nk), lambda i: (jnp.minimum(i, na - 1), 0)),
                   pl.BlockSpec((tm, nk), lambda i: (jnp.minimum(i, na - 1), 0))],
        out_shape=[jax.ShapeDtypeStruct((n, nq), BF16),
                   jax.ShapeDtypeStruct((N_KV_HEADS, LANES, n), BF16),
                   jax.ShapeDtypeStruct((n, N_KV_HEADS * LANES), BF16),
                   jax.ShapeDtypeStruct((n_p, nk), F32),
                   jax.ShapeDtypeStruct((n_p, nk), F32)],
        compiler_params=_params(("arbitrary",)),
        name="qkv",
    )(xp, xs, mod, g, w, qn, kn, eye4, cos, sin)


def _attn_group(q2a, q2b, kt, vv, bias, n_bias, sinks):
    tq = q2a.shape[0]
    nk = kt.shape[1]
    lo = lax.broadcasted_iota(I32, (tq, LANES), 1) < HEAD_DIM
    top = lax.broadcasted_iota(I32, kt.shape, 0) < HEAD_DIM
    zk = jnp.zeros_like(kt)
    kbd = jnp.concatenate([jnp.where(top, kt, zk), jnp.where(top, zk, kt)], axis=1)
    left = lax.broadcasted_iota(I32, vv.shape, 1) < HEAD_DIM
    zv = jnp.zeros_like(vv)
    vbd = jnp.concatenate([jnp.where(left, vv, zv), jnp.where(left, zv, vv)], axis=0)
    s = jnp.dot(jnp.concatenate([q2a, q2b], axis=0), kbd, preferred_element_type=F32)
    p_rows, scales = [], []
    for t in range(2):
        ph, rinv = [], []
        for hf in range(2):
            sh = s[t * tq:(t + 1) * tq, hf * nk:(hf + 1) * nk]
            if bias is not None:
                sh = jnp.concatenate([sh[:, :n_bias] + bias, sh[:, n_bias:]], axis=1)
            sk = sinks[2 * t + hf]
            m = jnp.maximum(jnp.max(sh, axis=-1, keepdims=True), sk)
            p = jnp.exp2(sh - m)
            rinv.append(1.0 / (jnp.sum(p, axis=-1, keepdims=True) + jnp.exp2(sk - m)))
            ph.append(p.astype(BF16))
        p_rows.append(jnp.concatenate(ph, axis=1))
        scales.append(jnp.where(lo, rinv[0], rinv[1]))
    o = jnp.dot(jnp.concatenate(p_rows, axis=0), vbd, preferred_element_type=F32)
    return o[0:tq] * scales[0], o[tq:2 * tq] * scales[1]


def _ctx_attn_kernel(sink_ref, q_ref, kt_ref, vv_ref, x_ref, mod_ref, wo_ref, o_ref, osc):
    for g in range(N_KV_HEADS):
        q2a = q_ref[:, (2 * g) * LANES:(2 * g + 1) * LANES]
        q2b = q_ref[:, (2 * g + 1) * LANES:(2 * g + 2) * LANES]
        sinks = [sink_ref[4 * g + j] * LOG2E for j in range(4)]
        oa, ob = _attn_group(q2a, q2b, kt_ref[g], vv_ref[:, g * LANES:(g + 1) * LANES], None, 0, sinks)
        osc[:, (2 * g) * LANES:(2 * g + 1) * LANES] = oa.astype(BF16)
        osc[:, (2 * g + 1) * LANES:(2 * g + 2) * LANES] = ob.astype(BF16)
    att = jnp.dot(osc[...], wo_ref[...], preferred_element_type=F32)
    o_ref[...] = x_ref[...] + mod_ref[2:3, :] * att


def _ctx_attention(sink, q, kt, vv, xp, seq, mod, wo):
    n_p, d = xp.shape
    nb = n_p // seq
    return pl.pallas_call(
        _ctx_attn_kernel,
        grid=(nb,),
        in_specs=[
            pl.BlockSpec(memory_space=pltpu.SMEM),
            pl.BlockSpec((seq, d), lambda b: (b, 0)),
            pl.BlockSpec((N_KV_HEADS, LANES, seq), lambda b: (0, 0, b)),
            pl.BlockSpec((seq, N_KV_HEADS * LANES), lambda b: (b, 0)),
            pl.BlockSpec((seq, d), lambda b: (b, 0)),
            pl.BlockSpec((None, 8, d), lambda b: (0, 0, 0)),
            _const_spec((d, d)),
        ],
        out_specs=pl.BlockSpec((seq, d), lambda b: (b, 0)),
        out_shape=jax.ShapeDtypeStruct((n_p, d), F32),
        scratch_shapes=[pltpu.VMEM((seq, d), BF16)],
        compiler_params=_params(("arbitrary",)),
        name="ctx_attention",
    )(sink, q, kt, vv, xp, mod, wo)


LAT_QB = 4


def _lat_attn_kernel(t_len, sink_ref, q_ref, ktp_ref, kto_ref, ktn_ref, vvp_ref, vvo_ref, vvn_ref,
                     ckt_ref, cvv_ref, x_ref, mod_ref, wo_ref, o_ref, osc):
    step = pl.program_id(1)
    tq = ATTN_BLOCK
    nloc = 3 * ATTN_BLOCK
    qi = lax.broadcasted_iota(I32, (tq, nloc), 0)
    kj = lax.broadcasted_iota(I32, (tq, nloc), 1)
    in_window = jnp.abs(qi + tq - kj) <= WINDOW
    for u in range(LAT_QB):
        n = step * LAT_QB + u
        kpos = (n - 1) * tq + kj
        bias = jnp.where(in_window & (kpos >= 0) & (kpos < t_len), 0.0, NEG_INF)
        rows = slice(u * tq, (u + 1) * tq)
        for g in range(N_KV_HEADS):
            vs = slice(g * LANES, (g + 1) * LANES)
            kts = [ktp_ref[g]] + [kto_ref[g, :, j * tq:(j + 1) * tq] for j in range(LAT_QB)] + [ktn_ref[g]]
            vvs = [vvp_ref[:, vs]] + [vvo_ref[j * tq:(j + 1) * tq, vs] for j in range(LAT_QB)] + [vvn_ref[:, vs]]
            kt = jnp.concatenate(kts[u:u + 3] + [ckt_ref[g]], axis=1)
            vv = jnp.concatenate(vvs[u:u + 3] + [cvv_ref[:, vs]], axis=0)
            q2a = q_ref[rows, (2 * g) * LANES:(2 * g + 1) * LANES]
            q2b = q_ref[rows, (2 * g + 1) * LANES:(2 * g + 2) * LANES]
            sinks = [sink_ref[4 * g + j] * LOG2E for j in range(4)]
            oa, ob = _attn_group(q2a, q2b, kt, vv, bias, nloc, sinks)
            osc[rows, (2 * g) * LANES:(2 * g + 1) * LANES] = oa.astype(BF16)
            osc[rows, (2 * g + 1) * LANES:(2 * g + 2) * LANES] = ob.astype(BF16)
    att = jnp.dot(osc[...], wo_ref[...], preferred_element_type=F32)
    o_ref[...] = x_ref[...] + mod_ref[2:3, :] * att


def _lat_attention(sink, q, kt, vv, ckt, cvv, xs, n_p, t_len, mod, wo):
    n_s, d = xs.shape
    db = n_s // t_len
    tq = ATTN_BLOCK
    ts = LAT_QB * tq
    nblk = t_len // tq
    nstep = t_len // ts
    assert n_p % ts == 0 and t_len % ts == 0
    off = n_p // tq
    npast = ckt.shape[3]
    g4 = N_KV_HEADS

    def nbr(delta):
        return lambda b, m: off + b * nblk + jnp.clip(m * LAT_QB + delta, 0, nblk - 1)

    own = lambda b, m: n_p // ts + b * nstep + m
    return pl.pallas_call(
        functools.partial(_lat_attn_kernel, t_len),
        grid=(db, nstep),
        in_specs=[
            pl.BlockSpec(memory_space=pltpu.SMEM),
            pl.BlockSpec((ts, d), lambda b, m: (own(b, m), 0)),
            pl.BlockSpec((g4, LANES, tq), lambda b, m: (0, 0, nbr(-1)(b, m))),
            pl.BlockSpec((g4, LANES, ts), lambda b, m: (0, 0, own(b, m))),
            pl.BlockSpec((g4, LANES, tq), lambda b, m: (0, 0, nbr(LAT_QB)(b, m))),
            pl.BlockSpec((tq, g4 * LANES), lambda b, m: (nbr(-1)(b, m), 0)),
            pl.BlockSpec((ts, g4 * LANES), lambda b, m: (own(b, m), 0)),
            pl.BlockSpec((tq, g4 * LANES), lambda b, m: (nbr(LAT_QB)(b, m), 0)),
            pl.BlockSpec((None, g4, LANES, npast), lambda b, m: (b, 0, 0, 0)),
            pl.BlockSpec((None, npast, g4 * LANES), lambda b, m: (b, 0, 0)),
            pl.BlockSpec((ts, d), lambda b, m: (b * nstep + m, 0)),
            pl.BlockSpec((None, 8, d), lambda b, m: (1 + b, 0, 0)),
            _const_spec((d, d)),
        ],
        out_specs=pl.BlockSpec((ts, d), lambda b, m: (b * nstep + m, 0)),
        out_shape=jax.ShapeDtypeStruct((n_s, d), F32),
        scratch_shapes=[pltpu.VMEM((ts, d), BF16)],
        compiler_params=_params(("arbitrary", "arbitrary")),
        name="lat_attention",
    )(sink, q, kt, kt, kt, vv, vv, vv, ckt, cvv, xs, mod, wo)


def _ffn_kernel(na, fc, xa_ref, xb_ref, mod_ref, g_ref, wg_ref, wu_ref, wd_ref, o_ref, acc):
    i = pl.program_id(0)
    x = jnp.where(i < na, xa_ref[...], xb_ref[...])
    a = _adaln(x, g_ref[...], mod_ref[3:4, :], mod_ref[4:5, :]).astype(BF16)
    nf = wg_ref.shape[1] // fc
    for f in range(nf):
        sl = slice(f * fc, (f + 1) * fc)
        hg = jnp.dot(a, wg_ref[:, sl], preferred_element_type=F32)
        hu = jnp.dot(a, wu_ref[:, sl], preferred_element_type=F32)
        act = (_silu(hg) * hu).astype(BF16)
        part = jnp.dot(act, wd_ref[sl, :], preferred_element_type=F32)
        if f == 0:
            acc[...] = part
        else:
            acc[...] += part
    o_ref[...] = x + mod_ref[5:6, :] * acc[...]


def _ffn(hp, hs, t_len, mod, g, wg, wu, wd):
    n_p, n_s = hp.shape[0], hs.shape[0]
    d, dff = wg.shape
    tm = _row_tile(n_p, t_len)
    na, nb = n_p // tm, n_s // tm
    nt_s = t_len // tm
    fc = 256
    assert dff % fc == 0
    mod_idx = lambda i: (jnp.where(i < na, 0, 1 + (i - na) // nt_s), 0, 0)
    return pl.pallas_call(
        functools.partial(_ffn_kernel, na, fc),
        grid=(na + nb,),
        in_specs=[
            pl.BlockSpec((tm, d), lambda i: (jnp.minimum(i, na - 1), 0)),
            pl.BlockSpec((tm, d), lambda i: (jnp.maximum(i - na, 0), 0)),
            pl.BlockSpec((None, 8, d), mod_idx),
            _const_spec((1, d)),
            _const_spec((d, dff)),
            _const_spec((d, dff)),
            _const_spec((dff, d)),
        ],
        out_specs=pl.BlockSpec((tm, d), lambda i: (i, 0)),
        out_shape=jax.ShapeDtypeStruct((n_p + n_s, d), F32),
        scratch_shapes=[pltpu.VMEM((tm, d), F32)],
        compiler_params=_params(("arbitrary",)),
        name="ffn",
    )(hp, hs, mod, g, wg, wu, wd)


def _split2(x):
    hi = x.astype(BF16)
    lo = (x - hi.astype(F32)).astype(BF16)
    return hi, lo


def _gla_in_kernel(x_ref, mod_ref, g_ref, w_ref, w1_ref, w2_ref, gb_ref, tri_ref,
                   qk_out, v_out, r_out, b_out):
    x = x_ref[...]
    tm = x.shape[0]
    a = _adaln(x, g_ref[...], mod_ref[0:1, :], mod_ref[1:2, :]).astype(BF16)
    hk = GLA_HEADS * GLA_DK
    hv = GLA_HEADS * GLA_DV
    q = jnp.dot(a, w_ref[:, 0:hk], preferred_element_type=F32)
    qk_out[:, 0:hk] = (q * (GLA_DK ** -0.5)).astype(BF16)
    qk_out[:, hk:2 * hk] = jnp.dot(a, w_ref[:, hk:2 * hk], preferred_element_type=F32).astype(BF16)
    v_out[...] = jnp.dot(a, w_ref[:, 2 * hk:2 * hk + hv], preferred_element_type=F32).astype(BF16)
    r_out[...] = jnp.dot(a, w_ref[:, 2 * hk + hv:2 * hk + 2 * hv], preferred_element_type=F32).astype(BF16)
    z1 = jnp.dot(a, w1_ref[...], preferred_element_type=F32)
    z = jnp.dot(z1.astype(BF16), w2_ref[...], preferred_element_type=F32) + gb_ref[...]
    gate = (jnp.minimum(z, 0.0) - jnp.log(1.0 + jnp.exp(-jnp.abs(z)))) * (1.0 / GLA_GATE_TAU)
    c = GLA_CHUNK
    tri = tri_ref[...]
    for j in range(tm // c):
        rows = slice(j * c, (j + 1) * c)
        for dr in range(2):
            cols = slice(dr * hk, (dr + 1) * hk)
            hi, lo = _split2(gate[rows, cols])
            b_out[rows, cols] = (jnp.dot(tri[dr], hi, preferred_element_type=F32)
                                 + jnp.dot(tri[dr], lo, preferred_element_type=F32))


def _gla_in(h, n_p, t_len, mod, g, w_in, w1, w2, gate_b):
    n, d = h.shape
    tm = _row_tile(n_p, t_len, cap=1024)
    na = n_p // tm
    nt_s = t_len // tm
    hk, hv = GLA_HEADS * GLA_DK, GLA_HEADS * GLA_DV
    c = GLA_CHUNK
    lower = np.tril(np.ones((c, c), np.float32))
    tri = jnp.asarray(np.stack([lower, lower.T]), dtype=BF16)
    mod_idx = lambda i: (jnp.where(i < na, 0, 1 + (i - na) // nt_s), 0, 0)
    row = lambda i: (i, 0)
    return pl.pallas_call(
        _gla_in_kernel,
        grid=(n // tm,),
        in_specs=[
            pl.BlockSpec((tm, d), row),
            pl.BlockSpec((None, 8, d), mod_idx),
            _const_spec((1, d)),
            _const_spec(w_in.shape),
            _const_spec(w1.shape),
            _const_spec(w2.shape),
            _const_spec((1, 2 * hk)),
            _const_spec((2, c, c)),
        ],
        out_specs=[pl.BlockSpec((tm, 2 * hk), row), pl.BlockSpec((tm, hv), row),
                   pl.BlockSpec((tm, hv), row), pl.BlockSpec((tm, 2 * hk), row)],
        out_shape=[jax.ShapeDtypeStruct((n, 2 * hk), BF16), jax.ShapeDtypeStruct((n, hv), BF16),
                   jax.ShapeDtypeStruct((n, hv), BF16), jax.ShapeDtypeStruct((n, 2 * hk), F32)],
        compiler_params=_params(("arbitrary",)),
        name="gla_in",
    )(h, mod, g, w_in, w1, w2, gate_b, tri)


def _gla_direction(qk_ref, v_ref, b_ref, st, o_ref, forward, qd_s, att_s, upd_s, dec_s):
    c = GLA_CHUNK
    ncb = qk_ref.shape[0] // c
    hk = GLA_HEADS * GLA_DK
    dk, dv = GLA_DK, GLA_DV
    ri = lax.broadcasted_iota(I32, (c, c), 0)
    ci_ = lax.broadcasted_iota(I32, (c, c), 1)
    keep = (ci_ <= ri) if forward else (ci_ >= ri)

    for ch in range(ncb):
        rows = slice(ch * c, (ch + 1) * c)
        b = b_ref[rows, :]
        if forward:
            b_end, b_mid = b[c - 1:c, :], b[c // 2 - 1:c // 2, :]
        else:
            b_end, b_mid = b[0:1, :], b[c // 2:c // 2 + 1, :]
        qa = qk_ref[rows, 0:hk] * jnp.exp(b - b_mid)
        ka = qk_ref[rows, hk:2 * hk] * jnp.exp(b_mid - b)
        qd_s[rows, :] = (qa * jnp.exp(b_mid)).astype(BF16)
        kd = (ka * jnp.exp(b_end - b_mid)).astype(BF16)
        dec_s[ch:ch + 1, :] = jnp.exp(b_end)
        qab, kab = qa.astype(BF16), ka.astype(BF16)
        for h in range(GLA_HEADS):
            ks = slice(h * dk, (h + 1) * dk)
            vs = slice(h * dv, (h + 1) * dv)
            att = lax.dot_general(qab[:, ks], kab[:, ks], (((1,), (1,)), ((), ())),
                                  preferred_element_type=F32)
            att_s[ch, h] = jnp.where(keep, att, 0.0).astype(BF16)
            upd_s[ch, h] = lax.dot_general(v_ref[rows, vs], kd[:, ks], (((0,), (0,)), ((), ())),
                                           preferred_element_type=F32)

    for ch in (range(ncb) if forward else reversed(range(ncb))):
        rows = slice(ch * c, (ch + 1) * c)
        for h in range(GLA_HEADS):
            ks = slice(h * dk, (h + 1) * dk)
            vs = slice(h * dv, (h + 1) * dv)
            s = st[h]
            o = lax.dot_general(qd_s[rows, ks], s.astype(BF16), (((1,), (1,)), ((), ())),
                                preferred_element_type=F32)
            o = o + jnp.dot(att_s[ch, h], v_ref[rows, vs], preferred_element_type=F32)
            o_ref[rows, vs] = o.astype(o_ref.dtype)
            st[h] = s * dec_s[ch:ch + 1, ks] + upd_s[ch, h]


def _gla_scan_kernel(shared, from_zero, *refs):
    if shared:
        qkf_ref, vf_ref, bf_ref, bb_ref = refs[:4]
        qkb_ref, vb_ref = qkf_ref, vf_ref
        rest = refs[4:]
    else:
        qkf_ref, vf_ref, bf_ref, qkb_ref, vb_ref, bb_ref = refs[:6]
        rest = refs[6:]
    if not from_zero:
        sf0_ref, sb0_ref = rest[:2]
        rest = rest[2:]
    of_ref, ob_ref, sf_ref, sb_ref, stf, stb, qd_s, att_s, upd_s, dec_s = rest
    i = pl.program_id(1)
    nblk = pl.num_programs(1)

    @pl.when(i == 0)
    def _():
        if from_zero:
            stf[...] = jnp.zeros_like(stf)
            stb[...] = jnp.zeros_like(stb)
        else:
            for h in range(GLA_HEADS):
                stf[h] = sf0_ref[h].T
                stb[h] = sb0_ref[h].T

    _gla_direction(qkf_ref, vf_ref, bf_ref, stf, of_ref, True, qd_s, att_s, upd_s, dec_s)
    _gla_direction(qkb_ref, vb_ref, bb_ref, stb, ob_ref, False, qd_s, att_s, upd_s, dec_s)

    @pl.when(i == nblk - 1)
    def _():
        for h in range(GLA_HEADS):
            sf_ref[h] = stf[h].T
            sb_ref[h] = stb[h].T


def _gla_scan(qk, v, bc, sf0, sb0, b_off, nb, t_len):
    n = qk.shape[0]
    assert n % t_len == 0
    ns = n // t_len
    tb = _row_tile(t_len)
    nblk = t_len // tb
    qk3 = qk.reshape(ns, t_len, qk.shape[1])
    v3 = v.reshape(ns, t_len, v.shape[1])
    bc3 = bc.reshape(ns, t_len, bc.shape[1])
    h, dk, dv = GLA_HEADS, GLA_DK, GLA_DV
    hk = h * dk
    st_spec = pl.BlockSpec((None, h, dk, dv), lambda b, i: (b, 0, 0, 0))
    fwd = lambda b, i: (b_off + b, i, 0)
    bwd = lambda b, i: (b_off + b, nblk - 1 - i, 0)
    bwd_b = pl.BlockSpec((None, tb, hk), lambda b, i: (b_off + b, nblk - 1 - i, 1))
    shared = nblk == 1
    in_specs = [pl.BlockSpec((None, tb, 2 * hk), fwd), pl.BlockSpec((None, tb, h * dv), fwd),
                pl.BlockSpec((None, tb, hk), fwd)]
    operands = [qk3, v3, bc3]
    if not shared:
        in_specs += [pl.BlockSpec((None, tb, 2 * hk), bwd), pl.BlockSpec((None, tb, h * dv), bwd)]
        operands += [qk3, v3]
    in_specs += [bwd_b]
    operands += [bc3]
    from_zero = sf0 is None
    if not from_zero:
        in_specs += [st_spec, st_spec]
        operands += [sf0, sb0]
    return pl.pallas_call(
        functools.partial(_gla_scan_kernel, shared, from_zero),
        grid=(nb, nblk),
        in_specs=in_specs,
        out_specs=[pl.BlockSpec((None, tb, h * dv), lambda b, i: (b, i, 0)),
                   pl.BlockSpec((None, tb, h * dv), lambda b, i: (b, nblk - 1 - i, 0)),
                   st_spec, st_spec],
        out_shape=[jax.ShapeDtypeStruct((nb, t_len, h * dv), BF16),
                   jax.ShapeDtypeStruct((nb, t_len, h * dv), BF16),
                   jax.ShapeDtypeStruct((nb, h, dk, dv), F32),
                   jax.ShapeDtypeStruct((nb, h, dk, dv), F32)],
        scratch_shapes=[pltpu.VMEM((h, dv, dk), F32), pltpu.VMEM((h, dv, dk), F32),
                        pltpu.VMEM((tb, hk), BF16),
                        pltpu.VMEM((tb // GLA_CHUNK, h, GLA_CHUNK, GLA_CHUNK), BF16),
                        pltpu.VMEM((tb // GLA_CHUNK, h, dv, dk), F32),
                        pltpu.VMEM((max(tb // GLA_CHUNK, 8), hk), F32)],
        compiler_params=_params(("arbitrary", "arbitrary")),
        name="gla_scan",
    )(*operands)


GLA_OUT_PARTS = 4


def _gla_out_kernel(na, ofa_ref, oba_ref, ofb_ref, obb_ref, r_ref, h_ref, mod_ref, on_ref, wo_ref,
                    g2_ref, rw_ref, tri_ref, h_out, f_out, route_out, wt_out, cnt_out, run):
    i = pl.program_id(0)
    is_ctx = i < na

    @pl.when(i == 0)
    def _():
        run[...] = jnp.zeros_like(run)

    tm = tri_ref.shape[0]
    for part in range(h_ref.shape[0] // tm):
        rows = slice(part * tm, (part + 1) * tm)
        _gla_out_rows(rows, tm, is_ctx, ofa_ref, oba_ref, ofb_ref, obb_ref, r_ref, h_ref, mod_ref, on_ref,
                      wo_ref, g2_ref, rw_ref, tri_ref, h_out, f_out, route_out, wt_out, run)
    cnt_out[...] = run[...]


def _gla_out_rows(rows, tm, is_ctx, ofa_ref, oba_ref, ofb_ref, obb_ref, r_ref, h_ref, mod_ref, on_ref,
                  wo_ref, g2_ref, rw_ref, tri_ref, h_out, f_out, route_out, wt_out, run):
    o = (jnp.where(is_ctx, ofa_ref[rows, :], ofb_ref[rows, :]).astype(F32)
         + jnp.where(is_ctx, oba_ref[rows, :], obb_ref[rows, :]).astype(F32))
    dv = GLA_DV
    parts = []
    for hh in range(GLA_HEADS):
        oh = o[:, hh * dv:(hh + 1) * dv]
        parts.append(_rms(oh) * on_ref[...])
    on = jnp.concatenate(parts, axis=1)
    r = r_ref[rows, :].astype(F32)
    gated = (on * _silu(r)).astype(BF16)
    h = h_ref[rows, :] + mod_ref[2:3, :] * jnp.dot(gated, wo_ref[...], preferred_element_type=F32)
    h_out[rows, :] = h
    f = _adaln(h, g2_ref[...], mod_ref[3:4, :], mod_ref[4:5, :])
    f_out[rows, :] = _pack_halves(f)
    fh = f.astype(BF16)
    fl = (f - fh.astype(F32)).astype(BF16)
    logits = (jnp.dot(fh, rw_ref[0], preferred_element_type=F32)
              + jnp.dot(fl, rw_ref[0], preferred_element_type=F32)
              + jnp.dot(fh, rw_ref[1], preferred_element_type=F32))
    lane = lax.broadcasted_iota(I32, (tm, LANES), 1)
    lane_f = lane.astype(F32)
    logits = jnp.where(lane < N_EXPERTS, logits, -jnp.inf)
    m1 = jnp.max(logits, axis=-1, keepdims=True)
    i1 = jnp.min(jnp.where(logits == m1, lane_f, float(LANES)), axis=-1, keepdims=True)
    rest = jnp.where(lane_f == i1, -jnp.inf, logits)
    m2 = jnp.max(rest, axis=-1, keepdims=True)
    i2 = jnp.min(jnp.where(rest == m2, lane_f, float(LANES)), axis=-1, keepdims=True)
    e2 = jnp.exp(m2 - m1)
    w1 = 1.0 / (1.0 + e2)
    w2 = e2 / (1.0 + e2)
    wt_out[rows, :] = jnp.where(lane == 0, w1, jnp.where(lane == 1, w2, 0.0))
    sel1 = lane_f == i1
    sel2 = lane_f == i2
    picked = jnp.where(sel1, 1.0, jnp.where(sel2, 1.0, 0.0))
    before = jnp.dot(tri_ref[...], picked.astype(BF16), preferred_element_type=F32) + run[0:1, :]
    rank1 = jnp.sum(jnp.where(sel1, before, 0.0), axis=-1, keepdims=True)
    rank2 = jnp.sum(jnp.where(sel2, before, 0.0), axis=-1, keepdims=True)
    run[...] = run[...] + jnp.sum(picked, axis=0, keepdims=True)
    routing = jnp.where(lane == 0, i1, jnp.where(lane == 1, i2,
                        jnp.where(lane == 2, rank1, jnp.where(lane == 3, rank2, 0.0))))
    route_out[:, rows] = routing.T[0:8, :]


def _gla_out(of_p, ob_p, of_s, ob_s, r, h, n_p, t_len, mod, out_norm, wo, g2, router_w):
    n, d = h.shape
    tm = _row_tile(n_p, t_len, cap=1024)
    na = n_p // tm
    nt_s = t_len // tm
    mod_idx = lambda i: (jnp.where(i < na, 0, 1 + (i - na) // nt_s), 0, 0)
    row = lambda i: (i, 0)
    ctx_row = lambda i: (jnp.minimum(i, na - 1), 0)
    lat_row = lambda i: (jnp.maximum(i - na, 0), 0)
    tp = tm // GLA_OUT_PARTS
    tri = jnp.asarray(np.tril(np.ones((tp, tp), np.float32), -1), dtype=BF16)
    return pl.pallas_call(
        functools.partial(_gla_out_kernel, na),
        grid=(n // tm,),
        in_specs=[
            pl.BlockSpec((tm, d), ctx_row),
            pl.BlockSpec((tm, d), ctx_row),
            pl.BlockSpec((tm, d), lat_row),
            pl.BlockSpec((tm, d), lat_row),
            pl.BlockSpec((tm, d), row),
            pl.BlockSpec((tm, d), row),
            pl.BlockSpec((None, 8, d), mod_idx),
            _const_spec((1, GLA_DV)),
            _const_spec((d, d)),
            _const_spec((1, d)),
            _const_spec((2, d, LANES)),
            _const_spec((tp, tp)),
        ],
        out_specs=[pl.BlockSpec((tm, d), row), pl.BlockSpec((tm, d // 2), row),
                   pl.BlockSpec((8, tm), lambda i: (0, i)), pl.BlockSpec((tm, LANES), row),
                   _const_spec((8, LANES))],
        out_shape=[jax.ShapeDtypeStruct((n, d), F32), jax.ShapeDtypeStruct((n, d // 2), jnp.uint32),
                   jax.ShapeDtypeStruct((8, n), F32), jax.ShapeDtypeStruct((n, LANES), F32),
                   jax.ShapeDtypeStruct((8, LANES), F32)],
        scratch_shapes=[pltpu.VMEM((8, LANES), F32)],
        compiler_params=_params(("arbitrary",)),
        name="gla_out_router",
    )(of_p, ob_p, of_s, ob_s, r, h, mod, out_norm, wo, g2, router_w, tri)


SC_CORES = 2
SC_SUBCORES = 16
SC_CHUNK = 64


def _sc_gather_rows(table, idx):
    nw = SC_CORES * SC_SUBCORES
    b = idx.shape[0]
    d = table.shape[1]
    assert b % (nw * SC_CHUNK) == 0
    per_w = b // nw
    n_chunks = per_w // SC_CHUNK
    mesh = plsc.VectorSubcoreMesh(core_axis_name="c", subcore_axis_name="s",
                                  num_cores=SC_CORES, num_subcores=SC_SUBCORES)

    assert n_chunks % 2 == 0

    def body(table_hbm, idx_hbm, out_hbm, idx_v, rows_v, gsem, wsem):
        wid = lax.axis_index("s") * SC_CORES + lax.axis_index("c")
        base = wid * per_w
        pltpu.sync_copy(idx_hbm.at[wid], idx_v)

        def gather(j, slot):
            return pltpu.make_async_copy(table_hbm.at[idx_v.at[j]], rows_v.at[slot], gsem.at[slot])

        def write(j, slot):
            return pltpu.make_async_copy(rows_v.at[slot], out_hbm.at[pl.ds(base + j * SC_CHUNK, SC_CHUNK)],
                                         wsem.at[slot])

        gather(0, 0).start()

        @pl.loop(0, n_chunks, step=2)
        def _(j):
            for slot in range(2):
                jj = j + slot
                gather(jj, slot).wait()

                @pl.when(jj + 1 < n_chunks)
                def _():
                    @pl.when(jj >= 1)
                    def _():
                        write(jj - 1, 1 - slot).wait()

                    gather(jj + 1, 1 - slot).start()

                write(jj, slot).start()

        write(n_chunks - 2, 0).wait()
        write(n_chunks - 1, 1).wait()

    return pl.kernel(
        body,
        out_type=jax.ShapeDtypeStruct((b, d), table.dtype),
        mesh=mesh,
        scratch_types=[pltpu.VMEM((n_chunks, SC_CHUNK), I32),
                       pltpu.VMEM((2, SC_CHUNK, d), table.dtype),
                       pltpu.SemaphoreType.DMA((2,)),
                       pltpu.SemaphoreType.DMA((2,))],
        name="sc_gather_rows",
    )(table, idx.reshape(nw, n_chunks, SC_CHUNK))


def _sc_scatter_rows(rows, idx2, p):
    nw = SC_CORES * SC_SUBCORES
    n, d = rows.shape
    assert n % (nw * SC_CHUNK) == 0
    per_w = n // nw
    n_chunks = per_w // SC_CHUNK
    mesh = plsc.VectorSubcoreMesh(core_axis_name="c", subcore_axis_name="s",
                                  num_cores=SC_CORES, num_subcores=SC_SUBCORES)

    assert n_chunks % 2 == 0

    def body(rows_hbm, idx0_hbm, idx1_hbm, out_hbm, idx0_v, idx1_v, rows_v, rsem, ssem):
        wid = lax.axis_index("s") * SC_CORES + lax.axis_index("c")
        base = wid * per_w
        pltpu.sync_copy(idx0_hbm.at[wid], idx0_v)
        pltpu.sync_copy(idx1_hbm.at[wid], idx1_v)

        def read(j, slot):
            return pltpu.make_async_copy(rows_hbm.at[pl.ds(base + j * SC_CHUNK, SC_CHUNK)], rows_v.at[slot],
                                         rsem.at[slot])

        read(0, 0).start()

        @pl.loop(0, n_chunks, step=2)
        def _(j):
            for slot in range(2):
                jj = j + slot
                read(jj, slot).wait()

                @pl.when(jj + 1 < n_chunks)
                def _():
                    read(jj + 1, 1 - slot).start()

                s0 = pltpu.make_async_copy(rows_v.at[slot], out_hbm.at[idx0_v.at[jj]], ssem.at[0])
                s1 = pltpu.make_async_copy(rows_v.at[slot], out_hbm.at[idx1_v.at[jj]], ssem.at[1])
                s0.start()
                s1.start()
                s0.wait()
                s1.wait()

    return pl.kernel(
        body,
        out_type=jax.ShapeDtypeStruct((p, d), rows.dtype),
        mesh=mesh,
        scratch_types=[pltpu.VMEM((n_chunks, SC_CHUNK), I32),
                       pltpu.VMEM((n_chunks, SC_CHUNK), I32),
                       pltpu.VMEM((2, SC_CHUNK, d), rows.dtype),
                       pltpu.SemaphoreType.DMA((2,)),
                       pltpu.SemaphoreType.DMA((2,))],
        name="sc_scatter_rows",
    )(rows, idx2[0].reshape(nw, n_chunks, SC_CHUNK), idx2[1].reshape(nw, n_chunks, SC_CHUNK))


EXPERT_TILE = 1024
EXPERT_SUB = 256
EXPERT_FF = 512
EXPERT_VMEM = 60 * 1024 * 1024


def _pack_halves(x):
    k = x.shape[1] // 2
    lo = lax.bitcast_convert_type(x[:, :k].astype(BF16).astype(F32), jnp.uint32)
    hi = lax.bitcast_convert_type(x[:, k:].astype(BF16).astype(F32), jnp.uint32)
    return (lo >> 16) | (hi & jnp.uint32(0xFFFF0000))


def _unpack_halves(w):
    lo = lax.bitcast_convert_type(w << 16, F32).astype(BF16)
    hi = lax.bitcast_convert_type(w & jnp.uint32(0xFFFF0000), F32).astype(BF16)
    return lo, hi


def _expert_kernel(nf, te_ref, ts_ref, nu_ref, nv_ref, x_ref, wg_ref, wu_ref, wd_ref, y_ref, acc, xb):
    i = pl.program_id(0)
    f = pl.program_id(1)
    tm = x_ref.shape[0]
    half = x_ref.shape[1]
    sub = EXPERT_SUB
    nvalid = nv_ref[i]

    def compute(nrows, first, last):
        rows = slice(0, nrows)
        if first:
            rid = lax.broadcasted_iota(I32, (nrows, half), 0)
            lo, hi = _unpack_halves(x_ref[rows, :])
            zero = jnp.zeros_like(lo)
            xb[rows, 0:half] = jnp.where(rid < nvalid, lo, zero)
            xb[rows, half:2 * half] = jnp.where(rid < nvalid, hi, zero)
        x = xb[rows, :]
        hg = jnp.dot(x, wg_ref[...].astype(BF16), preferred_element_type=F32)
        hu = jnp.dot(x, wu_ref[...].astype(BF16), preferred_element_type=F32)
        act = (_silu(hg) * hu).astype(BF16)
        part = jnp.dot(act, wd_ref[...].astype(BF16), preferred_element_type=F32)
        total = part if first else acc[rows, :] + part
        if last:
            y_ref[rows, :] = _pack_halves(total)
        else:
            acc[rows, :] = total

    assert nf >= 2
    for k in range(1, tm // sub + 1):
        has_k = (nvalid > (k - 1) * sub) & (nvalid <= k * sub)

        @pl.when(has_k & (f == 0))
        def _():
            compute(k * sub, True, False)

        @pl.when(has_k & (f > 0) & (f < nf - 1))
        def _():
            compute(k * sub, False, False)

        @pl.when(has_k & (f == nf - 1))
        def _():
            compute(k * sub, False, True)
            if k * sub < tm:
                y_ref[k * sub:tm, :] = jnp.zeros((tm - k * sub, half), y_ref.dtype)

    @pl.when((nvalid == 0) & (f == nf - 1))
    def _():
        y_ref[...] = jnp.zeros_like(y_ref)


def _experts(x, tile_expert, tile_src, n_used, n_valid, wg, wu, wd, tm, tf):
    p, half = x.shape
    d = 2 * half
    ne, _, dff = wg.shape
    nf = dff // tf
    assert dff % tf == 0 and tm % EXPERT_SUB == 0

    def fidx(i, f, nu):
        return jnp.where(i < nu[0], f, nf - 1)

    grid_spec = pltpu.PrefetchScalarGridSpec(
        num_scalar_prefetch=4,
        grid=(p // tm, nf),
        in_specs=[
            pl.BlockSpec((tm, half), lambda i, f, te, ts, nu, nv: (ts[i], 0)),
            pl.BlockSpec((None, d, tf), lambda i, f, te, ts, nu, nv: (te[i], 0, fidx(i, f, nu))),
            pl.BlockSpec((None, d, tf), lambda i, f, te, ts, nu, nv: (te[i], 0, fidx(i, f, nu))),
            pl.BlockSpec((None, tf, d), lambda i, f, te, ts, nu, nv: (te[i], fidx(i, f, nu), 0)),
        ],
        out_specs=pl.BlockSpec((tm, half), lambda i, f, te, ts, nu, nv: (i, 0)),
        scratch_shapes=[pltpu.VMEM((tm, d), F32), pltpu.VMEM((tm, d), BF16)],
    )
    return pl.pallas_call(
        functools.partial(_expert_kernel, nf),
        grid_spec=grid_spec,
        out_shape=jax.ShapeDtypeStruct((p, half), jnp.uint32),
        compiler_params=_params(("arbitrary", "arbitrary"), EXPERT_VMEM),
        name="moe_experts",
    )(tile_expert, tile_src, n_used, n_valid, x, wg, wu, wd)


def _combine_kernel(wt_ref, h_ref, mod_ref, y0_ref, y1_ref, o_ref):
    wt = wt_ref[...]
    half = y0_ref.shape[1]
    lo0, hi0 = _unpack_halves(y0_ref[...])
    lo1, hi1 = _unpack_halves(y1_ref[...])
    w0, w1 = wt[:, 0:1], wt[:, 1:2]
    h = h_ref[...]
    gate = mod_ref[5:6, :]
    o_ref[:, 0:half] = h[:, 0:half] + gate[:, 0:half] * (w0 * lo0.astype(F32) + w1 * lo1.astype(F32))
    o_ref[:, half:] = h[:, half:] + gate[:, half:] * (w0 * hi0.astype(F32) + w1 * hi1.astype(F32))


def _combine(yg, wt, h, mod, row_off, n_rows, n_mod, mod_off, tm):
    d = h.shape[1]
    nt = n_rows // tm
    toff = row_off // tm
    per_mod = n_rows // n_mod // tm
    return pl.pallas_call(
        _combine_kernel,
        grid=(nt,),
        in_specs=[
            pl.BlockSpec((tm, LANES), lambda i: (toff + i, 0)),
            pl.BlockSpec((tm, d), lambda i: (toff + i, 0)),
            pl.BlockSpec((None, 8, d), lambda i: (mod_off + i // per_mod, 0, 0)),
            pl.BlockSpec((None, tm, d // 2), lambda i: (0, i, 0)),
            pl.BlockSpec((None, tm, d // 2), lambda i: (1, i, 0)),
        ],
        out_specs=pl.BlockSpec((tm, d), lambda i: (i, 0)),
        out_shape=jax.ShapeDtypeStruct((n_rows, d), F32),
        compiler_params=_params(("arbitrary",)),
        name="moe_combine",
    )(wt, h, mod, yg, yg)


def _route(route, counts, n, tm):
    cnt = counts[0, :N_EXPERTS].astype(I32)
    padded = ((cnt + tm - 1) // tm) * tm
    gend = jnp.cumsum(padded)
    goff = gend - padded
    e = route[0:2].astype(I32)
    rank = route[2:4].astype(I32)
    onehot = e[:, :, None] == jnp.arange(N_EXPERTS, dtype=I32)[None, None, :]
    dest = jnp.sum(jnp.where(onehot, goff[None, None, :], 0), axis=-1) + rank
    p = 2 * n + N_EXPERTS * tm
    n_used = gend[-1] // tm
    tiles = jnp.arange(p // tm, dtype=I32)
    tile_src = jnp.minimum(tiles, n_used - 1)
    tile_expert = jnp.minimum(jnp.sum((gend[None, :] <= (tile_src * tm)[:, None]).astype(I32), axis=1),
                              N_EXPERTS - 1)
    used = goff[tile_expert] + cnt[tile_expert]
    n_valid = jnp.where(tiles < n_used, jnp.clip(used - tiles * tm, 0, tm), 0).astype(I32)
    return dest, p, tile_expert, tile_src, n_used.reshape(1).astype(I32), n_valid


def kernel(x_prompt, x_sample, cache_k, cache_v, state_fwd, state_bwd, c, c_ctx, ada_w, ada_b, norm1_g, norm2_g, attn_w_qkv, attn_q_norm, attn_k_norm, attn_sink, attn_w_o, gla_w_in, gla_gate_w1, gla_gate_w2, gla_gate_b, gla_out_norm, gla_w_o, ffn_w_gate, ffn_w_up, ffn_w_down, moe_router, moe_w_gate, moe_w_up, moe_w_down):
    bp, seq, d = x_prompt.shape
    db, t_len, _ = x_sample.shape
    n_p, n_s = bp * seq, db * t_len
    n = n_p + n_s
    xp = x_prompt.reshape(n_p, d)
    xs = x_sample.reshape(n_s, d)

    cond = jnp.concatenate([c_ctx[None, :], c], axis=0)
    assert cond.shape[0] <= 8
    cond_t = jnp.pad(cond, ((0, 8 - cond.shape[0]), (0, 0))).T
    mods = _modulation(cond_t, cond.shape[0], ada_w, ada_b)

    nk = N_KV_HEADS * HEAD_DIM
    qn = jnp.tile(attn_q_norm[0], N_HEADS)[None, :]
    kn = jnp.tile(attn_k_norm[0], N_KV_HEADS)[None, :]
    q, kt, vv, ck_new, cv_new = _qkv(xp, xs, t_len, mods[0], norm1_g[0][None, :],
                                     attn_w_qkv[0].astype(BF16), qn, kn)
    wo0 = attn_w_o[0].astype(BF16)
    sink = attn_sink[0]
    hp = _ctx_attention(sink, q, kt, vv, xp, seq, mods[0], wo0)
    ck = cache_k[:, 0].astype(BF16)
    cv = cache_v[:, 0].astype(BF16)
    ckt = jnp.concatenate([ck, ck], axis=-1).transpose(0, 2, 3, 1)
    cvv = jnp.concatenate([cv, cv], axis=-1).reshape(db, cv.shape[1], N_KV_HEADS * LANES)
    hs = _lat_attention(sink, q, kt, vv, ckt, cvv, xs, n_p, t_len, mods[0], wo0)
    new_k = ck_new.reshape(bp, 1, seq, N_KV_HEADS, HEAD_DIM)
    new_v = cv_new.reshape(bp, 1, seq, N_KV_HEADS, HEAD_DIM)
    h = _ffn(hp, hs, t_len, mods[0], norm2_g[0][None, :], ffn_w_gate[0].astype(BF16),
             ffn_w_up[0].astype(BF16), ffn_w_down[0].astype(BF16))

    hk = GLA_HEADS * GLA_DK
    rank = GLA_GATE_RANK
    w1 = jnp.concatenate([gla_gate_w1[0, 0], gla_gate_w1[0, 1]], axis=1)
    w1 = jnp.pad(w1, ((0, 0), (0, LANES - 2 * rank))).astype(BF16)
    w2 = jnp.zeros((LANES, 2 * hk), F32)
    w2 = w2.at[0:rank, 0:hk].set(gla_gate_w2[0, 0]).at[rank:2 * rank, hk:].set(gla_gate_w2[0, 1]).astype(BF16)
    gate_b = gla_gate_b[0].reshape(1, 2 * hk)
    qk, v, r, bc = _gla_in(h, n_p, t_len, mods[1], norm1_g[1][None, :], gla_w_in[0].astype(BF16),
                           w1, w2, gate_b)
    of_p, ob_p, new_sf, new_sb = _gla_scan(qk, v, bc, None, None, 0, bp, seq)
    of_s, ob_s, _, _ = _gla_scan(qk, v, bc, state_fwd[:, 0], state_bwd[:, 0], n_p // t_len, db, t_len)
    rw = jnp.pad(moe_router[0], ((0, 0), (0, LANES - N_EXPERTS)))
    rw_hi = rw.astype(BF16)
    rw_lo = (rw - rw_hi.astype(F32)).astype(BF16)
    h, f, route, wt, counts = _gla_out(
        of_p.reshape(n_p, d), ob_p.reshape(n_p, d), of_s.reshape(n_s, d), ob_s.reshape(n_s, d), r, h,
        n_p, t_len, mods[1], gla_out_norm[0][None, :], gla_w_o[0].astype(BF16), norm2_g[1][None, :],
        jnp.stack([rw_hi, rw_lo]))

    tm_e = EXPERT_TILE
    dest_k, p_rows, tile_expert, tile_src, n_used, n_valid = _route(route, counts, n, tm_e)
    xg = _sc_scatter_rows(f, dest_k, p_rows)
    xg, new_k = lax.optimization_barrier((xg, new_k))
    y = _experts(xg, tile_expert, tile_src, n_used, n_valid, moe_w_gate[0], moe_w_up[0],
                 moe_w_down[0], tm_e, EXPERT_FF)
    yg_p = _sc_gather_rows(y, dest_k[:, :n_p].reshape(2 * n_p)).reshape(2, n_p, d // 2)
    yg_s = _sc_gather_rows(y, dest_k[:, n_p:].reshape(2 * n_s)).reshape(2, n_s, d // 2)
    tm_c = _row_tile(n_p, t_len)
    y_p = _combine(yg_p, wt, h, mods[1], 0, n_p, 1, 0, tm_c)
    y_s = _combine(yg_s, wt, h, mods[1], n_p, n_s, db, 1, tm_c)

    return (y_p.reshape(bp, seq, d), y_s.reshape(db, t_len, d), new_k, new_v,
            new_sf[:, None], new_sb[:, None])
```

```python
import functools
import math

import numpy as np
import jax
import jax.numpy as jnp
from jax import lax
from jax.experimental import pallas as pl
from jax.experimental.pallas import tpu as pltpu
from jax.experimental.pallas import tpu_sc as plsc

F32 = jnp.float32
BF16 = jnp.bfloat16
I32 = jnp.int32

D_MODEL = 1024
N_HEADS = 16
N_KV_HEADS = 4
HEAD_DIM = 64
GRID_W = 64
WINDOW = 128
ATTN_BLOCK = 128
ROPE_THETA = 10000.0
GLA_HEADS = 4
GLA_DK = 128
GLA_DV = 256
GLA_GATE_RANK = 16
GLA_GATE_TAU = 16.0
GLA_CHUNK = 64
N_EXPERTS = 8
NORM_EPS = 1e-6
NEG_INF = -1e30

LANES = 128
VMEM_LIMIT = 56 * 1024 * 1024


def _params(sem, vmem=VMEM_LIMIT):
    return pltpu.CompilerParams(dimension_semantics=sem, vmem_limit_bytes=vmem)


def _row_tile(*counts, cap=512):
    t = cap
    while any(c % t for c in counts):
        t //= 2
    assert t >= 8
    return t


def _rms(x):
    return x * lax.rsqrt(jnp.mean(x * x, axis=-1, keepdims=True) + NORM_EPS)


def _adaln(x, g, shift, scale):
    return _rms(x) * (g * (1.0 + scale)) + shift


def _silu(x):
    return x * jax.nn.sigmoid(x)


def _const_spec(shape):
    nd = len(shape)
    return pl.BlockSpec(shape, lambda *_: (0,) * nd)


def _mod_kernel(n_cond, ct_ref, w_ref, b_ref, o_ref):
    ct = ct_ref[...]
    s = _silu(ct)
    w = w_ref[...]
    rows = [jnp.sum(w * s[:, r:r + 1], axis=0, keepdims=True) for r in range(n_cond)]
    rows += [jnp.zeros_like(rows[0])] * (8 - n_cond)
    o_ref[...] = jnp.concatenate(rows, axis=0) + b_ref[...]


def _modulation(cond_t, n_cond, ada_w, ada_b):
    depth, d, n6 = ada_w.shape
    tn = 2048
    out = pl.pallas_call(
        functools.partial(_mod_kernel, n_cond),
        grid=(depth, n6 // tn),
        in_specs=[
            pl.BlockSpec((d, 8), lambda l, j: (0, 0)),
            pl.BlockSpec((None, d, tn), lambda l, j: (l, 0, j)),
            pl.BlockSpec((None, 1, tn), lambda l, j: (l, 0, j)),
        ],
        out_specs=pl.BlockSpec((None, 8, tn), lambda l, j: (l, 0, j)),
        out_shape=jax.ShapeDtypeStruct((depth, 8, n6), F32),
        compiler_params=_params(("arbitrary", "arbitrary")),
        name="modulation",
    )(cond_t, ada_w, ada_b.reshape(depth, 1, n6))
    m = out[:, :n_cond].reshape(depth, n_cond, 6, d)
    return jnp.pad(m, ((0, 0), (0, 0), (0, 2), (0, 0)))


LOG2E = math.log2(math.e)


def _dup_half(k2, half):
    lane = lax.broadcasted_iota(I32, k2.shape, 1)
    lo = lane < HEAD_DIM
    r = pltpu.roll(k2, HEAD_DIM, 1)
    return jnp.where(lo, k2, r) if half == 0 else jnp.where(lo, r, k2)


QKV_PARTS = 2


def _qkv_kernel(na, xa_ref, xb_ref, mod_ref, g_ref, w_ref, qn_ref, kn_ref, bd_ref,
                cos_ref, sin_ref, q_out, kt_out, vv_out, ck_out, cv_out):
    i = pl.program_id(0)
    is_ctx = i < na
    tm = xa_ref.shape[0] // QKV_PARTS
    for part in range(QKV_PARTS):
        _qkv_rows(slice(part * tm, (part + 1) * tm), tm, is_ctx, xa_ref, xb_ref, mod_ref, g_ref, w_ref,
                  qn_ref, kn_ref, bd_ref, cos_ref, sin_ref, q_out, kt_out, vv_out, ck_out, cv_out)


def _qkv_rows(rows, tm, is_ctx, xa_ref, xb_ref, mod_ref, g_ref, w_ref, qn_ref, kn_ref, bd_ref,
              cos_ref, sin_ref, q_out, kt_out, vv_out, ck_out, cv_out):
    x = jnp.where(is_ctx, xa_ref[rows, :], xb_ref[rows, :])
    a = _adaln(x, g_ref[...], mod_ref[0:1, :], mod_ref[1:2, :])
    y = jnp.dot(a.astype(BF16), w_ref[...], preferred_element_type=F32)
    cos = jnp.where(is_ctx, 1.0, cos_ref[rows, :])
    sin = jnp.where(is_ctx, 0.0, sin_ref[rows, :])
    lane = lax.broadcasted_iota(I32, (tm, LANES), 1)
    first16 = (lane % 32) < 16

    def norm_rope(z, wt):
        ss = jnp.dot((z * z).astype(BF16), bd_ref[...], preferred_element_type=F32)
        zn = z * lax.rsqrt(ss * (1.0 / HEAD_DIM) + NORM_EPS) * wt
        outs = []
        for c in range(2):
            t = zn[:, c * LANES:(c + 1) * LANES]
            partner = jnp.where(first16, pltpu.roll(t, LANES - 16, 1), pltpu.roll(t, 16, 1))
            outs.append(t * cos + partner * sin)
        return jnp.concatenate(outs, axis=1)

    nq = N_HEADS * HEAD_DIM
    nk = N_KV_HEADS * HEAD_DIM
    scale = HEAD_DIM ** -0.5 * LOG2E
    for s in range(nq // 256):
        sl = slice(s * 256, (s + 1) * 256)
        q_out[rows, sl] = (norm_rope(y[:, sl], qn_ref[:, sl]) * scale).astype(BF16)
    k = norm_rope(y[:, nq:nq + nk], kn_ref[...])
    v = y[:, nq + nk:nq + 2 * nk]
    for g in range(N_KV_HEADS):
        c = g // 2
        kk = _dup_half(k[:, c * LANES:(c + 1) * LANES], g % 2)
        kt_out[g, :, rows] = kk.T.astype(BF16)
        vv_out[rows, g * LANES:(g + 1) * LANES] = _dup_half(v[:, c * LANES:(c + 1) * LANES], g % 2).astype(BF16)

    @pl.when(is_ctx)
    def _():
        ck_out[rows, :] = k
        cv_out[rows, :] = v


def _rope_tables(t_len):
    pos = np.arange(t_len)
    row = (pos // GRID_W).astype(np.float32)[:, None]
    col = (pos % GRID_W).astype(np.float32)[:, None]
    half = HEAD_DIM // 2
    inv = (np.float32(ROPE_THETA) ** (-np.arange(0, half, 2, dtype=np.float32) / np.float32(half)))[None, :]
    ar, ac = row * inv, col * inv
    cos = np.concatenate([np.cos(ar), np.cos(ar), np.cos(ac), np.cos(ac)], axis=1)
    sin = np.concatenate([-np.sin(ar), np.sin(ar), -np.sin(ac), np.sin(ac)], axis=1)
    return (jnp.asarray(np.tile(cos, (1, 2)), dtype=F32), jnp.asarray(np.tile(sin, (1, 2)), dtype=F32))


def _qkv(xp, xs, t_len, mod, g, w, qn, kn):
    n_p, n_s = xp.shape[0], xs.shape[0]
    tm = _row_tile(n_p, t_len, cap=1024)
    na, nb = n_p // tm, n_s // tm
    nt_s = t_len // tm
    n = n_p + n_s
    d = D_MODEL
    nq, nk = N_HEADS * HEAD_DIM, N_KV_HEADS * HEAD_DIM
    cos, sin = _rope_tables(t_len)
    eye4 = jnp.kron(jnp.eye(4, dtype=F32), jnp.ones((HEAD_DIM, HEAD_DIM), F32)).astype(BF16)
    mod_idx = lambda i: (jnp.where(i < na, 0, 1 + (i - na) // nt_s), 0, 0)
    pos_idx = lambda i: (jnp.maximum(i - na, 0) % nt_s, 0)
    return pl.pallas_call(
        functools.partial(_qkv_kernel, na),
        grid=(na + nb,),
        in_specs=[
            pl.BlockSpec((tm, d), lambda i: (jnp.minimum(i, na - 1), 0)),
            pl.BlockSpec((tm, d), lambda i: (jnp.maximum(i - na, 0), 0)),
            pl.BlockSpec((None, 8, d), mod_idx),
            _const_spec((1, d)),
            _const_spec((d, nq + 2 * nk)),
            _const_spec((1, nq)),
            _const_spec((1, nk)),
            _const_spec((256, 256)),
            pl.BlockSpec((tm, LANES), pos_idx),
            pl.BlockSpec((tm, LANES), pos_idx),
        ],
        out_specs=[pl.BlockSpec((tm, nq), lambda i: (i, 0)),
                   pl.BlockSpec((N_KV_HEADS, LANES, tm), lambda i: (0, 0, i)),
                   pl.BlockSpec((tm, N_KV_HEADS * LANES), lambda i: (i, 0)),
                   pl.BlockSpec((tm, nk), lambda i: (jnp.minimum(i, na - 1), 0)),
                   pl.BlockSpec((tm, nk), lambda i: (jnp.minimum(i, na - 1), 0))],
        out_shape=[jax.ShapeDtypeStruct((n, nq), BF16),
                   jax.ShapeDtypeStruct((N_KV_HEADS, LANES, n), BF16),
                   jax.ShapeDtypeStruct((n, N_KV_HEADS * LANES), BF16),
                   jax.ShapeDtypeStruct((n_p, nk), F32),
                   jax.ShapeDtypeStruct((n_p, nk), F32)],
        compiler_params=_params(("arbitrary",)),
        name="qkv",
    )(xp, xs, mod, g, w, qn, kn, eye4, cos, sin)


def _attn_group(q2a, q2b, kt, vv, bias, n_bias, sinks):
    tq = q2a.shape[0]
    nk = kt.shape[1]
    lo = lax.broadcasted_iota(I32, (tq, LANES), 1) < HEAD_DIM
    top = lax.broadcasted_iota(I32, kt.shape, 0) < HEAD_DIM
    zk = jnp.zeros_like(kt)
    kbd = jnp.concatenate([jnp.where(top, kt, zk), jnp.where(top, zk, kt)], axis=1)
    left = lax.broadcasted_iota(I32, vv.shape, 1) < HEAD_DIM
    zv = jnp.zeros_like(vv)
    vbd = jnp.concatenate([jnp.where(left, vv, zv), jnp.where(left, zv, vv)], axis=0)
    s = jnp.dot(jnp.concatenate([q2a, q2b], axis=0), kbd, preferred_element_type=F32)
    p_rows, scales = [], []
    for t in range(2):
        ph, rinv = [], []
        for hf in range(2):
            sh = s[t * tq:(t + 1) * tq, hf * nk:(hf + 1) * nk]
            if bias is not None:
                sh = jnp.concatenate([sh[:, :n_bias] + bias, sh[:, n_bias:]], axis=1)
            sk = sinks[2 * t + hf]
            m = jnp.maximum(jnp.max(sh, axis=-1, keepdims=True), sk)
            p = jnp.exp2(sh - m)
            rinv.append(1.0 / (jnp.sum(p, axis=-1, keepdims=True) + jnp.exp2(sk - m)))
            ph.append(p.astype(BF16))
        p_rows.append(jnp.concatenate(ph, axis=1))
        scales.append(jnp.where(lo, rinv[0], rinv[1]))
    o = jnp.dot(jnp.concatenate(p_rows, axis=0), vbd, preferred_element_type=F32)
    return o[0:tq] * scales[0], o[tq:2 * tq] * scales[1]


def _ctx_attn_kernel(sink_ref, q_ref, kt_ref, vv_ref, x_ref, mod_ref, wo_ref, o_ref, osc):
    for g in range(N_KV_HEADS):
        q2a = q_ref[:, (2 * g) * LANES:(2 * g + 1) * LANES]
        q2b = q_ref[:, (2 * g + 1) * LANES:(2 * g + 2) * LANES]
        sinks = [sink_ref[4 * g + j] * LOG2E for j in range(4)]
        oa, ob = _attn_group(q2a, q2b, kt_ref[g], vv_ref[:, g * LANES:(g + 1) * LANES], None, 0, sinks)
        osc[:, (2 * g) * LANES:(2 * g + 1) * LANES] = oa.astype(BF16)
        osc[:, (2 * g + 1) * LANES:(2 * g + 2) * LANES] = ob.astype(BF16)
    att = jnp.dot(osc[...], wo_ref[...], preferred_element_type=F32)
    o_ref[...] = x_ref[...] + mod_ref[2:3, :] * att


def _ctx_attention(sink, q, kt, vv, xp, seq, mod, wo):
    n_p, d = xp.shape
    nb = n_p // seq
    return pl.pallas_call(
        _ctx_attn_kernel,
        grid=(nb,),
        in_specs=[
            pl.BlockSpec(memory_space=pltpu.SMEM),
            pl.BlockSpec((seq, d), lambda b: (b, 0)),
            pl.BlockSpec((N_KV_HEADS, LANES, seq), lambda b: (0, 0, b)),
            pl.BlockSpec((seq, N_KV_HEADS * LANES), lambda b: (b, 0)),
            pl.BlockSpec((seq, d), lambda b: (b, 0)),
            pl.BlockSpec((None, 8, d), lambda b: (0, 0, 0)),
            _const_spec((d, d)),
        ],
        out_specs=pl.BlockSpec((seq, d), lambda b: (b, 0)),
        out_shape=jax.ShapeDtypeStruct((n_p, d), F32),
        scratch_shapes=[pltpu.VMEM((seq, d), BF16)],
        compiler_params=_params(("arbitrary",)),
        name="ctx_attention",
    )(sink, q, kt, vv, xp, mod, wo)


LAT_QB = 4


def _lat_attn_kernel(t_len, sink_ref, q_ref, ktp_ref, kto_ref, ktn_ref, vvp_ref, vvo_ref, vvn_ref,
                     ckt_ref, cvv_ref, x_ref, mod_ref, wo_ref, o_ref, osc):
    step = pl.program_id(1)
    tq = ATTN_BLOCK
    nloc = 3 * ATTN_BLOCK
    qi = lax.broadcasted_iota(I32, (tq, nloc), 0)
    kj = lax.broadcasted_iota(I32, (tq, nloc), 1)
    in_window = jnp.abs(qi + tq - kj) <= WINDOW
    for u in range(LAT_QB):
        n = step * LAT_QB + u
        kpos = (n - 1) * tq + kj
        bias = jnp.where(in_window & (kpos >= 0) & (kpos < t_len), 0.0, NEG_INF)
        rows = slice(u * tq, (u + 1) * tq)
        for g in range(N_KV_HEADS):
            vs = slice(g * LANES, (g + 1) * LANES)
            kts = [ktp_ref[g]] + [kto_ref[g, :, j * tq:(j + 1) * tq] for j in range(LAT_QB)] + [ktn_ref[g]]
            vvs = [vvp_ref[:, vs]] + [vvo_ref[j * tq:(j + 1) * tq, vs] for j in range(LAT_QB)] + [vvn_ref[:, vs]]
            kt = jnp.concatenate(kts[u:u + 3] + [ckt_ref[g]], axis=1)
            vv = jnp.concatenate(vvs[u:u + 3] + [cvv_ref[:, vs]], axis=0)
            q2a = q_ref[rows, (2 * g) * LANES:(2 * g + 1) * LANES]
            q2b = q_ref[rows, (2 * g + 1) * LANES:(2 * g + 2) * LANES]
            sinks = [sink_ref[4 * g + j] * LOG2E for j in range(4)]
            oa, ob = _attn_group(q2a, q2b, kt, vv, bias, nloc, sinks)
            osc[rows, (2 * g) * LANES:(2 * g + 1) * LANES] = oa.astype(BF16)
            osc[rows, (2 * g + 1) * LANES:(2 * g + 2) * LANES] = ob.astype(BF16)
    att = jnp.dot(osc[...], wo_ref[...], preferred_element_type=F32)
    o_ref[...] = x_ref[...] + mod_ref[2:3, :] * att


def _lat_attention(sink, q, kt, vv, ckt, cvv, xs, n_p, t_len, mod, wo):
    n_s, d = xs.shape
    db = n_s // t_len
    tq = ATTN_BLOCK
    ts = LAT_QB * tq
    nblk = t_len // tq
    nstep = t_len // ts
    assert n_p % ts == 0 and t_len % ts == 0
    off = n_p // tq
    npast = ckt.shape[3]
    g4 = N_KV_HEADS

    def nbr(delta):
        return lambda b, m: off + b * nblk + jnp.clip(m * LAT_QB + delta, 0, nblk - 1)

    own = lambda b, m: n_p // ts + b * nstep + m
    return pl.pallas_call(
        functools.partial(_lat_attn_kernel, t_len),
        grid=(db, nstep),
        in_specs=[
            pl.BlockSpec(memory_space=pltpu.SMEM),
            pl.BlockSpec((ts, d), lambda b, m: (own(b, m), 0)),
            pl.BlockSpec((g4, LANES, tq), lambda b, m: (0, 0, nbr(-1)(b, m))),
            pl.BlockSpec((g4, LANES, ts), lambda b, m: (0, 0, own(b, m))),
            pl.BlockSpec((g4, LANES, tq), lambda b, m: (0, 0, nbr(LAT_QB)(b, m))),
            pl.BlockSpec((tq, g4 * LANES), lambda b, m: (nbr(-1)(b, m), 0)),
            pl.BlockSpec((ts, g4 * LANES), lambda b, m: (own(b, m), 0)),
            pl.BlockSpec((tq, g4 * LANES), lambda b, m: (nbr(LAT_QB)(b, m), 0)),
            pl.BlockSpec((None, g4, LANES, npast), lambda b, m: (b, 0, 0, 0)),
            pl.BlockSpec((None, npast, g4 * LANES), lambda b, m: (b, 0, 0)),
            pl.BlockSpec((ts, d), lambda b, m: (b * nstep + m, 0)),
            pl.BlockSpec((None, 8, d), lambda b, m: (1 + b, 0, 0)),
            _const_spec((d, d)),
        ],
        out_specs=pl.BlockSpec((ts, d), lambda b, m: (b * nstep + m, 0)),
        out_shape=jax.ShapeDtypeStruct((n_s, d), F32),
        scratch_shapes=[pltpu.VMEM((ts, d), BF16)],
        compiler_params=_params(("arbitrary", "arbitrary")),
        name="lat_attention",
    )(sink, q, kt, kt, kt, vv, vv, vv, ckt, cvv, xs, mod, wo)


def _ffn_kernel(na, fc, xa_ref, xb_ref, mod_ref, g_ref, wg_ref, wu_ref, wd_ref, o_ref, acc):
    i = pl.program_id(0)
    x = jnp.where(i < na, xa_ref[...], xb_ref[...])
    a = _adaln(x, g_ref[...], mod_ref[3:4, :], mod_ref[4:5, :]).astype(BF16)
    nf = wg_ref.shape[1] // fc
    for f in range(nf):
        sl = slice(f * fc, (f + 1) * fc)
        hg = jnp.dot(a, wg_ref[:, sl], preferred_element_type=F32)
        hu = jnp.dot(a, wu_ref[:, sl], preferred_element_type=F32)
        act = (_silu(hg) * hu).astype(BF16)
        part = jnp.dot(act, wd_ref[sl, :], preferred_element_type=F32)
        if f == 0:
            acc[...] = part
        else:
            acc[...] += part
    o_ref[...] = x + mod_ref[5:6, :] * acc[...]


def _ffn(hp, hs, t_len, mod, g, wg, wu, wd):
    n_p, n_s = hp.shape[0], hs.shape[0]
    d, dff = wg.shape
    tm = _row_tile(n_p, t_len)
    na, nb = n_p // tm, n_s // tm
    nt_s = t_len // tm
    fc = 256
    assert dff % fc == 0
    mod_idx = lambda i: (jnp.where(i < na, 0, 1 + (i - na) // nt_s), 0, 0)
    return pl.pallas_call(
        functools.partial(_ffn_kernel, na, fc),
        grid=(na + nb,),
        in_specs=[
            pl.BlockSpec((tm, d), lambda i: (jnp.minimum(i, na - 1), 0)),
            pl.BlockSpec((tm, d), lambda i: (jnp.maximum(i - na, 0), 0)),
            pl.BlockSpec((None, 8, d), mod_idx),
            _const_spec((1, d)),
            _const_spec((d, dff)),
            _const_spec((d, dff)),
            _const_spec((dff, d)),
        ],
        out_specs=pl.BlockSpec((tm, d), lambda i: (i, 0)),
        out_shape=jax.ShapeDtypeStruct((n_p + n_s, d), F32),
        scratch_shapes=[pltpu.VMEM((tm, d), F32)],
        compiler_params=_params(("arbitrary",)),
        name="ffn",
    )(hp, hs, mod, g, wg, wu, wd)


def _split2(x):
    hi = x.astype(BF16)
    lo = (x - hi.astype(F32)).astype(BF16)
    return hi, lo


def _gla_in_kernel(x_ref, mod_ref, g_ref, w_ref, w1_ref, w2_ref, gb_ref, tri_ref,
                   qk_out, v_out, r_out, b_out):
    x = x_ref[...]
    tm = x.shape[0]
    a = _adaln(x, g_ref[...], mod_ref[0:1, :], mod_ref[1:2, :]).astype(BF16)
    hk = GLA_HEADS * GLA_DK
    hv = GLA_HEADS * GLA_DV
    q = jnp.dot(a, w_ref[:, 0:hk], preferred_element_type=F32)
    qk_out[:, 0:hk] = (q * (GLA_DK ** -0.5)).astype(BF16)
    qk_out[:, hk:2 * hk] = jnp.dot(a, w_ref[:, hk:2 * hk], preferred_element_type=F32).astype(BF16)
    v_out[...] = jnp.dot(a, w_ref[:, 2 * hk:2 * hk + hv], preferred_element_type=F32).astype(BF16)
    r_out[...] = jnp.dot(a, w_ref[:, 2 * hk + hv:2 * hk + 2 * hv], preferred_element_type=F32).astype(BF16)
    z1 = jnp.dot(a, w1_ref[...], preferred_element_type=F32)
    z = jnp.dot(z1.astype(BF16), w2_ref[...], preferred_element_type=F32) + gb_ref[...]
    gate = (jnp.minimum(z, 0.0) - jnp.log(1.0 + jnp.exp(-jnp.abs(z)))) * (1.0 / GLA_GATE_TAU)
    c = GLA_CHUNK
    tri = tri_ref[...]
    for j in range(tm // c):
        rows = slice(j * c, (j + 1) * c)
        for dr in range(2):
            cols = slice(dr * hk, (dr + 1) * hk)
            hi, lo = _split2(gate[rows, cols])
            b_out[rows, cols] = (jnp.dot(tri[dr], hi, preferred_element_type=F32)
                                 + jnp.dot(tri[dr], lo, preferred_element_type=F32))


def _gla_in(h, n_p, t_len, mod, g, w_in, w1, w2, gate_b):
    n, d = h.shape
    tm = _row_tile(n_p, t_len, cap=1024)
    na = n_p // tm
    nt_s = t_len // tm
    hk, hv = GLA_HEADS * GLA_DK, GLA_HEADS * GLA_DV
    c = GLA_CHUNK
    lower = jnp.tril(jnp.ones((c, c), F32))
    tri = jnp.stack([lower, lower.T]).astype(BF16)
    mod_idx = lambda i: (jnp.where(i < na, 0, 1 + (i - na) // nt_s), 0, 0)
    row = lambda i: (i, 0)
    return pl.pallas_call(
        _gla_in_kernel,
        grid=(n // tm,),
        in_specs=[
            pl.BlockSpec((tm, d), row),
            pl.BlockSpec((None, 8, d), mod_idx),
            _const_spec((1, d)),
            _const_spec(w_in.shape),
            _const_spec(w1.shape),
            _const_spec(w2.shape),
            _const_spec((1, 2 * hk)),
            _const_spec((2, c, c)),
        ],
        out_specs=[pl.BlockSpec((tm, 2 * hk), row), pl.BlockSpec((tm, hv), row),
                   pl.BlockSpec((tm, hv), row), pl.BlockSpec((tm, 2 * hk), row)],
        out_shape=[jax.ShapeDtypeStruct((n, 2 * hk), BF16), jax.ShapeDtypeStruct((n, hv), BF16),
                   jax.ShapeDtypeStruct((n, hv), BF16), jax.ShapeDtypeStruct((n, 2 * hk), F32)],
        compiler_params=_params(("arbitrary",)),
        name="gla_in",
    )(h, mod, g, w_in, w1, w2, gate_b, tri)


def _gla_direction(qk_ref, v_ref, b_ref, st, o_ref, forward, qd_s, att_s, upd_s, dec_s):
    c = GLA_CHUNK
    ncb = qk_ref.shape[0] // c
    hk = GLA_HEADS * GLA_DK
    dk, dv = GLA_DK, GLA_DV
    ri = lax.broadcasted_iota(I32, (c, c), 0)
    ci_ = lax.broadcasted_iota(I32, (c, c), 1)
    keep = (ci_ <= ri) if forward else (ci_ >= ri)

    for ch in range(ncb):
        rows = slice(ch * c, (ch + 1) * c)
        b = b_ref[rows, :]
        if forward:
            b_end, b_mid = b[c - 1:c, :], b[c // 2 - 1:c // 2, :]
        else:
            b_end, b_mid = b[0:1, :], b[c // 2:c // 2 + 1, :]
        qa = qk_ref[rows, 0:hk] * jnp.exp(b - b_mid)
        ka = qk_ref[rows, hk:2 * hk] * jnp.exp(b_mid - b)
        qd_s[rows, :] = (qa * jnp.exp(b_mid)).astype(BF16)
        kd = (ka * jnp.exp(b_end - b_mid)).astype(BF16)
        dec_s[ch:ch + 1, :] = jnp.exp(b_end)
        qab, kab = qa.astype(BF16), ka.astype(BF16)
        for h in range(GLA_HEADS):
            ks = slice(h * dk, (h + 1) * dk)
            vs = slice(h * dv, (h + 1) * dv)
            att = lax.dot_general(qab[:, ks], kab[:, ks], (((1,), (1,)), ((), ())),
                                  preferred_element_type=F32)
            att_s[ch, h] = jnp.where(keep, att, 0.0).astype(BF16)
            upd_s[ch, h] = lax.dot_general(v_ref[rows, vs], kd[:, ks], (((0,), (0,)), ((), ())),
                                           preferred_element_type=F32)

    for ch in (range(ncb) if forward else reversed(range(ncb))):
        rows = slice(ch * c, (ch + 1) * c)
        for h in range(GLA_HEADS):
            ks = slice(h * dk, (h + 1) * dk)
            vs = slice(h * dv, (h + 1) * dv)
            s = st[h]
            o = lax.dot_general(qd_s[rows, ks], s.astype(BF16), (((1,), (1,)), ((), ())),
                                preferred_element_type=F32)
            o = o + jnp.dot(att_s[ch, h], v_ref[rows, vs], preferred_element_type=F32)
            o_ref[rows, vs] = o.astype(o_ref.dtype)
            st[h] = s * dec_s[ch:ch + 1, ks] + upd_s[ch, h]


def _gla_scan_kernel(shared, from_zero, *refs):
    if shared:
        qkf_ref, vf_ref, bf_ref, bb_ref = refs[:4]
        qkb_ref, vb_ref = qkf_ref, vf_ref
        rest = refs[4:]
    else:
        qkf_ref, vf_ref, bf_ref, qkb_ref, vb_ref, bb_ref = refs[:6]
        rest = refs[6:]
    if not from_zero:
        sf0_ref, sb0_ref = rest[:2]
        rest = rest[2:]
    of_ref, ob_ref, sf_ref, sb_ref, stf, stb, qd_s, att_s, upd_s, dec_s = rest
    i = pl.program_id(1)
    nblk = pl.num_programs(1)

    @pl.when(i == 0)
    def _():
        if from_zero:
            stf[...] = jnp.zeros_like(stf)
            stb[...] = jnp.zeros_like(stb)
        else:
            for h in range(GLA_HEADS):
                stf[h] = sf0_ref[h].T
                stb[h] = sb0_ref[h].T

    _gla_direction(qkf_ref, vf_ref, bf_ref, stf, of_ref, True, qd_s, att_s, upd_s, dec_s)
    _gla_direction(qkb_ref, vb_ref, bb_ref, stb, ob_ref, False, qd_s, att_s, upd_s, dec_s)

    @pl.when(i == nblk - 1)
    def _():
        for h in range(GLA_HEADS):
            sf_ref[h] = stf[h].T
            sb_ref[h] = stb[h].T


def _gla_scan(qk, v, bc, sf0, sb0, b_off, nb, t_len):
    n = qk.shape[0]
    assert n % t_len == 0
    ns = n // t_len
    tb = _row_tile(t_len)
    nblk = t_len // tb
    qk3 = qk.reshape(ns, t_len, qk.shape[1])
    v3 = v.reshape(ns, t_len, v.shape[1])
    bc3 = bc.reshape(ns, t_len, bc.shape[1])
    h, dk, dv = GLA_HEADS, GLA_DK, GLA_DV
    hk = h * dk
    st_spec = pl.BlockSpec((None, h, dk, dv), lambda b, i: (b, 0, 0, 0))
    fwd = lambda b, i: (b_off + b, i, 0)
    bwd = lambda b, i: (b_off + b, nblk - 1 - i, 0)
    bwd_b = pl.BlockSpec((None, tb, hk), lambda b, i: (b_off + b, nblk - 1 - i, 1))
    shared = nblk == 1
    in_specs = [pl.BlockSpec((None, tb, 2 * hk), fwd), pl.BlockSpec((None, tb, h * dv), fwd),
                pl.BlockSpec((None, tb, hk), fwd)]
    operands = [qk3, v3, bc3]
    if not shared:
        in_specs += [pl.BlockSpec((None, tb, 2 * hk), bwd), pl.BlockSpec((None, tb, h * dv), bwd)]
        operands += [qk3, v3]
    in_specs += [bwd_b]
    operands += [bc3]
    from_zero = sf0 is None
    if not from_zero:
        in_specs += [st_spec, st_spec]
        operands += [sf0, sb0]
    return pl.pallas_call(
        functools.partial(_gla_scan_kernel, shared, from_zero),
        grid=(nb, nblk),
        in_specs=in_specs,
        out_specs=[pl.BlockSpec((None, tb, h * dv), lambda b, i: (b, i, 0)),
                   pl.BlockSpec((None, tb, h * dv), lambda b, i: (b, nblk - 1 - i, 0)),
                   st_spec, st_spec],
        out_shape=[jax.ShapeDtypeStruct((nb, t_len, h * dv), BF16),
                   jax.ShapeDtypeStruct((nb, t_len, h * dv), BF16),
                   jax.ShapeDtypeStruct((nb, h, dk, dv), F32),
                   jax.ShapeDtypeStruct((nb, h, dk, dv), F32)],
        scratch_shapes=[pltpu.VMEM((h, dv, dk), F32), pltpu.VMEM((h, dv, dk), F32),
                        pltpu.VMEM((tb, hk), BF16),
                        pltpu.VMEM((tb // GLA_CHUNK, h, GLA_CHUNK, GLA_CHUNK), BF16),
                        pltpu.VMEM((tb // GLA_CHUNK, h, dv, dk), F32),
                        pltpu.VMEM((max(tb // GLA_CHUNK, 8), hk), F32)],
        compiler_params=_params(("arbitrary", "arbitrary")),
        name="gla_scan",
    )(*operands)


GLA_OUT_PARTS = 4


def _gla_out_kernel(na, ofa_ref, oba_ref, ofb_ref, obb_ref, r_ref, h_ref, mod_ref, on_ref, wo_ref,
                    g2_ref, rw_ref, tri_ref, h_out, f_out, route_out, wt_out, cnt_out, run):
    i = pl.program_id(0)
    is_ctx = i < na

    @pl.when(i == 0)
    def _():
        run[...] = jnp.zeros_like(run)

    tm = tri_ref.shape[0]
    for part in range(h_ref.shape[0] // tm):
        rows = slice(part * tm, (part + 1) * tm)
        _gla_out_rows(rows, tm, is_ctx, ofa_ref, oba_ref, ofb_ref, obb_ref, r_ref, h_ref, mod_ref, on_ref,
                      wo_ref, g2_ref, rw_ref, tri_ref, h_out, f_out, route_out, wt_out, run)
    cnt_out[...] = run[...]


def _gla_out_rows(rows, tm, is_ctx, ofa_ref, oba_ref, ofb_ref, obb_ref, r_ref, h_ref, mod_ref, on_ref,
                  wo_ref, g2_ref, rw_ref, tri_ref, h_out, f_out, route_out, wt_out, run):
    o = (jnp.where(is_ctx, ofa_ref[rows, :], ofb_ref[rows, :]).astype(F32)
         + jnp.where(is_ctx, oba_ref[rows, :], obb_ref[rows, :]).astype(F32))
    dv = GLA_DV
    parts = []
    for hh in range(GLA_HEADS):
        oh = o[:, hh * dv:(hh + 1) * dv]
        parts.append(_rms(oh) * on_ref[...])
    on = jnp.concatenate(parts, axis=1)
    r = r_ref[rows, :].astype(F32)
    gated = (on * _silu(r)).astype(BF16)
    h = h_ref[rows, :] + mod_ref[2:3, :] * jnp.dot(gated, wo_ref[...], preferred_element_type=F32)
    h_out[rows, :] = h
    f = _adaln(h, g2_ref[...], mod_ref[3:4, :], mod_ref[4:5, :])
    f_out[rows, :] = _pack_halves(f)
    fh = f.astype(BF16)
    fl = (f - fh.astype(F32)).astype(BF16)
    logits = (jnp.dot(fh, rw_ref[0], preferred_element_type=F32)
              + jnp.dot(fl, rw_ref[0], preferred_element_type=F32)
              + jnp.dot(fh, rw_ref[1], preferred_element_type=F32))
    lane = lax.broadcasted_iota(I32, (tm, LANES), 1)
    lane_f = lane.astype(F32)
    logits = jnp.where(lane < N_EXPERTS, logits, -jnp.inf)
    m1 = jnp.max(logits, axis=-1, keepdims=True)
    i1 = jnp.min(jnp.where(logits == m1, lane_f, float(LANES)), axis=-1, keepdims=True)
    rest = jnp.where(lane_f == i1, -jnp.inf, logits)
    m2 = jnp.max(rest, axis=-1, keepdims=True)
    i2 = jnp.min(jnp.where(rest == m2, lane_f, float(LANES)), axis=-1, keepdims=True)
    e2 = jnp.exp(m2 - m1)
    w1 = 1.0 / (1.0 + e2)
    w2 = e2 / (1.0 + e2)
    wt_out[rows, :] = jnp.where(lane == 0, w1, jnp.where(lane == 1, w2, 0.0))
    sel1 = lane_f == i1
    sel2 = lane_f == i2
    picked = jnp.where(sel1, 1.0, jnp.where(sel2, 1.0, 0.0))
    before = jnp.dot(tri_ref[...], picked.astype(BF16), preferred_element_type=F32) + run[0:1, :]
    rank1 = jnp.sum(jnp.where(sel1, before, 0.0), axis=-1, keepdims=True)
    rank2 = jnp.sum(jnp.where(sel2, before, 0.0), axis=-1, keepdims=True)
    run[...] = run[...] + jnp.sum(picked, axis=0, keepdims=True)
    routing = jnp.where(lane == 0, i1, jnp.where(lane == 1, i2,
                        jnp.where(lane == 2, rank1, jnp.where(lane == 3, rank2, 0.0))))
    route_out[:, rows] = routing.T[0:8, :]


def _gla_out(of_p, ob_p, of_s, ob_s, r, h, n_p, t_len, mod, out_norm, wo, g2, router_w):
    n, d = h.shape
    tm = _row_tile(n_p, t_len, cap=1024)
    na = n_p // tm
    nt_s = t_len // tm
    mod_idx = lambda i: (jnp.where(i < na, 0, 1 + (i - na) // nt_s), 0, 0)
    row = lambda i: (i, 0)
    ctx_row = lambda i: (jnp.minimum(i, na - 1), 0)
    lat_row = lambda i: (jnp.maximum(i - na, 0), 0)
    tp = tm // GLA_OUT_PARTS
    tri = jnp.tril(jnp.ones((tp, tp), F32), -1).astype(BF16)
    return pl.pallas_call(
        functools.partial(_gla_out_kernel, na),
        grid=(n // tm,),
        in_specs=[
            pl.BlockSpec((tm, d), ctx_row),
            pl.BlockSpec((tm, d), ctx_row),
            pl.BlockSpec((tm, d), lat_row),
            pl.BlockSpec((tm, d), lat_row),
            pl.BlockSpec((tm, d), row),
            pl.BlockSpec((tm, d), row),
            pl.BlockSpec((None, 8, d), mod_idx),
            _const_spec((1, GLA_DV)),
            _const_spec((d, d)),
            _const_spec((1, d)),
            _const_spec((2, d, LANES)),
            _const_spec((tp, tp)),
        ],
        out_specs=[pl.BlockSpec((tm, d), row), pl.BlockSpec((tm, d // 2), row),
                   pl.BlockSpec((8, tm), lambda i: (0, i)), pl.BlockSpec((tm, LANES), row),
                   _const_spec((8, LANES))],
        out_shape=[jax.ShapeDtypeStruct((n, d), F32), jax.ShapeDtypeStruct((n, d // 2), jnp.uint32),
                   jax.ShapeDtypeStruct((8, n), F32), jax.ShapeDtypeStruct((n, LANES), F32),
                   jax.ShapeDtypeStruct((8, LANES), F32)],
        scratch_shapes=[pltpu.VMEM((8, LANES), F32)],
        compiler_params=_params(("arbitrary",)),
        name="gla_out_router",
    )(of_p, ob_p, of_s, ob_s, r, h, mod, out_norm, wo, g2, router_w, tri)


SC_CORES = 2
SC_SUBCORES = 16
SC_CHUNK = 64


def _sc_gather_rows(table, idx):
    nw = SC_CORES * SC_SUBCORES
    b = idx.shape[0]
    d = table.shape[1]
    assert b % (nw * SC_CHUNK) == 0
    per_w = b // nw
    n_chunks = per_w // SC_CHUNK
    mesh = plsc.VectorSubcoreMesh(core_axis_name="c", subcore_axis_name="s",
                                  num_cores=SC_CORES, num_subcores=SC_SUBCORES)

    assert n_chunks % 2 == 0

    def body(table_hbm, idx_hbm, out_hbm, idx_v, rows_v, gsem, wsem):
        wid = lax.axis_index("s") * SC_CORES + lax.axis_index("c")
        base = wid * per_w
        pltpu.sync_copy(idx_hbm.at[wid], idx_v)

        def gather(j, slot):
            return pltpu.make_async_copy(table_hbm.at[idx_v.at[j]], rows_v.at[slot], gsem.at[slot])

        def write(j, slot):
            return pltpu.make_async_copy(rows_v.at[slot], out_hbm.at[pl.ds(base + j * SC_CHUNK, SC_CHUNK)],
                                         wsem.at[slot])

        gather(0, 0).start()

        @pl.loop(0, n_chunks, step=2)
        def _(j):
            for slot in range(2):
                jj = j + slot
                gather(jj, slot).wait()

                @pl.when(jj + 1 < n_chunks)
                def _():
                    @pl.when(jj >= 1)
                    def _():
                        write(jj - 1, 1 - slot).wait()

                    gather(jj + 1, 1 - slot).start()

                write(jj, slot).start()

        write(n_chunks - 2, 0).wait()
        write(n_chunks - 1, 1).wait()

    return pl.kernel(
        body,
        out_type=jax.ShapeDtypeStruct((b, d), table.dtype),
        mesh=mesh,
        scratch_types=[pltpu.VMEM((n_chunks, SC_CHUNK), I32),
                       pltpu.VMEM((2, SC_CHUNK, d), table.dtype),
                       pltpu.SemaphoreType.DMA((2,)),
                       pltpu.SemaphoreType.DMA((2,))],
        name="sc_gather_rows",
    )(table, idx.reshape(nw, n_chunks, SC_CHUNK))


def _sc_scatter_rows(rows, idx2, p):
    nw = SC_CORES * SC_SUBCORES
    n, d = rows.shape
    assert n % (nw * SC_CHUNK) == 0
    per_w = n // nw
    n_chunks = per_w // SC_CHUNK
    mesh = plsc.VectorSubcoreMesh(core_axis_name="c", subcore_axis_name="s",
                                  num_cores=SC_CORES, num_subcores=SC_SUBCORES)

    assert n_chunks % 2 == 0

    def body(rows_hbm, idx0_hbm, idx1_hbm, out_hbm, idx0_v, idx1_v, rows_v, rsem, ssem):
        wid = lax.axis_index("s") * SC_CORES + lax.axis_index("c")
        base = wid * per_w
        pltpu.sync_copy(idx0_hbm.at[wid], idx0_v)
        pltpu.sync_copy(idx1_hbm.at[wid], idx1_v)

        def read(j, slot):
            return pltpu.make_async_copy(rows_hbm.at[pl.ds(base + j * SC_CHUNK, SC_CHUNK)], rows_v.at[slot],
                                         rsem.at[slot])

        read(0, 0).start()

        @pl.loop(0, n_chunks, step=2)
        def _(j):
            for slot in range(2):
                jj = j + slot
                read(jj, slot).wait()

                @pl.when(jj + 1 < n_chunks)
                def _():
                    read(jj + 1, 1 - slot).start()

                s0 = pltpu.make_async_copy(rows_v.at[slot], out_hbm.at[idx0_v.at[jj]], ssem.at[0])
                s1 = pltpu.make_async_copy(rows_v.at[slot], out_hbm.at[idx1_v.at[jj]], ssem.at[1])
                s0.start()
                s1.start()
                s0.wait()
                s1.wait()

    return pl.kernel(
        body,
        out_type=jax.ShapeDtypeStruct((p, d), rows.dtype),
        mesh=mesh,
        scratch_types=[pltpu.VMEM((n_chunks, SC_CHUNK), I32),
                       pltpu.VMEM((n_chunks, SC_CHUNK), I32),
                       pltpu.VMEM((2, SC_CHUNK, d), rows.dtype),
                       pltpu.SemaphoreType.DMA((2,)),
                       pltpu.SemaphoreType.DMA((2,))],
        name="sc_scatter_rows",
    )(rows, idx2[0].reshape(nw, n_chunks, SC_CHUNK), idx2[1].reshape(nw, n_chunks, SC_CHUNK))


EXPERT_TILE = 1280
EXPERT_SUB = 256
EXPERT_FF = 512
EXPERT_VMEM = 60 * 1024 * 1024


def _pack_halves(x):
    k = x.shape[1] // 2
    lo = lax.bitcast_convert_type(x[:, :k].astype(BF16).astype(F32), jnp.uint32)
    hi = lax.bitcast_convert_type(x[:, k:].astype(BF16).astype(F32), jnp.uint32)
    return (lo >> 16) | (hi & jnp.uint32(0xFFFF0000))


def _unpack_halves(w):
    lo = lax.bitcast_convert_type(w << 16, F32).astype(BF16)
    hi = lax.bitcast_convert_type(w & jnp.uint32(0xFFFF0000), F32).astype(BF16)
    return lo, hi


def _expert_kernel(nf, te_ref, ts_ref, nu_ref, nv_ref, x_ref, wg_ref, wu_ref, wd_ref, y_ref, acc, xb):
    i = pl.program_id(0)
    f = pl.program_id(1)
    tm = x_ref.shape[0]
    half = x_ref.shape[1]
    sub = EXPERT_SUB
    nvalid = nv_ref[i]

    def compute(nrows, first, last):
        rows = slice(0, nrows)
        if first:
            rid = lax.broadcasted_iota(I32, (nrows, half), 0)
            lo, hi = _unpack_halves(x_ref[rows, :])
            zero = jnp.zeros_like(lo)
            xb[rows, 0:half] = jnp.where(rid < nvalid, lo, zero)
            xb[rows, half:2 * half] = jnp.where(rid < nvalid, hi, zero)
        x = xb[rows, :]
        hg = jnp.dot(x, wg_ref[...].astype(BF16), preferred_element_type=F32)
        hu = jnp.dot(x, wu_ref[...].astype(BF16), preferred_element_type=F32)
        act = (_silu(hg) * hu).astype(BF16)
        part = jnp.dot(act, wd_ref[...].astype(BF16), preferred_element_type=F32)
        total = part if first else acc[rows, :] + part
        if last:
            y_ref[rows, :] = _pack_halves(total)
        else:
            acc[rows, :] = total

    assert nf >= 2
    for k in range(1, tm // sub + 1):
        has_k = (nvalid > (k - 1) * sub) & (nvalid <= k * sub)

        @pl.when(has_k & (f == 0))
        def _():
            compute(k * sub, True, False)

        @pl.when(has_k & (f > 0) & (f < nf - 1))
        def _():
            compute(k * sub, False, False)

        @pl.when(has_k & (f == nf - 1))
        def _():
            compute(k * sub, False, True)
            if k * sub < tm:
                y_ref[k * sub:tm, :] = jnp.zeros((tm - k * sub, half), y_ref.dtype)

    @pl.when((nvalid == 0) & (f == nf - 1))
    def _():
        y_ref[...] = jnp.zeros_like(y_ref)


def _experts(x, tile_expert, tile_src, n_used, n_valid, wg, wu, wd, tm, tf):
    p, half = x.shape
    d = 2 * half
    ne, _, dff = wg.shape
    nf = dff // tf
    assert dff % tf == 0 and tm % EXPERT_SUB == 0

    def fidx(i, f, nu):
        return jnp.where(i < nu[0], f, nf - 1)

    grid_spec = pltpu.PrefetchScalarGridSpec(
        num_scalar_prefetch=4,
        grid=(p // tm, nf),
        in_specs=[
            pl.BlockSpec((tm, half), lambda i, f, te, ts, nu, nv: (ts[i], 0)),
            pl.BlockSpec((None, d, tf), lambda i, f, te, ts, nu, nv: (te[i], 0, fidx(i, f, nu))),
            pl.BlockSpec((None, d, tf), lambda i, f, te, ts, nu, nv: (te[i], 0, fidx(i, f, nu))),
            pl.BlockSpec((None, tf, d), lambda i, f, te, ts, nu, nv: (te[i], fidx(i, f, nu), 0)),
        ],
        out_specs=pl.BlockSpec((tm, half), lambda i, f, te, ts, nu, nv: (i, 0)),
        scratch_shapes=[pltpu.VMEM((tm, d), F32), pltpu.VMEM((tm, d), BF16)],
    )
    return pl.pallas_call(
        functools.partial(_expert_kernel, nf),
        grid_spec=grid_spec,
        out_shape=jax.ShapeDtypeStruct((p, half), jnp.uint32),
        compiler_params=_params(("arbitrary", "arbitrary"), EXPERT_VMEM),
        name="moe_experts",
    )(tile_expert, tile_src, n_used, n_valid, x, wg, wu, wd)


def _combine_kernel(wt_ref, h_ref, mod_ref, y0_ref, y1_ref, o_ref):
    wt = wt_ref[...]
    half = y0_ref.shape[1]
    lo0, hi0 = _unpack_halves(y0_ref[...])
    lo1, hi1 = _unpack_halves(y1_ref[...])
    w0, w1 = wt[:, 0:1], wt[:, 1:2]
    h = h_ref[...]
    gate = mod_ref[5:6, :]
    o_ref[:, 0:half] = h[:, 0:half] + gate[:, 0:half] * (w0 * lo0.astype(F32) + w1 * lo1.astype(F32))
    o_ref[:, half:] = h[:, half:] + gate[:, half:] * (w0 * hi0.astype(F32) + w1 * hi1.astype(F32))


def _combine(yg, wt, h, mod, row_off, n_rows, n_mod, mod_off, tm):
    d = h.shape[1]
    nt = n_rows // tm
    toff = row_off // tm
    per_mod = n_rows // n_mod // tm
    return pl.pallas_call(
        _combine_kernel,
        grid=(nt,),
        in_specs=[
            pl.BlockSpec((tm, LANES), lambda i: (toff + i, 0)),
            pl.BlockSpec((tm, d), lambda i: (toff + i, 0)),
            pl.BlockSpec((None, 8, d), lambda i: (mod_off + i // per_mod, 0, 0)),
            pl.BlockSpec((None, tm, d // 2), lambda i: (0, i, 0)),
            pl.BlockSpec((None, tm, d // 2), lambda i: (1, i, 0)),
        ],
        out_specs=pl.BlockSpec((tm, d), lambda i: (i, 0)),
        out_shape=jax.ShapeDtypeStruct((n_rows, d), F32),
        compiler_params=_params(("arbitrary",)),
        name="moe_combine",
    )(wt, h, mod, yg, yg)


def _route(route, counts, n, tm):
    cnt = counts[0, :N_EXPERTS].astype(I32)
    padded = ((cnt + tm - 1) // tm) * tm
    gend = jnp.cumsum(padded)
    goff = gend - padded
    e = route[0:2].astype(I32)
    rank = route[2:4].astype(I32)
    onehot = e[:, :, None] == jnp.arange(N_EXPERTS, dtype=I32)[None, None, :]
    dest = jnp.sum(jnp.where(onehot, goff[None, None, :], 0), axis=-1) + rank
    p = (-(-2 * n // tm) + N_EXPERTS) * tm
    n_used = gend[-1] // tm
    tiles = jnp.arange(p // tm, dtype=I32)
    tile_src = jnp.minimum(tiles, n_used - 1)
    tile_expert = jnp.minimum(jnp.sum((gend[None, :] <= (tile_src * tm)[:, None]).astype(I32), axis=1),
                              N_EXPERTS - 1)
    used = goff[tile_expert] + cnt[tile_expert]
    n_valid = jnp.where(tiles < n_used, jnp.clip(used - tiles * tm, 0, tm), 0).astype(I32)
    return dest, p, tile_expert, tile_src, n_used.reshape(1).astype(I32), n_valid


def kernel(x_prompt, x_sample, cache_k, cache_v, state_fwd, state_bwd, c, c_ctx, ada_w, ada_b, norm1_g, norm2_g, attn_w_qkv, attn_q_norm, attn_k_norm, attn_sink, attn_w_o, gla_w_in, gla_gate_w1, gla_gate_w2, gla_gate_b, gla_out_norm, gla_w_o, ffn_w_gate, ffn_w_up, ffn_w_down, moe_router, moe_w_gate, moe_w_up, moe_w_down):
    bp, seq, d = x_prompt.shape
    db, t_len, _ = x_sample.shape
    n_p, n_s = bp * seq, db * t_len
    n = n_p + n_s
    xp = x_prompt.reshape(n_p, d)
    xs = x_sample.reshape(n_s, d)

    cond = jnp.concatenate([c_ctx[None, :], c], axis=0)
    assert cond.shape[0] <= 8
    cond_t = jnp.pad(cond, ((0, 8 - cond.shape[0]), (0, 0))).T
    mods = _modulation(cond_t, cond.shape[0], ada_w, ada_b)

    nk = N_KV_HEADS * HEAD_DIM
    qn = jnp.tile(attn_q_norm[0], N_HEADS)[None, :]
    kn = jnp.tile(attn_k_norm[0], N_KV_HEADS)[None, :]
    q, kt, vv, ck_new, cv_new = _qkv(xp, xs, t_len, mods[0], norm1_g[0][None, :],
                                     attn_w_qkv[0].astype(BF16), qn, kn)
    wo0 = attn_w_o[0].astype(BF16)
    sink = attn_sink[0]
    hp = _ctx_attention(sink, q, kt, vv, xp, seq, mods[0], wo0)
    ck = cache_k[:, 0].astype(BF16)
    cv = cache_v[:, 0].astype(BF16)
    ckt = jnp.concatenate([ck, ck], axis=-1).transpose(0, 2, 3, 1)
    cvv = jnp.concatenate([cv, cv], axis=-1).reshape(db, cv.shape[1], N_KV_HEADS * LANES)
    hs = _lat_attention(sink, q, kt, vv, ckt, cvv, xs, n_p, t_len, mods[0], wo0)
    new_k = ck_new.reshape(bp, 1, seq, N_KV_HEADS, HEAD_DIM)
    new_v = cv_new.reshape(bp, 1, seq, N_KV_HEADS, HEAD_DIM)
    h = _ffn(hp, hs, t_len, mods[0], norm2_g[0][None, :], ffn_w_gate[0].astype(BF16),
             ffn_w_up[0].astype(BF16), ffn_w_down[0].astype(BF16))

    hk = GLA_HEADS * GLA_DK
    rank = GLA_GATE_RANK
    w1 = jnp.concatenate([gla_gate_w1[0, 0], gla_gate_w1[0, 1]], axis=1)
    w1 = jnp.pad(w1, ((0, 0), (0, LANES - 2 * rank))).astype(BF16)
    w2 = jnp.zeros((LANES, 2 * hk), F32)
    w2 = w2.at[0:rank, 0:hk].set(gla_gate_w2[0, 0]).at[rank:2 * rank, hk:].set(gla_gate_w2[0, 1]).astype(BF16)
    gate_b = gla_gate_b[0].reshape(1, 2 * hk)
    qk, v, r, bc = _gla_in(h, n_p, t_len, mods[1], norm1_g[1][None, :], gla_w_in[0].astype(BF16),
                           w1, w2, gate_b)
    of_p, ob_p, new_sf, new_sb = _gla_scan(qk, v, bc, None, None, 0, bp, seq)
    of_s, ob_s, _, _ = _gla_scan(qk, v, bc, state_fwd[:, 0], state_bwd[:, 0], n_p // t_len, db, t_len)
    rw = jnp.pad(moe_router[0], ((0, 0), (0, LANES - N_EXPERTS)))
    rw_hi = rw.astype(BF16)
    rw_lo = (rw - rw_hi.astype(F32)).astype(BF16)
    h, f, route, wt, counts = _gla_out(
        of_p.reshape(n_p, d), ob_p.reshape(n_p, d), of_s.reshape(n_s, d), ob_s.reshape(n_s, d), r, h,
        n_p, t_len, mods[1], gla_out_norm[0][None, :], gla_w_o[0].astype(BF16), norm2_g[1][None, :],
        jnp.stack([rw_hi, rw_lo]))

    tm_e = EXPERT_TILE
    dest_k, p_rows, tile_expert, tile_src, n_used, n_valid = _route(route, counts, n, tm_e)
    xg = _sc_scatter_rows(f, dest_k, p_rows)
    xg, new_k = lax.optimization_barrier((xg, new_k))
    y = _experts(xg, tile_expert, tile_src, n_used, n_valid, moe_w_gate[0], moe_w_up[0],
                 moe_w_down[0], tm_e, EXPERT_FF)
    yg_p = _sc_gather_rows(y, dest_k[:, :n_p].reshape(2 * n_p)).reshape(2, n_p, d // 2)
    yg_s = _sc_gather_rows(y, dest_k[:, n_p:].reshape(2 * n_s)).reshape(2, n_s, d // 2)
    tm_c = _row_tile(n_p, t_len)
    y_p = _combine(yg_p, wt, h, mods[1], 0, n_p, 1, 0, tm_c)
    y_s = _combine(yg_s, wt, h, mods[1], n_p, n_s, db, 1, tm_c)

    return (y_p.reshape(bp, seq, d), y_s.reshape(db, t_len, d), new_k, new_v,
            new_sf[:, None], new_sb[:, None])
```

```python
import functools
import math

import numpy as np
import jax
import jax.numpy as jnp
from jax import lax
from jax.experimental import pallas as pl
from jax.experimental.pallas import tpu as pltpu
from jax.experimental.pallas import tpu_sc as plsc

F32 = jnp.float32
BF16 = jnp.bfloat16
I32 = jnp.int32

D_MODEL = 1024
N_HEADS = 16
N_KV_HEADS = 4
HEAD_DIM = 64
GRID_W = 64
WINDOW = 128
ATTN_BLOCK = 128
ROPE_THETA = 10000.0
GLA_HEADS = 4
GLA_DK = 128
GLA_DV = 256
GLA_GATE_RANK = 16
GLA_GATE_TAU = 16.0
GLA_CHUNK = 64
N_EXPERTS = 8
NORM_EPS = 1e-6
NEG_INF = -1e30

LANES = 128
VMEM_LIMIT = 56 * 1024 * 1024


def _params(sem, vmem=VMEM_LIMIT):
    return pltpu.CompilerParams(dimension_semantics=sem, vmem_limit_bytes=vmem)


def _row_tile(*counts, cap=512):
    t = cap
    while any(c % t for c in counts):
        t //= 2
    assert t >= 8
    return t


def _rms(x):
    return x * lax.rsqrt(jnp.mean(x * x, axis=-1, keepdims=True) + NORM_EPS)


def _adaln(x, g, shift, scale):
    return _rms(x) * (g * (1.0 + scale)) + shift


def _silu(x):
    return x * jax.nn.sigmoid(x)


def _const_spec(shape):
    nd = len(shape)
    return pl.BlockSpec(shape, lambda *_: (0,) * nd)


def _mod_kernel(n_cond, ct_ref, w_ref, b_ref, o_ref):
    ct = ct_ref[...]
    s = _silu(ct)
    w = w_ref[...]
    rows = [jnp.sum(w * s[:, r:r + 1], axis=0, keepdims=True) for r in range(n_cond)]
    rows += [jnp.zeros_like(rows[0])] * (8 - n_cond)
    o_ref[...] = jnp.concatenate(rows, axis=0) + b_ref[...]


def _modulation(cond_t, n_cond, ada_w, ada_b):
    depth, d, n6 = ada_w.shape
    tn = 2048
    out = pl.pallas_call(
        functools.partial(_mod_kernel, n_cond),
        grid=(depth, n6 // tn),
        in_specs=[
            pl.BlockSpec((d, 8), lambda l, j: (0, 0)),
            pl.BlockSpec((None, d, tn), lambda l, j: (l, 0, j)),
            pl.BlockSpec((None, 1, tn), lambda l, j: (l, 0, j)),
        ],
        out_specs=pl.BlockSpec((None, 8, tn), lambda l, j: (l, 0, j)),
        out_shape=jax.ShapeDtypeStruct((depth, 8, n6), F32),
        compiler_params=_params(("arbitrary", "arbitrary")),
        name="modulation",
    )(cond_t, ada_w, ada_b.reshape(depth, 1, n6))
    m = out[:, :n_cond].reshape(depth, n_cond, 6, d)
    return jnp.pad(m, ((0, 0), (0, 0), (0, 2), (0, 0)))


LOG2E = math.log2(math.e)


def _dup_half(k2, half):
    lane = lax.broadcasted_iota(I32, k2.shape, 1)
    lo = lane < HEAD_DIM
    r = pltpu.roll(k2, HEAD_DIM, 1)
    return jnp.where(lo, k2, r) if half == 0 else jnp.where(lo, r, k2)


QKV_PARTS = 2


def _qkv_kernel(na, xa_ref, xb_ref, mod_ref, g_ref, w_ref, qn_ref, kn_ref, bd_ref,
                cos_ref, sin_ref, q_out, kt_out, vv_out, ck_out, cv_out):
    i = pl.program_id(0)
    is_ctx = i < na
    tm = xa_ref.shape[0] // QKV_PARTS
    for part in range(QKV_PARTS):
        _qkv_rows(slice(part * tm, (part + 1) * tm), tm, is_ctx, xa_ref, xb_ref, mod_ref, g_ref, w_ref,
                  qn_ref, kn_ref, bd_ref, cos_ref, sin_ref, q_out, kt_out, vv_out, ck_out, cv_out)


def _qkv_rows(rows, tm, is_ctx, xa_ref, xb_ref, mod_ref, g_ref, w_ref, qn_ref, kn_ref, bd_ref,
              cos_ref, sin_ref, q_out, kt_out, vv_out, ck_out, cv_out):
    x = jnp.where(is_ctx, xa_ref[rows, :], xb_ref[rows, :])
    a = _adaln(x, g_ref[...], mod_ref[0:1, :], mod_ref[1:2, :])
    y = jnp.dot(a.astype(BF16), w_ref[...], preferred_element_type=F32)
    cos = jnp.where(is_ctx, 1.0, cos_ref[rows, :])
    sin = jnp.where(is_ctx, 0.0, sin_ref[rows, :])
    lane = lax.broadcasted_iota(I32, (tm, LANES), 1)
    first16 = (lane % 32) < 16

    def norm_rope(z, wt):
        ss = jnp.dot((z * z).astype(BF16), bd_ref[...], preferred_element_type=F32)
        zn = z * lax.rsqrt(ss * (1.0 / HEAD_DIM) + NORM_EPS) * wt
        outs = []
        for c in range(2):
            t = zn[:, c * LANES:(c + 1) * LANES]
            partner = jnp.where(first16, pltpu.roll(t, LANES - 16, 1), pltpu.roll(t, 16, 1))
            outs.append(t * cos + partner * sin)
        return jnp.concatenate(outs, axis=1)

    nq = N_HEADS * HEAD_DIM
    nk = N_KV_HEADS * HEAD_DIM
    scale = HEAD_DIM ** -0.5 * LOG2E
    for s in range(nq // 256):
        sl = slice(s * 256, (s + 1) * 256)
        q_out[rows, sl] = (norm_rope(y[:, sl], qn_ref[:, sl]) * scale).astype(BF16)
    k = norm_rope(y[:, nq:nq + nk], kn_ref[...])
    v = y[:, nq + nk:nq + 2 * nk]
    for g in range(N_KV_HEADS):
        c = g // 2
        kk = _dup_half(k[:, c * LANES:(c + 1) * LANES], g % 2)
        kt_out[g, :, rows] = kk.T.astype(BF16)
        vv_out[rows, g * LANES:(g + 1) * LANES] = _dup_half(v[:, c * LANES:(c + 1) * LANES], g % 2).astype(BF16)

    @pl.when(is_ctx)
    def _():
        ck_out[rows, :] = k
        cv_out[rows, :] = v


def _rope_tables(t_len):
    pos = np.arange(t_len)
    row = (pos // GRID_W).astype(np.float32)[:, None]
    col = (pos % GRID_W).astype(np.float32)[:, None]
    half = HEAD_DIM // 2
    inv = (np.float32(ROPE_THETA) ** (-np.arange(0, half, 2, dtype=np.float32) / np.float32(half)))[None, :]
    ar, ac = row * inv, col * inv
    cos = np.concatenate([np.cos(ar), np.cos(ar), np.cos(ac), np.cos(ac)], axis=1)
    sin = np.concatenate([-np.sin(ar), np.sin(ar), -np.sin(ac), np.sin(ac)], axis=1)
    return (jnp.asarray(np.tile(cos, (1, 2)), dtype=F32), jnp.asarray(np.tile(sin, (1, 2)), dtype=F32))


def _qkv(xp, xs, t_len, mod, g, w, qn, kn):
    n_p, n_s = xp.shape[0], xs.shape[0]
    tm = _row_tile(n_p, t_len, cap=1024)
    na, nb = n_p // tm, n_s // tm
    nt_s = t_len // tm
    n = n_p + n_s
    d = D_MODEL
    nq, nk = N_HEADS * HEAD_DIM, N_KV_HEADS * HEAD_DIM
    cos, sin = _rope_tables(t_len)
    eye4 = jnp.kron(jnp.eye(4, dtype=F32), jnp.ones((HEAD_DIM, HEAD_DIM), F32)).astype(BF16)
    mod_idx = lambda i: (jnp.where(i < na, 0, 1 + (i - na) // nt_s), 0, 0)
    pos_idx = lambda i: (jnp.maximum(i - na, 0) % nt_s, 0)
    return pl.pallas_call(
        functools.partial(_qkv_kernel, na),
        grid=(na + nb,),
        in_specs=[
            pl.BlockSpec((tm, d), lambda i: (jnp.minimum(i, na - 1), 0)),
            pl.BlockSpec((tm, d), lambda i: (jnp.maximum(i - na, 0), 0)),
            pl.BlockSpec((None, 8, d), mod_idx),
            _const_spec((1, d)),
            _const_spec((d, nq + 2 * nk)),
            _const_spec((1, nq)),
            _const_spec((1, nk)),
            _const_spec((256, 256)),
            pl.BlockSpec((tm, LANES), pos_idx),
            pl.BlockSpec((tm, LANES), pos_idx),
        ],
        out_specs=[pl.BlockSpec((tm, nq), lambda i: (i, 0)),
                   pl.BlockSpec((N_KV_HEADS, LANES, tm), lambda i: (0, 0, i)),
                   pl.BlockSpec((tm, N_KV_HEADS * LANES), lambda i: (i, 0)),
                   pl.BlockSpec((tm, nk), lambda i: (jnp.minimum(i, na - 1), 0)),
                   pl.BlockSpec((tm, nk), lambda i: (jnp.minimum(i, na - 1), 0))],
        out_shape=[jax.ShapeDtypeStruct((n, nq), BF16),
                   jax.ShapeDtypeStruct((N_KV_HEADS, LANES, n), BF16),
                   jax.ShapeDtypeStruct((n, N_KV_HEADS * LANES), BF16),
                   jax.ShapeDtypeStruct((n_p, nk), F32),
                   jax.ShapeDtypeStruct((n_p, nk), F32)],
        compiler_params=_params(("arbitrary",)),
        name="qkv",
    )(xp, xs, mod, g, w, qn, kn, eye4, cos, sin)


def _attn_group(q2a, q2b, kt, vv, bias, n_bias, sinks):
    tq = q2a.shape[0]
    nk = kt.shape[1]
    lo = lax.broadcasted_iota(I32, (tq, LANES), 1) < HEAD_DIM
    top = lax.broadcasted_iota(I32, kt.shape, 0) < HEAD_DIM
    zk = jnp.zeros_like(kt)
    kbd = jnp.concatenate([jnp.where(top, kt, zk), jnp.where(top, zk, kt)], axis=1)
    left = lax.broadcasted_iota(I32, vv.shape, 1) < HEAD_DIM
    zv = jnp.zeros_like(vv)
    vbd = jnp.concatenate([jnp.where(left, vv, zv), jnp.where(left, zv, vv)], axis=0)
    s = jnp.dot(jnp.concatenate([q2a, q2b], axis=0), kbd, preferred_element_type=F32)
    p_rows, scales = [], []
    for t in range(2):
        ph, rinv = [], []
        for hf in range(2):
            sh = s[t * tq:(t + 1) * tq, hf * nk:(hf + 1) * nk]
            if bias is not None:
                sh = jnp.concatenate([sh[:, :n_bias] + bias, sh[:, n_bias:]], axis=1)
            sk = sinks[2 * t + hf]
            m = jnp.maximum(jnp.max(sh, axis=-1, keepdims=True), sk)
            p = jnp.exp2(sh - m)
            rinv.append(1.0 / (jnp.sum(p, axis=-1, keepdims=True) + jnp.exp2(sk - m)))
            ph.append(p.astype(BF16))
        p_rows.append(jnp.concatenate(ph, axis=1))
        scales.append(jnp.where(lo, rinv[0], rinv[1]))
    o = jnp.dot(jnp.concatenate(p_rows, axis=0), vbd, preferred_element_type=F32)
    return o[0:tq] * scales[0], o[tq:2 * tq] * scales[1]


def _ctx_attn_kernel(sink_ref, q_ref, kt_ref, vv_ref, x_ref, mod_ref, wo_ref, o_ref, osc):
    for g in range(N_KV_HEADS):
        q2a = q_ref[:, (2 * g) * LANES:(2 * g + 1) * LANES]
        q2b = q_ref[:, (2 * g + 1) * LANES:(2 * g + 2) * LANES]
        sinks = [sink_ref[4 * g + j] * LOG2E for j in range(4)]
        oa, ob = _attn_group(q2a, q2b, kt_ref[g], vv_ref[:, g * LANES:(g + 1) * LANES], None, 0, sinks)
        osc[:, (2 * g) * LANES:(2 * g + 1) * LANES] = oa.astype(BF16)
        osc[:, (2 * g + 1) * LANES:(2 * g + 2) * LANES] = ob.astype(BF16)
    att = jnp.dot(osc[...], wo_ref[...], preferred_element_type=F32)
    o_ref[...] = x_ref[...] + mod_ref[2:3, :] * att


def _ctx_attention(sink, q, kt, vv, xp, seq, mod, wo):
    n_p, d = xp.shape
    nb = n_p // seq
    return pl.pallas_call(
        _ctx_attn_kernel,
        grid=(nb,),
        in_specs=[
            pl.BlockSpec(memory_space=pltpu.SMEM),
            pl.BlockSpec((seq, d), lambda b: (b, 0)),
            pl.BlockSpec((N_KV_HEADS, LANES, seq), lambda b: (0, 0, b)),
            pl.BlockSpec((seq, N_KV_HEADS * LANES), lambda b: (b, 0)),
            pl.BlockSpec((seq, d), lambda b: (b, 0)),
            pl.BlockSpec((None, 8, d), lambda b: (0, 0, 0)),
            _const_spec((d, d)),
        ],
        out_specs=pl.BlockSpec((seq, d), lambda b: (b, 0)),
        out_shape=jax.ShapeDtypeStruct((n_p, d), F32),
        scratch_shapes=[pltpu.VMEM((seq, d), BF16)],
        compiler_params=_params(("arbitrary",)),
        name="ctx_attention",
    )(sink, q, kt, vv, xp, mod, wo)


LAT_QB = 4


def _lat_attn_kernel(t_len, sink_ref, q_ref, ktp_ref, kto_ref, ktn_ref, vvp_ref, vvo_ref, vvn_ref,
                     ckt_ref, cvv_ref, x_ref, mod_ref, wo_ref, o_ref, osc):
    step = pl.program_id(1)
    tq = ATTN_BLOCK
    nloc = 3 * ATTN_BLOCK
    qi = lax.broadcasted_iota(I32, (tq, nloc), 0)
    kj = lax.broadcasted_iota(I32, (tq, nloc), 1)
    in_window = jnp.abs(qi + tq - kj) <= WINDOW
    for u in range(LAT_QB):
        n = step * LAT_QB + u
        kpos = (n - 1) * tq + kj
        bias = jnp.where(in_window & (kpos >= 0) & (kpos < t_len), 0.0, NEG_INF)
        rows = slice(u * tq, (u + 1) * tq)
        for g in range(N_KV_HEADS):
            vs = slice(g * LANES, (g + 1) * LANES)
            kts = [ktp_ref[g]] + [kto_ref[g, :, j * tq:(j + 1) * tq] for j in range(LAT_QB)] + [ktn_ref[g]]
            vvs = [vvp_ref[:, vs]] + [vvo_ref[j * tq:(j + 1) * tq, vs] for j in range(LAT_QB)] + [vvn_ref[:, vs]]
            kt = jnp.concatenate(kts[u:u + 3] + [ckt_ref[g]], axis=1)
            vv = jnp.concatenate(vvs[u:u + 3] + [cvv_ref[:, vs]], axis=0)
            q2a = q_ref[rows, (2 * g) * LANES:(2 * g + 1) * LANES]
            q2b = q_ref[rows, (2 * g + 1) * LANES:(2 * g + 2) * LANES]
            sinks = [sink_ref[4 * g + j] * LOG2E for j in range(4)]
            oa, ob = _attn_group(q2a, q2b, kt, vv, bias, nloc, sinks)
            osc[rows, (2 * g) * LANES:(2 * g + 1) * LANES] = oa.astype(BF16)
            osc[rows, (2 * g + 1) * LANES:(2 * g + 2) * LANES] = ob.astype(BF16)
    att = jnp.dot(osc[...], wo_ref[...], preferred_element_type=F32)
    o_ref[...] = x_ref[...] + mod_ref[2:3, :] * att


def _lat_attention(sink, q, kt, vv, ckt, cvv, xs, n_p, t_len, mod, wo):
    n_s, d = xs.shape
    db = n_s // t_len
    tq = ATTN_BLOCK
    ts = LAT_QB * tq
    nblk = t_len // tq
    nstep = t_len // ts
    assert n_p % ts == 0 and t_len % ts == 0
    off = n_p // tq
    npast = ckt.shape[3]
    g4 = N_KV_HEADS

    def nbr(delta):
        return lambda b, m: off + b * nblk + jnp.clip(m * LAT_QB + delta, 0, nblk - 1)

    own = lambda b, m: n_p // ts + b * nstep + m
    return pl.pallas_call(
        functools.partial(_lat_attn_kernel, t_len),
        grid=(db, nstep),
        in_specs=[
            pl.BlockSpec(memory_space=pltpu.SMEM),
            pl.BlockSpec((ts, d), lambda b, m: (own(b, m), 0)),
            pl.BlockSpec((g4, LANES, tq), lambda b, m: (0, 0, nbr(-1)(b, m))),
            pl.BlockSpec((g4, LANES, ts), lambda b, m: (0, 0, own(b, m))),
            pl.BlockSpec((g4, LANES, tq), lambda b, m: (0, 0, nbr(LAT_QB)(b, m))),
            pl.BlockSpec((tq, g4 * LANES), lambda b, m: (nbr(-1)(b, m), 0)),
            pl.BlockSpec((ts, g4 * LANES), lambda b, m: (own(b, m), 0)),
            pl.BlockSpec((tq, g4 * LANES), lambda b, m: (nbr(LAT_QB)(b, m), 0)),
            pl.BlockSpec((None, g4, LANES, npast), lambda b, m: (b, 0, 0, 0)),
            pl.BlockSpec((None, npast, g4 * LANES), lambda b, m: (b, 0, 0)),
            pl.BlockSpec((ts, d), lambda b, m: (b * nstep + m, 0)),
            pl.BlockSpec((None, 8, d), lambda b, m: (1 + b, 0, 0)),
            _const_spec((d, d)),
        ],
        out_specs=pl.BlockSpec((ts, d), lambda b, m: (b * nstep + m, 0)),
        out_shape=jax.ShapeDtypeStruct((n_s, d), F32),
        scratch_shapes=[pltpu.VMEM((ts, d), BF16)],
        compiler_params=_params(("arbitrary", "arbitrary")),
        name="lat_attention",
    )(sink, q, kt, kt, kt, vv, vv, vv, ckt, cvv, xs, mod, wo)


def _ffn_kernel(na, fc, xa_ref, xb_ref, mod_ref, g_ref, wg_ref, wu_ref, wd_ref, o_ref, acc):
    i = pl.program_id(0)
    x = jnp.where(i < na, xa_ref[...], xb_ref[...])
    a = _adaln(x, g_ref[...], mod_ref[3:4, :], mod_ref[4:5, :]).astype(BF16)
    nf = wg_ref.shape[1] // fc
    for f in range(nf):
        sl = slice(f * fc, (f + 1) * fc)
        hg = jnp.dot(a, wg_ref[:, sl], preferred_element_type=F32)
        hu = jnp.dot(a, wu_ref[:, sl], preferred_element_type=F32)
        act = (_silu(hg) * hu).astype(BF16)
        part = jnp.dot(act, wd_ref[sl, :], preferred_element_type=F32)
        if f == 0:
            acc[...] = part
        else:
            acc[...] += part
    o_ref[...] = x + mod_ref[5:6, :] * acc[...]


def _ffn(hp, hs, t_len, mod, g, wg, wu, wd):
    n_p, n_s = hp.shape[0], hs.shape[0]
    d, dff = wg.shape
    tm = _row_tile(n_p, t_len)
    na, nb = n_p // tm, n_s // tm
    nt_s = t_len // tm
    fc = 256
    assert dff % fc == 0
    mod_idx = lambda i: (jnp.where(i < na, 0, 1 + (i - na) // nt_s), 0, 0)
    return pl.pallas_call(
        functools.partial(_ffn_kernel, na, fc),
        grid=(na + nb,),
        in_specs=[
            pl.BlockSpec((tm, d), lambda i: (jnp.minimum(i, na - 1), 0)),
            pl.BlockSpec((tm, d), lambda i: (jnp.maximum(i - na, 0), 0)),
            pl.BlockSpec((None, 8, d), mod_idx),
            _const_spec((1, d)),
            _const_spec((d, dff)),
            _const_spec((d, dff)),
            _const_spec((dff, d)),
        ],
        out_specs=pl.BlockSpec((tm, d), lambda i: (i, 0)),
        out_shape=jax.ShapeDtypeStruct((n_p + n_s, d), F32),
        scratch_shapes=[pltpu.VMEM((tm, d), F32)],
        compiler_params=_params(("arbitrary",)),
        name="ffn",
    )(hp, hs, mod, g, wg, wu, wd)


def _split2(x):
    hi = x.astype(BF16)
    lo = (x - hi.astype(F32)).astype(BF16)
    return hi, lo


def _gla_in_kernel(x_ref, mod_ref, g_ref, w_ref, w1_ref, w2_ref, gb_ref, tri_ref,
                   qk_out, v_out, r_out, b_out):
    x = x_ref[...]
    tm = x.shape[0]
    a = _adaln(x, g_ref[...], mod_ref[0:1, :], mod_ref[1:2, :]).astype(BF16)
    hk = GLA_HEADS * GLA_DK
    hv = GLA_HEADS * GLA_DV
    q = jnp.dot(a, w_ref[:, 0:hk], preferred_element_type=F32)
    qk_out[:, 0:hk] = (q * (GLA_DK ** -0.5)).astype(BF16)
    qk_out[:, hk:2 * hk] = jnp.dot(a, w_ref[:, hk:2 * hk], preferred_element_type=F32).astype(BF16)
    v_out[...] = jnp.dot(a, w_ref[:, 2 * hk:2 * hk + hv], preferred_element_type=F32).astype(BF16)
    r_out[...] = jnp.dot(a, w_ref[:, 2 * hk + hv:2 * hk + 2 * hv], preferred_element_type=F32).astype(BF16)
    z1 = jnp.dot(a, w1_ref[...], preferred_element_type=F32)
    z = jnp.dot(z1.astype(BF16), w2_ref[...], preferred_element_type=F32) + gb_ref[...]
    gate = (jnp.minimum(z, 0.0) - jnp.log(1.0 + jnp.exp(-jnp.abs(z)))) * (1.0 / GLA_GATE_TAU)
    c = GLA_CHUNK
    tri = tri_ref[...]
    for j in range(tm // c):
        rows = slice(j * c, (j + 1) * c)
        for dr in range(2):
            cols = slice(dr * hk, (dr + 1) * hk)
            hi, lo = _split2(gate[rows, cols])
            b_out[rows, cols] = (jnp.dot(tri[dr], hi, preferred_element_type=F32)
                                 + jnp.dot(tri[dr], lo, preferred_element_type=F32))


def _gla_in(h, n_p, t_len, mod, g, w_in, w1, w2, gate_b):
    n, d = h.shape
    tm = _row_tile(n_p, t_len, cap=1024)
    na = n_p // tm
    nt_s = t_len // tm
    hk, hv = GLA_HEADS * GLA_DK, GLA_HEADS * GLA_DV
    c = GLA_CHUNK
    lower = jnp.tril(jnp.ones((c, c), F32))
    tri = jnp.stack([lower, lower.T]).astype(BF16)
    mod_idx = lambda i: (jnp.where(i < na, 0, 1 + (i - na) // nt_s), 0, 0)
    row = lambda i: (i, 0)
    return pl.pallas_call(
        _gla_in_kernel,
        grid=(n // tm,),
        in_specs=[
            pl.BlockSpec((tm, d), row),
            pl.BlockSpec((None, 8, d), mod_idx),
            _const_spec((1, d)),
            _const_spec(w_in.shape),
            _const_spec(w1.shape),
            _const_spec(w2.shape),
            _const_spec((1, 2 * hk)),
            _const_spec((2, c, c)),
        ],
        out_specs=[pl.BlockSpec((tm, 2 * hk), row), pl.BlockSpec((tm, hv), row),
                   pl.BlockSpec((tm, hv), row), pl.BlockSpec((tm, 2 * hk), row)],
        out_shape=[jax.ShapeDtypeStruct((n, 2 * hk), BF16), jax.ShapeDtypeStruct((n, hv), BF16),
                   jax.ShapeDtypeStruct((n, hv), BF16), jax.ShapeDtypeStruct((n, 2 * hk), F32)],
        compiler_params=_params(("arbitrary",)),
        name="gla_in",
    )(h, mod, g, w_in, w1, w2, gate_b, tri)


def _gla_direction(qk_ref, v_ref, b_ref, st, o_ref, forward, qd_s, att_s, upd_s, dec_s):
    c = GLA_CHUNK
    ncb = qk_ref.shape[0] // c
    hk = GLA_HEADS * GLA_DK
    dk, dv = GLA_DK, GLA_DV
    ri = lax.broadcasted_iota(I32, (c, c), 0)
    ci_ = lax.broadcasted_iota(I32, (c, c), 1)
    keep = (ci_ <= ri) if forward else (ci_ >= ri)

    for ch in range(ncb):
        rows = slice(ch * c, (ch + 1) * c)
        b = b_ref[rows, :]
        if forward:
            b_end, b_mid = b[c - 1:c, :], b[c // 2 - 1:c // 2, :]
        else:
            b_end, b_mid = b[0:1, :], b[c // 2:c // 2 + 1, :]
        qa = qk_ref[rows, 0:hk] * jnp.exp(b - b_mid)
        ka = qk_ref[rows, hk:2 * hk] * jnp.exp(b_mid - b)
        qd_s[rows, :] = (qa * jnp.exp(b_mid)).astype(BF16)
        kd = (ka * jnp.exp(b_end - b_mid)).astype(BF16)
        dec_s[ch:ch + 1, :] = jnp.exp(b_end)
        qab, kab = qa.astype(BF16), ka.astype(BF16)
        for h in range(GLA_HEADS):
            ks = slice(h * dk, (h + 1) * dk)
            vs = slice(h * dv, (h + 1) * dv)
            att = lax.dot_general(qab[:, ks], kab[:, ks], (((1,), (1,)), ((), ())),
                                  preferred_element_type=F32)
            att_s[ch, h] = jnp.where(keep, att, 0.0).astype(BF16)
            upd_s[ch, h] = lax.dot_general(v_ref[rows, vs], kd[:, ks], (((0,), (0,)), ((), ())),
                                           preferred_element_type=F32)

    for ch in (range(ncb) if forward else reversed(range(ncb))):
        rows = slice(ch * c, (ch + 1) * c)
        for h in range(GLA_HEADS):
            ks = slice(h * dk, (h + 1) * dk)
            vs = slice(h * dv, (h + 1) * dv)
            s = st[h]
            o = lax.dot_general(qd_s[rows, ks], s.astype(BF16), (((1,), (1,)), ((), ())),
                                preferred_element_type=F32)
            o = o + jnp.dot(att_s[ch, h], v_ref[rows, vs], preferred_element_type=F32)
            o_ref[rows, vs] = o.astype(o_ref.dtype)
            st[h] = s * dec_s[ch:ch + 1, ks] + upd_s[ch, h]


def _gla_scan_kernel(shared, from_zero, *refs):
    if shared:
        qkf_ref, vf_ref, bf_ref, bb_ref = refs[:4]
        qkb_ref, vb_ref = qkf_ref, vf_ref
        rest = refs[4:]
    else:
        qkf_ref, vf_ref, bf_ref, qkb_ref, vb_ref, bb_ref = refs[:6]
        rest = refs[6:]
    if not from_zero:
        sf0_ref, sb0_ref = rest[:2]
        rest = rest[2:]
    of_ref, ob_ref, sf_ref, sb_ref, stf, stb, qd_s, att_s, upd_s, dec_s = rest
    i = pl.program_id(1)
    nblk = pl.num_programs(1)

    @pl.when(i == 0)
    def _():
        if from_zero:
            stf[...] = jnp.zeros_like(stf)
            stb[...] = jnp.zeros_like(stb)
        else:
            for h in range(GLA_HEADS):
                stf[h] = sf0_ref[h].T
                stb[h] = sb0_ref[h].T

    _gla_direction(qkf_ref, vf_ref, bf_ref, stf, of_ref, True, qd_s, att_s, upd_s, dec_s)
    _gla_direction(qkb_ref, vb_ref, bb_ref, stb, ob_ref, False, qd_s, att_s, upd_s, dec_s)

    @pl.when(i == nblk - 1)
    def _():
        for h in range(GLA_HEADS):
            sf_ref[h] = stf[h].T
            sb_ref[h] = stb[h].T


def _gla_scan(qk, v, bc, sf0, sb0, b_off, nb, t_len):
    n = qk.shape[0]
    assert n % t_len == 0
    ns = n // t_len
    tb = _row_tile(t_len)
    nblk = t_len // tb
    qk3 = qk.reshape(ns, t_len, qk.shape[1])
    v3 = v.reshape(ns, t_len, v.shape[1])
    bc3 = bc.reshape(ns, t_len, bc.shape[1])
    h, dk, dv = GLA_HEADS, GLA_DK, GLA_DV
    hk = h * dk
    st_spec = pl.BlockSpec((None, h, dk, dv), lambda b, i: (b, 0, 0, 0))
    fwd = lambda b, i: (b_off + b, i, 0)
    bwd = lambda b, i: (b_off + b, nblk - 1 - i, 0)
    bwd_b = pl.BlockSpec((None, tb, hk), lambda b, i: (b_off + b, nblk - 1 - i, 1))
    shared = nblk == 1
    in_specs = [pl.BlockSpec((None, tb, 2 * hk), fwd), pl.BlockSpec((None, tb, h * dv), fwd),
                pl.BlockSpec((None, tb, hk), fwd)]
    operands = [qk3, v3, bc3]
    if not shared:
        in_specs += [pl.BlockSpec((None, tb, 2 * hk), bwd), pl.BlockSpec((None, tb, h * dv), bwd)]
        operands += [qk3, v3]
    in_specs += [bwd_b]
    operands += [bc3]
    from_zero = sf0 is None
    if not from_zero:
        in_specs += [st_spec, st_spec]
        operands += [sf0, sb0]
    return pl.pallas_call(
        functools.partial(_gla_scan_kernel, shared, from_zero),
        grid=(nb, nblk),
        in_specs=in_specs,
        out_specs=[pl.BlockSpec((None, tb, h * dv), lambda b, i: (b, i, 0)),
                   pl.BlockSpec((None, tb, h * dv), lambda b, i: (b, nblk - 1 - i, 0)),
                   st_spec, st_spec],
        out_shape=[jax.ShapeDtypeStruct((nb, t_len, h * dv), BF16),
                   jax.ShapeDtypeStruct((nb, t_len, h * dv), BF16),
                   jax.ShapeDtypeStruct((nb, h, dk, dv), F32),
                   jax.ShapeDtypeStruct((nb, h, dk, dv), F32)],
        scratch_shapes=[pltpu.VMEM((h, dv, dk), F32), pltpu.VMEM((h, dv, dk), F32),
                        pltpu.VMEM((tb, hk), BF16),
                        pltpu.VMEM((tb // GLA_CHUNK, h, GLA_CHUNK, GLA_CHUNK), BF16),
                        pltpu.VMEM((tb // GLA_CHUNK, h, dv, dk), F32),
                        pltpu.VMEM((max(tb // GLA_CHUNK, 8), hk), F32)],
        compiler_params=_params(("arbitrary", "arbitrary")),
        name="gla_scan",
    )(*operands)


GLA_OUT_PARTS = 4


def _gla_out_kernel(na, ofa_ref, oba_ref, ofb_ref, obb_ref, r_ref, h_ref, mod_ref, on_ref, wo_ref,
                    g2_ref, rw_ref, tri_ref, h_out, f_out, route_out, wt_out, cnt_out, run):
    i = pl.program_id(0)
    is_ctx = i < na

    @pl.when(i == 0)
    def _():
        run[...] = jnp.zeros_like(run)

    tm = tri_ref.shape[0]
    for part in range(h_ref.shape[0] // tm):
        rows = slice(part * tm, (part + 1) * tm)
        _gla_out_rows(rows, tm, is_ctx, ofa_ref, oba_ref, ofb_ref, obb_ref, r_ref, h_ref, mod_ref, on_ref,
                      wo_ref, g2_ref, rw_ref, tri_ref, h_out, f_out, route_out, wt_out, run)
    cnt_out[...] = run[...]


def _gla_out_rows(rows, tm, is_ctx, ofa_ref, oba_ref, ofb_ref, obb_ref, r_ref, h_ref, mod_ref, on_ref,
                  wo_ref, g2_ref, rw_ref, tri_ref, h_out, f_out, route_out, wt_out, run):
    o = (jnp.where(is_ctx, ofa_ref[rows, :], ofb_ref[rows, :]).astype(F32)
         + jnp.where(is_ctx, oba_ref[rows, :], obb_ref[rows, :]).astype(F32))
    dv = GLA_DV
    parts = []
    for hh in range(GLA_HEADS):
        oh = o[:, hh * dv:(hh + 1) * dv]
        parts.append(_rms(oh) * on_ref[...])
    on = jnp.concatenate(parts, axis=1)
    r = r_ref[rows, :].astype(F32)
    gated = (on * _silu(r)).astype(BF16)
    h = h_ref[rows, :] + mod_ref[2:3, :] * jnp.dot(gated, wo_ref[...], preferred_element_type=F32)
    h_out[rows, :] = h
    f = _adaln(h, g2_ref[...], mod_ref[3:4, :], mod_ref[4:5, :])
    f_out[rows, :] = _pack_halves(f)
    fh = f.astype(BF16)
    fl = (f - fh.astype(F32)).astype(BF16)
    logits = (jnp.dot(fh, rw_ref[0], preferred_element_type=F32)
              + jnp.dot(fl, rw_ref[0], preferred_element_type=F32)
              + jnp.dot(fh, rw_ref[1], preferred_element_type=F32))
    lane = lax.broadcasted_iota(I32, (tm, LANES), 1)
    lane_f = lane.astype(F32)
    logits = jnp.where(lane < N_EXPERTS, logits, -jnp.inf)
    m1 = jnp.max(logits, axis=-1, keepdims=True)
    i1 = jnp.min(jnp.where(logits == m1, lane_f, float(LANES)), axis=-1, keepdims=True)
    rest = jnp.where(lane_f == i1, -jnp.inf, logits)
    m2 = jnp.max(rest, axis=-1, keepdims=True)
    i2 = jnp.min(jnp.where(rest == m2, lane_f, float(LANES)), axis=-1, keepdims=True)
    e2 = jnp.exp(m2 - m1)
    w1 = 1.0 / (1.0 + e2)
    w2 = e2 / (1.0 + e2)
    wt_out[rows, :] = jnp.where(lane == 0, w1, jnp.where(lane == 1, w2, 0.0))
    sel1 = lane_f == i1
    sel2 = lane_f == i2
    picked = jnp.where(sel1, 1.0, jnp.where(sel2, 1.0, 0.0))
    before = jnp.dot(tri_ref[...], picked.astype(BF16), preferred_element_type=F32) + run[0:1, :]
    rank1 = jnp.sum(jnp.where(sel1, before, 0.0), axis=-1, keepdims=True)
    rank2 = jnp.sum(jnp.where(sel2, before, 0.0), axis=-1, keepdims=True)
    run[...] = run[...] + jnp.sum(picked, axis=0, keepdims=True)
    routing = jnp.where(lane == 0, i1, jnp.where(lane == 1, i2,
                        jnp.where(lane == 2, rank1, jnp.where(lane == 3, rank2, 0.0))))
    route_out[:, rows] = routing.T[0:8, :]


def _gla_out(of_p, ob_p, of_s, ob_s, r, h, n_p, t_len, mod, out_norm, wo, g2, router_w):
    n, d = h.shape
    tm = _row_tile(n_p, t_len, cap=1024)
    na = n_p // tm
    nt_s = t_len // tm
    mod_idx = lambda i: (jnp.where(i < na, 0, 1 + (i - na) // nt_s), 0, 0)
    row = lambda i: (i, 0)
    ctx_row = lambda i: (jnp.minimum(i, na - 1), 0)
    lat_row = lambda i: (jnp.maximum(i - na, 0), 0)
    tp = tm // GLA_OUT_PARTS
    tri = jnp.tril(jnp.ones((tp, tp), F32), -1).astype(BF16)
    return pl.pallas_call(
        functools.partial(_gla_out_kernel, na),
        grid=(n // tm,),
        in_specs=[
            pl.BlockSpec((tm, d), ctx_row),
            pl.BlockSpec((tm, d), ctx_row),
            pl.BlockSpec((tm, d), lat_row),
            pl.BlockSpec((tm, d), lat_row),
            pl.BlockSpec((tm, d), row),
            pl.BlockSpec((tm, d), row),
            pl.BlockSpec((None, 8, d), mod_idx),
            _const_spec((1, GLA_DV)),
            _const_spec((d, d)),
            _const_spec((1, d)),
            _const_spec((2, d, LANES)),
            _const_spec((tp, tp)),
        ],
        out_specs=[pl.BlockSpec((tm, d), row), pl.BlockSpec((tm, d // 2), row),
                   pl.BlockSpec((8, tm), lambda i: (0, i)), pl.BlockSpec((tm, LANES), row),
                   _const_spec((8, LANES))],
        out_shape=[jax.ShapeDtypeStruct((n, d), F32), jax.ShapeDtypeStruct((n, d // 2), jnp.uint32),
                   jax.ShapeDtypeStruct((8, n), F32), jax.ShapeDtypeStruct((n, LANES), F32),
                   jax.ShapeDtypeStruct((8, LANES), F32)],
        scratch_shapes=[pltpu.VMEM((8, LANES), F32)],
        compiler_params=_params(("arbitrary",)),
        name="gla_out_router",
    )(of_p, ob_p, of_s, ob_s, r, h, mod, out_norm, wo, g2, router_w, tri)


SC_CORES = 2
SC_SUBCORES = 16
SC_CHUNK = 64


def _sc_gather_rows(table, idx):
    nw = SC_CORES * SC_SUBCORES
    b = idx.shape[0]
    d = table.shape[1]
    assert b % (nw * SC_CHUNK) == 0
    per_w = b // nw
    n_chunks = per_w // SC_CHUNK
    mesh = plsc.VectorSubcoreMesh(core_axis_name="c", subcore_axis_name="s",
                                  num_cores=SC_CORES, num_subcores=SC_SUBCORES)

    assert n_chunks % 2 == 0

    def body(table_hbm, idx_hbm, out_hbm, idx_v, rows_v, gsem, wsem):
        wid = lax.axis_index("s") * SC_CORES + lax.axis_index("c")
        base = wid * per_w
        pltpu.sync_copy(idx_hbm.at[wid], idx_v)

        def gather(j, slot):
            return pltpu.make_async_copy(table_hbm.at[idx_v.at[j]], rows_v.at[slot], gsem.at[slot])

        def write(j, slot):
            return pltpu.make_async_copy(rows_v.at[slot], out_hbm.at[pl.ds(base + j * SC_CHUNK, SC_CHUNK)],
                                         wsem.at[slot])

        gather(0, 0).start()

        @pl.loop(0, n_chunks, step=2)
        def _(j):
            for slot in range(2):
                jj = j + slot
                gather(jj, slot).wait()

                @pl.when(jj + 1 < n_chunks)
                def _():
                    @pl.when(jj >= 1)
                    def _():
                        write(jj - 1, 1 - slot).wait()

                    gather(jj + 1, 1 - slot).start()

                write(jj, slot).start()

        write(n_chunks - 2, 0).wait()
        write(n_chunks - 1, 1).wait()

    return pl.kernel(
        body,
        out_type=jax.ShapeDtypeStruct((b, d), table.dtype),
        mesh=mesh,
        scratch_types=[pltpu.VMEM((n_chunks, SC_CHUNK), I32),
                       pltpu.VMEM((2, SC_CHUNK, d), table.dtype),
                       pltpu.SemaphoreType.DMA((2,)),
                       pltpu.SemaphoreType.DMA((2,))],
        name="sc_gather_rows",
    )(table, idx.reshape(nw, n_chunks, SC_CHUNK))


def _sc_scatter_rows(rows, idx2, p):
    nw = SC_CORES * SC_SUBCORES
    n, d = rows.shape
    assert n % (nw * SC_CHUNK) == 0
    per_w = n // nw
    n_chunks = per_w // SC_CHUNK
    mesh = plsc.VectorSubcoreMesh(core_axis_name="c", subcore_axis_name="s",
                                  num_cores=SC_CORES, num_subcores=SC_SUBCORES)

    assert n_chunks % 2 == 0

    def body(rows_hbm, idx0_hbm, idx1_hbm, out_hbm, idx0_v, idx1_v, rows_v, rsem, ssem):
        wid = lax.axis_index("s") * SC_CORES + lax.axis_index("c")
        base = wid * per_w
        pltpu.sync_copy(idx0_hbm.at[wid], idx0_v)
        pltpu.sync_copy(idx1_hbm.at[wid], idx1_v)

        def read(j, slot):
            return pltpu.make_async_copy(rows_hbm.at[pl.ds(base + j * SC_CHUNK, SC_CHUNK)], rows_v.at[slot],
                                         rsem.at[slot])

        read(0, 0).start()

        @pl.loop(0, n_chunks, step=2)
        def _(j):
            for slot in range(2):
                jj = j + slot
                read(jj, slot).wait()

                @pl.when(jj + 1 < n_chunks)
                def _():
                    read(jj + 1, 1 - slot).start()

                s0 = pltpu.make_async_copy(rows_v.at[slot], out_hbm.at[idx0_v.at[jj]], ssem.at[0])
                s1 = pltpu.make_async_copy(rows_v.at[slot], out_hbm.at[idx1_v.at[jj]], ssem.at[1])
                s0.start()
                s1.start()
                s0.wait()
                s1.wait()

    return pl.kernel(
        body,
        out_type=jax.ShapeDtypeStruct((p, d), rows.dtype),
        mesh=mesh,
        scratch_types=[pltpu.VMEM((n_chunks, SC_CHUNK), I32),
                       pltpu.VMEM((n_chunks, SC_CHUNK), I32),
                       pltpu.VMEM((2, SC_CHUNK, d), rows.dtype),
                       pltpu.SemaphoreType.DMA((2,)),
                       pltpu.SemaphoreType.DMA((2,))],
        name="sc_scatter_rows",
    )(rows, idx2[0].reshape(nw, n_chunks, SC_CHUNK), idx2[1].reshape(nw, n_chunks, SC_CHUNK))


EXPERT_TILE = 1792
EXPERT_SUB = 256
EXPERT_FF = 512
EXPERT_VMEM = 60 * 1024 * 1024


def _pack_halves(x):
    k = x.shape[1] // 2
    lo = lax.bitcast_convert_type(x[:, :k].astype(BF16).astype(F32), jnp.uint32)
    hi = lax.bitcast_convert_type(x[:, k:].astype(BF16).astype(F32), jnp.uint32)
    return (lo >> 16) | (hi & jnp.uint32(0xFFFF0000))


def _unpack_halves(w):
    lo = lax.bitcast_convert_type(w << 16, F32).astype(BF16)
    hi = lax.bitcast_convert_type(w & jnp.uint32(0xFFFF0000), F32).astype(BF16)
    return lo, hi


def _expert_kernel(nf, te_ref, ts_ref, nu_ref, nv_ref, x_ref, wg_ref, wu_ref, wd_ref, y_ref, acc, xb):
    i = pl.program_id(0)
    f = pl.program_id(1)
    tm = x_ref.shape[0]
    half = x_ref.shape[1]
    sub = EXPERT_SUB
    nvalid = nv_ref[i]

    def compute(nrows, first, last):
        rows = slice(0, nrows)
        if first:
            rid = lax.broadcasted_iota(I32, (nrows, half), 0)
            lo, hi = _unpack_halves(x_ref[rows, :])
            zero = jnp.zeros_like(lo)
            xb[rows, 0:half] = jnp.where(rid < nvalid, lo, zero)
            xb[rows, half:2 * half] = jnp.where(rid < nvalid, hi, zero)
        x = xb[rows, :]
        hg = jnp.dot(x, wg_ref[...].astype(BF16), preferred_element_type=F32)
        hu = jnp.dot(x, wu_ref[...].astype(BF16), preferred_element_type=F32)
        act = (_silu(hg) * hu).astype(BF16)
        part = jnp.dot(act, wd_ref[...].astype(BF16), preferred_element_type=F32)
        total = part if first else acc[rows, :] + part
        if last:
            y_ref[rows, :] = _pack_halves(total)
        else:
            acc[rows, :] = total

    assert nf >= 2
    for k in range(1, tm // sub + 1):
        has_k = (nvalid > (k - 1) * sub) & (nvalid <= k * sub)

        @pl.when(has_k & (f == 0))
        def _():
            compute(k * sub, True, False)

        @pl.when(has_k & (f > 0) & (f < nf - 1))
        def _():
            compute(k * sub, False, False)

        @pl.when(has_k & (f == nf - 1))
        def _():
            compute(k * sub, False, True)
            if k * sub < tm:
                y_ref[k * sub:tm, :] = jnp.zeros((tm - k * sub, half), y_ref.dtype)

    @pl.when((nvalid == 0) & (f == nf - 1))
    def _():
        y_ref[...] = jnp.zeros_like(y_ref)


def _experts(x, tile_expert, tile_src, n_used, n_valid, wg, wu, wd, tm, tf):
    p, half = x.shape
    d = 2 * half
    ne, _, dff = wg.shape
    nf = dff // tf
    assert dff % tf == 0 and tm % EXPERT_SUB == 0

    def fidx(i, f, nu):
        return jnp.where(i < nu[0], f, nf - 1)

    grid_spec = pltpu.PrefetchScalarGridSpec(
        num_scalar_prefetch=4,
        grid=(p // tm, nf),
        in_specs=[
            pl.BlockSpec((tm, half), lambda i, f, te, ts, nu, nv: (ts[i], 0)),
            pl.BlockSpec((None, d, tf), lambda i, f, te, ts, nu, nv: (te[i], 0, fidx(i, f, nu))),
            pl.BlockSpec((None, d, tf), lambda i, f, te, ts, nu, nv: (te[i], 0, fidx(i, f, nu))),
            pl.BlockSpec((None, tf, d), lambda i, f, te, ts, nu, nv: (te[i], fidx(i, f, nu), 0)),
        ],
        out_specs=pl.BlockSpec((tm, half), lambda i, f, te, ts, nu, nv: (i, 0)),
        scratch_shapes=[pltpu.VMEM((tm, d), F32), pltpu.VMEM((tm, d), BF16)],
    )
    return pl.pallas_call(
        functools.partial(_expert_kernel, nf),
        grid_spec=grid_spec,
        out_shape=jax.ShapeDtypeStruct((p, half), jnp.uint32),
        compiler_params=_params(("arbitrary", "arbitrary"), EXPERT_VMEM),
        name="moe_experts",
    )(tile_expert, tile_src, n_used, n_valid, x, wg, wu, wd)


def _combine_kernel(wt_ref, h_ref, mod_ref, y0_ref, y1_ref, o_ref):
    wt = wt_ref[...]
    half = y0_ref.shape[1]
    lo0, hi0 = _unpack_halves(y0_ref[...])
    lo1, hi1 = _unpack_halves(y1_ref[...])
    w0, w1 = wt[:, 0:1], wt[:, 1:2]
    h = h_ref[...]
    gate = mod_ref[5:6, :]
    o_ref[:, 0:half] = h[:, 0:half] + gate[:, 0:half] * (w0 * lo0.astype(F32) + w1 * lo1.astype(F32))
    o_ref[:, half:] = h[:, half:] + gate[:, half:] * (w0 * hi0.astype(F32) + w1 * hi1.astype(F32))


def _combine(yg, wt, h, mod, row_off, n_rows, n_mod, mod_off, tm):
    d = h.shape[1]
    nt = n_rows // tm
    toff = row_off // tm
    per_mod = n_rows // n_mod // tm
    return pl.pallas_call(
        _combine_kernel,
        grid=(nt,),
        in_specs=[
            pl.BlockSpec((tm, LANES), lambda i: (toff + i, 0)),
            pl.BlockSpec((tm, d), lambda i: (toff + i, 0)),
            pl.BlockSpec((None, 8, d), lambda i: (mod_off + i // per_mod, 0, 0)),
            pl.BlockSpec((None, tm, d // 2), lambda i: (0, i, 0)),
            pl.BlockSpec((None, tm, d // 2), lambda i: (1, i, 0)),
        ],
        out_specs=pl.BlockSpec((tm, d), lambda i: (i, 0)),
        out_shape=jax.ShapeDtypeStruct((n_rows, d), F32),
        compiler_params=_params(("arbitrary",)),
        name="moe_combine",
    )(wt, h, mod, yg, yg)


def _route(route, counts, n, tm):
    cnt = counts[0, :N_EXPERTS].astype(I32)
    padded = ((cnt + tm - 1) // tm) * tm
    gend = jnp.cumsum(padded)
    goff = gend - padded
    e = route[0:2].astype(I32)
    rank = route[2:4].astype(I32)
    onehot = e[:, :, None] == jnp.arange(N_EXPERTS, dtype=I32)[None, None, :]
    dest = jnp.sum(jnp.where(onehot, goff[None, None, :], 0), axis=-1) + rank
    p = (-(-2 * n // tm) + N_EXPERTS) * tm
    n_used = gend[-1] // tm
    tiles = jnp.arange(p // tm, dtype=I32)
    tile_src = jnp.minimum(tiles, n_used - 1)
    tile_expert = jnp.minimum(jnp.sum((gend[None, :] <= (tile_src * tm)[:, None]).astype(I32), axis=1),
                              N_EXPERTS - 1)
    used = goff[tile_expert] + cnt[tile_expert]
    n_valid = jnp.where(tiles < n_used, jnp.clip(used - tiles * tm, 0, tm), 0).astype(I32)
    return dest, p, tile_expert, tile_src, n_used.reshape(1).astype(I32), n_valid


def kernel(x_prompt, x_sample, cache_k, cache_v, state_fwd, state_bwd, c, c_ctx, ada_w, ada_b, norm1_g, norm2_g, attn_w_qkv, attn_q_norm, attn_k_norm, attn_sink, attn_w_o, gla_w_in, gla_gate_w1, gla_gate_w2, gla_gate_b, gla_out_norm, gla_w_o, ffn_w_gate, ffn_w_up, ffn_w_down, moe_router, moe_w_gate, moe_w_up, moe_w_down):
    bp, seq, d = x_prompt.shape
    db, t_len, _ = x_sample.shape
    n_p, n_s = bp * seq, db * t_len
    n = n_p + n_s
    xp = x_prompt.reshape(n_p, d)
    xs = x_sample.reshape(n_s, d)

    cond = jnp.concatenate([c_ctx[None, :], c], axis=0)
    assert cond.shape[0] <= 8
    cond_t = jnp.pad(cond, ((0, 8 - cond.shape[0]), (0, 0))).T
    mods = _modulation(cond_t, cond.shape[0], ada_w, ada_b)

    nk = N_KV_HEADS * HEAD_DIM
    qn = jnp.tile(attn_q_norm[0], N_HEADS)[None, :]
    kn = jnp.tile(attn_k_norm[0], N_KV_HEADS)[None, :]
    q, kt, vv, ck_new, cv_new = _qkv(xp, xs, t_len, mods[0], norm1_g[0][None, :],
                                     attn_w_qkv[0].astype(BF16), qn, kn)
    wo0 = attn_w_o[0].astype(BF16)
    sink = attn_sink[0]
    hp = _ctx_attention(sink, q, kt, vv, xp, seq, mods[0], wo0)
    ck = cache_k[:, 0].astype(BF16)
    cv = cache_v[:, 0].astype(BF16)
    ckt = jnp.concatenate([ck, ck], axis=-1).transpose(0, 2, 3, 1)
    cvv = jnp.concatenate([cv, cv], axis=-1).reshape(db, cv.shape[1], N_KV_HEADS * LANES)
    hs = _lat_attention(sink, q, kt, vv, ckt, cvv, xs, n_p, t_len, mods[0], wo0)
    new_k = ck_new.reshape(bp, 1, seq, N_KV_HEADS, HEAD_DIM)
    new_v = cv_new.reshape(bp, 1, seq, N_KV_HEADS, HEAD_DIM)
    h = _ffn(hp, hs, t_len, mods[0], norm2_g[0][None, :], ffn_w_gate[0].astype(BF16),
             ffn_w_up[0].astype(BF16), ffn_w_down[0].astype(BF16))

    hk = GLA_HEADS * GLA_DK
    rank = GLA_GATE_RANK
    w1 = jnp.concatenate([gla_gate_w1[0, 0], gla_gate_w1[0, 1]], axis=1)
    w1 = jnp.pad(w1, ((0, 0), (0, LANES - 2 * rank))).astype(BF16)
    w2 = jnp.zeros((LANES, 2 * hk), F32)
    w2 = w2.at[0:rank, 0:hk].set(gla_gate_w2[0, 0]).at[rank:2 * rank, hk:].set(gla_gate_w2[0, 1]).astype(BF16)
    gate_b = gla_gate_b[0].reshape(1, 2 * hk)
    qk, v, r, bc = _gla_in(h, n_p, t_len, mods[1], norm1_g[1][None, :], gla_w_in[0].astype(BF16),
                           w1, w2, gate_b)
    of_p, ob_p, new_sf, new_sb = _gla_scan(qk, v, bc, None, None, 0, bp, seq)
    of_s, ob_s, _, _ = _gla_scan(qk, v, bc, state_fwd[:, 0], state_bwd[:, 0], n_p // t_len, db, t_len)
    rw = jnp.pad(moe_router[0], ((0, 0), (0, LANES - N_EXPERTS)))
    rw_hi = rw.astype(BF16)
    rw_lo = (rw - rw_hi.astype(F32)).astype(BF16)
    h, f, route, wt, counts = _gla_out(
        of_p.reshape(n_p, d), ob_p.reshape(n_p, d), of_s.reshape(n_s, d), ob_s.reshape(n_s, d), r, h,
        n_p, t_len, mods[1], gla_out_norm[0][None, :], gla_w_o[0].astype(BF16), norm2_g[1][None, :],
        jnp.stack([rw_hi, rw_lo]))

    tm_e = EXPERT_TILE
    dest_k, p_rows, tile_expert, tile_src, n_used, n_valid = _route(route, counts, n, tm_e)
    xg = _sc_scatter_rows(f, dest_k, p_rows)
    xg, new_k = lax.optimization_barrier((xg, new_k))
    y = _experts(xg, tile_expert, tile_src, n_used, n_valid, moe_w_gate[0], moe_w_up[0],
                 moe_w_down[0], tm_e, EXPERT_FF)
    yg_p = _sc_gather_rows(y, dest_k[:, :n_p].reshape(2 * n_p)).reshape(2, n_p, d // 2)
    yg_s = _sc_gather_rows(y, dest_k[:, n_p:].reshape(2 * n_s)).reshape(2, n_s, d // 2)
    tm_c = _row_tile(n_p, t_len)
    y_p = _combine(yg_p, wt, h, mods[1], 0, n_p, 1, 0, tm_c)
    y_s = _combine(yg_s, wt, h, mods[1], n_p, n_s, db, 1, tm_c)

    return (y_p.reshape(bp, seq, d), y_s.reshape(db, t_len, d), new_k, new_v,
            new_sf[:, None], new_sb[:, None])
```

```python
import functools
import math

import numpy as np
import jax
import jax.numpy as jnp
from jax import lax
from jax.experimental import pallas as pl
from jax.experimental.pallas import tpu as pltpu
from jax.experimental.pallas import tpu_sc as plsc

F32 = jnp.float32
BF16 = jnp.bfloat16
I32 = jnp.int32

D_MODEL = 1024
N_HEADS = 16
N_KV_HEADS = 4
HEAD_DIM = 64
GRID_W = 64
WINDOW = 128
ATTN_BLOCK = 128
ROPE_THETA = 10000.0
GLA_HEADS = 4
GLA_DK = 128
GLA_DV = 256
GLA_GATE_RANK = 16
GLA_GATE_TAU = 16.0
GLA_CHUNK = 64
N_EXPERTS = 8
NORM_EPS = 1e-6
NEG_INF = -1e30

LANES = 128
VMEM_LIMIT = 56 * 1024 * 1024


def _params(sem, vmem=VMEM_LIMIT):
    return pltpu.CompilerParams(dimension_semantics=sem, vmem_limit_bytes=vmem)


def _row_tile(*counts, cap=512):
    t = cap
    while any(c % t for c in counts):
        t //= 2
    assert t >= 8
    return t


def _rms(x):
    return x * lax.rsqrt(jnp.mean(x * x, axis=-1, keepdims=True) + NORM_EPS)


def _adaln(x, g, shift, scale):
    return _rms(x) * (g * (1.0 + scale)) + shift


def _silu(x):
    return x * jax.nn.sigmoid(x)


def _const_spec(shape):
    nd = len(shape)
    return pl.BlockSpec(shape, lambda *_: (0,) * nd)


def _mod_kernel(n_cond, ct_ref, w_ref, b_ref, o_ref):
    ct = ct_ref[...]
    s = _silu(ct)
    w = w_ref[...]
    rows = [jnp.sum(w * s[:, r:r + 1], axis=0, keepdims=True) for r in range(n_cond)]
    rows += [jnp.zeros_like(rows[0])] * (8 - n_cond)
    o_ref[...] = jnp.concatenate(rows, axis=0) + b_ref[...]


def _modulation(cond_t, n_cond, ada_w, ada_b):
    depth, d, n6 = ada_w.shape
    tn = 2048
    out = pl.pallas_call(
        functools.partial(_mod_kernel, n_cond),
        grid=(depth, n6 // tn),
        in_specs=[
            pl.BlockSpec((d, 8), lambda l, j: (0, 0)),
            pl.BlockSpec((None, d, tn), lambda l, j: (l, 0, j)),
            pl.BlockSpec((None, 1, tn), lambda l, j: (l, 0, j)),
        ],
        out_specs=pl.BlockSpec((None, 8, tn), lambda l, j: (l, 0, j)),
        out_shape=jax.ShapeDtypeStruct((depth, 8, n6), F32),
        compiler_params=_params(("arbitrary", "arbitrary")),
        name="modulation",
    )(cond_t, ada_w, ada_b.reshape(depth, 1, n6))
    m = out[:, :n_cond].reshape(depth, n_cond, 6, d)
    return jnp.pad(m, ((0, 0), (0, 0), (0, 2), (0, 0)))


LOG2E = math.log2(math.e)


def _dup_half(k2, half):
    lane = lax.broadcasted_iota(I32, k2.shape, 1)
    lo = lane < HEAD_DIM
    r = pltpu.roll(k2, HEAD_DIM, 1)
    return jnp.where(lo, k2, r) if half == 0 else jnp.where(lo, r, k2)


QKV_PARTS = 2


def _qkv_kernel(na, xa_ref, xb_ref, mod_ref, g_ref, w_ref, qn_ref, kn_ref, bd_ref,
                cos_ref, sin_ref, q_out, kt_out, vv_out, ck_out, cv_out):
    i = pl.program_id(0)
    is_ctx = i < na
    tm = xa_ref.shape[0] // QKV_PARTS
    for part in range(QKV_PARTS):
        _qkv_rows(slice(part * tm, (part + 1) * tm), tm, is_ctx, xa_ref, xb_ref, mod_ref, g_ref, w_ref,
                  qn_ref, kn_ref, bd_ref, cos_ref, sin_ref, q_out, kt_out, vv_out, ck_out, cv_out)


def _qkv_rows(rows, tm, is_ctx, xa_ref, xb_ref, mod_ref, g_ref, w_ref, qn_ref, kn_ref, bd_ref,
              cos_ref, sin_ref, q_out, kt_out, vv_out, ck_out, cv_out):
    x = jnp.where(is_ctx, xa_ref[rows, :], xb_ref[rows, :])
    a = _adaln(x, g_ref[...], mod_ref[0:1, :], mod_ref[1:2, :])
    y = jnp.dot(a.astype(BF16), w_ref[...], preferred_element_type=F32)
    cos = jnp.where(is_ctx, 1.0, cos_ref[rows, :])
    sin = jnp.where(is_ctx, 0.0, sin_ref[rows, :])
    lane = lax.broadcasted_iota(I32, (tm, LANES), 1)
    first16 = (lane % 32) < 16

    def norm_rope(z, wt):
        ss = jnp.dot((z * z).astype(BF16), bd_ref[...], preferred_element_type=F32)
        zn = z * lax.rsqrt(ss * (1.0 / HEAD_DIM) + NORM_EPS) * wt
        outs = []
        for c in range(2):
            t = zn[:, c * LANES:(c + 1) * LANES]
            partner = jnp.where(first16, pltpu.roll(t, LANES - 16, 1), pltpu.roll(t, 16, 1))
            outs.append(t * cos + partner * sin)
        return jnp.concatenate(outs, axis=1)

    nq = N_HEADS * HEAD_DIM
    nk = N_KV_HEADS * HEAD_DIM
    scale = HEAD_DIM ** -0.5 * LOG2E
    for s in range(nq // 256):
        sl = slice(s * 256, (s + 1) * 256)
        q_out[rows, sl] = (norm_rope(y[:, sl], qn_ref[:, sl]) * scale).astype(BF16)
    k = norm_rope(y[:, nq:nq + nk], kn_ref[...])
    v = y[:, nq + nk:nq + 2 * nk]
    for g in range(N_KV_HEADS):
        c = g // 2
        kk = _dup_half(k[:, c * LANES:(c + 1) * LANES], g % 2)
        kt_out[g, :, rows] = kk.T.astype(BF16)
        vv_out[rows, g * LANES:(g + 1) * LANES] = _dup_half(v[:, c * LANES:(c + 1) * LANES], g % 2).astype(BF16)

    @pl.when(is_ctx)
    def _():
        ck_out[rows, :] = k
        cv_out[rows, :] = v


def _rope_tables(t_len):
    pos = np.arange(t_len)
    row = (pos // GRID_W).astype(np.float32)[:, None]
    col = (pos % GRID_W).astype(np.float32)[:, None]
    half = HEAD_DIM // 2
    inv = (np.float32(ROPE_THETA) ** (-np.arange(0, half, 2, dtype=np.float32) / np.float32(half)))[None, :]
    ar, ac = row * inv, col * inv
    cos = np.concatenate([np.cos(ar), np.cos(ar), np.cos(ac), np.cos(ac)], axis=1)
    sin = np.concatenate([-np.sin(ar), np.sin(ar), -np.sin(ac), np.sin(ac)], axis=1)
    return (jnp.asarray(np.tile(cos, (1, 2)), dtype=F32), jnp.asarray(np.tile(sin, (1, 2)), dtype=F32))


def _qkv(xp, xs, t_len, mod, g, w, qn, kn):
    n_p, n_s = xp.shape[0], xs.shape[0]
    tm = _row_tile(n_p, t_len, cap=1024)
    na, nb = n_p // tm, n_s // tm
    nt_s = t_len // tm
    n = n_p + n_s
    d = D_MODEL
    nq, nk = N_HEADS * HEAD_DIM, N_KV_HEADS * HEAD_DIM
    cos, sin = _rope_tables(t_len)
    eye4 = jnp.kron(jnp.eye(4, dtype=F32), jnp.ones((HEAD_DIM, HEAD_DIM), F32)).astype(BF16)
    mod_idx = lambda i: (jnp.where(i < na, 0, 1 + (i - na) // nt_s), 0, 0)
    pos_idx = lambda i: (jnp.maximum(i - na, 0) % nt_s, 0)
    return pl.pallas_call(
        functools.partial(_qkv_kernel, na),
        grid=(na + nb,),
        in_specs=[
            pl.BlockSpec((tm, d), lambda i: (jnp.minimum(i, na - 1), 0)),
            pl.BlockSpec((tm, d), lambda i: (jnp.maximum(i - na, 0), 0)),
            pl.BlockSpec((None, 8, d), mod_idx),
            _const_spec((1, d)),
            _const_spec((d, nq + 2 * nk)),
            _const_spec((1, nq)),
            _const_spec((1, nk)),
            _const_spec((256, 256)),
            pl.BlockSpec((tm, LANES), pos_idx),
            pl.BlockSpec((tm, LANES), pos_idx),
        ],
        out_specs=[pl.BlockSpec((tm, nq), lambda i: (i, 0)),
                   pl.BlockSpec((N_KV_HEADS, LANES, tm), lambda i: (0, 0, i)),
                   pl.BlockSpec((tm, N_KV_HEADS * LANES), lambda i: (i, 0)),
                   pl.BlockSpec((tm, nk), lambda i: (jnp.minimum(i, na - 1), 0)),
                   pl.BlockSpec((tm, nk), lambda i: (jnp.minimum(i, na - 1), 0))],
        out_shape=[jax.ShapeDtypeStruct((n, nq), BF16),
                   jax.ShapeDtypeStruct((N_KV_HEADS, LANES, n), BF16),
                   jax.ShapeDtypeStruct((n, N_KV_HEADS * LANES), BF16),
                   jax.ShapeDtypeStruct((n_p, nk), F32),
                   jax.ShapeDtypeStruct((n_p, nk), F32)],
        compiler_params=_params(("arbitrary",)),
        name="qkv",
    )(xp, xs, mod, g, w, qn, kn, eye4, cos, sin)


def _attn_group(q2a, q2b, kt, vv, bias, n_bias, sinks):
    tq = q2a.shape[0]
    nk = kt.shape[1]
    lo = lax.broadcasted_iota(I32, (tq, LANES), 1) < HEAD_DIM
    top = lax.broadcasted_iota(I32, kt.shape, 0) < HEAD_DIM
    zk = jnp.zeros_like(kt)
    kbd = jnp.concatenate([jnp.where(top, kt, zk), jnp.where(top, zk, kt)], axis=1)
    left = lax.broadcasted_iota(I32, vv.shape, 1) < HEAD_DIM
    zv = jnp.zeros_like(vv)
    vbd = jnp.concatenate([jnp.where(left, vv, zv), jnp.where(left, zv, vv)], axis=0)
    s = jnp.dot(jnp.concatenate([q2a, q2b], axis=0), kbd, preferred_element_type=F32)
    p_rows, scales = [], []
    for t in range(2):
        ph, rinv = [], []
        for hf in range(2):
            sh = s[t * tq:(t + 1) * tq, hf * nk:(hf + 1) * nk]
            if bias is not None:
                sh = jnp.concatenate([sh[:, :n_bias] + bias, sh[:, n_bias:]], axis=1)
            sk = sinks[2 * t + hf]
            m = jnp.maximum(jnp.max(sh, axis=-1, keepdims=True), sk)
            p = jnp.exp2(sh - m)
            rinv.append(1.0 / (jnp.sum(p, axis=-1, keepdims=True) + jnp.exp2(sk - m)))
            ph.append(p.astype(BF16))
        p_rows.append(jnp.concatenate(ph, axis=1))
        scales.append(jnp.where(lo, rinv[0], rinv[1]))
    o = jnp.dot(jnp.concatenate(p_rows, axis=0), vbd, preferred_element_type=F32)
    return o[0:tq] * scales[0], o[tq:2 * tq] * scales[1]


def _ctx_attn_kernel(sink_ref, q_ref, kt_ref, vv_ref, x_ref, mod_ref, wo_ref, o_ref, osc):
    for g in range(N_KV_HEADS):
        q2a = q_ref[:, (2 * g) * LANES:(2 * g + 1) * LANES]
        q2b = q_ref[:, (2 * g + 1) * LANES:(2 * g + 2) * LANES]
        sinks = [sink_ref[4 * g + j] * LOG2E for j in range(4)]
        oa, ob = _attn_group(q2a, q2b, kt_ref[g], vv_ref[:, g * LANES:(g + 1) * LANES], None, 0, sinks)
        osc[:, (2 * g) * LANES:(2 * g + 1) * LANES] = oa.astype(BF16)
        osc[:, (2 * g + 1) * LANES:(2 * g + 2) * LANES] = ob.astype(BF16)
    att = jnp.dot(osc[...], wo_ref[...], preferred_element_type=F32)
    o_ref[...] = x_ref[...] + mod_ref[2:3, :] * att


def _ctx_attention(sink, q, kt, vv, xp, seq, mod, wo):
    n_p, d = xp.shape
    nb = n_p // seq
    return pl.pallas_call(
        _ctx_attn_kernel,
        grid=(nb,),
        in_specs=[
            pl.BlockSpec(memory_space=pltpu.SMEM),
            pl.BlockSpec((seq, d), lambda b: (b, 0)),
            pl.BlockSpec((N_KV_HEADS, LANES, seq), lambda b: (0, 0, b)),
            pl.BlockSpec((seq, N_KV_HEADS * LANES), lambda b: (b, 0)),
            pl.BlockSpec((seq, d), lambda b: (b, 0)),
            pl.BlockSpec((None, 8, d), lambda b: (0, 0, 0)),
            _const_spec((d, d)),
        ],
        out_specs=pl.BlockSpec((seq, d), lambda b: (b, 0)),
        out_shape=jax.ShapeDtypeStruct((n_p, d), F32),
        scratch_shapes=[pltpu.VMEM((seq, d), BF16)],
        compiler_params=_params(("arbitrary",)),
        name="ctx_attention",
    )(sink, q, kt, vv, xp, mod, wo)


LAT_QB = 4


def _lat_attn_kernel(t_len, sink_ref, q_ref, ktp_ref, kto_ref, ktn_ref, vvp_ref, vvo_ref, vvn_ref,
                     ckt_ref, cvv_ref, x_ref, mod_ref, wo_ref, o_ref, osc):
    step = pl.program_id(1)
    tq = ATTN_BLOCK
    nloc = 3 * ATTN_BLOCK
    qi = lax.broadcasted_iota(I32, (tq, nloc), 0)
    kj = lax.broadcasted_iota(I32, (tq, nloc), 1)
    in_window = jnp.abs(qi + tq - kj) <= WINDOW
    for u in range(LAT_QB):
        n = step * LAT_QB + u
        kpos = (n - 1) * tq + kj
        bias = jnp.where(in_window & (kpos >= 0) & (kpos < t_len), 0.0, NEG_INF)
        rows = slice(u * tq, (u + 1) * tq)
        for g in range(N_KV_HEADS):
            vs = slice(g * LANES, (g + 1) * LANES)
            kts = [ktp_ref[g]] + [kto_ref[g, :, j * tq:(j + 1) * tq] for j in range(LAT_QB)] + [ktn_ref[g]]
            vvs = [vvp_ref[:, vs]] + [vvo_ref[j * tq:(j + 1) * tq, vs] for j in range(LAT_QB)] + [vvn_ref[:, vs]]
            kt = jnp.concatenate(kts[u:u + 3] + [ckt_ref[g]], axis=1)
            vv = jnp.concatenate(vvs[u:u + 3] + [cvv_ref[:, vs]], axis=0)
            q2a = q_ref[rows, (2 * g) * LANES:(2 * g + 1) * LANES]
            q2b = q_ref[rows, (2 * g + 1) * LANES:(2 * g + 2) * LANES]
            sinks = [sink_ref[4 * g + j] * LOG2E for j in range(4)]
            oa, ob = _attn_group(q2a, q2b, kt, vv, bias, nloc, sinks)
            osc[rows, (2 * g) * LANES:(2 * g + 1) * LANES] = oa.astype(BF16)
            osc[rows, (2 * g + 1) * LANES:(2 * g + 2) * LANES] = ob.astype(BF16)
    att = jnp.dot(osc[...], wo_ref[...], preferred_element_type=F32)
    o_ref[...] = x_ref[...] + mod_ref[2:3, :] * att


def _lat_attention(sink, q, kt, vv, ckt, cvv, xs, n_p, t_len, mod, wo):
    n_s, d = xs.shape
    db = n_s // t_len
    tq = ATTN_BLOCK
    ts = LAT_QB * tq
    nblk = t_len // tq
    nstep = t_len // ts
    assert n_p % ts == 0 and t_len % ts == 0
    off = n_p // tq
    npast = ckt.shape[3]
    g4 = N_KV_HEADS

    def nbr(delta):
        return lambda b, m: off + b * nblk + jnp.clip(m * LAT_QB + delta, 0, nblk - 1)

    own = lambda b, m: n_p // ts + b * nstep + m
    return pl.pallas_call(
        functools.partial(_lat_attn_kernel, t_len),
        grid=(db, nstep),
        in_specs=[
            pl.BlockSpec(memory_space=pltpu.SMEM),
            pl.BlockSpec((ts, d), lambda b, m: (own(b, m), 0)),
            pl.BlockSpec((g4, LANES, tq), lambda b, m: (0, 0, nbr(-1)(b, m))),
            pl.BlockSpec((g4, LANES, ts), lambda b, m: (0, 0, own(b, m))),
            pl.BlockSpec((g4, LANES, tq), lambda b, m: (0, 0, nbr(LAT_QB)(b, m))),
            pl.BlockSpec((tq, g4 * LANES), lambda b, m: (nbr(-1)(b, m), 0)),
            pl.BlockSpec((ts, g4 * LANES), lambda b, m: (own(b, m), 0)),
            pl.BlockSpec((tq, g4 * LANES), lambda b, m: (nbr(LAT_QB)(b, m), 0)),
            pl.BlockSpec((None, g4, LANES, npast), lambda b, m: (b, 0, 0, 0)),
            pl.BlockSpec((None, npast, g4 * LANES), lambda b, m: (b, 0, 0)),
            pl.BlockSpec((ts, d), lambda b, m: (b * nstep + m, 0)),
            pl.BlockSpec((None, 8, d), lambda b, m: (1 + b, 0, 0)),
            _const_spec((d, d)),
        ],
        out_specs=pl.BlockSpec((ts, d), lambda b, m: (b * nstep + m, 0)),
        out_shape=jax.ShapeDtypeStruct((n_s, d), F32),
        scratch_shapes=[pltpu.VMEM((ts, d), BF16)],
        compiler_params=_params(("arbitrary", "arbitrary")),
        name="lat_attention",
    )(sink, q, kt, kt, kt, vv, vv, vv, ckt, cvv, xs, mod, wo)


def _ffn_kernel(na, fc, xa_ref, xb_ref, mod_ref, g_ref, wg_ref, wu_ref, wd_ref, o_ref, acc):
    i = pl.program_id(0)
    x = jnp.where(i < na, xa_ref[...], xb_ref[...])
    a = _adaln(x, g_ref[...], mod_ref[3:4, :], mod_ref[4:5, :]).astype(BF16)
    nf = wg_ref.shape[1] // fc
    for f in range(nf):
        sl = slice(f * fc, (f + 1) * fc)
        hg = jnp.dot(a, wg_ref[:, sl], preferred_element_type=F32)
        hu = jnp.dot(a, wu_ref[:, sl], preferred_element_type=F32)
        act = (_silu(hg) * hu).astype(BF16)
        part = jnp.dot(act, wd_ref[sl, :], preferred_element_type=F32)
        if f == 0:
            acc[...] = part
        else:
            acc[...] += part
    o_ref[...] = x + mod_ref[5:6, :] * acc[...]


def _ffn(hp, hs, t_len, mod, g, wg, wu, wd):
    n_p, n_s = hp.shape[0], hs.shape[0]
    d, dff = wg.shape
    tm = _row_tile(n_p, t_len)
    na, nb = n_p // tm, n_s // tm
    nt_s = t_len // tm
    fc = 256
    assert dff % fc == 0
    mod_idx = lambda i: (jnp.where(i < na, 0, 1 + (i - na) // nt_s), 0, 0)
    return pl.pallas_call(
        functools.partial(_ffn_kernel, na, fc),
        grid=(na + nb,),
        in_specs=[
            pl.BlockSpec((tm, d), lambda i: (jnp.minimum(i, na - 1), 0)),
            pl.BlockSpec((tm, d), lambda i: (jnp.maximum(i - na, 0), 0)),
            pl.BlockSpec((None, 8, d), mod_idx),
            _const_spec((1, d)),
            _const_spec((d, dff)),
            _const_spec((d, dff)),
            _const_spec((dff, d)),
        ],
        out_specs=pl.BlockSpec((tm, d), lambda i: (i, 0)),
        out_shape=jax.ShapeDtypeStruct((n_p + n_s, d), F32),
        scratch_shapes=[pltpu.VMEM((tm, d), F32)],
        compiler_params=_params(("arbitrary",)),
        name="ffn",
    )(hp, hs, mod, g, wg, wu, wd)


def _split2(x):
    hi = x.astype(BF16)
    lo = (x - hi.astype(F32)).astype(BF16)
    return hi, lo


def _gla_in_kernel(x_ref, mod_ref, g_ref, w_ref, w1_ref, w2_ref, gb_ref, tri_ref,
                   qk_out, v_out, r_out, b_out):
    x = x_ref[...]
    tm = x.shape[0]
    a = _adaln(x, g_ref[...], mod_ref[0:1, :], mod_ref[1:2, :]).astype(BF16)
    hk = GLA_HEADS * GLA_DK
    hv = GLA_HEADS * GLA_DV
    q = jnp.dot(a, w_ref[:, 0:hk], preferred_element_type=F32)
    qk_out[:, 0:hk] = (q * (GLA_DK ** -0.5)).astype(BF16)
    qk_out[:, hk:2 * hk] = jnp.dot(a, w_ref[:, hk:2 * hk], preferred_element_type=F32).astype(BF16)
    v_out[...] = jnp.dot(a, w_ref[:, 2 * hk:2 * hk + hv], preferred_element_type=F32).astype(BF16)
    r_out[...] = jnp.dot(a, w_ref[:, 2 * hk + hv:2 * hk + 2 * hv], preferred_element_type=F32).astype(BF16)
    z1 = jnp.dot(a, w1_ref[...], preferred_element_type=F32)
    z = jnp.dot(z1.astype(BF16), w2_ref[...], preferred_element_type=F32) + gb_ref[...]
    gate = (jnp.minimum(z, 0.0) - jnp.log(1.0 + jnp.exp(-jnp.abs(z)))) * (1.0 / GLA_GATE_TAU)
    c = GLA_CHUNK
    tri = tri_ref[...]
    for j in range(tm // c):
        rows = slice(j * c, (j + 1) * c)
        for dr in range(2):
            cols = slice(dr * hk, (dr + 1) * hk)
            hi, lo = _split2(gate[rows, cols])
            b_out[rows, cols] = (jnp.dot(tri[dr], hi, preferred_element_type=F32)
                                 + jnp.dot(tri[dr], lo, preferred_element_type=F32))


def _gla_in(h, n_p, t_len, mod, g, w_in, w1, w2, gate_b):
    n, d = h.shape
    tm = _row_tile(n_p, t_len, cap=1024)
    na = n_p // tm
    nt_s = t_len // tm
    hk, hv = GLA_HEADS * GLA_DK, GLA_HEADS * GLA_DV
    c = GLA_CHUNK
    lower = jnp.tril(jnp.ones((c, c), F32))
    tri = jnp.stack([lower, lower.T]).astype(BF16)
    mod_idx = lambda i: (jnp.where(i < na, 0, 1 + (i - na) // nt_s), 0, 0)
    row = lambda i: (i, 0)
    return pl.pallas_call(
        _gla_in_kernel,
        grid=(n // tm,),
        in_specs=[
            pl.BlockSpec((tm, d), row),
            pl.BlockSpec((None, 8, d), mod_idx),
            _const_spec((1, d)),
            _const_spec(w_in.shape),
            _const_spec(w1.shape),
            _const_spec(w2.shape),
            _const_spec((1, 2 * hk)),
            _const_spec((2, c, c)),
        ],
        out_specs=[pl.BlockSpec((tm, 2 * hk), row), pl.BlockSpec((tm, hv), row),
                   pl.BlockSpec((tm, hv), row), pl.BlockSpec((tm, 2 * hk), row)],
        out_shape=[jax.ShapeDtypeStruct((n, 2 * hk), BF16), jax.ShapeDtypeStruct((n, hv), BF16),
                   jax.ShapeDtypeStruct((n, hv), BF16), jax.ShapeDtypeStruct((n, 2 * hk), F32)],
        compiler_params=_params(("arbitrary",)),
        name="gla_in",
    )(h, mod, g, w_in, w1, w2, gate_b, tri)


def _gla_direction(qk_ref, v_ref, b_ref, st, o_ref, forward, qd_s, att_s, upd_s, dec_s):
    c = GLA_CHUNK
    ncb = qk_ref.shape[0] // c
    hk = GLA_HEADS * GLA_DK
    dk, dv = GLA_DK, GLA_DV
    ri = lax.broadcasted_iota(I32, (c, c), 0)
    ci_ = lax.broadcasted_iota(I32, (c, c), 1)
    keep = (ci_ <= ri) if forward else (ci_ >= ri)

    for ch in range(ncb):
        rows = slice(ch * c, (ch + 1) * c)
        b = b_ref[rows, :]
        if forward:
            b_end, b_mid = b[c - 1:c, :], b[c // 2 - 1:c // 2, :]
        else:
            b_end, b_mid = b[0:1, :], b[c // 2:c // 2 + 1, :]
        qa = qk_ref[rows, 0:hk] * jnp.exp(b - b_mid)
        ka = qk_ref[rows, hk:2 * hk] * jnp.exp(b_mid - b)
        qd_s[rows, :] = (qa * jnp.exp(b_mid)).astype(BF16)
        kd = (ka * jnp.exp(b_end - b_mid)).astype(BF16)
        dec_s[ch:ch + 1, :] = jnp.exp(b_end)
        qab, kab = qa.astype(BF16), ka.astype(BF16)
        for h in range(GLA_HEADS):
            ks = slice(h * dk, (h + 1) * dk)
            vs = slice(h * dv, (h + 1) * dv)
            att = lax.dot_general(qab[:, ks], kab[:, ks], (((1,), (1,)), ((), ())),
                                  preferred_element_type=F32)
            att_s[ch, h] = jnp.where(keep, att, 0.0).astype(BF16)
            upd_s[ch, h] = lax.dot_general(v_ref[rows, vs], kd[:, ks], (((0,), (0,)), ((), ())),
                                           preferred_element_type=F32)

    for ch in (range(ncb) if forward else reversed(range(ncb))):
        rows = slice(ch * c, (ch + 1) * c)
        for h in range(GLA_HEADS):
            ks = slice(h * dk, (h + 1) * dk)
            vs = slice(h * dv, (h + 1) * dv)
            s = st[h]
            o = lax.dot_general(qd_s[rows, ks], s.astype(BF16), (((1,), (1,)), ((), ())),
                                preferred_element_type=F32)
            o = o + jnp.dot(att_s[ch, h], v_ref[rows, vs], preferred_element_type=F32)
            o_ref[rows, vs] = o.astype(o_ref.dtype)
            st[h] = s * dec_s[ch:ch + 1, ks] + upd_s[ch, h]


def _gla_scan_kernel(shared, from_zero, *refs):
    if shared:
        qkf_ref, vf_ref, bf_ref, bb_ref = refs[:4]
        qkb_ref, vb_ref = qkf_ref, vf_ref
        rest = refs[4:]
    else:
        qkf_ref, vf_ref, bf_ref, qkb_ref, vb_ref, bb_ref = refs[:6]
        rest = refs[6:]
    if not from_zero:
        sf0_ref, sb0_ref = rest[:2]
        rest = rest[2:]
    of_ref, ob_ref, sf_ref, sb_ref, stf, stb, qd_s, att_s, upd_s, dec_s = rest
    i = pl.program_id(1)
    nblk = pl.num_programs(1)

    @pl.when(i == 0)
    def _():
        if from_zero:
            stf[...] = jnp.zeros_like(stf)
            stb[...] = jnp.zeros_like(stb)
        else:
            for h in range(GLA_HEADS):
                stf[h] = sf0_ref[h].T
                stb[h] = sb0_ref[h].T

    _gla_direction(qkf_ref, vf_ref, bf_ref, stf, of_ref, True, qd_s, att_s, upd_s, dec_s)
    _gla_direction(qkb_ref, vb_ref, bb_ref, stb, ob_ref, False, qd_s, att_s, upd_s, dec_s)

    @pl.when(i == nblk - 1)
    def _():
        for h in range(GLA_HEADS):
            sf_ref[h] = stf[h].T
            sb_ref[h] = stb[h].T


def _gla_scan(qk, v, bc, sf0, sb0, b_off, nb, t_len):
    n = qk.shape[0]
    assert n % t_len == 0
    ns = n // t_len
    tb = _row_tile(t_len)
    nblk = t_len // tb
    qk3 = qk.reshape(ns, t_len, qk.shape[1])
    v3 = v.reshape(ns, t_len, v.shape[1])
    bc3 = bc.reshape(ns, t_len, bc.shape[1])
    h, dk, dv = GLA_HEADS, GLA_DK, GLA_DV
    hk = h * dk
    st_spec = pl.BlockSpec((None, h, dk, dv), lambda b, i: (b, 0, 0, 0))
    fwd = lambda b, i: (b_off + b, i, 0)
    bwd = lambda b, i: (b_off + b, nblk - 1 - i, 0)
    bwd_b = pl.BlockSpec((None, tb, hk), lambda b, i: (b_off + b, nblk - 1 - i, 1))
    shared = nblk == 1
    in_specs = [pl.BlockSpec((None, tb, 2 * hk), fwd), pl.BlockSpec((None, tb, h * dv), fwd),
                pl.BlockSpec((None, tb, hk), fwd)]
    operands = [qk3, v3, bc3]
    if not shared:
        in_specs += [pl.BlockSpec((None, tb, 2 * hk), bwd), pl.BlockSpec((None, tb, h * dv), bwd)]
        operands += [qk3, v3]
    in_specs += [bwd_b]
    operands += [bc3]
    from_zero = sf0 is None
    if not from_zero:
        in_specs += [st_spec, st_spec]
        operands += [sf0, sb0]
    return pl.pallas_call(
        functools.partial(_gla_scan_kernel, shared, from_zero),
        grid=(nb, nblk),
        in_specs=in_specs,
        out_specs=[pl.BlockSpec((None, tb, h * dv), lambda b, i: (b, i, 0)),
                   pl.BlockSpec((None, tb, h * dv), lambda b, i: (b, nblk - 1 - i, 0)),
                   st_spec, st_spec],
        out_shape=[jax.ShapeDtypeStruct((nb, t_len, h * dv), BF16),
                   jax.ShapeDtypeStruct((nb, t_len, h * dv), BF16),
                   jax.ShapeDtypeStruct((nb, h, dk, dv), F32),
                   jax.ShapeDtypeStruct((nb, h, dk, dv), F32)],
        scratch_shapes=[pltpu.VMEM((h, dv, dk), F32), pltpu.VMEM((h, dv, dk), F32),
                        pltpu.VMEM((tb, hk), BF16),
                        pltpu.VMEM((tb // GLA_CHUNK, h, GLA_CHUNK, GLA_CHUNK), BF16),
                        pltpu.VMEM((tb // GLA_CHUNK, h, dv, dk), F32),
                        pltpu.VMEM((max(tb // GLA_CHUNK, 8), hk), F32)],
        compiler_params=_params(("arbitrary", "arbitrary")),
        name="gla_scan",
    )(*operands)


GLA_OUT_PARTS = 4


def _gla_out_kernel(na, ofa_ref, oba_ref, ofb_ref, obb_ref, r_ref, h_ref, mod_ref, on_ref, wo_ref,
                    g2_ref, rw_ref, tri_ref, h_out, f_out, route_out, wt_out, cnt_out, run):
    i = pl.program_id(0)
    is_ctx = i < na

    @pl.when(i == 0)
    def _():
        run[...] = jnp.zeros_like(run)

    tm = tri_ref.shape[0]
    for part in range(h_ref.shape[0] // tm):
        rows = slice(part * tm, (part + 1) * tm)
        _gla_out_rows(rows, tm, is_ctx, ofa_ref, oba_ref, ofb_ref, obb_ref, r_ref, h_ref, mod_ref, on_ref,
                      wo_ref, g2_ref, rw_ref, tri_ref, h_out, f_out, route_out, wt_out, run)
    cnt_out[...] = run[...]


def _gla_out_rows(rows, tm, is_ctx, ofa_ref, oba_ref, ofb_ref, obb_ref, r_ref, h_ref, mod_ref, on_ref,
                  wo_ref, g2_ref, rw_ref, tri_ref, h_out, f_out, route_out, wt_out, run):
    o = (jnp.where(is_ctx, ofa_ref[rows, :], ofb_ref[rows, :]).astype(F32)
         + jnp.where(is_ctx, oba_ref[rows, :], obb_ref[rows, :]).astype(F32))
    dv = GLA_DV
    parts = []
    for hh in range(GLA_HEADS):
        oh = o[:, hh * dv:(hh + 1) * dv]
        parts.append(_rms(oh) * on_ref[...])
    on = jnp.concatenate(parts, axis=1)
    r = r_ref[rows, :].astype(F32)
    gated = (on * _silu(r)).astype(BF16)
    h = h_ref[rows, :] + mod_ref[2:3, :] * jnp.dot(gated, wo_ref[...], preferred_element_type=F32)
    h_out[rows, :] = h
    f = _adaln(h, g2_ref[...], mod_ref[3:4, :], mod_ref[4:5, :])
    f_out[rows, :] = _pack_halves(f)
    fh = f.astype(BF16)
    fl = (f - fh.astype(F32)).astype(BF16)
    logits = (jnp.dot(fh, rw_ref[0], preferred_element_type=F32)
              + jnp.dot(fl, rw_ref[0], preferred_element_type=F32)
              + jnp.dot(fh, rw_ref[1], preferred_element_type=F32))
    lane = lax.broadcasted_iota(I32, (tm, LANES), 1)
    lane_f = lane.astype(F32)
    logits = jnp.where(lane < N_EXPERTS, logits, -jnp.inf)
    m1 = jnp.max(logits, axis=-1, keepdims=True)
    i1 = jnp.min(jnp.where(logits == m1, lane_f, float(LANES)), axis=-1, keepdims=True)
    rest = jnp.where(lane_f == i1, -jnp.inf, logits)
    m2 = jnp.max(rest, axis=-1, keepdims=True)
    i2 = jnp.min(jnp.where(rest == m2, lane_f, float(LANES)), axis=-1, keepdims=True)
    e2 = jnp.exp(m2 - m1)
    w1 = 1.0 / (1.0 + e2)
    w2 = e2 / (1.0 + e2)
    wt_out[rows, :] = jnp.where(lane == 0, w1, jnp.where(lane == 1, w2, 0.0))
    sel1 = lane_f == i1
    sel2 = lane_f == i2
    picked = jnp.where(sel1, 1.0, jnp.where(sel2, 1.0, 0.0))
    before = jnp.dot(tri_ref[...], picked.astype(BF16), preferred_element_type=F32) + run[0:1, :]
    rank1 = jnp.sum(jnp.where(sel1, before, 0.0), axis=-1, keepdims=True)
    rank2 = jnp.sum(jnp.where(sel2, before, 0.0), axis=-1, keepdims=True)
    run[...] = run[...] + jnp.sum(picked, axis=0, keepdims=True)
    routing = jnp.where(lane == 0, i1, jnp.where(lane == 1, i2,
                        jnp.where(lane == 2, rank1, jnp.where(lane == 3, rank2, 0.0))))
    route_out[:, rows] = routing.T[0:8, :]


def _gla_out(of_p, ob_p, of_s, ob_s, r, h, n_p, t_len, mod, out_norm, wo, g2, router_w):
    n, d = h.shape
    tm = _row_tile(n_p, t_len, cap=1024)
    na = n_p // tm
    nt_s = t_len // tm
    mod_idx = lambda i: (jnp.where(i < na, 0, 1 + (i - na) // nt_s), 0, 0)
    row = lambda i: (i, 0)
    ctx_row = lambda i: (jnp.minimum(i, na - 1), 0)
    lat_row = lambda i: (jnp.maximum(i - na, 0), 0)
    tp = tm // GLA_OUT_PARTS
    tri = jnp.tril(jnp.ones((tp, tp), F32), -1).astype(BF16)
    return pl.pallas_call(
        functools.partial(_gla_out_kernel, na),
        grid=(n // tm,),
        in_specs=[
            pl.BlockSpec((tm, d), ctx_row),
            pl.BlockSpec((tm, d), ctx_row),
            pl.BlockSpec((tm, d), lat_row),
            pl.BlockSpec((tm, d), lat_row),
            pl.BlockSpec((tm, d), row),
            pl.BlockSpec((tm, d), row),
            pl.BlockSpec((None, 8, d), mod_idx),
            _const_spec((1, GLA_DV)),
            _const_spec((d, d)),
            _const_spec((1, d)),
            _const_spec((2, d, LANES)),
            _const_spec((tp, tp)),
        ],
        out_specs=[pl.BlockSpec((tm, d), row), pl.BlockSpec((tm, d // 2), row),
                   pl.BlockSpec((8, tm), lambda i: (0, i)), pl.BlockSpec((tm, LANES), row),
                   _const_spec((8, LANES))],
        out_shape=[jax.ShapeDtypeStruct((n, d), F32), jax.ShapeDtypeStruct((n, d // 2), jnp.uint32),
                   jax.ShapeDtypeStruct((8, n), F32), jax.ShapeDtypeStruct((n, LANES), F32),
                   jax.ShapeDtypeStruct((8, LANES), F32)],
        scratch_shapes=[pltpu.VMEM((8, LANES), F32)],
        compiler_params=_params(("arbitrary",)),
        name="gla_out_router",
    )(of_p, ob_p, of_s, ob_s, r, h, mod, out_norm, wo, g2, router_w, tri)


SC_CORES = 2
SC_SUBCORES = 16
SC_CHUNK = 64


def _sc_gather_rows(table, idx):
    nw = SC_CORES * SC_SUBCORES
    b = idx.shape[0]
    d = table.shape[1]
    assert b % (nw * SC_CHUNK) == 0
    per_w = b // nw
    n_chunks = per_w // SC_CHUNK
    mesh = plsc.VectorSubcoreMesh(core_axis_name="c", subcore_axis_name="s",
                                  num_cores=SC_CORES, num_subcores=SC_SUBCORES)

    assert n_chunks % 2 == 0

    def body(table_hbm, idx_hbm, out_hbm, idx_v, rows_v, gsem, wsem):
        wid = lax.axis_index("s") * SC_CORES + lax.axis_index("c")
        base = wid * per_w
        pltpu.sync_copy(idx_hbm.at[wid], idx_v)

        def gather(j, slot):
            return pltpu.make_async_copy(table_hbm.at[idx_v.at[j]], rows_v.at[slot], gsem.at[slot])

        def write(j, slot):
            return pltpu.make_async_copy(rows_v.at[slot], out_hbm.at[pl.ds(base + j * SC_CHUNK, SC_CHUNK)],
                                         wsem.at[slot])

        gather(0, 0).start()

        @pl.loop(0, n_chunks, step=2)
        def _(j):
            for slot in range(2):
                jj = j + slot
                gather(jj, slot).wait()

                @pl.when(jj + 1 < n_chunks)
                def _():
                    @pl.when(jj >= 1)
                    def _():
                        write(jj - 1, 1 - slot).wait()

                    gather(jj + 1, 1 - slot).start()

                write(jj, slot).start()

        write(n_chunks - 2, 0).wait()
        write(n_chunks - 1, 1).wait()

    return pl.kernel(
        body,
        out_type=jax.ShapeDtypeStruct((b, d), table.dtype),
        mesh=mesh,
        scratch_types=[pltpu.VMEM((n_chunks, SC_CHUNK), I32),
                       pltpu.VMEM((2, SC_CHUNK, d), table.dtype),
                       pltpu.SemaphoreType.DMA((2,)),
                       pltpu.SemaphoreType.DMA((2,))],
        name="sc_gather_rows",
    )(table, idx.reshape(nw, n_chunks, SC_CHUNK))


def _sc_scatter_rows(rows, idx2, p):
    nw = SC_CORES * SC_SUBCORES
    n, d = rows.shape
    assert n % (nw * SC_CHUNK) == 0
    per_w = n // nw
    n_chunks = per_w // SC_CHUNK
    mesh = plsc.VectorSubcoreMesh(core_axis_name="c", subcore_axis_name="s",
                                  num_cores=SC_CORES, num_subcores=SC_SUBCORES)

    assert n_chunks % 2 == 0

    def body(rows_hbm, idx0_hbm, idx1_hbm, out_hbm, idx0_v, idx1_v, rows_v, rsem, ssem):
        wid = lax.axis_index("s") * SC_CORES + lax.axis_index("c")
        base = wid * per_w
        pltpu.sync_copy(idx0_hbm.at[wid], idx0_v)
        pltpu.sync_copy(idx1_hbm.at[wid], idx1_v)

        def read(j, slot):
            return pltpu.make_async_copy(rows_hbm.at[pl.ds(base + j * SC_CHUNK, SC_CHUNK)], rows_v.at[slot],
                                         rsem.at[slot])

        read(0, 0).start()

        @pl.loop(0, n_chunks, step=2)
        def _(j):
            for slot in range(2):
                jj = j + slot
                read(jj, slot).wait()

                @pl.when(jj + 1 < n_chunks)
                def _():
                    read(jj + 1, 1 - slot).start()

                s0 = pltpu.make_async_copy(rows_v.at[slot], out_hbm.at[idx0_v.at[jj]], ssem.at[0])
                s1 = pltpu.make_async_copy(rows_v.at[slot], out_hbm.at[idx1_v.at[jj]], ssem.at[1])
                s0.start()
                s1.start()
                s0.wait()
                s1.wait()

    return pl.kernel(
        body,
        out_type=jax.ShapeDtypeStruct((p, d), rows.dtype),
        mesh=mesh,
        scratch_types=[pltpu.VMEM((n_chunks, SC_CHUNK), I32),
                       pltpu.VMEM((n_chunks, SC_CHUNK), I32),
                       pltpu.VMEM((2, SC_CHUNK, d), rows.dtype),
                       pltpu.SemaphoreType.DMA((2,)),
                       pltpu.SemaphoreType.DMA((2,))],
        name="sc_scatter_rows",
    )(rows, idx2[0].reshape(nw, n_chunks, SC_CHUNK), idx2[1].reshape(nw, n_chunks, SC_CHUNK))


EXPERT_TILE = 1280
EXPERT_SUB = 256
EXPERT_FF = 512
EXPERT_VMEM = 60 * 1024 * 1024


def _pack_halves(x):
    k = x.shape[1] // 2
    lo = lax.bitcast_convert_type(x[:, :k].astype(BF16).astype(F32), jnp.uint32)
    hi = lax.bitcast_convert_type(x[:, k:].astype(BF16).astype(F32), jnp.uint32)
    return (lo >> 16) | (hi & jnp.uint32(0xFFFF0000))


def _unpack_halves(w):
    lo = lax.bitcast_convert_type(w << 16, F32).astype(BF16)
    hi = lax.bitcast_convert_type(w & jnp.uint32(0xFFFF0000), F32).astype(BF16)
    return lo, hi


def _expert_kernel(nf, te_ref, ts_ref, nu_ref, nv_ref, x_ref, wg_ref, wu_ref, wd_ref, y_ref, acc, xb):
    i = pl.program_id(0)
    f = pl.program_id(1)
    tm = x_ref.shape[0]
    half = x_ref.shape[1]
    sub = EXPERT_SUB
    nvalid = nv_ref[i]

    def compute(nrows, first, last):
        rows = slice(0, nrows)
        if first:
            rid = lax.broadcasted_iota(I32, (nrows, half), 0)
            lo, hi = _unpack_halves(x_ref[rows, :])
            zero = jnp.zeros_like(lo)
            xb[rows, 0:half] = jnp.where(rid < nvalid, lo, zero)
            xb[rows, half:2 * half] = jnp.where(rid < nvalid, hi, zero)
        x = xb[rows, :]
        hg = jnp.dot(x, wg_ref[...].astype(BF16), preferred_element_type=F32)
        hu = jnp.dot(x, wu_ref[...].astype(BF16), preferred_element_type=F32)
        act = (_silu(hg) * hu).astype(BF16)
        part = jnp.dot(act, wd_ref[...].astype(BF16), preferred_element_type=F32)
        acc[rows, :] = part if first else acc[rows, :] + part
        del last

    assert nf >= 2
    for k in range(1, tm // sub + 1):
        has_k = (nvalid > (k - 1) * sub) & (nvalid <= k * sub)

        @pl.when(has_k & (f == 0))
        def _():
            compute(k * sub, True, False)

        @pl.when(has_k & (f > 0))
        def _():
            compute(k * sub, False, False)

        @pl.when(has_k & (f == nf - 1))
        def _():
            y_ref[0:k * sub, :] = _pack_halves(acc[0:k * sub, :])
            if k * sub < tm:
                y_ref[k * sub:tm, :] = jnp.zeros((tm - k * sub, half), y_ref.dtype)

    @pl.when((nvalid == 0) & (f == nf - 1))
    def _():
        y_ref[...] = jnp.zeros_like(y_ref)


def _experts(x, tile_expert, tile_src, n_used, n_valid, wg, wu, wd, tm, tf):
    p, half = x.shape
    d = 2 * half
    ne, _, dff = wg.shape
    nf = dff // tf
    assert dff % tf == 0 and tm % EXPERT_SUB == 0

    def fidx(i, f, nu):
        return jnp.where(i < nu[0], f, nf - 1)

    grid_spec = pltpu.PrefetchScalarGridSpec(
        num_scalar_prefetch=4,
        grid=(p // tm, nf),
        in_specs=[
            pl.BlockSpec((tm, half), lambda i, f, te, ts, nu, nv: (ts[i], 0)),
            pl.BlockSpec((None, d, tf), lambda i, f, te, ts, nu, nv: (te[i], 0, fidx(i, f, nu))),
            pl.BlockSpec((None, d, tf), lambda i, f, te, ts, nu, nv: (te[i], 0, fidx(i, f, nu))),
            pl.BlockSpec((None, tf, d), lambda i, f, te, ts, nu, nv: (te[i], fidx(i, f, nu), 0)),
        ],
        out_specs=pl.BlockSpec((tm, half), lambda i, f, te, ts, nu, nv: (i, 0)),
        scratch_shapes=[pltpu.VMEM((tm, d), F32), pltpu.VMEM((tm, d), BF16)],
    )
    return pl.pallas_call(
        functools.partial(_expert_kernel, nf),
        grid_spec=grid_spec,
        out_shape=jax.ShapeDtypeStruct((p, half), jnp.uint32),
        compiler_params=_params(("arbitrary", "arbitrary"), EXPERT_VMEM),
        name="moe_experts",
    )(tile_expert, tile_src, n_used, n_valid, x, wg, wu, wd)


def _combine_kernel(wt_ref, h_ref, mod_ref, y0_ref, y1_ref, o_ref):
    wt = wt_ref[...]
    half = y0_ref.shape[1]
    lo0, hi0 = _unpack_halves(y0_ref[...])
    lo1, hi1 = _unpack_halves(y1_ref[...])
    w0, w1 = wt[:, 0:1], wt[:, 1:2]
    h = h_ref[...]
    gate = mod_ref[5:6, :]
    o_ref[:, 0:half] = h[:, 0:half] + gate[:, 0:half] * (w0 * lo0.astype(F32) + w1 * lo1.astype(F32))
    o_ref[:, half:] = h[:, half:] + gate[:, half:] * (w0 * hi0.astype(F32) + w1 * hi1.astype(F32))


def _combine(yg, wt, h, mod, row_off, n_rows, n_mod, mod_off, tm):
    d = h.shape[1]
    nt = n_rows // tm
    toff = row_off // tm
    per_mod = n_rows // n_mod // tm
    return pl.pallas_call(
        _combine_kernel,
        grid=(nt,),
        in_specs=[
            pl.BlockSpec((tm, LANES), lambda i: (toff + i, 0)),
            pl.BlockSpec((tm, d), lambda i: (toff + i, 0)),
            pl.BlockSpec((None, 8, d), lambda i: (mod_off + i // per_mod, 0, 0)),
            pl.BlockSpec((None, tm, d // 2), lambda i: (0, i, 0)),
            pl.BlockSpec((None, tm, d // 2), lambda i: (1, i, 0)),
        ],
        out_specs=pl.BlockSpec((tm, d), lambda i: (i, 0)),
        out_shape=jax.ShapeDtypeStruct((n_rows, d), F32),
        compiler_params=_params(("arbitrary",)),
        name="moe_combine",
    )(wt, h, mod, yg, yg)


def _route(route, counts, n, tm):
    cnt = counts[0, :N_EXPERTS].astype(I32)
    padded = ((cnt + tm - 1) // tm) * tm
    gend = jnp.cumsum(padded)
    goff = gend - padded
    e = route[0:2].astype(I32)
    rank = route[2:4].astype(I32)
    onehot = e[:, :, None] == jnp.arange(N_EXPERTS, dtype=I32)[None, None, :]
    dest = jnp.sum(jnp.where(onehot, goff[None, None, :], 0), axis=-1) + rank
    p = (-(-2 * n // tm) + N_EXPERTS) * tm
    n_used = gend[-1] // tm
    tiles = jnp.arange(p // tm, dtype=I32)
    tile_src = jnp.minimum(tiles, n_used - 1)
    tile_expert = jnp.minimum(jnp.sum((gend[None, :] <= (tile_src * tm)[:, None]).astype(I32), axis=1),
                              N_EXPERTS - 1)
    used = goff[tile_expert] + cnt[tile_expert]
    n_valid = jnp.where(tiles < n_used, jnp.clip(used - tiles * tm, 0, tm), 0).astype(I32)
    return dest, p, tile_expert, tile_src, n_used.reshape(1).astype(I32), n_valid


def kernel(x_prompt, x_sample, cache_k, cache_v, state_fwd, state_bwd, c, c_ctx, ada_w, ada_b, norm1_g, norm2_g, attn_w_qkv, attn_q_norm, attn_k_norm, attn_sink, attn_w_o, gla_w_in, gla_gate_w1, gla_gate_w2, gla_gate_b, gla_out_norm, gla_w_o, ffn_w_gate, ffn_w_up, ffn_w_down, moe_router, moe_w_gate, moe_w_up, moe_w_down):
    bp, seq, d = x_prompt.shape
    db, t_len, _ = x_sample.shape
    n_p, n_s = bp * seq, db * t_len
    n = n_p + n_s
    xp = x_prompt.reshape(n_p, d)
    xs = x_sample.reshape(n_s, d)

    cond = jnp.concatenate([c_ctx[None, :], c], axis=0)
    assert cond.shape[0] <= 8
    cond_t = jnp.pad(cond, ((0, 8 - cond.shape[0]), (0, 0))).T
    mods = _modulation(cond_t, cond.shape[0], ada_w, ada_b)

    nk = N_KV_HEADS * HEAD_DIM
    qn = jnp.tile(attn_q_norm[0], N_HEADS)[None, :]
    kn = jnp.tile(attn_k_norm[0], N_KV_HEADS)[None, :]
    q, kt, vv, ck_new, cv_new = _qkv(xp, xs, t_len, mods[0], norm1_g[0][None, :],
                                     attn_w_qkv[0].astype(BF16), qn, kn)
    wo0 = attn_w_o[0].astype(BF16)
    sink = attn_sink[0]
    hp = _ctx_attention(sink, q, kt, vv, xp, seq, mods[0], wo0)
    ck = cache_k[:, 0].astype(BF16)
    cv = cache_v[:, 0].astype(BF16)
    ckt = jnp.concatenate([ck, ck], axis=-1).transpose(0, 2, 3, 1)
    cvv = jnp.concatenate([cv, cv], axis=-1).reshape(db, cv.shape[1], N_KV_HEADS * LANES)
    hs = _lat_attention(sink, q, kt, vv, ckt, cvv, xs, n_p, t_len, mods[0], wo0)
    new_k = ck_new.reshape(bp, 1, seq, N_KV_HEADS, HEAD_DIM)
    new_v = cv_new.reshape(bp, 1, seq, N_KV_HEADS, HEAD_DIM)
    h = _ffn(hp, hs, t_len, mods[0], norm2_g[0][None, :], ffn_w_gate[0].astype(BF16),
             ffn_w_up[0].astype(BF16), ffn_w_down[0].astype(BF16))

    hk = GLA_HEADS * GLA_DK
    rank = GLA_GATE_RANK
    w1 = jnp.concatenate([gla_gate_w1[0, 0], gla_gate_w1[0, 1]], axis=1)
    w1 = jnp.pad(w1, ((0, 0), (0, LANES - 2 * rank))).astype(BF16)
    w2 = jnp.zeros((LANES, 2 * hk), F32)
    w2 = w2.at[0:rank, 0:hk].set(gla_gate_w2[0, 0]).at[rank:2 * rank, hk:].set(gla_gate_w2[0, 1]).astype(BF16)
    gate_b = gla_gate_b[0].reshape(1, 2 * hk)
    qk, v, r, bc = _gla_in(h, n_p, t_len, mods[1], norm1_g[1][None, :], gla_w_in[0].astype(BF16),
                           w1, w2, gate_b)
    of_p, ob_p, new_sf, new_sb = _gla_scan(qk, v, bc, None, None, 0, bp, seq)
    of_s, ob_s, _, _ = _gla_scan(qk, v, bc, state_fwd[:, 0], state_bwd[:, 0], n_p // t_len, db, t_len)
    rw = jnp.pad(moe_router[0], ((0, 0), (0, LANES - N_EXPERTS)))
    rw_hi = rw.astype(BF16)
    rw_lo = (rw - rw_hi.astype(F32)).astype(BF16)
    h, f, route, wt, counts = _gla_out(
        of_p.reshape(n_p, d), ob_p.reshape(n_p, d), of_s.reshape(n_s, d), ob_s.reshape(n_s, d), r, h,
        n_p, t_len, mods[1], gla_out_norm[0][None, :], gla_w_o[0].astype(BF16), norm2_g[1][None, :],
        jnp.stack([rw_hi, rw_lo]))

    tm_e = EXPERT_TILE
    dest_k, p_rows, tile_expert, tile_src, n_used, n_valid = _route(route, counts, n, tm_e)
    xg = _sc_scatter_rows(f, dest_k, p_rows)
    xg, new_k = lax.optimization_barrier((xg, new_k))
    y = _experts(xg, tile_expert, tile_src, n_used, n_valid, moe_w_gate[0], moe_w_up[0],
                 moe_w_down[0], tm_e, EXPERT_FF)
    yg_p = _sc_gather_rows(y, dest_k[:, :n_p].reshape(2 * n_p)).reshape(2, n_p, d // 2)
    yg_s = _sc_gather_rows(y, dest_k[:, n_p:].reshape(2 * n_s)).reshape(2, n_s, d // 2)
    tm_c = _row_tile(n_p, t_len)
    y_p = _combine(yg_p, wt, h, mods[1], 0, n_p, 1, 0, tm_c)
    y_s = _combine(yg_s, wt, h, mods[1], n_p, n_s, db, 1, tm_c)

    return (y_p.reshape(bp, seq, d), y_s.reshape(db, t_len, d), new_k, new_v,
            new_sf[:, None], new_sb[:, None])
```
